```python
import jax, jax.numpy as jnp
from jax import lax
import numpy as np

D_MODEL = 1024
BATCH = 8
SEQ = 16384
DEPTH = 2

MIX_WIDTH = D_MODEL
CONV_WIDTH = MIX_WIDTH // 2
ATT_WIDTH = MIX_WIDTH - CONV_WIDTH
HEAD_DIM = 64
N_ATT_HEADS = ATT_WIDTH // HEAD_DIM
N_CONV_GROUPS = CONV_WIDTH // HEAD_DIM
CONV_K = 3
N_MEM = 256
N_XHEADS = 4
XHEAD_DIM = D_MODEL // N_XHEADS
D_FF = 2816
Q_BLOCK = 128
EPS = 1e-6
IN_COLS = 3 * CONV_WIDTH + 3 * ATT_WIDTH + N_ATT_HEADS
SPLITS = [CONV_WIDTH, 2 * CONV_WIDTH, 3 * CONV_WIDTH,
          3 * CONV_WIDTH + ATT_WIDTH, 3 * CONV_WIDTH + 2 * ATT_WIDTH,
          3 * CONV_WIDTH + 3 * ATT_WIDTH]

kernel_name = "hybrid_conv_fox_macaron_memxattn"


def rmsnorm(x, g):
    xf = x.astype(jnp.float32)
    y = xf * lax.rsqrt(jnp.mean(xf * xf, axis=-1, keepdims=True) + EPS)
    return (y * g.astype(jnp.float32)).astype(x.dtype)


def swiglu(h, w_gu, w_down):
    gate, up = jnp.split(h @ w_gu, 2, axis=-1)
    return (jax.nn.silu(gate) * up) @ w_down


def causal_depthwise_conv(u, w):
    kern = w[:, None, :].astype(u.dtype)
    return lax.conv_general_dilated(
        u, kern, window_strides=(1,), padding=[(CONV_K - 1, 0)],
        dimension_numbers=('NWC', 'WIO', 'NWC'), feature_group_count=u.shape[-1])


def forgetting_attention(q, k, v, log_f):
    B, S, H, Dh = q.shape
    nb = S // Q_BLOCK
    scale = Dh ** -0.5
    c = jnp.cumsum(log_f, axis=1).transpose(0, 2, 1)
    qb = q.reshape(B, nb, Q_BLOCK, H, Dh).transpose(1, 0, 3, 2, 4)
    cqb = c.reshape(B, H, nb, Q_BLOCK).transpose(2, 0, 1, 3)
    kpos = jnp.arange(S)

    def block(args):
        i, q_i, cq_i = args
        s = jnp.einsum('bhqd,bkhd->bhqk', q_i, k,
                       preferred_element_type=jnp.float32) * scale
        s = s + cq_i[..., None] - c[:, :, None, :]
        qpos = i * Q_BLOCK + jnp.arange(Q_BLOCK)
        mask = kpos[None, :] <= qpos[:, None]
        s = jnp.where(mask, s, -jnp.inf)
        p = jax.nn.softmax(s, axis=-1)
        return jnp.einsum('bhqk,bkhd->bqhd', p.astype(v.dtype), v)

    out = lax.map(block, (jnp.arange(nb), qb, cqb))
    return out.transpose(1, 0, 2, 3, 4).reshape(B, S, H * Dh)


def memory_cross_attention(h, m, w_q, w_kv, w_o):
    B, S, _ = h.shape
    M = m.shape[1]
    q = (h @ w_q).reshape(B, S, N_XHEADS, XHEAD_DIM)
    k, v = jnp.split(m @ w_kv, 2, axis=-1)
    k = k.reshape(B, M, N_XHEADS, XHEAD_DIM)
    v = v.reshape(B, M, N_XHEADS, XHEAD_DIM)
    s = jnp.einsum('bshd,bmhd->bhsm', q, k,
                   preferred_element_type=jnp.float32) * (XHEAD_DIM ** -0.5)
    p = jax.nn.softmax(s, axis=-1)
    o = jnp.einsum('bhsm,bmhd->bshd', p.astype(v.dtype), v).reshape(B, S, D_MODEL)
    return o @ w_o


def _fwd_setup_inputs(seed: int = 0) -> dict:
    key = jax.random.key(seed)
    ks = jax.random.split(key, 24)
    f32 = jnp.float32

    def w(k, shape, fan_in):
        return jax.random.normal(k, shape, f32) * (fan_in ** -0.5)

    def gain(k, shape):
        return 1.0 + 0.1 * jax.random.normal(k, shape, f32)

    return {
        "x": jax.random.normal(ks[0], (BATCH, SEQ, D_MODEL), f32),
        "mem": jax.random.normal(ks[1], (BATCH, N_MEM, D_MODEL), f32),
        "g_ffn1": gain(ks[2], (DEPTH, D_MODEL)),
        "w_ffn1_gu": w(ks[3], (DEPTH, D_MODEL, 2 * D_FF), D_MODEL),
        "w_ffn1_down": w(ks[4], (DEPTH, D_FF, D_MODEL), D_FF),
        "g_mix": gain(ks[5], (DEPTH, D_MODEL)),
        "w_mix_in": w(ks[6], (DEPTH, D_MODEL, IN_COLS), D_MODEL),
        "w_conv": w(ks[7], (DEPTH, CONV_K, CONV_WIDTH), CONV_K),
        "b_f": 2.0 + 0.5 * jax.random.normal(ks[8], (DEPTH, N_ATT_HEADS), f32),
        "g_conv_out": gain(ks[9], (DEPTH, CONV_WIDTH)),
        "g_att_out": gain(ks[10], (DEPTH, ATT_WIDTH)),
        "w_mix_out": w(ks[11], (DEPTH, MIX_WIDTH, D_MODEL), MIX_WIDTH),
        "g_xattn": gain(ks[12], (DEPTH, D_MODEL)),
        "g_mem": gain(ks[13], (DEPTH, D_MODEL)),
        "w_xq": w(ks[14], (DEPTH, D_MODEL, D_MODEL), D_MODEL),
        "w_xkv": w(ks[15], (DEPTH, D_MODEL, 2 * D_MODEL), D_MODEL),
        "w_xo": w(ks[16], (DEPTH, D_MODEL, D_MODEL), D_MODEL),
        "g_ffn2": gain(ks[17], (DEPTH, D_MODEL)),
        "w_ffn2_gu": w(ks[18], (DEPTH, D_MODEL, 2 * D_FF), D_MODEL),
        "w_ffn2_down": w(ks[19], (DEPTH, D_FF, D_MODEL), D_FF),
        "g_final": gain(ks[20], (D_MODEL,)),
    }


def _fwd_reference(x, mem, g_ffn1, w_ffn1_gu, w_ffn1_down, g_mix, w_mix_in, w_conv, b_f,
              g_conv_out, g_att_out, w_mix_out, g_xattn, g_mem, w_xq, w_xkv, w_xo,
              g_ffn2, w_ffn2_gu, w_ffn2_down, g_final):
    B, S, _ = x.shape
    for l in range(DEPTH):
        x = x + 0.5 * swiglu(rmsnorm(x, g_ffn1[l]), w_ffn1_gu[l], w_ffn1_down[l])

        h = rmsnorm(x, g_mix[l])
        z = h @ w_mix_in[l]
        zb, zc, zv, zq, zk, zval, zf = jnp.split(z, SPLITS, axis=-1)

        y_conv = zb * causal_depthwise_conv(zc * zv, w_conv[l])

        log_f = jax.nn.log_sigmoid((zf + b_f[l]).astype(jnp.float32))
        q = zq.reshape(B, S, N_ATT_HEADS, HEAD_DIM)
        k = zk.reshape(B, S, N_ATT_HEADS, HEAD_DIM)
        v = zval.reshape(B, S, N_ATT_HEADS, HEAD_DIM)
        y_att = forgetting_attention(q, k, v, log_f)

        y = jnp.concatenate([rmsnorm(y_conv, g_conv_out[l]),
                             rmsnorm(y_att, g_att_out[l])], axis=-1)
        x = x + y @ w_mix_out[l]

        x = x + memory_cross_attention(rmsnorm(x, g_xattn[l]), rmsnorm(mem, g_mem[l]),
                                       w_xq[l], w_xkv[l], w_xo[l])

        x = x + 0.5 * swiglu(rmsnorm(x, g_ffn2[l]), w_ffn2_gu[l], w_ffn2_down[l])
    return rmsnorm(x, g_final)


import jax as _jax
import jax.numpy as _jnp

TWIN_FORMAT = 'train_step'
FWD_PARAMS = ['x', 'mem', 'g_ffn1', 'w_ffn1_gu', 'w_ffn1_down', 'g_mix', 'w_mix_in', 'w_conv', 'b_f', 'g_conv_out', 'g_att_out', 'w_mix_out', 'g_xattn', 'g_mem', 'w_xq', 'w_xkv', 'w_xo', 'g_ffn2', 'w_ffn2_gu', 'w_ffn2_down', 'g_final']
TWIN_WEIGHTS = ['g_ffn1', 'w_ffn1_gu', 'w_ffn1_down', 'g_mix', 'w_mix_in', 'w_conv', 'b_f', 'g_conv_out', 'g_att_out', 'w_mix_out', 'g_xattn', 'g_mem', 'w_xq', 'w_xkv', 'w_xo', 'g_ffn2', 'w_ffn2_gu', 'w_ffn2_down', 'g_final']
TWIN_DIFF_INPUT = 'x'
TWIN_INPUTS = ['x', 'mem', 'g_ffn1', 'w_ffn1_gu', 'w_ffn1_down', 'g_mix', 'w_mix_in', 'w_conv', 'b_f', 'g_conv_out', 'g_att_out', 'w_mix_out', 'g_xattn', 'g_mem', 'w_xq', 'w_xkv', 'w_xo', 'g_ffn2', 'w_ffn2_gu', 'w_ffn2_down', 'g_final', 'loss_target', 'm_g_ffn1', 'm_w_ffn1_gu', 'm_w_ffn1_down', 'm_g_mix', 'm_w_mix_in', 'm_w_conv', 'm_b_f', 'm_g_conv_out', 'm_g_att_out', 'm_w_mix_out', 'm_g_xattn', 'm_g_mem', 'm_w_xq', 'm_w_xkv', 'm_w_xo', 'm_g_ffn2', 'm_w_ffn2_gu', 'm_w_ffn2_down', 'm_g_final', 'v_g_ffn1', 'v_w_ffn1_gu', 'v_w_ffn1_down', 'v_g_mix', 'v_w_mix_in', 'v_w_conv', 'v_b_f', 'v_g_conv_out', 'v_g_att_out', 'v_w_mix_out', 'v_g_xattn', 'v_g_mem', 'v_w_xq', 'v_w_xkv', 'v_w_xo', 'v_g_ffn2', 'v_w_ffn2_gu', 'v_w_ffn2_down', 'v_g_final']
TWIN_OUTPUTS = ['loss', 'grad_x', 'grad_g_ffn1', 'grad_w_ffn1_gu', 'grad_w_ffn1_down', 'grad_g_mix', 'grad_w_mix_in', 'grad_w_conv', 'grad_b_f', 'grad_g_conv_out', 'grad_g_att_out', 'grad_w_mix_out', 'grad_g_xattn', 'grad_g_mem', 'grad_w_xq', 'grad_w_xkv', 'grad_w_xo', 'grad_g_ffn2', 'grad_w_ffn2_gu', 'grad_w_ffn2_down', 'grad_g_final', 'delta_g_ffn1', 'delta_w_ffn1_gu', 'delta_w_ffn1_down', 'delta_g_mix', 'delta_w_mix_in', 'delta_w_conv', 'delta_b_f', 'delta_g_conv_out', 'delta_g_att_out', 'delta_w_mix_out', 'delta_g_xattn', 'delta_g_mem', 'delta_w_xq', 'delta_w_xkv', 'delta_w_xo', 'delta_g_ffn2', 'delta_w_ffn2_gu', 'delta_w_ffn2_down', 'delta_g_final', 'new_m_g_ffn1', 'new_m_w_ffn1_gu', 'new_m_w_ffn1_down', 'new_m_g_mix', 'new_m_w_mix_in', 'new_m_w_conv', 'new_m_b_f', 'new_m_g_conv_out', 'new_m_g_att_out', 'new_m_w_mix_out', 'new_m_g_xattn', 'new_m_g_mem', 'new_m_w_xq', 'new_m_w_xkv', 'new_m_w_xo', 'new_m_g_ffn2', 'new_m_w_ffn2_gu', 'new_m_w_ffn2_down', 'new_m_g_final', 'new_v_g_ffn1', 'new_v_w_ffn1_gu', 'new_v_w_ffn1_down', 'new_v_g_mix', 'new_v_w_mix_in', 'new_v_w_conv', 'new_v_b_f', 'new_v_g_conv_out', 'new_v_g_att_out', 'new_v_w_mix_out', 'new_v_g_xattn', 'new_v_g_mem', 'new_v_w_xq', 'new_v_w_xkv', 'new_v_w_xo', 'new_v_g_ffn2', 'new_v_w_ffn2_gu', 'new_v_w_ffn2_down', 'new_v_g_final']
TWIN_LEAF_KINDS = {'loss': 'loss', 'grad_x': 'grad_x', 'grad_g_ffn1': 'grad_w', 'grad_w_ffn1_gu': 'grad_w', 'grad_w_ffn1_down': 'grad_w', 'grad_g_mix': 'grad_w', 'grad_w_mix_in': 'grad_w', 'grad_w_conv': 'grad_w', 'grad_b_f': 'grad_w', 'grad_g_conv_out': 'grad_w', 'grad_g_att_out': 'grad_w', 'grad_w_mix_out': 'grad_w', 'grad_g_xattn': 'grad_w', 'grad_g_mem': 'grad_w', 'grad_w_xq': 'grad_w', 'grad_w_xkv': 'grad_w', 'grad_w_xo': 'grad_w', 'grad_g_ffn2': 'grad_w', 'grad_w_ffn2_gu': 'grad_w', 'grad_w_ffn2_down': 'grad_w', 'grad_g_final': 'grad_w', 'delta_g_ffn1': 'delta_w', 'delta_w_ffn1_gu': 'delta_w', 'delta_w_ffn1_down': 'delta_w', 'delta_g_mix': 'delta_w', 'delta_w_mix_in': 'delta_w', 'delta_w_conv': 'delta_w', 'delta_b_f': 'delta_w', 'delta_g_conv_out': 'delta_w', 'delta_g_att_out': 'delta_w', 'delta_w_mix_out': 'delta_w', 'delta_g_xattn': 'delta_w', 'delta_g_mem': 'delta_w', 'delta_w_xq': 'delta_w', 'delta_w_xkv': 'delta_w', 'delta_w_xo': 'delta_w', 'delta_g_ffn2': 'delta_w', 'delta_w_ffn2_gu': 'delta_w', 'delta_w_ffn2_down': 'delta_w', 'delta_g_final': 'delta_w', 'new_m_g_ffn1': 'new_m', 'new_m_w_ffn1_gu': 'new_m', 'new_m_w_ffn1_down': 'new_m', 'new_m_g_mix': 'new_m', 'new_m_w_mix_in': 'new_m', 'new_m_w_conv': 'new_m', 'new_m_b_f': 'new_m', 'new_m_g_conv_out': 'new_m', 'new_m_g_att_out': 'new_m', 'new_m_w_mix_out': 'new_m', 'new_m_g_xattn': 'new_m', 'new_m_g_mem': 'new_m', 'new_m_w_xq': 'new_m', 'new_m_w_xkv': 'new_m', 'new_m_w_xo': 'new_m', 'new_m_g_ffn2': 'new_m', 'new_m_w_ffn2_gu': 'new_m', 'new_m_w_ffn2_down': 'new_m', 'new_m_g_final': 'new_m', 'new_v_g_ffn1': 'new_v', 'new_v_w_ffn1_gu': 'new_v', 'new_v_w_ffn1_down': 'new_v', 'new_v_g_mix': 'new_v', 'new_v_w_mix_in': 'new_v', 'new_v_w_conv': 'new_v', 'new_v_b_f': 'new_v', 'new_v_g_conv_out': 'new_v', 'new_v_g_att_out': 'new_v', 'new_v_w_mix_out': 'new_v', 'new_v_g_xattn': 'new_v', 'new_v_g_mem': 'new_v', 'new_v_w_xq': 'new_v', 'new_v_w_xkv': 'new_v', 'new_v_w_xo': 'new_v', 'new_v_g_ffn2': 'new_v', 'new_v_w_ffn2_gu': 'new_v', 'new_v_w_ffn2_down': 'new_v', 'new_v_g_final': 'new_v'}


def _forward(args):
    return _fwd_reference(*[args[k] for k in FWD_PARAMS])


def _output_shape():
    def fwd():
        inp = _fwd_setup_inputs(0)
        return _fwd_reference(*[inp[k] for k in FWD_PARAMS])
    out = _jax.eval_shape(fwd)
    return out.shape, out.dtype

N_MICROBATCH = 1
ADAM_LR = 0.001
ADAM_B1 = 0.9
ADAM_B2 = 0.999
ADAM_EPS = 1e-08
ADAM_WD = 0.01
ADAM_STEP = 10
PER_EXAMPLE_BATCH_AXIS = {'x': 0, 'mem': 0, 'loss_target': 0}
SHARED_INPUTS = []
_WEIGHT_DTYPES = {'g_ffn1': _jnp.float32, 'w_ffn1_gu': _jnp.float32, 'w_ffn1_down': _jnp.float32, 'g_mix': _jnp.float32, 'w_mix_in': _jnp.float32, 'w_conv': _jnp.float32, 'b_f': _jnp.float32, 'g_conv_out': _jnp.float32, 'g_att_out': _jnp.float32, 'w_mix_out': _jnp.float32, 'g_xattn': _jnp.float32, 'g_mem': _jnp.float32, 'w_xq': _jnp.float32, 'w_xkv': _jnp.float32, 'w_xo': _jnp.float32, 'g_ffn2': _jnp.float32, 'w_ffn2_gu': _jnp.float32, 'w_ffn2_down': _jnp.float32, 'g_final': _jnp.float32}
MOMENT_SCALE = {'g_ffn1': 2.262551e-01, 'w_ffn1_gu': 8.634550e-02, 'w_ffn1_down': 1.417929e-01, 'g_mix': 4.656717e-01, 'w_mix_in': 2.524597e-01, 'w_conv': 2.937410e-01, 'b_f': 1.478812e+00, 'g_conv_out': 3.637536e-01, 'g_att_out': 3.189662e-01, 'w_mix_out': 3.014638e-01, 'g_xattn': 3.161423e-02, 'g_mem': 5.111147e-02, 'w_xq': 2.982127e-02, 'w_xkv': 3.166015e-02, 'w_xo': 3.336110e-02, 'g_ffn2': 1.214013e-01, 'w_ffn2_gu': 4.841731e-02, 'w_ffn2_down': 8.025680e-02, 'g_final': 1.292488e+02}


def _to_microbatches(a, axis):
    t = _jnp.moveaxis(a, axis, 0)
    t = t.reshape((N_MICROBATCH, t.shape[0] // N_MICROBATCH) + t.shape[1:])
    return _jnp.moveaxis(t, 1, axis + 1)


def setup_inputs(seed: int = 0) -> dict:
    inp = _fwd_setup_inputs(seed)
    key = _jax.random.fold_in(_jax.random.key(seed), 7919)
    shape, _ = _output_shape()
    out = dict(inp)
    out["loss_target"] = _jax.random.normal(_jax.random.fold_in(key, 0), shape, _jnp.float32)
    for i, name in enumerate(TWIN_WEIGHTS):
        w = inp[name].astype(_jnp.float32)
        if MOMENT_SCALE is None:
            s = _jnp.sqrt(_jnp.mean(_jnp.square(w)) + 1e-30)
        else:
            s = MOMENT_SCALE[name]
        km, kv = _jax.random.split(_jax.random.fold_in(key, i + 1))
        out[name] = w
        out["m_" + name] = s * _jax.random.normal(km, w.shape, _jnp.float32)
        out["v_" + name] = (s * s) * _jax.random.uniform(kv, w.shape, _jnp.float32, 0.5, 1.5)
    if N_MICROBATCH > 1:
        for name, axis in PER_EXAMPLE_BATCH_AXIS.items():
            out[name] = _to_microbatches(out[name], axis)
    return {'x': out['x'], 'mem': out['mem'], 'g_ffn1': out['g_ffn1'], 'w_ffn1_gu': out['w_ffn1_gu'], 'w_ffn1_down': out['w_ffn1_down'], 'g_mix': out['g_mix'], 'w_mix_in': out['w_mix_in'], 'w_conv': out['w_conv'], 'b_f': out['b_f'], 'g_conv_out': out['g_conv_out'], 'g_att_out': out['g_att_out'], 'w_mix_out': out['w_mix_out'], 'g_xattn': out['g_xattn'], 'g_mem': out['g_mem'], 'w_xq': out['w_xq'], 'w_xkv': out['w_xkv'], 'w_xo': out['w_xo'], 'g_ffn2': out['g_ffn2'], 'w_ffn2_gu': out['w_ffn2_gu'], 'w_ffn2_down': out['w_ffn2_down'], 'g_final': out['g_final'], 'loss_target': out['loss_target'], 'm_g_ffn1': out['m_g_ffn1'], 'm_w_ffn1_gu': out['m_w_ffn1_gu'], 'm_w_ffn1_down': out['m_w_ffn1_down'], 'm_g_mix': out['m_g_mix'], 'm_w_mix_in': out['m_w_mix_in'], 'm_w_conv': out['m_w_conv'], 'm_b_f': out['m_b_f'], 'm_g_conv_out': out['m_g_conv_out'], 'm_g_att_out': out['m_g_att_out'], 'm_w_mix_out': out['m_w_mix_out'], 'm_g_xattn': out['m_g_xattn'], 'm_g_mem': out['m_g_mem'], 'm_w_xq': out['m_w_xq'], 'm_w_xkv': out['m_w_xkv'], 'm_w_xo': out['m_w_xo'], 'm_g_ffn2': out['m_g_ffn2'], 'm_w_ffn2_gu': out['m_w_ffn2_gu'], 'm_w_ffn2_down': out['m_w_ffn2_down'], 'm_g_final': out['m_g_final'], 'v_g_ffn1': out['v_g_ffn1'], 'v_w_ffn1_gu': out['v_w_ffn1_gu'], 'v_w_ffn1_down': out['v_w_ffn1_down'], 'v_g_mix': out['v_g_mix'], 'v_w_mix_in': out['v_w_mix_in'], 'v_w_conv': out['v_w_conv'], 'v_b_f': out['v_b_f'], 'v_g_conv_out': out['v_g_conv_out'], 'v_g_att_out': out['v_g_att_out'], 'v_w_mix_out': out['v_w_mix_out'], 'v_g_xattn': out['v_g_xattn'], 'v_g_mem': out['v_g_mem'], 'v_w_xq': out['v_w_xq'], 'v_w_xkv': out['v_w_xkv'], 'v_w_xo': out['v_w_xo'], 'v_g_ffn2': out['v_g_ffn2'], 'v_w_ffn2_gu': out['v_w_ffn2_gu'], 'v_w_ffn2_down': out['v_w_ffn2_down'], 'v_g_final': out['v_g_final']}


def _loss(weights, diff, rest, loss_target):
    with _jax.named_scope("forward"):
        args = {**rest, TWIN_DIFF_INPUT: diff, **{k: w.astype(_WEIGHT_DTYPES[k]) for k, w in weights.items()}}
        y = _forward(args)
    with _jax.named_scope("loss_head"):
        err = _jnp.square(y.astype(_jnp.float32) - loss_target)
        return 0.5 * _jnp.sum(_jnp.mean(err, axis=-1)) if err.ndim else 0.5 * err


def _adamw(w, g, m, v):
    m = ADAM_B1 * m + (1.0 - ADAM_B1) * g
    v = ADAM_B2 * v + (1.0 - ADAM_B2) * _jnp.square(g)
    m_hat = m / (1.0 - ADAM_B1 ** ADAM_STEP)
    v_hat = v / (1.0 - ADAM_B2 ** ADAM_STEP)
    delta = -ADAM_LR * (m_hat / (_jnp.sqrt(v_hat) + ADAM_EPS) + ADAM_WD * w)
    return delta, m, v


def reference(x, mem, g_ffn1, w_ffn1_gu, w_ffn1_down, g_mix, w_mix_in, w_conv, b_f, g_conv_out, g_att_out, w_mix_out, g_xattn, g_mem, w_xq, w_xkv, w_xo, g_ffn2, w_ffn2_gu, w_ffn2_down, g_final, loss_target, m_g_ffn1, m_w_ffn1_gu, m_w_ffn1_down, m_g_mix, m_w_mix_in, m_w_conv, m_b_f, m_g_conv_out, m_g_att_out, m_w_mix_out, m_g_xattn, m_g_mem, m_w_xq, m_w_xkv, m_w_xo, m_g_ffn2, m_w_ffn2_gu, m_w_ffn2_down, m_g_final, v_g_ffn1, v_w_ffn1_gu, v_w_ffn1_down, v_g_mix, v_w_mix_in, v_w_conv, v_b_f, v_g_conv_out, v_g_att_out, v_w_mix_out, v_g_xattn, v_g_mem, v_w_xq, v_w_xkv, v_w_xo, v_g_ffn2, v_w_ffn2_gu, v_w_ffn2_down, v_g_final):
    given = dict(x=x, mem=mem, g_ffn1=g_ffn1, w_ffn1_gu=w_ffn1_gu, w_ffn1_down=w_ffn1_down, g_mix=g_mix, w_mix_in=w_mix_in, w_conv=w_conv, b_f=b_f, g_conv_out=g_conv_out, g_att_out=g_att_out, w_mix_out=w_mix_out, g_xattn=g_xattn, g_mem=g_mem, w_xq=w_xq, w_xkv=w_xkv, w_xo=w_xo, g_ffn2=g_ffn2, w_ffn2_gu=w_ffn2_gu, w_ffn2_down=w_ffn2_down, g_final=g_final, loss_target=loss_target, m_g_ffn1=m_g_ffn1, m_w_ffn1_gu=m_w_ffn1_gu, m_w_ffn1_down=m_w_ffn1_down, m_g_mix=m_g_mix, m_w_mix_in=m_w_mix_in, m_w_conv=m_w_conv, m_b_f=m_b_f, m_g_conv_out=m_g_conv_out, m_g_att_out=m_g_att_out, m_w_mix_out=m_w_mix_out, m_g_xattn=m_g_xattn, m_g_mem=m_g_mem, m_w_xq=m_w_xq, m_w_xkv=m_w_xkv, m_w_xo=m_w_xo, m_g_ffn2=m_g_ffn2, m_w_ffn2_gu=m_w_ffn2_gu, m_w_ffn2_down=m_w_ffn2_down, m_g_final=m_g_final, v_g_ffn1=v_g_ffn1, v_w_ffn1_gu=v_w_ffn1_gu, v_w_ffn1_down=v_w_ffn1_down, v_g_mix=v_g_mix, v_w_mix_in=v_w_mix_in, v_w_conv=v_w_conv, v_b_f=v_b_f, v_g_conv_out=v_g_conv_out, v_g_att_out=v_g_att_out, v_w_mix_out=v_w_mix_out, v_g_xattn=v_g_xattn, v_g_mem=v_g_mem, v_w_xq=v_w_xq, v_w_xkv=v_w_xkv, v_w_xo=v_w_xo, v_g_ffn2=v_g_ffn2, v_w_ffn2_gu=v_w_ffn2_gu, v_w_ffn2_down=v_w_ffn2_down, v_g_final=v_g_final)
    weights = {n: given[n] for n in TWIN_WEIGHTS}
    shared = {n: given[n] for n in SHARED_INPUTS}
    per_example = {n: given[n] for n in ['x', 'mem']}
    grad_fn = _jax.value_and_grad(_loss, argnums=(0, 1))

    def one_microbatch(ex, loss_target):
        ex = dict(ex)
        diff = ex.pop(TWIN_DIFF_INPUT)
        return grad_fn(weights, diff, {**shared, **ex}, loss_target)

    if N_MICROBATCH == 1:
        loss, (grad_w, grad_x) = one_microbatch(per_example, given["loss_target"])
    else:
        def body(carry, xs):
            loss_sum, grad_sum = carry
            l_k, (gw_k, gx_k) = one_microbatch(xs[0], xs[1])
            with _jax.named_scope("update"):
                return (loss_sum + l_k, _jax.tree.map(_jnp.add, grad_sum, gw_k)), gx_k

        init = (_jnp.zeros((), _jnp.float32), _jax.tree.map(_jnp.zeros_like, weights))
        (loss, grad_w), grad_x = _jax.lax.scan(body, init, (per_example, given["loss_target"]))
    with _jax.named_scope("update"):
        delta_w, new_m, new_v = {}, {}, {}
        for n in TWIN_WEIGHTS:
            delta_w[n], new_m[n], new_v[n] = _adamw(weights[n], grad_w[n], given["m_" + n], given["v_" + n])
    return (loss, grad_x, *[grad_w[n] for n in TWIN_WEIGHTS], *[delta_w[n] for n in TWIN_WEIGHTS],
            *[new_m[n] for n in TWIN_WEIGHTS], *[new_v[n] for n in TWIN_WEIGHTS])
```

```python
import jax
import jax.numpy as jnp
from jax import lax
from jax.experimental import pallas as pl
from jax.experimental.pallas import tpu as pltpu

F32 = jnp.float32
BF16 = jnp.bfloat16
SDS = jax.ShapeDtypeStruct

EPS = 1e-6
HEAD_DIM = 64
N_XHEADS = 4
N_DEV = 8
LANES = 128
SUBLANES = 8
AUG = 128
ATT_BLOCK = 512
NEG = -1e30
PACK_COLS = 1024
VMEM_LIMIT = 56 * 1024 * 1024

ADAM_LR = 0.001
ADAM_B1 = 0.9
ADAM_B2 = 0.999
ADAM_EPS = 1e-08
ADAM_WD = 0.01
ADAM_STEP = 10

IN_NAMES = ['x', 'mem', 'g_ffn1', 'w_ffn1_gu', 'w_ffn1_down', 'g_mix', 'w_mix_in', 'w_conv', 'b_f', 'g_conv_out',
            'g_att_out', 'w_mix_out', 'g_xattn', 'g_mem', 'w_xq', 'w_xkv', 'w_xo', 'g_ffn2', 'w_ffn2_gu',
            'w_ffn2_down', 'g_final']
WEIGHT_NAMES = IN_NAMES[2:]
SHARDED = [('w_ffn1_gu', 'col'), ('w_ffn1_down', 'row'), ('w_mix_in', 'col'), ('w_conv', 'col'), ('w_mix_out', 'row'),
           ('w_xq', 'row'), ('w_xkv', 'col'), ('w_xo', 'row'), ('w_ffn2_gu', 'col'), ('w_ffn2_down', 'row')]
SHARD_KIND = dict(SHARDED)
REPLICATED = [n for n in WEIGHT_NAMES if n not in SHARD_KIND]


def _tile(n, pref, mult):
    t = min(pref, n) // mult * mult
    while t >= mult:
        if n % t == 0:
            return t
        t -= mult
    return n


def _cp(*sem):
    return pltpu.CompilerParams(dimension_semantics=sem, vmem_limit_bytes=VMEM_LIMIT)


def _nt(a, b):
    return lax.dot_general(a, b, (((1,), (1,)), ((), ())), preferred_element_type=F32)


def _tn(a, b):
    return lax.dot_general(a, b, (((0,), (0,)), ((), ())), preferred_element_type=F32)


def _nn(a, b):
    return jnp.dot(a, b, preferred_element_type=F32)


def _rstd(xv):
    return lax.rsqrt(jnp.mean(xv * xv, axis=-1, keepdims=True) + EPS)


def _norm_mm(x, g, w, out_dtype, name, tm=1024, tn=512):
    T, D = x.shape
    N = w.shape[1]
    tm = _tile(T, tm, SUBLANES)
    tn = _tile(N, tn, LANES)

    def body(x_ref, g_ref, w_ref, o_ref, h_ref):
        @pl.when(pl.program_id(1) == 0)
        def _():
            xv = x_ref[...]
            h_ref[...] = (xv * _rstd(xv) * g_ref[...]).astype(BF16)
        o_ref[...] = _nn(h_ref[...], w_ref[...]).astype(o_ref.dtype)

    return pl.pallas_call(
        body, name=name, grid=(T // tm, N // tn),
        in_specs=[pl.BlockSpec((tm, D), lambda i, j: (i, 0)), pl.BlockSpec((1, D), lambda i, j: (0, 0)),
                  pl.BlockSpec((D, tn), lambda i, j: (0, j))],
        out_specs=[pl.BlockSpec((tm, tn), lambda i, j: (i, j)), pl.BlockSpec((tm, D), lambda i, j: (i, 0))],
        out_shape=[SDS((T, N), out_dtype), SDS((T, D), BF16)],
        compiler_params=_cp("parallel", "arbitrary"))(x, g, w)


def _norm_mm_swiglu(x, g, wgu, name, tm=1024, tn=256):
    T, D = x.shape
    F = wgu.shape[1] // 2
    tm = _tile(T, tm, SUBLANES)
    tn = _tile(F, tn, LANES)
    nj = F // tn

    def body(x_ref, g_ref, wg_ref, wu_ref, gate_ref, up_ref, act_ref, h_ref):
        @pl.when(pl.program_id(1) == 0)
        def _():
            xv = x_ref[...]
            h_ref[...] = (xv * _rstd(xv) * g_ref[...]).astype(BF16)
        hv = h_ref[...]
        gt = _nn(hv, wg_ref[...])
        up = _nn(hv, wu_ref[...])
        gate_ref[...] = gt.astype(BF16)
        up_ref[...] = up.astype(BF16)
        act_ref[...] = (gt * jax.nn.sigmoid(gt) * up).astype(BF16)

    tile = pl.BlockSpec((tm, tn), lambda i, j: (i, j))
    return pl.pallas_call(
        body, name=name, grid=(T // tm, nj),
        in_specs=[pl.BlockSpec((tm, D), lambda i, j: (i, 0)), pl.BlockSpec((1, D), lambda i, j: (0, 0)),
                  pl.BlockSpec((D, tn), lambda i, j: (0, j)), pl.BlockSpec((D, tn), lambda i, j: (0, j + nj))],
        out_specs=[tile, tile, tile, pl.BlockSpec((tm, D), lambda i, j: (i, 0))],
        out_shape=[SDS((T, F), BF16), SDS((T, F), BF16), SDS((T, F), BF16), SDS((T, D), BF16)],
        compiler_params=_cp("parallel", "arbitrary"))(x, g, wgu, wgu)


def _mm_nn(a, w, out_dtype, name, res=None, alpha=1.0, tm=512, tn=1024):
    T, K = a.shape
    N = w.shape[1]
    tm = _tile(T, tm, SUBLANES)
    tn = _tile(N, tn, LANES)

    def body(*refs):
        if res is None:
            a_ref, w_ref, o_ref = refs
        else:
            a_ref, w_ref, r_ref, o_ref = refs
        acc = _nn(a_ref[...], w_ref[...])
        if alpha != 1.0:
            acc = alpha * acc
        if res is not None:
            acc = r_ref[...] + acc
        o_ref[...] = acc.astype(o_ref.dtype)

    tile = pl.BlockSpec((tm, tn), lambda i, j: (i, j))
    in_specs = [pl.BlockSpec((tm, K), lambda i, j: (i, 0)), pl.BlockSpec((K, tn), lambda i, j: (0, j))]
    args = [a, w]
    if res is not None:
        in_specs.append(tile)
        args.append(res)
    return pl.pallas_call(
        body, name=name, grid=(T // tm, N // tn), in_specs=in_specs, out_specs=tile,
        out_shape=SDS((T, N), out_dtype), compiler_params=_cp("parallel", "parallel"))(*args)


def _mm_nt(a, w, out_dtype, name, tm=512, tn=1024):
    T, K = a.shape
    N = w.shape[0]
    tm = _tile(T, tm, SUBLANES)
    tn = _tile(N, tn, LANES)

    def body(a_ref, w_ref, o_ref):
        o_ref[...] = _nt(a_ref[...], w_ref[...]).astype(o_ref.dtype)

    return pl.pallas_call(
        body, name=name, grid=(T // tm, N // tn),
        in_specs=[pl.BlockSpec((tm, K), lambda i, j: (i, 0)), pl.BlockSpec((tn, K), lambda i, j: (j, 0))],
        out_specs=pl.BlockSpec((tm, tn), lambda i, j: (i, j)),
        out_shape=SDS((T, N), out_dtype), compiler_params=_cp("parallel", "parallel"))(a, w)


def _mm_tn(a, b, name, alpha=1.0, tk=1536, tn=1024, tt=512):
    T, K = a.shape
    N = b.shape[1]
    tk = _tile(K, tk, LANES)
    tn = _tile(N, tn, LANES)
    tt = _tile(T, tt, 2 * SUBLANES)
    nt = T // tt

    def body(a_ref, b_ref, o_ref):
        t = pl.program_id(2)

        @pl.when(t == 0)
        def _():
            o_ref[...] = jnp.zeros_like(o_ref)
        o_ref[...] += _tn(a_ref[...], b_ref[...])
        if alpha != 1.0:
            @pl.when(t == nt - 1)
            def _():
                o_ref[...] = alpha * o_ref[...]

    return pl.pallas_call(
        body, name=name, grid=(K // tk, N // tn, nt),
        in_specs=[pl.BlockSpec((tt, tk), lambda i, j, t: (t, i)), pl.BlockSpec((tt, tn), lambda i, j, t: (t, j))],
        out_specs=pl.BlockSpec((tk, tn), lambda i, j, t: (i, j)),
        out_shape=SDS((K, N), F32), compiler_params=_cp("parallel", "parallel", "arbitrary"))(a, b)


def _ffn_bwd_act(dyb, wd, gate, up, name, alpha=0.5, tm=512, tn=256):
    T, D = dyb.shape
    F = wd.shape[0]
    tm = _tile(T, tm, SUBLANES)
    tn = _tile(F, tn, LANES)

    def body(dy_ref, wd_ref, gate_ref, up_ref, dg_ref, du_ref):
        da = alpha * _nt(dy_ref[...], wd_ref[...])
        gt = gate_ref[...].astype(F32)
        up = up_ref[...].astype(F32)
        sg = jax.nn.sigmoid(gt)
        dg_ref[...] = (da * up * (sg * (1.0 + gt * (1.0 - sg)))).astype(BF16)
        du_ref[...] = (da * (gt * sg)).astype(BF16)

    tile = pl.BlockSpec((tm, tn), lambda i, j: (i, j))
    return pl.pallas_call(
        body, name=name, grid=(T // tm, F // tn),
        in_specs=[pl.BlockSpec((tm, D), lambda i, j: (i, 0)), pl.BlockSpec((tn, D), lambda i, j: (j, 0)), tile, tile],
        out_specs=[tile, tile], out_shape=[SDS((T, F), BF16), SDS((T, F), BF16)],
        compiler_params=_cp("parallel", "parallel"))(dyb, wd, gate, up)


def _mm_nt_normbwd(pairs, kc, x, g, dres, name, tm=512, tkc=512):
    T, D = x.shape
    tm = _tile(T, tm, SUBLANES)
    tkc = _tile(kc, tkc, LANES)
    nk = kc // tkc
    n_pairs = len(pairs)

    def body(*refs):
        a_refs = refs[:n_pairs]
        w_refs = refs[n_pairs:2 * n_pairs]
        x_ref, g_ref, r_ref, dx_ref, dxb_ref, dg_ref, acc_ref = refs[2 * n_pairs:]
        i = pl.program_id(0)
        k = pl.program_id(1)

        @pl.when(k == 0)
        def _():
            acc_ref[...] = jnp.zeros_like(acc_ref)
        for a_ref, w_ref in zip(a_refs, w_refs):
            acc_ref[...] += _nt(a_ref[...], w_ref[...])

        @pl.when(k == nk - 1)
        def _():
            dh = acc_ref[...]
            xv = x_ref[...]
            r = _rstd(xv)
            xh = xv * r
            dxh = dh * g_ref[...]
            dx = r * (dxh - xh * jnp.mean(dxh * xh, axis=-1, keepdims=True)) + r_ref[...]
            dx_ref[...] = dx
            dxb_ref[...] = dx.astype(BF16)
            part = jnp.sum(dh * xh, axis=0, keepdims=True)

            @pl.when(i == 0)
            def _():
                dg_ref[...] = part

            @pl.when(i > 0)
            def _():
                dg_ref[...] += part

    in_specs = []
    args = []
    for a, ao, _, _ in pairs:
        in_specs.append(pl.BlockSpec((tm, tkc), lambda i, k, o=ao // tkc: (i, k + o)))
        args.append(a)
    for _, _, w, wo in pairs:
        in_specs.append(pl.BlockSpec((D, tkc), lambda i, k, o=wo // tkc: (0, k + o)))
        args.append(w)
    row = pl.BlockSpec((tm, D), lambda i, k: (i, 0))
    vec = pl.BlockSpec((1, D), lambda i, k: (0, 0))
    in_specs += [row, vec, row]
    args += [x, g, dres]
    return pl.pallas_call(
        body, name=name, grid=(T // tm, nk), in_specs=in_specs, out_specs=[row, row, vec],
        out_shape=[SDS((T, D), F32), SDS((T, D), BF16), SDS((1, D), F32)],
        scratch_shapes=[pltpu.VMEM((tm, D), F32)],
        compiler_params=_cp("arbitrary", "arbitrary"))(*args)


def _loss_head(x, g, tgt, name, tm=512):
    T, D = x.shape
    tm = _tile(T, tm, SUBLANES)

    def body(x_ref, g_ref, t_ref, loss_ref, dx_ref, dxb_ref, dg_ref):
        i = pl.program_id(0)
        xv = x_ref[...]
        r = _rstd(xv)
        xh = xv * r
        e = xh * g_ref[...] - t_ref[...]
        part = 0.5 * jnp.sum(jnp.mean(e * e, axis=-1, keepdims=True), axis=0, keepdims=True)
        dy = e * (1.0 / D)
        dxh = dy * g_ref[...]
        dx = r * (dxh - xh * jnp.mean(dxh * xh, axis=-1, keepdims=True))
        dx_ref[...] = dx
        dxb_ref[...] = dx.astype(BF16)
        dgp = jnp.sum(dy * xh, axis=0, keepdims=True)

        @pl.when(i == 0)
        def _():
            loss_ref[...] = jnp.broadcast_to(part, loss_ref.shape)
            dg_ref[...] = dgp

        @pl.when(i > 0)
        def _():
            loss_ref[...] += jnp.broadcast_to(part, loss_ref.shape)
            dg_ref[...] += dgp

    row = pl.BlockSpec((tm, D), lambda i: (i, 0))
    vec = pl.BlockSpec((1, D), lambda i: (0, 0))
    return pl.pallas_call(
        body, name=name, grid=(T // tm,), in_specs=[row, vec, row],
        out_specs=[pl.BlockSpec((SUBLANES, LANES), lambda i: (0, 0)), row, row, vec],
        out_shape=[SDS((SUBLANES, LANES), F32), SDS((T, D), F32), SDS((T, D), BF16), SDS((1, D), F32)],
        compiler_params=_cp("arbitrary"))(x, g, tgt)


def _shift_down(u, k, prev_row):
    rows = lax.broadcasted_iota(jnp.int32, u.shape, 0)
    s = pltpu.roll(u, k, 0)
    for t in range(k):
        s = jnp.where(rows == t, prev_row(SUBLANES - k + t), s)
    return s


def _shift_up(u, k, next_row):
    n = u.shape[0]
    rows = lax.broadcasted_iota(jnp.int32, u.shape, 0)
    s = pltpu.roll(u, n - k, 0)
    for t in range(k):
        s = jnp.where(rows == n - k + t, next_row(t), s)
    return s


def _conv_taps(z_ref, zp_ref, w_ref, cw, first):
    u = z_ref[:, cw:2 * cw] * z_ref[:, 2 * cw:]

    def prev_row(r):
        return jnp.where(first, 0.0, zp_ref[r:r + 1, cw:2 * cw] * zp_ref[r:r + 1, 2 * cw:])

    u1 = _shift_down(u, 1, prev_row)
    u2 = _shift_down(u, 2, prev_row)
    cv = w_ref[0:1, :] * u2 + w_ref[1:2, :] * u1 + w_ref[2:3, :] * u
    return u, u1, u2, cv


def _conv_fwd(zc, wc, gc, name, tm=512):
    T = zc.shape[0]
    cw = zc.shape[1] // 3
    tm = _tile(T, tm, SUBLANES)
    hb = tm // SUBLANES

    def body(z_ref, zp_ref, w_ref, g_ref, o_ref):
        _, _, _, cv = _conv_taps(z_ref, zp_ref, w_ref, cw, pl.program_id(0) == 0)
        y = z_ref[:, :cw] * cv
        o_ref[...] = (y * _rstd(y) * g_ref[...]).astype(BF16)

    return pl.pallas_call(
        body, name=name, grid=(T // tm,),
        in_specs=[pl.BlockSpec((tm, 3 * cw), lambda i: (i, 0)),
                  pl.BlockSpec((SUBLANES, 3 * cw), lambda i: (jnp.maximum(i * hb - 1, 0), 0)),
                  pl.BlockSpec((SUBLANES, cw), lambda i: (0, 0)), pl.BlockSpec((1, cw), lambda i: (0, 0))],
        out_specs=pl.BlockSpec((tm, cw), lambda i: (i, 0)), out_shape=SDS((T, cw), BF16),
        compiler_params=_cp("parallel"))(zc, zc, wc, gc)


def _conv_bwd(zc, dy, wc, gc, name, tm=512):
    T = zc.shape[0]
    cw = zc.shape[1] // 3
    tm = _tile(T, tm, SUBLANES)
    hb = tm // SUBLANES
    nb = T // tm

    def body(z_ref, zp_ref, dy_ref, w_ref, g_ref, dz_ref, dw_ref, dg_ref, carry_ref):
        s = pl.program_id(0)
        u, u1, u2, cv = _conv_taps(z_ref, zp_ref, w_ref, cw, s == nb - 1)
        zb = z_ref[:, :cw]
        y = zb * cv
        r = _rstd(y)
        yh = y * r
        dyn = dy_ref[...]
        dyh = dyn * g_ref[...]
        dyc = r * (dyh - yh * jnp.mean(dyh * yh, axis=-1, keepdims=True))
        dcv = dyc * zb

        def next_row(t):
            return jnp.where(s == 0, 0.0, carry_ref[t:t + 1, :])

        du = w_ref[2:3, :] * dcv + w_ref[1:2, :] * _shift_up(dcv, 1, next_row) + w_ref[0:1, :] * _shift_up(dcv, 2, next_row)
        carry_ref[...] = dcv[0:SUBLANES, :]
        dz_ref[:, :cw] = (dyc * cv).astype(BF16)
        dz_ref[:, cw:2 * cw] = (du * z_ref[:, 2 * cw:]).astype(BF16)
        dz_ref[:, 2 * cw:] = (du * z_ref[:, cw:2 * cw]).astype(BF16)
        dgp = jnp.sum(dyn * yh, axis=0, keepdims=True)
        tap = lax.broadcasted_iota(jnp.int32, (SUBLANES, cw), 0)
        dwp = jnp.where(tap == 0, jnp.sum(dcv * u2, axis=0, keepdims=True),
                        jnp.where(tap == 1, jnp.sum(dcv * u1, axis=0, keepdims=True),
                                  jnp.where(tap == 2, jnp.sum(dcv * u, axis=0, keepdims=True), 0.0)))

        @pl.when(s == 0)
        def _():
            dw_ref[...] = dwp
            dg_ref[...] = dgp

        @pl.when(s > 0)
        def _():
            dw_ref[...] += dwp
            dg_ref[...] += dgp

    rev = lambda s: nb - 1 - s
    return pl.pallas_call(
        body, name=name, grid=(nb,),
        in_specs=[pl.BlockSpec((tm, 3 * cw), lambda s: (rev(s), 0)),
                  pl.BlockSpec((SUBLANES, 3 * cw), lambda s: (jnp.maximum(rev(s) * hb - 1, 0), 0)),
                  pl.BlockSpec((tm, cw), lambda s: (rev(s), 0)),
                  pl.BlockSpec((SUBLANES, cw), lambda s: (0, 0)), pl.BlockSpec((1, cw), lambda s: (0, 0))],
        out_specs=[pl.BlockSpec((tm, 3 * cw), lambda s: (rev(s), 0)), pl.BlockSpec((SUBLANES, cw), lambda s: (0, 0)),
                   pl.BlockSpec((1, cw), lambda s: (0, 0))],
        out_shape=[SDS((T, 3 * cw), BF16), SDS((SUBLANES, cw), F32), SDS((1, cw), F32)],
        scratch_shapes=[pltpu.VMEM((SUBLANES, cw), F32)],
        compiler_params=_cp("arbitrary"))(zc, zc, dy, wc, gc)


def _gnorm_fwd(y, g, name, tm=1024):
    T, W = y.shape
    tm = _tile(T, tm, SUBLANES)

    def body(y_ref, g_ref, o_ref):
        yv = y_ref[...]
        o_ref[...] = (yv * _rstd(yv) * g_ref[...]).astype(BF16)

    row = pl.BlockSpec((tm, W), lambda i: (i, 0))
    return pl.pallas_call(body, name=name, grid=(T // tm,), in_specs=[row, pl.BlockSpec((1, W), lambda i: (0, 0))],
                          out_specs=row, out_shape=SDS((T, W), BF16), compiler_params=_cp("parallel"))(y, g)


def _gnorm_bwd(y, g, dy, col_block, name, tm=1024):
    T, W = y.shape
    tm = _tile(T, tm, SUBLANES)

    def body(y_ref, g_ref, dy_ref, o_ref, dg_ref):
        i = pl.program_id(0)
        yv = y_ref[...]
        r = _rstd(yv)
        yh = yv * r
        dyn = dy_ref[...]
        dyh = dyn * g_ref[...]
        o_ref[...] = r * (dyh - yh * jnp.mean(dyh * yh, axis=-1, keepdims=True))
        dgp = jnp.sum(dyn * yh, axis=0, keepdims=True)

        @pl.when(i == 0)
        def _():
            dg_ref[...] = dgp

        @pl.when(i > 0)
        def _():
            dg_ref[...] += dgp

    row = pl.BlockSpec((tm, W), lambda i: (i, 0))
    vec = pl.BlockSpec((1, W), lambda i: (0, 0))
    return pl.pallas_call(
        body, name=name, grid=(T // tm,),
        in_specs=[row, vec, pl.BlockSpec((tm, W), lambda i: (i, col_block))], out_specs=[row, vec],
        out_shape=[SDS((T, W), F32), SDS((1, W), F32)], compiler_params=_cp("arbitrary"))(y, g, dy)


def _split3(v):
    hi = v.astype(BF16).astype(F32)
    r1 = v - hi
    mid = r1.astype(BF16).astype(F32)
    lo = (r1 - mid).astype(BF16).astype(F32)
    return hi, mid, lo


def _tri_dot3(v, tri):
    hi, mid, lo = _split3(v)
    return _nn(hi.astype(BF16), tri) + _nn(mid.astype(BF16), tri) + _nn(lo.astype(BF16), tri)


def _logf_cumsum(zft, b, name, tb=512):
    H, T = zft.shape
    tb = _tile(T, tb, LANES)

    def body(z_ref, b_ref, c_ref, hi_ref, mid_ref, lo_ref, carry_ref):
        @pl.when(pl.program_id(0) == 0)
        def _():
            carry_ref[...] = jnp.zeros_like(carry_ref)
        xv = z_ref[...] + b_ref[...]
        lf = jnp.minimum(xv, 0.0) - jnp.log(1.0 + jnp.exp(-jnp.abs(xv)))
        src = lax.broadcasted_iota(jnp.int32, (tb, tb), 0)
        dst = lax.broadcasted_iota(jnp.int32, (tb, tb), 1)
        tri = jnp.where(src <= dst, 1.0, 0.0).astype(BF16)
        cs = _tri_dot3(lf, tri) + carry_ref[...]
        c_ref[...] = cs
        hi, mid, lo = _split3(cs)
        hi_ref[...] = hi
        mid_ref[...] = mid
        lo_ref[...] = lo
        col = lax.broadcasted_iota(jnp.int32, cs.shape, 1)
        carry_ref[...] = jnp.sum(jnp.where(col == tb - 1, cs, 0.0), axis=1, keepdims=True)

    blk = pl.BlockSpec((H, tb), lambda i: (0, i))
    return pl.pallas_call(
        body, name=name, grid=(T // tb,), in_specs=[blk, pl.BlockSpec((H, 1), lambda i: (0, 0))],
        out_specs=[blk, blk, blk, blk], out_shape=[SDS((H, T), F32)] * 4,
        scratch_shapes=[pltpu.VMEM((H, 1), F32)], compiler_params=_cp("arbitrary"))(zft, b)


def _logf_cumsum_bwd(dcq, dck, zft, b, name, tb=512):
    H, T = zft.shape
    tb = _tile(T, tb, LANES)
    nb = T // tb

    def body(dq_ref, dk_ref, z_ref, b_ref, o_ref, db_ref, carry_ref):
        s = pl.program_id(0)

        @pl.when(s == 0)
        def _():
            carry_ref[...] = jnp.zeros_like(carry_ref)
        dc = dq_ref[...] - dk_ref[...]
        src = lax.broadcasted_iota(jnp.int32, (tb, tb), 0)
        dst = lax.broadcasted_iota(jnp.int32, (tb, tb), 1)
        tri = jnp.where(src >= dst, 1.0, 0.0).astype(BF16)
        dl = _tri_dot3(dc, tri) + carry_ref[...]
        col = lax.broadcasted_iota(jnp.int32, dl.shape, 1)
        carry_ref[...] = jnp.sum(jnp.where(col == 0, dl, 0.0), axis=1, keepdims=True)
        dz = dl * jax.nn.sigmoid(-(z_ref[...] + b_ref[...]))
        o_ref[...] = dz
        part = jnp.sum(dz, axis=1, keepdims=True)

        @pl.when(s == 0)
        def _():
            db_ref[...] = part

        @pl.when(s > 0)
        def _():
            db_ref[...] += part

    blk = pl.BlockSpec((H, tb), lambda s: (0, nb - 1 - s))
    vec = pl.BlockSpec((H, 1), lambda s: (0, 0))
    return pl.pallas_call(
        body, name=name, grid=(nb,), in_specs=[blk, blk, blk, vec], out_specs=[blk, vec],
        out_shape=[SDS((H, T), F32), SDS((H, 1), F32)],
        scratch_shapes=[pltpu.VMEM((H, 1), F32)], compiler_params=_cp("arbitrary"))(dcq, dck, zft, b)


def _causal_mask(B):
    krow = lax.broadcasted_iota(jnp.int32, (B, B), 0)
    qcol = lax.broadcasted_iota(jnp.int32, (B, B), 1)
    return krow <= qcol


def _attn_fwd(kaug, qaugt, vt, name):
    H, T, _ = kaug.shape
    _, nb, hd, B = vt.shape

    def body(k_ref, q_ref, v_ref, o_ref, lse_ref):
        i = pl.program_id(1)
        qt = q_ref[...]

        def block(j, carry, masked):
            m, l, acc = carry
            kj = k_ref[pl.ds(pl.multiple_of(j * B, B), B), :]
            s = _nn(kj, qt)
            if masked:
                s = jnp.where(_causal_mask(B), s, NEG)
            m_new = jnp.maximum(m, jnp.max(s, axis=0, keepdims=True))
            p = jnp.exp(s - m_new)
            a = jnp.exp(m - m_new)
            l = a * l + jnp.sum(p, axis=0, keepdims=True)
            acc = a * acc + _nn(v_ref[j], p.astype(BF16))
            return m_new, l, acc

        init = (jnp.full((1, B), NEG, F32), jnp.zeros((1, B), F32), jnp.zeros((hd, B), F32))
        carry = lax.fori_loop(0, i, lambda j, c: block(j, c, False), init)
        m, l, acc = block(i, carry, True)
        o_ref[...] = acc / l
        lse_ref[...] = m + jnp.log(l)

    return pl.pallas_call(
        body, name=name, grid=(H, nb),
        in_specs=[pl.BlockSpec((None, T, AUG), lambda h, i: (h, 0, 0)),
                  pl.BlockSpec((None, None, AUG, B), lambda h, i: (h, i, 0, 0)),
                  pl.BlockSpec((None, nb, hd, B), lambda h, i: (h, 0, 0, 0))],
        out_specs=[pl.BlockSpec((None, None, hd, B), lambda h, i: (h, i, 0, 0)),
                   pl.BlockSpec((None, None, 1, B), lambda h, i: (h, i, 0, 0))],
        out_shape=[SDS((H, nb, hd, B), F32), SDS((H, nb, 1, B), F32)],
        compiler_params=_cp("parallel", "parallel"))(kaug, qaugt, vt)


def _attn_delta(dot, ot, name):
    H, nb, hd, B = ot.shape

    def body(do_ref, o_ref, dl_ref, dob_ref):
        dov = do_ref[...]
        dl_ref[...] = jnp.sum(dov * o_ref[...], axis=0, keepdims=True)
        dob_ref[...] = dov.astype(BF16)

    blk = pl.BlockSpec((None, None, hd, B), lambda h, i: (h, i, 0, 0))
    return pl.pallas_call(
        body, name=name, grid=(H, nb), in_specs=[blk, blk],
        out_specs=[pl.BlockSpec((None, None, 1, B), lambda h, i: (h, i, 0, 0)), blk],
        out_shape=[SDS((H, nb, 1, B), F32), SDS((H, nb, hd, B), BF16)],
        compiler_params=_cp("parallel", "parallel"))(dot, ot)


def _attn_bwd(kaug, kaugt, v, qaugt, dot, lse, delta, name):
    H, T, _ = kaug.shape
    _, nb, hd, B = dot.shape

    def body(k_ref, kt_ref, v_ref, q_ref, do_ref, lse_ref, dl_ref, dq_ref, dk_ref, dv_ref):
        j = pl.program_id(1)

        @pl.when(j == 0)
        def _():
            dq_ref[...] = jnp.zeros_like(dq_ref)
        kj = k_ref[...]
        ktj = kt_ref[...]
        vj = v_ref[...]

        def block(i, carry, masked):
            dk, dv = carry
            qt = q_ref[i]
            dov = do_ref[i]
            s = _nn(kj, qt)
            if masked:
                s = jnp.where(_causal_mask(B), s, NEG)
            p = jnp.exp(s - lse_ref[i])
            dp = _nn(vj, dov)
            ds = (p * (dp - dl_ref[i])).astype(BF16)
            dv = dv + _nt(dov, p.astype(BF16))
            dk = dk + _nt(qt, ds)
            dq_ref[i] += _nn(ktj, ds)
            return dk, dv

        carry = block(j, (jnp.zeros((AUG, B), F32), jnp.zeros((hd, B), F32)), True)
        dk, dv = lax.fori_loop(j + 1, nb, lambda i, c: block(i, c, False), carry)
        dk_ref[...] = dk
        dv_ref[...] = dv

    qside = lambda r: pl.BlockSpec((None, nb, r, B), lambda h, j: (h, 0, 0, 0))
    kblk = lambda r: pl.BlockSpec((None, None, r, B), lambda h, j: (h, j, 0, 0))
    return pl.pallas_call(
        body, name=name, grid=(H, nb),
        in_specs=[pl.BlockSpec((None, B, AUG), lambda h, j: (h, j, 0)), kblk(AUG),
                  pl.BlockSpec((None, B, hd), lambda h, j: (h, j, 0)),
                  qside(AUG), qside(hd), qside(1), qside(1)],
        out_specs=[qside(AUG), kblk(AUG), kblk(hd)],
        out_shape=[SDS((H, nb, AUG, B), F32), SDS((H, nb, AUG, B), F32), SDS((H, nb, hd, B), F32)],
        compiler_params=_cp("arbitrary", "arbitrary"))(kaug, kaugt, v, qaugt, dot, lse, delta)


def _xattn_probs(q_ref, kv_ref, h, dx, scale):
    qh = q_ref[:, h * dx:(h + 1) * dx]
    kh = kv_ref[:, h * dx:(h + 1) * dx]
    s = _nt(qh, kh) * scale
    p = jnp.exp(s - jnp.max(s, axis=-1, keepdims=True))
    return qh, kh, p / jnp.sum(p, axis=-1, keepdims=True)


def _xattn_fwd(q, kv, name, tm=512):
    T, D = q.shape
    M = kv.shape[0]
    dx = D // N_XHEADS
    scale = dx ** -0.5
    tm = _tile(T, tm, SUBLANES)

    def body(q_ref, kv_ref, o_ref):
        for h in range(N_XHEADS):
            _, _, p = _xattn_probs(q_ref, kv_ref, h, dx, scale)
            vh = kv_ref[:, D + h * dx:D + (h + 1) * dx]
            o_ref[:, h * dx:(h + 1) * dx] = _nn(p.astype(BF16), vh).astype(BF16)

    row = pl.BlockSpec((tm, D), lambda i: (i, 0))
    return pl.pallas_call(body, name=name, grid=(T // tm,),
                          in_specs=[row, pl.BlockSpec((M, 2 * D), lambda i: (0, 0))], out_specs=row,
                          out_shape=SDS((T, D), BF16), compiler_params=_cp("parallel"))(q, kv)


def _xattn_bwd(q, kv, do, name, tm=512):
    T, D = q.shape
    M = kv.shape[0]
    dx = D // N_XHEADS
    scale = dx ** -0.5
    tm = _tile(T, tm, SUBLANES)

    def body(q_ref, kv_ref, do_ref, dq_ref, dkv_ref):
        @pl.when(pl.program_id(0) == 0)
        def _():
            dkv_ref[...] = jnp.zeros_like(dkv_ref)
        for h in range(N_XHEADS):
            qh, kh, p = _xattn_probs(q_ref, kv_ref, h, dx, scale)
            vh = kv_ref[:, D + h * dx:D + (h + 1) * dx]
            doh = do_ref[:, h * dx:(h + 1) * dx]
            dp = _nt(doh, vh)
            ds = (p * (dp - jnp.sum(p * dp, axis=-1, keepdims=True)) * scale).astype(BF16)
            dq_ref[:, h * dx:(h + 1) * dx] = _nn(ds, kh).astype(BF16)
            dkv_ref[:, h * dx:(h + 1) * dx] += _tn(ds, qh)
            dkv_ref[:, D + h * dx:D + (h + 1) * dx] += _tn(p.astype(BF16), doh)

    row = pl.BlockSpec((tm, D), lambda i: (i, 0))
    full = pl.BlockSpec((M, 2 * D), lambda i: (0, 0))
    return pl.pallas_call(body, name=name, grid=(T // tm,), in_specs=[row, full, row], out_specs=[row, full],
                          out_shape=[SDS((T, D), BF16), SDS((M, 2 * D), F32)],
                          compiler_params=_cp("arbitrary"))(q, kv, do)


def _mesh_position():
    return lax.axis_index("x"), lax.axis_index("y"), lax.axis_index("c")


def _all_gather(buf, name):
    R, C = buf.shape

    def body(x_ref, out_ref, send_sems, recv_sems, local_sem):
        x, y, c = _mesh_position()
        me, sibling = (x, y, c), (x, y, 1 - c)
        chips = [(1 - x, y), (x, 1 - y), (1 - x, 1 - y)]

        def slot(px, py, pc):
            return out_ref.at[4 * px + 2 * py + pc]

        def copy(k, block, to, src=None):
            return pltpu.make_async_remote_copy(
                src_ref=slot(*block) if src is None else src, dst_ref=slot(*block),
                send_sem=send_sems.at[k], recv_sem=recv_sems.at[k], device_id=to, device_id_type=pl.DeviceIdType.MESH)

        mine = pltpu.make_async_copy(x_ref, slot(*me), local_sem)
        mine.start()
        first = [copy(0, me, sibling, src=x_ref)]
        first += [copy(1 + j, me, (*chip, c), src=x_ref) for j, chip in enumerate(chips)]
        for cp in first:
            cp.start()
        passed = [copy(4 + j, (*chip, c), sibling) for j, chip in enumerate(chips)]
        for j, chip in enumerate(chips):
            copy(1 + j, (*chip, c), me).wait_recv()
            passed[j].start()
        copy(0, sibling, me).wait_recv()
        for j, chip in enumerate(chips):
            copy(4 + j, (*chip, 1 - c), me).wait_recv()
        for cp in first + passed:
            cp.wait_send()
        mine.wait()

    return pl.pallas_call(
        body, name=name, out_shape=SDS((N_DEV, R, C), buf.dtype),
        in_specs=[pl.BlockSpec(memory_space=pl.ANY)], out_specs=pl.BlockSpec(memory_space=pl.ANY),
        scratch_shapes=[pltpu.SemaphoreType.DMA((7,)), pltpu.SemaphoreType.DMA((7,)), pltpu.SemaphoreType.DMA(())],
    )(buf)


def _all_to_all(buf, name):
    _, R, C = buf.shape

    def body(x_ref, out_ref, send_sems, recv_sems, local_sem):
        x, y, c = _mesh_position()
        me = 4 * x + 2 * y + c
        mine = pltpu.make_async_copy(x_ref.at[me], out_ref.at[me], local_sem)
        mine.start()
        copies = []
        for k in range(1, N_DEV):
            px = 1 - x if k & 4 else x
            py = 1 - y if k & 2 else y
            pc = 1 - c if k & 1 else c
            peer = 4 * px + 2 * py + pc
            copies.append(pltpu.make_async_remote_copy(
                src_ref=x_ref.at[peer], dst_ref=out_ref.at[me], send_sem=send_sems.at[k - 1],
                recv_sem=recv_sems.at[k - 1], device_id=(px, py, pc), device_id_type=pl.DeviceIdType.MESH))
        for cp in copies:
            cp.start()
        for cp in copies:
            cp.wait_recv()
        for cp in copies:
            cp.wait_send()
        mine.wait()

    return pl.pallas_call(
        body, name=name, out_shape=SDS((N_DEV, R, C), buf.dtype),
        in_specs=[pl.BlockSpec(memory_space=pl.ANY)], out_specs=pl.BlockSpec(memory_space=pl.ANY),
        scratch_shapes=[pltpu.SemaphoreType.DMA((7,)), pltpu.SemaphoreType.DMA((7,)), pltpu.SemaphoreType.DMA(())],
    )(buf)


def _sum_adamw(parts, w, m, v, name, tr=256):
    _, R, C = parts.shape
    tr = _tile(R, tr, SUBLANES)

    def body(p_ref, w_ref, m_ref, v_ref, g_ref, d_ref, mo_ref, vo_ref):
        g = p_ref[0]
        for s in range(1, N_DEV):
            g = g + p_ref[s]
        mn = ADAM_B1 * m_ref[...] + (1.0 - ADAM_B1) * g
        vn = ADAM_B2 * v_ref[...] + (1.0 - ADAM_B2) * jnp.square(g)
        m_hat = mn / (1.0 - ADAM_B1 ** ADAM_STEP)
        v_hat = vn / (1.0 - ADAM_B2 ** ADAM_STEP)
        g_ref[...] = g
        d_ref[...] = -ADAM_LR * (m_hat / (jnp.sqrt(v_hat) + ADAM_EPS) + ADAM_WD * w_ref[...])
        mo_ref[...] = mn
        vo_ref[...] = vn

    row = pl.BlockSpec((tr, C), lambda i: (i, 0))
    return pl.pallas_call(
        body, name=name, grid=(R // tr,),
        in_specs=[pl.BlockSpec((N_DEV, tr, C), lambda i: (0, i, 0)), row, row, row], out_specs=[row] * 4,
        out_shape=[SDS((R, C), F32)] * 4, compiler_params=_cp("parallel"))(parts, w, m, v)


def _pack(flat_list, row_mult):
    flat = jnp.concatenate(flat_list, axis=-1)
    n = flat.shape[-1]
    chunk = row_mult * PACK_COLS
    pad = -n % chunk
    flat = jnp.pad(flat, [(0, 0)] * (flat.ndim - 1) + [(0, pad)])
    return flat.reshape(flat.shape[:-1] + ((n + pad) // PACK_COLS, PACK_COLS))


def _unpack(packed, shapes):
    lead = packed.shape[:-2]
    flat = packed.reshape(lead + (-1,))
    out = []
    off = 0
    for shp in shapes:
        n = 1
        for d in shp:
            n *= d
        out.append(flat[..., off:off + n].reshape(lead + tuple(shp)))
        off += n
    return out


def _to_shards(full, kind):
    L, a, b = full.shape
    if kind == 'col':
        return full.reshape(L, a, N_DEV, b // N_DEV).transpose(2, 0, 1, 3).reshape(N_DEV, -1)
    return full.reshape(L, N_DEV, a // N_DEV, b).transpose(1, 0, 2, 3).reshape(N_DEV, -1)


def _from_shards(shards, kind):
    _, L, a, b = shards.shape
    if kind == 'col':
        return shards.transpose(1, 2, 0, 3).reshape(L, a, N_DEV * b)
    return shards.transpose(1, 0, 2, 3).reshape(L, N_DEV * a, b)


def _heads_tb(a, B):
    T = a.shape[0]
    H = a.shape[1] // HEAD_DIM
    return a.reshape(T // B, B, H, HEAD_DIM).transpose(2, 0, 3, 1)


def _tb_heads(a):
    H, nb, r, B = a.shape
    return a.transpose(1, 3, 0, 2).reshape(nb * B, H * r)


def _rows_tb(a, B):
    H, T, r = a.shape
    return a.reshape(H, T // B, B, r).transpose(0, 1, 3, 2)


def _ffn_fwd(x, g, wgu, wd, tag):
    gate, up, act, h = _norm_mm_swiglu(x, g, wgu, f"{tag}_gu")
    xo = _mm_nn(act, wd, F32, f"{tag}_down", res=x, alpha=0.5)
    return xo, (x, g, h, gate, up, act)


def _ffn_bwd(dx, dxb, saved, wgu, wd, tag):
    x, g, h, gate, up, act = saved
    F = wd.shape[0]
    dgate, dup = _ffn_bwd_act(dxb, wd, gate, up, f"{tag}_bwd_act")
    dwd = _mm_tn(act, dxb, f"{tag}_dwd", alpha=0.5)
    dwgu = jnp.concatenate([_mm_tn(h, dgate, f"{tag}_dwg"), _mm_tn(h, dup, f"{tag}_dwu")], axis=1)
    dx, dxb, dg = _mm_nt_normbwd([(dgate, 0, wgu, 0), (dup, 0, wgu, F)], F, x, g, dx, f"{tag}_bwd_in")
    return dx, dxb, dg, dwgu, dwd


def _mix_fwd(x, g, w_in, w_conv8, b_f, g_conv, g_att, w_out, tag):
    T, D = x.shape
    cw = D // 2
    aw = D - cw
    H = aw // HEAD_DIM
    B = _tile(T, ATT_BLOCK, LANES)
    w_c = w_in[:, :3 * cw]
    w_a = w_in[:, 3 * cw:3 * cw + 3 * aw]
    w_f = jnp.pad(w_in[:, 3 * cw + 3 * aw:], ((0, 0), (0, LANES - H)))
    zc, h = _norm_mm(x, g, w_c, F32, f"{tag}_in_conv")
    za = _mm_nn(h, w_a, BF16, f"{tag}_in_att")
    zf = _mm_nn(h, w_f, F32, f"{tag}_in_f")
    zft = zf[:, :H].T
    c, chi, cmid, clo = _logf_cumsum(zft, b_f.reshape(H, 1), f"{tag}_cumsum")

    def col(a):
        return a[:, :, None].astype(BF16)

    q3 = za[:, :aw].reshape(T, H, HEAD_DIM).transpose(1, 0, 2)
    k3 = za[:, aw:2 * aw].reshape(T, H, HEAD_DIM).transpose(1, 0, 2)
    v3 = za[:, 2 * aw:].reshape(T, H, HEAD_DIM).transpose(1, 0, 2)
    one = jnp.ones((H, T, 3), BF16)
    zero = jnp.zeros((H, T, AUG - HEAD_DIM - 6), BF16)
    kaug = jnp.concatenate([k3, col(-chi), col(-cmid), col(-clo), one, zero], axis=2)
    qaug = jnp.concatenate([q3 * 0.125, one, col(chi), col(cmid), col(clo), zero], axis=2)
    kaug_s = jnp.concatenate([k3 * 0.125, col(-chi), col(-cmid), col(-clo), one, zero], axis=2)
    qaugt = _rows_tb(qaug, B)
    kaugt = _rows_tb(kaug_s, B)
    vt = _rows_tb(v3, B)
    ot, lse = _attn_fwd(kaug, qaugt, vt, f"{tag}_attn")
    y_att = _tb_heads(ot)
    ync = _conv_fwd(zc, w_conv8, g_conv, f"{tag}_conv")
    yna = _gnorm_fwd(y_att, g_att, f"{tag}_gnorm")
    y = jnp.concatenate([ync, yna], axis=1)
    xo = _mm_nn(y, w_out, F32, f"{tag}_out", res=x)
    saved = (x, g, h, zc, zft, b_f, kaug, kaugt, v3, qaugt, ot, lse, y_att, y, w_c, w_a, w_f)
    return xo, saved


def _mix_bwd(dx, dxb, saved, w_conv8, g_conv, g_att, w_out, tag):
    x, g, h, zc, zft, b_f, kaug, kaugt, v3, qaugt, ot, lse, y_att, y, w_c, w_a, w_f = saved
    T, D = x.shape
    cw = D // 2
    H = zft.shape[0]
    B = ot.shape[-1]
    dw_out = _mm_tn(y, dxb, f"{tag}_dwout")
    dy = _mm_nt(dxb, w_out, F32, f"{tag}_bwd_out")
    dzc, dwc8, dg_conv = _conv_bwd(zc, dy, w_conv8, g_conv, f"{tag}_conv_bwd")
    dy_att, dg_att = _gnorm_bwd(y_att, g_att, dy, 1, f"{tag}_gnorm_bwd")
    delta, dotb = _attn_delta(_heads_tb(dy_att, B), ot, f"{tag}_attn_delta")
    dqt, dkt, dvt = _attn_bwd(kaug, kaugt, v3, qaugt, dotb, lse, delta, f"{tag}_attn_bwd")
    dq = _tb_heads(dqt[:, :, :HEAD_DIM, :])
    dk = _tb_heads(dkt[:, :, :HEAD_DIM, :])
    dv = _tb_heads(dvt)
    dcq = dqt[:, :, HEAD_DIM + 3, :].reshape(H, T)
    dck = dkt[:, :, HEAD_DIM, :].reshape(H, T)
    dzft, db = _logf_cumsum_bwd(dcq, dck, zft, b_f.reshape(H, 1), f"{tag}_cumsum_bwd")
    dzf = jnp.pad(dzft.T, ((0, 0), (0, LANES - H))).astype(BF16)
    dz = jnp.concatenate([dzc, dq.astype(BF16), dk.astype(BF16), dv.astype(BF16), dzf], axis=1)
    w_pad = jnp.concatenate([w_c, w_a, w_f], axis=1)
    n_in = w_pad.shape[1] - LANES + H
    dw_in = _mm_tn(h, dz, f"{tag}_dwin", tn=640)[:, :n_in]
    dx, dxb, dg = _mm_nt_normbwd([(dz, 0, w_pad, 0)], dz.shape[1], x, g, dx, f"{tag}_bwd_in", tkc=640)
    return dx, dxb, dg, dw_in, dwc8[:3], db.reshape(H), dg_conv, dg_att, dw_out


def _xattn_block_fwd(x, g, mem, g_mem, w_q, w_kv, w_o, tag):
    q, h = _norm_mm(x, g, w_q, BF16, f"{tag}_q")
    kv, hm = _norm_mm(mem, g_mem, w_kv, BF16, f"{tag}_kv", tn=1024)
    o = _xattn_fwd(q, kv, f"{tag}_attn")
    xo = _mm_nn(o, w_o, F32, f"{tag}_o", res=x)
    return xo, (x, g, h, mem, g_mem, hm, q, kv, o)


def _xattn_block_bwd(dx, dxb, saved, w_q, w_kv, w_o, tag):
    x, g, h, mem, g_mem, hm, q, kv, o = saved
    dw_o = _mm_tn(o, dxb, f"{tag}_dwo")
    do = _mm_nt(dxb, w_o, BF16, f"{tag}_bwd_o")
    dq, dkv = _xattn_bwd(q, kv, do, f"{tag}_attn_bwd")
    dkvb = dkv.astype(BF16)
    dw_q = _mm_tn(h, dq, f"{tag}_dwq")
    dw_kv = _mm_tn(hm, dkvb, f"{tag}_dwkv")
    _, _, dg_mem = _mm_nt_normbwd([(dkvb, 0, w_kv, 0)], dkvb.shape[1], mem, g_mem, jnp.zeros_like(mem), f"{tag}_bwd_kv")
    dx, dxb, dg = _mm_nt_normbwd([(dq, 0, w_q, 0)], dq.shape[1], x, g, dx, f"{tag}_bwd_q")
    return dx, dxb, dg, dg_mem, dw_q, dw_kv, dw_o


def _local_step(x, mem, tgt, W):
    L = W['g_ffn1'].shape[0]
    saved = []
    for l in range(L):
        row = lambda n: W[n][l][None, :]
        wc8 = jnp.pad(W['w_conv'][l], ((0, SUBLANES - 3), (0, 0)))
        x, s1 = _ffn_fwd(x, row('g_ffn1'), W['w_ffn1_gu'][l], W['w_ffn1_down'][l], f"l{l}_ffn1")
        x, s2 = _mix_fwd(x, row('g_mix'), W['w_mix_in'][l], wc8, W['b_f'][l], row('g_conv_out'), row('g_att_out'),
                         W['w_mix_out'][l], f"l{l}_mix")
        x, s3 = _xattn_block_fwd(x, row('g_xattn'), mem, row('g_mem'), W['w_xq'][l], W['w_xkv'][l], W['w_xo'][l],
                                 f"l{l}_xattn")
        x, s4 = _ffn_fwd(x, row('g_ffn2'), W['w_ffn2_gu'][l], W['w_ffn2_down'][l], f"l{l}_ffn2")
        saved.append((s1, s2, s3, s4, wc8))
    loss, dx, dxb, dg_final = _loss_head(x, W['g_final'][None, :], tgt, "loss_head")
    G = {n: [None] * L for n in WEIGHT_NAMES if n != 'g_final'}
    for l in reversed(range(L)):
        row = lambda n: W[n][l][None, :]
        s1, s2, s3, s4, wc8 = saved[l]
        dx, dxb, G['g_ffn2'][l], G['w_ffn2_gu'][l], G['w_ffn2_down'][l] = _ffn_bwd(
            dx, dxb, s4, W['w_ffn2_gu'][l], W['w_ffn2_down'][l], f"l{l}_ffn2")
        dx, dxb, G['g_xattn'][l], G['g_mem'][l], G['w_xq'][l], G['w_xkv'][l], G['w_xo'][l] = _xattn_block_bwd(
            dx, dxb, s3, W['w_xq'][l], W['w_xkv'][l], W['w_xo'][l], f"l{l}_xattn")
        (dx, dxb, G['g_mix'][l], G['w_mix_in'][l], G['w_conv'][l], G['b_f'][l], G['g_conv_out'][l], G['g_att_out'][l],
         G['w_mix_out'][l]) = _mix_bwd(dx, dxb, s2, wc8, row('g_conv_out'), row('g_att_out'), W['w_mix_out'][l],
                                       f"l{l}_mix")
        dx, dxb, G['g_ffn1'][l], G['w_ffn1_gu'][l], G['w_ffn1_down'][l] = _ffn_bwd(
            dx, dxb, s1, W['w_ffn1_gu'][l], W['w_ffn1_down'][l], f"l{l}_ffn1")
    grads = {n: jnp.stack([a.reshape(W[n].shape[1:]) for a in G[n]]) for n in G}
    grads['g_final'] = dg_final.reshape(-1)
    return loss, dx, grads


def kernel(x, mem, g_ffn1, w_ffn1_gu, w_ffn1_down, g_mix, w_mix_in, w_conv, b_f, g_conv_out, g_att_out, w_mix_out, g_xattn, g_mem, w_xq, w_xkv, w_xo, g_ffn2, w_ffn2_gu, w_ffn2_down, g_final, loss_target, m_g_ffn1, m_w_ffn1_gu, m_w_ffn1_down, m_g_mix, m_w_mix_in, m_w_conv, m_b_f, m_g_conv_out, m_g_att_out, m_w_mix_out, m_g_xattn, m_g_mem, m_w_xq, m_w_xkv, m_w_xo, m_g_ffn2, m_w_ffn2_gu, m_w_ffn2_down, m_g_final, v_g_ffn1, v_w_ffn1_gu, v_w_ffn1_down, v_g_mix, v_w_mix_in, v_w_conv, v_b_f, v_g_conv_out, v_g_att_out, v_w_mix_out, v_g_xattn, v_g_mem, v_w_xq, v_w_xkv, v_w_xo, v_g_ffn2, v_w_ffn2_gu, v_w_ffn2_down, v_g_final):
    args = (x, mem, g_ffn1, w_ffn1_gu, w_ffn1_down, g_mix, w_mix_in, w_conv, b_f, g_conv_out, g_att_out, w_mix_out,
            g_xattn, g_mem, w_xq, w_xkv, w_xo, g_ffn2, w_ffn2_gu, w_ffn2_down, g_final)
    P = dict(zip(IN_NAMES, args))
    moms = (m_g_ffn1, m_w_ffn1_gu, m_w_ffn1_down, m_g_mix, m_w_mix_in, m_w_conv, m_b_f, m_g_conv_out, m_g_att_out,
            m_w_mix_out, m_g_xattn, m_g_mem, m_w_xq, m_w_xkv, m_w_xo, m_g_ffn2, m_w_ffn2_gu, m_w_ffn2_down, m_g_final)
    vars_ = (v_g_ffn1, v_w_ffn1_gu, v_w_ffn1_down, v_g_mix, v_w_mix_in, v_w_conv, v_b_f, v_g_conv_out, v_g_att_out,
             v_w_mix_out, v_g_xattn, v_g_mem, v_w_xq, v_w_xkv, v_w_xo, v_g_ffn2, v_w_ffn2_gu, v_w_ffn2_down, v_g_final)
    MOM = dict(zip(WEIGHT_NAMES, moms))
    VAR = dict(zip(WEIGHT_NAMES, vars_))

    mats = [n for n, _ in SHARDED if n != 'w_conv']
    shard_shapes = [P[n].shape for n in mats]
    gathered = _all_gather(_pack([P[n].astype(BF16).reshape(-1) for n in mats], 2 * SUBLANES), "gather_weights")
    W = {n: _from_shards(s, SHARD_KIND[n]) for n, s in zip(mats, _unpack(gathered, shard_shapes))}
    conv_g = _all_gather(_pack([w_conv.reshape(-1)], SUBLANES), "gather_conv")
    W['w_conv'] = _from_shards(_unpack(conv_g, [w_conv.shape])[0], 'col')
    for n in REPLICATED:
        W[n] = P[n]

    loss_part, dx, grads = _local_step(x[0], mem[0], loss_target[0], W)

    big = [n for n, _ in SHARDED]
    parts = _all_to_all(_pack([_to_shards(grads[n], SHARD_KIND[n]) for n in big], SUBLANES), "scatter_grads")
    small_shapes = [P[n].shape for n in REPLICATED] + [(1,)]
    small = _pack([grads[n].reshape(-1) for n in REPLICATED] + [loss_part[0, :1]], SUBLANES)
    small_parts = _all_gather(small, "gather_small_grads")

    def update(parts_, names, extra):
        shapes = [P[n].shape for n in names] + extra
        pad = [jnp.zeros((1,), F32)] * len(extra)
        w_, m_, v_ = (_pack([d[n].reshape(-1) for n in names] + pad, SUBLANES) for d in (P, MOM, VAR))
        outs = _sum_adamw(parts_, w_, m_, v_, "adamw_" + names[0])
        return [dict(zip(names + ['loss'], _unpack(o, shapes))) for o in outs]

    gb, db, mb, vb = update(parts, big, [])
    gs, ds, ms, vs = update(small_parts, list(REPLICATED), [(1,)])
    out = []
    for src_b, src_s in ((gb, gs), (db, ds), (mb, ms), (vb, vs)):
        out += [src_b[n] if n in SHARD_KIND else src_s[n] for n in WEIGHT_NAMES]
    return (gs['loss'].reshape(()), dx[None], *out)
```

```python
import jax
import jax.numpy as jnp
from jax import lax
from jax.experimental import pallas as pl
from jax.experimental.pallas import tpu as pltpu

F32 = jnp.float32
BF16 = jnp.bfloat16
I32 = jnp.int32
SDS = jax.ShapeDtypeStruct

EPS = 1e-6
HEAD_DIM = 64
N_XHEADS = 4
N_DEV = 8
LANES = 128
SUBLANES = 8
AUG = 16
ATT_BLOCK = 512
NEG = -1e30
SKIP_MARGIN = 115.0
PACK_COLS = 1024
VMEM_LIMIT = 58 * 1024 * 1024

ADAM_LR = 0.001
ADAM_B1 = 0.9
ADAM_B2 = 0.999
ADAM_EPS = 1e-08
ADAM_WD = 0.01
ADAM_STEP = 10

IN_NAMES = ['x', 'mem', 'g_ffn1', 'w_ffn1_gu', 'w_ffn1_down', 'g_mix', 'w_mix_in', 'w_conv', 'b_f', 'g_conv_out',
            'g_att_out', 'w_mix_out', 'g_xattn', 'g_mem', 'w_xq', 'w_xkv', 'w_xo', 'g_ffn2', 'w_ffn2_gu',
            'w_ffn2_down', 'g_final']
WEIGHT_NAMES = IN_NAMES[2:]
SHARDED = ['w_ffn1_gu', 'w_ffn1_down', 'w_mix_in', 'w_conv', 'w_mix_out', 'w_xq', 'w_xkv', 'w_xo', 'w_ffn2_gu',
           'w_ffn2_down']
REPLICATED = [n for n in WEIGHT_NAMES if n not in SHARDED]


def _tile(n, pref, mult):
    t = min(pref, n) // mult * mult
    while t >= mult:
        if n % t == 0:
            return t
        t -= mult
    return n


def _cp(*sem):
    return pltpu.CompilerParams(dimension_semantics=sem, vmem_limit_bytes=VMEM_LIMIT)


def _nt(a, b):
    return lax.dot_general(a, b, (((1,), (1,)), ((), ())), preferred_element_type=F32)


def _tn(a, b):
    return lax.dot_general(a, b, (((0,), (0,)), ((), ())), preferred_element_type=F32)


def _nn(a, b):
    return jnp.dot(a, b, preferred_element_type=F32)


def _rstd(xv):
    return lax.rsqrt(jnp.mean(xv * xv, axis=-1, keepdims=True) + EPS)


def _full(a):
    nd = a.ndim
    return pl.BlockSpec(a.shape, lambda *_: (0,) * nd)


def _slots(w, l):
    return pl.BlockSpec((N_DEV, None) + w.shape[2:], lambda *_: (0, l, 0, 0))


def _rows2d(w_ref):
    s, a, b = w_ref.shape
    return w_ref[...].reshape(s * a, b)


def _accumulate(ref, first, part):
    @pl.when(first)
    def _():
        ref[...] = part

    @pl.when(jnp.logical_not(first))
    def _():
        ref[...] += part


def _norm_bwd_store(dh, x_ref, g_ref, r_ref, dx_ref, dxb_ref, dg_ref, first):
    xv = x_ref[...]
    r = _rstd(xv)
    xh = xv * r
    dxh = dh * g_ref[...]
    dx = r * (dxh - xh * jnp.mean(dxh * xh, axis=-1, keepdims=True)) + r_ref[...]
    dx_ref[...] = dx
    dxb_ref[...] = dx.astype(BF16)
    _accumulate(dg_ref, first, jnp.sum(dh * xh, axis=0, keepdims=True))


def _norm_mm(x, g, w, l, out_dtype, name, tm=512):
    T, D = x.shape
    N = w.shape[-1]
    tm = _tile(T, tm, SUBLANES)

    def body(x_ref, g_ref, w_ref, o_ref, h_ref):
        xv = x_ref[...]
        h = (xv * _rstd(xv) * g_ref[...]).astype(BF16)
        h_ref[...] = h
        o_ref[...] = _nn(h, w_ref[...] if l is None else _rows2d(w_ref)).astype(o_ref.dtype)

    row = lambda n: pl.BlockSpec((tm, n), lambda i: (i, 0))
    return pl.pallas_call(
        body, name=name, grid=(T // tm,),
        in_specs=[row(D), _full(g), _full(w) if l is None else _slots(w, l)], out_specs=[row(N), row(D)],
        out_shape=[SDS((T, N), out_dtype), SDS((T, D), BF16)], compiler_params=_cp("parallel"))(x, g, w)


def _mm_res(a_list, w, l, res, name, tm=512):
    T = res.shape[0]
    N = w.shape[-1]
    tm = _tile(T, tm, SUBLANES)
    n_a = len(a_list)

    def body(*refs):
        a_refs = refs[:n_a]
        w_ref, r_ref, o_ref = refs[n_a:]
        wv = _rows2d(w_ref)
        acc = r_ref[...]
        off = 0
        for a_ref in a_refs:
            k = a_ref.shape[1]
            acc = acc + _nn(a_ref[...], wv[off:off + k, :])
            off += k
        o_ref[...] = acc

    row = lambda n: pl.BlockSpec((tm, n), lambda i: (i, 0))
    return pl.pallas_call(
        body, name=name, grid=(T // tm,),
        in_specs=[row(a.shape[1]) for a in a_list] + [_slots(w, l), row(N)], out_specs=row(N),
        out_shape=SDS((T, N), F32), compiler_params=_cp("parallel"))(*a_list, w, res)


def _mm_nt(a, w, l, out_dtype, name, tm=512):
    T = a.shape[0]
    K = N_DEV * w.shape[2]
    tm = _tile(T, tm, SUBLANES)

    def body(a_ref, w_ref, o_ref):
        o_ref[...] = _nt(a_ref[...], _rows2d(w_ref)).astype(o_ref.dtype)

    return pl.pallas_call(
        body, name=name, grid=(T // tm,),
        in_specs=[pl.BlockSpec((tm, a.shape[1]), lambda i: (i, 0)), _slots(w, l)],
        out_specs=pl.BlockSpec((tm, K), lambda i: (i, 0)),
        out_shape=SDS((T, K), out_dtype), compiler_params=_cp("parallel"))(a, w)


def _mm_tn(a, b, name, tt=2048):
    T, K = a.shape
    N = b.shape[1]
    tt = _tile(T, tt, 2 * SUBLANES)

    def body(a_ref, b_ref, o_ref):
        _accumulate(o_ref, pl.program_id(0) == 0, _tn(a_ref[...], b_ref[...]))

    return pl.pallas_call(
        body, name=name, grid=(T // tt,),
        in_specs=[pl.BlockSpec((tt, K), lambda t: (t, 0)), pl.BlockSpec((tt, N), lambda t: (t, 0))],
        out_specs=pl.BlockSpec((K, N), lambda t: (0, 0)),
        out_shape=SDS((K, N), F32), compiler_params=_cp("arbitrary"))(a, b)


def _mm_nt_normbwd(a, w, l, x, g, dres, name, tm=512):
    T, D = x.shape
    tm = _tile(T, tm, SUBLANES)

    def body(a_ref, w_ref, x_ref, g_ref, r_ref, dx_ref, dxb_ref, dg_ref):
        dh = _nt(a_ref[...], _rows2d(w_ref))
        _norm_bwd_store(dh, x_ref, g_ref, r_ref, dx_ref, dxb_ref, dg_ref, pl.program_id(0) == 0)

    row = lambda n: pl.BlockSpec((tm, n), lambda i: (i, 0))
    return pl.pallas_call(
        body, name=name, grid=(T // tm,),
        in_specs=[row(a.shape[1]), _slots(w, l), row(D), _full(g), row(D)],
        out_specs=[row(D), row(D), pl.BlockSpec((1, D), lambda i: (0, 0))],
        out_shape=[SDS((T, D), F32), SDS((T, D), BF16), SDS((1, D), F32)],
        compiler_params=_cp("arbitrary"))(a, w, x, g, dres)


def _ffn_gu(x, g, wgu, l, name, tm=512):
    T, D = x.shape
    fb = wgu.shape[3]
    nh = N_DEV // 2
    tm = _tile(T, tm, SUBLANES)

    def body(x_ref, g_ref, w_ref, gu_ref, act_ref, h_ref):
        xv = x_ref[...]
        h = (xv * _rstd(xv) * g_ref[...]).astype(BF16)
        h_ref[...] = h
        for d in range(nh):
            gt = _nn(h, w_ref[d])
            up = _nn(h, w_ref[d + nh])
            gu_ref[d] = gt.astype(BF16)
            gu_ref[d + nh] = up.astype(BF16)
            act_ref[d] = (gt * jax.nn.sigmoid(gt) * up).astype(BF16)

    row = pl.BlockSpec((tm, D), lambda i: (i, 0))
    blocks = lambda n: pl.BlockSpec((n, tm, fb), lambda i: (0, i, 0))
    return pl.pallas_call(
        body, name=name, grid=(T // tm,), in_specs=[row, _full(g), _slots(wgu, l)],
        out_specs=[blocks(N_DEV), blocks(nh), row],
        out_shape=[SDS((N_DEV, T, fb), BF16), SDS((nh, T, fb), BF16), SDS((T, D), BF16)],
        compiler_params=_cp("parallel"))(x, g, wgu)


def _ffn_down(act, wd, l, res, name, tm=1024):
    nh, T, fb = act.shape
    D = wd.shape[3]
    tm = _tile(T, tm, SUBLANES)

    def body(a_ref, w_ref, r_ref, o_ref):
        wv = w_ref[...].reshape(nh, fb, D)
        acc = _nn(a_ref[0], wv[0])
        for d in range(1, nh):
            acc = acc + _nn(a_ref[d], wv[d])
        o_ref[...] = r_ref[...] + 0.5 * acc

    row = pl.BlockSpec((tm, D), lambda i: (i, 0))
    return pl.pallas_call(
        body, name=name, grid=(T // tm,),
        in_specs=[pl.BlockSpec((nh, tm, fb), lambda i: (0, i, 0)), _slots(wd, l), row], out_specs=row,
        out_shape=SDS((T, D), F32), compiler_params=_cp("parallel"))(act, wd, res)


def _ffn_bwd_act(dyb, wd, l, gu, name, tm=512):
    T, D = dyb.shape
    _, _, fb = gu.shape
    nh = N_DEV // 2
    tm = _tile(T, tm, SUBLANES)

    def body(dy_ref, w_ref, gu_ref, o_ref):
        wv = w_ref[...].reshape(nh, fb, D)
        dy = dy_ref[...]
        for d in range(nh):
            da = 0.5 * _nt(dy, wv[d])
            gt = gu_ref[d].astype(F32)
            up = gu_ref[d + nh].astype(F32)
            sg = jax.nn.sigmoid(gt)
            o_ref[d] = (da * up * (sg * (1.0 + gt * (1.0 - sg)))).astype(BF16)
            o_ref[d + nh] = (da * (gt * sg)).astype(BF16)

    blocks = pl.BlockSpec((N_DEV, tm, fb), lambda i: (0, i, 0))
    return pl.pallas_call(
        body, name=name, grid=(T // tm,),
        in_specs=[pl.BlockSpec((tm, D), lambda i: (i, 0)), _slots(wd, l), blocks], out_specs=blocks,
        out_shape=SDS((N_DEV, T, fb), BF16), compiler_params=_cp("parallel"))(dyb, wd, gu)


def _ffn_dwd(act, dyb, name, tt=2048):
    nh, T, fb = act.shape
    D = dyb.shape[1]
    tt = _tile(T, tt, 2 * SUBLANES)
    nt = T // tt

    def body(a_ref, b_ref, o_ref):
        t = pl.program_id(1)

        @pl.when(t == 0)
        def _():
            o_ref[...] = jnp.zeros_like(o_ref)
        for s in range(2):
            o_ref[s] += _tn(a_ref[s], b_ref[...])

        @pl.when(t == nt - 1)
        def _():
            o_ref[...] = 0.5 * o_ref[...]

    return pl.pallas_call(
        body, name=name, grid=(nh // 2, nt),
        in_specs=[pl.BlockSpec((2, tt, fb), lambda j, t: (j, t, 0)), pl.BlockSpec((tt, D), lambda j, t: (t, 0))],
        out_specs=pl.BlockSpec((2, fb, D), lambda j, t: (j, 0, 0)),
        out_shape=SDS((nh, fb, D), F32), compiler_params=_cp("parallel", "arbitrary"))(act, dyb)


def _ffn_dwgu(h, dgu, name, tt=2048):
    T, D = h.shape
    _, _, fb = dgu.shape
    tt = _tile(T, tt, 2 * SUBLANES)

    def body(a_ref, b_ref, o_ref):
        @pl.when(pl.program_id(1) == 0)
        def _():
            o_ref[...] = jnp.zeros_like(o_ref)
        for s in range(2):
            o_ref[s] += _tn(a_ref[...], b_ref[s])

    return pl.pallas_call(
        body, name=name, grid=(N_DEV // 2, T // tt),
        in_specs=[pl.BlockSpec((tt, D), lambda j, t: (t, 0)), pl.BlockSpec((2, tt, fb), lambda j, t: (j, t, 0))],
        out_specs=pl.BlockSpec((2, D, fb), lambda j, t: (j, 0, 0)),
        out_shape=SDS((N_DEV, D, fb), F32), compiler_params=_cp("parallel", "arbitrary"))(h, dgu)


def _ffn_bwd_in(dgu, wgu, l, x, g, dres, name, tm=512):
    T, D = x.shape
    _, _, fb = dgu.shape
    tm = _tile(T, tm, SUBLANES)

    def body(a_ref, w_ref, x_ref, g_ref, r_ref, dx_ref, dxb_ref, dg_ref):
        dh = _nt(a_ref[0], w_ref[0])
        for d in range(1, N_DEV):
            dh = dh + _nt(a_ref[d], w_ref[d])
        _norm_bwd_store(dh, x_ref, g_ref, r_ref, dx_ref, dxb_ref, dg_ref, pl.program_id(0) == 0)

    row = pl.BlockSpec((tm, D), lambda i: (i, 0))
    return pl.pallas_call(
        body, name=name, grid=(T // tm,),
        in_specs=[pl.BlockSpec((N_DEV, tm, fb), lambda i: (0, i, 0)), _slots(wgu, l), row, _full(g), row],
        out_specs=[row, row, pl.BlockSpec((1, D), lambda i: (0, 0))],
        out_shape=[SDS((T, D), F32), SDS((T, D), BF16), SDS((1, D), F32)],
        compiler_params=_cp("arbitrary"))(dgu, wgu, x, g, dres)


def _loss_head(x, g, tgt, name, tm=512):
    T, D = x.shape
    tm = _tile(T, tm, SUBLANES)

    def body(x_ref, g_ref, t_ref, loss_ref, dx_ref, dxb_ref, dg_ref):
        first = pl.program_id(0) == 0
        xv = x_ref[...]
        r = _rstd(xv)
        xh = xv * r
        e = xh * g_ref[...] - t_ref[...]
        part = 0.5 * jnp.sum(jnp.mean(e * e, axis=-1, keepdims=True), axis=0, keepdims=True)
        dy = e * (1.0 / D)
        dxh = dy * g_ref[...]
        dx = r * (dxh - xh * jnp.mean(dxh * xh, axis=-1, keepdims=True))
        dx_ref[...] = dx
        dxb_ref[...] = dx.astype(BF16)
        _accumulate(loss_ref, first, jnp.broadcast_to(part, loss_ref.shape))
        _accumulate(dg_ref, first, jnp.sum(dy * xh, axis=0, keepdims=True))

    row = pl.BlockSpec((tm, D), lambda i: (i, 0))
    vec = pl.BlockSpec((1, D), lambda i: (0, 0))
    return pl.pallas_call(
        body, name=name, grid=(T // tm,), in_specs=[row, vec, row],
        out_specs=[pl.BlockSpec((SUBLANES, LANES), lambda i: (0, 0)), row, row, vec],
        out_shape=[SDS((SUBLANES, LANES), F32), SDS((T, D), F32), SDS((T, D), BF16), SDS((1, D), F32)],
        compiler_params=_cp("arbitrary"))(x, g, tgt)


def _shift_down(u, k, prev_row):
    rows = lax.broadcasted_iota(I32, u.shape, 0)
    s = pltpu.roll(u, k, 0)
    for t in range(k):
        s = jnp.where(rows == t, prev_row(SUBLANES - k + t), s)
    return s


def _shift_up(u, k, next_row):
    n = u.shape[0]
    rows = lax.broadcasted_iota(I32, u.shape, 0)
    s = pltpu.roll(u, n - k, 0)
    for t in range(k):
        s = jnp.where(rows == n - k + t, next_row(t), s)
    return s


def _conv_taps(z_ref, zp_ref, w_ref, cw, first):
    u = z_ref[:, cw:2 * cw] * z_ref[:, 2 * cw:]

    def prev_row(r):
        return jnp.where(first, 0.0, zp_ref[r:r + 1, cw:2 * cw] * zp_ref[r:r + 1, 2 * cw:])

    u1 = _shift_down(u, 1, prev_row)
    u2 = _shift_down(u, 2, prev_row)
    cv = w_ref[0:1, :] * u2 + w_ref[1:2, :] * u1 + w_ref[2:3, :] * u
    return u, u1, u2, cv


def _conv_fwd(zc, wc, gc, name, tm=512):
    T = zc.shape[0]
    cw = zc.shape[1] // 3
    tm = _tile(T, tm, SUBLANES)
    hb = tm // SUBLANES

    def body(z_ref, zp_ref, w_ref, g_ref, o_ref):
        _, _, _, cv = _conv_taps(z_ref, zp_ref, w_ref, cw, pl.program_id(0) == 0)
        y = z_ref[:, :cw] * cv
        o_ref[...] = (y * _rstd(y) * g_ref[...]).astype(BF16)

    return pl.pallas_call(
        body, name=name, grid=(T // tm,),
        in_specs=[pl.BlockSpec((tm, 3 * cw), lambda i: (i, 0)),
                  pl.BlockSpec((SUBLANES, 3 * cw), lambda i: (jnp.maximum(i * hb - 1, 0), 0)), _full(wc), _full(gc)],
        out_specs=pl.BlockSpec((tm, cw), lambda i: (i, 0)), out_shape=SDS((T, cw), BF16),
        compiler_params=_cp("parallel"))(zc, zc, wc, gc)


def _conv_bwd(zc, dy, wc, gc, name, tm=512):
    T = zc.shape[0]
    cw = zc.shape[1] // 3
    tm = _tile(T, tm, SUBLANES)
    hb = tm // SUBLANES
    nb = T // tm

    def body(z_ref, zp_ref, dy_ref, w_ref, g_ref, dz_ref, dw_ref, dg_ref, carry_ref):
        s = pl.program_id(0)
        u, u1, u2, cv = _conv_taps(z_ref, zp_ref, w_ref, cw, s == nb - 1)
        zb = z_ref[:, :cw]
        y = zb * cv
        r = _rstd(y)
        yh = y * r
        dyn = dy_ref[...]
        dyh = dyn * g_ref[...]
        dyc = r * (dyh - yh * jnp.mean(dyh * yh, axis=-1, keepdims=True))
        dcv = dyc * zb

        def next_row(t):
            return jnp.where(s == 0, 0.0, carry_ref[t:t + 1, :])

        du = w_ref[2:3, :] * dcv + w_ref[1:2, :] * _shift_up(dcv, 1, next_row) + w_ref[0:1, :] * _shift_up(dcv, 2, next_row)
        carry_ref[...] = dcv[0:SUBLANES, :]
        dz_ref[:, :cw] = (dyc * cv).astype(BF16)
        dz_ref[:, cw:2 * cw] = (du * z_ref[:, 2 * cw:]).astype(BF16)
        dz_ref[:, 2 * cw:] = (du * z_ref[:, cw:2 * cw]).astype(BF16)
        tap = lax.broadcasted_iota(I32, (SUBLANES, cw), 0)
        dwp = jnp.where(tap == 0, jnp.sum(dcv * u2, axis=0, keepdims=True),
                        jnp.where(tap == 1, jnp.sum(dcv * u1, axis=0, keepdims=True),
                                  jnp.where(tap == 2, jnp.sum(dcv * u, axis=0, keepdims=True), 0.0)))
        _accumulate(dw_ref, s == 0, dwp)
        _accumulate(dg_ref, s == 0, jnp.sum(dyn * yh, axis=0, keepdims=True))

    rev = lambda s: nb - 1 - s
    return pl.pallas_call(
        body, name=name, grid=(nb,),
        in_specs=[pl.BlockSpec((tm, 3 * cw), lambda s: (rev(s), 0)),
                  pl.BlockSpec((SUBLANES, 3 * cw), lambda s: (jnp.maximum(rev(s) * hb - 1, 0), 0)),
                  pl.BlockSpec((tm, cw), lambda s: (rev(s), 0)), _full(wc), _full(gc)],
        out_specs=[pl.BlockSpec((tm, 3 * cw), lambda s: (rev(s), 0)), pl.BlockSpec((SUBLANES, cw), lambda s: (0, 0)),
                   pl.BlockSpec((1, cw), lambda s: (0, 0))],
        out_shape=[SDS((T, 3 * cw), BF16), SDS((SUBLANES, cw), F32), SDS((1, cw), F32)],
        scratch_shapes=[pltpu.VMEM((SUBLANES, cw), F32)],
        compiler_params=_cp("arbitrary"))(zc, zc, dy, wc, gc)


def _proj_t(h, wat, wft, B, name):
    T, D = h.shape
    R = wat.shape[0]
    nb = T // B

    def body(h_ref, wa_ref, wf_ref, za_ref, zf_ref):
        hv = h_ref[...]
        za_ref[...] = _nt(wa_ref[...], hv).astype(BF16)
        zf_ref[...] = _nt(wf_ref[...], hv)

    return pl.pallas_call(
        body, name=name, grid=(nb,), in_specs=[pl.BlockSpec((B, D), lambda i: (i, 0)), _full(wat), _full(wft)],
        out_specs=[pl.BlockSpec((None, R, B), lambda i: (i, 0, 0)), pl.BlockSpec((SUBLANES, B), lambda i: (0, i))],
        out_shape=[SDS((nb, R, B), BF16), SDS((SUBLANES, T), F32)], compiler_params=_cp("parallel"))(h, wat, wft)


def _split3(v):
    hi = v.astype(BF16).astype(F32)
    r1 = v - hi
    mid = r1.astype(BF16).astype(F32)
    lo = (r1 - mid).astype(BF16).astype(F32)
    return hi, mid, lo


def _tri_dot3(v, tri):
    hi, mid, lo = _split3(v)
    return _nn(hi.astype(BF16), tri) + _nn(mid.astype(BF16), tri) + _nn(lo.astype(BF16), tri)


def _logf_cumsum(zft, b, name, tb=512):
    H, T = zft.shape
    tb = _tile(T, tb, LANES)

    def body(z_ref, b_ref, c_ref, hi_ref, mid_ref, lo_ref, carry_ref):
        @pl.when(pl.program_id(0) == 0)
        def _():
            carry_ref[...] = jnp.zeros_like(carry_ref)
        xv = z_ref[...] + b_ref[...]
        lf = jnp.minimum(xv, 0.0) - jnp.log(1.0 + jnp.exp(-jnp.abs(xv)))
        src = lax.broadcasted_iota(I32, (tb, tb), 0)
        dst = lax.broadcasted_iota(I32, (tb, tb), 1)
        tri = jnp.where(src <= dst, 1.0, 0.0).astype(BF16)
        cs = _tri_dot3(lf, tri) + carry_ref[...]
        c_ref[...] = cs
        hi, mid, lo = _split3(cs)
        hi_ref[...] = hi
        mid_ref[...] = mid
        lo_ref[...] = lo
        col = lax.broadcasted_iota(I32, cs.shape, 1)
        carry_ref[...] = jnp.sum(jnp.where(col == tb - 1, cs, 0.0), axis=1, keepdims=True)

    blk = pl.BlockSpec((H, tb), lambda i: (0, i))
    return pl.pallas_call(
        body, name=name, grid=(T // tb,), in_specs=[blk, _full(b)],
        out_specs=[blk, blk, blk, blk], out_shape=[SDS((H, T), F32)] * 4,
        scratch_shapes=[pltpu.VMEM((H, 1), F32)], compiler_params=_cp("arbitrary"))(zft, b)


def _logf_cumsum_bwd(dcq, dck, zft, b, name, tb=512):
    H, T = zft.shape
    tb = _tile(T, tb, LANES)
    nb = T // tb

    def body(dq_ref, dk_ref, z_ref, b_ref, o_ref, db_ref, carry_ref):
        s = pl.program_id(0)

        @pl.when(s == 0)
        def _():
            carry_ref[...] = jnp.zeros_like(carry_ref)
        dc = dq_ref[...] - dk_ref[...]
        src = lax.broadcasted_iota(I32, (tb, tb), 0)
        dst = lax.broadcasted_iota(I32, (tb, tb), 1)
        tri = jnp.where(src >= dst, 1.0, 0.0).astype(BF16)
        dl = _tri_dot3(dc, tri) + carry_ref[...]
        col = lax.broadcasted_iota(I32, dl.shape, 1)
        carry_ref[...] = jnp.sum(jnp.where(col == 0, dl, 0.0), axis=1, keepdims=True)
        dz = dl * jax.nn.sigmoid(-(z_ref[...] + b_ref[...]))
        o_ref[...] = dz
        _accumulate(db_ref, s == 0, jnp.sum(dz, axis=1, keepdims=True))

    blk = pl.BlockSpec((H, tb), lambda s: (0, nb - 1 - s))
    vec = pl.BlockSpec((H, 1), lambda s: (0, 0))
    return pl.pallas_call(
        body, name=name, grid=(nb,), in_specs=[blk, blk, blk, vec], out_specs=[blk, vec],
        out_shape=[SDS((H, T), F32), SDS((H, 1), F32)],
        scratch_shapes=[pltpu.VMEM((H, 1), F32)], compiler_params=_cp("arbitrary"))(dcq, dck, zft, b)


def _head_norms(zat, name):
    nb, R, B = zat.shape
    aw = R // 3
    g = 2 * aw // HEAD_DIM

    def body(z_ref, o_ref):
        zv = z_ref[...].astype(F32)
        ss = jnp.sum((zv * zv).reshape(g, HEAD_DIM, B), axis=1)
        mx = jnp.broadcast_to(jnp.max(ss, axis=1, keepdims=True), o_ref.shape)

        @pl.when(pl.program_id(0) == 0)
        def _():
            o_ref[...] = mx

        @pl.when(pl.program_id(0) > 0)
        def _():
            o_ref[...] = jnp.maximum(o_ref[...], mx)

    return pl.pallas_call(
        body, name=name, grid=(nb,), in_specs=[pl.BlockSpec((None, 2 * aw, B), lambda i: (i, 0, 0))],
        out_specs=pl.BlockSpec((g, LANES), lambda i: (0, 0)), out_shape=SDS((g, LANES), F32),
        compiler_params=_cp("arbitrary"))(zat)


def _skip_table(cs_col, ce_row, norms, name):
    H, nb, _ = cs_col.shape

    def body(cs_ref, ce_ref, n_ref, jm_ref, im_ref):
        h = pl.program_id(0)
        nq = n_ref[pl.ds(h, 1), 0:1]
        nk = n_ref[pl.ds(H + h, 1), 0:1]
        bound = 2.0 * jnp.sqrt(nq * nk) * 0.125
        skip = jnp.where(bound + cs_ref[...] - ce_ref[...] <= -SKIP_MARGIN, 1, 0).astype(I32)
        jm_ref[...] = jnp.sum(skip, axis=1, keepdims=True)
        im_ref[...] = nb - 1 - jnp.sum(skip, axis=0, keepdims=True)

    col = pl.BlockSpec((None, nb, 1), lambda h: (h, 0, 0))
    row = pl.BlockSpec((None, 1, nb), lambda h: (h, 0, 0))
    return pl.pallas_call(
        body, name=name, grid=(H,), in_specs=[col, row, _full(norms)], out_specs=[col, row],
        out_shape=[SDS((H, nb, 1), I32), SDS((H, 1, nb), I32)], compiler_params=_cp("parallel"))(cs_col, ce_row, norms)


def _causal_mask(B):
    krow = lax.broadcasted_iota(I32, (B, B), 0)
    qcol = lax.broadcasted_iota(I32, (B, B), 1)
    return krow <= qcol


def _attn_fwd(jmin, zat, qa, ka, name):
    nb, R, B = zat.shape
    aw = R // 3
    H = aw // HEAD_DIM
    hd = HEAD_DIM

    def body(jm_ref, q_ref, k_ref, v_ref, qa_ref, ka_ref, o_ref, lse_ref):
        h = pl.program_id(0)
        i = pl.program_id(1)
        qt = jnp.concatenate([q_ref[...] * 0.125, qa_ref[...]], axis=0)

        def block(j, carry, masked):
            m, l, acc = carry
            s = _tn(jnp.concatenate([k_ref[j], ka_ref[j]], axis=0), qt)
            if masked:
                s = jnp.where(_causal_mask(B), s, NEG)
            m_new = jnp.maximum(m, jnp.max(s, axis=0, keepdims=True))
            p = jnp.exp(s - m_new)
            a = jnp.exp(m - m_new)
            l = a * l + jnp.sum(p, axis=0, keepdims=True)
            acc = a * acc + _nn(v_ref[j], p.astype(BF16))
            return m_new, l, acc

        init = (jnp.full((1, B), NEG, F32), jnp.zeros((1, B), F32), jnp.zeros((hd, B), F32))
        carry = lax.fori_loop(jm_ref[h * nb + i], i, lambda j, c: block(j, c, False), init)
        m, l, acc = block(i, carry, True)
        o_ref[...] = acc / l
        lse_ref[...] = m + jnp.log(l)

    grid_spec = pltpu.PrefetchScalarGridSpec(
        num_scalar_prefetch=1, grid=(H, nb),
        in_specs=[pl.BlockSpec((None, hd, B), lambda h, i, jm: (i, h, 0)),
                  pl.BlockSpec((nb, hd, B), lambda h, i, jm: (0, H + h, 0)),
                  pl.BlockSpec((nb, hd, B), lambda h, i, jm: (0, 2 * H + h, 0)),
                  pl.BlockSpec((None, None, AUG, B), lambda h, i, jm: (h, i, 0, 0)),
                  pl.BlockSpec((None, nb, AUG, B), lambda h, i, jm: (h, 0, 0, 0))],
        out_specs=[pl.BlockSpec((None, hd, B), lambda h, i, jm: (i, h, 0)),
                   pl.BlockSpec((None, None, 1, B), lambda h, i, jm: (h, i, 0, 0))])
    return pl.pallas_call(
        body, name=name, grid_spec=grid_spec,
        out_shape=[SDS((nb, aw, B), F32), SDS((H, nb, 1, B), F32)],
        compiler_params=_cp("parallel", "parallel"))(jmin, zat, zat, zat, qa, ka)


def _gnorm_t_fwd(ot, g_col, name):
    nb, aw, B = ot.shape

    def body(o_ref, g_ref, y_ref):
        ov = o_ref[...]
        r = lax.rsqrt(jnp.mean(ov * ov, axis=0, keepdims=True) + EPS)
        y_ref[...] = (ov * r * g_ref[...]).T.astype(BF16)

    return pl.pallas_call(
        body, name=name, grid=(nb,), in_specs=[pl.BlockSpec((None, aw, B), lambda i: (i, 0, 0)), _full(g_col)],
        out_specs=pl.BlockSpec((B, aw), lambda i: (i, 0)), out_shape=SDS((nb * B, aw), BF16),
        compiler_params=_cp("parallel"))(ot, g_col)


def _gnorm_t_bwd(ot, g_col, dy, name):
    nb, aw, B = ot.shape
    H = aw // HEAD_DIM

    def body(o_ref, g_ref, dy_ref, do_ref, dl_ref, dg_ref):
        ov = o_ref[...]
        r = lax.rsqrt(jnp.mean(ov * ov, axis=0, keepdims=True) + EPS)
        yh = ov * r
        dyn = dy_ref[...].T
        dyh = dyn * g_ref[...]
        dov = r * (dyh - yh * jnp.mean(dyh * yh, axis=0, keepdims=True))
        do_ref[...] = dov.astype(BF16)
        dl_ref[...] = jnp.sum((dov * ov).reshape(H, HEAD_DIM, B), axis=1)
        dgp = jnp.broadcast_to(jnp.sum(dyn * yh, axis=1, keepdims=True), dg_ref.shape)
        _accumulate(dg_ref, pl.program_id(0) == 0, dgp)

    blk = pl.BlockSpec((None, aw, B), lambda i: (i, 0, 0))
    return pl.pallas_call(
        body, name=name, grid=(nb,), in_specs=[blk, _full(g_col), pl.BlockSpec((B, aw), lambda i: (i, 1))],
        out_specs=[blk, pl.BlockSpec((None, H, B), lambda i: (i, 0, 0)), pl.BlockSpec((aw, LANES), lambda i: (0, 0))],
        out_shape=[SDS((nb, aw, B), BF16), SDS((nb, H, B), F32), SDS((aw, LANES), F32)],
        compiler_params=_cp("arbitrary"))(ot, g_col, dy)


def _attn_bwd(imax, zat, qa, ka, dot, lse, delta, name):
    nb, R, B = zat.shape
    aw = R // 3
    H = aw // HEAD_DIM
    hd = HEAD_DIM

    def body(im_ref, k_ref, v_ref, ka_ref, q_ref, qa_ref, do_ref, lse_ref, dl_ref,
             dq_ref, dqa_ref, dk_ref, dka_ref, dv_ref):
        h = pl.program_id(0)
        j = pl.program_id(1)

        @pl.when(j == 0)
        def _():
            dq_ref[...] = jnp.zeros_like(dq_ref)
            dqa_ref[...] = jnp.zeros_like(dqa_ref)
        kt = jnp.concatenate([k_ref[...], ka_ref[...]], axis=0)
        vt = v_ref[...]

        def block(i, carry, masked):
            dk, dv = carry
            qt = jnp.concatenate([q_ref[i] * 0.125, qa_ref[i]], axis=0)
            dov = do_ref[i]
            s = _tn(kt, qt)
            if masked:
                s = jnp.where(_causal_mask(B), s, NEG)
            p = jnp.exp(s - lse_ref[i])
            dp = _tn(vt, dov)
            ds = (p * (dp - dl_ref[i, pl.ds(h, 1), :])).astype(BF16)
            dv = dv + _nt(dov, p.astype(BF16))
            dk = dk + _nt(qt, ds)
            r = _nn(kt, ds)
            dq_ref[i] += 0.125 * r[:hd]
            dqa_ref[i] += r[hd:]
            return dk, dv

        carry = block(j, (jnp.zeros((hd + AUG, B), F32), jnp.zeros((hd, B), F32)), True)
        dk, dv = lax.fori_loop(j + 1, im_ref[h * nb + j] + 1, lambda i, c: block(i, c, False), carry)
        dk_ref[...] = dk[:hd]
        dka_ref[...] = dk[hd:]
        dv_ref[...] = dv

    head_rows = lambda off: pl.BlockSpec((nb, hd, B), lambda h, j, im: (0, off + h, 0))
    key_rows = lambda off: pl.BlockSpec((None, hd, B), lambda h, j, im: (j, off + h, 0))
    aug_all = pl.BlockSpec((None, nb, AUG, B), lambda h, j, im: (h, 0, 0, 0))
    aug_one = pl.BlockSpec((None, None, AUG, B), lambda h, j, im: (h, j, 0, 0))
    grid_spec = pltpu.PrefetchScalarGridSpec(
        num_scalar_prefetch=1, grid=(H, nb),
        in_specs=[key_rows(H), key_rows(2 * H), aug_one, head_rows(0), aug_all, head_rows(0),
                  pl.BlockSpec((None, nb, 1, B), lambda h, j, im: (h, 0, 0, 0)),
                  pl.BlockSpec((nb, H, B), lambda h, j, im: (0, 0, 0))],
        out_specs=[head_rows(0), aug_all, key_rows(0), aug_one, key_rows(0)])
    return pl.pallas_call(
        body, name=name, grid_spec=grid_spec,
        out_shape=[SDS((nb, aw, B), F32), SDS((H, nb, AUG, B), F32), SDS((nb, aw, B), F32),
                   SDS((H, nb, AUG, B), F32), SDS((nb, aw, B), F32)],
        compiler_params=_cp("arbitrary", "arbitrary"))(imax, zat, zat, ka, zat, qa, dot, lse, delta)


def _mix_dwt(dq, dk, dv, dzft, h, name):
    nb, aw, B = dq.shape
    D = h.shape[1]

    def body(dq_ref, dk_ref, dv_ref, dzf_ref, h_ref, o_ref, of_ref):
        first = pl.program_id(0) == 0
        hv = h_ref[...]
        part = jnp.concatenate([_nn(r[...].astype(BF16), hv) for r in (dq_ref, dk_ref, dv_ref)], axis=0)
        _accumulate(o_ref, first, part)
        _accumulate(of_ref, first, _nn(dzf_ref[...].astype(BF16), hv))

    blk = pl.BlockSpec((None, aw, B), lambda i: (i, 0, 0))
    return pl.pallas_call(
        body, name=name, grid=(nb,),
        in_specs=[blk, blk, blk, pl.BlockSpec((SUBLANES, B), lambda i: (0, i)), pl.BlockSpec((B, D), lambda i: (i, 0))],
        out_specs=[pl.BlockSpec((3 * aw, D), lambda i: (0, 0)), pl.BlockSpec((SUBLANES, D), lambda i: (0, 0))],
        out_shape=[SDS((3 * aw, D), F32), SDS((SUBLANES, D), F32)],
        compiler_params=_cp("arbitrary"))(dq, dk, dv, dzft, h)


def _mix_bwd_in(dzc, dq, dk, dv, dzft, wc, wat, wft, x, g, dres, name):
    T, D = x.shape
    nb, aw, B = dq.shape

    def body(dzc_ref, dq_ref, dk_ref, dv_ref, dzf_ref, wc_ref, wa_ref, wf_ref, x_ref, g_ref, r_ref,
             dx_ref, dxb_ref, dg_ref):
        dh = _nt(dzc_ref[...], wc_ref[...])
        for s, r in enumerate((dq_ref, dk_ref, dv_ref)):
            dh = dh + _tn(r[...].astype(BF16), wa_ref[s * aw:(s + 1) * aw, :])
        dh = dh + _tn(dzf_ref[...].astype(BF16), wf_ref[...])
        _norm_bwd_store(dh, x_ref, g_ref, r_ref, dx_ref, dxb_ref, dg_ref, pl.program_id(0) == 0)

    blk = pl.BlockSpec((None, aw, B), lambda i: (i, 0, 0))
    row = lambda n: pl.BlockSpec((B, n), lambda i: (i, 0))
    return pl.pallas_call(
        body, name=name, grid=(nb,),
        in_specs=[row(dzc.shape[1]), blk, blk, blk, pl.BlockSpec((SUBLANES, B), lambda i: (0, i)),
                  _full(wc), _full(wat), _full(wft), row(D), _full(g), row(D)],
        out_specs=[row(D), row(D), pl.BlockSpec((1, D), lambda i: (0, 0))],
        out_shape=[SDS((T, D), F32), SDS((T, D), BF16), SDS((1, D), F32)],
        compiler_params=_cp("arbitrary"))(dzc, dq, dk, dv, dzft, wc, wat, wft, x, g, dres)


def _mem_kv(mem, g, wkv, l, name):
    M, D = mem.shape
    dx = wkv.shape[3]

    def body(m_ref, g_ref, w_ref, kv_ref, h_ref):
        mv = m_ref[...]
        h = (mv * _rstd(mv) * g_ref[...]).astype(BF16)
        h_ref[...] = h
        for s in range(N_DEV):
            kv_ref[s] = _nn(h, w_ref[s]).astype(BF16)

    return pl.pallas_call(
        body, name=name, grid=(1,), in_specs=[_full(mem), _full(g), _slots(wkv, l)],
        out_specs=[pl.BlockSpec((N_DEV, M, dx), lambda i: (0, 0, 0)), _full(mem)],
        out_shape=[SDS((N_DEV, M, dx), BF16), SDS((M, D), BF16)], compiler_params=_cp("arbitrary"))(mem, g, wkv)


def _mem_kv_bwd(dkv, hm, wkv, l, mem, g, name):
    M, D = mem.shape
    dx = wkv.shape[3]

    def body(dkv_ref, h_ref, w_ref, m_ref, g_ref, dw_ref, dg_ref):
        hv = h_ref[...]
        dh = jnp.zeros((M, D), F32)
        for s in range(N_DEV):
            d = dkv_ref[s].astype(BF16)
            dw_ref[s] = _tn(hv, d)
            dh = dh + _nt(d, w_ref[s])
        mv = m_ref[...]
        dg_ref[...] = jnp.sum(dh * (mv * _rstd(mv)), axis=0, keepdims=True)

    return pl.pallas_call(
        body, name=name, grid=(1,), in_specs=[_full(dkv), _full(hm), _slots(wkv, l), _full(mem), _full(g)],
        out_specs=[pl.BlockSpec((N_DEV, D, dx), lambda i: (0, 0, 0)), pl.BlockSpec((1, D), lambda i: (0, 0))],
        out_shape=[SDS((N_DEV, D, dx), F32), SDS((1, D), F32)], compiler_params=_cp("arbitrary"))(dkv, hm, wkv, mem, g)


def _xattn_probs(q_ref, kv_ref, h, dx, scale):
    qh = q_ref[:, h * dx:(h + 1) * dx]
    kh = kv_ref[h]
    s = _nt(qh, kh) * scale
    p = jnp.exp(s - jnp.max(s, axis=-1, keepdims=True))
    return qh, kh, p / jnp.sum(p, axis=-1, keepdims=True)


def _xattn_fwd(q, kv, name, tm=512):
    T, D = q.shape
    dx = D // N_XHEADS
    scale = dx ** -0.5
    tm = _tile(T, tm, SUBLANES)

    def body(q_ref, kv_ref, o_ref):
        for h in range(N_XHEADS):
            _, _, p = _xattn_probs(q_ref, kv_ref, h, dx, scale)
            o_ref[:, h * dx:(h + 1) * dx] = _nn(p.astype(BF16), kv_ref[N_XHEADS + h]).astype(BF16)

    row = pl.BlockSpec((tm, D), lambda i: (i, 0))
    return pl.pallas_call(body, name=name, grid=(T // tm,), in_specs=[row, _full(kv)], out_specs=row,
                          out_shape=SDS((T, D), BF16), compiler_params=_cp("parallel"))(q, kv)


def _xattn_bwd(q, kv, do, name, tm=512):
    T, D = q.shape
    dx = D // N_XHEADS
    scale = dx ** -0.5
    tm = _tile(T, tm, SUBLANES)

    def body(q_ref, kv_ref, do_ref, dq_ref, dkv_ref):
        @pl.when(pl.program_id(0) == 0)
        def _():
            dkv_ref[...] = jnp.zeros_like(dkv_ref)
        for h in range(N_XHEADS):
            qh, kh, p = _xattn_probs(q_ref, kv_ref, h, dx, scale)
            doh = do_ref[:, h * dx:(h + 1) * dx]
            dp = _nt(doh, kv_ref[N_XHEADS + h])
            ds = (p * (dp - jnp.sum(p * dp, axis=-1, keepdims=True)) * scale).astype(BF16)
            dq_ref[:, h * dx:(h + 1) * dx] = _nn(ds, kh).astype(BF16)
            dkv_ref[h] += _tn(ds, qh)
            dkv_ref[N_XHEADS + h] += _tn(p.astype(BF16), doh)

    row = pl.BlockSpec((tm, D), lambda i: (i, 0))
    return pl.pallas_call(body, name=name, grid=(T // tm,), in_specs=[row, _full(kv), row],
                          out_specs=[row, _full(kv)], out_shape=[SDS((T, D), BF16), SDS(kv.shape, F32)],
                          compiler_params=_cp("arbitrary"))(q, kv, do)


def _mesh_position():
    return lax.axis_index("x"), lax.axis_index("y"), lax.axis_index("c")


def _all_gather(shards, name):
    n = len(shards)

    def body(*refs):
        x_refs, out_refs = refs[:n], refs[n:2 * n]
        send_sems, recv_sems, local_sems = refs[2 * n:]
        x, y, c = _mesh_position()
        me, sibling = (x, y, c), (x, y, 1 - c)
        chips = [(1 - x, y), (x, 1 - y), (1 - x, 1 - y)]

        def slot(w, px, py, pc):
            return out_refs[w].at[4 * px + 2 * py + pc]

        def copy(w, k, block, to, src=None):
            return pltpu.make_async_remote_copy(
                src_ref=slot(w, *block) if src is None else src, dst_ref=slot(w, *block),
                send_sem=send_sems.at[w, k], recv_sem=recv_sems.at[w, k], device_id=to,
                device_id_type=pl.DeviceIdType.MESH)

        mine = [pltpu.make_async_copy(x_refs[w], slot(w, *me), local_sems.at[w]) for w in range(n)]
        first = []
        for w in range(n):
            mine[w].start()
            first.append(copy(w, 0, me, sibling, src=x_refs[w]))
            first += [copy(w, 1 + j, me, (*chip, c), src=x_refs[w]) for j, chip in enumerate(chips)]
        for cp in first:
            cp.start()
        passed = []
        for w in range(n):
            for j, chip in enumerate(chips):
                copy(w, 1 + j, (*chip, c), me).wait_recv()
                fwd = copy(w, 4 + j, (*chip, c), sibling)
                fwd.start()
                passed.append(fwd)
        for w in range(n):
            copy(w, 0, sibling, me).wait_recv()
            for j, chip in enumerate(chips):
                copy(w, 4 + j, (*chip, 1 - c), me).wait_recv()
        for cp in first + passed:
            cp.wait_send()
        for w in range(n):
            mine[w].wait()

    any_spec = pl.BlockSpec(memory_space=pl.ANY)
    return pl.pallas_call(
        body, name=name, out_shape=[SDS((N_DEV,) + s.shape, s.dtype) for s in shards],
        in_specs=[any_spec] * n, out_specs=[any_spec] * n,
        scratch_shapes=[pltpu.SemaphoreType.DMA((n, 7)), pltpu.SemaphoreType.DMA((n, 7)), pltpu.SemaphoreType.DMA((n,))],
    )(*shards)


def _all_to_all(grads, name):
    n = len(grads)
    L = len(grads[0])
    flat = [g for gl in grads for g in gl]

    def body(*refs):
        x_refs, out_refs = refs[:n * L], refs[n * L:n * L + n]
        send_sems, recv_sems, local_sems = refs[n * L + n:]
        x, y, c = _mesh_position()
        me = 4 * x + 2 * y + c
        copies = []
        mine = []
        for w in range(n):
            for l in range(L):
                src = x_refs[w * L + l]
                own = pltpu.make_async_copy(src.at[me], out_refs[w].at[me, l], local_sems.at[w * L + l])
                own.start()
                mine.append(own)
                for k in range(1, N_DEV):
                    px = 1 - x if k & 4 else x
                    py = 1 - y if k & 2 else y
                    pc = 1 - c if k & 1 else c
                    peer = 4 * px + 2 * py + pc
                    copies.append(pltpu.make_async_remote_copy(
                        src_ref=src.at[peer], dst_ref=out_refs[w].at[me, l], send_sem=send_sems.at[w * L + l, k - 1],
                        recv_sem=recv_sems.at[w * L + l, k - 1], device_id=(px, py, pc),
                        device_id_type=pl.DeviceIdType.MESH))
        for cp in copies:
            cp.start()
        for cp in copies:
            cp.wait_recv()
        for cp in copies:
            cp.wait_send()
        for own in mine:
            own.wait()

    any_spec = pl.BlockSpec(memory_space=pl.ANY)
    return pl.pallas_call(
        body, name=name,
        out_shape=[SDS((N_DEV, L) + gl[0].shape[1:], gl[0].dtype) for gl in grads],
        in_specs=[any_spec] * (n * L), out_specs=[any_spec] * n,
        scratch_shapes=[pltpu.SemaphoreType.DMA((n * L, 7)), pltpu.SemaphoreType.DMA((n * L, 7)),
                        pltpu.SemaphoreType.DMA((n * L,))],
    )(*flat)


def _sum_adamw(parts, w, m, v, name, tr=256):
    _, R, C = parts.shape
    tr = _tile(R, tr, SUBLANES)

    def body(p_ref, w_ref, m_ref, v_ref, g_ref, d_ref, mo_ref, vo_ref):
        g = p_ref[0]
        for s in range(1, N_DEV):
            g = g + p_ref[s]
        mn = ADAM_B1 * m_ref[...] + (1.0 - ADAM_B1) * g
        vn = ADAM_B2 * v_ref[...] + (1.0 - ADAM_B2) * jnp.square(g)
        m_hat = mn / (1.0 - ADAM_B1 ** ADAM_STEP)
        v_hat = vn / (1.0 - ADAM_B2 ** ADAM_STEP)
        g_ref[...] = g
        d_ref[...] = -ADAM_LR * (m_hat / (jnp.sqrt(v_hat) + ADAM_EPS) + ADAM_WD * w_ref[...])
        mo_ref[...] = mn
        vo_ref[...] = vn

    row = pl.BlockSpec((tr, C), lambda i: (i, 0))
    return pl.pallas_call(
        body, name=name, grid=(R // tr,),
        in_specs=[pl.BlockSpec((N_DEV, tr, C), lambda i: (0, i, 0)), row, row, row], out_specs=[row] * 4,
        out_shape=[SDS((R, C), F32)] * 4, compiler_params=_cp("parallel"))(parts, w, m, v)


def _pack(flat_list, row_mult):
    flat = jnp.concatenate(flat_list, axis=-1)
    n = flat.shape[-1]
    chunk = row_mult * PACK_COLS
    pad = -n % chunk
    flat = jnp.pad(flat, [(0, 0)] * (flat.ndim - 1) + [(0, pad)])
    return flat.reshape(flat.shape[:-1] + ((n + pad) // PACK_COLS, PACK_COLS))


def _unpack(packed, shapes):
    lead = packed.shape[:-2]
    flat = packed.reshape(lead + (-1,))
    out = []
    off = 0
    for shp in shapes:
        n = 1
        for d in shp:
            n *= d
        out.append(flat[..., off:off + n].reshape(lead + tuple(shp)))
        off += n
    return out


def _ffn_fwd(x, g, wgu, wd, l, tag):
    gu, act, h = _ffn_gu(x, g, wgu, l, f"{tag}_gu")
    xo = _ffn_down(act, wd, l, x, f"{tag}_down")
    return xo, (x, g, h, gu, act)


def _ffn_bwd(dx, dxb, saved, wgu, wd, l, tag):
    x, g, h, gu, act = saved
    dgu = _ffn_bwd_act(dxb, wd, l, gu, f"{tag}_bwd_act")
    dwd = _ffn_dwd(act, dxb, f"{tag}_dwd")
    dwgu = _ffn_dwgu(h, dgu, f"{tag}_dwgu")
    dx, dxb, dg = _ffn_bwd_in(dgu, wgu, l, x, g, dx, f"{tag}_bwd_in")
    return dx, dxb, dg, dwgu, dwd.reshape(N_DEV, -1, dwd.shape[-1])


def _aug_rows(rows, H, nb, B):
    a = jnp.stack(rows + [jnp.zeros_like(rows[0])] * (AUG - len(rows)), axis=1)
    return a.reshape(H, AUG, nb, B).transpose(0, 2, 1, 3).astype(BF16)


def _mix_fwd(x, g, w_in, w_conv8, b_f, g_conv, g_att, w_out, l, tag):
    T, D = x.shape
    cw = D // 2
    aw = D - cw
    H = aw // HEAD_DIM
    B = _tile(T, ATT_BLOCK, LANES)
    nb = T // B
    w_c = w_in[:, :3 * cw]
    w_at = w_in[:, 3 * cw:3 * cw + 3 * aw].T
    w_ft = jnp.pad(w_in[:, 3 * cw + 3 * aw:].T, ((0, SUBLANES - H), (0, 0)))
    zc, h = _norm_mm(x, g, w_c, None, F32, f"{tag}_in_conv")
    zat, zft8 = _proj_t(h, w_at, w_ft, B, f"{tag}_in_att")
    zft = zft8[:H]
    bcol = b_f.reshape(H, 1)
    c, chi, cmid, clo = _logf_cumsum(zft, bcol, f"{tag}_cumsum")
    one = jnp.ones_like(c)
    qa = _aug_rows([one, one, one, chi, cmid, clo], H, nb, B)
    ka = _aug_rows([-chi, -cmid, -clo, one, one, one], H, nb, B)
    cb = c.reshape(H, nb, B)
    jmin, imax = _skip_table(cb[:, :, 0:1], cb[:, :, B - 1].reshape(H, 1, nb), _head_norms(zat, f"{tag}_norms"),
                             f"{tag}_skip")
    ot, lse = _attn_fwd(jmin.reshape(H * nb), zat, qa, ka, f"{tag}_attn")
    g_col = g_att.reshape(aw, 1)
    yna = _gnorm_t_fwd(ot, g_col, f"{tag}_gnorm")
    ync = _conv_fwd(zc, w_conv8, g_conv, f"{tag}_conv")
    xo = _mm_res([ync, yna], w_out, l, x, f"{tag}_out")
    saved = (x, g, h, zc, zft, bcol, zat, qa, ka, imax.reshape(H * nb), ot, lse, ync, yna, w_c, w_at, w_ft, g_col)
    return xo, saved


def _mix_bwd(dx, dxb, saved, w_conv8, g_conv, w_out, l, tag):
    x, g, h, zc, zft, bcol, zat, qa, ka, imax, ot, lse, ync, yna, w_c, w_at, w_ft, g_col = saved
    T, D = x.shape
    H = zft.shape[0]
    dw_out = jnp.concatenate([_mm_tn(ync, dxb, f"{tag}_dwout_c"), _mm_tn(yna, dxb, f"{tag}_dwout_a")], axis=0)
    dy = _mm_nt(dxb, w_out, l, F32, f"{tag}_bwd_out")
    dzc, dwc8, dg_conv = _conv_bwd(zc, dy, w_conv8, g_conv, f"{tag}_conv_bwd")
    dot, delta, dg_att = _gnorm_t_bwd(ot, g_col, dy, f"{tag}_gnorm_bwd")
    dq, dqa, dk, dka, dv = _attn_bwd(imax, zat, qa, ka, dot, lse, delta, f"{tag}_attn_bwd")
    dcq = dqa[:, :, 3, :].reshape(H, T)
    dck = dka[:, :, 0, :].reshape(H, T)
    dzft, db = _logf_cumsum_bwd(dcq, dck, zft, bcol, f"{tag}_cumsum_bwd")
    dzft8 = jnp.pad(dzft, ((0, SUBLANES - H), (0, 0)))
    dw_c = _mm_tn(h, dzc, f"{tag}_dwin_c")
    dw_at, dw_ft = _mix_dwt(dq, dk, dv, dzft8, h, f"{tag}_dwin_a")
    dw_in = jnp.concatenate([dw_c, dw_at.T, dw_ft[:H].T], axis=1)
    dx, dxb, dg = _mix_bwd_in(dzc, dq, dk, dv, dzft8, w_c, w_at, w_ft, x, g, dx, f"{tag}_bwd_in")
    return dx, dxb, dg, dw_in, dwc8[:3], db.reshape(H), dg_conv, dg_att[:, 0], dw_out


def _xattn_block_fwd(x, g, mem, g_mem, w_q, w_kv, w_o, l, tag):
    q, h = _norm_mm(x, g, w_q, l, BF16, f"{tag}_q")
    kv, hm = _mem_kv(mem, g_mem, w_kv, l, f"{tag}_kv")
    o = _xattn_fwd(q, kv, f"{tag}_attn")
    xo = _mm_res([o], w_o, l, x, f"{tag}_o")
    return xo, (x, g, h, mem, g_mem, hm, q, kv, o)


def _xattn_block_bwd(dx, dxb, saved, w_q, w_kv, w_o, l, tag):
    x, g, h, mem, g_mem, hm, q, kv, o = saved
    dw_o = _mm_tn(o, dxb, f"{tag}_dwo")
    do = _mm_nt(dxb, w_o, l, BF16, f"{tag}_bwd_o")
    dq, dkv = _xattn_bwd(q, kv, do, f"{tag}_attn_bwd")
    dw_q = _mm_tn(h, dq, f"{tag}_dwq")
    dw_kv, dg_mem = _mem_kv_bwd(dkv, hm, w_kv, l, mem, g_mem, f"{tag}_kv_bwd")
    dx, dxb, dg = _mm_nt_normbwd(dq, w_q, l, x, g, dx, f"{tag}_bwd_q")
    return dx, dxb, dg, dg_mem, dw_q, dw_kv, dw_o


def _local_step(x, mem, tgt, W):
    L = W['g_ffn1'].shape[0]
    saved = []
    for l in range(L):
        row = lambda n: W[n][l][None, :]
        wc8 = jnp.pad(W['w_conv'][l], ((0, SUBLANES - 3), (0, 0)))
        x, s1 = _ffn_fwd(x, row('g_ffn1'), W['w_ffn1_gu'], W['w_ffn1_down'], l, f"l{l}_ffn1")
        x, s2 = _mix_fwd(x, row('g_mix'), W['w_mix_in'][l], wc8, W['b_f'][l], row('g_conv_out'), W['g_att_out'][l],
                         W['w_mix_out'], l, f"l{l}_mix")
        x, s3 = _xattn_block_fwd(x, row('g_xattn'), mem, row('g_mem'), W['w_xq'], W['w_xkv'], W['w_xo'], l,
                                 f"l{l}_xattn")
        x, s4 = _ffn_fwd(x, row('g_ffn2'), W['w_ffn2_gu'], W['w_ffn2_down'], l, f"l{l}_ffn2")
        saved.append((s1, s2, s3, s4, wc8))
    loss, dx, dxb, dg_final = _loss_head(x, W['g_final'][None, :], tgt, "loss_head")
    G = {n: [None] * L for n in WEIGHT_NAMES if n != 'g_final'}
    for l in reversed(range(L)):
        row = lambda n: W[n][l][None, :]
        s1, s2, s3, s4, wc8 = saved[l]
        dx, dxb, G['g_ffn2'][l], G['w_ffn2_gu'][l], G['w_ffn2_down'][l] = _ffn_bwd(
            dx, dxb, s4, W['w_ffn2_gu'], W['w_ffn2_down'], l, f"l{l}_ffn2")
        dx, dxb, G['g_xattn'][l], G['g_mem'][l], dwq, G['w_xkv'][l], dwo = _xattn_block_bwd(
            dx, dxb, s3, W['w_xq'], W['w_xkv'], W['w_xo'], l, f"l{l}_xattn")
        (dx, dxb, G['g_mix'][l], dwin, dwconv, G['b_f'][l], G['g_conv_out'][l], G['g_att_out'][l], dwout) = _mix_bwd(
            dx, dxb, s2, wc8, row('g_conv_out'), W['w_mix_out'], l, f"l{l}_mix")
        dx, dxb, G['g_ffn1'][l], G['w_ffn1_gu'][l], G['w_ffn1_down'][l] = _ffn_bwd(
            dx, dxb, s1, W['w_ffn1_gu'], W['w_ffn1_down'], l, f"l{l}_ffn1")
        D = dwq.shape[0]
        G['w_xq'][l] = dwq.reshape(N_DEV, D // N_DEV, -1)
        G['w_xo'][l] = dwo.reshape(N_DEV, D // N_DEV, -1)
        G['w_mix_out'][l] = dwout.reshape(N_DEV, D // N_DEV, -1)
        G['w_mix_in'][l] = dwin.reshape(D, N_DEV, -1).transpose(1, 0, 2)
        G['w_conv'][l] = dwconv.reshape(3, N_DEV, -1).transpose(1, 0, 2)
    grads = {n: (G[n] if n in SHARDED else jnp.stack([a.reshape(W[n].shape[1:]) for a in G[n]])) for n in G}
    grads['g_final'] = dg_final.reshape(-1)
    return loss, dx, grads


def kernel(x, mem, g_ffn1, w_ffn1_gu, w_ffn1_down, g_mix, w_mix_in, w_conv, b_f, g_conv_out, g_att_out, w_mix_out, g_xattn, g_mem, w_xq, w_xkv, w_xo, g_ffn2, w_ffn2_gu, w_ffn2_down, g_final, loss_target, m_g_ffn1, m_w_ffn1_gu, m_w_ffn1_down, m_g_mix, m_w_mix_in, m_w_conv, m_b_f, m_g_conv_out, m_g_att_out, m_w_mix_out, m_g_xattn, m_g_mem, m_w_xq, m_w_xkv, m_w_xo, m_g_ffn2, m_w_ffn2_gu, m_w_ffn2_down, m_g_final, v_g_ffn1, v_w_ffn1_gu, v_w_ffn1_down, v_g_mix, v_w_mix_in, v_w_conv, v_b_f, v_g_conv_out, v_g_att_out, v_w_mix_out, v_g_xattn, v_g_mem, v_w_xq, v_w_xkv, v_w_xo, v_g_ffn2, v_w_ffn2_gu, v_w_ffn2_down, v_g_final):
    args = (x, mem, g_ffn1, w_ffn1_gu, w_ffn1_down, g_mix, w_mix_in, w_conv, b_f, g_conv_out, g_att_out, w_mix_out,
            g_xattn, g_mem, w_xq, w_xkv, w_xo, g_ffn2, w_ffn2_gu, w_ffn2_down, g_final)
    P = dict(zip(IN_NAMES, args))
    moms = (m_g_ffn1, m_w_ffn1_gu, m_w_ffn1_down, m_g_mix, m_w_mix_in, m_w_conv, m_b_f, m_g_conv_out, m_g_att_out,
            m_w_mix_out, m_g_xattn, m_g_mem, m_w_xq, m_w_xkv, m_w_xo, m_g_ffn2, m_w_ffn2_gu, m_w_ffn2_down, m_g_final)
    vars_ = (v_g_ffn1, v_w_ffn1_gu, v_w_ffn1_down, v_g_mix, v_w_mix_in, v_w_conv, v_b_f, v_g_conv_out, v_g_att_out,
             v_w_mix_out, v_g_xattn, v_g_mem, v_w_xq, v_w_xkv, v_w_xo, v_g_ffn2, v_w_ffn2_gu, v_w_ffn2_down, v_g_final)
    MOM = dict(zip(WEIGHT_NAMES, moms))
    VAR = dict(zip(WEIGHT_NAMES, vars_))

    gathered = _all_gather([P[n] if n == 'w_conv' else P[n].astype(BF16) for n in SHARDED], "gather_weights")
    W = dict(zip(SHARDED, gathered))
    for n in ('w_mix_in', 'w_conv'):
        _, L, a, b = W[n].shape
        W[n] = W[n].transpose(1, 2, 0, 3).reshape(L, a, N_DEV * b)
    for n in REPLICATED:
        W[n] = P[n]

    loss_part, dx, grads = _local_step(x[0], mem[0], loss_target[0], W)

    parts = dict(zip(SHARDED, _all_to_all([grads[n] for n in SHARDED], "scatter_grads")))
    small_shapes = [P[n].shape for n in REPLICATED] + [(1,)]
    small = _pack([grads[n].reshape(-1) for n in REPLICATED] + [loss_part[0, :1]], SUBLANES)
    small_parts = _all_gather([small], "gather_small_grads")[0]

    res = {}
    for n in SHARDED:
        shp = P[n].shape
        two_d = lambda a: a.reshape(-1, shp[-1])
        outs = _sum_adamw(parts[n].reshape(N_DEV, -1, shp[-1]), two_d(P[n]), two_d(MOM[n]), two_d(VAR[n]), "adamw_" + n)
        res[n] = [o.reshape(shp) for o in outs]
    zero = [jnp.zeros((1,), F32)]
    w_, m_, v_ = (_pack([d[n].reshape(-1) for n in REPLICATED] + zero, SUBLANES) for d in (P, MOM, VAR))
    small_out = [_unpack(o, small_shapes) for o in _sum_adamw(small_parts, w_, m_, v_, "adamw_vectors")]
    for k, n in enumerate(REPLICATED):
        res[n] = [o[k] for o in small_out]
    out = []
    for k in range(4):
        out += [res[n][k] for n in WEIGHT_NAMES]
    return (small_out[0][-1].reshape(()), dx[None], *out)
```

```python
import jax
import jax.numpy as jnp
from jax import lax
from jax.experimental import pallas as pl
from jax.experimental.pallas import tpu as pltpu

F32 = jnp.float32
BF16 = jnp.bfloat16
I32 = jnp.int32
SDS = jax.ShapeDtypeStruct

EPS = 1e-6
HEAD_DIM = 64
N_XHEADS = 4
N_DEV = 8
LANES = 128
SUBLANES = 8
AUG = 16
ATT_BLOCK = 512
NEG = -1e30
SKIP_MARGIN = 115.0
PACK_COLS = 1024
VMEM_LIMIT = 58 * 1024 * 1024

ADAM_LR = 0.001
ADAM_B1 = 0.9
ADAM_B2 = 0.999
ADAM_EPS = 1e-08
ADAM_WD = 0.01
ADAM_STEP = 10

IN_NAMES = ['x', 'mem', 'g_ffn1', 'w_ffn1_gu', 'w_ffn1_down', 'g_mix', 'w_mix_in', 'w_conv', 'b_f', 'g_conv_out',
            'g_att_out', 'w_mix_out', 'g_xattn', 'g_mem', 'w_xq', 'w_xkv', 'w_xo', 'g_ffn2', 'w_ffn2_gu',
            'w_ffn2_down', 'g_final']
WEIGHT_NAMES = IN_NAMES[2:]
SHARDED = ['w_ffn1_gu', 'w_ffn1_down', 'w_mix_in', 'w_conv', 'w_mix_out', 'w_xq', 'w_xkv', 'w_xo', 'w_ffn2_gu',
           'w_ffn2_down']
REPLICATED = [n for n in WEIGHT_NAMES if n not in SHARDED]


def _tile(n, pref, mult):
    t = min(pref, n) // mult * mult
    while t >= mult:
        if n % t == 0:
            return t
        t -= mult
    return n


def _cp(*sem):
    return pltpu.CompilerParams(dimension_semantics=sem, vmem_limit_bytes=VMEM_LIMIT)


def _nt(a, b):
    return lax.dot_general(a, b, (((1,), (1,)), ((), ())), preferred_element_type=F32)


def _tn(a, b):
    return lax.dot_general(a, b, (((0,), (0,)), ((), ())), preferred_element_type=F32)


def _nn(a, b):
    return jnp.dot(a, b, preferred_element_type=F32)


def _rstd(xv):
    return lax.rsqrt(jnp.mean(xv * xv, axis=-1, keepdims=True) + EPS)


def _full(a):
    nd = a.ndim
    return pl.BlockSpec(a.shape, lambda *_: (0,) * nd)


def _slots(w, l):
    return pl.BlockSpec((N_DEV, None) + w.shape[2:], lambda *_: (0, l, 0, 0))


def _rows2d(w_ref):
    s, a, b = w_ref.shape
    return w_ref[...].reshape(s * a, b)


def _accumulate(ref, first, part):
    @pl.when(first)
    def _():
        ref[...] = part

    @pl.when(jnp.logical_not(first))
    def _():
        ref[...] += part


def _norm_bwd_store(dh, x_ref, g_ref, r_ref, dx_ref, dxb_ref, dg_ref, first):
    xv = x_ref[...]
    r = _rstd(xv)
    xh = xv * r
    dxh = dh * g_ref[...]
    dx = r * (dxh - xh * jnp.mean(dxh * xh, axis=-1, keepdims=True)) + r_ref[...]
    dx_ref[...] = dx
    dxb_ref[...] = dx.astype(BF16)
    _accumulate(dg_ref, first, jnp.sum(dh * xh, axis=0, keepdims=True))


def _norm_mm(x, g, w, l, out_dtype, name, tm=512):
    T, D = x.shape
    N = w.shape[-1]
    tm = _tile(T, tm, SUBLANES)

    def body(x_ref, g_ref, w_ref, o_ref, h_ref):
        xv = x_ref[...]
        h = (xv * _rstd(xv) * g_ref[...]).astype(BF16)
        h_ref[...] = h
        o_ref[...] = _nn(h, w_ref[...] if l is None else _rows2d(w_ref)).astype(o_ref.dtype)

    row = lambda n: pl.BlockSpec((tm, n), lambda i: (i, 0))
    return pl.pallas_call(
        body, name=name, grid=(T // tm,),
        in_specs=[row(D), _full(g), _full(w) if l is None else _slots(w, l)], out_specs=[row(N), row(D)],
        out_shape=[SDS((T, N), out_dtype), SDS((T, D), BF16)], compiler_params=_cp("parallel"))(x, g, w)


def _mm_res(a_list, w, l, res, name, tm=512):
    T = res.shape[0]
    N = w.shape[-1]
    tm = _tile(T, tm, SUBLANES)
    n_a = len(a_list)

    def body(*refs):
        a_refs = refs[:n_a]
        w_ref, r_ref, o_ref = refs[n_a:]
        wv = _rows2d(w_ref)
        acc = r_ref[...]
        off = 0
        for a_ref in a_refs:
            k = a_ref.shape[1]
            acc = acc + _nn(a_ref[...], wv[off:off + k, :])
            off += k
        o_ref[...] = acc

    row = lambda n: pl.BlockSpec((tm, n), lambda i: (i, 0))
    return pl.pallas_call(
        body, name=name, grid=(T // tm,),
        in_specs=[row(a.shape[1]) for a in a_list] + [_slots(w, l), row(N)], out_specs=row(N),
        out_shape=SDS((T, N), F32), compiler_params=_cp("parallel"))(*a_list, w, res)


def _mm_nt(a, w, l, out_dtype, name, tm=512):
    T = a.shape[0]
    K = N_DEV * w.shape[2]
    tm = _tile(T, tm, SUBLANES)

    def body(a_ref, w_ref, o_ref):
        o_ref[...] = _nt(a_ref[...], _rows2d(w_ref)).astype(o_ref.dtype)

    return pl.pallas_call(
        body, name=name, grid=(T // tm,),
        in_specs=[pl.BlockSpec((tm, a.shape[1]), lambda i: (i, 0)), _slots(w, l)],
        out_specs=pl.BlockSpec((tm, K), lambda i: (i, 0)),
        out_shape=SDS((T, K), out_dtype), compiler_params=_cp("parallel"))(a, w)


def _mm_tn(a, b, name, out_dtype=BF16, tt=2048):
    T, K = a.shape
    N = b.shape[1]
    tt = _tile(T, tt, 2 * SUBLANES)
    nt = T // tt

    def body(a_ref, b_ref, o_ref, acc_ref):
        t = pl.program_id(0)
        _accumulate(acc_ref, t == 0, _tn(a_ref[...], b_ref[...]))

        @pl.when(t == nt - 1)
        def _():
            o_ref[...] = acc_ref[...].astype(o_ref.dtype)

    return pl.pallas_call(
        body, name=name, grid=(nt,),
        in_specs=[pl.BlockSpec((tt, K), lambda t: (t, 0)), pl.BlockSpec((tt, N), lambda t: (t, 0))],
        out_specs=pl.BlockSpec((K, N), lambda t: (0, 0)), out_shape=SDS((K, N), out_dtype),
        scratch_shapes=[pltpu.VMEM((K, N), F32)], compiler_params=_cp("arbitrary"))(a, b)


def _hosted_call(body, name, n_steps, in_specs, out_specs, out_shape, args, scatter, scratch_shapes=()):
    n = len(scatter)
    n_in, n_out, n_scr = len(in_specs), len(out_specs), len(scratch_shapes)

    def wrapped(*refs):
        ins, xs = refs[:n_in], refs[n_in:n_in + n]
        outs, recvs = refs[n_in + n:n_in + n + n_out], refs[n_in + n + n_out:n_in + 2 * n + n_out]
        scr, sems = refs[n_in + 2 * n + n_out:n_in + 2 * n + n_out + n_scr], refs[n_in + 2 * n + n_out + n_scr:]
        step = pl.program_id(0)

        @pl.when(step == 0)
        def _():
            _scatter_start(xs, recvs, *sems)
        body(*ins, *outs, *scr)

        @pl.when(step == n_steps - 1)
        def _():
            _scatter_wait(xs, recvs, *sems)

    any_spec = pl.BlockSpec(memory_space=pl.ANY)
    sems = [pltpu.SemaphoreType.DMA((n, N_DEV - 1)), pltpu.SemaphoreType.DMA((n, N_DEV - 1)),
            pltpu.SemaphoreType.DMA((n,))]
    return pl.pallas_call(
        wrapped, name=name, grid=(n_steps,), in_specs=list(in_specs) + [any_spec] * n,
        out_specs=list(out_specs) + [any_spec] * n,
        out_shape=list(out_shape) + [SDS(b.shape, b.dtype) for b in scatter],
        scratch_shapes=list(scratch_shapes) + sems, compiler_params=_cp("arbitrary"))(*args, *scatter)


def _mm_nt_normbwd(a, w, l, x, g, dres, name, scatter, tm=512):
    T, D = x.shape
    tm = _tile(T, tm, SUBLANES)

    def body(a_ref, w_ref, x_ref, g_ref, r_ref, dx_ref, dxb_ref, dg_ref):
        dh = _nt(a_ref[...], _rows2d(w_ref))
        _norm_bwd_store(dh, x_ref, g_ref, r_ref, dx_ref, dxb_ref, dg_ref, pl.program_id(0) == 0)

    row = lambda n: pl.BlockSpec((tm, n), lambda i: (i, 0))
    return _hosted_call(
        body, name, T // tm, [row(a.shape[1]), _slots(w, l), row(D), _full(g), row(D)],
        [row(D), row(D), pl.BlockSpec((1, D), lambda i: (0, 0))],
        [SDS((T, D), F32), SDS((T, D), BF16), SDS((1, D), F32)], (a, w, x, g, dres), scatter)


def _ffn_gu(x, g, wgu, l, name, tm=512):
    T, D = x.shape
    fb = wgu.shape[3]
    nh = N_DEV // 2
    tm = _tile(T, tm, SUBLANES)

    def body(x_ref, g_ref, w_ref, gu_ref, act_ref, h_ref):
        xv = x_ref[...]
        h = (xv * _rstd(xv) * g_ref[...]).astype(BF16)
        h_ref[...] = h
        for d in range(nh):
            gt = _nn(h, w_ref[d])
            up = _nn(h, w_ref[d + nh])
            gu_ref[d] = gt.astype(BF16)
            gu_ref[d + nh] = up.astype(BF16)
            act_ref[d] = (gt * jax.nn.sigmoid(gt) * up).astype(BF16)

    row = pl.BlockSpec((tm, D), lambda i: (i, 0))
    blocks = lambda n: pl.BlockSpec((n, tm, fb), lambda i: (0, i, 0))
    return pl.pallas_call(
        body, name=name, grid=(T // tm,), in_specs=[row, _full(g), _slots(wgu, l)],
        out_specs=[blocks(N_DEV), blocks(nh), row],
        out_shape=[SDS((N_DEV, T, fb), BF16), SDS((nh, T, fb), BF16), SDS((T, D), BF16)],
        compiler_params=_cp("parallel"))(x, g, wgu)


def _ffn_down(act, wd, l, res, name, tm=1024):
    nh, T, fb = act.shape
    D = wd.shape[3]
    tm = _tile(T, tm, SUBLANES)

    def body(a_ref, w_ref, r_ref, o_ref):
        wv = w_ref[...].reshape(nh, fb, D)
        acc = _nn(a_ref[0], wv[0])
        for d in range(1, nh):
            acc = acc + _nn(a_ref[d], wv[d])
        o_ref[...] = r_ref[...] + 0.5 * acc

    row = pl.BlockSpec((tm, D), lambda i: (i, 0))
    return pl.pallas_call(
        body, name=name, grid=(T // tm,),
        in_specs=[pl.BlockSpec((nh, tm, fb), lambda i: (0, i, 0)), _slots(wd, l), row], out_specs=row,
        out_shape=SDS((T, D), F32), compiler_params=_cp("parallel"))(act, wd, res)


def _ffn_bwd_act(dyb, wd, l, gu, name, tm=512):
    T, D = dyb.shape
    _, _, fb = gu.shape
    nh = N_DEV // 2
    tm = _tile(T, tm, SUBLANES)

    def body(dy_ref, w_ref, gu_ref, o_ref):
        wv = w_ref[...].reshape(nh, fb, D)
        dy = dy_ref[...]
        for d in range(nh):
            da = 0.5 * _nt(dy, wv[d])
            gt = gu_ref[d].astype(F32)
            up = gu_ref[d + nh].astype(F32)
            sg = jax.nn.sigmoid(gt)
            o_ref[d] = (da * up * (sg * (1.0 + gt * (1.0 - sg)))).astype(BF16)
            o_ref[d + nh] = (da * (gt * sg)).astype(BF16)

    blocks = pl.BlockSpec((N_DEV, tm, fb), lambda i: (0, i, 0))
    return pl.pallas_call(
        body, name=name, grid=(T // tm,),
        in_specs=[pl.BlockSpec((tm, D), lambda i: (i, 0)), _slots(wd, l), blocks], out_specs=blocks,
        out_shape=SDS((N_DEV, T, fb), BF16), compiler_params=_cp("parallel"))(dyb, wd, gu)


def _ffn_dwd(act, dyb, name, tt=2048):
    nh, T, fb = act.shape
    D = dyb.shape[1]
    tt = _tile(T, tt, 2 * SUBLANES)
    nt = T // tt

    def body(a_ref, b_ref, o_ref, acc_ref):
        t = pl.program_id(1)

        @pl.when(t == 0)
        def _():
            acc_ref[...] = jnp.zeros_like(acc_ref)
        for s in range(2):
            acc_ref[s] += _tn(a_ref[s], b_ref[...])

        @pl.when(t == nt - 1)
        def _():
            o_ref[...] = (0.5 * acc_ref[...]).astype(BF16)

    return pl.pallas_call(
        body, name=name, grid=(nh // 2, nt),
        in_specs=[pl.BlockSpec((2, tt, fb), lambda j, t: (j, t, 0)), pl.BlockSpec((tt, D), lambda j, t: (t, 0))],
        out_specs=pl.BlockSpec((2, fb, D), lambda j, t: (j, 0, 0)), out_shape=SDS((nh, fb, D), BF16),
        scratch_shapes=[pltpu.VMEM((2, fb, D), F32)], compiler_params=_cp("parallel", "arbitrary"))(act, dyb)


def _ffn_dwgu(h, dgu, name, tt=2048):
    T, D = h.shape
    _, _, fb = dgu.shape
    tt = _tile(T, tt, 2 * SUBLANES)
    nt = T // tt

    def body(a_ref, b_ref, o_ref, acc_ref):
        t = pl.program_id(1)

        @pl.when(t == 0)
        def _():
            acc_ref[...] = jnp.zeros_like(acc_ref)
        for s in range(2):
            acc_ref[s] += _tn(a_ref[...], b_ref[s])

        @pl.when(t == nt - 1)
        def _():
            o_ref[...] = acc_ref[...].astype(BF16)

    return pl.pallas_call(
        body, name=name, grid=(N_DEV // 2, nt),
        in_specs=[pl.BlockSpec((tt, D), lambda j, t: (t, 0)), pl.BlockSpec((2, tt, fb), lambda j, t: (j, t, 0))],
        out_specs=pl.BlockSpec((2, D, fb), lambda j, t: (j, 0, 0)), out_shape=SDS((N_DEV, D, fb), BF16),
        scratch_shapes=[pltpu.VMEM((2, D, fb), F32)], compiler_params=_cp("parallel", "arbitrary"))(h, dgu)


def _ffn_bwd_in(dgu, wgu, l, x, g, dres, name, scatter, tm=512):
    T, D = x.shape
    _, _, fb = dgu.shape
    tm = _tile(T, tm, SUBLANES)

    def body(a_ref, w_ref, x_ref, g_ref, r_ref, dx_ref, dxb_ref, dg_ref):
        dh = _nt(a_ref[0], w_ref[0])
        for d in range(1, N_DEV):
            dh = dh + _nt(a_ref[d], w_ref[d])
        _norm_bwd_store(dh, x_ref, g_ref, r_ref, dx_ref, dxb_ref, dg_ref, pl.program_id(0) == 0)

    row = pl.BlockSpec((tm, D), lambda i: (i, 0))
    return _hosted_call(
        body, name, T // tm, [pl.BlockSpec((N_DEV, tm, fb), lambda i: (0, i, 0)), _slots(wgu, l), row, _full(g), row],
        [row, row, pl.BlockSpec((1, D), lambda i: (0, 0))],
        [SDS((T, D), F32), SDS((T, D), BF16), SDS((1, D), F32)], (dgu, wgu, x, g, dres), scatter)


def _loss_head(x, g, tgt, name, tm=512):
    T, D = x.shape
    tm = _tile(T, tm, SUBLANES)

    def body(x_ref, g_ref, t_ref, loss_ref, dx_ref, dxb_ref, dg_ref):
        first = pl.program_id(0) == 0
        xv = x_ref[...]
        r = _rstd(xv)
        xh = xv * r
        e = xh * g_ref[...] - t_ref[...]
        part = 0.5 * jnp.sum(jnp.mean(e * e, axis=-1, keepdims=True), axis=0, keepdims=True)
        dy = e * (1.0 / D)
        dxh = dy * g_ref[...]
        dx = r * (dxh - xh * jnp.mean(dxh * xh, axis=-1, keepdims=True))
        dx_ref[...] = dx
        dxb_ref[...] = dx.astype(BF16)
        _accumulate(loss_ref, first, jnp.broadcast_to(part, loss_ref.shape))
        _accumulate(dg_ref, first, jnp.sum(dy * xh, axis=0, keepdims=True))

    row = pl.BlockSpec((tm, D), lambda i: (i, 0))
    vec = pl.BlockSpec((1, D), lambda i: (0, 0))
    return pl.pallas_call(
        body, name=name, grid=(T // tm,), in_specs=[row, vec, row],
        out_specs=[pl.BlockSpec((SUBLANES, LANES), lambda i: (0, 0)), row, row, vec],
        out_shape=[SDS((SUBLANES, LANES), F32), SDS((T, D), F32), SDS((T, D), BF16), SDS((1, D), F32)],
        compiler_params=_cp("arbitrary"))(x, g, tgt)


def _shift_down(u, k, prev_row):
    rows = lax.broadcasted_iota(I32, u.shape, 0)
    s = pltpu.roll(u, k, 0)
    for t in range(k):
        s = jnp.where(rows == t, prev_row(SUBLANES - k + t), s)
    return s


def _shift_up(u, k, next_row):
    n = u.shape[0]
    rows = lax.broadcasted_iota(I32, u.shape, 0)
    s = pltpu.roll(u, n - k, 0)
    for t in range(k):
        s = jnp.where(rows == n - k + t, next_row(t), s)
    return s


def _conv_taps(z_ref, zp_ref, w_ref, cw, first):
    u = z_ref[:, cw:2 * cw] * z_ref[:, 2 * cw:]

    def prev_row(r):
        return jnp.where(first, 0.0, zp_ref[r:r + 1, cw:2 * cw] * zp_ref[r:r + 1, 2 * cw:])

    u1 = _shift_down(u, 1, prev_row)
    u2 = _shift_down(u, 2, prev_row)
    cv = w_ref[0:1, :] * u2 + w_ref[1:2, :] * u1 + w_ref[2:3, :] * u
    return u, u1, u2, cv


def _conv_fwd(zc, wc, gc, name, tm=512):
    T = zc.shape[0]
    cw = zc.shape[1] // 3
    tm = _tile(T, tm, SUBLANES)
    hb = tm // SUBLANES

    def body(z_ref, zp_ref, w_ref, g_ref, o_ref):
        _, _, _, cv = _conv_taps(z_ref, zp_ref, w_ref, cw, pl.program_id(0) == 0)
        y = z_ref[:, :cw] * cv
        o_ref[...] = (y * _rstd(y) * g_ref[...]).astype(BF16)

    return pl.pallas_call(
        body, name=name, grid=(T // tm,),
        in_specs=[pl.BlockSpec((tm, 3 * cw), lambda i: (i, 0)),
                  pl.BlockSpec((SUBLANES, 3 * cw), lambda i: (jnp.maximum(i * hb - 1, 0), 0)), _full(wc), _full(gc)],
        out_specs=pl.BlockSpec((tm, cw), lambda i: (i, 0)), out_shape=SDS((T, cw), BF16),
        compiler_params=_cp("parallel"))(zc, zc, wc, gc)


def _conv_bwd(zc, dy, wc, gc, name, tm=512):
    T = zc.shape[0]
    cw = zc.shape[1] // 3
    tm = _tile(T, tm, SUBLANES)
    hb = tm // SUBLANES
    nb = T // tm

    def body(z_ref, zp_ref, dy_ref, w_ref, g_ref, dz_ref, dw_ref, dg_ref, carry_ref):
        s = pl.program_id(0)
        u, u1, u2, cv = _conv_taps(z_ref, zp_ref, w_ref, cw, s == nb - 1)
        zb = z_ref[:, :cw]
        y = zb * cv
        r = _rstd(y)
        yh = y * r
        dyn = dy_ref[...]
        dyh = dyn * g_ref[...]
        dyc = r * (dyh - yh * jnp.mean(dyh * yh, axis=-1, keepdims=True))
        dcv = dyc * zb

        def next_row(t):
            return jnp.where(s == 0, 0.0, carry_ref[t:t + 1, :])

        du = w_ref[2:3, :] * dcv + w_ref[1:2, :] * _shift_up(dcv, 1, next_row) + w_ref[0:1, :] * _shift_up(dcv, 2, next_row)
        carry_ref[...] = dcv[0:SUBLANES, :]
        dz_ref[:, :cw] = (dyc * cv).astype(BF16)
        dz_ref[:, cw:2 * cw] = (du * z_ref[:, 2 * cw:]).astype(BF16)
        dz_ref[:, 2 * cw:] = (du * z_ref[:, cw:2 * cw]).astype(BF16)
        tap = lax.broadcasted_iota(I32, (SUBLANES, cw), 0)
        dwp = jnp.where(tap == 0, jnp.sum(dcv * u2, axis=0, keepdims=True),
                        jnp.where(tap == 1, jnp.sum(dcv * u1, axis=0, keepdims=True),
                                  jnp.where(tap == 2, jnp.sum(dcv * u, axis=0, keepdims=True), 0.0)))
        _accumulate(dw_ref, s == 0, dwp)
        _accumulate(dg_ref, s == 0, jnp.sum(dyn * yh, axis=0, keepdims=True))

    rev = lambda s: nb - 1 - s
    return pl.pallas_call(
        body, name=name, grid=(nb,),
        in_specs=[pl.BlockSpec((tm, 3 * cw), lambda s: (rev(s), 0)),
                  pl.BlockSpec((SUBLANES, 3 * cw), lambda s: (jnp.maximum(rev(s) * hb - 1, 0), 0)),
                  pl.BlockSpec((tm, cw), lambda s: (rev(s), 0)), _full(wc), _full(gc)],
        out_specs=[pl.BlockSpec((tm, 3 * cw), lambda s: (rev(s), 0)), pl.BlockSpec((SUBLANES, cw), lambda s: (0, 0)),
                   pl.BlockSpec((1, cw), lambda s: (0, 0))],
        out_shape=[SDS((T, 3 * cw), BF16), SDS((SUBLANES, cw), F32), SDS((1, cw), F32)],
        scratch_shapes=[pltpu.VMEM((SUBLANES, cw), F32)],
        compiler_params=_cp("arbitrary"))(zc, zc, dy, wc, gc)


def _proj_t(h, wat, wft, B, name):
    T, D = h.shape
    R = wat.shape[0]
    nb = T // B

    def body(h_ref, wa_ref, wf_ref, za_ref, zf_ref):
        hv = h_ref[...]
        za_ref[...] = _nt(wa_ref[...], hv).astype(BF16)
        zf_ref[...] = _nt(wf_ref[...], hv)

    return pl.pallas_call(
        body, name=name, grid=(nb,), in_specs=[pl.BlockSpec((B, D), lambda i: (i, 0)), _full(wat), _full(wft)],
        out_specs=[pl.BlockSpec((None, R, B), lambda i: (i, 0, 0)), pl.BlockSpec((SUBLANES, B), lambda i: (0, i))],
        out_shape=[SDS((nb, R, B), BF16), SDS((SUBLANES, T), F32)], compiler_params=_cp("parallel"))(h, wat, wft)


def _split3(v):
    hi = v.astype(BF16).astype(F32)
    r1 = v - hi
    mid = r1.astype(BF16).astype(F32)
    lo = (r1 - mid).astype(BF16).astype(F32)
    return hi, mid, lo


def _tri_dot3(v, tri):
    hi, mid, lo = _split3(v)
    return _nn(hi.astype(BF16), tri) + _nn(mid.astype(BF16), tri) + _nn(lo.astype(BF16), tri)


def _logf_cumsum(zft, b, name, tb=512):
    H, T = zft.shape
    tb = _tile(T, tb, LANES)

    def body(z_ref, b_ref, c_ref, hi_ref, mid_ref, lo_ref, carry_ref):
        @pl.when(pl.program_id(0) == 0)
        def _():
            carry_ref[...] = jnp.zeros_like(carry_ref)
        xv = z_ref[...] + b_ref[...]
        lf = jnp.minimum(xv, 0.0) - jnp.log(1.0 + jnp.exp(-jnp.abs(xv)))
        src = lax.broadcasted_iota(I32, (tb, tb), 0)
        dst = lax.broadcasted_iota(I32, (tb, tb), 1)
        tri = jnp.where(src <= dst, 1.0, 0.0).astype(BF16)
        cs = _tri_dot3(lf, tri) + carry_ref[...]
        c_ref[...] = cs
        hi, mid, lo = _split3(cs)
        hi_ref[...] = hi
        mid_ref[...] = mid
        lo_ref[...] = lo
        col = lax.broadcasted_iota(I32, cs.shape, 1)
        carry_ref[...] = jnp.sum(jnp.where(col == tb - 1, cs, 0.0), axis=1, keepdims=True)

    blk = pl.BlockSpec((H, tb), lambda i: (0, i))
    return pl.pallas_call(
        body, name=name, grid=(T // tb,), in_specs=[blk, _full(b)],
        out_specs=[blk, blk, blk, blk], out_shape=[SDS((H, T), F32)] * 4,
        scratch_shapes=[pltpu.VMEM((H, 1), F32)], compiler_params=_cp("arbitrary"))(zft, b)


def _logf_cumsum_bwd(dcq, dck, zft, b, name, tb=512):
    H, T = zft.shape
    tb = _tile(T, tb, LANES)
    nb = T // tb

    def body(dq_ref, dk_ref, z_ref, b_ref, o_ref, db_ref, carry_ref):
        s = pl.program_id(0)

        @pl.when(s == 0)
        def _():
            carry_ref[...] = jnp.zeros_like(carry_ref)
        dc = dq_ref[...] - dk_ref[...]
        src = lax.broadcasted_iota(I32, (tb, tb), 0)
        dst = lax.broadcasted_iota(I32, (tb, tb), 1)
        tri = jnp.where(src >= dst, 1.0, 0.0).astype(BF16)
        dl = _tri_dot3(dc, tri) + carry_ref[...]
        col = lax.broadcasted_iota(I32, dl.shape, 1)
        carry_ref[...] = jnp.sum(jnp.where(col == 0, dl, 0.0), axis=1, keepdims=True)
        dz = dl * jax.nn.sigmoid(-(z_ref[...] + b_ref[...]))
        o_ref[...] = dz
        _accumulate(db_ref, s == 0, jnp.sum(dz, axis=1, keepdims=True))

    blk = pl.BlockSpec((H, tb), lambda s: (0, nb - 1 - s))
    vec = pl.BlockSpec((H, 1), lambda s: (0, 0))
    return pl.pallas_call(
        body, name=name, grid=(nb,), in_specs=[blk, blk, blk, vec], out_specs=[blk, vec],
        out_shape=[SDS((H, T), F32), SDS((H, 1), F32)],
        scratch_shapes=[pltpu.VMEM((H, 1), F32)], compiler_params=_cp("arbitrary"))(dcq, dck, zft, b)


def _head_norms(zat, name):
    nb, R, B = zat.shape
    aw = R // 3
    g = 2 * aw // HEAD_DIM

    def body(z_ref, o_ref):
        zv = z_ref[...].astype(F32)
        ss = jnp.sum((zv * zv).reshape(g, HEAD_DIM, B), axis=1)
        mx = jnp.broadcast_to(jnp.max(ss, axis=1, keepdims=True), o_ref.shape)

        @pl.when(pl.program_id(0) == 0)
        def _():
            o_ref[...] = mx

        @pl.when(pl.program_id(0) > 0)
        def _():
            o_ref[...] = jnp.maximum(o_ref[...], mx)

    return pl.pallas_call(
        body, name=name, grid=(nb,), in_specs=[pl.BlockSpec((None, 2 * aw, B), lambda i: (i, 0, 0))],
        out_specs=pl.BlockSpec((g, LANES), lambda i: (0, 0)), out_shape=SDS((g, LANES), F32),
        compiler_params=_cp("arbitrary"))(zat)


def _skip_table(cs_col, ce_row, norms, name):
    H, nb, _ = cs_col.shape

    def body(cs_ref, ce_ref, n_ref, jm_ref, im_ref):
        h = pl.program_id(0)
        nq = n_ref[pl.ds(h, 1), 0:1]
        nk = n_ref[pl.ds(H + h, 1), 0:1]
        bound = 2.0 * jnp.sqrt(nq * nk) * 0.125
        skip = jnp.where(bound + cs_ref[...] - ce_ref[...] <= -SKIP_MARGIN, 1, 0).astype(I32)
        jm_ref[...] = jnp.sum(skip, axis=1, keepdims=True)
        im_ref[...] = nb - 1 - jnp.sum(skip, axis=0, keepdims=True)

    col = pl.BlockSpec((None, nb, 1), lambda h: (h, 0, 0))
    row = pl.BlockSpec((None, 1, nb), lambda h: (h, 0, 0))
    return pl.pallas_call(
        body, name=name, grid=(H,), in_specs=[col, row, _full(norms)], out_specs=[col, row],
        out_shape=[SDS((H, nb, 1), I32), SDS((H, 1, nb), I32)], compiler_params=_cp("parallel"))(cs_col, ce_row, norms)


def _causal_mask(B):
    krow = lax.broadcasted_iota(I32, (B, B), 0)
    qcol = lax.broadcasted_iota(I32, (B, B), 1)
    return krow <= qcol


def _attn_fwd(jmin, zat, qa, ka, name):
    nb, R, B = zat.shape
    aw = R // 3
    H = aw // HEAD_DIM
    hd = HEAD_DIM

    def body(jm_ref, q_ref, k_ref, v_ref, qa_ref, ka_ref, o_ref, lse_ref):
        h = pl.program_id(0)
        i = pl.program_id(1)
        qt = jnp.concatenate([q_ref[...] * 0.125, qa_ref[...]], axis=0)

        def block(j, carry, masked):
            m, l, acc = carry
            s = _tn(jnp.concatenate([k_ref[j], ka_ref[j]], axis=0), qt)
            if masked:
                s = jnp.where(_causal_mask(B), s, NEG)
            m_new = jnp.maximum(m, jnp.max(s, axis=0, keepdims=True))
            p = jnp.exp(s - m_new)
            a = jnp.exp(m - m_new)
            l = a * l + jnp.sum(p, axis=0, keepdims=True)
            acc = a * acc + _nn(v_ref[j], p.astype(BF16))
            return m_new, l, acc

        init = (jnp.full((1, B), NEG, F32), jnp.zeros((1, B), F32), jnp.zeros((hd, B), F32))
        carry = lax.fori_loop(jm_ref[h * nb + i], i, lambda j, c: block(j, c, False), init)
        m, l, acc = block(i, carry, True)
        o_ref[...] = acc / l
        lse_ref[...] = m + jnp.log(l)

    grid_spec = pltpu.PrefetchScalarGridSpec(
        num_scalar_prefetch=1, grid=(H, nb),
        in_specs=[pl.BlockSpec((None, hd, B), lambda h, i, jm: (i, h, 0)),
                  pl.BlockSpec((nb, hd, B), lambda h, i, jm: (0, H + h, 0)),
                  pl.BlockSpec((nb, hd, B), lambda h, i, jm: (0, 2 * H + h, 0)),
                  pl.BlockSpec((None, None, AUG, B), lambda h, i, jm: (h, i, 0, 0)),
                  pl.BlockSpec((None, nb, AUG, B), lambda h, i, jm: (h, 0, 0, 0))],
        out_specs=[pl.BlockSpec((None, hd, B), lambda h, i, jm: (i, h, 0)),
                   pl.BlockSpec((None, None, 1, B), lambda h, i, jm: (h, i, 0, 0))])
    return pl.pallas_call(
        body, name=name, grid_spec=grid_spec,
        out_shape=[SDS((nb, aw, B), F32), SDS((H, nb, 1, B), F32)],
        compiler_params=_cp("parallel", "parallel"))(jmin, zat, zat, zat, qa, ka)


def _gnorm_t_fwd(ot, g_col, name):
    nb, aw, B = ot.shape

    def body(o_ref, g_ref, y_ref):
        ov = o_ref[...]
        r = lax.rsqrt(jnp.mean(ov * ov, axis=0, keepdims=True) + EPS)
        y_ref[...] = (ov * r * g_ref[...]).T.astype(BF16)

    return pl.pallas_call(
        body, name=name, grid=(nb,), in_specs=[pl.BlockSpec((None, aw, B), lambda i: (i, 0, 0)), _full(g_col)],
        out_specs=pl.BlockSpec((B, aw), lambda i: (i, 0)), out_shape=SDS((nb * B, aw), BF16),
        compiler_params=_cp("parallel"))(ot, g_col)


def _gnorm_t_bwd(ot, g_col, dy, name):
    nb, aw, B = ot.shape
    H = aw // HEAD_DIM

    def body(o_ref, g_ref, dy_ref, do_ref, dl_ref, dg_ref):
        ov = o_ref[...]
        r = lax.rsqrt(jnp.mean(ov * ov, axis=0, keepdims=True) + EPS)
        yh = ov * r
        dyn = dy_ref[...].T
        dyh = dyn * g_ref[...]
        dov = r * (dyh - yh * jnp.mean(dyh * yh, axis=0, keepdims=True))
        do_ref[...] = dov.astype(BF16)
        dl_ref[...] = jnp.sum((dov * ov).reshape(H, HEAD_DIM, B), axis=1)
        dgp = jnp.broadcast_to(jnp.sum(dyn * yh, axis=1, keepdims=True), dg_ref.shape)
        _accumulate(dg_ref, pl.program_id(0) == 0, dgp)

    blk = pl.BlockSpec((None, aw, B), lambda i: (i, 0, 0))
    return pl.pallas_call(
        body, name=name, grid=(nb,), in_specs=[blk, _full(g_col), pl.BlockSpec((B, aw), lambda i: (i, 1))],
        out_specs=[blk, pl.BlockSpec((None, H, B), lambda i: (i, 0, 0)), pl.BlockSpec((aw, LANES), lambda i: (0, 0))],
        out_shape=[SDS((nb, aw, B), BF16), SDS((nb, H, B), F32), SDS((aw, LANES), F32)],
        compiler_params=_cp("arbitrary"))(ot, g_col, dy)


def _attn_bwd(imax, zat, qa, ka, dot, lse, delta, name):
    nb, R, B = zat.shape
    aw = R // 3
    H = aw // HEAD_DIM
    hd = HEAD_DIM

    def body(im_ref, k_ref, v_ref, ka_ref, q_ref, qa_ref, do_ref, lse_ref, dl_ref,
             dq_ref, dqa_ref, dk_ref, dka_ref, dv_ref):
        h = pl.program_id(0)
        j = pl.program_id(1)

        @pl.when(j == 0)
        def _():
            dq_ref[...] = jnp.zeros_like(dq_ref)
            dqa_ref[...] = jnp.zeros_like(dqa_ref)
        kt = jnp.concatenate([k_ref[...], ka_ref[...]], axis=0)
        vt = v_ref[...]

        def block(i, carry, masked):
            dk, dv = carry
            qt = jnp.concatenate([q_ref[i] * 0.125, qa_ref[i]], axis=0)
            dov = do_ref[i]
            s = _tn(kt, qt)
            if masked:
                s = jnp.where(_causal_mask(B), s, NEG)
            p = jnp.exp(s - lse_ref[i])
            dp = _tn(vt, dov)
            ds = (p * (dp - dl_ref[i, pl.ds(h, 1), :])).astype(BF16)
            dv = dv + _nt(dov, p.astype(BF16))
            dk = dk + _nt(qt, ds)
            r = _nn(kt, ds)
            dq_ref[i] += 0.125 * r[:hd]
            dqa_ref[i] += r[hd:]
            return dk, dv

        carry = block(j, (jnp.zeros((hd + AUG, B), F32), jnp.zeros((hd, B), F32)), True)
        dk, dv = lax.fori_loop(j + 1, im_ref[h * nb + j] + 1, lambda i, c: block(i, c, False), carry)
        dk_ref[...] = dk[:hd]
        dka_ref[...] = dk[hd:]
        dv_ref[...] = dv

    head_rows = lambda off: pl.BlockSpec((nb, hd, B), lambda h, j, im: (0, off + h, 0))
    key_rows = lambda off: pl.BlockSpec((None, hd, B), lambda h, j, im: (j, off + h, 0))
    aug_all = pl.BlockSpec((None, nb, AUG, B), lambda h, j, im: (h, 0, 0, 0))
    aug_one = pl.BlockSpec((None, None, AUG, B), lambda h, j, im: (h, j, 0, 0))
    grid_spec = pltpu.PrefetchScalarGridSpec(
        num_scalar_prefetch=1, grid=(H, nb),
        in_specs=[key_rows(H), key_rows(2 * H), aug_one, head_rows(0), aug_all, head_rows(0),
                  pl.BlockSpec((None, nb, 1, B), lambda h, j, im: (h, 0, 0, 0)),
                  pl.BlockSpec((nb, H, B), lambda h, j, im: (0, 0, 0))],
        out_specs=[head_rows(0), aug_all, key_rows(0), aug_one, key_rows(0)])
    return pl.pallas_call(
        body, name=name, grid_spec=grid_spec,
        out_shape=[SDS((nb, aw, B), F32), SDS((H, nb, AUG, B), F32), SDS((nb, aw, B), F32),
                   SDS((H, nb, AUG, B), F32), SDS((nb, aw, B), F32)],
        compiler_params=_cp("arbitrary", "arbitrary"))(imax, zat, zat, ka, zat, qa, dot, lse, delta)


def _mix_dwt(dq, dk, dv, dzft, h, name):
    nb, aw, B = dq.shape
    D = h.shape[1]

    def body(dq_ref, dk_ref, dv_ref, dzf_ref, h_ref, o_ref, of_ref):
        first = pl.program_id(0) == 0
        hv = h_ref[...]
        part = jnp.concatenate([_nn(r[...].astype(BF16), hv) for r in (dq_ref, dk_ref, dv_ref)], axis=0)
        _accumulate(o_ref, first, part)
        _accumulate(of_ref, first, _nn(dzf_ref[...].astype(BF16), hv))

    blk = pl.BlockSpec((None, aw, B), lambda i: (i, 0, 0))
    return pl.pallas_call(
        body, name=name, grid=(nb,),
        in_specs=[blk, blk, blk, pl.BlockSpec((SUBLANES, B), lambda i: (0, i)), pl.BlockSpec((B, D), lambda i: (i, 0))],
        out_specs=[pl.BlockSpec((3 * aw, D), lambda i: (0, 0)), pl.BlockSpec((SUBLANES, D), lambda i: (0, 0))],
        out_shape=[SDS((3 * aw, D), F32), SDS((SUBLANES, D), F32)],
        compiler_params=_cp("arbitrary"))(dq, dk, dv, dzft, h)


def _mix_bwd_in(dzc, dq, dk, dv, dzft, wc, wat, wft, x, g, dres, name, scatter):
    T, D = x.shape
    nb, aw, B = dq.shape

    def body(dzc_ref, dq_ref, dk_ref, dv_ref, dzf_ref, wc_ref, wa_ref, wf_ref, x_ref, g_ref, r_ref,
             dx_ref, dxb_ref, dg_ref):
        dh = _nt(dzc_ref[...], wc_ref[...])
        for s, r in enumerate((dq_ref, dk_ref, dv_ref)):
            dh = dh + _tn(r[...].astype(BF16), wa_ref[s * aw:(s + 1) * aw, :])
        dh = dh + _tn(dzf_ref[...].astype(BF16), wf_ref[...])
        _norm_bwd_store(dh, x_ref, g_ref, r_ref, dx_ref, dxb_ref, dg_ref, pl.program_id(0) == 0)

    blk = pl.BlockSpec((None, aw, B), lambda i: (i, 0, 0))
    row = lambda n: pl.BlockSpec((B, n), lambda i: (i, 0))
    return _hosted_call(
        body, name, nb,
        [row(dzc.shape[1]), blk, blk, blk, pl.BlockSpec((SUBLANES, B), lambda i: (0, i)),
         _full(wc), _full(wat), _full(wft), row(D), _full(g), row(D)],
        [row(D), row(D), pl.BlockSpec((1, D), lambda i: (0, 0))],
        [SDS((T, D), F32), SDS((T, D), BF16), SDS((1, D), F32)],
        (dzc, dq, dk, dv, dzft, wc, wat, wft, x, g, dres), scatter)


def _mem_kv(mem, g, wkv, l, name):
    M, D = mem.shape
    dx = wkv.shape[3]

    def body(m_ref, g_ref, w_ref, kv_ref, h_ref):
        mv = m_ref[...]
        h = (mv * _rstd(mv) * g_ref[...]).astype(BF16)
        h_ref[...] = h
        for s in range(N_DEV):
            kv_ref[s] = _nn(h, w_ref[s]).astype(BF16)

    return pl.pallas_call(
        body, name=name, grid=(1,), in_specs=[_full(mem), _full(g), _slots(wkv, l)],
        out_specs=[pl.BlockSpec((N_DEV, M, dx), lambda i: (0, 0, 0)), _full(mem)],
        out_shape=[SDS((N_DEV, M, dx), BF16), SDS((M, D), BF16)], compiler_params=_cp("arbitrary"))(mem, g, wkv)


def _mem_kv_bwd(dkv, hm, wkv, l, mem, g, name):
    M, D = mem.shape
    dx = wkv.shape[3]

    def body(dkv_ref, h_ref, w_ref, m_ref, g_ref, dw_ref, dg_ref):
        hv = h_ref[...]
        dh = jnp.zeros((M, D), F32)
        for s in range(N_DEV):
            d = dkv_ref[s].astype(BF16)
            dw_ref[s] = _tn(hv, d).astype(BF16)
            dh = dh + _nt(d, w_ref[s])
        mv = m_ref[...]
        dg_ref[...] = jnp.sum(dh * (mv * _rstd(mv)), axis=0, keepdims=True)

    return pl.pallas_call(
        body, name=name, grid=(1,), in_specs=[_full(dkv), _full(hm), _slots(wkv, l), _full(mem), _full(g)],
        out_specs=[pl.BlockSpec((N_DEV, D, dx), lambda i: (0, 0, 0)), pl.BlockSpec((1, D), lambda i: (0, 0))],
        out_shape=[SDS((N_DEV, D, dx), BF16), SDS((1, D), F32)], compiler_params=_cp("arbitrary"))(dkv, hm, wkv, mem, g)


def _xattn_probs(q_ref, kv_ref, h, dx, scale):
    qh = q_ref[:, h * dx:(h + 1) * dx]
    kh = kv_ref[h]
    s = _nt(qh, kh) * scale
    p = jnp.exp(s - jnp.max(s, axis=-1, keepdims=True))
    return qh, kh, p / jnp.sum(p, axis=-1, keepdims=True)


def _xattn_fwd(q, kv, name, tm=512):
    T, D = q.shape
    dx = D // N_XHEADS
    scale = dx ** -0.5
    tm = _tile(T, tm, SUBLANES)

    def body(q_ref, kv_ref, o_ref):
        for h in range(N_XHEADS):
            _, _, p = _xattn_probs(q_ref, kv_ref, h, dx, scale)
            o_ref[:, h * dx:(h + 1) * dx] = _nn(p.astype(BF16), kv_ref[N_XHEADS + h]).astype(BF16)

    row = pl.BlockSpec((tm, D), lambda i: (i, 0))
    return pl.pallas_call(body, name=name, grid=(T // tm,), in_specs=[row, _full(kv)], out_specs=row,
                          out_shape=SDS((T, D), BF16), compiler_params=_cp("parallel"))(q, kv)


def _xattn_bwd(q, kv, do, name, tm=512):
    T, D = q.shape
    dx = D // N_XHEADS
    scale = dx ** -0.5
    tm = _tile(T, tm, SUBLANES)

    def body(q_ref, kv_ref, do_ref, dq_ref, dkv_ref):
        @pl.when(pl.program_id(0) == 0)
        def _():
            dkv_ref[...] = jnp.zeros_like(dkv_ref)
        for h in range(N_XHEADS):
            qh, kh, p = _xattn_probs(q_ref, kv_ref, h, dx, scale)
            doh = do_ref[:, h * dx:(h + 1) * dx]
            dp = _nt(doh, kv_ref[N_XHEADS + h])
            ds = (p * (dp - jnp.sum(p * dp, axis=-1, keepdims=True)) * scale).astype(BF16)
            dq_ref[:, h * dx:(h + 1) * dx] = _nn(ds, kh).astype(BF16)
            dkv_ref[h] += _tn(ds, qh)
            dkv_ref[N_XHEADS + h] += _tn(p.astype(BF16), doh)

    row = pl.BlockSpec((tm, D), lambda i: (i, 0))
    return pl.pallas_call(body, name=name, grid=(T // tm,), in_specs=[row, _full(kv), row],
                          out_specs=[row, _full(kv)], out_shape=[SDS((T, D), BF16), SDS(kv.shape, F32)],
                          compiler_params=_cp("arbitrary"))(q, kv, do)


def _mesh_position():
    return lax.axis_index("x"), lax.axis_index("y"), lax.axis_index("c")


def _all_gather(shards, name):
    n = len(shards)

    def body(*refs):
        x_refs, out_refs = refs[:n], refs[n:2 * n]
        send_sems, recv_sems, local_sems = refs[2 * n:]
        x, y, c = _mesh_position()
        me, sibling = (x, y, c), (x, y, 1 - c)
        chips = [(1 - x, y), (x, 1 - y), (1 - x, 1 - y)]

        def slot(w, px, py, pc):
            return out_refs[w].at[4 * px + 2 * py + pc]

        def copy(w, k, block, to, src=None):
            return pltpu.make_async_remote_copy(
                src_ref=slot(w, *block) if src is None else src, dst_ref=slot(w, *block),
                send_sem=send_sems.at[w, k], recv_sem=recv_sems.at[w, k], device_id=to,
                device_id_type=pl.DeviceIdType.MESH)

        mine = [pltpu.make_async_copy(x_refs[w], slot(w, *me), local_sems.at[w]) for w in range(n)]
        first = []
        for w in range(n):
            mine[w].start()
            first.append(copy(w, 0, me, sibling, src=x_refs[w]))
            first += [copy(w, 1 + j, me, (*chip, c), src=x_refs[w]) for j, chip in enumerate(chips)]
        for cp in first:
            cp.start()
        passed = []
        for w in range(n):
            for j, chip in enumerate(chips):
                copy(w, 1 + j, (*chip, c), me).wait_recv()
                fwd = copy(w, 4 + j, (*chip, c), sibling)
                fwd.start()
                passed.append(fwd)
        for w in range(n):
            copy(w, 0, sibling, me).wait_recv()
            for j, chip in enumerate(chips):
                copy(w, 4 + j, (*chip, 1 - c), me).wait_recv()
        for cp in first + passed:
            cp.wait_send()
        for w in range(n):
            mine[w].wait()

    any_spec = pl.BlockSpec(memory_space=pl.ANY)
    return pl.pallas_call(
        body, name=name, out_shape=[SDS((N_DEV,) + s.shape, s.dtype) for s in shards],
        in_specs=[any_spec] * n, out_specs=[any_spec] * n,
        scratch_shapes=[pltpu.SemaphoreType.DMA((n, 7)), pltpu.SemaphoreType.DMA((n, 7)), pltpu.SemaphoreType.DMA((n,))],
    )(*shards)


def _scatter_copies(x_refs, out_refs, send_sems, recv_sems, local_sems):
    x, y, c = _mesh_position()
    me = 4 * x + 2 * y + c
    own, remote = [], []
    for w, (src, dst) in enumerate(zip(x_refs, out_refs)):
        own.append(pltpu.make_async_copy(src.at[me], dst.at[me], local_sems.at[w]))
        for k in range(1, N_DEV):
            px = 1 - x if k & 4 else x
            py = 1 - y if k & 2 else y
            pc = 1 - c if k & 1 else c
            remote.append(pltpu.make_async_remote_copy(
                src_ref=src.at[4 * px + 2 * py + pc], dst_ref=dst.at[me], send_sem=send_sems.at[w, k - 1],
                recv_sem=recv_sems.at[w, k - 1], device_id=(px, py, pc), device_id_type=pl.DeviceIdType.MESH))
    return own, remote


def _scatter_start(x_refs, out_refs, send_sems, recv_sems, local_sems):
    own, remote = _scatter_copies(x_refs, out_refs, send_sems, recv_sems, local_sems)
    for cp in own + remote:
        cp.start()


def _scatter_wait(x_refs, out_refs, send_sems, recv_sems, local_sems):
    own, remote = _scatter_copies(x_refs, out_refs, send_sems, recv_sems, local_sems)
    for cp in remote:
        cp.wait_recv()
    for cp in remote:
        cp.wait_send()
    for cp in own:
        cp.wait()


def _sum_adamw(parts, w, m, v, name, tr=256):
    L, R, C = w.shape
    tr = _tile(R, tr, SUBLANES)
    nblk = R // tr

    def body(*refs):
        p_refs = refs[:L]
        w_ref, m_ref, v_ref, g_ref, d_ref, mo_ref, vo_ref = refs[L:]
        layer = pl.program_id(0)
        g = None
        for k, p_ref in enumerate(p_refs):
            gk = p_ref[0].astype(F32)
            for s in range(1, N_DEV):
                gk = gk + p_ref[s].astype(F32)
            g = gk if g is None else jnp.where(layer == k, gk, g)
        mn = ADAM_B1 * m_ref[...] + (1.0 - ADAM_B1) * g
        vn = ADAM_B2 * v_ref[...] + (1.0 - ADAM_B2) * jnp.square(g)
        m_hat = mn / (1.0 - ADAM_B1 ** ADAM_STEP)
        v_hat = vn / (1.0 - ADAM_B2 ** ADAM_STEP)
        g_ref[...] = g
        d_ref[...] = -ADAM_LR * (m_hat / (jnp.sqrt(v_hat) + ADAM_EPS) + ADAM_WD * w_ref[...])
        mo_ref[...] = mn
        vo_ref[...] = vn

    def part_spec(k):
        return pl.BlockSpec((N_DEV, tr, C),
                            lambda l, i: (0, jnp.where(l == k, i, jnp.where(l < k, 0, nblk - 1)), 0))

    row = pl.BlockSpec((None, tr, C), lambda l, i: (l, i, 0))
    return pl.pallas_call(
        body, name=name, grid=(L, nblk), in_specs=[part_spec(k) for k in range(L)] + [row, row, row],
        out_specs=[row] * 4, out_shape=[SDS((L, R, C), F32)] * 4,
        compiler_params=_cp("arbitrary", "arbitrary"))(*parts, w, m, v)


def _pack(flat_list, row_mult):
    flat = jnp.concatenate(flat_list, axis=-1)
    n = flat.shape[-1]
    chunk = row_mult * PACK_COLS
    pad = -n % chunk
    flat = jnp.pad(flat, [(0, 0)] * (flat.ndim - 1) + [(0, pad)])
    return flat.reshape(flat.shape[:-1] + ((n + pad) // PACK_COLS, PACK_COLS))


def _unpack(packed, shapes):
    lead = packed.shape[:-2]
    flat = packed.reshape(lead + (-1,))
    out = []
    off = 0
    for shp in shapes:
        n = 1
        for d in shp:
            n *= d
        out.append(flat[..., off:off + n].reshape(lead + tuple(shp)))
        off += n
    return out


def _ffn_fwd(x, g, wgu, wd, l, tag):
    gu, act, h = _ffn_gu(x, g, wgu, l, f"{tag}_gu")
    xo = _ffn_down(act, wd, l, x, f"{tag}_down")
    return xo, (x, g, h, gu, act)


def _ffn_bwd(dx, dxb, saved, wgu, wd, l, tag):
    x, g, h, gu, act = saved
    dgu = _ffn_bwd_act(dxb, wd, l, gu, f"{tag}_bwd_act")
    dwd = _ffn_dwd(act, dxb, f"{tag}_dwd")
    dwgu = _ffn_dwgu(h, dgu, f"{tag}_dwgu")
    return _ffn_bwd_in(dgu, wgu, l, x, g, dx, f"{tag}_bwd_in", [dwgu, dwd.reshape(N_DEV, -1, dwd.shape[-1])])


def _aug_rows(rows, H, nb, B):
    a = jnp.stack(rows + [jnp.zeros_like(rows[0])] * (AUG - len(rows)), axis=1)
    return a.reshape(H, AUG, nb, B).transpose(0, 2, 1, 3).astype(BF16)


def _mix_fwd(x, g, w_in, w_conv8, b_f, g_conv, g_att, w_out, l, tag):
    T, D = x.shape
    cw = D // 2
    aw = D - cw
    H = aw // HEAD_DIM
    B = _tile(T, ATT_BLOCK, LANES)
    nb = T // B
    w_c = w_in[:, :3 * cw]
    w_at = w_in[:, 3 * cw:3 * cw + 3 * aw].T
    w_ft = jnp.pad(w_in[:, 3 * cw + 3 * aw:].T, ((0, SUBLANES - H), (0, 0)))
    zc, h = _norm_mm(x, g, w_c, None, F32, f"{tag}_in_conv")
    zat, zft8 = _proj_t(h, w_at, w_ft, B, f"{tag}_in_att")
    zft = zft8[:H]
    bcol = b_f.reshape(H, 1)
    c, chi, cmid, clo = _logf_cumsum(zft, bcol, f"{tag}_cumsum")
    one = jnp.ones_like(c)
    qa = _aug_rows([one, one, one, chi, cmid, clo], H, nb, B)
    ka = _aug_rows([-chi, -cmid, -clo, one, one, one], H, nb, B)
    cb = c.reshape(H, nb, B)
    jmin, imax = _skip_table(cb[:, :, 0:1], cb[:, :, B - 1].reshape(H, 1, nb), _head_norms(zat, f"{tag}_norms"),
                             f"{tag}_skip")
    ot, lse = _attn_fwd(jmin.reshape(H * nb), zat, qa, ka, f"{tag}_attn")
    g_col = g_att.reshape(aw, 1)
    yna = _gnorm_t_fwd(ot, g_col, f"{tag}_gnorm")
    ync = _conv_fwd(zc, w_conv8, g_conv, f"{tag}_conv")
    xo = _mm_res([ync, yna], w_out, l, x, f"{tag}_out")
    saved = (x, g, h, zc, zft, bcol, zat, qa, ka, imax.reshape(H * nb), ot, lse, ync, yna, w_c, w_at, w_ft, g_col)
    return xo, saved


def _mix_bwd(dx, dxb, saved, w_conv8, g_conv, w_out, l, tag):
    x, g, h, zc, zft, bcol, zat, qa, ka, imax, ot, lse, ync, yna, w_c, w_at, w_ft, g_col = saved
    T, D = x.shape
    H = zft.shape[0]
    dw_out = jnp.concatenate([_mm_tn(ync, dxb, f"{tag}_dwout_c"), _mm_tn(yna, dxb, f"{tag}_dwout_a")], axis=0)
    dy = _mm_nt(dxb, w_out, l, F32, f"{tag}_bwd_out")
    dzc, dwc8, dg_conv = _conv_bwd(zc, dy, w_conv8, g_conv, f"{tag}_conv_bwd")
    dot, delta, dg_att = _gnorm_t_bwd(ot, g_col, dy, f"{tag}_gnorm_bwd")
    dq, dqa, dk, dka, dv = _attn_bwd(imax, zat, qa, ka, dot, lse, delta, f"{tag}_attn_bwd")
    dcq = dqa[:, :, 3, :].reshape(H, T)
    dck = dka[:, :, 0, :].reshape(H, T)
    dzft, db = _logf_cumsum_bwd(dcq, dck, zft, bcol, f"{tag}_cumsum_bwd")
    dzft8 = jnp.pad(dzft, ((0, SUBLANES - H), (0, 0)))
    dw_c = _mm_tn(h, dzc, f"{tag}_dwin_c")
    dw_at, dw_ft = _mix_dwt(dq, dk, dv, dzft8, h, f"{tag}_dwin_a")
    dw_in = jnp.concatenate([dw_c, dw_at.T.astype(BF16), dw_ft[:H].T.astype(BF16)], axis=1)
    scatter = [dw_in.reshape(D, N_DEV, -1).transpose(1, 0, 2), dwc8[:3].reshape(3, N_DEV, -1).transpose(1, 0, 2),
               dw_out.reshape(N_DEV, D // N_DEV, D)]
    dx, dxb, dg, p_in, p_conv, p_out = _mix_bwd_in(dzc, dq, dk, dv, dzft8, w_c, w_at, w_ft, x, g, dx, f"{tag}_bwd_in",
                                                   scatter)
    return dx, dxb, dg, p_in, p_conv, db.reshape(H), dg_conv, dg_att[:, 0], p_out


def _xattn_block_fwd(x, g, mem, g_mem, w_q, w_kv, w_o, l, tag):
    q, h = _norm_mm(x, g, w_q, l, BF16, f"{tag}_q")
    kv, hm = _mem_kv(mem, g_mem, w_kv, l, f"{tag}_kv")
    o = _xattn_fwd(q, kv, f"{tag}_attn")
    xo = _mm_res([o], w_o, l, x, f"{tag}_o")
    return xo, (x, g, h, mem, g_mem, hm, q, kv, o)


def _xattn_block_bwd(dx, dxb, saved, w_q, w_kv, w_o, l, tag):
    x, g, h, mem, g_mem, hm, q, kv, o = saved
    dw_o = _mm_tn(o, dxb, f"{tag}_dwo")
    do = _mm_nt(dxb, w_o, l, BF16, f"{tag}_bwd_o")
    dq, dkv = _xattn_bwd(q, kv, do, f"{tag}_attn_bwd")
    dw_q = _mm_tn(h, dq, f"{tag}_dwq")
    dw_kv, dg_mem = _mem_kv_bwd(dkv, hm, w_kv, l, mem, g_mem, f"{tag}_kv_bwd")
    D = dw_q.shape[0]
    scatter = [dw_q.reshape(N_DEV, D // N_DEV, D), dw_kv, dw_o.reshape(N_DEV, D // N_DEV, D)]
    dx, dxb, dg, p_q, p_kv, p_o = _mm_nt_normbwd(dq, w_q, l, x, g, dx, f"{tag}_bwd_q", scatter)
    return dx, dxb, dg, dg_mem, p_q, p_kv, p_o


def _local_step(x, mem, tgt, W):
    L = W['g_ffn1'].shape[0]
    saved = []
    for l in range(L):
        row = lambda n: W[n][l][None, :]
        wc8 = jnp.pad(W['w_conv'][l], ((0, SUBLANES - 3), (0, 0)))
        x, s1 = _ffn_fwd(x, row('g_ffn1'), W['w_ffn1_gu'], W['w_ffn1_down'], l, f"l{l}_ffn1")
        x, s2 = _mix_fwd(x, row('g_mix'), W['w_mix_in'][l], wc8, W['b_f'][l], row('g_conv_out'), W['g_att_out'][l],
                         W['w_mix_out'], l, f"l{l}_mix")
        x, s3 = _xattn_block_fwd(x, row('g_xattn'), mem, row('g_mem'), W['w_xq'], W['w_xkv'], W['w_xo'], l,
                                 f"l{l}_xattn")
        x, s4 = _ffn_fwd(x, row('g_ffn2'), W['w_ffn2_gu'], W['w_ffn2_down'], l, f"l{l}_ffn2")
        saved.append((s1, s2, s3, s4, wc8))
    loss, dx, dxb, dg_final = _loss_head(x, W['g_final'][None, :], tgt, "loss_head")
    G = {n: [None] * L for n in WEIGHT_NAMES if n != 'g_final'}
    for l in reversed(range(L)):
        row = lambda n: W[n][l][None, :]
        s1, s2, s3, s4, wc8 = saved[l]
        dx, dxb, G['g_ffn2'][l], G['w_ffn2_gu'][l], G['w_ffn2_down'][l] = _ffn_bwd(
            dx, dxb, s4, W['w_ffn2_gu'], W['w_ffn2_down'], l, f"l{l}_ffn2")
        dx, dxb, G['g_xattn'][l], G['g_mem'][l], G['w_xq'][l], G['w_xkv'][l], G['w_xo'][l] = _xattn_block_bwd(
            dx, dxb, s3, W['w_xq'], W['w_xkv'], W['w_xo'], l, f"l{l}_xattn")
        (dx, dxb, G['g_mix'][l], G['w_mix_in'][l], G['w_conv'][l], G['b_f'][l], G['g_conv_out'][l], G['g_att_out'][l],
         G['w_mix_out'][l]) = _mix_bwd(dx, dxb, s2, wc8, row('g_conv_out'), W['w_mix_out'], l, f"l{l}_mix")
        dx, dxb, G['g_ffn1'][l], G['w_ffn1_gu'][l], G['w_ffn1_down'][l] = _ffn_bwd(
            dx, dxb, s1, W['w_ffn1_gu'], W['w_ffn1_down'], l, f"l{l}_ffn1")
    grads = {n: (G[n] if n in SHARDED else jnp.stack([a.reshape(W[n].shape[1:]) for a in G[n]])) for n in G}
    grads['g_final'] = dg_final.reshape(-1)
    return loss, dx, grads


def kernel(x, mem, g_ffn1, w_ffn1_gu, w_ffn1_down, g_mix, w_mix_in, w_conv, b_f, g_conv_out, g_att_out, w_mix_out, g_xattn, g_mem, w_xq, w_xkv, w_xo, g_ffn2, w_ffn2_gu, w_ffn2_down, g_final, loss_target, m_g_ffn1, m_w_ffn1_gu, m_w_ffn1_down, m_g_mix, m_w_mix_in, m_w_conv, m_b_f, m_g_conv_out, m_g_att_out, m_w_mix_out, m_g_xattn, m_g_mem, m_w_xq, m_w_xkv, m_w_xo, m_g_ffn2, m_w_ffn2_gu, m_w_ffn2_down, m_g_final, v_g_ffn1, v_w_ffn1_gu, v_w_ffn1_down, v_g_mix, v_w_mix_in, v_w_conv, v_b_f, v_g_conv_out, v_g_att_out, v_w_mix_out, v_g_xattn, v_g_mem, v_w_xq, v_w_xkv, v_w_xo, v_g_ffn2, v_w_ffn2_gu, v_w_ffn2_down, v_g_final):
    args = (x, mem, g_ffn1, w_ffn1_gu, w_ffn1_down, g_mix, w_mix_in, w_conv, b_f, g_conv_out, g_att_out, w_mix_out,
            g_xattn, g_mem, w_xq, w_xkv, w_xo, g_ffn2, w_ffn2_gu, w_ffn2_down, g_final)
    P = dict(zip(IN_NAMES, args))
    moms = (m_g_ffn1, m_w_ffn1_gu, m_w_ffn1_down, m_g_mix, m_w_mix_in, m_w_conv, m_b_f, m_g_conv_out, m_g_att_out,
            m_w_mix_out, m_g_xattn, m_g_mem, m_w_xq, m_w_xkv, m_w_xo, m_g_ffn2, m_w_ffn2_gu, m_w_ffn2_down, m_g_final)
    vars_ = (v_g_ffn1, v_w_ffn1_gu, v_w_ffn1_down, v_g_mix, v_w_mix_in, v_w_conv, v_b_f, v_g_conv_out, v_g_att_out,
             v_w_mix_out, v_g_xattn, v_g_mem, v_w_xq, v_w_xkv, v_w_xo, v_g_ffn2, v_w_ffn2_gu, v_w_ffn2_down, v_g_final)
    MOM = dict(zip(WEIGHT_NAMES, moms))
    VAR = dict(zip(WEIGHT_NAMES, vars_))

    gathered = _all_gather([P[n] if n == 'w_conv' else P[n].astype(BF16) for n in SHARDED], "gather_weights")
    W = dict(zip(SHARDED, gathered))
    for n in ('w_mix_in', 'w_conv'):
        _, L, a, b = W[n].shape
        W[n] = W[n].transpose(1, 2, 0, 3).reshape(L, a, N_DEV * b)
    for n in REPLICATED:
        W[n] = P[n]

    loss_part, dx, grads = _local_step(x[0], mem[0], loss_target[0], W)

    small_shapes = [P[n].shape for n in REPLICATED] + [(1,)]
    small = _pack([grads[n].reshape(-1) for n in REPLICATED] + [loss_part[0, :1]], SUBLANES)
    small_parts = _all_gather([small], "gather_small_grads")[0]

    res = {n: _sum_adamw(grads[n], P[n], MOM[n], VAR[n], "adamw_" + n) for n in SHARDED}
    zero = [jnp.zeros((1,), F32)]
    w_, m_, v_ = (_pack([d[n].reshape(-1) for n in REPLICATED] + zero, SUBLANES)[None] for d in (P, MOM, VAR))
    small_out = [_unpack(o[0], small_shapes) for o in _sum_adamw([small_parts], w_, m_, v_, "adamw_vectors")]
    for k, n in enumerate(REPLICATED):
        res[n] = [o[k] for o in small_out]
    out = []
    for k in range(4):
        out += [res[n][k] for n in WEIGHT_NAMES]
    return (small_out[0][-1].reshape(()), dx[None], *out)
```

```python
import jax
import jax.numpy as jnp
from jax import lax
from jax.experimental import pallas as pl
from jax.experimental.pallas import tpu as pltpu

F32 = jnp.float32
BF16 = jnp.bfloat16
I32 = jnp.int32
SDS = jax.ShapeDtypeStruct

EPS = 1e-6
HEAD_DIM = 64
N_XHEADS = 4
N_DEV = 8
LANES = 128
SUBLANES = 8
AUG = 16
ATT_BLOCK = 512
NEG = -1e30
SKIP_MARGIN = 115.0
PACK_COLS = 1024
VMEM_LIMIT = 58 * 1024 * 1024

ADAM_LR = 0.001
ADAM_B1 = 0.9
ADAM_B2 = 0.999
ADAM_EPS = 1e-08
ADAM_WD = 0.01
ADAM_STEP = 10

IN_NAMES = ['x', 'mem', 'g_ffn1', 'w_ffn1_gu', 'w_ffn1_down', 'g_mix', 'w_mix_in', 'w_conv', 'b_f', 'g_conv_out',
            'g_att_out', 'w_mix_out', 'g_xattn', 'g_mem', 'w_xq', 'w_xkv', 'w_xo', 'g_ffn2', 'w_ffn2_gu',
            'w_ffn2_down', 'g_final']
WEIGHT_NAMES = IN_NAMES[2:]
SHARDED = ['w_ffn1_gu', 'w_ffn1_down', 'w_mix_in', 'w_conv', 'w_mix_out', 'w_xq', 'w_xkv', 'w_xo', 'w_ffn2_gu',
           'w_ffn2_down']
REPLICATED = [n for n in WEIGHT_NAMES if n not in SHARDED]


def _tile(n, pref, mult):
    t = min(pref, n) // mult * mult
    while t >= mult:
        if n % t == 0:
            return t
        t -= mult
    return n


def _cp(*sem):
    return pltpu.CompilerParams(dimension_semantics=sem, vmem_limit_bytes=VMEM_LIMIT)


def _nt(a, b):
    return lax.dot_general(a, b, (((1,), (1,)), ((), ())), preferred_element_type=F32)


def _tn(a, b):
    return lax.dot_general(a, b, (((0,), (0,)), ((), ())), preferred_element_type=F32)


def _nn(a, b):
    return jnp.dot(a, b, preferred_element_type=F32)


def _rstd(xv):
    return lax.rsqrt(jnp.mean(xv * xv, axis=-1, keepdims=True) + EPS)


def _full(a):
    nd = a.ndim
    return pl.BlockSpec(a.shape, lambda *_: (0,) * nd)


def _rows2d(w_ref):
    s, a, b = w_ref.shape
    return w_ref[...].reshape(s * a, b)


def _accumulate(ref, first, part):
    @pl.when(first)
    def _():
        ref[...] = part

    @pl.when(jnp.logical_not(first))
    def _():
        ref[...] += part


def _norm_bwd_store(dh, x_ref, g_ref, r_ref, dx_ref, dxb_ref, dg_ref, first):
    xv = x_ref[...]
    r = _rstd(xv)
    xh = xv * r
    dxh = dh * g_ref[...]
    dx = r * (dxh - xh * jnp.mean(dxh * xh, axis=-1, keepdims=True)) + r_ref[...]
    dx_ref[...] = dx
    dxb_ref[...] = dx.astype(BF16)
    _accumulate(dg_ref, first, jnp.sum(dh * xh, axis=0, keepdims=True))


def _norm_mm(x, g, w, out_dtype, name, tm=512):
    T, D = x.shape
    N = w.shape[-1]
    tm = _tile(T, tm, SUBLANES)

    def body(x_ref, g_ref, w_ref, o_ref, h_ref):
        xv = x_ref[...]
        h = (xv * _rstd(xv) * g_ref[...]).astype(BF16)
        h_ref[...] = h
        o_ref[...] = _nn(h, w_ref[...] if w.ndim == 2 else _rows2d(w_ref)).astype(o_ref.dtype)

    row = lambda n: pl.BlockSpec((tm, n), lambda i: (i, 0))
    return pl.pallas_call(
        body, name=name, grid=(T // tm,),
        in_specs=[row(D), _full(g), _full(w)], out_specs=[row(N), row(D)],
        out_shape=[SDS((T, N), out_dtype), SDS((T, D), BF16)], compiler_params=_cp("parallel"))(x, g, w)


def _mm_res(a_list, w, res, name, tm=512):
    T = res.shape[0]
    N = w.shape[-1]
    tm = _tile(T, tm, SUBLANES)
    n_a = len(a_list)

    def body(*refs):
        a_refs = refs[:n_a]
        w_ref, r_ref, o_ref = refs[n_a:]
        wv = _rows2d(w_ref)
        acc = r_ref[...]
        off = 0
        for a_ref in a_refs:
            k = a_ref.shape[1]
            acc = acc + _nn(a_ref[...], wv[off:off + k, :])
            off += k
        o_ref[...] = acc

    row = lambda n: pl.BlockSpec((tm, n), lambda i: (i, 0))
    return pl.pallas_call(
        body, name=name, grid=(T // tm,),
        in_specs=[row(a.shape[1]) for a in a_list] + [_full(w), row(N)], out_specs=row(N),
        out_shape=SDS((T, N), F32), compiler_params=_cp("parallel"))(*a_list, w, res)


def _mm_nt(a, w, out_dtype, name, tm=512):
    T = a.shape[0]
    K = N_DEV * w.shape[1]
    tm = _tile(T, tm, SUBLANES)

    def body(a_ref, w_ref, o_ref):
        o_ref[...] = _nt(a_ref[...], _rows2d(w_ref)).astype(o_ref.dtype)

    return pl.pallas_call(
        body, name=name, grid=(T // tm,),
        in_specs=[pl.BlockSpec((tm, a.shape[1]), lambda i: (i, 0)), _full(w)],
        out_specs=pl.BlockSpec((tm, K), lambda i: (i, 0)),
        out_shape=SDS((T, K), out_dtype), compiler_params=_cp("parallel"))(a, w)


def _mm_tn(a, b, name, out_dtype=BF16, tt=2048):
    T, K = a.shape
    N = b.shape[1]
    tt = _tile(T, tt, 2 * SUBLANES)
    nt = T // tt

    def body(a_ref, b_ref, o_ref, acc_ref):
        t = pl.program_id(0)
        _accumulate(acc_ref, t == 0, _tn(a_ref[...], b_ref[...]))

        @pl.when(t == nt - 1)
        def _():
            o_ref[...] = acc_ref[...].astype(o_ref.dtype)

    return pl.pallas_call(
        body, name=name, grid=(nt,),
        in_specs=[pl.BlockSpec((tt, K), lambda t: (t, 0)), pl.BlockSpec((tt, N), lambda t: (t, 0))],
        out_specs=pl.BlockSpec((K, N), lambda t: (0, 0)), out_shape=SDS((K, N), out_dtype),
        scratch_shapes=[pltpu.VMEM((K, N), F32)], compiler_params=_cp("arbitrary"))(a, b)


def _hosted_call(body, name, n_steps, in_specs, out_specs, out_shape, args, scatter=(), gather=()):
    scatter, gather = list(scatter), list(gather)
    n_s, n_g = len(scatter), len(gather)
    if n_s + n_g == 0:
        return pl.pallas_call(body, name=name, grid=(n_steps,), in_specs=list(in_specs), out_specs=list(out_specs),
                              out_shape=list(out_shape), compiler_params=_cp("arbitrary"))(*args)
    n_in, n_out = len(in_specs), len(out_specs)
    relay_step = min(max(1, 3 * n_steps // 4), n_steps - 1)

    def wrapped(*refs):
        ins, refs = refs[:n_in], refs[n_in:]
        xs, shards, refs = refs[:n_s], refs[n_s:n_s + n_g], refs[n_s + n_g:]
        outs, refs = refs[:n_out], refs[n_out:]
        recvs, gathered, sems = refs[:n_s], refs[n_s:n_s + n_g], refs[n_s + n_g:]
        s_sems, g_sems = (sems[:3], sems[3:]) if n_s else ((), sems)
        step = pl.program_id(0)

        @pl.when(step == 0)
        def _():
            if n_s:
                _scatter_start(xs, recvs, *s_sems)
            if n_g:
                _gather_start(shards, gathered, *g_sems)
        body(*ins, *outs)
        if n_g:
            @pl.when(step == relay_step)
            def _():
                _gather_relay(shards, gathered, *g_sems)

        @pl.when(step == n_steps - 1)
        def _():
            if n_s:
                _scatter_wait(xs, recvs, *s_sems)
            if n_g:
                _gather_finish(shards, gathered, *g_sems)

    any_spec = pl.BlockSpec(memory_space=pl.ANY)
    sems = []
    for n in (n_s, n_g):
        if n:
            sems += [pltpu.SemaphoreType.DMA((n, N_DEV - 1)), pltpu.SemaphoreType.DMA((n, N_DEV - 1)),
                     pltpu.SemaphoreType.DMA((n,))]
    return pl.pallas_call(
        wrapped, name=name, grid=(n_steps,), in_specs=list(in_specs) + [any_spec] * (n_s + n_g),
        out_specs=list(out_specs) + [any_spec] * (n_s + n_g),
        out_shape=list(out_shape) + [SDS(b.shape, b.dtype) for b in scatter]
        + [SDS((N_DEV,) + b.shape, b.dtype) for b in gather],
        scratch_shapes=sems, compiler_params=_cp("arbitrary"))(*args, *scatter, *gather)


def _mm_nt_normbwd(a, w, x, g, dres, name, scatter, tm=512):
    T, D = x.shape
    tm = _tile(T, tm, SUBLANES)

    def body(a_ref, w_ref, x_ref, g_ref, r_ref, dx_ref, dxb_ref, dg_ref):
        dh = _nt(a_ref[...], _rows2d(w_ref))
        _norm_bwd_store(dh, x_ref, g_ref, r_ref, dx_ref, dxb_ref, dg_ref, pl.program_id(0) == 0)

    row = lambda n: pl.BlockSpec((tm, n), lambda i: (i, 0))
    return _hosted_call(
        body, name, T // tm, [row(a.shape[1]), _full(w), row(D), _full(g), row(D)],
        [row(D), row(D), pl.BlockSpec((1, D), lambda i: (0, 0))],
        [SDS((T, D), F32), SDS((T, D), BF16), SDS((1, D), F32)], (a, w, x, g, dres), scatter)


def _ffn_gu(x, g, wgu, name, gather=(), tm=512):
    T, D = x.shape
    fb = wgu.shape[2]
    nh = N_DEV // 2
    tm = _tile(T, tm, SUBLANES)

    def body(x_ref, g_ref, w_ref, ab_ref, act_ref, h_ref):
        xv = x_ref[...]
        h = (xv * _rstd(xv) * g_ref[...]).astype(BF16)
        h_ref[...] = h
        for d in range(nh):
            gt = _nn(h, w_ref[d])
            up = _nn(h, w_ref[d + nh])
            sg = jax.nn.sigmoid(gt)
            silu = gt * sg
            ab_ref[d] = (0.5 * up * (sg * (1.0 + gt * (1.0 - sg)))).astype(BF16)
            ab_ref[d + nh] = (0.5 * silu).astype(BF16)
            act_ref[d] = (silu * up).astype(BF16)

    row = pl.BlockSpec((tm, D), lambda i: (i, 0))
    blocks = lambda n: pl.BlockSpec((n, tm, fb), lambda i: (0, i, 0))
    return _hosted_call(
        body, name, T // tm, [row, _full(g), _full(wgu)], [blocks(N_DEV), blocks(nh), row],
        [SDS((N_DEV, T, fb), BF16), SDS((nh, T, fb), BF16), SDS((T, D), BF16)], (x, g, wgu), gather=gather)


def _ffn_down(act, wd, res, name, gather=(), tm=1024):
    nh, T, fb = act.shape
    D = wd.shape[2]
    tm = _tile(T, tm, SUBLANES)

    def body(a_ref, w_ref, r_ref, o_ref):
        wv = w_ref[...].reshape(nh, fb, D)
        acc = _nn(a_ref[0], wv[0])
        for d in range(1, nh):
            acc = acc + _nn(a_ref[d], wv[d])
        o_ref[...] = r_ref[...] + 0.5 * acc

    row = pl.BlockSpec((tm, D), lambda i: (i, 0))
    return _hosted_call(
        body, name, T // tm, [pl.BlockSpec((nh, tm, fb), lambda i: (0, i, 0)), _full(wd), row], [row],
        [SDS((T, D), F32)], (act, wd, res), gather=gather)


def _ffn_bwd_act(dyb, wd, ab, name, tm=512):
    T, D = dyb.shape
    _, _, fb = ab.shape
    nh = N_DEV // 2
    tm = _tile(T, tm, SUBLANES)

    def body(dy_ref, w_ref, ab_ref, o_ref):
        wv = w_ref[...].reshape(nh, fb, D)
        dy = dy_ref[...]
        for d in range(nh):
            da = _nt(dy, wv[d])
            o_ref[d] = (da * ab_ref[d].astype(F32)).astype(BF16)
            o_ref[d + nh] = (da * ab_ref[d + nh].astype(F32)).astype(BF16)

    blocks = pl.BlockSpec((N_DEV, tm, fb), lambda i: (0, i, 0))
    return pl.pallas_call(
        body, name=name, grid=(T // tm,),
        in_specs=[pl.BlockSpec((tm, D), lambda i: (i, 0)), _full(wd), blocks], out_specs=blocks,
        out_shape=SDS((N_DEV, T, fb), BF16), compiler_params=_cp("parallel"))(dyb, wd, ab)


def _ffn_dwd(act, dyb, name, tt=2048):
    nh, T, fb = act.shape
    D = dyb.shape[1]
    tt = _tile(T, tt, 2 * SUBLANES)
    nt = T // tt

    def body(a_ref, b_ref, o_ref, acc_ref):
        t = pl.program_id(1)

        @pl.when(t == 0)
        def _():
            acc_ref[...] = jnp.zeros_like(acc_ref)
        for s in range(2):
            acc_ref[s] += _tn(a_ref[s], b_ref[...])

        @pl.when(t == nt - 1)
        def _():
            o_ref[...] = (0.5 * acc_ref[...]).astype(BF16)

    return pl.pallas_call(
        body, name=name, grid=(nh // 2, nt),
        in_specs=[pl.BlockSpec((2, tt, fb), lambda j, t: (j, t, 0)), pl.BlockSpec((tt, D), lambda j, t: (t, 0))],
        out_specs=pl.BlockSpec((2, fb, D), lambda j, t: (j, 0, 0)), out_shape=SDS((nh, fb, D), BF16),
        scratch_shapes=[pltpu.VMEM((2, fb, D), F32)], compiler_params=_cp("parallel", "arbitrary"))(act, dyb)


def _ffn_dwgu(h, dgu, name, tt=2048):
    T, D = h.shape
    _, _, fb = dgu.shape
    tt = _tile(T, tt, 2 * SUBLANES)
    nt = T // tt

    def body(a_ref, b_ref, o_ref, acc_ref):
        t = pl.program_id(1)

        @pl.when(t == 0)
        def _():
            acc_ref[...] = jnp.zeros_like(acc_ref)
        for s in range(2):
            acc_ref[s] += _tn(a_ref[...], b_ref[s])

        @pl.when(t == nt - 1)
        def _():
            o_ref[...] = acc_ref[...].astype(BF16)

    return pl.pallas_call(
        body, name=name, grid=(N_DEV // 2, nt),
        in_specs=[pl.BlockSpec((tt, D), lambda j, t: (t, 0)), pl.BlockSpec((2, tt, fb), lambda j, t: (j, t, 0))],
        out_specs=pl.BlockSpec((2, D, fb), lambda j, t: (j, 0, 0)), out_shape=SDS((N_DEV, D, fb), BF16),
        scratch_shapes=[pltpu.VMEM((2, D, fb), F32)], compiler_params=_cp("parallel", "arbitrary"))(h, dgu)


def _ffn_bwd_in(dgu, wgu, x, g, dres, name, scatter, tm=512):
    T, D = x.shape
    _, _, fb = dgu.shape
    tm = _tile(T, tm, SUBLANES)

    def body(a_ref, w_ref, x_ref, g_ref, r_ref, dx_ref, dxb_ref, dg_ref):
        dh = _nt(a_ref[0], w_ref[0])
        for d in range(1, N_DEV):
            dh = dh + _nt(a_ref[d], w_ref[d])
        _norm_bwd_store(dh, x_ref, g_ref, r_ref, dx_ref, dxb_ref, dg_ref, pl.program_id(0) == 0)

    row = pl.BlockSpec((tm, D), lambda i: (i, 0))
    return _hosted_call(
        body, name, T // tm, [pl.BlockSpec((N_DEV, tm, fb), lambda i: (0, i, 0)), _full(wgu), row, _full(g), row],
        [row, row, pl.BlockSpec((1, D), lambda i: (0, 0))],
        [SDS((T, D), F32), SDS((T, D), BF16), SDS((1, D), F32)], (dgu, wgu, x, g, dres), scatter)


def _loss_head(x, g, tgt, name, tm=512):
    T, D = x.shape
    tm = _tile(T, tm, SUBLANES)

    def body(x_ref, g_ref, t_ref, loss_ref, dx_ref, dxb_ref, dg_ref):
        first = pl.program_id(0) == 0
        xv = x_ref[...]
        r = _rstd(xv)
        xh = xv * r
        e = xh * g_ref[...] - t_ref[...]
        part = 0.5 * jnp.sum(jnp.mean(e * e, axis=-1, keepdims=True), axis=0, keepdims=True)
        dy = e * (1.0 / D)
        dxh = dy * g_ref[...]
        dx = r * (dxh - xh * jnp.mean(dxh * xh, axis=-1, keepdims=True))
        dx_ref[...] = dx
        dxb_ref[...] = dx.astype(BF16)
        _accumulate(loss_ref, first, jnp.broadcast_to(part, loss_ref.shape))
        _accumulate(dg_ref, first, jnp.sum(dy * xh, axis=0, keepdims=True))

    row = pl.BlockSpec((tm, D), lambda i: (i, 0))
    vec = pl.BlockSpec((1, D), lambda i: (0, 0))
    return pl.pallas_call(
        body, name=name, grid=(T // tm,), in_specs=[row, vec, row],
        out_specs=[pl.BlockSpec((SUBLANES, LANES), lambda i: (0, 0)), row, row, vec],
        out_shape=[SDS((SUBLANES, LANES), F32), SDS((T, D), F32), SDS((T, D), BF16), SDS((1, D), F32)],
        compiler_params=_cp("arbitrary"))(x, g, tgt)


def _shift_down(u, k, prev_row):
    rows = lax.broadcasted_iota(I32, u.shape, 0)
    s = pltpu.roll(u, k, 0)
    for t in range(k):
        s = jnp.where(rows == t, prev_row(SUBLANES - k + t), s)
    return s


def _shift_up(u, k, next_row):
    n = u.shape[0]
    rows = lax.broadcasted_iota(I32, u.shape, 0)
    s = pltpu.roll(u, n - k, 0)
    for t in range(k):
        s = jnp.where(rows == n - k + t, next_row(t), s)
    return s


def _conv_taps(z_ref, zp_ref, w_ref, cw, first):
    u = z_ref[:, cw:2 * cw] * z_ref[:, 2 * cw:]

    def prev_row(r):
        return jnp.where(first, 0.0, zp_ref[r:r + 1, cw:2 * cw] * zp_ref[r:r + 1, 2 * cw:])

    u1 = _shift_down(u, 1, prev_row)
    u2 = _shift_down(u, 2, prev_row)
    cv = w_ref[0:1, :] * u2 + w_ref[1:2, :] * u1 + w_ref[2:3, :] * u
    return u, u1, u2, cv


def _conv_fwd(zc, wc, gc, name, tm=512):
    T = zc.shape[0]
    cw = zc.shape[1] // 3
    tm = _tile(T, tm, SUBLANES)
    hb = tm // SUBLANES

    def body(z_ref, zp_ref, w_ref, g_ref, o_ref):
        _, _, _, cv = _conv_taps(z_ref, zp_ref, w_ref, cw, pl.program_id(0) == 0)
        y = z_ref[:, :cw] * cv
        o_ref[...] = (y * _rstd(y) * g_ref[...]).astype(BF16)

    return pl.pallas_call(
        body, name=name, grid=(T // tm,),
        in_specs=[pl.BlockSpec((tm, 3 * cw), lambda i: (i, 0)),
                  pl.BlockSpec((SUBLANES, 3 * cw), lambda i: (jnp.maximum(i * hb - 1, 0), 0)), _full(wc), _full(gc)],
        out_specs=pl.BlockSpec((tm, cw), lambda i: (i, 0)), out_shape=SDS((T, cw), BF16),
        compiler_params=_cp("parallel"))(zc, zc, wc, gc)


def _conv_bwd(zc, dy, wc, gc, name, tm=512):
    T = zc.shape[0]
    cw = zc.shape[1] // 3
    tm = _tile(T, tm, SUBLANES)
    hb = tm // SUBLANES
    nb = T // tm

    def body(z_ref, zp_ref, dy_ref, w_ref, g_ref, dz_ref, dw_ref, dg_ref, carry_ref):
        s = pl.program_id(0)
        u, u1, u2, cv = _conv_taps(z_ref, zp_ref, w_ref, cw, s == nb - 1)
        zb = z_ref[:, :cw]
        y = zb * cv
        r = _rstd(y)
        yh = y * r
        dyn = dy_ref[...]
        dyh = dyn * g_ref[...]
        dyc = r * (dyh - yh * jnp.mean(dyh * yh, axis=-1, keepdims=True))
        dcv = dyc * zb

        def next_row(t):
            return jnp.where(s == 0, 0.0, carry_ref[t:t + 1, :])

        du = w_ref[2:3, :] * dcv + w_ref[1:2, :] * _shift_up(dcv, 1, next_row) + w_ref[0:1, :] * _shift_up(dcv, 2, next_row)
        carry_ref[...] = dcv[0:SUBLANES, :]
        dz_ref[:, :cw] = (dyc * cv).astype(BF16)
        dz_ref[:, cw:2 * cw] = (du * z_ref[:, 2 * cw:]).astype(BF16)
        dz_ref[:, 2 * cw:] = (du * z_ref[:, cw:2 * cw]).astype(BF16)
        tap = lax.broadcasted_iota(I32, (SUBLANES, cw), 0)
        dwp = jnp.where(tap == 0, jnp.sum(dcv * u2, axis=0, keepdims=True),
                        jnp.where(tap == 1, jnp.sum(dcv * u1, axis=0, keepdims=True),
                                  jnp.where(tap == 2, jnp.sum(dcv * u, axis=0, keepdims=True), 0.0)))
        _accumulate(dw_ref, s == 0, dwp)
        _accumulate(dg_ref, s == 0, jnp.sum(dyn * yh, axis=0, keepdims=True))

    rev = lambda s: nb - 1 - s
    return pl.pallas_call(
        body, name=name, grid=(nb,),
        in_specs=[pl.BlockSpec((tm, 3 * cw), lambda s: (rev(s), 0)),
                  pl.BlockSpec((SUBLANES, 3 * cw), lambda s: (jnp.maximum(rev(s) * hb - 1, 0), 0)),
                  pl.BlockSpec((tm, cw), lambda s: (rev(s), 0)), _full(wc), _full(gc)],
        out_specs=[pl.BlockSpec((tm, 3 * cw), lambda s: (rev(s), 0)), pl.BlockSpec((SUBLANES, cw), lambda s: (0, 0)),
                   pl.BlockSpec((1, cw), lambda s: (0, 0))],
        out_shape=[SDS((T, 3 * cw), BF16), SDS((SUBLANES, cw), F32), SDS((1, cw), F32)],
        scratch_shapes=[pltpu.VMEM((SUBLANES, cw), F32)],
        compiler_params=_cp("arbitrary"))(zc, zc, dy, wc, gc)


def _proj_t(h, wat, wft, B, name):
    T, D = h.shape
    R = wat.shape[0]
    nb = T // B

    def body(h_ref, wa_ref, wf_ref, za_ref, zf_ref):
        hv = h_ref[...]
        za_ref[...] = _nt(wa_ref[...], hv).astype(BF16)
        zf_ref[...] = _nt(wf_ref[...], hv)

    return pl.pallas_call(
        body, name=name, grid=(nb,), in_specs=[pl.BlockSpec((B, D), lambda i: (i, 0)), _full(wat), _full(wft)],
        out_specs=[pl.BlockSpec((None, R, B), lambda i: (i, 0, 0)), pl.BlockSpec((SUBLANES, B), lambda i: (0, i))],
        out_shape=[SDS((nb, R, B), BF16), SDS((SUBLANES, T), F32)], compiler_params=_cp("parallel"))(h, wat, wft)


def _split3(v):
    hi = v.astype(BF16).astype(F32)
    r1 = v - hi
    mid = r1.astype(BF16).astype(F32)
    lo = (r1 - mid).astype(BF16).astype(F32)
    return hi, mid, lo


def _tri_dot3(v, tri):
    hi, mid, lo = _split3(v)
    return _nn(hi.astype(BF16), tri) + _nn(mid.astype(BF16), tri) + _nn(lo.astype(BF16), tri)


def _logf_cumsum(zft, b, name, tb=512):
    H, T = zft.shape
    tb = _tile(T, tb, LANES)

    def body(z_ref, b_ref, c_ref, hi_ref, mid_ref, lo_ref, carry_ref):
        @pl.when(pl.program_id(0) == 0)
        def _():
            carry_ref[...] = jnp.zeros_like(carry_ref)
        xv = z_ref[...] + b_ref[...]
        lf = jnp.minimum(xv, 0.0) - jnp.log(1.0 + jnp.exp(-jnp.abs(xv)))
        src = lax.broadcasted_iota(I32, (tb, tb), 0)
        dst = lax.broadcasted_iota(I32, (tb, tb), 1)
        tri = jnp.where(src <= dst, 1.0, 0.0).astype(BF16)
        cs = _tri_dot3(lf, tri) + carry_ref[...]
        c_ref[...] = cs
        hi, mid, lo = _split3(cs)
        hi_ref[...] = hi
        mid_ref[...] = mid
        lo_ref[...] = lo
        col = lax.broadcasted_iota(I32, cs.shape, 1)
        carry_ref[...] = jnp.sum(jnp.where(col == tb - 1, cs, 0.0), axis=1, keepdims=True)

    blk = pl.BlockSpec((H, tb), lambda i: (0, i))
    return pl.pallas_call(
        body, name=name, grid=(T // tb,), in_specs=[blk, _full(b)],
        out_specs=[blk, blk, blk, blk], out_shape=[SDS((H, T), F32)] * 4,
        scratch_shapes=[pltpu.VMEM((H, 1), F32)], compiler_params=_cp("arbitrary"))(zft, b)


def _logf_cumsum_bwd(dcq, dck, zft, b, name, tb=512):
    H, T = zft.shape
    tb = _tile(T, tb, LANES)
    nb = T // tb

    def body(dq_ref, dk_ref, z_ref, b_ref, o_ref, db_ref, carry_ref):
        s = pl.program_id(0)

        @pl.when(s == 0)
        def _():
            carry_ref[...] = jnp.zeros_like(carry_ref)
        dc = dq_ref[...] - dk_ref[...]
        src = lax.broadcasted_iota(I32, (tb, tb), 0)
        dst = lax.broadcasted_iota(I32, (tb, tb), 1)
        tri = jnp.where(src >= dst, 1.0, 0.0).astype(BF16)
        dl = _tri_dot3(dc, tri) + carry_ref[...]
        col = lax.broadcasted_iota(I32, dl.shape, 1)
        carry_ref[...] = jnp.sum(jnp.where(col == 0, dl, 0.0), axis=1, keepdims=True)
        dz = dl * jax.nn.sigmoid(-(z_ref[...] + b_ref[...]))
        o_ref[...] = dz
        _accumulate(db_ref, s == 0, jnp.sum(dz, axis=1, keepdims=True))

    blk = pl.BlockSpec((H, tb), lambda s: (0, nb - 1 - s))
    vec = pl.BlockSpec((H, 1), lambda s: (0, 0))
    return pl.pallas_call(
        body, name=name, grid=(nb,), in_specs=[blk, blk, blk, vec], out_specs=[blk, vec],
        out_shape=[SDS((H, T), F32), SDS((H, 1), F32)],
        scratch_shapes=[pltpu.VMEM((H, 1), F32)], compiler_params=_cp("arbitrary"))(dcq, dck, zft, b)


def _head_norms(zat, name):
    nb, R, B = zat.shape
    aw = R // 3
    g = 2 * aw // HEAD_DIM

    def body(z_ref, o_ref):
        zv = z_ref[...].astype(F32)
        ss = jnp.sum((zv * zv).reshape(g, HEAD_DIM, B), axis=1)
        mx = jnp.broadcast_to(jnp.max(ss, axis=1, keepdims=True), o_ref.shape)

        @pl.when(pl.program_id(0) == 0)
        def _():
            o_ref[...] = mx

        @pl.when(pl.program_id(0) > 0)
        def _():
            o_ref[...] = jnp.maximum(o_ref[...], mx)

    return pl.pallas_call(
        body, name=name, grid=(nb,), in_specs=[pl.BlockSpec((None, 2 * aw, B), lambda i: (i, 0, 0))],
        out_specs=pl.BlockSpec((g, LANES), lambda i: (0, 0)), out_shape=SDS((g, LANES), F32),
        compiler_params=_cp("arbitrary"))(zat)


def _skip_table(cs_col, ce_row, norms, name):
    H, nb, _ = cs_col.shape

    def body(cs_ref, ce_ref, n_ref, jm_ref, im_ref):
        h = pl.program_id(0)
        nq = n_ref[pl.ds(h, 1), 0:1]
        nk = n_ref[pl.ds(H + h, 1), 0:1]
        bound = 2.0 * jnp.sqrt(nq * nk) * 0.125
        skip = jnp.where(bound + cs_ref[...] - ce_ref[...] <= -SKIP_MARGIN, 1, 0).astype(I32)
        jm_ref[...] = jnp.sum(skip, axis=1, keepdims=True)
        im_ref[...] = nb - 1 - jnp.sum(skip, axis=0, keepdims=True)

    col = pl.BlockSpec((None, nb, 1), lambda h: (h, 0, 0))
    row = pl.BlockSpec((None, 1, nb), lambda h: (h, 0, 0))
    return pl.pallas_call(
        body, name=name, grid=(H,), in_specs=[col, row, _full(norms)], out_specs=[col, row],
        out_shape=[SDS((H, nb, 1), I32), SDS((H, 1, nb), I32)], compiler_params=_cp("parallel"))(cs_col, ce_row, norms)


def _causal_mask(B):
    krow = lax.broadcasted_iota(I32, (B, B), 0)
    qcol = lax.broadcasted_iota(I32, (B, B), 1)
    return krow <= qcol


def _attn_fwd(jmin, zat, qa, ka, name):
    nb, R, B = zat.shape
    aw = R // 3
    H = aw // HEAD_DIM
    hd = HEAD_DIM

    def body(jm_ref, q_ref, k_ref, v_ref, qa_ref, ka_ref, o_ref, lse_ref):
        h = pl.program_id(0)
        i = pl.program_id(1)
        qt = jnp.concatenate([q_ref[...] * 0.125, qa_ref[...]], axis=0)

        def block(j, carry, masked):
            m, l, acc = carry
            s = _tn(jnp.concatenate([k_ref[j], ka_ref[j]], axis=0), qt)
            if masked:
                s = jnp.where(_causal_mask(B), s, NEG)
            m_new = jnp.maximum(m, jnp.max(s, axis=0, keepdims=True))
            p = jnp.exp(s - m_new)
            a = jnp.exp(m - m_new)
            l = a * l + jnp.sum(p, axis=0, keepdims=True)
            acc = a * acc + _nn(v_ref[j], p.astype(BF16))
            return m_new, l, acc

        init = (jnp.full((1, B), NEG, F32), jnp.zeros((1, B), F32), jnp.zeros((hd, B), F32))
        carry = lax.fori_loop(jm_ref[h * nb + i], i, lambda j, c: block(j, c, False), init)
        m, l, acc = block(i, carry, True)
        o_ref[...] = acc / l
        lse_ref[...] = m + jnp.log(l)

    grid_spec = pltpu.PrefetchScalarGridSpec(
        num_scalar_prefetch=1, grid=(H, nb),
        in_specs=[pl.BlockSpec((None, hd, B), lambda h, i, jm: (i, h, 0)),
                  pl.BlockSpec((nb, hd, B), lambda h, i, jm: (0, H + h, 0)),
                  pl.BlockSpec((nb, hd, B), lambda h, i, jm: (0, 2 * H + h, 0)),
                  pl.BlockSpec((None, None, AUG, B), lambda h, i, jm: (h, i, 0, 0)),
                  pl.BlockSpec((None, nb, AUG, B), lambda h, i, jm: (h, 0, 0, 0))],
        out_specs=[pl.BlockSpec((None, hd, B), lambda h, i, jm: (i, h, 0)),
                   pl.BlockSpec((None, None, 1, B), lambda h, i, jm: (h, i, 0, 0))])
    return pl.pallas_call(
        body, name=name, grid_spec=grid_spec,
        out_shape=[SDS((nb, aw, B), F32), SDS((H, nb, 1, B), F32)],
        compiler_params=_cp("parallel", "parallel"))(jmin, zat, zat, zat, qa, ka)


def _gnorm_t_fwd(ot, g_col, name):
    nb, aw, B = ot.shape

    def body(o_ref, g_ref, y_ref):
        ov = o_ref[...]
        r = lax.rsqrt(jnp.mean(ov * ov, axis=0, keepdims=True) + EPS)
        y_ref[...] = (ov * r * g_ref[...]).T.astype(BF16)

    return pl.pallas_call(
        body, name=name, grid=(nb,), in_specs=[pl.BlockSpec((None, aw, B), lambda i: (i, 0, 0)), _full(g_col)],
        out_specs=pl.BlockSpec((B, aw), lambda i: (i, 0)), out_shape=SDS((nb * B, aw), BF16),
        compiler_params=_cp("parallel"))(ot, g_col)


def _gnorm_t_bwd(ot, g_col, dy, name):
    nb, aw, B = ot.shape
    H = aw // HEAD_DIM

    def body(o_ref, g_ref, dy_ref, do_ref, dl_ref, dg_ref):
        ov = o_ref[...]
        r = lax.rsqrt(jnp.mean(ov * ov, axis=0, keepdims=True) + EPS)
        yh = ov * r
        dyn = dy_ref[...].T
        dyh = dyn * g_ref[...]
        dov = r * (dyh - yh * jnp.mean(dyh * yh, axis=0, keepdims=True))
        do_ref[...] = dov.astype(BF16)
        dl_ref[...] = jnp.sum((dov * ov).reshape(H, HEAD_DIM, B), axis=1)
        dgp = jnp.broadcast_to(jnp.sum(dyn * yh, axis=1, keepdims=True), dg_ref.shape)
        _accumulate(dg_ref, pl.program_id(0) == 0, dgp)

    blk = pl.BlockSpec((None, aw, B), lambda i: (i, 0, 0))
    return pl.pallas_call(
        body, name=name, grid=(nb,), in_specs=[blk, _full(g_col), pl.BlockSpec((B, aw), lambda i: (i, 1))],
        out_specs=[blk, pl.BlockSpec((None, H, B), lambda i: (i, 0, 0)), pl.BlockSpec((aw, LANES), lambda i: (0, 0))],
        out_shape=[SDS((nb, aw, B), BF16), SDS((nb, H, B), F32), SDS((aw, LANES), F32)],
        compiler_params=_cp("arbitrary"))(ot, g_col, dy)


def _attn_bwd(imax, zat, qa, ka, dot, lse, delta, name):
    nb, R, B = zat.shape
    aw = R // 3
    H = aw // HEAD_DIM
    hd = HEAD_DIM

    def body(im_ref, k_ref, v_ref, ka_ref, q_ref, qa_ref, do_ref, lse_ref, dl_ref,
             dq_ref, dqa_ref, dk_ref, dka_ref, dv_ref):
        h = pl.program_id(0)
        j = pl.program_id(1)

        @pl.when(j == 0)
        def _():
            dq_ref[...] = jnp.zeros_like(dq_ref)
            dqa_ref[...] = jnp.zeros_like(dqa_ref)
        kt = jnp.concatenate([k_ref[...], ka_ref[...]], axis=0)
        vt = v_ref[...]

        def block(i, carry, masked):
            dk, dv = carry
            qt = jnp.concatenate([q_ref[i] * 0.125, qa_ref[i]], axis=0)
            dov = do_ref[i]
            s = _tn(kt, qt)
            if masked:
                s = jnp.where(_causal_mask(B), s, NEG)
            p = jnp.exp(s - lse_ref[i])
            dp = _tn(vt, dov)
            ds = (p * (dp - dl_ref[i, pl.ds(h, 1), :])).astype(BF16)
            dv = dv + _nt(dov, p.astype(BF16))
            dk = dk + _nt(qt, ds)
            r = _nn(kt, ds)
            dq_ref[i] += 0.125 * r[:hd]
            dqa_ref[i] += r[hd:]
            return dk, dv

        carry = block(j, (jnp.zeros((hd + AUG, B), F32), jnp.zeros((hd, B), F32)), True)
        dk, dv = lax.fori_loop(j + 1, im_ref[h * nb + j] + 1, lambda i, c: block(i, c, False), carry)
        dk_ref[...] = dk[:hd]
        dka_ref[...] = dk[hd:]
        dv_ref[...] = dv

    head_rows = lambda off: pl.BlockSpec((nb, hd, B), lambda h, j, im: (0, off + h, 0))
    key_rows = lambda off: pl.BlockSpec((None, hd, B), lambda h, j, im: (j, off + h, 0))
    aug_all = pl.BlockSpec((None, nb, AUG, B), lambda h, j, im: (h, 0, 0, 0))
    aug_one = pl.BlockSpec((None, None, AUG, B), lambda h, j, im: (h, j, 0, 0))
    grid_spec = pltpu.PrefetchScalarGridSpec(
        num_scalar_prefetch=1, grid=(H, nb),
        in_specs=[key_rows(H), key_rows(2 * H), aug_one, head_rows(0), aug_all, head_rows(0),
                  pl.BlockSpec((None, nb, 1, B), lambda h, j, im: (h, 0, 0, 0)),
                  pl.BlockSpec((nb, H, B), lambda h, j, im: (0, 0, 0))],
        out_specs=[head_rows(0), aug_all, key_rows(0), aug_one, key_rows(0)])
    return pl.pallas_call(
        body, name=name, grid_spec=grid_spec,
        out_shape=[SDS((nb, aw, B), F32), SDS((H, nb, AUG, B), F32), SDS((nb, aw, B), F32),
                   SDS((H, nb, AUG, B), F32), SDS((nb, aw, B), F32)],
        compiler_params=_cp("arbitrary", "arbitrary"))(imax, zat, zat, ka, zat, qa, dot, lse, delta)


def _mix_dwt(dq, dk, dv, dzft, h, name):
    nb, aw, B = dq.shape
    D = h.shape[1]

    def body(dq_ref, dk_ref, dv_ref, dzf_ref, h_ref, o_ref, of_ref):
        first = pl.program_id(0) == 0
        hv = h_ref[...]
        part = jnp.concatenate([_nn(r[...].astype(BF16), hv) for r in (dq_ref, dk_ref, dv_ref)], axis=0)
        _accumulate(o_ref, first, part)
        _accumulate(of_ref, first, _nn(dzf_ref[...].astype(BF16), hv))

    blk = pl.BlockSpec((None, aw, B), lambda i: (i, 0, 0))
    return pl.pallas_call(
        body, name=name, grid=(nb,),
        in_specs=[blk, blk, blk, pl.BlockSpec((SUBLANES, B), lambda i: (0, i)), pl.BlockSpec((B, D), lambda i: (i, 0))],
        out_specs=[pl.BlockSpec((3 * aw, D), lambda i: (0, 0)), pl.BlockSpec((SUBLANES, D), lambda i: (0, 0))],
        out_shape=[SDS((3 * aw, D), F32), SDS((SUBLANES, D), F32)],
        compiler_params=_cp("arbitrary"))(dq, dk, dv, dzft, h)


def _mix_bwd_in(dzc, dq, dk, dv, dzft, wc, wat, wft, x, g, dres, name, scatter):
    T, D = x.shape
    nb, aw, B = dq.shape

    def body(dzc_ref, dq_ref, dk_ref, dv_ref, dzf_ref, wc_ref, wa_ref, wf_ref, x_ref, g_ref, r_ref,
             dx_ref, dxb_ref, dg_ref):
        dh = _nt(dzc_ref[...], wc_ref[...])
        for s, r in enumerate((dq_ref, dk_ref, dv_ref)):
            dh = dh + _tn(r[...].astype(BF16), wa_ref[s * aw:(s + 1) * aw, :])
        dh = dh + _tn(dzf_ref[...].astype(BF16), wf_ref[...])
        _norm_bwd_store(dh, x_ref, g_ref, r_ref, dx_ref, dxb_ref, dg_ref, pl.program_id(0) == 0)

    blk = pl.BlockSpec((None, aw, B), lambda i: (i, 0, 0))
    row = lambda n: pl.BlockSpec((B, n), lambda i: (i, 0))
    return _hosted_call(
        body, name, nb,
        [row(dzc.shape[1]), blk, blk, blk, pl.BlockSpec((SUBLANES, B), lambda i: (0, i)),
         _full(wc), _full(wat), _full(wft), row(D), _full(g), row(D)],
        [row(D), row(D), pl.BlockSpec((1, D), lambda i: (0, 0))],
        [SDS((T, D), F32), SDS((T, D), BF16), SDS((1, D), F32)],
        (dzc, dq, dk, dv, dzft, wc, wat, wft, x, g, dres), scatter)


def _mem_kv(mem, g, wkv, name):
    M, D = mem.shape
    dx = wkv.shape[2]

    def body(m_ref, g_ref, w_ref, kv_ref, h_ref):
        mv = m_ref[...]
        h = (mv * _rstd(mv) * g_ref[...]).astype(BF16)
        h_ref[...] = h
        for s in range(N_DEV):
            kv_ref[s] = _nn(h, w_ref[s]).astype(BF16)

    return pl.pallas_call(
        body, name=name, grid=(1,), in_specs=[_full(mem), _full(g), _full(wkv)],
        out_specs=[pl.BlockSpec((N_DEV, M, dx), lambda i: (0, 0, 0)), _full(mem)],
        out_shape=[SDS((N_DEV, M, dx), BF16), SDS((M, D), BF16)], compiler_params=_cp("arbitrary"))(mem, g, wkv)


def _mem_kv_bwd(dkv, hm, wkv, mem, g, name):
    M, D = mem.shape
    dx = wkv.shape[2]

    def body(dkv_ref, h_ref, w_ref, m_ref, g_ref, dw_ref, dg_ref):
        hv = h_ref[...]
        dh = jnp.zeros((M, D), F32)
        for s in range(N_DEV):
            d = dkv_ref[s].astype(BF16)
            dw_ref[s] = _tn(hv, d).astype(BF16)
            dh = dh + _nt(d, w_ref[s])
        mv = m_ref[...]
        dg_ref[...] = jnp.sum(dh * (mv * _rstd(mv)), axis=0, keepdims=True)

    return pl.pallas_call(
        body, name=name, grid=(1,), in_specs=[_full(dkv), _full(hm), _full(wkv), _full(mem), _full(g)],
        out_specs=[pl.BlockSpec((N_DEV, D, dx), lambda i: (0, 0, 0)), pl.BlockSpec((1, D), lambda i: (0, 0))],
        out_shape=[SDS((N_DEV, D, dx), BF16), SDS((1, D), F32)], compiler_params=_cp("arbitrary"))(dkv, hm, wkv, mem, g)


def _xattn_probs(q_ref, kv_ref, h, dx, scale):
    qh = q_ref[:, h * dx:(h + 1) * dx]
    kh = kv_ref[h]
    s = _nt(qh, kh) * scale
    p = jnp.exp(s - jnp.max(s, axis=-1, keepdims=True))
    return qh, kh, p / jnp.sum(p, axis=-1, keepdims=True)


def _xattn_fwd(q, kv, name, tm=512):
    T, D = q.shape
    dx = D // N_XHEADS
    scale = dx ** -0.5
    tm = _tile(T, tm, SUBLANES)

    def body(q_ref, kv_ref, o_ref):
        for h in range(N_XHEADS):
            _, _, p = _xattn_probs(q_ref, kv_ref, h, dx, scale)
            o_ref[:, h * dx:(h + 1) * dx] = _nn(p.astype(BF16), kv_ref[N_XHEADS + h]).astype(BF16)

    row = pl.BlockSpec((tm, D), lambda i: (i, 0))
    return pl.pallas_call(body, name=name, grid=(T // tm,), in_specs=[row, _full(kv)], out_specs=row,
                          out_shape=SDS((T, D), BF16), compiler_params=_cp("parallel"))(q, kv)


def _xattn_bwd(q, kv, do, name, tm=512):
    T, D = q.shape
    dx = D // N_XHEADS
    scale = dx ** -0.5
    tm = _tile(T, tm, SUBLANES)

    def body(q_ref, kv_ref, do_ref, dq_ref, dkv_ref):
        @pl.when(pl.program_id(0) == 0)
        def _():
            dkv_ref[...] = jnp.zeros_like(dkv_ref)
        for h in range(N_XHEADS):
            qh, kh, p = _xattn_probs(q_ref, kv_ref, h, dx, scale)
            doh = do_ref[:, h * dx:(h + 1) * dx]
            dp = _nt(doh, kv_ref[N_XHEADS + h])
            ds = (p * (dp - jnp.sum(p * dp, axis=-1, keepdims=True)) * scale).astype(BF16)
            dq_ref[:, h * dx:(h + 1) * dx] = _nn(ds, kh).astype(BF16)
            dkv_ref[h] += _tn(ds, qh)
            dkv_ref[N_XHEADS + h] += _tn(p.astype(BF16), doh)

    row = pl.BlockSpec((tm, D), lambda i: (i, 0))
    return pl.pallas_call(body, name=name, grid=(T // tm,), in_specs=[row, _full(kv), row],
                          out_specs=[row, _full(kv)], out_shape=[SDS((T, D), BF16), SDS(kv.shape, F32)],
                          compiler_params=_cp("arbitrary"))(q, kv, do)


def _mesh_position():
    return lax.axis_index("x"), lax.axis_index("y"), lax.axis_index("c")


class _GatherCopies:
    def __init__(self, x_refs, out_refs, send_sems, recv_sems, local_sems):
        self.x_refs, self.out_refs, self.n = x_refs, out_refs, len(x_refs)
        self.send_sems, self.recv_sems, self.local_sems = send_sems, recv_sems, local_sems
        x, y, c = _mesh_position()
        self.c = c
        self.me, self.sibling = (x, y, c), (x, y, 1 - c)
        self.chips = [(1 - x, y), (x, 1 - y), (1 - x, 1 - y)]

    def slot(self, w, px, py, pc):
        return self.out_refs[w].at[4 * px + 2 * py + pc]

    def copy(self, w, k, block, to, src=None):
        return pltpu.make_async_remote_copy(
            src_ref=self.slot(w, *block) if src is None else src, dst_ref=self.slot(w, *block),
            send_sem=self.send_sems.at[w, k], recv_sem=self.recv_sems.at[w, k], device_id=to,
            device_id_type=pl.DeviceIdType.MESH)

    def mine(self, w):
        return pltpu.make_async_copy(self.x_refs[w], self.slot(w, *self.me), self.local_sems.at[w])

    def first(self, w):
        src = self.x_refs[w]
        return [self.copy(w, 0, self.me, self.sibling, src=src)] + [
            self.copy(w, 1 + j, self.me, (*chip, self.c), src=src) for j, chip in enumerate(self.chips)]

    def passed(self, w):
        return [self.copy(w, 4 + j, (*chip, self.c), self.sibling) for j, chip in enumerate(self.chips)]


def _gather_start(*refs):
    cp = _GatherCopies(*refs)
    for w in range(cp.n):
        cp.mine(w).start()
        for f in cp.first(w):
            f.start()


def _gather_relay(*refs):
    cp = _GatherCopies(*refs)
    for w in range(cp.n):
        for j, (chip, fwd) in enumerate(zip(cp.chips, cp.passed(w))):
            cp.copy(w, 1 + j, (*chip, cp.c), cp.me).wait_recv()
            fwd.start()


def _gather_finish(*refs):
    cp = _GatherCopies(*refs)
    for w in range(cp.n):
        cp.copy(w, 0, cp.sibling, cp.me).wait_recv()
        for j, chip in enumerate(cp.chips):
            cp.copy(w, 4 + j, (*chip, 1 - cp.c), cp.me).wait_recv()
    for w in range(cp.n):
        for f in cp.first(w) + cp.passed(w):
            f.wait_send()
        cp.mine(w).wait()


def _all_gather(shards, name):
    n = len(shards)

    def body(*refs):
        parts = (refs[:n], refs[n:2 * n]) + refs[2 * n:]
        _gather_start(*parts)
        _gather_relay(*parts)
        _gather_finish(*parts)

    any_spec = pl.BlockSpec(memory_space=pl.ANY)
    return pl.pallas_call(
        body, name=name, out_shape=[SDS((N_DEV,) + s.shape, s.dtype) for s in shards],
        in_specs=[any_spec] * n, out_specs=[any_spec] * n,
        scratch_shapes=[pltpu.SemaphoreType.DMA((n, 7)), pltpu.SemaphoreType.DMA((n, 7)), pltpu.SemaphoreType.DMA((n,))],
    )(*shards)


def _scatter_copies(x_refs, out_refs, send_sems, recv_sems, local_sems):
    x, y, c = _mesh_position()
    me = 4 * x + 2 * y + c
    own, remote = [], []
    for w, (src, dst) in enumerate(zip(x_refs, out_refs)):
        own.append(pltpu.make_async_copy(src.at[me], dst.at[me], local_sems.at[w]))
        for k in range(1, N_DEV):
            px = 1 - x if k & 4 else x
            py = 1 - y if k & 2 else y
            pc = 1 - c if k & 1 else c
            remote.append(pltpu.make_async_remote_copy(
                src_ref=src.at[4 * px + 2 * py + pc], dst_ref=dst.at[me], send_sem=send_sems.at[w, k - 1],
                recv_sem=recv_sems.at[w, k - 1], device_id=(px, py, pc), device_id_type=pl.DeviceIdType.MESH))
    return own, remote


def _scatter_start(x_refs, out_refs, send_sems, recv_sems, local_sems):
    own, remote = _scatter_copies(x_refs, out_refs, send_sems, recv_sems, local_sems)
    for cp in own + remote:
        cp.start()


def _scatter_wait(x_refs, out_refs, send_sems, recv_sems, local_sems):
    own, remote = _scatter_copies(x_refs, out_refs, send_sems, recv_sems, local_sems)
    for cp in remote:
        cp.wait_recv()
    for cp in remote:
        cp.wait_send()
    for cp in own:
        cp.wait()


def _sum_adamw(parts, w, m, v, name, tr=256):
    L, R, C = w.shape
    tr = _tile(R, tr, SUBLANES)
    nblk = R // tr

    def body(*refs):
        p_refs = refs[:L]
        w_ref, m_ref, v_ref, g_ref, d_ref, mo_ref, vo_ref = refs[L:]
        layer = pl.program_id(0)
        g = None
        for k, p_ref in enumerate(p_refs):
            gk = p_ref[0].astype(F32)
            for s in range(1, N_DEV):
                gk = gk + p_ref[s].astype(F32)
            g = gk if g is None else jnp.where(layer == k, gk, g)
        mn = ADAM_B1 * m_ref[...] + (1.0 - ADAM_B1) * g
        vn = ADAM_B2 * v_ref[...] + (1.0 - ADAM_B2) * jnp.square(g)
        m_hat = mn / (1.0 - ADAM_B1 ** ADAM_STEP)
        v_hat = vn / (1.0 - ADAM_B2 ** ADAM_STEP)
        g_ref[...] = g
        d_ref[...] = -ADAM_LR * (m_hat / (jnp.sqrt(v_hat) + ADAM_EPS) + ADAM_WD * w_ref[...])
        mo_ref[...] = mn
        vo_ref[...] = vn

    def part_spec(k):
        return pl.BlockSpec((N_DEV, tr, C),
                            lambda l, i: (0, jnp.where(l == k, i, jnp.where(l < k, 0, nblk - 1)), 0))

    row = pl.BlockSpec((None, tr, C), lambda l, i: (l, i, 0))
    return pl.pallas_call(
        body, name=name, grid=(L, nblk), in_specs=[part_spec(k) for k in range(L)] + [row, row, row],
        out_specs=[row] * 4, out_shape=[SDS((L, R, C), F32)] * 4,
        compiler_params=_cp("arbitrary", "arbitrary"))(*parts, w, m, v)


def _pack(flat_list, row_mult):
    flat = jnp.concatenate(flat_list, axis=-1)
    n = flat.shape[-1]
    chunk = row_mult * PACK_COLS
    pad = -n % chunk
    flat = jnp.pad(flat, [(0, 0)] * (flat.ndim - 1) + [(0, pad)])
    return flat.reshape(flat.shape[:-1] + ((n + pad) // PACK_COLS, PACK_COLS))


def _unpack(packed, shapes):
    lead = packed.shape[:-2]
    flat = packed.reshape(lead + (-1,))
    out = []
    off = 0
    for shp in shapes:
        n = 1
        for d in shp:
            n *= d
        out.append(flat[..., off:off + n].reshape(lead + tuple(shp)))
        off += n
    return out


def _ffn_bwd(dx, dxb, saved, wgu, wd, tag):
    x, g, h, ab, act = saved
    dgu = _ffn_bwd_act(dxb, wd, ab, f"{tag}_bwd_act")
    dwd = _ffn_dwd(act, dxb, f"{tag}_dwd")
    dwgu = _ffn_dwgu(h, dgu, f"{tag}_dwgu")
    return _ffn_bwd_in(dgu, wgu, x, g, dx, f"{tag}_bwd_in", [dwgu, dwd.reshape(N_DEV, -1, dwd.shape[-1])])


def _aug_rows(rows, H, nb, B):
    a = jnp.stack(rows + [jnp.zeros_like(rows[0])] * (AUG - len(rows)), axis=1)
    return a.reshape(H, AUG, nb, B).transpose(0, 2, 1, 3).astype(BF16)


def _mix_fwd(x, g, w_in8, w_conv8, b_f, g_conv, g_att, w_out, tag):
    T, D = x.shape
    w_in = w_in8.transpose(1, 0, 2).reshape(D, -1)
    cw = D // 2
    aw = D - cw
    H = aw // HEAD_DIM
    B = _tile(T, ATT_BLOCK, LANES)
    nb = T // B
    w_c = w_in[:, :3 * cw]
    w_at = w_in[:, 3 * cw:3 * cw + 3 * aw].T
    w_ft = jnp.pad(w_in[:, 3 * cw + 3 * aw:].T, ((0, SUBLANES - H), (0, 0)))
    zc, h = _norm_mm(x, g, w_c, F32, f"{tag}_in_conv")
    zat, zft8 = _proj_t(h, w_at, w_ft, B, f"{tag}_in_att")
    zft = zft8[:H]
    bcol = b_f.reshape(H, 1)
    c, chi, cmid, clo = _logf_cumsum(zft, bcol, f"{tag}_cumsum")
    one = jnp.ones_like(c)
    qa = _aug_rows([one, one, one, chi, cmid, clo], H, nb, B)
    ka = _aug_rows([-chi, -cmid, -clo, one, one, one], H, nb, B)
    cb = c.reshape(H, nb, B)
    jmin, imax = _skip_table(cb[:, :, 0:1], cb[:, :, B - 1].reshape(H, 1, nb), _head_norms(zat, f"{tag}_norms"),
                             f"{tag}_skip")
    ot, lse = _attn_fwd(jmin.reshape(H * nb), zat, qa, ka, f"{tag}_attn")
    g_col = g_att.reshape(aw, 1)
    yna = _gnorm_t_fwd(ot, g_col, f"{tag}_gnorm")
    ync = _conv_fwd(zc, w_conv8, g_conv, f"{tag}_conv")
    xo = _mm_res([ync, yna], w_out, x, f"{tag}_out")
    saved = (x, g, h, zc, zft, bcol, zat, qa, ka, imax.reshape(H * nb), ot, lse, ync, yna, w_c, w_at, w_ft, g_col)
    return xo, saved


def _mix_bwd(dx, dxb, saved, w_conv8, g_conv, w_out, tag):
    x, g, h, zc, zft, bcol, zat, qa, ka, imax, ot, lse, ync, yna, w_c, w_at, w_ft, g_col = saved
    T, D = x.shape
    H = zft.shape[0]
    dw_out = jnp.concatenate([_mm_tn(ync, dxb, f"{tag}_dwout_c"), _mm_tn(yna, dxb, f"{tag}_dwout_a")], axis=0)
    dy = _mm_nt(dxb, w_out, F32, f"{tag}_bwd_out")
    dzc, dwc8, dg_conv = _conv_bwd(zc, dy, w_conv8, g_conv, f"{tag}_conv_bwd")
    dot, delta, dg_att = _gnorm_t_bwd(ot, g_col, dy, f"{tag}_gnorm_bwd")
    dq, dqa, dk, dka, dv = _attn_bwd(imax, zat, qa, ka, dot, lse, delta, f"{tag}_attn_bwd")
    dcq = dqa[:, :, 3, :].reshape(H, T)
    dck = dka[:, :, 0, :].reshape(H, T)
    dzft, db = _logf_cumsum_bwd(dcq, dck, zft, bcol, f"{tag}_cumsum_bwd")
    dzft8 = jnp.pad(dzft, ((0, SUBLANES - H), (0, 0)))
    dw_c = _mm_tn(h, dzc, f"{tag}_dwin_c")
    dw_at, dw_ft = _mix_dwt(dq, dk, dv, dzft8, h, f"{tag}_dwin_a")
    dw_in = jnp.concatenate([dw_c, dw_at.T.astype(BF16), dw_ft[:H].T.astype(BF16)], axis=1)
    scatter = [dw_in.reshape(D, N_DEV, -1).transpose(1, 0, 2), dwc8[:3].reshape(3, N_DEV, -1).transpose(1, 0, 2),
               dw_out.reshape(N_DEV, D // N_DEV, D)]
    dx, dxb, dg, p_in, p_conv, p_out = _mix_bwd_in(dzc, dq, dk, dv, dzft8, w_c, w_at, w_ft, x, g, dx, f"{tag}_bwd_in",
                                                   scatter)
    return dx, dxb, dg, p_in, p_conv, db.reshape(H), dg_conv, dg_att[:, 0], p_out


def _xattn_block_fwd(x, g, mem, g_mem, w_q, w_kv, w_o, tag):
    q, h = _norm_mm(x, g, w_q, BF16, f"{tag}_q")
    kv, hm = _mem_kv(mem, g_mem, w_kv, f"{tag}_kv")
    o = _xattn_fwd(q, kv, f"{tag}_attn")
    xo = _mm_res([o], w_o, x, f"{tag}_o")
    return xo, (x, g, h, mem, g_mem, hm, q, kv, o)


def _xattn_block_bwd(dx, dxb, saved, w_q, w_kv, w_o, tag):
    x, g, h, mem, g_mem, hm, q, kv, o = saved
    dw_o = _mm_tn(o, dxb, f"{tag}_dwo")
    do = _mm_nt(dxb, w_o, BF16, f"{tag}_bwd_o")
    dq, dkv = _xattn_bwd(q, kv, do, f"{tag}_attn_bwd")
    dw_q = _mm_tn(h, dq, f"{tag}_dwq")
    dw_kv, dg_mem = _mem_kv_bwd(dkv, hm, w_kv, mem, g_mem, f"{tag}_kv_bwd")
    D = dw_q.shape[0]
    scatter = [dw_q.reshape(N_DEV, D // N_DEV, D), dw_kv, dw_o.reshape(N_DEV, D // N_DEV, D)]
    dx, dxb, dg, p_q, p_kv, p_o = _mm_nt_normbwd(dq, w_q, x, g, dx, f"{tag}_bwd_q", scatter)
    return dx, dxb, dg, dg_mem, p_q, p_kv, p_o


GATHER_FIRST = ['w_ffn1_gu']
GATHER_REST = ['w_ffn1_down', 'w_mix_in', 'w_mix_out', 'w_xq', 'w_xkv', 'w_xo', 'w_ffn2_gu', 'w_ffn2_down']
GATHER_NEXT_A = ['w_ffn1_gu', 'w_ffn1_down', 'w_mix_in', 'w_mix_out', 'w_xq', 'w_xkv', 'w_xo']
GATHER_NEXT_B = ['w_ffn2_gu', 'w_ffn2_down']


def _local_step(x, mem, tgt, P):
    L = P['g_ffn1'].shape[0]
    shards = lambda names, l: [P[n][l].astype(BF16) for n in names]
    W = {n: [None] * L for n in SHARDED}

    def keep(names, l, arrays):
        for n, a in zip(names, arrays):
            W[n][l] = a

    *first, conv = _all_gather(shards(GATHER_FIRST, 0) + [P['w_conv']], "gather_first")
    keep(GATHER_FIRST, 0, first)
    w_conv = conv.transpose(1, 2, 0, 3).reshape(L, 3, -1)
    saved = []
    for l in range(L):
        row = lambda n: P[n][l][None, :]
        wc8 = jnp.pad(w_conv[l], ((0, SUBLANES - 3), (0, 0)))
        x1 = x
        bring = GATHER_REST if l == 0 else []
        ab, act, h, *got = _ffn_gu(x, row('g_ffn1'), W['w_ffn1_gu'][l], f"l{l}_ffn1_gu", gather=shards(bring, l))
        keep(bring, l, got)
        x, = _ffn_down(act, W['w_ffn1_down'][l], x, f"l{l}_ffn1_down")
        s1 = (x1, row('g_ffn1'), h, ab, act)
        x, s2 = _mix_fwd(x, row('g_mix'), W['w_mix_in'][l], wc8, P['b_f'][l], row('g_conv_out'), P['g_att_out'][l],
                         W['w_mix_out'][l], f"l{l}_mix")
        x, s3 = _xattn_block_fwd(x, row('g_xattn'), mem, row('g_mem'), W['w_xq'][l], W['w_xkv'][l], W['w_xo'][l],
                                 f"l{l}_xattn")
        x4 = x
        bring_a, bring_b = (GATHER_NEXT_A, GATHER_NEXT_B) if l + 1 < L else ([], [])
        ab, act, h, *got = _ffn_gu(x, row('g_ffn2'), W['w_ffn2_gu'][l], f"l{l}_ffn2_gu", gather=shards(bring_a, l + 1))
        keep(bring_a, l + 1, got)
        x, *got = _ffn_down(act, W['w_ffn2_down'][l], x, f"l{l}_ffn2_down", gather=shards(bring_b, l + 1))
        keep(bring_b, l + 1, got)
        saved.append((s1, s2, s3, (x4, row('g_ffn2'), h, ab, act), wc8))
    loss, dx, dxb, dg_final = _loss_head(x, P['g_final'][None, :], tgt, "loss_head")
    G = {n: [None] * L for n in WEIGHT_NAMES if n != 'g_final'}
    for l in reversed(range(L)):
        row = lambda n: P[n][l][None, :]
        s1, s2, s3, s4, wc8 = saved[l]
        dx, dxb, G['g_ffn2'][l], G['w_ffn2_gu'][l], G['w_ffn2_down'][l] = _ffn_bwd(
            dx, dxb, s4, W['w_ffn2_gu'][l], W['w_ffn2_down'][l], f"l{l}_ffn2")
        dx, dxb, G['g_xattn'][l], G['g_mem'][l], G['w_xq'][l], G['w_xkv'][l], G['w_xo'][l] = _xattn_block_bwd(
            dx, dxb, s3, W['w_xq'][l], W['w_xkv'][l], W['w_xo'][l], f"l{l}_xattn")
        (dx, dxb, G['g_mix'][l], G['w_mix_in'][l], G['w_conv'][l], G['b_f'][l], G['g_conv_out'][l], G['g_att_out'][l],
         G['w_mix_out'][l]) = _mix_bwd(dx, dxb, s2, wc8, row('g_conv_out'), W['w_mix_out'][l], f"l{l}_mix")
        dx, dxb, G['g_ffn1'][l], G['w_ffn1_gu'][l], G['w_ffn1_down'][l] = _ffn_bwd(
            dx, dxb, s1, W['w_ffn1_gu'][l], W['w_ffn1_down'][l], f"l{l}_ffn1")
    grads = {n: (G[n] if n in SHARDED else jnp.stack([a.reshape(P[n].shape[1:]) for a in G[n]])) for n in G}
    grads['g_final'] = dg_final.reshape(-1)
    return loss, dx, grads


def kernel(x, mem, g_ffn1, w_ffn1_gu, w_ffn1_down, g_mix, w_mix_in, w_conv, b_f, g_conv_out, g_att_out, w_mix_out, g_xattn, g_mem, w_xq, w_xkv, w_xo, g_ffn2, w_ffn2_gu, w_ffn2_down, g_final, loss_target, m_g_ffn1, m_w_ffn1_gu, m_w_ffn1_down, m_g_mix, m_w_mix_in, m_w_conv, m_b_f, m_g_conv_out, m_g_att_out, m_w_mix_out, m_g_xattn, m_g_mem, m_w_xq, m_w_xkv, m_w_xo, m_g_ffn2, m_w_ffn2_gu, m_w_ffn2_down, m_g_final, v_g_ffn1, v_w_ffn1_gu, v_w_ffn1_down, v_g_mix, v_w_mix_in, v_w_conv, v_b_f, v_g_conv_out, v_g_att_out, v_w_mix_out, v_g_xattn, v_g_mem, v_w_xq, v_w_xkv, v_w_xo, v_g_ffn2, v_w_ffn2_gu, v_w_ffn2_down, v_g_final):
    args = (x, mem, g_ffn1, w_ffn1_gu, w_ffn1_down, g_mix, w_mix_in, w_conv, b_f, g_conv_out, g_att_out, w_mix_out,
            g_xattn, g_mem, w_xq, w_xkv, w_xo, g_ffn2, w_ffn2_gu, w_ffn2_down, g_final)
    P = dict(zip(IN_NAMES, args))
    moms = (m_g_ffn1, m_w_ffn1_gu, m_w_ffn1_down, m_g_mix, m_w_mix_in, m_w_conv, m_b_f, m_g_conv_out, m_g_att_out,
            m_w_mix_out, m_g_xattn, m_g_mem, m_w_xq, m_w_xkv, m_w_xo, m_g_ffn2, m_w_ffn2_gu, m_w_ffn2_down, m_g_final)
    vars_ = (v_g_ffn1, v_w_ffn1_gu, v_w_ffn1_down, v_g_mix, v_w_mix_in, v_w_conv, v_b_f, v_g_conv_out, v_g_att_out,
             v_w_mix_out, v_g_xattn, v_g_mem, v_w_xq, v_w_xkv, v_w_xo, v_g_ffn2, v_w_ffn2_gu, v_w_ffn2_down, v_g_final)
    MOM = dict(zip(WEIGHT_NAMES, moms))
    VAR = dict(zip(WEIGHT_NAMES, vars_))

    loss_part, dx, grads = _local_step(x[0], mem[0], loss_target[0], {n: P[n] for n in WEIGHT_NAMES})

    small_shapes = [P[n].shape for n in REPLICATED] + [(1,)]
    small = _pack([grads[n].reshape(-1) for n in REPLICATED] + [loss_part[0, :1]], SUBLANES)
    small_parts = _all_gather([small], "gather_small_grads")[0]

    res = {n: _sum_adamw(grads[n], P[n], MOM[n], VAR[n], "adamw_" + n) for n in SHARDED}
    zero = [jnp.zeros((1,), F32)]
    w_, m_, v_ = (_pack([d[n].reshape(-1) for n in REPLICATED] + zero, SUBLANES)[None] for d in (P, MOM, VAR))
    small_out = [_unpack(o[0], small_shapes) for o in _sum_adamw([small_parts], w_, m_, v_, "adamw_vectors")]
    for k, n in enumerate(REPLICATED):
        res[n] = [o[k] for o in small_out]
    out = []
    for k in range(4):
        out += [res[n][k] for n in WEIGHT_NAMES]
    return (small_out[0][-1].reshape(()), dx[None], *out)
```

```python
import jax
import jax.numpy as jnp
from jax import lax
from jax.experimental import pallas as pl
from jax.experimental.pallas import tpu as pltpu

F32 = jnp.float32
BF16 = jnp.bfloat16
I32 = jnp.int32
SDS = jax.ShapeDtypeStruct

EPS = 1e-6
HEAD_DIM = 64
N_XHEADS = 4
N_DEV = 8
LANES = 128
SUBLANES = 8
AUG = 16
ATT_BLOCK = 512
NEG = -1e30
SKIP_MARGIN = 115.0
PACK_COLS = 1024
VMEM_LIMIT = 58 * 1024 * 1024

ADAM_LR = 0.001
ADAM_B1 = 0.9
ADAM_B2 = 0.999
ADAM_EPS = 1e-08
ADAM_WD = 0.01
ADAM_STEP = 10

IN_NAMES = ['x', 'mem', 'g_ffn1', 'w_ffn1_gu', 'w_ffn1_down', 'g_mix', 'w_mix_in', 'w_conv', 'b_f', 'g_conv_out',
            'g_att_out', 'w_mix_out', 'g_xattn', 'g_mem', 'w_xq', 'w_xkv', 'w_xo', 'g_ffn2', 'w_ffn2_gu',
            'w_ffn2_down', 'g_final']
WEIGHT_NAMES = IN_NAMES[2:]
SHARDED = ['w_ffn1_gu', 'w_ffn1_down', 'w_mix_in', 'w_conv', 'w_mix_out', 'w_xq', 'w_xkv', 'w_xo', 'w_ffn2_gu',
           'w_ffn2_down']
REPLICATED = [n for n in WEIGHT_NAMES if n not in SHARDED]


def _tile(n, pref, mult):
    t = min(pref, n) // mult * mult
    while t >= mult:
        if n % t == 0:
            return t
        t -= mult
    return n


def _cp(*sem):
    return pltpu.CompilerParams(dimension_semantics=sem, vmem_limit_bytes=VMEM_LIMIT)


def _nt(a, b):
    return lax.dot_general(a, b, (((1,), (1,)), ((), ())), preferred_element_type=F32)


def _tn(a, b):
    return lax.dot_general(a, b, (((0,), (0,)), ((), ())), preferred_element_type=F32)


def _nn(a, b):
    return jnp.dot(a, b, preferred_element_type=F32)


def _rstd(xv):
    return lax.rsqrt(jnp.mean(xv * xv, axis=-1, keepdims=True) + EPS)


def _full(a):
    nd = a.ndim
    return pl.BlockSpec(a.shape, lambda *_: (0,) * nd, pipeline_mode=pl.Buffered(1))


def _rows2d(w_ref):
    s, a, b = w_ref.shape
    return w_ref[...].reshape(s * a, b)


def _accumulate(ref, first, part):
    @pl.when(first)
    def _():
        ref[...] = part

    @pl.when(jnp.logical_not(first))
    def _():
        ref[...] += part


def _norm_bwd_store(dh, x_ref, g_ref, r_ref, dx_ref, dxb_ref, dg_ref, first):
    xv = x_ref[...]
    r = _rstd(xv)
    xh = xv * r
    dxh = dh * g_ref[...]
    dx = r * (dxh - xh * jnp.mean(dxh * xh, axis=-1, keepdims=True)) + r_ref[...]
    dx_ref[...] = dx
    dxb_ref[...] = dx.astype(BF16)
    _accumulate(dg_ref, first, jnp.sum(dh * xh, axis=0, keepdims=True))


def _norm_mm(x, g, w, out_dtype, name, tm=512):
    T, D = x.shape
    N = w.shape[-1]
    tm = _tile(T, tm, SUBLANES)

    def body(x_ref, g_ref, w_ref, o_ref, h_ref):
        xv = x_ref[...]
        h = (xv * _rstd(xv) * g_ref[...]).astype(BF16)
        h_ref[...] = h
        o_ref[...] = _nn(h, w_ref[...] if w.ndim == 2 else _rows2d(w_ref)).astype(o_ref.dtype)

    row = lambda n: pl.BlockSpec((tm, n), lambda i: (i, 0))
    return pl.pallas_call(
        body, name=name, grid=(T // tm,),
        in_specs=[row(D), _full(g), _full(w)], out_specs=[row(N), row(D)],
        out_shape=[SDS((T, N), out_dtype), SDS((T, D), BF16)], compiler_params=_cp("parallel"))(x, g, w)


def _mm_res(a_list, w, res, name, tm=512):
    T = res.shape[0]
    N = w.shape[-1]
    tm = _tile(T, tm, SUBLANES)
    n_a = len(a_list)

    def body(*refs):
        a_refs = refs[:n_a]
        w_ref, r_ref, o_ref = refs[n_a:]
        wv = _rows2d(w_ref)
        acc = r_ref[...]
        off = 0
        for a_ref in a_refs:
            k = a_ref.shape[1]
            acc = acc + _nn(a_ref[...], wv[off:off + k, :])
            off += k
        o_ref[...] = acc

    row = lambda n: pl.BlockSpec((tm, n), lambda i: (i, 0))
    return pl.pallas_call(
        body, name=name, grid=(T // tm,),
        in_specs=[row(a.shape[1]) for a in a_list] + [_full(w), row(N)], out_specs=row(N),
        out_shape=SDS((T, N), F32), compiler_params=_cp("parallel"))(*a_list, w, res)


def _mm_tn(a, b, name, out_dtype=BF16, tt=2048):
    T, K = a.shape
    N = b.shape[1]
    tt = _tile(T, tt, 2 * SUBLANES)
    nt = T // tt

    def body(a_ref, b_ref, o_ref, acc_ref):
        t = pl.program_id(0)
        _accumulate(acc_ref, t == 0, _tn(a_ref[...], b_ref[...]))

        @pl.when(t == nt - 1)
        def _():
            o_ref[...] = acc_ref[...].astype(o_ref.dtype)

    return pl.pallas_call(
        body, name=name, grid=(nt,),
        in_specs=[pl.BlockSpec((tt, K), lambda t: (t, 0)), pl.BlockSpec((tt, N), lambda t: (t, 0))],
        out_specs=pl.BlockSpec((K, N), lambda t: (0, 0)), out_shape=SDS((K, N), out_dtype),
        scratch_shapes=[pltpu.VMEM((K, N), F32)], compiler_params=_cp("arbitrary"))(a, b)


def _hosted_call(body, name, n_steps, in_specs, out_specs, out_shape, args, scatter=(), gather=()):
    scatter, gather = list(scatter), list(gather)
    n_s, n_g = len(scatter), len(gather)
    if n_s + n_g == 0:
        return pl.pallas_call(body, name=name, grid=(n_steps,), in_specs=list(in_specs), out_specs=list(out_specs),
                              out_shape=list(out_shape), compiler_params=_cp("arbitrary"))(*args)
    n_in, n_out = len(in_specs), len(out_specs)
    relay_step = min(max(1, 3 * n_steps // 4), n_steps - 1)

    def wrapped(*refs):
        ins, refs = refs[:n_in], refs[n_in:]
        xs, shards, refs = refs[:n_s], refs[n_s:n_s + n_g], refs[n_s + n_g:]
        outs, refs = refs[:n_out], refs[n_out:]
        recvs, gathered, sems = refs[:n_s], refs[n_s:n_s + n_g], refs[n_s + n_g:]
        s_sems, g_sems = (sems[:3], sems[3:]) if n_s else ((), sems)
        step = pl.program_id(0)

        @pl.when(step == 0)
        def _():
            if n_s:
                _scatter_start(xs, recvs, *s_sems)
            if n_g:
                _gather_start(shards, gathered, *g_sems)
        body(*ins, *outs)
        if n_g:
            @pl.when(step == relay_step)
            def _():
                _gather_relay(shards, gathered, *g_sems)

        @pl.when(step == n_steps - 1)
        def _():
            if n_s:
                _scatter_wait(xs, recvs, *s_sems)
            if n_g:
                _gather_finish(shards, gathered, *g_sems)

    any_spec = pl.BlockSpec(memory_space=pl.ANY)
    sems = []
    for n in (n_s, n_g):
        if n:
            sems += [pltpu.SemaphoreType.DMA((n, N_DEV - 1)), pltpu.SemaphoreType.DMA((n, N_DEV - 1)),
                     pltpu.SemaphoreType.DMA((n,))]
    return pl.pallas_call(
        wrapped, name=name, grid=(n_steps,), in_specs=list(in_specs) + [any_spec] * (n_s + n_g),
        out_specs=list(out_specs) + [any_spec] * (n_s + n_g),
        out_shape=list(out_shape) + [SDS(b.shape, b.dtype) for b in scatter]
        + [SDS((N_DEV,) + b.shape, b.dtype) for b in gather],
        scratch_shapes=sems, compiler_params=_cp("arbitrary"))(*args, *scatter, *gather)


def _ffn_gu(x, g, wgu, name, gather=(), tm=512):
    T, D = x.shape
    fb = wgu.shape[2]
    nh = N_DEV // 2
    tm = _tile(T, tm, SUBLANES)

    def body(x_ref, g_ref, w_ref, ab_ref, act_ref, h_ref):
        xv = x_ref[...]
        h = (xv * _rstd(xv) * g_ref[...]).astype(BF16)
        h_ref[...] = h
        for d in range(nh):
            gt = _nn(h, w_ref[d])
            up = _nn(h, w_ref[d + nh])
            sg = jax.nn.sigmoid(gt)
            silu = gt * sg
            ab_ref[d] = (0.5 * up * (sg * (1.0 + gt * (1.0 - sg)))).astype(BF16)
            ab_ref[d + nh] = (0.5 * silu).astype(BF16)
            act_ref[d] = (silu * up).astype(BF16)

    row = pl.BlockSpec((tm, D), lambda i: (i, 0))
    blocks = lambda n: pl.BlockSpec((n, tm, fb), lambda i: (0, i, 0))
    return _hosted_call(
        body, name, T // tm, [row, _full(g), _full(wgu)], [blocks(N_DEV), blocks(nh), row],
        [SDS((N_DEV, T, fb), BF16), SDS((nh, T, fb), BF16), SDS((T, D), BF16)], (x, g, wgu), gather=gather)


def _ffn_down(act, wd, res, name, gather=(), tm=1024):
    nh, T, fb = act.shape
    D = wd.shape[2]
    tm = _tile(T, tm, SUBLANES)

    def body(a_ref, w_ref, r_ref, o_ref):
        wv = w_ref[...].reshape(nh, fb, D)
        acc = _nn(a_ref[0], wv[0])
        for d in range(1, nh):
            acc = acc + _nn(a_ref[d], wv[d])
        o_ref[...] = r_ref[...] + 0.5 * acc

    row = pl.BlockSpec((tm, D), lambda i: (i, 0))
    return _hosted_call(
        body, name, T // tm, [pl.BlockSpec((nh, tm, fb), lambda i: (0, i, 0)), _full(wd), row], [row],
        [SDS((T, D), F32)], (act, wd, res), gather=gather)


def _ffn_bwd_act(dyb, wd, ab, name, tm=512):
    T, D = dyb.shape
    _, _, fb = ab.shape
    nh = N_DEV // 2
    tm = _tile(T, tm, SUBLANES)

    def body(dy_ref, w_ref, ab_ref, o_ref):
        wv = w_ref[...].reshape(nh, fb, D)
        dy = dy_ref[...]
        for d in range(nh):
            da = _nt(dy, wv[d])
            o_ref[d] = (da * ab_ref[d].astype(F32)).astype(BF16)
            o_ref[d + nh] = (da * ab_ref[d + nh].astype(F32)).astype(BF16)

    blocks = pl.BlockSpec((N_DEV, tm, fb), lambda i: (0, i, 0))
    return pl.pallas_call(
        body, name=name, grid=(T // tm,),
        in_specs=[pl.BlockSpec((tm, D), lambda i: (i, 0)), _full(wd), blocks], out_specs=blocks,
        out_shape=SDS((N_DEV, T, fb), BF16), compiler_params=_cp("parallel"))(dyb, wd, ab)


def _ffn_dwd(act, dyb, name, tt=2048):
    nh, T, fb = act.shape
    D = dyb.shape[1]
    tt = _tile(T, tt, 2 * SUBLANES)
    nt = T // tt

    def body(a_ref, b_ref, o_ref, acc_ref):
        t = pl.program_id(1)

        @pl.when(t == 0)
        def _():
            acc_ref[...] = jnp.zeros_like(acc_ref)
        for s in range(2):
            acc_ref[s] += _tn(a_ref[s], b_ref[...])

        @pl.when(t == nt - 1)
        def _():
            o_ref[...] = (0.5 * acc_ref[...]).astype(BF16)

    return pl.pallas_call(
        body, name=name, grid=(nh // 2, nt),
        in_specs=[pl.BlockSpec((2, tt, fb), lambda j, t: (j, t, 0)), pl.BlockSpec((tt, D), lambda j, t: (t, 0))],
        out_specs=pl.BlockSpec((2, fb, D), lambda j, t: (j, 0, 0)), out_shape=SDS((nh, fb, D), BF16),
        scratch_shapes=[pltpu.VMEM((2, fb, D), F32)], compiler_params=_cp("parallel", "arbitrary"))(act, dyb)


def _ffn_dwgu(h, dgu, name, tt=2048):
    T, D = h.shape
    _, _, fb = dgu.shape
    tt = _tile(T, tt, 2 * SUBLANES)
    nt = T // tt

    def body(a_ref, b_ref, o_ref, acc_ref):
        t = pl.program_id(1)

        @pl.when(t == 0)
        def _():
            acc_ref[...] = jnp.zeros_like(acc_ref)
        for s in range(2):
            acc_ref[s] += _tn(a_ref[...], b_ref[s])

        @pl.when(t == nt - 1)
        def _():
            o_ref[...] = acc_ref[...].astype(BF16)

    return pl.pallas_call(
        body, name=name, grid=(N_DEV // 2, nt),
        in_specs=[pl.BlockSpec((tt, D), lambda j, t: (t, 0)), pl.BlockSpec((2, tt, fb), lambda j, t: (j, t, 0))],
        out_specs=pl.BlockSpec((2, D, fb), lambda j, t: (j, 0, 0)), out_shape=SDS((N_DEV, D, fb), BF16),
        scratch_shapes=[pltpu.VMEM((2, D, fb), F32)], compiler_params=_cp("parallel", "arbitrary"))(h, dgu)


def _ffn_bwd_in(dgu, wgu, x, g, dres, name, scatter, tm=512):
    T, D = x.shape
    _, _, fb = dgu.shape
    tm = _tile(T, tm, SUBLANES)

    def body(a_ref, w_ref, x_ref, g_ref, r_ref, dx_ref, dxb_ref, dg_ref):
        dh = _nt(a_ref[0], w_ref[0])
        for d in range(1, N_DEV):
            dh = dh + _nt(a_ref[d], w_ref[d])
        _norm_bwd_store(dh, x_ref, g_ref, r_ref, dx_ref, dxb_ref, dg_ref, pl.program_id(0) == 0)

    row = pl.BlockSpec((tm, D), lambda i: (i, 0))
    return _hosted_call(
        body, name, T // tm, [pl.BlockSpec((N_DEV, tm, fb), lambda i: (0, i, 0)), _full(wgu), row, _full(g), row],
        [row, row, pl.BlockSpec((1, D), lambda i: (0, 0))],
        [SDS((T, D), F32), SDS((T, D), BF16), SDS((1, D), F32)], (dgu, wgu, x, g, dres), scatter)


def _loss_head(x, g, tgt, name, tm=512):
    T, D = x.shape
    tm = _tile(T, tm, SUBLANES)

    def body(x_ref, g_ref, t_ref, loss_ref, dx_ref, dxb_ref, dg_ref):
        first = pl.program_id(0) == 0
        xv = x_ref[...]
        r = _rstd(xv)
        xh = xv * r
        e = xh * g_ref[...] - t_ref[...]
        part = 0.5 * jnp.sum(jnp.mean(e * e, axis=-1, keepdims=True), axis=0, keepdims=True)
        dy = e * (1.0 / D)
        dxh = dy * g_ref[...]
        dx = r * (dxh - xh * jnp.mean(dxh * xh, axis=-1, keepdims=True))
        dx_ref[...] = dx
        dxb_ref[...] = dx.astype(BF16)
        _accumulate(loss_ref, first, jnp.broadcast_to(part, loss_ref.shape))
        _accumulate(dg_ref, first, jnp.sum(dy * xh, axis=0, keepdims=True))

    row = pl.BlockSpec((tm, D), lambda i: (i, 0))
    vec = pl.BlockSpec((1, D), lambda i: (0, 0))
    return pl.pallas_call(
        body, name=name, grid=(T // tm,), in_specs=[row, vec, row],
        out_specs=[pl.BlockSpec((SUBLANES, LANES), lambda i: (0, 0)), row, row, vec],
        out_shape=[SDS((SUBLANES, LANES), F32), SDS((T, D), F32), SDS((T, D), BF16), SDS((1, D), F32)],
        compiler_params=_cp("arbitrary"))(x, g, tgt)


def _shift_down(u, k, prev_row):
    rows = lax.broadcasted_iota(I32, u.shape, 0)
    s = pltpu.roll(u, k, 0)
    for t in range(k):
        s = jnp.where(rows == t, prev_row(SUBLANES - k + t), s)
    return s


def _shift_up(u, k, next_row):
    n = u.shape[0]
    rows = lax.broadcasted_iota(I32, u.shape, 0)
    s = pltpu.roll(u, n - k, 0)
    for t in range(k):
        s = jnp.where(rows == n - k + t, next_row(t), s)
    return s


def _conv_taps(z_ref, zp_ref, w_ref, cw, first):
    u = z_ref[:, cw:2 * cw] * z_ref[:, 2 * cw:]

    def prev_row(r):
        return jnp.where(first, 0.0, zp_ref[r:r + 1, cw:2 * cw] * zp_ref[r:r + 1, 2 * cw:])

    u1 = _shift_down(u, 1, prev_row)
    u2 = _shift_down(u, 2, prev_row)
    cv = w_ref[0:1, :] * u2 + w_ref[1:2, :] * u1 + w_ref[2:3, :] * u
    return u, u1, u2, cv


def _conv_fwd(zc, wc, gc, name, tm=512):
    T = zc.shape[0]
    cw = zc.shape[1] // 3
    tm = _tile(T, tm, SUBLANES)
    hb = tm // SUBLANES

    def body(z_ref, zp_ref, w_ref, g_ref, o_ref):
        _, _, _, cv = _conv_taps(z_ref, zp_ref, w_ref, cw, pl.program_id(0) == 0)
        y = z_ref[:, :cw] * cv
        o_ref[...] = (y * _rstd(y) * g_ref[...]).astype(BF16)

    return pl.pallas_call(
        body, name=name, grid=(T // tm,),
        in_specs=[pl.BlockSpec((tm, 3 * cw), lambda i: (i, 0)),
                  pl.BlockSpec((SUBLANES, 3 * cw), lambda i: (jnp.maximum(i * hb - 1, 0), 0)), _full(wc), _full(gc)],
        out_specs=pl.BlockSpec((tm, cw), lambda i: (i, 0)), out_shape=SDS((T, cw), BF16),
        compiler_params=_cp("parallel"))(zc, zc, wc, gc)


def _mix_bwd_head(dyb, w_out, zc, wc, gc, ot, g_col, name):
    nb, aw, tm = ot.shape
    T, D = dyb.shape
    cw = zc.shape[1] // 3
    H = aw // HEAD_DIM
    hb = tm // SUBLANES

    def body(dyb_ref, wo_ref, z_ref, zp_ref, w_ref, g_ref, o_ref, ga_ref,
             dz_ref, dw_ref, dg_ref, do_ref, dl_ref, dga_ref, carry_ref):
        s = pl.program_id(0)
        dy = _nt(dyb_ref[...], _rows2d(wo_ref))
        ov = o_ref[...]
        ra = lax.rsqrt(jnp.mean(ov * ov, axis=0, keepdims=True) + EPS)
        oh = ov * ra
        dya = dy[:, cw:].T
        dyah = dya * ga_ref[...]
        dov = ra * (dyah - oh * jnp.mean(dyah * oh, axis=0, keepdims=True))
        do_ref[...] = dov.astype(BF16)
        dl_ref[...] = jnp.sum((dov * ov).reshape(H, HEAD_DIM, tm), axis=1)
        _accumulate(dga_ref, s == 0, jnp.broadcast_to(jnp.sum(dya * oh, axis=1, keepdims=True), dga_ref.shape))
        u, u1, u2, cv = _conv_taps(z_ref, zp_ref, w_ref, cw, s == nb - 1)
        zb = z_ref[:, :cw]
        y = zb * cv
        r = _rstd(y)
        yh = y * r
        dyn = dy[:, :cw]
        dyh = dyn * g_ref[...]
        dyc = r * (dyh - yh * jnp.mean(dyh * yh, axis=-1, keepdims=True))
        dcv = dyc * zb

        def next_row(t):
            return jnp.where(s == 0, 0.0, carry_ref[t:t + 1, :])

        du = w_ref[2:3, :] * dcv + w_ref[1:2, :] * _shift_up(dcv, 1, next_row) + w_ref[0:1, :] * _shift_up(dcv, 2, next_row)
        carry_ref[...] = dcv[0:SUBLANES, :]
        dz_ref[:, :cw] = (dyc * cv).astype(BF16)
        dz_ref[:, cw:2 * cw] = (du * z_ref[:, 2 * cw:]).astype(BF16)
        dz_ref[:, 2 * cw:] = (du * z_ref[:, cw:2 * cw]).astype(BF16)
        tap = lax.broadcasted_iota(I32, (SUBLANES, cw), 0)
        dwp = jnp.where(tap == 0, jnp.sum(dcv * u2, axis=0, keepdims=True),
                        jnp.where(tap == 1, jnp.sum(dcv * u1, axis=0, keepdims=True),
                                  jnp.where(tap == 2, jnp.sum(dcv * u, axis=0, keepdims=True), 0.0)))
        _accumulate(dw_ref, s == 0, dwp)
        _accumulate(dg_ref, s == 0, jnp.sum(dyn * yh, axis=0, keepdims=True))

    rev = lambda s: nb - 1 - s
    rows = lambda n: pl.BlockSpec((tm, n), lambda s: (rev(s), 0))
    blk = pl.BlockSpec((None, aw, tm), lambda s: (rev(s), 0, 0))
    return pl.pallas_call(
        body, name=name, grid=(nb,),
        in_specs=[rows(D), _full(w_out), rows(3 * cw),
                  pl.BlockSpec((SUBLANES, 3 * cw), lambda s: (jnp.maximum(rev(s) * hb - 1, 0), 0)),
                  _full(wc), _full(gc), blk, _full(g_col)],
        out_specs=[rows(3 * cw), pl.BlockSpec((SUBLANES, cw), lambda s: (0, 0)), pl.BlockSpec((1, cw), lambda s: (0, 0)),
                   blk, pl.BlockSpec((None, H, tm), lambda s: (rev(s), 0, 0)),
                   pl.BlockSpec((aw, LANES), lambda s: (0, 0))],
        out_shape=[SDS((T, 3 * cw), BF16), SDS((SUBLANES, cw), F32), SDS((1, cw), F32),
                   SDS((nb, aw, tm), BF16), SDS((nb, H, tm), F32), SDS((aw, LANES), F32)],
        scratch_shapes=[pltpu.VMEM((SUBLANES, cw), F32)],
        compiler_params=_cp("arbitrary"))(dyb, w_out, zc, zc, wc, gc, ot, g_col)


def _proj_t(h, wat, wft, B, name):
    T, D = h.shape
    R = wat.shape[0]
    nb = T // B
    qk = 2 * R // 3
    g = qk // HEAD_DIM

    def body(h_ref, wa_ref, wf_ref, za_ref, zf_ref, n_ref):
        hv = h_ref[...]
        zab = _nt(wa_ref[...], hv).astype(BF16)
        za_ref[...] = zab
        zf_ref[...] = _nt(wf_ref[...], hv)
        zv = zab[:qk].astype(F32)
        ss = jnp.sum((zv * zv).reshape(g, HEAD_DIM, B), axis=1)
        mx = jnp.broadcast_to(jnp.max(ss, axis=1, keepdims=True), n_ref.shape)

        @pl.when(pl.program_id(0) == 0)
        def _():
            n_ref[...] = mx

        @pl.when(pl.program_id(0) > 0)
        def _():
            n_ref[...] = jnp.maximum(n_ref[...], mx)

    return pl.pallas_call(
        body, name=name, grid=(nb,), in_specs=[pl.BlockSpec((B, D), lambda i: (i, 0)), _full(wat), _full(wft)],
        out_specs=[pl.BlockSpec((None, R, B), lambda i: (i, 0, 0)), pl.BlockSpec((SUBLANES, B), lambda i: (0, i)),
                   pl.BlockSpec((g, LANES), lambda i: (0, 0))],
        out_shape=[SDS((nb, R, B), BF16), SDS((SUBLANES, T), F32), SDS((g, LANES), F32)],
        compiler_params=_cp("arbitrary"))(h, wat, wft)


def _split3(v):
    hi = v.astype(BF16).astype(F32)
    r1 = v - hi
    mid = r1.astype(BF16).astype(F32)
    lo = (r1 - mid).astype(BF16).astype(F32)
    return hi, mid, lo


def _tri_dot3(v, tri):
    hi, mid, lo = _split3(v)
    return _nn(hi.astype(BF16), tri) + _nn(mid.astype(BF16), tri) + _nn(lo.astype(BF16), tri)


def _logf_cumsum(zft, b, name, tb=512):
    H, T = zft.shape
    tb = _tile(T, tb, LANES)

    def body(z_ref, b_ref, c_ref, hi_ref, mid_ref, lo_ref, carry_ref):
        @pl.when(pl.program_id(0) == 0)
        def _():
            carry_ref[...] = jnp.zeros_like(carry_ref)
        xv = z_ref[...] + b_ref[...]
        lf = jnp.minimum(xv, 0.0) - jnp.log(1.0 + jnp.exp(-jnp.abs(xv)))
        src = lax.broadcasted_iota(I32, (tb, tb), 0)
        dst = lax.broadcasted_iota(I32, (tb, tb), 1)
        tri = jnp.where(src <= dst, 1.0, 0.0).astype(BF16)
        cs = _tri_dot3(lf, tri) + carry_ref[...]
        c_ref[...] = cs
        hi, mid, lo = _split3(cs)
        hi_ref[...] = hi
        mid_ref[...] = mid
        lo_ref[...] = lo
        col = lax.broadcasted_iota(I32, cs.shape, 1)
        carry_ref[...] = jnp.sum(jnp.where(col == tb - 1, cs, 0.0), axis=1, keepdims=True)

    blk = pl.BlockSpec((H, tb), lambda i: (0, i))
    return pl.pallas_call(
        body, name=name, grid=(T // tb,), in_specs=[blk, _full(b)],
        out_specs=[blk, blk, blk, blk], out_shape=[SDS((H, T), F32)] * 4,
        scratch_shapes=[pltpu.VMEM((H, 1), F32)], compiler_params=_cp("arbitrary"))(zft, b)


def _logf_cumsum_bwd(dcq, dck, zft, b, name, tb=512):
    H, T = zft.shape
    tb = _tile(T, tb, LANES)
    nb = T // tb

    def body(dq_ref, dk_ref, z_ref, b_ref, o_ref, db_ref, carry_ref):
        s = pl.program_id(0)

        @pl.when(s == 0)
        def _():
            carry_ref[...] = jnp.zeros_like(carry_ref)
        dc = dq_ref[...] - dk_ref[...]
        src = lax.broadcasted_iota(I32, (tb, tb), 0)
        dst = lax.broadcasted_iota(I32, (tb, tb), 1)
        tri = jnp.where(src >= dst, 1.0, 0.0).astype(BF16)
        dl = _tri_dot3(dc, tri) + carry_ref[...]
        col = lax.broadcasted_iota(I32, dl.shape, 1)
        carry_ref[...] = jnp.sum(jnp.where(col == 0, dl, 0.0), axis=1, keepdims=True)
        dz = dl * jax.nn.sigmoid(-(z_ref[...] + b_ref[...]))
        o_ref[...] = dz
        _accumulate(db_ref, s == 0, jnp.sum(dz, axis=1, keepdims=True))

    blk = pl.BlockSpec((H, tb), lambda s: (0, nb - 1 - s))
    vec = pl.BlockSpec((H, 1), lambda s: (0, 0))
    return pl.pallas_call(
        body, name=name, grid=(nb,), in_specs=[blk, blk, blk, vec], out_specs=[blk, vec],
        out_shape=[SDS((H, T), F32), SDS((H, 1), F32)],
        scratch_shapes=[pltpu.VMEM((H, 1), F32)], compiler_params=_cp("arbitrary"))(dcq, dck, zft, b)


def _skip_table(cs_col, ce_row, norms, name):
    H, nb, _ = cs_col.shape

    def body(cs_ref, ce_ref, n_ref, jm_ref, im_ref):
        h = pl.program_id(0)
        nq = n_ref[pl.ds(h, 1), 0:1]
        nk = n_ref[pl.ds(H + h, 1), 0:1]
        bound = 2.0 * jnp.sqrt(nq * nk) * 0.125
        skip = jnp.where(bound + cs_ref[...] - ce_ref[...] <= -SKIP_MARGIN, 1, 0).astype(I32)
        jm_ref[...] = jnp.sum(skip, axis=1, keepdims=True)
        im_ref[...] = nb - 1 - jnp.sum(skip, axis=0, keepdims=True)

    col = pl.BlockSpec((None, nb, 1), lambda h: (h, 0, 0))
    row = pl.BlockSpec((None, 1, nb), lambda h: (h, 0, 0))
    return pl.pallas_call(
        body, name=name, grid=(H,), in_specs=[col, row, _full(norms)], out_specs=[col, row],
        out_shape=[SDS((H, nb, 1), I32), SDS((H, 1, nb), I32)], compiler_params=_cp("parallel"))(cs_col, ce_row, norms)


def _causal_mask(B):
    krow = lax.broadcasted_iota(I32, (B, B), 0)
    qcol = lax.broadcasted_iota(I32, (B, B), 1)
    return krow <= qcol


def _attn_fwd(jmin, zat, qa, ka, name):
    nb, R, B = zat.shape
    aw = R // 3
    H = aw // HEAD_DIM
    hd = HEAD_DIM

    def body(jm_ref, q_ref, k_ref, v_ref, qa_ref, ka_ref, o_ref, lse_ref):
        h = pl.program_id(0)
        i = pl.program_id(1)
        qt = jnp.concatenate([q_ref[...] * 0.125, qa_ref[...]], axis=0)

        def block(j, carry, masked):
            m, l, acc = carry
            s = _tn(jnp.concatenate([k_ref[j], ka_ref[j]], axis=0), qt)
            if masked:
                s = jnp.where(_causal_mask(B), s, NEG)
            m_new = jnp.maximum(m, jnp.max(s, axis=0, keepdims=True))
            p = jnp.exp(s - m_new)
            a = jnp.exp(m - m_new)
            l = a * l + jnp.sum(p, axis=0, keepdims=True)
            acc = a * acc + _nn(v_ref[j], p.astype(BF16))
            return m_new, l, acc

        init = (jnp.full((1, B), NEG, F32), jnp.zeros((1, B), F32), jnp.zeros((hd, B), F32))
        carry = lax.fori_loop(jm_ref[h * nb + i], i, lambda j, c: block(j, c, False), init)
        m, l, acc = block(i, carry, True)
        o_ref[...] = acc / l
        lse_ref[...] = m + jnp.log(l)

    grid_spec = pltpu.PrefetchScalarGridSpec(
        num_scalar_prefetch=1, grid=(H, nb),
        in_specs=[pl.BlockSpec((None, hd, B), lambda h, i, jm: (i, h, 0)),
                  pl.BlockSpec((nb, hd, B), lambda h, i, jm: (0, H + h, 0)),
                  pl.BlockSpec((nb, hd, B), lambda h, i, jm: (0, 2 * H + h, 0)),
                  pl.BlockSpec((None, None, AUG, B), lambda h, i, jm: (h, i, 0, 0)),
                  pl.BlockSpec((None, nb, AUG, B), lambda h, i, jm: (h, 0, 0, 0))],
        out_specs=[pl.BlockSpec((None, hd, B), lambda h, i, jm: (i, h, 0)),
                   pl.BlockSpec((None, None, 1, B), lambda h, i, jm: (h, i, 0, 0))])
    return pl.pallas_call(
        body, name=name, grid_spec=grid_spec,
        out_shape=[SDS((nb, aw, B), F32), SDS((H, nb, 1, B), F32)],
        compiler_params=_cp("parallel", "parallel"))(jmin, zat, zat, zat, qa, ka)


def _gnorm_t_fwd(ot, g_col, name):
    nb, aw, B = ot.shape

    def body(o_ref, g_ref, y_ref):
        ov = o_ref[...]
        r = lax.rsqrt(jnp.mean(ov * ov, axis=0, keepdims=True) + EPS)
        y_ref[...] = (ov * r * g_ref[...]).T.astype(BF16)

    return pl.pallas_call(
        body, name=name, grid=(nb,), in_specs=[pl.BlockSpec((None, aw, B), lambda i: (i, 0, 0)), _full(g_col)],
        out_specs=pl.BlockSpec((B, aw), lambda i: (i, 0)), out_shape=SDS((nb * B, aw), BF16),
        compiler_params=_cp("parallel"))(ot, g_col)


def _attn_bwd(imax, zat, qa, ka, dot, lse, delta, name):
    nb, R, B = zat.shape
    aw = R // 3
    H = aw // HEAD_DIM
    hd = HEAD_DIM

    def body(im_ref, k_ref, v_ref, ka_ref, q_ref, qa_ref, do_ref, lse_ref, dl_ref,
             dq_ref, dqa_ref, dk_ref, dka_ref, dv_ref):
        h = pl.program_id(0)
        j = pl.program_id(1)

        @pl.when(j == 0)
        def _():
            dq_ref[...] = jnp.zeros_like(dq_ref)
            dqa_ref[...] = jnp.zeros_like(dqa_ref)
        kt = jnp.concatenate([k_ref[...], ka_ref[...]], axis=0)
        vt = v_ref[...]

        def block(i, carry, masked):
            dk, dv = carry
            qt = jnp.concatenate([q_ref[i] * 0.125, qa_ref[i]], axis=0)
            dov = do_ref[i]
            s = _tn(kt, qt)
            if masked:
                s = jnp.where(_causal_mask(B), s, NEG)
            p = jnp.exp(s - lse_ref[i])
            dp = _tn(vt, dov)
            ds = (p * (dp - dl_ref[i, pl.ds(h, 1), :])).astype(BF16)
            dv = dv + _nt(dov, p.astype(BF16))
            dk = dk + _nt(qt, ds)
            r = _nn(kt, ds)
            dq_ref[i] += 0.125 * r[:hd]
            dqa_ref[i] += r[hd:]
            return dk, dv

        carry = block(j, (jnp.zeros((hd + AUG, B), F32), jnp.zeros((hd, B), F32)), True)
        dk, dv = lax.fori_loop(j + 1, im_ref[h * nb + j] + 1, lambda i, c: block(i, c, False), carry)
        dk_ref[...] = dk[:hd]
        dka_ref[...] = dk[hd:]
        dv_ref[...] = dv

    head_rows = lambda off: pl.BlockSpec((nb, hd, B), lambda h, j, im: (0, off + h, 0))
    key_rows = lambda off: pl.BlockSpec((None, hd, B), lambda h, j, im: (j, off + h, 0))
    aug_all = pl.BlockSpec((None, nb, AUG, B), lambda h, j, im: (h, 0, 0, 0))
    aug_one = pl.BlockSpec((None, None, AUG, B), lambda h, j, im: (h, j, 0, 0))
    grid_spec = pltpu.PrefetchScalarGridSpec(
        num_scalar_prefetch=1, grid=(H, nb),
        in_specs=[key_rows(H), key_rows(2 * H), aug_one, head_rows(0), aug_all, head_rows(0),
                  pl.BlockSpec((None, nb, 1, B), lambda h, j, im: (h, 0, 0, 0)),
                  pl.BlockSpec((nb, H, B), lambda h, j, im: (0, 0, 0))],
        out_specs=[head_rows(0), aug_all, key_rows(0), aug_one, key_rows(0)])
    return pl.pallas_call(
        body, name=name, grid_spec=grid_spec,
        out_shape=[SDS((nb, aw, B), F32), SDS((H, nb, AUG, B), F32), SDS((nb, aw, B), F32),
                   SDS((H, nb, AUG, B), F32), SDS((nb, aw, B), F32)],
        compiler_params=_cp("arbitrary", "arbitrary"))(imax, zat, zat, ka, zat, qa, dot, lse, delta)


def _mix_dwt(dq, dk, dv, dzft, h, name):
    nb, aw, B = dq.shape
    D = h.shape[1]

    def body(dq_ref, dk_ref, dv_ref, dzf_ref, h_ref, o_ref, of_ref):
        first = pl.program_id(0) == 0
        hv = h_ref[...]
        part = jnp.concatenate([_nn(r[...].astype(BF16), hv) for r in (dq_ref, dk_ref, dv_ref)], axis=0)
        _accumulate(o_ref, first, part)
        _accumulate(of_ref, first, _nn(dzf_ref[...].astype(BF16), hv))

    blk = pl.BlockSpec((None, aw, B), lambda i: (i, 0, 0))
    return pl.pallas_call(
        body, name=name, grid=(nb,),
        in_specs=[blk, blk, blk, pl.BlockSpec((SUBLANES, B), lambda i: (0, i)), pl.BlockSpec((B, D), lambda i: (i, 0))],
        out_specs=[pl.BlockSpec((3 * aw, D), lambda i: (0, 0)), pl.BlockSpec((SUBLANES, D), lambda i: (0, 0))],
        out_shape=[SDS((3 * aw, D), F32), SDS((SUBLANES, D), F32)],
        compiler_params=_cp("arbitrary"))(dq, dk, dv, dzft, h)


def _mix_bwd_in(dzc, dq, dk, dv, dzft, wc, wat, wft, x, g, dres, name, scatter):
    T, D = x.shape
    nb, aw, B = dq.shape

    def body(dzc_ref, dq_ref, dk_ref, dv_ref, dzf_ref, wc_ref, wa_ref, wf_ref, x_ref, g_ref, r_ref,
             dx_ref, dxb_ref, dg_ref):
        dh = _nt(dzc_ref[...], wc_ref[...])
        for s, r in enumerate((dq_ref, dk_ref, dv_ref)):
            dh = dh + _tn(r[...].astype(BF16), wa_ref[s * aw:(s + 1) * aw, :])
        dh = dh + _tn(dzf_ref[...].astype(BF16), wf_ref[...])
        _norm_bwd_store(dh, x_ref, g_ref, r_ref, dx_ref, dxb_ref, dg_ref, pl.program_id(0) == 0)

    blk = pl.BlockSpec((None, aw, B), lambda i: (i, 0, 0))
    row = lambda n: pl.BlockSpec((B, n), lambda i: (i, 0))
    return _hosted_call(
        body, name, nb,
        [row(dzc.shape[1]), blk, blk, blk, pl.BlockSpec((SUBLANES, B), lambda i: (0, i)),
         _full(wc), _full(wat), _full(wft), row(D), _full(g), row(D)],
        [row(D), row(D), pl.BlockSpec((1, D), lambda i: (0, 0))],
        [SDS((T, D), F32), SDS((T, D), BF16), SDS((1, D), F32)],
        (dzc, dq, dk, dv, dzft, wc, wat, wft, x, g, dres), scatter)


def _mem_kv(mem, g, wkv, name):
    M, D = mem.shape
    dx = wkv.shape[2]

    def body(m_ref, g_ref, w_ref, kv_ref, h_ref):
        mv = m_ref[...]
        h = (mv * _rstd(mv) * g_ref[...]).astype(BF16)
        h_ref[...] = h
        for s in range(N_DEV):
            kv_ref[s] = _nn(h, w_ref[s]).astype(BF16)

    return pl.pallas_call(
        body, name=name, grid=(1,), in_specs=[_full(mem), _full(g), _full(wkv)],
        out_specs=[pl.BlockSpec((N_DEV, M, dx), lambda i: (0, 0, 0)), pl.BlockSpec((M, D), lambda i: (0, 0))],
        out_shape=[SDS((N_DEV, M, dx), BF16), SDS((M, D), BF16)], compiler_params=_cp("arbitrary"))(mem, g, wkv)


def _mem_kv_bwd(dkv, hm, wkv, mem, g, name):
    M, D = mem.shape
    dx = wkv.shape[2]

    def body(dkv_ref, h_ref, w_ref, m_ref, g_ref, dw_ref, dg_ref):
        hv = h_ref[...]
        dh = jnp.zeros((M, D), F32)
        for s in range(N_DEV):
            d = dkv_ref[s].astype(BF16)
            dw_ref[s] = _tn(hv, d).astype(BF16)
            dh = dh + _nt(d, w_ref[s])
        mv = m_ref[...]
        dg_ref[...] = jnp.sum(dh * (mv * _rstd(mv)), axis=0, keepdims=True)

    return pl.pallas_call(
        body, name=name, grid=(1,), in_specs=[_full(dkv), _full(hm), _full(wkv), _full(mem), _full(g)],
        out_specs=[pl.BlockSpec((N_DEV, D, dx), lambda i: (0, 0, 0)), pl.BlockSpec((1, D), lambda i: (0, 0))],
        out_shape=[SDS((N_DEV, D, dx), BF16), SDS((1, D), F32)], compiler_params=_cp("arbitrary"))(dkv, hm, wkv, mem, g)


def _xattn_probs(q_ref, kv_ref, h, dx, scale):
    qh = q_ref[:, h * dx:(h + 1) * dx]
    kh = kv_ref[h]
    s = _nt(qh, kh) * scale
    p = jnp.exp(s - jnp.max(s, axis=-1, keepdims=True))
    return qh, kh, p / jnp.sum(p, axis=-1, keepdims=True)


def _xattn(x, g, wq, kv, wo, name, tm=512):
    T, D = x.shape
    dx = D // N_XHEADS
    scale = dx ** -0.5
    tm = _tile(T, tm, SUBLANES)

    def body(x_ref, g_ref, wq_ref, kv_ref, wo_ref, xo_ref, h_ref, q_ref, o_ref):
        xv = x_ref[...]
        h = (xv * _rstd(xv) * g_ref[...]).astype(BF16)
        h_ref[...] = h
        q_ref[...] = _nn(h, _rows2d(wq_ref)).astype(BF16)
        for hd in range(N_XHEADS):
            _, _, p = _xattn_probs(q_ref, kv_ref, hd, dx, scale)
            o_ref[:, hd * dx:(hd + 1) * dx] = _nn(p.astype(BF16), kv_ref[N_XHEADS + hd]).astype(BF16)
        xo_ref[...] = xv + _nn(o_ref[...], _rows2d(wo_ref))

    row = pl.BlockSpec((tm, D), lambda i: (i, 0))
    return pl.pallas_call(
        body, name=name, grid=(T // tm,), in_specs=[row, _full(g), _full(wq), _full(kv), _full(wo)],
        out_specs=[row] * 4, out_shape=[SDS((T, D), F32)] + [SDS((T, D), BF16)] * 3,
        compiler_params=_cp("parallel"))(x, g, wq, kv, wo)


def _xattn_bwd(dyb, dres, x, g, q, kv, wq, wo, name, tm=512):
    T, D = x.shape
    dx = D // N_XHEADS
    scale = dx ** -0.5
    tm = _tile(T, tm, SUBLANES)

    def body(dy_ref, r_ref, x_ref, g_ref, q_ref, kv_ref, wq_ref, wo_ref, dx_ref, dxb_ref, dg_ref, dq_ref, dkv_ref):
        first = pl.program_id(0) == 0

        @pl.when(first)
        def _():
            dkv_ref[...] = jnp.zeros_like(dkv_ref)
        do = _nt(dy_ref[...], _rows2d(wo_ref)).astype(BF16)
        for hd in range(N_XHEADS):
            qh, kh, p = _xattn_probs(q_ref, kv_ref, hd, dx, scale)
            doh = do[:, hd * dx:(hd + 1) * dx]
            dp = _nt(doh, kv_ref[N_XHEADS + hd])
            ds = (p * (dp - jnp.sum(p * dp, axis=-1, keepdims=True)) * scale).astype(BF16)
            dq_ref[:, hd * dx:(hd + 1) * dx] = _nn(ds, kh).astype(BF16)
            dkv_ref[hd] += _tn(ds, qh)
            dkv_ref[N_XHEADS + hd] += _tn(p.astype(BF16), doh)
        dh = _nt(dq_ref[...], _rows2d(wq_ref))
        _norm_bwd_store(dh, x_ref, g_ref, r_ref, dx_ref, dxb_ref, dg_ref, first)

    row = pl.BlockSpec((tm, D), lambda i: (i, 0))
    return pl.pallas_call(
        body, name=name, grid=(T // tm,),
        in_specs=[row, row, row, _full(g), row, _full(kv), _full(wq), _full(wo)],
        out_specs=[row, row, pl.BlockSpec((1, D), lambda i: (0, 0)), row, pl.BlockSpec(kv.shape, lambda i: (0, 0, 0))],
        out_shape=[SDS((T, D), F32), SDS((T, D), BF16), SDS((1, D), F32), SDS((T, D), BF16), SDS(kv.shape, F32)],
        compiler_params=_cp("arbitrary"))(dyb, dres, x, g, q, kv, wq, wo)


def _mesh_position():
    return lax.axis_index("x"), lax.axis_index("y"), lax.axis_index("c")


class _GatherCopies:
    def __init__(self, x_refs, out_refs, send_sems, recv_sems, local_sems):
        self.x_refs, self.out_refs, self.n = x_refs, out_refs, len(x_refs)
        self.send_sems, self.recv_sems, self.local_sems = send_sems, recv_sems, local_sems
        x, y, c = _mesh_position()
        self.c = c
        self.me, self.sibling = (x, y, c), (x, y, 1 - c)
        self.chips = [(1 - x, y), (x, 1 - y), (1 - x, 1 - y)]

    def slot(self, w, px, py, pc):
        return self.out_refs[w].at[4 * px + 2 * py + pc]

    def copy(self, w, k, block, to, src=None):
        return pltpu.make_async_remote_copy(
            src_ref=self.slot(w, *block) if src is None else src, dst_ref=self.slot(w, *block),
            send_sem=self.send_sems.at[w, k], recv_sem=self.recv_sems.at[w, k], device_id=to,
            device_id_type=pl.DeviceIdType.MESH)

    def mine(self, w):
        return pltpu.make_async_copy(self.x_refs[w], self.slot(w, *self.me), self.local_sems.at[w])

    def first(self, w):
        src = self.x_refs[w]
        return [self.copy(w, 0, self.me, self.sibling, src=src)] + [
            self.copy(w, 1 + j, self.me, (*chip, self.c), src=src) for j, chip in enumerate(self.chips)]

    def passed(self, w):
        return [self.copy(w, 4 + j, (*chip, self.c), self.sibling) for j, chip in enumerate(self.chips)]


def _gather_start(*refs):
    cp = _GatherCopies(*refs)
    for w in range(cp.n):
        cp.mine(w).start()
        for f in cp.first(w):
            f.start()


def _gather_relay(*refs):
    cp = _GatherCopies(*refs)
    for w in range(cp.n):
        for j, (chip, fwd) in enumerate(zip(cp.chips, cp.passed(w))):
            cp.copy(w, 1 + j, (*chip, cp.c), cp.me).wait_recv()
            fwd.start()


def _gather_finish(*refs):
    cp = _GatherCopies(*refs)
    for w in range(cp.n):
        cp.copy(w, 0, cp.sibling, cp.me).wait_recv()
        for j, chip in enumerate(cp.chips):
            cp.copy(w, 4 + j, (*chip, 1 - cp.c), cp.me).wait_recv()
    for w in range(cp.n):
        for f in cp.first(w) + cp.passed(w):
            f.wait_send()
        cp.mine(w).wait()


def _all_gather(shards, name):
    n = len(shards)

    def body(*refs):
        parts = (refs[:n], refs[n:2 * n]) + refs[2 * n:]
        _gather_start(*parts)
        _gather_relay(*parts)
        _gather_finish(*parts)

    any_spec = pl.BlockSpec(memory_space=pl.ANY)
    return pl.pallas_call(
        body, name=name, out_shape=[SDS((N_DEV,) + s.shape, s.dtype) for s in shards],
        in_specs=[any_spec] * n, out_specs=[any_spec] * n,
        scratch_shapes=[pltpu.SemaphoreType.DMA((n, 7)), pltpu.SemaphoreType.DMA((n, 7)), pltpu.SemaphoreType.DMA((n,))],
    )(*shards)


def _scatter_copies(x_refs, out_refs, send_sems, recv_sems, local_sems):
    x, y, c = _mesh_position()
    me = 4 * x + 2 * y + c
    own, remote = [], []
    for w, (src, dst) in enumerate(zip(x_refs, out_refs)):
        own.append(pltpu.make_async_copy(src.at[me], dst.at[me], local_sems.at[w]))
        for k in range(1, N_DEV):
            px = 1 - x if k & 4 else x
            py = 1 - y if k & 2 else y
            pc = 1 - c if k & 1 else c
            remote.append(pltpu.make_async_remote_copy(
                src_ref=src.at[4 * px + 2 * py + pc], dst_ref=dst.at[me], send_sem=send_sems.at[w, k - 1],
                recv_sem=recv_sems.at[w, k - 1], device_id=(px, py, pc), device_id_type=pl.DeviceIdType.MESH))
    return own, remote


def _scatter_start(x_refs, out_refs, send_sems, recv_sems, local_sems):
    own, remote = _scatter_copies(x_refs, out_refs, send_sems, recv_sems, local_sems)
    for cp in own + remote:
        cp.start()


def _scatter_wait(x_refs, out_refs, send_sems, recv_sems, local_sems):
    own, remote = _scatter_copies(x_refs, out_refs, send_sems, recv_sems, local_sems)
    for cp in remote:
        cp.wait_recv()
    for cp in remote:
        cp.wait_send()
    for cp in own:
        cp.wait()


def _sum_adamw(parts, w, m, v, name, tr=256):
    L, R, C = w.shape
    tr = _tile(R, tr, SUBLANES)
    nblk = R // tr

    def body(*refs):
        p_refs = refs[:L]
        w_ref, m_ref, v_ref, g_ref, d_ref, mo_ref, vo_ref = refs[L:]
        layer = pl.program_id(0)
        g = None
        for k, p_ref in enumerate(p_refs):
            gk = p_ref[0].astype(F32)
            for s in range(1, N_DEV):
                gk = gk + p_ref[s].astype(F32)
            g = gk if g is None else jnp.where(layer == k, gk, g)
        mn = ADAM_B1 * m_ref[...] + (1.0 - ADAM_B1) * g
        vn = ADAM_B2 * v_ref[...] + (1.0 - ADAM_B2) * jnp.square(g)
        m_hat = mn / (1.0 - ADAM_B1 ** ADAM_STEP)
        v_hat = vn / (1.0 - ADAM_B2 ** ADAM_STEP)
        g_ref[...] = g
        d_ref[...] = -ADAM_LR * (m_hat / (jnp.sqrt(v_hat) + ADAM_EPS) + ADAM_WD * w_ref[...])
        mo_ref[...] = mn
        vo_ref[...] = vn

    def part_spec(k):
        return pl.BlockSpec((N_DEV, tr, C),
                            lambda l, i: (0, jnp.where(l == k, i, jnp.where(l < k, 0, nblk - 1)), 0))

    row = pl.BlockSpec((None, tr, C), lambda l, i: (l, i, 0))
    return pl.pallas_call(
        body, name=name, grid=(L, nblk), in_specs=[part_spec(k) for k in range(L)] + [row, row, row],
        out_specs=[row] * 4, out_shape=[SDS((L, R, C), F32)] * 4,
        compiler_params=_cp("arbitrary", "arbitrary"))(*parts, w, m, v)


def _pack(flat_list, row_mult):
    flat = jnp.concatenate(flat_list, axis=-1)
    n = flat.shape[-1]
    chunk = row_mult * PACK_COLS
    pad = -n % chunk
    flat = jnp.pad(flat, [(0, 0)] * (flat.ndim - 1) + [(0, pad)])
    return flat.reshape(flat.shape[:-1] + ((n + pad) // PACK_COLS, PACK_COLS))


def _unpack(packed, shapes):
    lead = packed.shape[:-2]
    flat = packed.reshape(lead + (-1,))
    out = []
    off = 0
    for shp in shapes:
        n = 1
        for d in shp:
            n *= d
        out.append(flat[..., off:off + n].reshape(lead + tuple(shp)))
        off += n
    return out


def _ffn_bwd(dx, dxb, saved, wgu, wd, tag):
    x, g, h, ab, act = saved
    dgu = _ffn_bwd_act(dxb, wd, ab, f"{tag}_bwd_act")
    dwd = _ffn_dwd(act, dxb, f"{tag}_dwd")
    dwgu = _ffn_dwgu(h, dgu, f"{tag}_dwgu")
    return _ffn_bwd_in(dgu, wgu, x, g, dx, f"{tag}_bwd_in", [dwgu, dwd.reshape(N_DEV, -1, dwd.shape[-1])])


def _aug_rows(rows, H, nb, B):
    a = jnp.stack(rows + [jnp.zeros_like(rows[0])] * (AUG - len(rows)), axis=1)
    return a.reshape(H, AUG, nb, B).transpose(0, 2, 1, 3).astype(BF16)


def _mix_fwd(x, g, w_in8, w_conv8, b_f, g_conv, g_att, w_out, tag):
    T, D = x.shape
    w_in = w_in8.transpose(1, 0, 2).reshape(D, -1)
    cw = D // 2
    aw = D - cw
    H = aw // HEAD_DIM
    B = _tile(T, ATT_BLOCK, LANES)
    nb = T // B
    w_c = w_in[:, :3 * cw]
    w_at = w_in[:, 3 * cw:3 * cw + 3 * aw].T
    w_ft = jnp.pad(w_in[:, 3 * cw + 3 * aw:].T, ((0, SUBLANES - H), (0, 0)))
    zc, h = _norm_mm(x, g, w_c, F32, f"{tag}_in_conv")
    zat, zft8, norms = _proj_t(h, w_at, w_ft, B, f"{tag}_in_att")
    zft = zft8[:H]
    bcol = b_f.reshape(H, 1)
    c, chi, cmid, clo = _logf_cumsum(zft, bcol, f"{tag}_cumsum")
    one = jnp.ones_like(c)
    qa = _aug_rows([one, one, one, chi, cmid, clo], H, nb, B)
    ka = _aug_rows([-chi, -cmid, -clo, one, one, one], H, nb, B)
    cb = c.reshape(H, nb, B)
    jmin, imax = _skip_table(cb[:, :, 0:1], cb[:, :, B - 1].reshape(H, 1, nb), norms, f"{tag}_skip")
    ot, lse = _attn_fwd(jmin.reshape(H * nb), zat, qa, ka, f"{tag}_attn")
    g_col = g_att.reshape(aw, 1)
    yna = _gnorm_t_fwd(ot, g_col, f"{tag}_gnorm")
    ync = _conv_fwd(zc, w_conv8, g_conv, f"{tag}_conv")
    xo = _mm_res([ync, yna], w_out, x, f"{tag}_out")
    saved = (x, g, h, zc, zft, bcol, zat, qa, ka, imax.reshape(H * nb), ot, lse, ync, yna, w_c, w_at, w_ft, g_col)
    return xo, saved


def _mix_bwd(dx, dxb, saved, w_conv8, g_conv, w_out, tag, also_scatter):
    x, g, h, zc, zft, bcol, zat, qa, ka, imax, ot, lse, ync, yna, w_c, w_at, w_ft, g_col = saved
    T, D = x.shape
    H = zft.shape[0]
    dw_out = jnp.concatenate([_mm_tn(ync, dxb, f"{tag}_dwout_c"), _mm_tn(yna, dxb, f"{tag}_dwout_a")], axis=0)
    dzc, dwc8, dg_conv, dot, delta, dg_att = _mix_bwd_head(dxb, w_out, zc, w_conv8, g_conv, ot, g_col, f"{tag}_bwd_out")
    dq, dqa, dk, dka, dv = _attn_bwd(imax, zat, qa, ka, dot, lse, delta, f"{tag}_attn_bwd")
    dcq = dqa[:, :, 3, :].reshape(H, T)
    dck = dka[:, :, 0, :].reshape(H, T)
    dzft, db = _logf_cumsum_bwd(dcq, dck, zft, bcol, f"{tag}_cumsum_bwd")
    dzft8 = jnp.pad(dzft, ((0, SUBLANES - H), (0, 0)))
    dw_c = _mm_tn(h, dzc, f"{tag}_dwin_c")
    dw_at, dw_ft = _mix_dwt(dq, dk, dv, dzft8, h, f"{tag}_dwin_a")
    dw_in = jnp.concatenate([dw_c, dw_at.T.astype(BF16), dw_ft[:H].T.astype(BF16)], axis=1)
    scatter = [dw_in.reshape(D, N_DEV, -1).transpose(1, 0, 2), dwc8[:3].reshape(3, N_DEV, -1).transpose(1, 0, 2),
               dw_out.reshape(N_DEV, D // N_DEV, D)]
    dx, dxb, dg, p_in, p_conv, p_out, *p_also = _mix_bwd_in(dzc, dq, dk, dv, dzft8, w_c, w_at, w_ft, x, g, dx,
                                                            f"{tag}_bwd_in", scatter + list(also_scatter))
    return dx, dxb, dg, p_in, p_conv, db.reshape(H), dg_conv, dg_att[:, 0], p_out, p_also


def _xattn_block_fwd(x, g, mem, g_mem, w_q, w_kv, w_o, tag):
    kv, hm = _mem_kv(mem, g_mem, w_kv, f"{tag}_kv")
    xo, h, q, o = _xattn(x, g, w_q, kv, w_o, f"{tag}_block")
    return xo, (x, g, h, mem, g_mem, hm, q, kv, o)


def _xattn_block_bwd(dx, dxb, saved, w_q, w_kv, w_o, tag):
    x, g, h, mem, g_mem, hm, q, kv, o = saved
    dw_o = _mm_tn(o, dxb, f"{tag}_dwo")
    dx, dxb, dg, dq, dkv = _xattn_bwd(dxb, dx, x, g, q, kv, w_q, w_o, f"{tag}_block_bwd")
    dw_q = _mm_tn(h, dq, f"{tag}_dwq")
    dw_kv, dg_mem = _mem_kv_bwd(dkv, hm, w_kv, mem, g_mem, f"{tag}_kv_bwd")
    D = dw_q.shape[0]
    return dx, dxb, dg, dg_mem, [dw_q.reshape(N_DEV, D // N_DEV, D), dw_kv, dw_o.reshape(N_DEV, D // N_DEV, D)]


GATHER_FIRST = ['w_ffn1_gu']
GATHER_REST = ['w_ffn1_down', 'w_mix_in', 'w_mix_out', 'w_xq', 'w_xkv', 'w_xo', 'w_ffn2_gu', 'w_ffn2_down']
GATHER_NEXT_A = ['w_ffn1_gu', 'w_ffn1_down', 'w_mix_in', 'w_mix_out', 'w_xq', 'w_xkv', 'w_xo']
GATHER_NEXT_B = ['w_ffn2_gu', 'w_ffn2_down']


def _local_step(x, mem, tgt, P):
    L = P['g_ffn1'].shape[0]
    shards = lambda names, l: [P[n][l].astype(BF16) for n in names]
    W = {n: [None] * L for n in SHARDED}

    def keep(names, l, arrays):
        for n, a in zip(names, arrays):
            W[n][l] = a

    *first, conv = _all_gather(shards(GATHER_FIRST, 0) + [P['w_conv']], "gather_first")
    keep(GATHER_FIRST, 0, first)
    w_conv = conv.transpose(1, 2, 0, 3).reshape(L, 3, -1)
    saved = []
    for l in range(L):
        row = lambda n: P[n][l][None, :]
        wc8 = jnp.pad(w_conv[l], ((0, SUBLANES - 3), (0, 0)))
        x1 = x
        bring = GATHER_REST if l == 0 else []
        ab, act, h, *got = _ffn_gu(x, row('g_ffn1'), W['w_ffn1_gu'][l], f"l{l}_ffn1_gu", gather=shards(bring, l))
        keep(bring, l, got)
        x, = _ffn_down(act, W['w_ffn1_down'][l], x, f"l{l}_ffn1_down")
        s1 = (x1, row('g_ffn1'), h, ab, act)
        x, s2 = _mix_fwd(x, row('g_mix'), W['w_mix_in'][l], wc8, P['b_f'][l], row('g_conv_out'), P['g_att_out'][l],
                         W['w_mix_out'][l], f"l{l}_mix")
        x, s3 = _xattn_block_fwd(x, row('g_xattn'), mem, row('g_mem'), W['w_xq'][l], W['w_xkv'][l], W['w_xo'][l],
                                 f"l{l}_xattn")
        x4 = x
        bring_a, bring_b = (GATHER_NEXT_A, GATHER_NEXT_B) if l + 1 < L else ([], [])
        ab, act, h, *got = _ffn_gu(x, row('g_ffn2'), W['w_ffn2_gu'][l], f"l{l}_ffn2_gu", gather=shards(bring_a, l + 1))
        keep(bring_a, l + 1, got)
        x, *got = _ffn_down(act, W['w_ffn2_down'][l], x, f"l{l}_ffn2_down", gather=shards(bring_b, l + 1))
        keep(bring_b, l + 1, got)
        saved.append((s1, s2, s3, (x4, row('g_ffn2'), h, ab, act), wc8))
    loss, dx, dxb, dg_final = _loss_head(x, P['g_final'][None, :], tgt, "loss_head")
    G = {n: [None] * L for n in WEIGHT_NAMES if n != 'g_final'}
    for l in reversed(range(L)):
        row = lambda n: P[n][l][None, :]
        s1, s2, s3, s4, wc8 = saved[l]
        dx, dxb, G['g_ffn2'][l], G['w_ffn2_gu'][l], G['w_ffn2_down'][l] = _ffn_bwd(
            dx, dxb, s4, W['w_ffn2_gu'][l], W['w_ffn2_down'][l], f"l{l}_ffn2")
        dx, dxb, G['g_xattn'][l], G['g_mem'][l], xattn_grads = _xattn_block_bwd(
            dx, dxb, s3, W['w_xq'][l], W['w_xkv'][l], W['w_xo'][l], f"l{l}_xattn")
        (dx, dxb, G['g_mix'][l], G['w_mix_in'][l], G['w_conv'][l], G['b_f'][l], G['g_conv_out'][l], G['g_att_out'][l],
         G['w_mix_out'][l], (G['w_xq'][l], G['w_xkv'][l], G['w_xo'][l])) = _mix_bwd(
            dx, dxb, s2, wc8, row('g_conv_out'), W['w_mix_out'][l], f"l{l}_mix", xattn_grads)
        dx, dxb, G['g_ffn1'][l], G['w_ffn1_gu'][l], G['w_ffn1_down'][l] = _ffn_bwd(
            dx, dxb, s1, W['w_ffn1_gu'][l], W['w_ffn1_down'][l], f"l{l}_ffn1")
    grads = {n: (G[n] if n in SHARDED else jnp.stack([a.reshape(P[n].shape[1:]) for a in G[n]])) for n in G}
    grads['g_final'] = dg_final.reshape(-1)
    return loss, dx, grads


def kernel(x, mem, g_ffn1, w_ffn1_gu, w_ffn1_down, g_mix, w_mix_in, w_conv, b_f, g_conv_out, g_att_out, w_mix_out, g_xattn, g_mem, w_xq, w_xkv, w_xo, g_ffn2, w_ffn2_gu, w_ffn2_down, g_final, loss_target, m_g_ffn1, m_w_ffn1_gu, m_w_ffn1_down, m_g_mix, m_w_mix_in, m_w_conv, m_b_f, m_g_conv_out, m_g_att_out, m_w_mix_out, m_g_xattn, m_g_mem, m_w_xq, m_w_xkv, m_w_xo, m_g_ffn2, m_w_ffn2_gu, m_w_ffn2_down, m_g_final, v_g_ffn1, v_w_ffn1_gu, v_w_ffn1_down, v_g_mix, v_w_mix_in, v_w_conv, v_b_f, v_g_conv_out, v_g_att_out, v_w_mix_out, v_g_xattn, v_g_mem, v_w_xq, v_w_xkv, v_w_xo, v_g_ffn2, v_w_ffn2_gu, v_w_ffn2_down, v_g_final):
    args = (x, mem, g_ffn1, w_ffn1_gu, w_ffn1_down, g_mix, w_mix_in, w_conv, b_f, g_conv_out, g_att_out, w_mix_out,
            g_xattn, g_mem, w_xq, w_xkv, w_xo, g_ffn2, w_ffn2_gu, w_ffn2_down, g_final)
    P = dict(zip(IN_NAMES, args))
    moms = (m_g_ffn1, m_w_ffn1_gu, m_w_ffn1_down, m_g_mix, m_w_mix_in, m_w_conv, m_b_f, m_g_conv_out, m_g_att_out,
            m_w_mix_out, m_g_xattn, m_g_mem, m_w_xq, m_w_xkv, m_w_xo, m_g_ffn2, m_w_ffn2_gu, m_w_ffn2_down, m_g_final)
    vars_ = (v_g_ffn1, v_w_ffn1_gu, v_w_ffn1_down, v_g_mix, v_w_mix_in, v_w_conv, v_b_f, v_g_conv_out, v_g_att_out,
             v_w_mix_out, v_g_xattn, v_g_mem, v_w_xq, v_w_xkv, v_w_xo, v_g_ffn2, v_w_ffn2_gu, v_w_ffn2_down, v_g_final)
    MOM = dict(zip(WEIGHT_NAMES, moms))
    VAR = dict(zip(WEIGHT_NAMES, vars_))

    loss_part, dx, grads = _local_step(x[0], mem[0], loss_target[0], {n: P[n] for n in WEIGHT_NAMES})

    small_shapes = [P[n].shape for n in REPLICATED] + [(1,)]
    small = _pack([grads[n].reshape(-1) for n in REPLICATED] + [loss_part[0, :1]], SUBLANES)
    small_parts = _all_gather([small], "gather_small_grads")[0]

    res = {n: _sum_adamw(grads[n], P[n], MOM[n], VAR[n], "adamw_" + n) for n in SHARDED}
    zero = [jnp.zeros((1,), F32)]
    w_, m_, v_ = (_pack([d[n].reshape(-1) for n in REPLICATED] + zero, SUBLANES)[None] for d in (P, MOM, VAR))
    small_out = [_unpack(o[0], small_shapes) for o in _sum_adamw([small_parts], w_, m_, v_, "adamw_vectors")]
    for k, n in enumerate(REPLICATED):
        res[n] = [o[k] for o in small_out]
    out = []
    for k in range(4):
        out += [res[n][k] for n in WEIGHT_NAMES]
    return (small_out[0][-1].reshape(()), dx[None], *out)
```

```python
import jax
import jax.numpy as jnp
from jax import lax
from jax.experimental import pallas as pl
from jax.experimental.pallas import tpu as pltpu

F32 = jnp.float32
BF16 = jnp.bfloat16
I32 = jnp.int32
SDS = jax.ShapeDtypeStruct

EPS = 1e-6
HEAD_DIM = 64
N_XHEADS = 4
N_DEV = 8
LANES = 128
SUBLANES = 8
AUG = 16
ATT_BLOCK = 512
NEG = -1e30
SKIP_MARGIN = 115.0
PACK_COLS = 1024
VMEM_LIMIT = 58 * 1024 * 1024

ADAM_LR = 0.001
ADAM_B1 = 0.9
ADAM_B2 = 0.999
ADAM_EPS = 1e-08
ADAM_WD = 0.01
ADAM_STEP = 10

IN_NAMES = ['x', 'mem', 'g_ffn1', 'w_ffn1_gu', 'w_ffn1_down', 'g_mix', 'w_mix_in', 'w_conv', 'b_f', 'g_conv_out',
            'g_att_out', 'w_mix_out', 'g_xattn', 'g_mem', 'w_xq', 'w_xkv', 'w_xo', 'g_ffn2', 'w_ffn2_gu',
            'w_ffn2_down', 'g_final']
WEIGHT_NAMES = IN_NAMES[2:]
SHARDED = ['w_ffn1_gu', 'w_ffn1_down', 'w_mix_in', 'w_conv', 'w_mix_out', 'w_xq', 'w_xkv', 'w_xo', 'w_ffn2_gu',
           'w_ffn2_down']
REPLICATED = [n for n in WEIGHT_NAMES if n not in SHARDED]


def _tile(n, pref, mult):
    t = min(pref, n) // mult * mult
    while t >= mult:
        if n % t == 0:
            return t
        t -= mult
    return n


def _cp(*sem):
    return pltpu.CompilerParams(dimension_semantics=sem, vmem_limit_bytes=VMEM_LIMIT)


def _nt(a, b):
    return lax.dot_general(a, b, (((1,), (1,)), ((), ())), preferred_element_type=F32)


def _tn(a, b):
    return lax.dot_general(a, b, (((0,), (0,)), ((), ())), preferred_element_type=F32)


def _nn(a, b):
    return jnp.dot(a, b, preferred_element_type=F32)


def _rstd(xv):
    return lax.rsqrt(jnp.mean(xv * xv, axis=-1, keepdims=True) + EPS)


def _full(a):
    nd = a.ndim
    return pl.BlockSpec(a.shape, lambda *_: (0,) * nd, pipeline_mode=pl.Buffered(1))


def _rows2d(w_ref):
    s, a, b = w_ref.shape
    return w_ref[...].reshape(s * a, b)


def _accumulate(ref, first, part):
    @pl.when(first)
    def _():
        ref[...] = part

    @pl.when(jnp.logical_not(first))
    def _():
        ref[...] += part


def _norm_bwd_store(dh, x_ref, g_ref, r_ref, dx_ref, dxb_ref, dg_ref, first):
    xv = x_ref[...]
    r = _rstd(xv)
    xh = xv * r
    dxh = dh * g_ref[...]
    dx = r * (dxh - xh * jnp.mean(dxh * xh, axis=-1, keepdims=True)) + r_ref[...]
    dx_ref[...] = dx
    dxb_ref[...] = dx.astype(BF16)
    _accumulate(dg_ref, first, jnp.sum(dh * xh, axis=0, keepdims=True))


def _mm_res(a_list, w, res, name, tm=512):
    T = res.shape[0]
    N = w.shape[-1]
    tm = _tile(T, tm, SUBLANES)
    n_a = len(a_list)

    def body(*refs):
        a_refs = refs[:n_a]
        w_ref, r_ref, o_ref = refs[n_a:]
        wv = _rows2d(w_ref)
        acc = r_ref[...]
        off = 0
        for a_ref in a_refs:
            k = a_ref.shape[1]
            acc = acc + _nn(a_ref[...], wv[off:off + k, :])
            off += k
        o_ref[...] = acc

    row = lambda n: pl.BlockSpec((tm, n), lambda i: (i, 0))
    return pl.pallas_call(
        body, name=name, grid=(T // tm,),
        in_specs=[row(a.shape[1]) for a in a_list] + [_full(w), row(N)], out_specs=row(N),
        out_shape=SDS((T, N), F32), compiler_params=_cp("parallel"))(*a_list, w, res)


def _mm_tn(a, b, name, out_dtype=BF16, tt=2048):
    T, K = a.shape
    N = b.shape[1]
    tt = _tile(T, tt, 2 * SUBLANES)
    nt = T // tt

    def body(a_ref, b_ref, o_ref, acc_ref):
        t = pl.program_id(0)
        _accumulate(acc_ref, t == 0, _tn(a_ref[...], b_ref[...]))

        @pl.when(t == nt - 1)
        def _():
            o_ref[...] = acc_ref[...].astype(o_ref.dtype)

    return pl.pallas_call(
        body, name=name, grid=(nt,),
        in_specs=[pl.BlockSpec((tt, K), lambda t: (t, 0)), pl.BlockSpec((tt, N), lambda t: (t, 0))],
        out_specs=pl.BlockSpec((K, N), lambda t: (0, 0)), out_shape=SDS((K, N), out_dtype),
        scratch_shapes=[pltpu.VMEM((K, N), F32)], compiler_params=_cp("arbitrary"))(a, b)


def _hosted_call(body, name, n_steps, in_specs, out_specs, out_shape, args, scatter=(), gather=(), scratch_shapes=()):
    scatter, gather, scratch_shapes = list(scatter), list(gather), list(scratch_shapes)
    n_s, n_g = len(scatter), len(gather)
    if n_s + n_g == 0:
        return pl.pallas_call(body, name=name, grid=(n_steps,), in_specs=list(in_specs), out_specs=list(out_specs),
                              out_shape=list(out_shape), scratch_shapes=scratch_shapes,
                              compiler_params=_cp("arbitrary"))(*args)
    n_in, n_out, n_scr = len(in_specs), len(out_specs), len(scratch_shapes)
    relay_step = min(max(1, 3 * n_steps // 4), n_steps - 1)

    def wrapped(*refs):
        ins, refs = refs[:n_in], refs[n_in:]
        xs, shards, refs = refs[:n_s], refs[n_s:n_s + n_g], refs[n_s + n_g:]
        outs, refs = refs[:n_out], refs[n_out:]
        recvs, gathered, refs = refs[:n_s], refs[n_s:n_s + n_g], refs[n_s + n_g:]
        outs, sems = outs + refs[:n_scr], refs[n_scr:]
        s_sems, g_sems = (sems[:3], sems[3:]) if n_s else ((), sems)
        step = pl.program_id(0)

        @pl.when(step == 0)
        def _():
            if n_s:
                _scatter_start(xs, recvs, *s_sems)
            if n_g:
                _gather_start(shards, gathered, *g_sems)
        body(*ins, *outs)
        if n_g:
            @pl.when(step == relay_step)
            def _():
                _gather_relay(shards, gathered, *g_sems)

        @pl.when(step == n_steps - 1)
        def _():
            if n_s:
                _scatter_wait(xs, recvs, *s_sems)
            if n_g:
                _gather_finish(shards, gathered, *g_sems)

    any_spec = pl.BlockSpec(memory_space=pl.ANY)
    sems = []
    for n in (n_s, n_g):
        if n:
            sems += [pltpu.SemaphoreType.DMA((n, N_DEV - 1)), pltpu.SemaphoreType.DMA((n, N_DEV - 1)),
                     pltpu.SemaphoreType.DMA((n,))]
    return pl.pallas_call(
        wrapped, name=name, grid=(n_steps,), in_specs=list(in_specs) + [any_spec] * (n_s + n_g),
        out_specs=list(out_specs) + [any_spec] * (n_s + n_g),
        out_shape=list(out_shape) + [SDS(b.shape, b.dtype) for b in scatter]
        + [SDS((N_DEV,) + b.shape, b.dtype) for b in gather],
        scratch_shapes=scratch_shapes + sems, compiler_params=_cp("arbitrary"))(*args, *scatter, *gather)


def _ffn_gu(x, g, wgu, name, gather=(), tm=512):
    T, D = x.shape
    fb = wgu.shape[2]
    nh = N_DEV // 2
    tm = _tile(T, tm, SUBLANES)

    def body(x_ref, g_ref, w_ref, ab_ref, act_ref, h_ref):
        xv = x_ref[...]
        h = (xv * _rstd(xv) * g_ref[...]).astype(BF16)
        h_ref[...] = h
        for d in range(nh):
            gt = _nn(h, w_ref[d])
            up = _nn(h, w_ref[d + nh])
            sg = jax.nn.sigmoid(gt)
            silu = gt * sg
            ab_ref[d] = (0.5 * up * (sg * (1.0 + gt * (1.0 - sg)))).astype(BF16)
            ab_ref[d + nh] = (0.5 * silu).astype(BF16)
            act_ref[d] = (silu * up).astype(BF16)

    row = pl.BlockSpec((tm, D), lambda i: (i, 0))
    blocks = lambda n: pl.BlockSpec((n, tm, fb), lambda i: (0, i, 0))
    return _hosted_call(
        body, name, T // tm, [row, _full(g), _full(wgu)], [blocks(N_DEV), blocks(nh), row],
        [SDS((N_DEV, T, fb), BF16), SDS((nh, T, fb), BF16), SDS((T, D), BF16)], (x, g, wgu), gather=gather)


def _ffn_down(act, wd, res, name, gather=(), tm=1024):
    nh, T, fb = act.shape
    D = wd.shape[2]
    tm = _tile(T, tm, SUBLANES)

    def body(a_ref, w_ref, r_ref, o_ref):
        wv = w_ref[...].reshape(nh, fb, D)
        acc = _nn(a_ref[0], wv[0])
        for d in range(1, nh):
            acc = acc + _nn(a_ref[d], wv[d])
        o_ref[...] = r_ref[...] + 0.5 * acc

    row = pl.BlockSpec((tm, D), lambda i: (i, 0))
    return _hosted_call(
        body, name, T // tm, [pl.BlockSpec((nh, tm, fb), lambda i: (0, i, 0)), _full(wd), row], [row],
        [SDS((T, D), F32)], (act, wd, res), gather=gather)


def _ffn_bwd_act(dyb, wd, ab, name, tm=512):
    T, D = dyb.shape
    _, _, fb = ab.shape
    nh = N_DEV // 2
    tm = _tile(T, tm, SUBLANES)

    def body(dy_ref, w_ref, ab_ref, o_ref):
        wv = w_ref[...].reshape(nh, fb, D)
        dy = dy_ref[...]
        for d in range(nh):
            da = _nt(dy, wv[d])
            o_ref[d] = (da * ab_ref[d].astype(F32)).astype(BF16)
            o_ref[d + nh] = (da * ab_ref[d + nh].astype(F32)).astype(BF16)

    blocks = pl.BlockSpec((N_DEV, tm, fb), lambda i: (0, i, 0))
    return pl.pallas_call(
        body, name=name, grid=(T // tm,),
        in_specs=[pl.BlockSpec((tm, D), lambda i: (i, 0)), _full(wd), blocks], out_specs=blocks,
        out_shape=SDS((N_DEV, T, fb), BF16), compiler_params=_cp("parallel"))(dyb, wd, ab)


def _ffn_dwd(act, dyb, name, tt=2048):
    nh, T, fb = act.shape
    D = dyb.shape[1]
    tt = _tile(T, tt, 2 * SUBLANES)
    nt = T // tt

    def body(a_ref, b_ref, o_ref, acc_ref):
        t = pl.program_id(1)

        @pl.when(t == 0)
        def _():
            acc_ref[...] = jnp.zeros_like(acc_ref)
        for s in range(2):
            acc_ref[s] += _tn(a_ref[s], b_ref[...])

        @pl.when(t == nt - 1)
        def _():
            o_ref[...] = (0.5 * acc_ref[...]).astype(BF16)

    return pl.pallas_call(
        body, name=name, grid=(nh // 2, nt),
        in_specs=[pl.BlockSpec((2, tt, fb), lambda j, t: (j, t, 0)), pl.BlockSpec((tt, D), lambda j, t: (t, 0))],
        out_specs=pl.BlockSpec((2, fb, D), lambda j, t: (j, 0, 0)), out_shape=SDS((nh, fb, D), BF16),
        scratch_shapes=[pltpu.VMEM((2, fb, D), F32)], compiler_params=_cp("parallel", "arbitrary"))(act, dyb)


def _ffn_dwgu(h, dgu, name, tt=2048):
    T, D = h.shape
    _, _, fb = dgu.shape
    tt = _tile(T, tt, 2 * SUBLANES)
    nt = T // tt

    def body(a_ref, b_ref, o_ref, acc_ref):
        t = pl.program_id(1)

        @pl.when(t == 0)
        def _():
            acc_ref[...] = jnp.zeros_like(acc_ref)
        for s in range(2):
            acc_ref[s] += _tn(a_ref[...], b_ref[s])

        @pl.when(t == nt - 1)
        def _():
            o_ref[...] = acc_ref[...].astype(BF16)

    return pl.pallas_call(
        body, name=name, grid=(N_DEV // 2, nt),
        in_specs=[pl.BlockSpec((tt, D), lambda j, t: (t, 0)), pl.BlockSpec((2, tt, fb), lambda j, t: (j, t, 0))],
        out_specs=pl.BlockSpec((2, D, fb), lambda j, t: (j, 0, 0)), out_shape=SDS((N_DEV, D, fb), BF16),
        scratch_shapes=[pltpu.VMEM((2, D, fb), F32)], compiler_params=_cp("parallel", "arbitrary"))(h, dgu)


def _ffn_bwd_in(dgu, wgu, x, g, dres, name, scatter, tm=512):
    T, D = x.shape
    _, _, fb = dgu.shape
    tm = _tile(T, tm, SUBLANES)

    def body(a_ref, w_ref, x_ref, g_ref, r_ref, dx_ref, dxb_ref, dg_ref):
        dh = _nt(a_ref[0], w_ref[0])
        for d in range(1, N_DEV):
            dh = dh + _nt(a_ref[d], w_ref[d])
        _norm_bwd_store(dh, x_ref, g_ref, r_ref, dx_ref, dxb_ref, dg_ref, pl.program_id(0) == 0)

    row = pl.BlockSpec((tm, D), lambda i: (i, 0))
    return _hosted_call(
        body, name, T // tm, [pl.BlockSpec((N_DEV, tm, fb), lambda i: (0, i, 0)), _full(wgu), row, _full(g), row],
        [row, row, pl.BlockSpec((1, D), lambda i: (0, 0))],
        [SDS((T, D), F32), SDS((T, D), BF16), SDS((1, D), F32)], (dgu, wgu, x, g, dres), scatter)


def _loss_head(x, g, tgt, name, tm=512):
    T, D = x.shape
    tm = _tile(T, tm, SUBLANES)

    def body(x_ref, g_ref, t_ref, loss_ref, dx_ref, dxb_ref, dg_ref):
        first = pl.program_id(0) == 0
        xv = x_ref[...]
        r = _rstd(xv)
        xh = xv * r
        e = xh * g_ref[...] - t_ref[...]
        part = 0.5 * jnp.sum(jnp.mean(e * e, axis=-1, keepdims=True), axis=0, keepdims=True)
        dy = e * (1.0 / D)
        dxh = dy * g_ref[...]
        dx = r * (dxh - xh * jnp.mean(dxh * xh, axis=-1, keepdims=True))
        dx_ref[...] = dx
        dxb_ref[...] = dx.astype(BF16)
        _accumulate(loss_ref, first, jnp.broadcast_to(part, loss_ref.shape))
        _accumulate(dg_ref, first, jnp.sum(dy * xh, axis=0, keepdims=True))

    row = pl.BlockSpec((tm, D), lambda i: (i, 0))
    vec = pl.BlockSpec((1, D), lambda i: (0, 0))
    return pl.pallas_call(
        body, name=name, grid=(T // tm,), in_specs=[row, vec, row],
        out_specs=[pl.BlockSpec((SUBLANES, LANES), lambda i: (0, 0)), row, row, vec],
        out_shape=[SDS((SUBLANES, LANES), F32), SDS((T, D), F32), SDS((T, D), BF16), SDS((1, D), F32)],
        compiler_params=_cp("arbitrary"))(x, g, tgt)


def _shift_down(u, k, prev_row):
    rows = lax.broadcasted_iota(I32, u.shape, 0)
    s = pltpu.roll(u, k, 0)
    for t in range(k):
        s = jnp.where(rows == t, prev_row(SUBLANES - k + t), s)
    return s


def _shift_up(u, k, next_row):
    n = u.shape[0]
    rows = lax.broadcasted_iota(I32, u.shape, 0)
    s = pltpu.roll(u, n - k, 0)
    for t in range(k):
        s = jnp.where(rows == n - k + t, next_row(t), s)
    return s


def _conv_taps(z_ref, zp_ref, w_ref, cw, first):
    u = z_ref[:, cw:2 * cw] * z_ref[:, 2 * cw:]

    def prev_row(r):
        return jnp.where(first, 0.0, zp_ref[r:r + 1, cw:2 * cw] * zp_ref[r:r + 1, 2 * cw:])

    u1 = _shift_down(u, 1, prev_row)
    u2 = _shift_down(u, 2, prev_row)
    cv = w_ref[0:1, :] * u2 + w_ref[1:2, :] * u1 + w_ref[2:3, :] * u
    return u, u1, u2, cv


def _mix_in_conv(x, g, w, wc, gc, name, tm=512):
    T, D = x.shape
    cw = w.shape[1] // 3
    tm = _tile(T, tm, SUBLANES)

    def body(x_ref, g_ref, w_ref, wc_ref, gc_ref, z_ref, h_ref, o_ref, prev_ref):
        xv = x_ref[...]
        h = (xv * _rstd(xv) * g_ref[...]).astype(BF16)
        h_ref[...] = h
        z_ref[...] = _nn(h, w_ref[...])
        _, _, _, cv = _conv_taps(z_ref, prev_ref, wc_ref, cw, pl.program_id(0) == 0)
        y = z_ref[:, :cw] * cv
        o_ref[...] = (y * _rstd(y) * gc_ref[...]).astype(BF16)
        prev_ref[...] = z_ref[tm - SUBLANES:, :]

    row = lambda n: pl.BlockSpec((tm, n), lambda i: (i, 0))
    return pl.pallas_call(
        body, name=name, grid=(T // tm,), in_specs=[row(D), _full(g), _full(w), _full(wc), _full(gc)],
        out_specs=[row(3 * cw), row(D), row(cw)],
        out_shape=[SDS((T, 3 * cw), F32), SDS((T, D), BF16), SDS((T, cw), BF16)],
        scratch_shapes=[pltpu.VMEM((SUBLANES, 3 * cw), F32)], compiler_params=_cp("arbitrary"))(x, g, w, wc, gc)


def _mix_bwd_head(dyb, w_out, zc, wc, gc, ot, g_col, name, scatter):
    nb, aw, tm = ot.shape
    T, D = dyb.shape
    cw = zc.shape[1] // 3
    H = aw // HEAD_DIM
    hb = tm // SUBLANES

    def body(dyb_ref, wo_ref, z_ref, zp_ref, w_ref, g_ref, o_ref, ga_ref,
             dz_ref, dw_ref, dg_ref, do_ref, dl_ref, dga_ref, carry_ref):
        s = pl.program_id(0)
        dy = _nt(dyb_ref[...], _rows2d(wo_ref))
        ov = o_ref[...]
        ra = lax.rsqrt(jnp.mean(ov * ov, axis=0, keepdims=True) + EPS)
        oh = ov * ra
        dya = dy[:, cw:].T
        dyah = dya * ga_ref[...]
        dov = ra * (dyah - oh * jnp.mean(dyah * oh, axis=0, keepdims=True))
        do_ref[...] = dov.astype(BF16)
        dl_ref[...] = jnp.sum((dov * ov).reshape(H, HEAD_DIM, tm), axis=1)
        _accumulate(dga_ref, s == 0, jnp.broadcast_to(jnp.sum(dya * oh, axis=1, keepdims=True), dga_ref.shape))
        u, u1, u2, cv = _conv_taps(z_ref, zp_ref, w_ref, cw, s == nb - 1)
        zb = z_ref[:, :cw]
        y = zb * cv
        r = _rstd(y)
        yh = y * r
        dyn = dy[:, :cw]
        dyh = dyn * g_ref[...]
        dyc = r * (dyh - yh * jnp.mean(dyh * yh, axis=-1, keepdims=True))
        dcv = dyc * zb

        def next_row(t):
            return jnp.where(s == 0, 0.0, carry_ref[t:t + 1, :])

        du = w_ref[2:3, :] * dcv + w_ref[1:2, :] * _shift_up(dcv, 1, next_row) + w_ref[0:1, :] * _shift_up(dcv, 2, next_row)
        carry_ref[...] = dcv[0:SUBLANES, :]
        dz_ref[:, :cw] = (dyc * cv).astype(BF16)
        dz_ref[:, cw:2 * cw] = (du * z_ref[:, 2 * cw:]).astype(BF16)
        dz_ref[:, 2 * cw:] = (du * z_ref[:, cw:2 * cw]).astype(BF16)
        tap = lax.broadcasted_iota(I32, (SUBLANES, cw), 0)
        dwp = jnp.where(tap == 0, jnp.sum(dcv * u2, axis=0, keepdims=True),
                        jnp.where(tap == 1, jnp.sum(dcv * u1, axis=0, keepdims=True),
                                  jnp.where(tap == 2, jnp.sum(dcv * u, axis=0, keepdims=True), 0.0)))
        _accumulate(dw_ref, s == 0, dwp)
        _accumulate(dg_ref, s == 0, jnp.sum(dyn * yh, axis=0, keepdims=True))

    rev = lambda s: nb - 1 - s
    rows = lambda n: pl.BlockSpec((tm, n), lambda s: (rev(s), 0))
    blk = pl.BlockSpec((None, aw, tm), lambda s: (rev(s), 0, 0))
    return _hosted_call(
        body, name, nb,
        [rows(D), _full(w_out), rows(3 * cw),
         pl.BlockSpec((SUBLANES, 3 * cw), lambda s: (jnp.maximum(rev(s) * hb - 1, 0), 0)),
         _full(wc), _full(gc), blk, _full(g_col)],
        [rows(3 * cw), pl.BlockSpec((SUBLANES, cw), lambda s: (0, 0)), pl.BlockSpec((1, cw), lambda s: (0, 0)),
         blk, pl.BlockSpec((None, H, tm), lambda s: (rev(s), 0, 0)), pl.BlockSpec((aw, LANES), lambda s: (0, 0))],
        [SDS((T, 3 * cw), BF16), SDS((SUBLANES, cw), F32), SDS((1, cw), F32),
         SDS((nb, aw, tm), BF16), SDS((nb, H, tm), F32), SDS((aw, LANES), F32)],
        (dyb, w_out, zc, zc, wc, gc, ot, g_col), scatter=scatter, scratch_shapes=[pltpu.VMEM((SUBLANES, cw), F32)])


def _proj_t(h, wat, wft, B, name):
    T, D = h.shape
    R = wat.shape[0]
    nb = T // B
    qk = 2 * R // 3
    g = qk // HEAD_DIM

    def body(h_ref, wa_ref, wf_ref, za_ref, zf_ref, n_ref):
        hv = h_ref[...]
        zab = _nt(wa_ref[...], hv).astype(BF16)
        za_ref[...] = zab
        zf_ref[...] = _nt(wf_ref[...], hv)
        zv = zab[:qk].astype(F32)
        ss = jnp.sum((zv * zv).reshape(g, HEAD_DIM, B), axis=1)
        mx = jnp.broadcast_to(jnp.max(ss, axis=1, keepdims=True), n_ref.shape)

        @pl.when(pl.program_id(0) == 0)
        def _():
            n_ref[...] = mx

        @pl.when(pl.program_id(0) > 0)
        def _():
            n_ref[...] = jnp.maximum(n_ref[...], mx)

    return pl.pallas_call(
        body, name=name, grid=(nb,), in_specs=[pl.BlockSpec((B, D), lambda i: (i, 0)), _full(wat), _full(wft)],
        out_specs=[pl.BlockSpec((None, R, B), lambda i: (i, 0, 0)), pl.BlockSpec((SUBLANES, B), lambda i: (0, i)),
                   pl.BlockSpec((g, LANES), lambda i: (0, 0))],
        out_shape=[SDS((nb, R, B), BF16), SDS((SUBLANES, T), F32), SDS((g, LANES), F32)],
        compiler_params=_cp("arbitrary"))(h, wat, wft)


def _split3(v):
    hi = v.astype(BF16).astype(F32)
    r1 = v - hi
    mid = r1.astype(BF16).astype(F32)
    lo = (r1 - mid).astype(BF16).astype(F32)
    return hi, mid, lo


def _tri_dot3(v, tri):
    hi, mid, lo = _split3(v)
    return _nn(hi.astype(BF16), tri) + _nn(mid.astype(BF16), tri) + _nn(lo.astype(BF16), tri)


def _logf_cumsum(zft, b, name, tb=512):
    H, T = zft.shape
    tb = _tile(T, tb, LANES)

    def body(z_ref, b_ref, c_ref, hi_ref, mid_ref, lo_ref, carry_ref):
        @pl.when(pl.program_id(0) == 0)
        def _():
            carry_ref[...] = jnp.zeros_like(carry_ref)
        xv = z_ref[...] + b_ref[...]
        lf = jnp.minimum(xv, 0.0) - jnp.log(1.0 + jnp.exp(-jnp.abs(xv)))
        src = lax.broadcasted_iota(I32, (tb, tb), 0)
        dst = lax.broadcasted_iota(I32, (tb, tb), 1)
        tri = jnp.where(src <= dst, 1.0, 0.0).astype(BF16)
        cs = _tri_dot3(lf, tri) + carry_ref[...]
        c_ref[...] = cs
        hi, mid, lo = _split3(cs)
        hi_ref[...] = hi
        mid_ref[...] = mid
        lo_ref[...] = lo
        col = lax.broadcasted_iota(I32, cs.shape, 1)
        carry_ref[...] = jnp.sum(jnp.where(col == tb - 1, cs, 0.0), axis=1, keepdims=True)

    blk = pl.BlockSpec((H, tb), lambda i: (0, i))
    return pl.pallas_call(
        body, name=name, grid=(T // tb,), in_specs=[blk, _full(b)],
        out_specs=[blk, blk, blk, blk], out_shape=[SDS((H, T), F32)] * 4,
        scratch_shapes=[pltpu.VMEM((H, 1), F32)], compiler_params=_cp("arbitrary"))(zft, b)


def _logf_cumsum_bwd(dcq, dck, zft, b, name, tb=512):
    H, T = zft.shape
    tb = _tile(T, tb, LANES)
    nb = T // tb

    def body(dq_ref, dk_ref, z_ref, b_ref, o_ref, db_ref, carry_ref):
        s = pl.program_id(0)

        @pl.when(s == 0)
        def _():
            carry_ref[...] = jnp.zeros_like(carry_ref)
        dc = dq_ref[...] - dk_ref[...]
        src = lax.broadcasted_iota(I32, (tb, tb), 0)
        dst = lax.broadcasted_iota(I32, (tb, tb), 1)
        tri = jnp.where(src >= dst, 1.0, 0.0).astype(BF16)
        dl = _tri_dot3(dc, tri) + carry_ref[...]
        col = lax.broadcasted_iota(I32, dl.shape, 1)
        carry_ref[...] = jnp.sum(jnp.where(col == 0, dl, 0.0), axis=1, keepdims=True)
        dz = dl * jax.nn.sigmoid(-(z_ref[...] + b_ref[...]))
        o_ref[...] = dz
        _accumulate(db_ref, s == 0, jnp.sum(dz, axis=1, keepdims=True))

    blk = pl.BlockSpec((H, tb), lambda s: (0, nb - 1 - s))
    vec = pl.BlockSpec((H, 1), lambda s: (0, 0))
    return pl.pallas_call(
        body, name=name, grid=(nb,), in_specs=[blk, blk, blk, vec], out_specs=[blk, vec],
        out_shape=[SDS((H, T), F32), SDS((H, 1), F32)],
        scratch_shapes=[pltpu.VMEM((H, 1), F32)], compiler_params=_cp("arbitrary"))(dcq, dck, zft, b)


def _skip_table(cs_col, ce_row, norms, name):
    H, nb, _ = cs_col.shape

    def body(cs_ref, ce_ref, n_ref, jm_ref, im_ref):
        h = pl.program_id(0)
        nq = n_ref[pl.ds(h, 1), 0:1]
        nk = n_ref[pl.ds(H + h, 1), 0:1]
        bound = 2.0 * jnp.sqrt(nq * nk) * 0.125
        skip = jnp.where(bound + cs_ref[...] - ce_ref[...] <= -SKIP_MARGIN, 1, 0).astype(I32)
        jm_ref[...] = jnp.sum(skip, axis=1, keepdims=True)
        im_ref[...] = nb - 1 - jnp.sum(skip, axis=0, keepdims=True)

    col = pl.BlockSpec((None, nb, 1), lambda h: (h, 0, 0))
    row = pl.BlockSpec((None, 1, nb), lambda h: (h, 0, 0))
    return pl.pallas_call(
        body, name=name, grid=(H,), in_specs=[col, row, _full(norms)], out_specs=[col, row],
        out_shape=[SDS((H, nb, 1), I32), SDS((H, 1, nb), I32)], compiler_params=_cp("parallel"))(cs_col, ce_row, norms)


def _causal_mask(B):
    krow = lax.broadcasted_iota(I32, (B, B), 0)
    qcol = lax.broadcasted_iota(I32, (B, B), 1)
    return krow <= qcol


def _attn_fwd(jmin, zat, qa, ka, name):
    nb, R, B = zat.shape
    aw = R // 3
    H = aw // HEAD_DIM
    hd = HEAD_DIM

    def body(jm_ref, q_ref, k_ref, v_ref, qa_ref, ka_ref, o_ref, lse_ref):
        h = pl.program_id(0)
        i = pl.program_id(1)
        qt = jnp.concatenate([q_ref[...] * 0.125, qa_ref[...]], axis=0)

        def block(j, carry, masked):
            m, l, acc = carry
            s = _tn(jnp.concatenate([k_ref[j], ka_ref[j]], axis=0), qt)
            if masked:
                s = jnp.where(_causal_mask(B), s, NEG)
            m_new = jnp.maximum(m, jnp.max(s, axis=0, keepdims=True))
            p = jnp.exp(s - m_new)
            a = jnp.exp(m - m_new)
            l = a * l + jnp.sum(p, axis=0, keepdims=True)
            acc = a * acc + _nn(v_ref[j], p.astype(BF16))
            return m_new, l, acc

        init = (jnp.full((1, B), NEG, F32), jnp.zeros((1, B), F32), jnp.zeros((hd, B), F32))
        carry = lax.fori_loop(jm_ref[h * nb + i], i, lambda j, c: block(j, c, False), init)
        m, l, acc = block(i, carry, True)
        o_ref[...] = acc / l
        lse_ref[...] = m + jnp.log(l)

    grid_spec = pltpu.PrefetchScalarGridSpec(
        num_scalar_prefetch=1, grid=(H, nb),
        in_specs=[pl.BlockSpec((None, hd, B), lambda h, i, jm: (i, h, 0)),
                  pl.BlockSpec((nb, hd, B), lambda h, i, jm: (0, H + h, 0)),
                  pl.BlockSpec((nb, hd, B), lambda h, i, jm: (0, 2 * H + h, 0)),
                  pl.BlockSpec((None, None, AUG, B), lambda h, i, jm: (h, i, 0, 0)),
                  pl.BlockSpec((None, nb, AUG, B), lambda h, i, jm: (h, 0, 0, 0))],
        out_specs=[pl.BlockSpec((None, hd, B), lambda h, i, jm: (i, h, 0)),
                   pl.BlockSpec((None, None, 1, B), lambda h, i, jm: (h, i, 0, 0))])
    return pl.pallas_call(
        body, name=name, grid_spec=grid_spec,
        out_shape=[SDS((nb, aw, B), F32), SDS((H, nb, 1, B), F32)],
        compiler_params=_cp("parallel", "parallel"))(jmin, zat, zat, zat, qa, ka)


def _gnorm_t_fwd(ot, g_col, name):
    nb, aw, B = ot.shape

    def body(o_ref, g_ref, y_ref):
        ov = o_ref[...]
        r = lax.rsqrt(jnp.mean(ov * ov, axis=0, keepdims=True) + EPS)
        y_ref[...] = (ov * r * g_ref[...]).T.astype(BF16)

    return pl.pallas_call(
        body, name=name, grid=(nb,), in_specs=[pl.BlockSpec((None, aw, B), lambda i: (i, 0, 0)), _full(g_col)],
        out_specs=pl.BlockSpec((B, aw), lambda i: (i, 0)), out_shape=SDS((nb * B, aw), BF16),
        compiler_params=_cp("parallel"))(ot, g_col)


def _attn_bwd(imax, zat, qa, ka, dot, lse, delta, name):
    nb, R, B = zat.shape
    aw = R // 3
    H = aw // HEAD_DIM
    hd = HEAD_DIM

    def body(im_ref, k_ref, v_ref, ka_ref, q_ref, qa_ref, do_ref, lse_ref, dl_ref,
             dq_ref, dqa_ref, dk_ref, dka_ref, dv_ref, dq_acc):
        h = pl.program_id(0)
        j = pl.program_id(1)

        @pl.when(j == 0)
        def _():
            dq_acc[...] = jnp.zeros_like(dq_acc)
            dqa_ref[...] = jnp.zeros_like(dqa_ref)
        kt = jnp.concatenate([k_ref[...], ka_ref[...]], axis=0)
        vt = v_ref[...]

        def block(i, carry, masked):
            dk, dv = carry
            qt = jnp.concatenate([q_ref[i] * 0.125, qa_ref[i]], axis=0)
            dov = do_ref[i]
            s = _tn(kt, qt)
            if masked:
                s = jnp.where(_causal_mask(B), s, NEG)
            p = jnp.exp(s - lse_ref[i])
            dp = _tn(vt, dov)
            ds = (p * (dp - dl_ref[i, pl.ds(h, 1), :])).astype(BF16)
            dv = dv + _nt(dov, p.astype(BF16))
            dk = dk + _nt(qt, ds)
            r = _nn(kt, ds)
            dq_acc[i] += 0.125 * r[:hd]
            dqa_ref[i] += r[hd:]
            return dk, dv

        carry = block(j, (jnp.zeros((hd + AUG, B), F32), jnp.zeros((hd, B), F32)), True)
        dk, dv = lax.fori_loop(j + 1, im_ref[h * nb + j] + 1, lambda i, c: block(i, c, False), carry)
        dk_ref[...] = dk[:hd].astype(BF16)
        dka_ref[...] = dk[hd:]
        dv_ref[...] = dv.astype(BF16)

        @pl.when(j == nb - 1)
        def _():
            dq_ref[...] = dq_acc[...].astype(BF16)

    head_rows = lambda off: pl.BlockSpec((nb, hd, B), lambda h, j, im: (0, off + h, 0))
    key_rows = lambda off: pl.BlockSpec((None, hd, B), lambda h, j, im: (j, off + h, 0))
    aug_all = pl.BlockSpec((None, nb, AUG, B), lambda h, j, im: (h, 0, 0, 0))
    aug_one = pl.BlockSpec((None, None, AUG, B), lambda h, j, im: (h, j, 0, 0))
    grid_spec = pltpu.PrefetchScalarGridSpec(
        num_scalar_prefetch=1, grid=(H, nb),
        in_specs=[key_rows(H), key_rows(2 * H), aug_one, head_rows(0), aug_all, head_rows(0),
                  pl.BlockSpec((None, nb, 1, B), lambda h, j, im: (h, 0, 0, 0)),
                  pl.BlockSpec((nb, H, B), lambda h, j, im: (0, 0, 0))],
        out_specs=[head_rows(0), aug_all, key_rows(0), aug_one, key_rows(0)],
        scratch_shapes=[pltpu.VMEM((nb, hd, B), F32)])
    return pl.pallas_call(
        body, name=name, grid_spec=grid_spec,
        out_shape=[SDS((nb, aw, B), BF16), SDS((H, nb, AUG, B), F32), SDS((nb, aw, B), BF16),
                   SDS((H, nb, AUG, B), F32), SDS((nb, aw, B), BF16)],
        compiler_params=_cp("arbitrary", "arbitrary"))(imax, zat, zat, ka, zat, qa, dot, lse, delta)


def _mix_dwt(dq, dk, dv, dzft, h, name):
    nb, aw, B = dq.shape
    D = h.shape[1]

    def body(dq_ref, dk_ref, dv_ref, dzf_ref, h_ref, o_ref, of_ref):
        first = pl.program_id(0) == 0
        hv = h_ref[...]
        part = jnp.concatenate([_nn(r[...], hv) for r in (dq_ref, dk_ref, dv_ref)], axis=0)
        _accumulate(o_ref, first, part)
        _accumulate(of_ref, first, _nn(dzf_ref[...].astype(BF16), hv))

    blk = pl.BlockSpec((None, aw, B), lambda i: (i, 0, 0))
    return pl.pallas_call(
        body, name=name, grid=(nb,),
        in_specs=[blk, blk, blk, pl.BlockSpec((SUBLANES, B), lambda i: (0, i)), pl.BlockSpec((B, D), lambda i: (i, 0))],
        out_specs=[pl.BlockSpec((3 * aw, D), lambda i: (0, 0)), pl.BlockSpec((SUBLANES, D), lambda i: (0, 0))],
        out_shape=[SDS((3 * aw, D), F32), SDS((SUBLANES, D), F32)],
        compiler_params=_cp("arbitrary"))(dq, dk, dv, dzft, h)


def _mix_bwd_in(dzc, dq, dk, dv, dzft, wc, wat, wft, x, g, dres, name, scatter):
    T, D = x.shape
    nb, aw, B = dq.shape

    def body(dzc_ref, dq_ref, dk_ref, dv_ref, dzf_ref, wc_ref, wa_ref, wf_ref, x_ref, g_ref, r_ref,
             dx_ref, dxb_ref, dg_ref):
        dh = _nt(dzc_ref[...], wc_ref[...])
        for s, r in enumerate((dq_ref, dk_ref, dv_ref)):
            dh = dh + _tn(r[...], wa_ref[s * aw:(s + 1) * aw, :])
        dh = dh + _tn(dzf_ref[...].astype(BF16), wf_ref[...])
        _norm_bwd_store(dh, x_ref, g_ref, r_ref, dx_ref, dxb_ref, dg_ref, pl.program_id(0) == 0)

    blk = pl.BlockSpec((None, aw, B), lambda i: (i, 0, 0))
    row = lambda n: pl.BlockSpec((B, n), lambda i: (i, 0))
    return _hosted_call(
        body, name, nb,
        [row(dzc.shape[1]), blk, blk, blk, pl.BlockSpec((SUBLANES, B), lambda i: (0, i)),
         _full(wc), _full(wat), _full(wft), row(D), _full(g), row(D)],
        [row(D), row(D), pl.BlockSpec((1, D), lambda i: (0, 0))],
        [SDS((T, D), F32), SDS((T, D), BF16), SDS((1, D), F32)],
        (dzc, dq, dk, dv, dzft, wc, wat, wft, x, g, dres), scatter)


def _mem_kv(mem, g, wkv, name):
    M, D = mem.shape
    dx = wkv.shape[2]

    def body(m_ref, g_ref, w_ref, kv_ref, h_ref):
        mv = m_ref[...]
        h = (mv * _rstd(mv) * g_ref[...]).astype(BF16)
        h_ref[...] = h
        for s in range(N_DEV):
            kv_ref[s] = _nn(h, w_ref[s]).astype(BF16)

    return pl.pallas_call(
        body, name=name, grid=(1,), in_specs=[_full(mem), _full(g), _full(wkv)],
        out_specs=[pl.BlockSpec((N_DEV, M, dx), lambda i: (0, 0, 0)), pl.BlockSpec((M, D), lambda i: (0, 0))],
        out_shape=[SDS((N_DEV, M, dx), BF16), SDS((M, D), BF16)], compiler_params=_cp("arbitrary"))(mem, g, wkv)


def _mem_kv_bwd(dkv, hm, wkv, mem, g, name):
    M, D = mem.shape
    dx = wkv.shape[2]

    def body(dkv_ref, h_ref, w_ref, m_ref, g_ref, dw_ref, dg_ref):
        hv = h_ref[...]
        dh = jnp.zeros((M, D), F32)
        for s in range(N_DEV):
            d = dkv_ref[s].astype(BF16)
            dw_ref[s] = _tn(hv, d).astype(BF16)
            dh = dh + _nt(d, w_ref[s])
        mv = m_ref[...]
        dg_ref[...] = jnp.sum(dh * (mv * _rstd(mv)), axis=0, keepdims=True)

    return pl.pallas_call(
        body, name=name, grid=(1,), in_specs=[_full(dkv), _full(hm), _full(wkv), _full(mem), _full(g)],
        out_specs=[pl.BlockSpec((N_DEV, D, dx), lambda i: (0, 0, 0)), pl.BlockSpec((1, D), lambda i: (0, 0))],
        out_shape=[SDS((N_DEV, D, dx), BF16), SDS((1, D), F32)], compiler_params=_cp("arbitrary"))(dkv, hm, wkv, mem, g)


def _xattn_probs(q_ref, kv_ref, h, dx, scale):
    qh = q_ref[:, h * dx:(h + 1) * dx]
    kh = kv_ref[h]
    s = _nt(qh, kh) * scale
    p = jnp.exp(s - jnp.max(s, axis=-1, keepdims=True))
    return qh, kh, p / jnp.sum(p, axis=-1, keepdims=True)


def _xattn(x, g, wq, kv, wo, name, tm=512):
    T, D = x.shape
    dx = D // N_XHEADS
    scale = dx ** -0.5
    tm = _tile(T, tm, SUBLANES)

    def body(x_ref, g_ref, wq_ref, kv_ref, wo_ref, xo_ref, h_ref, q_ref, o_ref):
        xv = x_ref[...]
        h = (xv * _rstd(xv) * g_ref[...]).astype(BF16)
        h_ref[...] = h
        q_ref[...] = _nn(h, _rows2d(wq_ref)).astype(BF16)
        for hd in range(N_XHEADS):
            _, _, p = _xattn_probs(q_ref, kv_ref, hd, dx, scale)
            o_ref[:, hd * dx:(hd + 1) * dx] = _nn(p.astype(BF16), kv_ref[N_XHEADS + hd]).astype(BF16)
        xo_ref[...] = xv + _nn(o_ref[...], _rows2d(wo_ref))

    row = pl.BlockSpec((tm, D), lambda i: (i, 0))
    return pl.pallas_call(
        body, name=name, grid=(T // tm,), in_specs=[row, _full(g), _full(wq), _full(kv), _full(wo)],
        out_specs=[row] * 4, out_shape=[SDS((T, D), F32)] + [SDS((T, D), BF16)] * 3,
        compiler_params=_cp("parallel"))(x, g, wq, kv, wo)


def _xattn_bwd(dyb, dres, x, g, q, kv, wq, wo, name, tm=512):
    T, D = x.shape
    dx = D // N_XHEADS
    scale = dx ** -0.5
    tm = _tile(T, tm, SUBLANES)

    def body(dy_ref, r_ref, x_ref, g_ref, q_ref, kv_ref, wq_ref, wo_ref, dx_ref, dxb_ref, dg_ref, dq_ref, dkv_ref):
        first = pl.program_id(0) == 0

        @pl.when(first)
        def _():
            dkv_ref[...] = jnp.zeros_like(dkv_ref)
        do = _nt(dy_ref[...], _rows2d(wo_ref)).astype(BF16)
        for hd in range(N_XHEADS):
            qh, kh, p = _xattn_probs(q_ref, kv_ref, hd, dx, scale)
            doh = do[:, hd * dx:(hd + 1) * dx]
            dp = _nt(doh, kv_ref[N_XHEADS + hd])
            ds = (p * (dp - jnp.sum(p * dp, axis=-1, keepdims=True)) * scale).astype(BF16)
            dq_ref[:, hd * dx:(hd + 1) * dx] = _nn(ds, kh).astype(BF16)
            dkv_ref[hd] += _tn(ds, qh)
            dkv_ref[N_XHEADS + hd] += _tn(p.astype(BF16), doh)
        dh = _nt(dq_ref[...], _rows2d(wq_ref))
        _norm_bwd_store(dh, x_ref, g_ref, r_ref, dx_ref, dxb_ref, dg_ref, first)

    row = pl.BlockSpec((tm, D), lambda i: (i, 0))
    return pl.pallas_call(
        body, name=name, grid=(T // tm,),
        in_specs=[row, row, row, _full(g), row, _full(kv), _full(wq), _full(wo)],
        out_specs=[row, row, pl.BlockSpec((1, D), lambda i: (0, 0)), row, pl.BlockSpec(kv.shape, lambda i: (0, 0, 0))],
        out_shape=[SDS((T, D), F32), SDS((T, D), BF16), SDS((1, D), F32), SDS((T, D), BF16), SDS(kv.shape, F32)],
        compiler_params=_cp("arbitrary"))(dyb, dres, x, g, q, kv, wq, wo)


def _mesh_position():
    return lax.axis_index("x"), lax.axis_index("y"), lax.axis_index("c")


class _GatherCopies:
    def __init__(self, x_refs, out_refs, send_sems, recv_sems, local_sems):
        self.x_refs, self.out_refs, self.n = x_refs, out_refs, len(x_refs)
        self.send_sems, self.recv_sems, self.local_sems = send_sems, recv_sems, local_sems
        x, y, c = _mesh_position()
        self.c = c
        self.me, self.sibling = (x, y, c), (x, y, 1 - c)
        self.chips = [(1 - x, y), (x, 1 - y), (1 - x, 1 - y)]

    def slot(self, w, px, py, pc):
        return self.out_refs[w].at[4 * px + 2 * py + pc]

    def copy(self, w, k, block, to, src=None):
        return pltpu.make_async_remote_copy(
            src_ref=self.slot(w, *block) if src is None else src, dst_ref=self.slot(w, *block),
            send_sem=self.send_sems.at[w, k], recv_sem=self.recv_sems.at[w, k], device_id=to,
            device_id_type=pl.DeviceIdType.MESH)

    def mine(self, w):
        return pltpu.make_async_copy(self.x_refs[w], self.slot(w, *self.me), self.local_sems.at[w])

    def first(self, w):
        src = self.x_refs[w]
        return [self.copy(w, 0, self.me, self.sibling, src=src)] + [
            self.copy(w, 1 + j, self.me, (*chip, self.c), src=src) for j, chip in enumerate(self.chips)]

    def passed(self, w):
        return [self.copy(w, 4 + j, (*chip, self.c), self.sibling) for j, chip in enumerate(self.chips)]


def _gather_start(*refs):
    cp = _GatherCopies(*refs)
    for w in range(cp.n):
        cp.mine(w).start()
        for f in cp.first(w):
            f.start()


def _gather_relay(*refs):
    cp = _GatherCopies(*refs)
    for w in range(cp.n):
        for j, (chip, fwd) in enumerate(zip(cp.chips, cp.passed(w))):
            cp.copy(w, 1 + j, (*chip, cp.c), cp.me).wait_recv()
            fwd.start()


def _gather_finish(*refs):
    cp = _GatherCopies(*refs)
    for w in range(cp.n):
        cp.copy(w, 0, cp.sibling, cp.me).wait_recv()
        for j, chip in enumerate(cp.chips):
            cp.copy(w, 4 + j, (*chip, 1 - cp.c), cp.me).wait_recv()
    for w in range(cp.n):
        for f in cp.first(w) + cp.passed(w):
            f.wait_send()
        cp.mine(w).wait()


def _all_gather(shards, name):
    n = len(shards)

    def body(*refs):
        parts = (refs[:n], refs[n:2 * n]) + refs[2 * n:]
        _gather_start(*parts)
        _gather_relay(*parts)
        _gather_finish(*parts)

    any_spec = pl.BlockSpec(memory_space=pl.ANY)
    return pl.pallas_call(
        body, name=name, out_shape=[SDS((N_DEV,) + s.shape, s.dtype) for s in shards],
        in_specs=[any_spec] * n, out_specs=[any_spec] * n,
        scratch_shapes=[pltpu.SemaphoreType.DMA((n, 7)), pltpu.SemaphoreType.DMA((n, 7)), pltpu.SemaphoreType.DMA((n,))],
    )(*shards)


def _scatter_copies(x_refs, out_refs, send_sems, recv_sems, local_sems):
    x, y, c = _mesh_position()
    me = 4 * x + 2 * y + c
    own, remote = [], []
    for w, (src, dst) in enumerate(zip(x_refs, out_refs)):
        own.append(pltpu.make_async_copy(src.at[me], dst.at[me], local_sems.at[w]))
        for k in range(1, N_DEV):
            px = 1 - x if k & 4 else x
            py = 1 - y if k & 2 else y
            pc = 1 - c if k & 1 else c
            remote.append(pltpu.make_async_remote_copy(
                src_ref=src.at[4 * px + 2 * py + pc], dst_ref=dst.at[me], send_sem=send_sems.at[w, k - 1],
                recv_sem=recv_sems.at[w, k - 1], device_id=(px, py, pc), device_id_type=pl.DeviceIdType.MESH))
    return own, remote


def _scatter_start(x_refs, out_refs, send_sems, recv_sems, local_sems):
    own, remote = _scatter_copies(x_refs, out_refs, send_sems, recv_sems, local_sems)
    for cp in own + remote:
        cp.start()


def _scatter_wait(x_refs, out_refs, send_sems, recv_sems, local_sems):
    own, remote = _scatter_copies(x_refs, out_refs, send_sems, recv_sems, local_sems)
    for cp in remote:
        cp.wait_recv()
    for cp in remote:
        cp.wait_send()
    for cp in own:
        cp.wait()


def _sum_adamw(parts, w, m, v, name, tr=256):
    L, R, C = w.shape
    tr = _tile(R, tr, SUBLANES)
    nblk = R // tr

    def body(*refs):
        p_refs = refs[:L]
        w_ref, m_ref, v_ref, g_ref, d_ref, mo_ref, vo_ref = refs[L:]
        layer = pl.program_id(0)
        g = None
        for k, p_ref in enumerate(p_refs):
            gk = p_ref[0].astype(F32)
            for s in range(1, N_DEV):
                gk = gk + p_ref[s].astype(F32)
            g = gk if g is None else jnp.where(layer == k, gk, g)
        mn = ADAM_B1 * m_ref[...] + (1.0 - ADAM_B1) * g
        vn = ADAM_B2 * v_ref[...] + (1.0 - ADAM_B2) * jnp.square(g)
        m_hat = mn / (1.0 - ADAM_B1 ** ADAM_STEP)
        v_hat = vn / (1.0 - ADAM_B2 ** ADAM_STEP)
        g_ref[...] = g
        d_ref[...] = -ADAM_LR * (m_hat / (jnp.sqrt(v_hat) + ADAM_EPS) + ADAM_WD * w_ref[...])
        mo_ref[...] = mn
        vo_ref[...] = vn

    def part_spec(k):
        return pl.BlockSpec((N_DEV, tr, C),
                            lambda l, i: (0, jnp.where(l == k, i, jnp.where(l < k, 0, nblk - 1)), 0))

    row = pl.BlockSpec((None, tr, C), lambda l, i: (l, i, 0))
    return pl.pallas_call(
        body, name=name, grid=(L, nblk), in_specs=[part_spec(k) for k in range(L)] + [row, row, row],
        out_specs=[row] * 4, out_shape=[SDS((L, R, C), F32)] * 4,
        compiler_params=_cp("arbitrary", "arbitrary"))(*parts, w, m, v)


def _pack(flat_list, row_mult):
    flat = jnp.concatenate(flat_list, axis=-1)
    n = flat.shape[-1]
    chunk = row_mult * PACK_COLS
    pad = -n % chunk
    flat = jnp.pad(flat, [(0, 0)] * (flat.ndim - 1) + [(0, pad)])
    return flat.reshape(flat.shape[:-1] + ((n + pad) // PACK_COLS, PACK_COLS))


def _unpack(packed, shapes):
    lead = packed.shape[:-2]
    flat = packed.reshape(lead + (-1,))
    out = []
    off = 0
    for shp in shapes:
        n = 1
        for d in shp:
            n *= d
        out.append(flat[..., off:off + n].reshape(lead + tuple(shp)))
        off += n
    return out


def _ffn_bwd(dx, dxb, saved, wgu, wd, tag):
    x, g, h, ab, act = saved
    dgu = _ffn_bwd_act(dxb, wd, ab, f"{tag}_bwd_act")
    dwd = _ffn_dwd(act, dxb, f"{tag}_dwd")
    dwgu = _ffn_dwgu(h, dgu, f"{tag}_dwgu")
    return _ffn_bwd_in(dgu, wgu, x, g, dx, f"{tag}_bwd_in", [dwgu, dwd.reshape(N_DEV, -1, dwd.shape[-1])])


def _aug_rows(rows, H, nb, B):
    a = jnp.stack(rows + [jnp.zeros_like(rows[0])] * (AUG - len(rows)), axis=1)
    return a.reshape(H, AUG, nb, B).transpose(0, 2, 1, 3).astype(BF16)


def _mix_fwd(x, g, w_in8, w_conv8, b_f, g_conv, g_att, w_out, tag):
    T, D = x.shape
    w_in = w_in8.transpose(1, 0, 2).reshape(D, -1)
    cw = D // 2
    aw = D - cw
    H = aw // HEAD_DIM
    B = _tile(T, ATT_BLOCK, LANES)
    nb = T // B
    w_c = w_in[:, :3 * cw]
    w_at = w_in[:, 3 * cw:3 * cw + 3 * aw].T
    w_ft = jnp.pad(w_in[:, 3 * cw + 3 * aw:].T, ((0, SUBLANES - H), (0, 0)))
    zc, h, ync = _mix_in_conv(x, g, w_c, w_conv8, g_conv, f"{tag}_in_conv")
    zat, zft8, norms = _proj_t(h, w_at, w_ft, B, f"{tag}_in_att")
    zft = zft8[:H]
    bcol = b_f.reshape(H, 1)
    c, chi, cmid, clo = _logf_cumsum(zft, bcol, f"{tag}_cumsum")
    one = jnp.ones_like(c)
    qa = _aug_rows([one, one, one, chi, cmid, clo], H, nb, B)
    ka = _aug_rows([-chi, -cmid, -clo, one, one, one], H, nb, B)
    cb = c.reshape(H, nb, B)
    jmin, imax = _skip_table(cb[:, :, 0:1], cb[:, :, B - 1].reshape(H, 1, nb), norms, f"{tag}_skip")
    ot, lse = _attn_fwd(jmin.reshape(H * nb), zat, qa, ka, f"{tag}_attn")
    g_col = g_att.reshape(aw, 1)
    yna = _gnorm_t_fwd(ot, g_col, f"{tag}_gnorm")
    xo = _mm_res([ync, yna], w_out, x, f"{tag}_out")
    saved = (x, g, h, zc, zft, bcol, zat, qa, ka, imax.reshape(H * nb), ot, lse, ync, yna, w_c, w_at, w_ft, g_col)
    return xo, saved


def _mix_bwd(dx, dxb, saved, w_conv8, g_conv, w_out, tag, also_scatter):
    x, g, h, zc, zft, bcol, zat, qa, ka, imax, ot, lse, ync, yna, w_c, w_at, w_ft, g_col = saved
    T, D = x.shape
    H = zft.shape[0]
    dw_out = jnp.concatenate([_mm_tn(ync, dxb, f"{tag}_dwout_c"), _mm_tn(yna, dxb, f"{tag}_dwout_a")], axis=0)
    dzc, dwc8, dg_conv, dot, delta, dg_att, *p_also = _mix_bwd_head(dxb, w_out, zc, w_conv8, g_conv, ot, g_col,
                                                                   f"{tag}_bwd_out", also_scatter)
    dq, dqa, dk, dka, dv = _attn_bwd(imax, zat, qa, ka, dot, lse, delta, f"{tag}_attn_bwd")
    dcq = dqa[:, :, 3, :].reshape(H, T)
    dck = dka[:, :, 0, :].reshape(H, T)
    dzft, db = _logf_cumsum_bwd(dcq, dck, zft, bcol, f"{tag}_cumsum_bwd")
    dzft8 = jnp.pad(dzft, ((0, SUBLANES - H), (0, 0)))
    dw_c = _mm_tn(h, dzc, f"{tag}_dwin_c")
    dw_at, dw_ft = _mix_dwt(dq, dk, dv, dzft8, h, f"{tag}_dwin_a")
    dw_in = jnp.concatenate([dw_c, dw_at.T.astype(BF16), dw_ft[:H].T.astype(BF16)], axis=1)
    scatter = [dw_in.reshape(D, N_DEV, -1).transpose(1, 0, 2), dwc8[:3].reshape(3, N_DEV, -1).transpose(1, 0, 2),
               dw_out.reshape(N_DEV, D // N_DEV, D)]
    dx, dxb, dg, p_in, p_conv, p_out = _mix_bwd_in(dzc, dq, dk, dv, dzft8, w_c, w_at, w_ft, x, g, dx, f"{tag}_bwd_in",
                                                   scatter)
    return dx, dxb, dg, p_in, p_conv, db.reshape(H), dg_conv, dg_att[:, 0], p_out, p_also


def _xattn_block_fwd(x, g, mem, g_mem, w_q, w_kv, w_o, tag):
    kv, hm = _mem_kv(mem, g_mem, w_kv, f"{tag}_kv")
    xo, h, q, o = _xattn(x, g, w_q, kv, w_o, f"{tag}_block")
    return xo, (x, g, h, mem, g_mem, hm, q, kv, o)


def _xattn_block_bwd(dx, dxb, saved, w_q, w_kv, w_o, tag):
    x, g, h, mem, g_mem, hm, q, kv, o = saved
    dw_o = _mm_tn(o, dxb, f"{tag}_dwo")
    dx, dxb, dg, dq, dkv = _xattn_bwd(dxb, dx, x, g, q, kv, w_q, w_o, f"{tag}_block_bwd")
    dw_q = _mm_tn(h, dq, f"{tag}_dwq")
    dw_kv, dg_mem = _mem_kv_bwd(dkv, hm, w_kv, mem, g_mem, f"{tag}_kv_bwd")
    D = dw_q.shape[0]
    return dx, dxb, dg, dg_mem, [dw_q.reshape(N_DEV, D // N_DEV, D), dw_kv, dw_o.reshape(N_DEV, D // N_DEV, D)]


GATHER_FIRST = ['w_ffn1_gu']
GATHER_REST = ['w_ffn1_down', 'w_mix_in', 'w_mix_out', 'w_xq', 'w_xkv', 'w_xo', 'w_ffn2_gu', 'w_ffn2_down']
GATHER_NEXT_A = ['w_ffn1_gu', 'w_ffn1_down', 'w_mix_in', 'w_mix_out', 'w_xq', 'w_xkv', 'w_xo']
GATHER_NEXT_B = ['w_ffn2_gu', 'w_ffn2_down']


def _local_step(x, mem, tgt, P):
    L = P['g_ffn1'].shape[0]
    shards = lambda names, l: [P[n][l].astype(BF16) for n in names]
    W = {n: [None] * L for n in SHARDED}

    def keep(names, l, arrays):
        for n, a in zip(names, arrays):
            W[n][l] = a

    *first, conv = _all_gather(shards(GATHER_FIRST, 0) + [P['w_conv']], "gather_first")
    keep(GATHER_FIRST, 0, first)
    w_conv = conv.transpose(1, 2, 0, 3).reshape(L, 3, -1)
    saved = []
    for l in range(L):
        row = lambda n: P[n][l][None, :]
        wc8 = jnp.pad(w_conv[l], ((0, SUBLANES - 3), (0, 0)))
        x1 = x
        bring = GATHER_REST if l == 0 else []
        ab, act, h, *got = _ffn_gu(x, row('g_ffn1'), W['w_ffn1_gu'][l], f"l{l}_ffn1_gu", gather=shards(bring, l))
        keep(bring, l, got)
        x, = _ffn_down(act, W['w_ffn1_down'][l], x, f"l{l}_ffn1_down")
        s1 = (x1, row('g_ffn1'), h, ab, act)
        x, s2 = _mix_fwd(x, row('g_mix'), W['w_mix_in'][l], wc8, P['b_f'][l], row('g_conv_out'), P['g_att_out'][l],
                         W['w_mix_out'][l], f"l{l}_mix")
        x, s3 = _xattn_block_fwd(x, row('g_xattn'), mem, row('g_mem'), W['w_xq'][l], W['w_xkv'][l], W['w_xo'][l],
                                 f"l{l}_xattn")
        x4 = x
        bring_a, bring_b = (GATHER_NEXT_A, GATHER_NEXT_B) if l + 1 < L else ([], [])
        ab, act, h, *got = _ffn_gu(x, row('g_ffn2'), W['w_ffn2_gu'][l], f"l{l}_ffn2_gu", gather=shards(bring_a, l + 1))
        keep(bring_a, l + 1, got)
        x, *got = _ffn_down(act, W['w_ffn2_down'][l], x, f"l{l}_ffn2_down", gather=shards(bring_b, l + 1))
        keep(bring_b, l + 1, got)
        saved.append((s1, s2, s3, (x4, row('g_ffn2'), h, ab, act), wc8))
    loss, dx, dxb, dg_final = _loss_head(x, P['g_final'][None, :], tgt, "loss_head")
    G = {n: [None] * L for n in WEIGHT_NAMES if n != 'g_final'}
    for l in reversed(range(L)):
        row = lambda n: P[n][l][None, :]
        s1, s2, s3, s4, wc8 = saved[l]
        dx, dxb, G['g_ffn2'][l], G['w_ffn2_gu'][l], G['w_ffn2_down'][l] = _ffn_bwd(
            dx, dxb, s4, W['w_ffn2_gu'][l], W['w_ffn2_down'][l], f"l{l}_ffn2")
        dx, dxb, G['g_xattn'][l], G['g_mem'][l], xattn_grads = _xattn_block_bwd(
            dx, dxb, s3, W['w_xq'][l], W['w_xkv'][l], W['w_xo'][l], f"l{l}_xattn")
        (dx, dxb, G['g_mix'][l], G['w_mix_in'][l], G['w_conv'][l], G['b_f'][l], G['g_conv_out'][l], G['g_att_out'][l],
         G['w_mix_out'][l], (G['w_xq'][l], G['w_xkv'][l], G['w_xo'][l])) = _mix_bwd(
            dx, dxb, s2, wc8, row('g_conv_out'), W['w_mix_out'][l], f"l{l}_mix", xattn_grads)
        dx, dxb, G['g_ffn1'][l], G['w_ffn1_gu'][l], G['w_ffn1_down'][l] = _ffn_bwd(
            dx, dxb, s1, W['w_ffn1_gu'][l], W['w_ffn1_down'][l], f"l{l}_ffn1")
    G['g_final'] = [dg_final]
    return loss, dx, G


def kernel(x, mem, g_ffn1, w_ffn1_gu, w_ffn1_down, g_mix, w_mix_in, w_conv, b_f, g_conv_out, g_att_out, w_mix_out, g_xattn, g_mem, w_xq, w_xkv, w_xo, g_ffn2, w_ffn2_gu, w_ffn2_down, g_final, loss_target, m_g_ffn1, m_w_ffn1_gu, m_w_ffn1_down, m_g_mix, m_w_mix_in, m_w_conv, m_b_f, m_g_conv_out, m_g_att_out, m_w_mix_out, m_g_xattn, m_g_mem, m_w_xq, m_w_xkv, m_w_xo, m_g_ffn2, m_w_ffn2_gu, m_w_ffn2_down, m_g_final, v_g_ffn1, v_w_ffn1_gu, v_w_ffn1_down, v_g_mix, v_w_mix_in, v_w_conv, v_b_f, v_g_conv_out, v_g_att_out, v_w_mix_out, v_g_xattn, v_g_mem, v_w_xq, v_w_xkv, v_w_xo, v_g_ffn2, v_w_ffn2_gu, v_w_ffn2_down, v_g_final):
    args = (x, mem, g_ffn1, w_ffn1_gu, w_ffn1_down, g_mix, w_mix_in, w_conv, b_f, g_conv_out, g_att_out, w_mix_out,
            g_xattn, g_mem, w_xq, w_xkv, w_xo, g_ffn2, w_ffn2_gu, w_ffn2_down, g_final)
    P = dict(zip(IN_NAMES, args))
    moms = (m_g_ffn1, m_w_ffn1_gu, m_w_ffn1_down, m_g_mix, m_w_mix_in, m_w_conv, m_b_f, m_g_conv_out, m_g_att_out,
            m_w_mix_out, m_g_xattn, m_g_mem, m_w_xq, m_w_xkv, m_w_xo, m_g_ffn2, m_w_ffn2_gu, m_w_ffn2_down, m_g_final)
    vars_ = (v_g_ffn1, v_w_ffn1_gu, v_w_ffn1_down, v_g_mix, v_w_mix_in, v_w_conv, v_b_f, v_g_conv_out, v_g_att_out,
             v_w_mix_out, v_g_xattn, v_g_mem, v_w_xq, v_w_xkv, v_w_xo, v_g_ffn2, v_w_ffn2_gu, v_w_ffn2_down, v_g_final)
    MOM = dict(zip(WEIGHT_NAMES, moms))
    VAR = dict(zip(WEIGHT_NAMES, vars_))

    loss_part, dx, grads = _local_step(x[0], mem[0], loss_target[0], {n: P[n] for n in WEIGHT_NAMES})

    small_shapes = [P[n].shape for n in REPLICATED] + [(1,)]
    small = _pack([a.reshape(-1) for n in REPLICATED for a in grads[n]] + [loss_part[0, :1]], SUBLANES)
    small_parts = _all_gather([small], "gather_small_grads")[0]

    res = {n: _sum_adamw(grads[n], P[n], MOM[n], VAR[n], "adamw_" + n) for n in SHARDED}
    zero = [jnp.zeros((1,), F32)]
    w_, m_, v_ = (_pack([d[n].reshape(-1) for n in REPLICATED] + zero, SUBLANES)[None] for d in (P, MOM, VAR))
    small_out = [_unpack(o[0], small_shapes) for o in _sum_adamw([small_parts], w_, m_, v_, "adamw_vectors")]
    for k, n in enumerate(REPLICATED):
        res[n] = [o[k] for o in small_out]
    out = []
    for k in range(4):
        out += [res[n][k] for n in WEIGHT_NAMES]
    return (small_out[0][-1].reshape(()), dx[None], *out)
```

```python
import jax
import jax.numpy as jnp
from jax import lax
from jax.experimental import pallas as pl
from jax.experimental.pallas import tpu as pltpu

F32 = jnp.float32
BF16 = jnp.bfloat16
I32 = jnp.int32
SDS = jax.ShapeDtypeStruct

EPS = 1e-6
HEAD_DIM = 64
N_XHEADS = 4
N_DEV = 8
LANES = 128
SUBLANES = 8
AUG = 16
ATT_BLOCK = 512
NEG = -1e30
SKIP_MARGIN = 115.0
PACK_COLS = 1024
VMEM_LIMIT = 58 * 1024 * 1024

ADAM_LR = 0.001
ADAM_B1 = 0.9
ADAM_B2 = 0.999
ADAM_EPS = 1e-08
ADAM_WD = 0.01
ADAM_STEP = 10

IN_NAMES = ['x', 'mem', 'g_ffn1', 'w_ffn1_gu', 'w_ffn1_down', 'g_mix', 'w_mix_in', 'w_conv', 'b_f', 'g_conv_out',
            'g_att_out', 'w_mix_out', 'g_xattn', 'g_mem', 'w_xq', 'w_xkv', 'w_xo', 'g_ffn2', 'w_ffn2_gu',
            'w_ffn2_down', 'g_final']
WEIGHT_NAMES = IN_NAMES[2:]
SHARDED = ['w_ffn1_gu', 'w_ffn1_down', 'w_mix_in', 'w_conv', 'w_mix_out', 'w_xq', 'w_xkv', 'w_xo', 'w_ffn2_gu',
           'w_ffn2_down']
REPLICATED = [n for n in WEIGHT_NAMES if n not in SHARDED]
LONG_AXIS_LAST = ['w_ffn1_gu', 'w_ffn2_gu', 'w_mix_in']


def _tile(n, pref, mult):
    t = min(pref, n) // mult * mult
    while t >= mult:
        if n % t == 0:
            return t
        t -= mult
    return n


def _cp(*sem):
    return pltpu.CompilerParams(dimension_semantics=sem, vmem_limit_bytes=VMEM_LIMIT)


def _nt(a, b):
    return lax.dot_general(a, b, (((1,), (1,)), ((), ())), preferred_element_type=F32)


def _tn(a, b):
    return lax.dot_general(a, b, (((0,), (0,)), ((), ())), preferred_element_type=F32)


def _nn(a, b):
    return jnp.dot(a, b, preferred_element_type=F32)


def _rstd(xv):
    return lax.rsqrt(jnp.mean(xv * xv, axis=-1, keepdims=True) + EPS)


def _full(a):
    nd = a.ndim
    return pl.BlockSpec(a.shape, lambda *_: (0,) * nd, pipeline_mode=pl.Buffered(1))


def _rows2d(w_ref):
    s, a, b = w_ref.shape
    return w_ref[...].reshape(s * a, b)


def _accumulate(ref, first, part):
    @pl.when(first)
    def _():
        ref[...] = part

    @pl.when(jnp.logical_not(first))
    def _():
        ref[...] += part


def _norm_bwd_store(dh, x_ref, g_ref, r_ref, dx_ref, dxb_ref, dg_ref, first):
    xv = x_ref[...]
    r = _rstd(xv)
    xh = xv * r
    dxh = dh * g_ref[...]
    dx = r * (dxh - xh * jnp.mean(dxh * xh, axis=-1, keepdims=True)) + r_ref[...]
    dx_ref[...] = dx
    dxb_ref[...] = dx.astype(BF16)
    _accumulate(dg_ref, first, jnp.sum(dh * xh, axis=0, keepdims=True))


def _mm_res(a_list, w, res, name, tm=512):
    T = res.shape[0]
    N = w.shape[-1]
    tm = _tile(T, tm, SUBLANES)
    n_a = len(a_list)

    def body(*refs):
        a_refs = refs[:n_a]
        w_ref, r_ref, o_ref = refs[n_a:]
        wv = _rows2d(w_ref)
        acc = r_ref[...]
        off = 0
        for a_ref in a_refs:
            k = a_ref.shape[1]
            acc = acc + _nn(a_ref[...], wv[off:off + k, :])
            off += k
        o_ref[...] = acc

    row = lambda n: pl.BlockSpec((tm, n), lambda i: (i, 0))
    return pl.pallas_call(
        body, name=name, grid=(T // tm,),
        in_specs=[row(a.shape[1]) for a in a_list] + [_full(w), row(N)], out_specs=row(N),
        out_shape=SDS((T, N), F32), compiler_params=_cp("parallel"))(*a_list, w, res)


def _mm_tn(a, b, name, out_dtype=BF16, tt=2048):
    T, K = a.shape
    N = b.shape[1]
    tt = _tile(T, tt, 2 * SUBLANES)
    nt = T // tt

    def body(a_ref, b_ref, o_ref, acc_ref):
        t = pl.program_id(0)
        _accumulate(acc_ref, t == 0, _tn(a_ref[...], b_ref[...]))

        @pl.when(t == nt - 1)
        def _():
            o_ref[...] = acc_ref[...].astype(o_ref.dtype)

    return pl.pallas_call(
        body, name=name, grid=(nt,),
        in_specs=[pl.BlockSpec((tt, K), lambda t: (t, 0)), pl.BlockSpec((tt, N), lambda t: (t, 0))],
        out_specs=pl.BlockSpec((K, N), lambda t: (0, 0)), out_shape=SDS((K, N), out_dtype),
        scratch_shapes=[pltpu.VMEM((K, N), F32)], compiler_params=_cp("arbitrary"))(a, b)


def _hosted_call(body, name, n_steps, in_specs, out_specs, out_shape, args, scatter=(), gather=(), scratch_shapes=()):
    scatter, gather, scratch_shapes = list(scatter), list(gather), list(scratch_shapes)
    n_s, n_g = len(scatter), len(gather)
    if n_s + n_g == 0:
        return pl.pallas_call(body, name=name, grid=(n_steps,), in_specs=list(in_specs), out_specs=list(out_specs),
                              out_shape=list(out_shape), scratch_shapes=scratch_shapes,
                              compiler_params=_cp("arbitrary"))(*args)
    n_in, n_out, n_scr = len(in_specs), len(out_specs), len(scratch_shapes)
    relay_step = min(max(1, 3 * n_steps // 4), n_steps - 1)

    def wrapped(*refs):
        ins, refs = refs[:n_in], refs[n_in:]
        xs, shards, refs = refs[:n_s], refs[n_s:n_s + n_g], refs[n_s + n_g:]
        outs, refs = refs[:n_out], refs[n_out:]
        recvs, gathered, refs = refs[:n_s], refs[n_s:n_s + n_g], refs[n_s + n_g:]
        outs, sems = outs + refs[:n_scr], refs[n_scr:]
        s_sems, g_sems = (sems[:3], sems[3:]) if n_s else ((), sems)
        step = pl.program_id(0)

        @pl.when(step == 0)
        def _():
            if n_s:
                _scatter_start(xs, recvs, *s_sems)
            if n_g:
                _gather_start(shards, gathered, *g_sems)
        body(*ins, *outs)
        if n_g:
            @pl.when(step == relay_step)
            def _():
                _gather_relay(shards, gathered, *g_sems)

        @pl.when(step == n_steps - 1)
        def _():
            if n_s:
                _scatter_wait(xs, recvs, *s_sems)
            if n_g:
                _gather_finish(shards, gathered, *g_sems)

    any_spec = pl.BlockSpec(memory_space=pl.ANY)
    sems = []
    for n in (n_s, n_g):
        if n:
            sems += [pltpu.SemaphoreType.DMA((n, N_DEV - 1)), pltpu.SemaphoreType.DMA((n, N_DEV - 1)),
                     pltpu.SemaphoreType.DMA((n,))]
    return pl.pallas_call(
        wrapped, name=name, grid=(n_steps,), in_specs=list(in_specs) + [any_spec] * (n_s + n_g),
        out_specs=list(out_specs) + [any_spec] * (n_s + n_g),
        out_shape=list(out_shape) + [SDS(b.shape, b.dtype) for b in scatter]
        + [SDS((N_DEV,) + b.shape, b.dtype) for b in gather],
        scratch_shapes=scratch_shapes + sems, compiler_params=_cp("arbitrary"))(*args, *scatter, *gather)


def _ffn_gu(x, g, wgu, name, gather=(), tm=512):
    T, D = x.shape
    fb = wgu.shape[2]
    nh = N_DEV // 2
    tm = _tile(T, tm, SUBLANES)

    def body(x_ref, g_ref, w_ref, ab_ref, act_ref, h_ref):
        xv = x_ref[...]
        h = (xv * _rstd(xv) * g_ref[...]).astype(BF16)
        h_ref[...] = h
        for d in range(nh):
            gt = _nn(h, w_ref[d])
            up = _nn(h, w_ref[d + nh])
            sg = jax.nn.sigmoid(gt)
            silu = gt * sg
            ab_ref[d] = (0.5 * up * (sg * (1.0 + gt * (1.0 - sg)))).astype(BF16)
            ab_ref[d + nh] = (0.5 * silu).astype(BF16)
            act_ref[d] = (silu * up).astype(BF16)

    row = pl.BlockSpec((tm, D), lambda i: (i, 0))
    blocks = lambda n: pl.BlockSpec((n, tm, fb), lambda i: (0, i, 0))
    return _hosted_call(
        body, name, T // tm, [row, _full(g), _full(wgu)], [blocks(N_DEV), blocks(nh), row],
        [SDS((N_DEV, T, fb), BF16), SDS((nh, T, fb), BF16), SDS((T, D), BF16)], (x, g, wgu), gather=gather)


def _ffn_down(act, wd, res, name, gather=(), tm=1024):
    nh, T, fb = act.shape
    D = wd.shape[2]
    tm = _tile(T, tm, SUBLANES)

    def body(a_ref, w_ref, r_ref, o_ref):
        wv = w_ref[...].reshape(nh, fb, D)
        acc = _nn(a_ref[0], wv[0])
        for d in range(1, nh):
            acc = acc + _nn(a_ref[d], wv[d])
        o_ref[...] = r_ref[...] + 0.5 * acc

    row = pl.BlockSpec((tm, D), lambda i: (i, 0))
    return _hosted_call(
        body, name, T // tm, [pl.BlockSpec((nh, tm, fb), lambda i: (0, i, 0)), _full(wd), row], [row],
        [SDS((T, D), F32)], (act, wd, res), gather=gather)


def _ffn_bwd_act(dyb, wd, ab, name, tm=512):
    T, D = dyb.shape
    _, _, fb = ab.shape
    nh = N_DEV // 2
    tm = _tile(T, tm, SUBLANES)

    def body(dy_ref, w_ref, ab_ref, o_ref):
        wv = w_ref[...].reshape(nh, fb, D)
        dy = dy_ref[...]
        for d in range(nh):
            da = _nt(dy, wv[d])
            o_ref[d] = (da * ab_ref[d].astype(F32)).astype(BF16)
            o_ref[d + nh] = (da * ab_ref[d + nh].astype(F32)).astype(BF16)

    blocks = pl.BlockSpec((N_DEV, tm, fb), lambda i: (0, i, 0))
    return pl.pallas_call(
        body, name=name, grid=(T // tm,),
        in_specs=[pl.BlockSpec((tm, D), lambda i: (i, 0)), _full(wd), blocks], out_specs=blocks,
        out_shape=SDS((N_DEV, T, fb), BF16), compiler_params=_cp("parallel"))(dyb, wd, ab)


def _ffn_dwd(act, dyb, name, tt=2048):
    nh, T, fb = act.shape
    D = dyb.shape[1]
    tt = _tile(T, tt, 2 * SUBLANES)
    nt = T // tt

    def body(a_ref, b_ref, o_ref, acc_ref):
        t = pl.program_id(1)

        @pl.when(t == 0)
        def _():
            acc_ref[...] = jnp.zeros_like(acc_ref)
        for s in range(2):
            acc_ref[s] += _tn(a_ref[s], b_ref[...])

        @pl.when(t == nt - 1)
        def _():
            o_ref[...] = (0.5 * acc_ref[...]).astype(BF16)

    return pl.pallas_call(
        body, name=name, grid=(nh // 2, nt),
        in_specs=[pl.BlockSpec((2, tt, fb), lambda j, t: (j, t, 0)), pl.BlockSpec((tt, D), lambda j, t: (t, 0))],
        out_specs=pl.BlockSpec((2, fb, D), lambda j, t: (j, 0, 0)), out_shape=SDS((nh, fb, D), BF16),
        scratch_shapes=[pltpu.VMEM((2, fb, D), F32)], compiler_params=_cp("parallel", "arbitrary"))(act, dyb)


def _ffn_dwgu(h, dgu, name, tt=2048):
    T, D = h.shape
    _, _, fb = dgu.shape
    tt = _tile(T, tt, 2 * SUBLANES)
    nt = T // tt

    def body(a_ref, b_ref, o_ref, acc_ref):
        t = pl.program_id(1)

        @pl.when(t == 0)
        def _():
            acc_ref[...] = jnp.zeros_like(acc_ref)
        for s in range(2):
            acc_ref[s] += _tn(b_ref[s], a_ref[...])

        @pl.when(t == nt - 1)
        def _():
            o_ref[...] = acc_ref[...].astype(BF16)

    return pl.pallas_call(
        body, name=name, grid=(N_DEV // 2, nt),
        in_specs=[pl.BlockSpec((tt, D), lambda j, t: (t, 0)), pl.BlockSpec((2, tt, fb), lambda j, t: (j, t, 0))],
        out_specs=pl.BlockSpec((2, fb, D), lambda j, t: (j, 0, 0)), out_shape=SDS((N_DEV, fb, D), BF16),
        scratch_shapes=[pltpu.VMEM((2, fb, D), F32)], compiler_params=_cp("parallel", "arbitrary"))(h, dgu)


def _ffn_bwd_in(dgu, wgu, x, g, dres, name, scatter, tm=512):
    T, D = x.shape
    _, _, fb = dgu.shape
    tm = _tile(T, tm, SUBLANES)

    def body(a_ref, w_ref, x_ref, g_ref, r_ref, dx_ref, dxb_ref, dg_ref):
        dh = _nt(a_ref[0], w_ref[0])
        for d in range(1, N_DEV):
            dh = dh + _nt(a_ref[d], w_ref[d])
        _norm_bwd_store(dh, x_ref, g_ref, r_ref, dx_ref, dxb_ref, dg_ref, pl.program_id(0) == 0)

    row = pl.BlockSpec((tm, D), lambda i: (i, 0))
    return _hosted_call(
        body, name, T // tm, [pl.BlockSpec((N_DEV, tm, fb), lambda i: (0, i, 0)), _full(wgu), row, _full(g), row],
        [row, row, pl.BlockSpec((1, D), lambda i: (0, 0))],
        [SDS((T, D), F32), SDS((T, D), BF16), SDS((1, D), F32)], (dgu, wgu, x, g, dres), scatter)


def _loss_head(x, g, tgt, name, tm=512):
    T, D = x.shape
    tm = _tile(T, tm, SUBLANES)

    def body(x_ref, g_ref, t_ref, loss_ref, dx_ref, dxb_ref, dg_ref):
        first = pl.program_id(0) == 0
        xv = x_ref[...]
        r = _rstd(xv)
        xh = xv * r
        e = xh * g_ref[...] - t_ref[...]
        part = 0.5 * jnp.sum(jnp.mean(e * e, axis=-1, keepdims=True), axis=0, keepdims=True)
        dy = e * (1.0 / D)
        dxh = dy * g_ref[...]
        dx = r * (dxh - xh * jnp.mean(dxh * xh, axis=-1, keepdims=True))
        dx_ref[...] = dx
        dxb_ref[...] = dx.astype(BF16)
        _accumulate(loss_ref, first, jnp.broadcast_to(part, loss_ref.shape))
        _accumulate(dg_ref, first, jnp.sum(dy * xh, axis=0, keepdims=True))

    row = pl.BlockSpec((tm, D), lambda i: (i, 0))
    vec = pl.BlockSpec((1, D), lambda i: (0, 0))
    return pl.pallas_call(
        body, name=name, grid=(T // tm,), in_specs=[row, vec, row],
        out_specs=[pl.BlockSpec((SUBLANES, LANES), lambda i: (0, 0)), row, row, vec],
        out_shape=[SDS((SUBLANES, LANES), F32), SDS((T, D), F32), SDS((T, D), BF16), SDS((1, D), F32)],
        compiler_params=_cp("arbitrary"))(x, g, tgt)


def _shift_down(u, k, prev_row):
    rows = lax.broadcasted_iota(I32, u.shape, 0)
    s = pltpu.roll(u, k, 0)
    for t in range(k):
        s = jnp.where(rows == t, prev_row(SUBLANES - k + t), s)
    return s


def _shift_up(u, k, next_row):
    n = u.shape[0]
    rows = lax.broadcasted_iota(I32, u.shape, 0)
    s = pltpu.roll(u, n - k, 0)
    for t in range(k):
        s = jnp.where(rows == n - k + t, next_row(t), s)
    return s


def _conv_taps(z_ref, zp_ref, w_ref, cw, first):
    u = z_ref[:, cw:2 * cw] * z_ref[:, 2 * cw:]

    def prev_row(r):
        return jnp.where(first, 0.0, zp_ref[r:r + 1, cw:2 * cw] * zp_ref[r:r + 1, 2 * cw:])

    u1 = _shift_down(u, 1, prev_row)
    u2 = _shift_down(u, 2, prev_row)
    cv = w_ref[0:1, :] * u2 + w_ref[1:2, :] * u1 + w_ref[2:3, :] * u
    return u, u1, u2, cv


def _mix_in_conv(x, g, w, wc, gc, name, tm=512):
    T, D = x.shape
    cw = w.shape[0] // 3
    tm = _tile(T, tm, SUBLANES)

    def body(x_ref, g_ref, w_ref, wc_ref, gc_ref, z_ref, h_ref, o_ref, prev_ref):
        xv = x_ref[...]
        h = (xv * _rstd(xv) * g_ref[...]).astype(BF16)
        h_ref[...] = h
        z_ref[...] = _nt(h, w_ref[...])
        _, _, _, cv = _conv_taps(z_ref, prev_ref, wc_ref, cw, pl.program_id(0) == 0)
        y = z_ref[:, :cw] * cv
        o_ref[...] = (y * _rstd(y) * gc_ref[...]).astype(BF16)
        prev_ref[...] = z_ref[tm - SUBLANES:, :]

    row = lambda n: pl.BlockSpec((tm, n), lambda i: (i, 0))
    return pl.pallas_call(
        body, name=name, grid=(T // tm,), in_specs=[row(D), _full(g), _full(w), _full(wc), _full(gc)],
        out_specs=[row(3 * cw), row(D), row(cw)],
        out_shape=[SDS((T, 3 * cw), F32), SDS((T, D), BF16), SDS((T, cw), BF16)],
        scratch_shapes=[pltpu.VMEM((SUBLANES, 3 * cw), F32)], compiler_params=_cp("arbitrary"))(x, g, w, wc, gc)


def _mix_bwd_head(dyb, w_out, zc, wc, gc, ot, g_col, name, scatter):
    nb, aw, tm = ot.shape
    T, D = dyb.shape
    cw = zc.shape[1] // 3
    H = aw // HEAD_DIM
    hb = tm // SUBLANES

    def body(dyb_ref, wo_ref, z_ref, zp_ref, w_ref, g_ref, o_ref, ga_ref,
             dz_ref, dw_ref, dg_ref, do_ref, dl_ref, dga_ref, carry_ref):
        s = pl.program_id(0)
        dy = _nt(dyb_ref[...], _rows2d(wo_ref))
        ov = o_ref[...]
        ra = lax.rsqrt(jnp.mean(ov * ov, axis=0, keepdims=True) + EPS)
        oh = ov * ra
        dya = dy[:, cw:].T
        dyah = dya * ga_ref[...]
        dov = ra * (dyah - oh * jnp.mean(dyah * oh, axis=0, keepdims=True))
        do_ref[...] = dov.astype(BF16)
        dl_ref[...] = jnp.sum((dov * ov).reshape(H, HEAD_DIM, tm), axis=1)
        _accumulate(dga_ref, s == 0, jnp.broadcast_to(jnp.sum(dya * oh, axis=1, keepdims=True), dga_ref.shape))
        u, u1, u2, cv = _conv_taps(z_ref, zp_ref, w_ref, cw, s == nb - 1)
        zb = z_ref[:, :cw]
        y = zb * cv
        r = _rstd(y)
        yh = y * r
        dyn = dy[:, :cw]
        dyh = dyn * g_ref[...]
        dyc = r * (dyh - yh * jnp.mean(dyh * yh, axis=-1, keepdims=True))
        dcv = dyc * zb

        def next_row(t):
            return jnp.where(s == 0, 0.0, carry_ref[t:t + 1, :])

        du = w_ref[2:3, :] * dcv + w_ref[1:2, :] * _shift_up(dcv, 1, next_row) + w_ref[0:1, :] * _shift_up(dcv, 2, next_row)
        carry_ref[...] = dcv[0:SUBLANES, :]
        dz_ref[:, :cw] = (dyc * cv).astype(BF16)
        dz_ref[:, cw:2 * cw] = (du * z_ref[:, 2 * cw:]).astype(BF16)
        dz_ref[:, 2 * cw:] = (du * z_ref[:, cw:2 * cw]).astype(BF16)
        tap = lax.broadcasted_iota(I32, (SUBLANES, cw), 0)
        dwp = jnp.where(tap == 0, jnp.sum(dcv * u2, axis=0, keepdims=True),
                        jnp.where(tap == 1, jnp.sum(dcv * u1, axis=0, keepdims=True),
                                  jnp.where(tap == 2, jnp.sum(dcv * u, axis=0, keepdims=True), 0.0)))
        _accumulate(dw_ref, s == 0, dwp)
        _accumulate(dg_ref, s == 0, jnp.sum(dyn * yh, axis=0, keepdims=True))

    rev = lambda s: nb - 1 - s
    rows = lambda n: pl.BlockSpec((tm, n), lambda s: (rev(s), 0))
    blk = pl.BlockSpec((None, aw, tm), lambda s: (rev(s), 0, 0))
    return _hosted_call(
        body, name, nb,
        [rows(D), _full(w_out), rows(3 * cw),
         pl.BlockSpec((SUBLANES, 3 * cw), lambda s: (jnp.maximum(rev(s) * hb - 1, 0), 0)),
         _full(wc), _full(gc), blk, _full(g_col)],
        [rows(3 * cw), pl.BlockSpec((SUBLANES, cw), lambda s: (0, 0)), pl.BlockSpec((1, cw), lambda s: (0, 0)),
         blk, pl.BlockSpec((None, H, tm), lambda s: (rev(s), 0, 0)), pl.BlockSpec((aw, LANES), lambda s: (0, 0))],
        [SDS((T, 3 * cw), BF16), SDS((SUBLANES, cw), F32), SDS((1, cw), F32),
         SDS((nb, aw, tm), BF16), SDS((nb, H, tm), F32), SDS((aw, LANES), F32)],
        (dyb, w_out, zc, zc, wc, gc, ot, g_col), scatter=scatter, scratch_shapes=[pltpu.VMEM((SUBLANES, cw), F32)])


def _proj_t(h, wat, wft, B, name):
    T, D = h.shape
    R = wat.shape[0]
    nb = T // B
    qk = 2 * R // 3
    g = qk // HEAD_DIM

    def body(h_ref, wa_ref, wf_ref, za_ref, zf_ref, n_ref):
        hv = h_ref[...]
        zab = _nt(wa_ref[...], hv).astype(BF16)
        za_ref[...] = zab
        zf_ref[...] = _nt(wf_ref[...], hv)
        zv = zab[:qk].astype(F32)
        ss = jnp.sum((zv * zv).reshape(g, HEAD_DIM, B), axis=1)
        mx = jnp.broadcast_to(jnp.max(ss, axis=1, keepdims=True), n_ref.shape)

        @pl.when(pl.program_id(0) == 0)
        def _():
            n_ref[...] = mx

        @pl.when(pl.program_id(0) > 0)
        def _():
            n_ref[...] = jnp.maximum(n_ref[...], mx)

    return pl.pallas_call(
        body, name=name, grid=(nb,), in_specs=[pl.BlockSpec((B, D), lambda i: (i, 0)), _full(wat), _full(wft)],
        out_specs=[pl.BlockSpec((None, R, B), lambda i: (i, 0, 0)), pl.BlockSpec((SUBLANES, B), lambda i: (0, i)),
                   pl.BlockSpec((g, LANES), lambda i: (0, 0))],
        out_shape=[SDS((nb, R, B), BF16), SDS((SUBLANES, T), F32), SDS((g, LANES), F32)],
        compiler_params=_cp("arbitrary"))(h, wat, wft)


def _split3(v):
    hi = v.astype(BF16).astype(F32)
    r1 = v - hi
    mid = r1.astype(BF16).astype(F32)
    lo = (r1 - mid).astype(BF16).astype(F32)
    return hi, mid, lo


def _tri_dot3(v, tri):
    hi, mid, lo = _split3(v)
    return _nn(hi.astype(BF16), tri) + _nn(mid.astype(BF16), tri) + _nn(lo.astype(BF16), tri)


def _logf_cumsum(zft, b, tb, name):
    H, T = zft.shape

    def body(z_ref, b_ref, c_ref, qa_ref, ka_ref, carry_ref, hi_ref, mid_ref, lo_ref):
        @pl.when(pl.program_id(0) == 0)
        def _():
            carry_ref[...] = jnp.zeros_like(carry_ref)
        xv = z_ref[...] + b_ref[...]
        lf = jnp.minimum(xv, 0.0) - jnp.log(1.0 + jnp.exp(-jnp.abs(xv)))
        src = lax.broadcasted_iota(I32, (tb, tb), 0)
        dst = lax.broadcasted_iota(I32, (tb, tb), 1)
        tri = jnp.where(src <= dst, 1.0, 0.0).astype(BF16)
        cs = _tri_dot3(lf, tri) + carry_ref[...]
        c_ref[...] = cs
        hi_ref[...], mid_ref[...], lo_ref[...] = _split3(cs)
        col = lax.broadcasted_iota(I32, cs.shape, 1)
        carry_ref[...] = jnp.sum(jnp.where(col == tb - 1, cs, 0.0), axis=1, keepdims=True)
        row = lax.broadcasted_iota(I32, (AUG, tb), 0)
        for h in range(H):
            terms = [r[h:h + 1, :] for r in (hi_ref, mid_ref, lo_ref)]
            qa, ka = jnp.where(row < 3, 1.0, 0.0), jnp.where((row >= 3) & (row < 6), 1.0, 0.0)
            for k, t in enumerate(terms):
                qa = jnp.where(row == 3 + k, t, qa)
                ka = jnp.where(row == k, -t, ka)
            qa_ref[h] = qa
            ka_ref[h] = ka

    blk = pl.BlockSpec((H, tb), lambda i: (0, i))
    aug = pl.BlockSpec((H, None, AUG, tb), lambda i: (0, i, 0, 0))
    return pl.pallas_call(
        body, name=name, grid=(T // tb,), in_specs=[blk, _full(b)], out_specs=[blk, aug, aug],
        out_shape=[SDS((H, T), F32), SDS((H, T // tb, AUG, tb), F32), SDS((H, T // tb, AUG, tb), F32)],
        scratch_shapes=[pltpu.VMEM((H, 1), F32)] + [pltpu.VMEM((H, tb), F32)] * 3,
        compiler_params=_cp("arbitrary"))(zft, b)


def _logf_cumsum_bwd(dcq, dck, zft, b, name, tb=512):
    H, T = zft.shape
    tb = _tile(T, tb, LANES)
    nb = T // tb

    def body(dq_ref, dk_ref, z_ref, b_ref, o_ref, db_ref, carry_ref):
        s = pl.program_id(0)

        @pl.when(s == 0)
        def _():
            carry_ref[...] = jnp.zeros_like(carry_ref)
        dc = dq_ref[...] - dk_ref[...]
        src = lax.broadcasted_iota(I32, (tb, tb), 0)
        dst = lax.broadcasted_iota(I32, (tb, tb), 1)
        tri = jnp.where(src >= dst, 1.0, 0.0).astype(BF16)
        dl = _tri_dot3(dc, tri) + carry_ref[...]
        col = lax.broadcasted_iota(I32, dl.shape, 1)
        carry_ref[...] = jnp.sum(jnp.where(col == 0, dl, 0.0), axis=1, keepdims=True)
        dz = dl * jax.nn.sigmoid(-(z_ref[...] + b_ref[...]))
        o_ref[...] = dz
        _accumulate(db_ref, s == 0, jnp.sum(dz, axis=1, keepdims=True))

    blk = pl.BlockSpec((H, tb), lambda s: (0, nb - 1 - s))
    vec = pl.BlockSpec((H, 1), lambda s: (0, 0))
    return pl.pallas_call(
        body, name=name, grid=(nb,), in_specs=[blk, blk, blk, vec], out_specs=[blk, vec],
        out_shape=[SDS((H, T), F32), SDS((H, 1), F32)],
        scratch_shapes=[pltpu.VMEM((H, 1), F32)], compiler_params=_cp("arbitrary"))(dcq, dck, zft, b)


def _skip_table(cs_col, ce_row, norms, name):
    H, nb, _ = cs_col.shape

    def body(cs_ref, ce_ref, n_ref, jm_ref, im_ref):
        h = pl.program_id(0)
        nq = n_ref[pl.ds(h, 1), 0:1]
        nk = n_ref[pl.ds(H + h, 1), 0:1]
        bound = 2.0 * jnp.sqrt(nq * nk) * 0.125
        skip = jnp.where(bound + cs_ref[...] - ce_ref[...] <= -SKIP_MARGIN, 1, 0).astype(I32)
        jm_ref[...] = jnp.sum(skip, axis=1, keepdims=True)
        im_ref[...] = nb - 1 - jnp.sum(skip, axis=0, keepdims=True)

    col = pl.BlockSpec((None, nb, 1), lambda h: (h, 0, 0))
    row = pl.BlockSpec((None, 1, nb), lambda h: (h, 0, 0))
    return pl.pallas_call(
        body, name=name, grid=(H,), in_specs=[col, row, _full(norms)], out_specs=[col, row],
        out_shape=[SDS((H, nb, 1), I32), SDS((H, 1, nb), I32)], compiler_params=_cp("parallel"))(cs_col, ce_row, norms)


def _causal_mask(B):
    krow = lax.broadcasted_iota(I32, (B, B), 0)
    qcol = lax.broadcasted_iota(I32, (B, B), 1)
    return krow <= qcol


def _attn_fwd(jmin, zat, qa, ka, name):
    nb, R, B = zat.shape
    aw = R // 3
    H = aw // HEAD_DIM
    hd = HEAD_DIM

    def body(jm_ref, q_ref, k_ref, v_ref, qa_ref, ka_ref, o_ref, lse_ref):
        h = pl.program_id(0)
        i = pl.program_id(1)
        qt = jnp.concatenate([q_ref[...] * 0.125, qa_ref[...].astype(BF16)], axis=0)

        def block(j, carry, masked):
            m, l, acc = carry
            s = _tn(jnp.concatenate([k_ref[j], ka_ref[j].astype(BF16)], axis=0), qt)
            if masked:
                s = jnp.where(_causal_mask(B), s, NEG)
            m_new = jnp.maximum(m, jnp.max(s, axis=0, keepdims=True))
            p = jnp.exp(s - m_new)
            a = jnp.exp(m - m_new)
            l = a * l + jnp.sum(p, axis=0, keepdims=True)
            acc = a * acc + _nn(v_ref[j], p.astype(BF16))
            return m_new, l, acc

        init = (jnp.full((1, B), NEG, F32), jnp.zeros((1, B), F32), jnp.zeros((hd, B), F32))
        carry = lax.fori_loop(jm_ref[h * nb + i], i, lambda j, c: block(j, c, False), init)
        m, l, acc = block(i, carry, True)
        o_ref[...] = acc / l
        lse_ref[...] = m + jnp.log(l)

    grid_spec = pltpu.PrefetchScalarGridSpec(
        num_scalar_prefetch=1, grid=(H, nb),
        in_specs=[pl.BlockSpec((None, hd, B), lambda h, i, jm: (i, h, 0)),
                  pl.BlockSpec((nb, hd, B), lambda h, i, jm: (0, H + h, 0)),
                  pl.BlockSpec((nb, hd, B), lambda h, i, jm: (0, 2 * H + h, 0)),
                  pl.BlockSpec((None, None, AUG, B), lambda h, i, jm: (h, i, 0, 0)),
                  pl.BlockSpec((None, nb, AUG, B), lambda h, i, jm: (h, 0, 0, 0))],
        out_specs=[pl.BlockSpec((None, hd, B), lambda h, i, jm: (i, h, 0)),
                   pl.BlockSpec((None, None, 1, B), lambda h, i, jm: (h, i, 0, 0))])
    return pl.pallas_call(
        body, name=name, grid_spec=grid_spec,
        out_shape=[SDS((nb, aw, B), F32), SDS((H, nb, 1, B), F32)],
        compiler_params=_cp("parallel", "parallel"))(jmin, zat, zat, zat, qa, ka)


def _gnorm_t_fwd(ot, g_col, name):
    nb, aw, B = ot.shape

    def body(o_ref, g_ref, y_ref):
        ov = o_ref[...]
        r = lax.rsqrt(jnp.mean(ov * ov, axis=0, keepdims=True) + EPS)
        y_ref[...] = (ov * r * g_ref[...]).T.astype(BF16)

    return pl.pallas_call(
        body, name=name, grid=(nb,), in_specs=[pl.BlockSpec((None, aw, B), lambda i: (i, 0, 0)), _full(g_col)],
        out_specs=pl.BlockSpec((B, aw), lambda i: (i, 0)), out_shape=SDS((nb * B, aw), BF16),
        compiler_params=_cp("parallel"))(ot, g_col)


def _attn_bwd(imax, zat, qa, ka, dot, lse, delta, name):
    nb, R, B = zat.shape
    aw = R // 3
    H = aw // HEAD_DIM
    hd = HEAD_DIM

    def body(im_ref, k_ref, v_ref, ka_ref, q_ref, qa_ref, do_ref, lse_ref, dl_ref,
             dq_ref, dqa_ref, dk_ref, dka_ref, dv_ref, dq_acc):
        h = pl.program_id(0)
        j = pl.program_id(1)

        @pl.when(j == 0)
        def _():
            dq_acc[...] = jnp.zeros_like(dq_acc)
            dqa_ref[...] = jnp.zeros_like(dqa_ref)
        kt = jnp.concatenate([k_ref[...], ka_ref[...].astype(BF16)], axis=0)
        vt = v_ref[...]

        def block(i, carry, masked):
            dk, dv = carry
            qt = jnp.concatenate([q_ref[i] * 0.125, qa_ref[i].astype(BF16)], axis=0)
            dov = do_ref[i]
            s = _tn(kt, qt)
            if masked:
                s = jnp.where(_causal_mask(B), s, NEG)
            p = jnp.exp(s - lse_ref[i])
            dp = _tn(vt, dov)
            ds = (p * (dp - dl_ref[i, pl.ds(h, 1), :])).astype(BF16)
            dv = dv + _nt(dov, p.astype(BF16))
            dk = dk + _nt(qt, ds)
            r = _nn(kt, ds)
            dq_acc[i] += 0.125 * r[:hd]
            dqa_ref[i] += r[hd:]
            return dk, dv

        carry = block(j, (jnp.zeros((hd + AUG, B), F32), jnp.zeros((hd, B), F32)), True)
        dk, dv = lax.fori_loop(j + 1, im_ref[h * nb + j] + 1, lambda i, c: block(i, c, False), carry)
        dk_ref[...] = dk[:hd].astype(BF16)
        dka_ref[...] = dk[hd:]
        dv_ref[...] = dv.astype(BF16)

        @pl.when(j == nb - 1)
        def _():
            dq_ref[...] = dq_acc[...].astype(BF16)

    head_rows = lambda off: pl.BlockSpec((nb, hd, B), lambda h, j, im: (0, off + h, 0))
    key_rows = lambda off: pl.BlockSpec((None, hd, B), lambda h, j, im: (j, off + h, 0))
    aug_all = pl.BlockSpec((None, nb, AUG, B), lambda h, j, im: (h, 0, 0, 0))
    aug_one = pl.BlockSpec((None, None, AUG, B), lambda h, j, im: (h, j, 0, 0))
    grid_spec = pltpu.PrefetchScalarGridSpec(
        num_scalar_prefetch=1, grid=(H, nb),
        in_specs=[key_rows(H), key_rows(2 * H), aug_one, head_rows(0), aug_all, head_rows(0),
                  pl.BlockSpec((None, nb, 1, B), lambda h, j, im: (h, 0, 0, 0)),
                  pl.BlockSpec((nb, H, B), lambda h, j, im: (0, 0, 0))],
        out_specs=[head_rows(0), aug_all, key_rows(0), aug_one, key_rows(0)],
        scratch_shapes=[pltpu.VMEM((nb, hd, B), F32)])
    return pl.pallas_call(
        body, name=name, grid_spec=grid_spec,
        out_shape=[SDS((nb, aw, B), BF16), SDS((H, nb, AUG, B), F32), SDS((nb, aw, B), BF16),
                   SDS((H, nb, AUG, B), F32), SDS((nb, aw, B), BF16)],
        compiler_params=_cp("arbitrary", "arbitrary"))(imax, zat, zat, ka, zat, qa, dot, lse, delta)


def _mix_dwt(dq, dk, dv, dzft, h, name):
    nb, aw, B = dq.shape
    D = h.shape[1]

    def body(dq_ref, dk_ref, dv_ref, dzf_ref, h_ref, o_ref, of_ref):
        first = pl.program_id(0) == 0
        hv = h_ref[...]
        part = jnp.concatenate([_nn(r[...], hv) for r in (dq_ref, dk_ref, dv_ref)], axis=0)
        _accumulate(o_ref, first, part)
        _accumulate(of_ref, first, _nn(dzf_ref[...].astype(BF16), hv))

    blk = pl.BlockSpec((None, aw, B), lambda i: (i, 0, 0))
    return pl.pallas_call(
        body, name=name, grid=(nb,),
        in_specs=[blk, blk, blk, pl.BlockSpec((SUBLANES, B), lambda i: (0, i)), pl.BlockSpec((B, D), lambda i: (i, 0))],
        out_specs=[pl.BlockSpec((3 * aw, D), lambda i: (0, 0)), pl.BlockSpec((SUBLANES, D), lambda i: (0, 0))],
        out_shape=[SDS((3 * aw, D), F32), SDS((SUBLANES, D), F32)],
        compiler_params=_cp("arbitrary"))(dq, dk, dv, dzft, h)


def _mix_bwd_in(dzc, dq, dk, dv, dzft, wc, wat, wft, x, g, dres, name, scatter):
    T, D = x.shape
    nb, aw, B = dq.shape

    def body(dzc_ref, dq_ref, dk_ref, dv_ref, dzf_ref, wc_ref, wa_ref, wf_ref, x_ref, g_ref, r_ref,
             dx_ref, dxb_ref, dg_ref):
        dh = _nn(dzc_ref[...], wc_ref[...])
        for s, r in enumerate((dq_ref, dk_ref, dv_ref)):
            dh = dh + _tn(r[...], wa_ref[s * aw:(s + 1) * aw, :])
        dh = dh + _tn(dzf_ref[...].astype(BF16), wf_ref[...])
        _norm_bwd_store(dh, x_ref, g_ref, r_ref, dx_ref, dxb_ref, dg_ref, pl.program_id(0) == 0)

    blk = pl.BlockSpec((None, aw, B), lambda i: (i, 0, 0))
    row = lambda n: pl.BlockSpec((B, n), lambda i: (i, 0))
    return _hosted_call(
        body, name, nb,
        [row(dzc.shape[1]), blk, blk, blk, pl.BlockSpec((SUBLANES, B), lambda i: (0, i)),
         _full(wc), _full(wat), _full(wft), row(D), _full(g), row(D)],
        [row(D), row(D), pl.BlockSpec((1, D), lambda i: (0, 0))],
        [SDS((T, D), F32), SDS((T, D), BF16), SDS((1, D), F32)],
        (dzc, dq, dk, dv, dzft, wc, wat, wft, x, g, dres), scatter)


def _mem_kv(mem, g, wkv, name):
    M, D = mem.shape
    dx = wkv.shape[2]

    def body(m_ref, g_ref, w_ref, kv_ref, h_ref):
        mv = m_ref[...]
        h = (mv * _rstd(mv) * g_ref[...]).astype(BF16)
        h_ref[...] = h
        for s in range(N_DEV):
            kv_ref[s] = _nn(h, w_ref[s]).astype(BF16)

    return pl.pallas_call(
        body, name=name, grid=(1,), in_specs=[_full(mem), _full(g), _full(wkv)],
        out_specs=[pl.BlockSpec((N_DEV, M, dx), lambda i: (0, 0, 0)), pl.BlockSpec((M, D), lambda i: (0, 0))],
        out_shape=[SDS((N_DEV, M, dx), BF16), SDS((M, D), BF16)], compiler_params=_cp("arbitrary"))(mem, g, wkv)


def _mem_kv_bwd(dkv, hm, wkv, mem, g, name):
    M, D = mem.shape
    dx = wkv.shape[2]

    def body(dkv_ref, h_ref, w_ref, m_ref, g_ref, dw_ref, dg_ref):
        hv = h_ref[...]
        dh = jnp.zeros((M, D), F32)
        for s in range(N_DEV):
            d = dkv_ref[s].astype(BF16)
            dw_ref[s] = _tn(hv, d).astype(BF16)
            dh = dh + _nt(d, w_ref[s])
        mv = m_ref[...]
        dg_ref[...] = jnp.sum(dh * (mv * _rstd(mv)), axis=0, keepdims=True)

    return pl.pallas_call(
        body, name=name, grid=(1,), in_specs=[_full(dkv), _full(hm), _full(wkv), _full(mem), _full(g)],
        out_specs=[pl.BlockSpec((N_DEV, D, dx), lambda i: (0, 0, 0)), pl.BlockSpec((1, D), lambda i: (0, 0))],
        out_shape=[SDS((N_DEV, D, dx), BF16), SDS((1, D), F32)], compiler_params=_cp("arbitrary"))(dkv, hm, wkv, mem, g)


def _xattn_probs(q_ref, kv_ref, h, dx, scale):
    qh = q_ref[:, h * dx:(h + 1) * dx]
    kh = kv_ref[h]
    s = _nt(qh, kh) * scale
    p = jnp.exp(s - jnp.max(s, axis=-1, keepdims=True))
    return qh, kh, p / jnp.sum(p, axis=-1, keepdims=True)


def _xattn(x, g, wq, kv, wo, name, tm=512):
    T, D = x.shape
    dx = D // N_XHEADS
    scale = dx ** -0.5
    tm = _tile(T, tm, SUBLANES)

    def body(x_ref, g_ref, wq_ref, kv_ref, wo_ref, xo_ref, h_ref, q_ref, o_ref):
        xv = x_ref[...]
        h = (xv * _rstd(xv) * g_ref[...]).astype(BF16)
        h_ref[...] = h
        q_ref[...] = _nn(h, _rows2d(wq_ref)).astype(BF16)
        for hd in range(N_XHEADS):
            _, _, p = _xattn_probs(q_ref, kv_ref, hd, dx, scale)
            o_ref[:, hd * dx:(hd + 1) * dx] = _nn(p.astype(BF16), kv_ref[N_XHEADS + hd]).astype(BF16)
        xo_ref[...] = xv + _nn(o_ref[...], _rows2d(wo_ref))

    row = pl.BlockSpec((tm, D), lambda i: (i, 0))
    return pl.pallas_call(
        body, name=name, grid=(T // tm,), in_specs=[row, _full(g), _full(wq), _full(kv), _full(wo)],
        out_specs=[row] * 4, out_shape=[SDS((T, D), F32)] + [SDS((T, D), BF16)] * 3,
        compiler_params=_cp("parallel"))(x, g, wq, kv, wo)


def _xattn_bwd(dyb, dres, x, g, q, kv, wq, wo, name, tm=512):
    T, D = x.shape
    dx = D // N_XHEADS
    scale = dx ** -0.5
    tm = _tile(T, tm, SUBLANES)

    def body(dy_ref, r_ref, x_ref, g_ref, q_ref, kv_ref, wq_ref, wo_ref, dx_ref, dxb_ref, dg_ref, dq_ref, dkv_ref):
        first = pl.program_id(0) == 0

        @pl.when(first)
        def _():
            dkv_ref[...] = jnp.zeros_like(dkv_ref)
        do = _nt(dy_ref[...], _rows2d(wo_ref)).astype(BF16)
        for hd in range(N_XHEADS):
            qh, kh, p = _xattn_probs(q_ref, kv_ref, hd, dx, scale)
            doh = do[:, hd * dx:(hd + 1) * dx]
            dp = _nt(doh, kv_ref[N_XHEADS + hd])
            ds = (p * (dp - jnp.sum(p * dp, axis=-1, keepdims=True)) * scale).astype(BF16)
            dq_ref[:, hd * dx:(hd + 1) * dx] = _nn(ds, kh).astype(BF16)
            dkv_ref[hd] += _tn(ds, qh)
            dkv_ref[N_XHEADS + hd] += _tn(p.astype(BF16), doh)
        dh = _nt(dq_ref[...], _rows2d(wq_ref))
        _norm_bwd_store(dh, x_ref, g_ref, r_ref, dx_ref, dxb_ref, dg_ref, first)

    row = pl.BlockSpec((tm, D), lambda i: (i, 0))
    return pl.pallas_call(
        body, name=name, grid=(T // tm,),
        in_specs=[row, row, row, _full(g), row, _full(kv), _full(wq), _full(wo)],
        out_specs=[row, row, pl.BlockSpec((1, D), lambda i: (0, 0)), row, pl.BlockSpec(kv.shape, lambda i: (0, 0, 0))],
        out_shape=[SDS((T, D), F32), SDS((T, D), BF16), SDS((1, D), F32), SDS((T, D), BF16), SDS(kv.shape, F32)],
        compiler_params=_cp("arbitrary"))(dyb, dres, x, g, q, kv, wq, wo)


def _mesh_position():
    return lax.axis_index("x"), lax.axis_index("y"), lax.axis_index("c")


class _GatherCopies:
    def __init__(self, x_refs, out_refs, send_sems, recv_sems, local_sems):
        self.x_refs, self.out_refs, self.n = x_refs, out_refs, len(x_refs)
        self.send_sems, self.recv_sems, self.local_sems = send_sems, recv_sems, local_sems
        x, y, c = _mesh_position()
        self.c = c
        self.me, self.sibling = (x, y, c), (x, y, 1 - c)
        self.chips = [(1 - x, y), (x, 1 - y), (1 - x, 1 - y)]

    def slot(self, w, px, py, pc):
        return self.out_refs[w].at[4 * px + 2 * py + pc]

    def copy(self, w, k, block, to, src=None):
        return pltpu.make_async_remote_copy(
            src_ref=self.slot(w, *block) if src is None else src, dst_ref=self.slot(w, *block),
            send_sem=self.send_sems.at[w, k], recv_sem=self.recv_sems.at[w, k], device_id=to,
            device_id_type=pl.DeviceIdType.MESH)

    def mine(self, w):
        return pltpu.make_async_copy(self.x_refs[w], self.slot(w, *self.me), self.local_sems.at[w])

    def first(self, w):
        src = self.x_refs[w]
        return [self.copy(w, 0, self.me, self.sibling, src=src)] + [
            self.copy(w, 1 + j, self.me, (*chip, self.c), src=src) for j, chip in enumerate(self.chips)]

    def passed(self, w):
        return [self.copy(w, 4 + j, (*chip, self.c), self.sibling) for j, chip in enumerate(self.chips)]


def _gather_start(*refs):
    cp = _GatherCopies(*refs)
    for w in range(cp.n):
        cp.mine(w).start()
        for f in cp.first(w):
            f.start()


def _gather_relay(*refs):
    cp = _GatherCopies(*refs)
    for w in range(cp.n):
        for j, (chip, fwd) in enumerate(zip(cp.chips, cp.passed(w))):
            cp.copy(w, 1 + j, (*chip, cp.c), cp.me).wait_recv()
            fwd.start()


def _gather_finish(*refs):
    cp = _GatherCopies(*refs)
    for w in range(cp.n):
        cp.copy(w, 0, cp.sibling, cp.me).wait_recv()
        for j, chip in enumerate(cp.chips):
            cp.copy(w, 4 + j, (*chip, 1 - cp.c), cp.me).wait_recv()
    for w in range(cp.n):
        for f in cp.first(w) + cp.passed(w):
            f.wait_send()
        cp.mine(w).wait()


def _all_gather(shards, name):
    n = len(shards)

    def body(*refs):
        parts = (refs[:n], refs[n:2 * n]) + refs[2 * n:]
        _gather_start(*parts)
        _gather_relay(*parts)
        _gather_finish(*parts)

    any_spec = pl.BlockSpec(memory_space=pl.ANY)
    return pl.pallas_call(
        body, name=name, out_shape=[SDS((N_DEV,) + s.shape, s.dtype) for s in shards],
        in_specs=[any_spec] * n, out_specs=[any_spec] * n,
        scratch_shapes=[pltpu.SemaphoreType.DMA((n, 7)), pltpu.SemaphoreType.DMA((n, 7)), pltpu.SemaphoreType.DMA((n,))],
    )(*shards)


def _scatter_copies(x_refs, out_refs, send_sems, recv_sems, local_sems):
    x, y, c = _mesh_position()
    me = 4 * x + 2 * y + c
    own, remote = [], []
    for w, (src, dst) in enumerate(zip(x_refs, out_refs)):
        own.append(pltpu.make_async_copy(src.at[me], dst.at[me], local_sems.at[w]))
        for k in range(1, N_DEV):
            px = 1 - x if k & 4 else x
            py = 1 - y if k & 2 else y
            pc = 1 - c if k & 1 else c
            remote.append(pltpu.make_async_remote_copy(
                src_ref=src.at[4 * px + 2 * py + pc], dst_ref=dst.at[me], send_sem=send_sems.at[w, k - 1],
                recv_sem=recv_sems.at[w, k - 1], device_id=(px, py, pc), device_id_type=pl.DeviceIdType.MESH))
    return own, remote


def _scatter_start(x_refs, out_refs, send_sems, recv_sems, local_sems):
    own, remote = _scatter_copies(x_refs, out_refs, send_sems, recv_sems, local_sems)
    for cp in own + remote:
        cp.start()


def _scatter_wait(x_refs, out_refs, send_sems, recv_sems, local_sems):
    own, remote = _scatter_copies(x_refs, out_refs, send_sems, recv_sems, local_sems)
    for cp in remote:
        cp.wait_recv()
    for cp in remote:
        cp.wait_send()
    for cp in own:
        cp.wait()


def _sum_adamw(parts, w, m, v, name, tr=256):
    L, R, C = w.shape
    tr = _tile(R, tr, SUBLANES)
    nblk = R // tr

    def body(*refs):
        p_refs = refs[:L]
        w_ref, m_ref, v_ref, g_ref, d_ref, mo_ref, vo_ref = refs[L:]
        layer = pl.program_id(0)
        g = None
        for k, p_ref in enumerate(p_refs):
            gk = p_ref[0].astype(F32)
            for s in range(1, N_DEV):
                gk = gk + p_ref[s].astype(F32)
            g = gk if g is None else jnp.where(layer == k, gk, g)
        mn = ADAM_B1 * m_ref[...] + (1.0 - ADAM_B1) * g
        vn = ADAM_B2 * v_ref[...] + (1.0 - ADAM_B2) * jnp.square(g)
        m_hat = mn / (1.0 - ADAM_B1 ** ADAM_STEP)
        v_hat = vn / (1.0 - ADAM_B2 ** ADAM_STEP)
        g_ref[...] = g
        d_ref[...] = -ADAM_LR * (m_hat / (jnp.sqrt(v_hat) + ADAM_EPS) + ADAM_WD * w_ref[...])
        mo_ref[...] = mn
        vo_ref[...] = vn

    def part_spec(k):
        return pl.BlockSpec((N_DEV, tr, C),
                            lambda l, i: (0, jnp.where(l == k, i, jnp.where(l < k, 0, nblk - 1)), 0))

    row = pl.BlockSpec((None, tr, C), lambda l, i: (l, i, 0))
    return pl.pallas_call(
        body, name=name, grid=(L, nblk), in_specs=[part_spec(k) for k in range(L)] + [row, row, row],
        out_specs=[row] * 4, out_shape=[SDS((L, R, C), F32)] * 4,
        compiler_params=_cp("arbitrary", "arbitrary"))(*parts, w, m, v)


def _pack(flat_list, row_mult):
    flat = jnp.concatenate(flat_list, axis=-1)
    n = flat.shape[-1]
    chunk = row_mult * PACK_COLS
    pad = -n % chunk
    flat = jnp.pad(flat, [(0, 0)] * (flat.ndim - 1) + [(0, pad)])
    return flat.reshape(flat.shape[:-1] + ((n + pad) // PACK_COLS, PACK_COLS))


def _unpack(packed, shapes):
    lead = packed.shape[:-2]
    flat = packed.reshape(lead + (-1,))
    out = []
    off = 0
    for shp in shapes:
        n = 1
        for d in shp:
            n *= d
        out.append(flat[..., off:off + n].reshape(lead + tuple(shp)))
        off += n
    return out


def _ffn_bwd(dx, dxb, saved, wgu, wd, tag):
    x, g, h, ab, act = saved
    dgu = _ffn_bwd_act(dxb, wd, ab, f"{tag}_bwd_act")
    dwd = _ffn_dwd(act, dxb, f"{tag}_dwd")
    dwgu = _ffn_dwgu(h, dgu, f"{tag}_dwgu")
    return _ffn_bwd_in(dgu, wgu, x, g, dx, f"{tag}_bwd_in", [dwgu, dwd.reshape(N_DEV, -1, dwd.shape[-1])])


def _mix_fwd(x, g, w_in8, w_conv8, b_f, g_conv, g_att, w_out, tag):
    T, D = x.shape
    w_int = w_in8.reshape(-1, D)
    cw = D // 2
    aw = D - cw
    H = aw // HEAD_DIM
    B = _tile(T, ATT_BLOCK, LANES)
    nb = T // B
    w_c = w_int[:3 * cw]
    w_at = w_int[3 * cw:3 * cw + 3 * aw]
    w_ft = jnp.pad(w_int[3 * cw + 3 * aw:], ((0, SUBLANES - H), (0, 0)))
    zc, h, ync = _mix_in_conv(x, g, w_c, w_conv8, g_conv, f"{tag}_in_conv")
    zat, zft8, norms = _proj_t(h, w_at, w_ft, B, f"{tag}_in_att")
    zft = zft8[:H]
    bcol = b_f.reshape(H, 1)
    c, qa, ka = _logf_cumsum(zft, bcol, B, f"{tag}_cumsum")
    cb = c.reshape(H, nb, B)
    jmin, imax = _skip_table(cb[:, :, 0:1], cb[:, :, B - 1].reshape(H, 1, nb), norms, f"{tag}_skip")
    ot, lse = _attn_fwd(jmin.reshape(H * nb), zat, qa, ka, f"{tag}_attn")
    g_col = g_att.reshape(aw, 1)
    yna = _gnorm_t_fwd(ot, g_col, f"{tag}_gnorm")
    xo = _mm_res([ync, yna], w_out, x, f"{tag}_out")
    saved = (x, g, h, zc, zft, bcol, zat, qa, ka, imax.reshape(H * nb), ot, lse, ync, yna, w_c, w_at, w_ft, g_col)
    return xo, saved


def _mix_bwd(dx, dxb, saved, w_conv8, g_conv, w_out, tag, also_scatter):
    x, g, h, zc, zft, bcol, zat, qa, ka, imax, ot, lse, ync, yna, w_c, w_at, w_ft, g_col = saved
    T, D = x.shape
    H = zft.shape[0]
    dw_out = jnp.concatenate([_mm_tn(ync, dxb, f"{tag}_dwout_c"), _mm_tn(yna, dxb, f"{tag}_dwout_a")], axis=0)
    dzc, dwc8, dg_conv, dot, delta, dg_att, *p_also = _mix_bwd_head(dxb, w_out, zc, w_conv8, g_conv, ot, g_col,
                                                                   f"{tag}_bwd_out", also_scatter)
    dq, dqa, dk, dka, dv = _attn_bwd(imax, zat, qa, ka, dot, lse, delta, f"{tag}_attn_bwd")
    dcq = dqa[:, :, 3, :].reshape(H, T)
    dck = dka[:, :, 0, :].reshape(H, T)
    dzft, db = _logf_cumsum_bwd(dcq, dck, zft, bcol, f"{tag}_cumsum_bwd")
    dzft8 = jnp.pad(dzft, ((0, SUBLANES - H), (0, 0)))
    dw_c = _mm_tn(dzc, h, f"{tag}_dwin_c")
    dw_at, dw_ft = _mix_dwt(dq, dk, dv, dzft8, h, f"{tag}_dwin_a")
    dw_int = jnp.concatenate([dw_c, dw_at.astype(BF16), dw_ft[:H].astype(BF16)], axis=0)
    scatter = [dw_int.reshape(N_DEV, -1, D), dwc8[:3].reshape(3, N_DEV, -1).transpose(1, 0, 2),
               dw_out.reshape(N_DEV, D // N_DEV, D)]
    dx, dxb, dg, p_in, p_conv, p_out = _mix_bwd_in(dzc, dq, dk, dv, dzft8, w_c, w_at, w_ft, x, g, dx, f"{tag}_bwd_in",
                                                   scatter)
    return dx, dxb, dg, p_in, p_conv, db.reshape(H), dg_conv, dg_att[:, 0], p_out, p_also


def _xattn_block_fwd(x, g, mem, g_mem, w_q, w_kv, w_o, tag):
    kv, hm = _mem_kv(mem, g_mem, w_kv, f"{tag}_kv")
    xo, h, q, o = _xattn(x, g, w_q, kv, w_o, f"{tag}_block")
    return xo, (x, g, h, mem, g_mem, hm, q, kv, o)


def _xattn_block_bwd(dx, dxb, saved, w_q, w_kv, w_o, tag):
    x, g, h, mem, g_mem, hm, q, kv, o = saved
    dw_o = _mm_tn(o, dxb, f"{tag}_dwo")
    dx, dxb, dg, dq, dkv = _xattn_bwd(dxb, dx, x, g, q, kv, w_q, w_o, f"{tag}_block_bwd")
    dw_q = _mm_tn(h, dq, f"{tag}_dwq")
    dw_kv, dg_mem = _mem_kv_bwd(dkv, hm, w_kv, mem, g_mem, f"{tag}_kv_bwd")
    D = dw_q.shape[0]
    return dx, dxb, dg, dg_mem, [dw_q.reshape(N_DEV, D // N_DEV, D), dw_kv, dw_o.reshape(N_DEV, D // N_DEV, D)]


GATHER_FIRST = ['w_ffn1_gu']
GATHER_REST = ['w_ffn1_down', 'w_mix_in', 'w_mix_out', 'w_xq', 'w_xkv', 'w_xo', 'w_ffn2_gu', 'w_ffn2_down']
GATHER_NEXT_A = ['w_ffn1_gu', 'w_ffn1_down', 'w_mix_in', 'w_mix_out', 'w_xq', 'w_xkv', 'w_xo']
GATHER_NEXT_B = ['w_ffn2_gu', 'w_ffn2_down']


def _local_step(x, mem, tgt, P):
    L = P['g_ffn1'].shape[0]
    shards = lambda names, l: [(P[n][l].T if n == 'w_mix_in' else P[n][l]).astype(BF16) for n in names]
    W = {n: [None] * L for n in SHARDED}

    def keep(names, l, arrays):
        for n, a in zip(names, arrays):
            W[n][l] = a

    *first, conv = _all_gather(shards(GATHER_FIRST, 0) + [P['w_conv']], "gather_first")
    keep(GATHER_FIRST, 0, first)
    w_conv = conv.transpose(1, 2, 0, 3).reshape(L, 3, -1)
    saved = []
    for l in range(L):
        row = lambda n: P[n][l][None, :]
        wc8 = jnp.pad(w_conv[l], ((0, SUBLANES - 3), (0, 0)))
        x1 = x
        bring = GATHER_REST if l == 0 else []
        ab, act, h, *got = _ffn_gu(x, row('g_ffn1'), W['w_ffn1_gu'][l], f"l{l}_ffn1_gu", gather=shards(bring, l))
        keep(bring, l, got)
        x, = _ffn_down(act, W['w_ffn1_down'][l], x, f"l{l}_ffn1_down")
        s1 = (x1, row('g_ffn1'), h, ab, act)
        x, s2 = _mix_fwd(x, row('g_mix'), W['w_mix_in'][l], wc8, P['b_f'][l], row('g_conv_out'), P['g_att_out'][l],
                         W['w_mix_out'][l], f"l{l}_mix")
        x, s3 = _xattn_block_fwd(x, row('g_xattn'), mem, row('g_mem'), W['w_xq'][l], W['w_xkv'][l], W['w_xo'][l],
                                 f"l{l}_xattn")
        x4 = x
        bring_a, bring_b = (GATHER_NEXT_A, GATHER_NEXT_B) if l + 1 < L else ([], [])
        ab, act, h, *got = _ffn_gu(x, row('g_ffn2'), W['w_ffn2_gu'][l], f"l{l}_ffn2_gu", gather=shards(bring_a, l + 1))
        keep(bring_a, l + 1, got)
        x, *got = _ffn_down(act, W['w_ffn2_down'][l], x, f"l{l}_ffn2_down", gather=shards(bring_b, l + 1))
        keep(bring_b, l + 1, got)
        saved.append((s1, s2, s3, (x4, row('g_ffn2'), h, ab, act), wc8))
    loss, dx, dxb, dg_final = _loss_head(x, P['g_final'][None, :], tgt, "loss_head")
    G = {n: [None] * L for n in WEIGHT_NAMES if n != 'g_final'}
    for l in reversed(range(L)):
        row = lambda n: P[n][l][None, :]
        s1, s2, s3, s4, wc8 = saved[l]
        dx, dxb, G['g_ffn2'][l], G['w_ffn2_gu'][l], G['w_ffn2_down'][l] = _ffn_bwd(
            dx, dxb, s4, W['w_ffn2_gu'][l], W['w_ffn2_down'][l], f"l{l}_ffn2")
        dx, dxb, G['g_xattn'][l], G['g_mem'][l], xattn_grads = _xattn_block_bwd(
            dx, dxb, s3, W['w_xq'][l], W['w_xkv'][l], W['w_xo'][l], f"l{l}_xattn")
        (dx, dxb, G['g_mix'][l], G['w_mix_in'][l], G['w_conv'][l], G['b_f'][l], G['g_conv_out'][l], G['g_att_out'][l],
         G['w_mix_out'][l], (G['w_xq'][l], G['w_xkv'][l], G['w_xo'][l])) = _mix_bwd(
            dx, dxb, s2, wc8, row('g_conv_out'), W['w_mix_out'][l], f"l{l}_mix", xattn_grads)
        dx, dxb, G['g_ffn1'][l], G['w_ffn1_gu'][l], G['w_ffn1_down'][l] = _ffn_bwd(
            dx, dxb, s1, W['w_ffn1_gu'][l], W['w_ffn1_down'][l], f"l{l}_ffn1")
    G['g_final'] = [dg_final]
    return loss, dx, G


def kernel(x, mem, g_ffn1, w_ffn1_gu, w_ffn1_down, g_mix, w_mix_in, w_conv, b_f, g_conv_out, g_att_out, w_mix_out, g_xattn, g_mem, w_xq, w_xkv, w_xo, g_ffn2, w_ffn2_gu, w_ffn2_down, g_final, loss_target, m_g_ffn1, m_w_ffn1_gu, m_w_ffn1_down, m_g_mix, m_w_mix_in, m_w_conv, m_b_f, m_g_conv_out, m_g_att_out, m_w_mix_out, m_g_xattn, m_g_mem, m_w_xq, m_w_xkv, m_w_xo, m_g_ffn2, m_w_ffn2_gu, m_w_ffn2_down, m_g_final, v_g_ffn1, v_w_ffn1_gu, v_w_ffn1_down, v_g_mix, v_w_mix_in, v_w_conv, v_b_f, v_g_conv_out, v_g_att_out, v_w_mix_out, v_g_xattn, v_g_mem, v_w_xq, v_w_xkv, v_w_xo, v_g_ffn2, v_w_ffn2_gu, v_w_ffn2_down, v_g_final):
    args = (x, mem, g_ffn1, w_ffn1_gu, w_ffn1_down, g_mix, w_mix_in, w_conv, b_f, g_conv_out, g_att_out, w_mix_out,
            g_xattn, g_mem, w_xq, w_xkv, w_xo, g_ffn2, w_ffn2_gu, w_ffn2_down, g_final)
    P = dict(zip(IN_NAMES, args))
    moms = (m_g_ffn1, m_w_ffn1_gu, m_w_ffn1_down, m_g_mix, m_w_mix_in, m_w_conv, m_b_f, m_g_conv_out, m_g_att_out,
            m_w_mix_out, m_g_xattn, m_g_mem, m_w_xq, m_w_xkv, m_w_xo, m_g_ffn2, m_w_ffn2_gu, m_w_ffn2_down, m_g_final)
    vars_ = (v_g_ffn1, v_w_ffn1_gu, v_w_ffn1_down, v_g_mix, v_w_mix_in, v_w_conv, v_b_f, v_g_conv_out, v_g_att_out,
             v_w_mix_out, v_g_xattn, v_g_mem, v_w_xq, v_w_xkv, v_w_xo, v_g_ffn2, v_w_ffn2_gu, v_w_ffn2_down, v_g_final)
    MOM = dict(zip(WEIGHT_NAMES, moms))
    VAR = dict(zip(WEIGHT_NAMES, vars_))

    loss_part, dx, grads = _local_step(x[0], mem[0], loss_target[0], {n: P[n] for n in WEIGHT_NAMES})

    small_shapes = [P[n].shape for n in REPLICATED] + [(1,)]
    small = _pack([a.reshape(-1) for n in REPLICATED for a in grads[n]] + [loss_part[0, :1]], SUBLANES)
    small_parts = _all_gather([small], "gather_small_grads")[0]

    res = {}
    for n in SHARDED:
        t = (lambda a: a.transpose(0, 2, 1)) if n in LONG_AXIS_LAST else (lambda a: a)
        res[n] = [t(o) for o in _sum_adamw(grads[n], t(P[n]), t(MOM[n]), t(VAR[n]), "adamw_" + n)]
    zero = [jnp.zeros((1,), F32)]
    w_, m_, v_ = (_pack([d[n].reshape(-1) for n in REPLICATED] + zero, SUBLANES)[None] for d in (P, MOM, VAR))
    small_out = [_unpack(o[0], small_shapes) for o in _sum_adamw([small_parts], w_, m_, v_, "adamw_vectors")]
    for k, n in enumerate(REPLICATED):
        res[n] = [o[k] for o in small_out]
    out = []
    for k in range(4):
        out += [res[n][k] for n in WEIGHT_NAMES]
    return (small_out[0][-1].reshape(()), dx[None], *out)
```

```python
import jax
import jax.numpy as jnp
from jax import lax
from jax.experimental import pallas as pl
from jax.experimental.pallas import tpu as pltpu

F32 = jnp.float32
BF16 = jnp.bfloat16
I32 = jnp.int32
SDS = jax.ShapeDtypeStruct

EPS = 1e-6
HEAD_DIM = 64
N_XHEADS = 4
N_DEV = 8
LANES = 128
SUBLANES = 8
MXU_COLS = 256
AUG = 16
ATT_BLOCK = 512
NEG = -1e30
SKIP_MARGIN = 115.0
PACK_COLS = 1024
VMEM_LIMIT = 58 * 1024 * 1024

ADAM_LR = 0.001
ADAM_B1 = 0.9
ADAM_B2 = 0.999
ADAM_EPS = 1e-08
ADAM_WD = 0.01
ADAM_STEP = 10

IN_NAMES = ['x', 'mem', 'g_ffn1', 'w_ffn1_gu', 'w_ffn1_down', 'g_mix', 'w_mix_in', 'w_conv', 'b_f', 'g_conv_out',
            'g_att_out', 'w_mix_out', 'g_xattn', 'g_mem', 'w_xq', 'w_xkv', 'w_xo', 'g_ffn2', 'w_ffn2_gu',
            'w_ffn2_down', 'g_final']
WEIGHT_NAMES = IN_NAMES[2:]
SHARDED = ['w_ffn1_gu', 'w_ffn1_down', 'w_mix_in', 'w_conv', 'w_mix_out', 'w_xq', 'w_xkv', 'w_xo', 'w_ffn2_gu',
           'w_ffn2_down']
REPLICATED = [n for n in WEIGHT_NAMES if n not in SHARDED]
LONG_AXIS_LAST = ['w_ffn1_gu', 'w_ffn2_gu', 'w_mix_in']


def _tile(n, pref, mult):
    t = min(pref, n) // mult * mult
    while t >= mult:
        if n % t == 0:
            return t
        t -= mult
    return n


def _cp(*sem):
    return pltpu.CompilerParams(dimension_semantics=sem, vmem_limit_bytes=VMEM_LIMIT)


def _nt(a, b):
    return lax.dot_general(a, b, (((1,), (1,)), ((), ())), preferred_element_type=F32)


def _tn(a, b):
    return lax.dot_general(a, b, (((0,), (0,)), ((), ())), preferred_element_type=F32)


def _nn(a, b):
    return jnp.dot(a, b, preferred_element_type=F32)


def _rstd(xv):
    return lax.rsqrt(jnp.mean(xv * xv, axis=-1, keepdims=True) + EPS)


def _full(a):
    nd = a.ndim
    return pl.BlockSpec(a.shape, lambda *_: (0,) * nd, pipeline_mode=pl.Buffered(1))


def _rows2d(w_ref):
    s, a, b = w_ref.shape
    return w_ref[...].reshape(s * a, b)


def _accumulate(ref, first, part):
    @pl.when(first)
    def _():
        ref[...] = part

    @pl.when(jnp.logical_not(first))
    def _():
        ref[...] += part


def _norm_bwd_store(dh, x_ref, g_ref, r_ref, dx_ref, dxb_ref, dg_ref, first):
    xv = x_ref[...]
    r = _rstd(xv)
    xh = xv * r
    dxh = dh * g_ref[...]
    dx = r * (dxh - xh * jnp.mean(dxh * xh, axis=-1, keepdims=True)) + r_ref[...]
    dx_ref[...] = dx
    dxb_ref[...] = dx.astype(BF16)
    _accumulate(dg_ref, first, jnp.sum(dh * xh, axis=0, keepdims=True))


def _mm_res(a_list, w, res, name, tm=512):
    T = res.shape[0]
    N = w.shape[-1]
    tm = _tile(T, tm, SUBLANES)
    n_a = len(a_list)

    def body(*refs):
        a_refs = refs[:n_a]
        w_ref, r_ref, o_ref = refs[n_a:]
        wv = _rows2d(w_ref)
        acc = r_ref[...]
        off = 0
        for a_ref in a_refs:
            k = a_ref.shape[1]
            acc = acc + _nn(a_ref[...], wv[off:off + k, :])
            off += k
        o_ref[...] = acc

    row = lambda n: pl.BlockSpec((tm, n), lambda i: (i, 0))
    return pl.pallas_call(
        body, name=name, grid=(T // tm,),
        in_specs=[row(a.shape[1]) for a in a_list] + [_full(w), row(N)], out_specs=row(N),
        out_shape=SDS((T, N), F32), compiler_params=_cp("parallel"))(*a_list, w, res)


def _mm_tn(a, b, name, out_dtype=BF16, tt=2048):
    T, K = a.shape
    N = b.shape[1]
    tt = _tile(T, tt, 2 * SUBLANES)
    nt = T // tt

    def body(a_ref, b_ref, o_ref, acc_ref):
        t = pl.program_id(0)
        _accumulate(acc_ref, t == 0, _tn(a_ref[...], b_ref[...]))

        @pl.when(t == nt - 1)
        def _():
            o_ref[...] = acc_ref[...].astype(o_ref.dtype)

    return pl.pallas_call(
        body, name=name, grid=(nt,),
        in_specs=[pl.BlockSpec((tt, K), lambda t: (t, 0)), pl.BlockSpec((tt, N), lambda t: (t, 0))],
        out_specs=pl.BlockSpec((K, N), lambda t: (0, 0)), out_shape=SDS((K, N), out_dtype),
        scratch_shapes=[pltpu.VMEM((K, N), F32)], compiler_params=_cp("arbitrary"))(a, b)


def _hosted_call(body, name, n_steps, in_specs, out_specs, out_shape, args, scatter=(), gather=(), scratch_shapes=()):
    scatter, gather, scratch_shapes = list(scatter), list(gather), list(scratch_shapes)
    n_s, n_g = len(scatter), len(gather)
    if n_s + n_g == 0:
        return pl.pallas_call(body, name=name, grid=(n_steps,), in_specs=list(in_specs), out_specs=list(out_specs),
                              out_shape=list(out_shape), scratch_shapes=scratch_shapes,
                              compiler_params=_cp("arbitrary"))(*args)
    n_in, n_out, n_scr = len(in_specs), len(out_specs), len(scratch_shapes)
    relay_step = min(max(1, 3 * n_steps // 4), n_steps - 1)

    def wrapped(*refs):
        ins, refs = refs[:n_in], refs[n_in:]
        xs, shards, refs = refs[:n_s], refs[n_s:n_s + n_g], refs[n_s + n_g:]
        outs, refs = refs[:n_out], refs[n_out:]
        recvs, gathered, refs = refs[:n_s], refs[n_s:n_s + n_g], refs[n_s + n_g:]
        outs, sems = outs + refs[:n_scr], refs[n_scr:]
        s_sems, g_sems = (sems[:3], sems[3:]) if n_s else ((), sems)
        step = pl.program_id(0)

        @pl.when(step == 0)
        def _():
            if n_s:
                _scatter_start(xs, recvs, *s_sems)
            if n_g:
                _gather_start(shards, gathered, *g_sems)
        body(*ins, *outs)
        if n_g:
            @pl.when(step == relay_step)
            def _():
                _gather_relay(shards, gathered, *g_sems)

        @pl.when(step == n_steps - 1)
        def _():
            if n_s:
                _scatter_wait(xs, recvs, *s_sems)
            if n_g:
                _gather_finish(shards, gathered, *g_sems)

    any_spec = pl.BlockSpec(memory_space=pl.ANY)
    sems = []
    for n in (n_s, n_g):
        if n:
            sems += [pltpu.SemaphoreType.DMA((n, N_DEV - 1)), pltpu.SemaphoreType.DMA((n, N_DEV - 1)),
                     pltpu.SemaphoreType.DMA((n,))]
    return pl.pallas_call(
        wrapped, name=name, grid=(n_steps,), in_specs=list(in_specs) + [any_spec] * (n_s + n_g),
        out_specs=list(out_specs) + [any_spec] * (n_s + n_g),
        out_shape=list(out_shape) + [SDS(b.shape, b.dtype) for b in scatter]
        + [SDS((N_DEV,) + b.shape, b.dtype) for b in gather],
        scratch_shapes=scratch_shapes + sems, compiler_params=_cp("arbitrary"))(*args, *scatter, *gather)


def _hidden_chunks(F):
    edges = list(range(0, F, 3 * MXU_COLS)) + [F]
    return list(zip(edges[:-1], edges[1:]))


def _ffn_gu(x, g, wgut, name, gather=(), tm=512):
    T, D = x.shape
    F = wgut.shape[0] // 2
    tm = _tile(T, tm, SUBLANES)

    def body(x_ref, g_ref, w_ref, ab_ref, act_ref, h_ref):
        xv = x_ref[...]
        h = (xv * _rstd(xv) * g_ref[...]).astype(BF16)
        h_ref[...] = h
        for c0, c1 in _hidden_chunks(F):
            gt = _nt(h, w_ref[c0:c1, :])
            up = _nt(h, w_ref[F + c0:F + c1, :])
            sg = jax.nn.sigmoid(gt)
            silu = gt * sg
            ab_ref[0, :, c0:c1] = (0.5 * up * (sg * (1.0 + gt * (1.0 - sg)))).astype(BF16)
            ab_ref[1, :, c0:c1] = (0.5 * silu).astype(BF16)
            act_ref[:, c0:c1] = (silu * up).astype(BF16)

    row = lambda n: pl.BlockSpec((tm, n), lambda i: (i, 0))
    return _hosted_call(
        body, name, T // tm, [row(D), _full(g), _full(wgut)],
        [pl.BlockSpec((2, tm, F), lambda i: (0, i, 0)), row(F), row(D)],
        [SDS((2, T, F), BF16), SDS((T, F), BF16), SDS((T, D), BF16)], (x, g, wgut), gather=gather)


def _ffn_down(act, wd, res, name, gather=(), tm=1024):
    T, F = act.shape
    D = wd.shape[2]
    tm = _tile(T, tm, SUBLANES)

    def body(a_ref, w_ref, r_ref, o_ref):
        o_ref[...] = r_ref[...] + 0.5 * _nn(a_ref[...], _rows2d(w_ref))

    row = lambda n: pl.BlockSpec((tm, n), lambda i: (i, 0))
    return _hosted_call(body, name, T // tm, [row(F), _full(wd), row(D)], [row(D)], [SDS((T, D), F32)],
                        (act, wd, res), gather=gather)


def _ffn_bwd_act(dyb, wd, ab, name, tm=512):
    T, D = dyb.shape
    _, _, F = ab.shape
    tm = _tile(T, tm, SUBLANES)

    def body(dy_ref, w_ref, ab_ref, o_ref):
        wv = _rows2d(w_ref)
        dy = dy_ref[...]
        for c0, c1 in _hidden_chunks(F):
            da = _nt(dy, wv[c0:c1, :])
            o_ref[0, :, c0:c1] = (da * ab_ref[0, :, c0:c1].astype(F32)).astype(BF16)
            o_ref[1, :, c0:c1] = (da * ab_ref[1, :, c0:c1].astype(F32)).astype(BF16)

    blocks = pl.BlockSpec((2, tm, F), lambda i: (0, i, 0))
    return pl.pallas_call(
        body, name=name, grid=(T // tm,),
        in_specs=[pl.BlockSpec((tm, D), lambda i: (i, 0)), _full(wd), blocks], out_specs=blocks,
        out_shape=SDS((2, T, F), BF16), compiler_params=_cp("parallel"))(dyb, wd, ab)


def _ffn_dwd(act, dyb, name, tt=2048):
    T, F = act.shape
    D = dyb.shape[1]
    tt = _tile(T, tt, 2 * SUBLANES)
    nt = T // tt
    fh = F // 2

    def body(a_ref, b_ref, o_ref, acc_ref):
        t = pl.program_id(1)
        _accumulate(acc_ref, t == 0, _tn(a_ref[...], b_ref[...]))

        @pl.when(t == nt - 1)
        def _():
            o_ref[...] = (0.5 * acc_ref[...]).astype(BF16)

    return pl.pallas_call(
        body, name=name, grid=(2, nt),
        in_specs=[pl.BlockSpec((tt, fh), lambda j, t: (t, j)), pl.BlockSpec((tt, D), lambda j, t: (t, 0))],
        out_specs=pl.BlockSpec((fh, D), lambda j, t: (j, 0)), out_shape=SDS((F, D), BF16),
        scratch_shapes=[pltpu.VMEM((fh, D), F32)], compiler_params=_cp("parallel", "arbitrary"))(act, dyb)


def _ffn_dwgu(h, dgu, name, tt=2048):
    T, D = h.shape
    _, _, F = dgu.shape
    tt = _tile(T, tt, 2 * SUBLANES)
    nt = T // tt
    fh = F // 2

    def body(a_ref, b_ref, o_ref, acc_ref):
        t = pl.program_id(1)
        _accumulate(acc_ref, t == 0, _tn(b_ref[...], a_ref[...]))

        @pl.when(t == nt - 1)
        def _():
            o_ref[...] = acc_ref[...].astype(BF16)

    return pl.pallas_call(
        body, name=name, grid=(4, nt),
        in_specs=[pl.BlockSpec((tt, D), lambda q, t: (t, 0)),
                  pl.BlockSpec((None, tt, fh), lambda q, t: (q // 2, t, q % 2))],
        out_specs=pl.BlockSpec((fh, D), lambda q, t: (q, 0)), out_shape=SDS((2 * F, D), BF16),
        scratch_shapes=[pltpu.VMEM((fh, D), F32)], compiler_params=_cp("parallel", "arbitrary"))(h, dgu)


def _ffn_bwd_in(dgu, wgut, x, g, dres, name, scatter, tm=512):
    T, D = x.shape
    _, _, F = dgu.shape
    tm = _tile(T, tm, SUBLANES)

    def body(a_ref, w_ref, x_ref, g_ref, r_ref, dx_ref, dxb_ref, dg_ref):
        dh = _nn(a_ref[0], w_ref[:F, :]) + _nn(a_ref[1], w_ref[F:, :])
        _norm_bwd_store(dh, x_ref, g_ref, r_ref, dx_ref, dxb_ref, dg_ref, pl.program_id(0) == 0)

    row = pl.BlockSpec((tm, D), lambda i: (i, 0))
    return _hosted_call(
        body, name, T // tm, [pl.BlockSpec((2, tm, F), lambda i: (0, i, 0)), _full(wgut), row, _full(g), row],
        [row, row, pl.BlockSpec((1, D), lambda i: (0, 0))],
        [SDS((T, D), F32), SDS((T, D), BF16), SDS((1, D), F32)], (dgu, wgut, x, g, dres), scatter)


def _loss_head(x, g, tgt, name, tm=512):
    T, D = x.shape
    tm = _tile(T, tm, SUBLANES)

    def body(x_ref, g_ref, t_ref, loss_ref, dx_ref, dxb_ref, dg_ref):
        first = pl.program_id(0) == 0
        xv = x_ref[...]
        r = _rstd(xv)
        xh = xv * r
        e = xh * g_ref[...] - t_ref[...]
        part = 0.5 * jnp.sum(jnp.mean(e * e, axis=-1, keepdims=True), axis=0, keepdims=True)
        dy = e * (1.0 / D)
        dxh = dy * g_ref[...]
        dx = r * (dxh - xh * jnp.mean(dxh * xh, axis=-1, keepdims=True))
        dx_ref[...] = dx
        dxb_ref[...] = dx.astype(BF16)
        _accumulate(loss_ref, first, jnp.broadcast_to(part, loss_ref.shape))
        _accumulate(dg_ref, first, jnp.sum(dy * xh, axis=0, keepdims=True))

    row = pl.BlockSpec((tm, D), lambda i: (i, 0))
    vec = pl.BlockSpec((1, D), lambda i: (0, 0))
    return pl.pallas_call(
        body, name=name, grid=(T // tm,), in_specs=[row, vec, row],
        out_specs=[pl.BlockSpec((SUBLANES, LANES), lambda i: (0, 0)), row, row, vec],
        out_shape=[SDS((SUBLANES, LANES), F32), SDS((T, D), F32), SDS((T, D), BF16), SDS((1, D), F32)],
        compiler_params=_cp("arbitrary"))(x, g, tgt)


def _shift_down(u, k, prev_row):
    rows = lax.broadcasted_iota(I32, u.shape, 0)
    s = pltpu.roll(u, k, 0)
    for t in range(k):
        s = jnp.where(rows == t, prev_row(SUBLANES - k + t), s)
    return s


def _shift_up(u, k, next_row):
    n = u.shape[0]
    rows = lax.broadcasted_iota(I32, u.shape, 0)
    s = pltpu.roll(u, n - k, 0)
    for t in range(k):
        s = jnp.where(rows == n - k + t, next_row(t), s)
    return s


def _conv_taps(z_ref, zp_ref, w_ref, cw, first):
    u = z_ref[:, cw:2 * cw] * z_ref[:, 2 * cw:]

    def prev_row(r):
        return jnp.where(first, 0.0, zp_ref[r:r + 1, cw:2 * cw] * zp_ref[r:r + 1, 2 * cw:])

    u1 = _shift_down(u, 1, prev_row)
    u2 = _shift_down(u, 2, prev_row)
    cv = w_ref[0:1, :] * u2 + w_ref[1:2, :] * u1 + w_ref[2:3, :] * u
    return u, u1, u2, cv


def _mix_in_conv(x, g, w, wc, gc, name, tm=512):
    T, D = x.shape
    cw = w.shape[0] // 3
    tm = _tile(T, tm, SUBLANES)

    def body(x_ref, g_ref, w_ref, wc_ref, gc_ref, z_ref, h_ref, o_ref, prev_ref):
        xv = x_ref[...]
        h = (xv * _rstd(xv) * g_ref[...]).astype(BF16)
        h_ref[...] = h
        z_ref[...] = _nt(h, w_ref[...])
        _, _, _, cv = _conv_taps(z_ref, prev_ref, wc_ref, cw, pl.program_id(0) == 0)
        y = z_ref[:, :cw] * cv
        o_ref[...] = (y * _rstd(y) * gc_ref[...]).astype(BF16)
        prev_ref[...] = z_ref[tm - SUBLANES:, :]

    row = lambda n: pl.BlockSpec((tm, n), lambda i: (i, 0))
    return pl.pallas_call(
        body, name=name, grid=(T // tm,), in_specs=[row(D), _full(g), _full(w), _full(wc), _full(gc)],
        out_specs=[row(3 * cw), row(D), row(cw)],
        out_shape=[SDS((T, 3 * cw), F32), SDS((T, D), BF16), SDS((T, cw), BF16)],
        scratch_shapes=[pltpu.VMEM((SUBLANES, 3 * cw), F32)], compiler_params=_cp("arbitrary"))(x, g, w, wc, gc)


def _mix_bwd_head(dyb, w_out, zc, wc, gc, ot, g_col, name, scatter):
    nb, aw, tm = ot.shape
    T, D = dyb.shape
    cw = zc.shape[1] // 3
    H = aw // HEAD_DIM
    hb = tm // SUBLANES

    def body(dyb_ref, wo_ref, z_ref, zp_ref, w_ref, g_ref, o_ref, ga_ref,
             dz_ref, dw_ref, dg_ref, do_ref, dl_ref, dga_ref, carry_ref):
        s = pl.program_id(0)
        dy = _nt(dyb_ref[...], _rows2d(wo_ref))
        ov = o_ref[...]
        ra = lax.rsqrt(jnp.mean(ov * ov, axis=0, keepdims=True) + EPS)
        oh = ov * ra
        dya = dy[:, cw:].T
        dyah = dya * ga_ref[...]
        dov = ra * (dyah - oh * jnp.mean(dyah * oh, axis=0, keepdims=True))
        do_ref[...] = dov.astype(BF16)
        dl_ref[...] = jnp.sum((dov * ov).reshape(H, HEAD_DIM, tm), axis=1)
        _accumulate(dga_ref, s == 0, jnp.broadcast_to(jnp.sum(dya * oh, axis=1, keepdims=True), dga_ref.shape))
        u, u1, u2, cv = _conv_taps(z_ref, zp_ref, w_ref, cw, s == nb - 1)
        zb = z_ref[:, :cw]
        y = zb * cv
        r = _rstd(y)
        yh = y * r
        dyn = dy[:, :cw]
        dyh = dyn * g_ref[...]
        dyc = r * (dyh - yh * jnp.mean(dyh * yh, axis=-1, keepdims=True))
        dcv = dyc * zb

        def next_row(t):
            return jnp.where(s == 0, 0.0, carry_ref[t:t + 1, :])

        du = w_ref[2:3, :] * dcv + w_ref[1:2, :] * _shift_up(dcv, 1, next_row) + w_ref[0:1, :] * _shift_up(dcv, 2, next_row)
        carry_ref[...] = dcv[0:SUBLANES, :]
        dz_ref[:, :cw] = (dyc * cv).astype(BF16)
        dz_ref[:, cw:2 * cw] = (du * z_ref[:, 2 * cw:]).astype(BF16)
        dz_ref[:, 2 * cw:] = (du * z_ref[:, cw:2 * cw]).astype(BF16)
        tap = lax.broadcasted_iota(I32, (SUBLANES, cw), 0)
        dwp = jnp.where(tap == 0, jnp.sum(dcv * u2, axis=0, keepdims=True),
                        jnp.where(tap == 1, jnp.sum(dcv * u1, axis=0, keepdims=True),
                                  jnp.where(tap == 2, jnp.sum(dcv * u, axis=0, keepdims=True), 0.0)))
        _accumulate(dw_ref, s == 0, dwp)
        _accumulate(dg_ref, s == 0, jnp.sum(dyn * yh, axis=0, keepdims=True))

    rev = lambda s: nb - 1 - s
    rows = lambda n: pl.BlockSpec((tm, n), lambda s: (rev(s), 0))
    blk = pl.BlockSpec((None, aw, tm), lambda s: (rev(s), 0, 0))
    return _hosted_call(
        body, name, nb,
        [rows(D), _full(w_out), rows(3 * cw),
         pl.BlockSpec((SUBLANES, 3 * cw), lambda s: (jnp.maximum(rev(s) * hb - 1, 0), 0)),
         _full(wc), _full(gc), blk, _full(g_col)],
        [rows(3 * cw), pl.BlockSpec((SUBLANES, cw), lambda s: (0, 0)), pl.BlockSpec((1, cw), lambda s: (0, 0)),
         blk, pl.BlockSpec((None, H, tm), lambda s: (rev(s), 0, 0)), pl.BlockSpec((aw, LANES), lambda s: (0, 0))],
        [SDS((T, 3 * cw), BF16), SDS((SUBLANES, cw), F32), SDS((1, cw), F32),
         SDS((nb, aw, tm), BF16), SDS((nb, H, tm), F32), SDS((aw, LANES), F32)],
        (dyb, w_out, zc, zc, wc, gc, ot, g_col), scatter=scatter, scratch_shapes=[pltpu.VMEM((SUBLANES, cw), F32)])


def _proj_t(h, wat, wft, B, name):
    T, D = h.shape
    R = wat.shape[0]
    nb = T // B
    qk = 2 * R // 3
    g = qk // HEAD_DIM

    def body(h_ref, wa_ref, wf_ref, za_ref, zf_ref, n_ref):
        hv = h_ref[...]
        zab = _nt(wa_ref[...], hv).astype(BF16)
        za_ref[...] = zab
        zf_ref[...] = _nt(wf_ref[...], hv)
        zv = zab[:qk].astype(F32)
        ss = jnp.sum((zv * zv).reshape(g, HEAD_DIM, B), axis=1)
        mx = jnp.broadcast_to(jnp.max(ss, axis=1, keepdims=True), n_ref.shape)

        @pl.when(pl.program_id(0) == 0)
        def _():
            n_ref[...] = mx

        @pl.when(pl.program_id(0) > 0)
        def _():
            n_ref[...] = jnp.maximum(n_ref[...], mx)

    return pl.pallas_call(
        body, name=name, grid=(nb,), in_specs=[pl.BlockSpec((B, D), lambda i: (i, 0)), _full(wat), _full(wft)],
        out_specs=[pl.BlockSpec((None, R, B), lambda i: (i, 0, 0)), pl.BlockSpec((SUBLANES, B), lambda i: (0, i)),
                   pl.BlockSpec((g, LANES), lambda i: (0, 0))],
        out_shape=[SDS((nb, R, B), BF16), SDS((SUBLANES, T), F32), SDS((g, LANES), F32)],
        compiler_params=_cp("arbitrary"))(h, wat, wft)


def _split3(v):
    hi = v.astype(BF16).astype(F32)
    r1 = v - hi
    mid = r1.astype(BF16).astype(F32)
    lo = (r1 - mid).astype(BF16).astype(F32)
    return hi, mid, lo


def _tri_dot3(v, tri):
    hi, mid, lo = _split3(v)
    return _nn(hi.astype(BF16), tri) + _nn(mid.astype(BF16), tri) + _nn(lo.astype(BF16), tri)


def _logf_cumsum(zft, b, tb, name):
    H, T = zft.shape

    def body(z_ref, b_ref, c_ref, qa_ref, ka_ref, carry_ref, hi_ref, mid_ref, lo_ref):
        @pl.when(pl.program_id(0) == 0)
        def _():
            carry_ref[...] = jnp.zeros_like(carry_ref)
        xv = z_ref[...] + b_ref[...]
        lf = jnp.minimum(xv, 0.0) - jnp.log(1.0 + jnp.exp(-jnp.abs(xv)))
        src = lax.broadcasted_iota(I32, (tb, tb), 0)
        dst = lax.broadcasted_iota(I32, (tb, tb), 1)
        tri = jnp.where(src <= dst, 1.0, 0.0).astype(BF16)
        cs = _tri_dot3(lf, tri) + carry_ref[...]
        c_ref[...] = cs
        hi_ref[...], mid_ref[...], lo_ref[...] = _split3(cs)
        col = lax.broadcasted_iota(I32, cs.shape, 1)
        carry_ref[...] = jnp.sum(jnp.where(col == tb - 1, cs, 0.0), axis=1, keepdims=True)
        row = lax.broadcasted_iota(I32, (AUG, tb), 0)
        for h in range(H):
            terms = [r[h:h + 1, :] for r in (hi_ref, mid_ref, lo_ref)]
            qa, ka = jnp.where(row < 3, 1.0, 0.0), jnp.where((row >= 3) & (row < 6), 1.0, 0.0)
            for k, t in enumerate(terms):
                qa = jnp.where(row == 3 + k, t, qa)
                ka = jnp.where(row == k, -t, ka)
            qa_ref[h] = qa
            ka_ref[h] = ka

    blk = pl.BlockSpec((H, tb), lambda i: (0, i))
    aug = pl.BlockSpec((H, None, AUG, tb), lambda i: (0, i, 0, 0))
    return pl.pallas_call(
        body, name=name, grid=(T // tb,), in_specs=[blk, _full(b)], out_specs=[blk, aug, aug],
        out_shape=[SDS((H, T), F32), SDS((H, T // tb, AUG, tb), F32), SDS((H, T // tb, AUG, tb), F32)],
        scratch_shapes=[pltpu.VMEM((H, 1), F32)] + [pltpu.VMEM((H, tb), F32)] * 3,
        compiler_params=_cp("arbitrary"))(zft, b)


def _logf_cumsum_bwd(dcq, dck, zft, b, name, tb=512):
    H, T = zft.shape
    tb = _tile(T, tb, LANES)
    nb = T // tb

    def body(dq_ref, dk_ref, z_ref, b_ref, o_ref, db_ref, carry_ref):
        s = pl.program_id(0)

        @pl.when(s == 0)
        def _():
            carry_ref[...] = jnp.zeros_like(carry_ref)
        dc = dq_ref[...] - dk_ref[...]
        src = lax.broadcasted_iota(I32, (tb, tb), 0)
        dst = lax.broadcasted_iota(I32, (tb, tb), 1)
        tri = jnp.where(src >= dst, 1.0, 0.0).astype(BF16)
        dl = _tri_dot3(dc, tri) + carry_ref[...]
        col = lax.broadcasted_iota(I32, dl.shape, 1)
        carry_ref[...] = jnp.sum(jnp.where(col == 0, dl, 0.0), axis=1, keepdims=True)
        dz = dl * jax.nn.sigmoid(-(z_ref[...] + b_ref[...]))
        o_ref[...] = dz
        _accumulate(db_ref, s == 0, jnp.sum(dz, axis=1, keepdims=True))

    blk = pl.BlockSpec((H, tb), lambda s: (0, nb - 1 - s))
    vec = pl.BlockSpec((H, 1), lambda s: (0, 0))
    return pl.pallas_call(
        body, name=name, grid=(nb,), in_specs=[blk, blk, blk, vec], out_specs=[blk, vec],
        out_shape=[SDS((H, T), F32), SDS((H, 1), F32)],
        scratch_shapes=[pltpu.VMEM((H, 1), F32)], compiler_params=_cp("arbitrary"))(dcq, dck, zft, b)


def _skip_table(cs_col, ce_row, norms, name):
    H, nb, _ = cs_col.shape

    def body(cs_ref, ce_ref, n_ref, jm_ref, im_ref):
        h = pl.program_id(0)
        nq = n_ref[pl.ds(h, 1), 0:1]
        nk = n_ref[pl.ds(H + h, 1), 0:1]
        bound = 2.0 * jnp.sqrt(nq * nk) * 0.125
        skip = jnp.where(bound + cs_ref[...] - ce_ref[...] <= -SKIP_MARGIN, 1, 0).astype(I32)
        jm_ref[...] = jnp.sum(skip, axis=1, keepdims=True)
        im_ref[...] = nb - 1 - jnp.sum(skip, axis=0, keepdims=True)

    col = pl.BlockSpec((None, nb, 1), lambda h: (h, 0, 0))
    row = pl.BlockSpec((None, 1, nb), lambda h: (h, 0, 0))
    return pl.pallas_call(
        body, name=name, grid=(H,), in_specs=[col, row, _full(norms)], out_specs=[col, row],
        out_shape=[SDS((H, nb, 1), I32), SDS((H, 1, nb), I32)], compiler_params=_cp("parallel"))(cs_col, ce_row, norms)


def _causal_mask(B):
    krow = lax.broadcasted_iota(I32, (B, B), 0)
    qcol = lax.broadcasted_iota(I32, (B, B), 1)
    return krow <= qcol


def _attn_fwd(jmin, zat, qa, ka, name):
    nb, R, B = zat.shape
    aw = R // 3
    H = aw // HEAD_DIM
    hd = HEAD_DIM

    def body(jm_ref, q_ref, k_ref, v_ref, qa_ref, ka_ref, o_ref, lse_ref):
        h = pl.program_id(0)
        i = pl.program_id(1)
        qt = jnp.concatenate([q_ref[...] * 0.125, qa_ref[...].astype(BF16)], axis=0)

        def block(j, carry, masked):
            m, l, acc = carry
            s = _tn(jnp.concatenate([k_ref[j], ka_ref[j].astype(BF16)], axis=0), qt)
            if masked:
                s = jnp.where(_causal_mask(B), s, NEG)
            m_new = jnp.maximum(m, jnp.max(s, axis=0, keepdims=True))
            p = jnp.exp(s - m_new)
            a = jnp.exp(m - m_new)
            l = a * l + jnp.sum(p, axis=0, keepdims=True)
            acc = a * acc + _nn(v_ref[j], p.astype(BF16))
            return m_new, l, acc

        init = (jnp.full((1, B), NEG, F32), jnp.zeros((1, B), F32), jnp.zeros((hd, B), F32))
        carry = lax.fori_loop(jm_ref[h * nb + i], i, lambda j, c: block(j, c, False), init)
        m, l, acc = block(i, carry, True)
        o_ref[...] = acc / l
        lse_ref[...] = m + jnp.log(l)

    grid_spec = pltpu.PrefetchScalarGridSpec(
        num_scalar_prefetch=1, grid=(H, nb),
        in_specs=[pl.BlockSpec((None, hd, B), lambda h, i, jm: (i, h, 0)),
                  pl.BlockSpec((nb, hd, B), lambda h, i, jm: (0, H + h, 0)),
                  pl.BlockSpec((nb, hd, B), lambda h, i, jm: (0, 2 * H + h, 0)),
                  pl.BlockSpec((None, None, AUG, B), lambda h, i, jm: (h, i, 0, 0)),
                  pl.BlockSpec((None, nb, AUG, B), lambda h, i, jm: (h, 0, 0, 0))],
        out_specs=[pl.BlockSpec((None, hd, B), lambda h, i, jm: (i, h, 0)),
                   pl.BlockSpec((None, None, 1, B), lambda h, i, jm: (h, i, 0, 0))])
    return pl.pallas_call(
        body, name=name, grid_spec=grid_spec,
        out_shape=[SDS((nb, aw, B), F32), SDS((H, nb, 1, B), F32)],
        compiler_params=_cp("parallel", "parallel"))(jmin, zat, zat, zat, qa, ka)


def _gnorm_t_fwd(ot, g_col, name):
    nb, aw, B = ot.shape

    def body(o_ref, g_ref, y_ref):
        ov = o_ref[...]
        r = lax.rsqrt(jnp.mean(ov * ov, axis=0, keepdims=True) + EPS)
        y_ref[...] = (ov * r * g_ref[...]).T.astype(BF16)

    return pl.pallas_call(
        body, name=name, grid=(nb,), in_specs=[pl.BlockSpec((None, aw, B), lambda i: (i, 0, 0)), _full(g_col)],
        out_specs=pl.BlockSpec((B, aw), lambda i: (i, 0)), out_shape=SDS((nb * B, aw), BF16),
        compiler_params=_cp("parallel"))(ot, g_col)


def _attn_bwd(imax, zat, qa, ka, dot, lse, delta, name):
    nb, R, B = zat.shape
    aw = R // 3
    H = aw // HEAD_DIM
    hd = HEAD_DIM

    def body(im_ref, k_ref, v_ref, ka_ref, q_ref, qa_ref, do_ref, lse_ref, dl_ref,
             dq_ref, dqa_ref, dk_ref, dka_ref, dv_ref, dq_acc):
        h = pl.program_id(0)
        j = pl.program_id(1)

        @pl.when(j == 0)
        def _():
            dq_acc[...] = jnp.zeros_like(dq_acc)
            dqa_ref[...] = jnp.zeros_like(dqa_ref)
        kt = jnp.concatenate([k_ref[...], ka_ref[...].astype(BF16)], axis=0)
        vt = v_ref[...]

        def block(i, carry, masked):
            dk, dv = carry
            qt = jnp.concatenate([q_ref[i] * 0.125, qa_ref[i].astype(BF16)], axis=0)
            dov = do_ref[i]
            s = _tn(kt, qt)
            if masked:
                s = jnp.where(_causal_mask(B), s, NEG)
            p = jnp.exp(s - lse_ref[i])
            dp = _tn(vt, dov)
            ds = (p * (dp - dl_ref[i, pl.ds(h, 1), :])).astype(BF16)
            dv = dv + _nt(dov, p.astype(BF16))
            dk = dk + _nt(qt, ds)
            r = _nn(kt, ds)
            dq_acc[i] += 0.125 * r[:hd]
            dqa_ref[i] += r[hd:]
            return dk, dv

        carry = block(j, (jnp.zeros((hd + AUG, B), F32), jnp.zeros((hd, B), F32)), True)
        dk, dv = lax.fori_loop(j + 1, im_ref[h * nb + j] + 1, lambda i, c: block(i, c, False), carry)
        dk_ref[...] = dk[:hd].astype(BF16)
        dka_ref[...] = dk[hd:]
        dv_ref[...] = dv.astype(BF16)

        @pl.when(j == nb - 1)
        def _():
            dq_ref[...] = dq_acc[...].astype(BF16)

    head_rows = lambda off: pl.BlockSpec((nb, hd, B), lambda h, j, im: (0, off + h, 0))
    key_rows = lambda off: pl.BlockSpec((None, hd, B), lambda h, j, im: (j, off + h, 0))
    aug_all = pl.BlockSpec((None, nb, AUG, B), lambda h, j, im: (h, 0, 0, 0))
    aug_one = pl.BlockSpec((None, None, AUG, B), lambda h, j, im: (h, j, 0, 0))
    grid_spec = pltpu.PrefetchScalarGridSpec(
        num_scalar_prefetch=1, grid=(H, nb),
        in_specs=[key_rows(H), key_rows(2 * H), aug_one, head_rows(0), aug_all, head_rows(0),
                  pl.BlockSpec((None, nb, 1, B), lambda h, j, im: (h, 0, 0, 0)),
                  pl.BlockSpec((nb, H, B), lambda h, j, im: (0, 0, 0))],
        out_specs=[head_rows(0), aug_all, key_rows(0), aug_one, key_rows(0)],
        scratch_shapes=[pltpu.VMEM((nb, hd, B), F32)])
    return pl.pallas_call(
        body, name=name, grid_spec=grid_spec,
        out_shape=[SDS((nb, aw, B), BF16), SDS((H, nb, AUG, B), F32), SDS((nb, aw, B), BF16),
                   SDS((H, nb, AUG, B), F32), SDS((nb, aw, B), BF16)],
        compiler_params=_cp("arbitrary", "arbitrary"))(imax, zat, zat, ka, zat, qa, dot, lse, delta)


def _mix_dwt(dq, dk, dv, dzft, h, name):
    nb, aw, B = dq.shape
    D = h.shape[1]

    def body(dq_ref, dk_ref, dv_ref, dzf_ref, h_ref, o_ref, of_ref):
        first = pl.program_id(0) == 0
        hv = h_ref[...]
        part = jnp.concatenate([_nn(r[...], hv) for r in (dq_ref, dk_ref, dv_ref)], axis=0)
        _accumulate(o_ref, first, part)
        _accumulate(of_ref, first, _nn(dzf_ref[...].astype(BF16), hv))

    blk = pl.BlockSpec((None, aw, B), lambda i: (i, 0, 0))
    return pl.pallas_call(
        body, name=name, grid=(nb,),
        in_specs=[blk, blk, blk, pl.BlockSpec((SUBLANES, B), lambda i: (0, i)), pl.BlockSpec((B, D), lambda i: (i, 0))],
        out_specs=[pl.BlockSpec((3 * aw, D), lambda i: (0, 0)), pl.BlockSpec((SUBLANES, D), lambda i: (0, 0))],
        out_shape=[SDS((3 * aw, D), F32), SDS((SUBLANES, D), F32)],
        compiler_params=_cp("arbitrary"))(dq, dk, dv, dzft, h)


def _mix_bwd_in(dzc, dq, dk, dv, dzft, wc, wat, wft, x, g, dres, name, scatter):
    T, D = x.shape
    nb, aw, B = dq.shape

    def body(dzc_ref, dq_ref, dk_ref, dv_ref, dzf_ref, wc_ref, wa_ref, wf_ref, x_ref, g_ref, r_ref,
             dx_ref, dxb_ref, dg_ref):
        dh = _nn(dzc_ref[...], wc_ref[...])
        for s, r in enumerate((dq_ref, dk_ref, dv_ref)):
            dh = dh + _tn(r[...], wa_ref[s * aw:(s + 1) * aw, :])
        dh = dh + _tn(dzf_ref[...].astype(BF16), wf_ref[...])
        _norm_bwd_store(dh, x_ref, g_ref, r_ref, dx_ref, dxb_ref, dg_ref, pl.program_id(0) == 0)

    blk = pl.BlockSpec((None, aw, B), lambda i: (i, 0, 0))
    row = lambda n: pl.BlockSpec((B, n), lambda i: (i, 0))
    return _hosted_call(
        body, name, nb,
        [row(dzc.shape[1]), blk, blk, blk, pl.BlockSpec((SUBLANES, B), lambda i: (0, i)),
         _full(wc), _full(wat), _full(wft), row(D), _full(g), row(D)],
        [row(D), row(D), pl.BlockSpec((1, D), lambda i: (0, 0))],
        [SDS((T, D), F32), SDS((T, D), BF16), SDS((1, D), F32)],
        (dzc, dq, dk, dv, dzft, wc, wat, wft, x, g, dres), scatter)


def _mem_kv(mem, g, wkv, name):
    M, D = mem.shape
    dx = wkv.shape[2]

    def body(m_ref, g_ref, w_ref, kv_ref, h_ref):
        mv = m_ref[...]
        h = (mv * _rstd(mv) * g_ref[...]).astype(BF16)
        h_ref[...] = h
        for s in range(N_DEV):
            kv_ref[s] = _nn(h, w_ref[s]).astype(BF16)

    return pl.pallas_call(
        body, name=name, grid=(1,), in_specs=[_full(mem), _full(g), _full(wkv)],
        out_specs=[pl.BlockSpec((N_DEV, M, dx), lambda i: (0, 0, 0)), pl.BlockSpec((M, D), lambda i: (0, 0))],
        out_shape=[SDS((N_DEV, M, dx), BF16), SDS((M, D), BF16)], compiler_params=_cp("arbitrary"))(mem, g, wkv)


def _mem_kv_bwd(dkv, hm, wkv, mem, g, name):
    M, D = mem.shape
    dx = wkv.shape[2]

    def body(dkv_ref, h_ref, w_ref, m_ref, g_ref, dw_ref, dg_ref):
        hv = h_ref[...]
        dh = jnp.zeros((M, D), F32)
        for s in range(N_DEV):
            d = dkv_ref[s].astype(BF16)
            dw_ref[s] = _tn(hv, d).astype(BF16)
            dh = dh + _nt(d, w_ref[s])
        mv = m_ref[...]
        dg_ref[...] = jnp.sum(dh * (mv * _rstd(mv)), axis=0, keepdims=True)

    return pl.pallas_call(
        body, name=name, grid=(1,), in_specs=[_full(dkv), _full(hm), _full(wkv), _full(mem), _full(g)],
        out_specs=[pl.BlockSpec((N_DEV, D, dx), lambda i: (0, 0, 0)), pl.BlockSpec((1, D), lambda i: (0, 0))],
        out_shape=[SDS((N_DEV, D, dx), BF16), SDS((1, D), F32)], compiler_params=_cp("arbitrary"))(dkv, hm, wkv, mem, g)


def _xattn_probs(q_ref, kv_ref, h, dx, scale):
    qh = q_ref[:, h * dx:(h + 1) * dx]
    kh = kv_ref[h]
    s = _nt(qh, kh) * scale
    p = jnp.exp(s - jnp.max(s, axis=-1, keepdims=True))
    return qh, kh, p / jnp.sum(p, axis=-1, keepdims=True)


def _xattn(x, g, wq, kv, wo, name, tm=512):
    T, D = x.shape
    dx = D // N_XHEADS
    scale = dx ** -0.5
    tm = _tile(T, tm, SUBLANES)

    def body(x_ref, g_ref, wq_ref, kv_ref, wo_ref, xo_ref, h_ref, q_ref, o_ref):
        xv = x_ref[...]
        h = (xv * _rstd(xv) * g_ref[...]).astype(BF16)
        h_ref[...] = h
        q_ref[...] = _nn(h, _rows2d(wq_ref)).astype(BF16)
        for hd in range(N_XHEADS):
            _, _, p = _xattn_probs(q_ref, kv_ref, hd, dx, scale)
            o_ref[:, hd * dx:(hd + 1) * dx] = _nn(p.astype(BF16), kv_ref[N_XHEADS + hd]).astype(BF16)
        xo_ref[...] = xv + _nn(o_ref[...], _rows2d(wo_ref))

    row = pl.BlockSpec((tm, D), lambda i: (i, 0))
    return pl.pallas_call(
        body, name=name, grid=(T // tm,), in_specs=[row, _full(g), _full(wq), _full(kv), _full(wo)],
        out_specs=[row] * 4, out_shape=[SDS((T, D), F32)] + [SDS((T, D), BF16)] * 3,
        compiler_params=_cp("parallel"))(x, g, wq, kv, wo)


def _xattn_bwd(dyb, dres, x, g, q, kv, wq, wo, name, tm=512):
    T, D = x.shape
    dx = D // N_XHEADS
    scale = dx ** -0.5
    tm = _tile(T, tm, SUBLANES)

    def body(dy_ref, r_ref, x_ref, g_ref, q_ref, kv_ref, wq_ref, wo_ref, dx_ref, dxb_ref, dg_ref, dq_ref, dkv_ref):
        first = pl.program_id(0) == 0

        @pl.when(first)
        def _():
            dkv_ref[...] = jnp.zeros_like(dkv_ref)
        do = _nt(dy_ref[...], _rows2d(wo_ref)).astype(BF16)
        for hd in range(N_XHEADS):
            qh, kh, p = _xattn_probs(q_ref, kv_ref, hd, dx, scale)
            doh = do[:, hd * dx:(hd + 1) * dx]
            dp = _nt(doh, kv_ref[N_XHEADS + hd])
            ds = (p * (dp - jnp.sum(p * dp, axis=-1, keepdims=True)) * scale).astype(BF16)
            dq_ref[:, hd * dx:(hd + 1) * dx] = _nn(ds, kh).astype(BF16)
            dkv_ref[hd] += _tn(ds, qh)
            dkv_ref[N_XHEADS + hd] += _tn(p.astype(BF16), doh)
        dh = _nt(dq_ref[...], _rows2d(wq_ref))
        _norm_bwd_store(dh, x_ref, g_ref, r_ref, dx_ref, dxb_ref, dg_ref, first)

    row = pl.BlockSpec((tm, D), lambda i: (i, 0))
    return pl.pallas_call(
        body, name=name, grid=(T // tm,),
        in_specs=[row, row, row, _full(g), row, _full(kv), _full(wq), _full(wo)],
        out_specs=[row, row, pl.BlockSpec((1, D), lambda i: (0, 0)), row, pl.BlockSpec(kv.shape, lambda i: (0, 0, 0))],
        out_shape=[SDS((T, D), F32), SDS((T, D), BF16), SDS((1, D), F32), SDS((T, D), BF16), SDS(kv.shape, F32)],
        compiler_params=_cp("arbitrary"))(dyb, dres, x, g, q, kv, wq, wo)


def _mesh_position():
    return lax.axis_index("x"), lax.axis_index("y"), lax.axis_index("c")


class _GatherCopies:
    def __init__(self, x_refs, out_refs, send_sems, recv_sems, local_sems):
        self.x_refs, self.out_refs, self.n = x_refs, out_refs, len(x_refs)
        self.send_sems, self.recv_sems, self.local_sems = send_sems, recv_sems, local_sems
        x, y, c = _mesh_position()
        self.c = c
        self.me, self.sibling = (x, y, c), (x, y, 1 - c)
        self.chips = [(1 - x, y), (x, 1 - y), (1 - x, 1 - y)]

    def slot(self, w, px, py, pc):
        return self.out_refs[w].at[4 * px + 2 * py + pc]

    def copy(self, w, k, block, to, src=None):
        return pltpu.make_async_remote_copy(
            src_ref=self.slot(w, *block) if src is None else src, dst_ref=self.slot(w, *block),
            send_sem=self.send_sems.at[w, k], recv_sem=self.recv_sems.at[w, k], device_id=to,
            device_id_type=pl.DeviceIdType.MESH)

    def mine(self, w):
        return pltpu.make_async_copy(self.x_refs[w], self.slot(w, *self.me), self.local_sems.at[w])

    def first(self, w):
        src = self.x_refs[w]
        return [self.copy(w, 0, self.me, self.sibling, src=src)] + [
            self.copy(w, 1 + j, self.me, (*chip, self.c), src=src) for j, chip in enumerate(self.chips)]

    def passed(self, w):
        return [self.copy(w, 4 + j, (*chip, self.c), self.sibling) for j, chip in enumerate(self.chips)]


def _gather_start(*refs):
    cp = _GatherCopies(*refs)
    for w in range(cp.n):
        cp.mine(w).start()
        for f in cp.first(w):
            f.start()


def _gather_relay(*refs):
    cp = _GatherCopies(*refs)
    for w in range(cp.n):
        for j, (chip, fwd) in enumerate(zip(cp.chips, cp.passed(w))):
            cp.copy(w, 1 + j, (*chip, cp.c), cp.me).wait_recv()
            fwd.start()


def _gather_finish(*refs):
    cp = _GatherCopies(*refs)
    for w in range(cp.n):
        cp.copy(w, 0, cp.sibling, cp.me).wait_recv()
        for j, chip in enumerate(cp.chips):
            cp.copy(w, 4 + j, (*chip, 1 - cp.c), cp.me).wait_recv()
    for w in range(cp.n):
        for f in cp.first(w) + cp.passed(w):
            f.wait_send()
        cp.mine(w).wait()


def _all_gather(shards, name):
    n = len(shards)

    def body(*refs):
        parts = (refs[:n], refs[n:2 * n]) + refs[2 * n:]
        _gather_start(*parts)
        _gather_relay(*parts)
        _gather_finish(*parts)

    any_spec = pl.BlockSpec(memory_space=pl.ANY)
    return pl.pallas_call(
        body, name=name, out_shape=[SDS((N_DEV,) + s.shape, s.dtype) for s in shards],
        in_specs=[any_spec] * n, out_specs=[any_spec] * n,
        scratch_shapes=[pltpu.SemaphoreType.DMA((n, 7)), pltpu.SemaphoreType.DMA((n, 7)), pltpu.SemaphoreType.DMA((n,))],
    )(*shards)


def _scatter_copies(x_refs, out_refs, send_sems, recv_sems, local_sems):
    x, y, c = _mesh_position()
    me = 4 * x + 2 * y + c
    own, remote = [], []
    for w, (src, dst) in enumerate(zip(x_refs, out_refs)):
        own.append(pltpu.make_async_copy(src.at[me], dst.at[me], local_sems.at[w]))
        for k in range(1, N_DEV):
            px = 1 - x if k & 4 else x
            py = 1 - y if k & 2 else y
            pc = 1 - c if k & 1 else c
            remote.append(pltpu.make_async_remote_copy(
                src_ref=src.at[4 * px + 2 * py + pc], dst_ref=dst.at[me], send_sem=send_sems.at[w, k - 1],
                recv_sem=recv_sems.at[w, k - 1], device_id=(px, py, pc), device_id_type=pl.DeviceIdType.MESH))
    return own, remote


def _scatter_start(x_refs, out_refs, send_sems, recv_sems, local_sems):
    own, remote = _scatter_copies(x_refs, out_refs, send_sems, recv_sems, local_sems)
    for cp in own + remote:
        cp.start()


def _scatter_wait(x_refs, out_refs, send_sems, recv_sems, local_sems):
    own, remote = _scatter_copies(x_refs, out_refs, send_sems, recv_sems, local_sems)
    for cp in remote:
        cp.wait_recv()
    for cp in remote:
        cp.wait_send()
    for cp in own:
        cp.wait()


def _sum_adamw(parts, w, m, v, name, tr=256):
    L, R, C = w.shape
    tr = _tile(R, tr, SUBLANES)
    nblk = R // tr

    def body(*refs):
        p_refs = refs[:L]
        w_ref, m_ref, v_ref, g_ref, d_ref, mo_ref, vo_ref = refs[L:]
        layer = pl.program_id(0)
        g = None
        for k, p_ref in enumerate(p_refs):
            gk = p_ref[0].astype(F32)
            for s in range(1, N_DEV):
                gk = gk + p_ref[s].astype(F32)
            g = gk if g is None else jnp.where(layer == k, gk, g)
        mn = ADAM_B1 * m_ref[...] + (1.0 - ADAM_B1) * g
        vn = ADAM_B2 * v_ref[...] + (1.0 - ADAM_B2) * jnp.square(g)
        m_hat = mn / (1.0 - ADAM_B1 ** ADAM_STEP)
        v_hat = vn / (1.0 - ADAM_B2 ** ADAM_STEP)
        g_ref[...] = g
        d_ref[...] = -ADAM_LR * (m_hat / (jnp.sqrt(v_hat) + ADAM_EPS) + ADAM_WD * w_ref[...])
        mo_ref[...] = mn
        vo_ref[...] = vn

    def part_spec(k):
        return pl.BlockSpec((N_DEV, tr, C),
                            lambda l, i: (0, jnp.where(l == k, i, jnp.where(l < k, 0, nblk - 1)), 0))

    row = pl.BlockSpec((None, tr, C), lambda l, i: (l, i, 0))
    return pl.pallas_call(
        body, name=name, grid=(L, nblk), in_specs=[part_spec(k) for k in range(L)] + [row, row, row],
        out_specs=[row] * 4, out_shape=[SDS((L, R, C), F32)] * 4,
        compiler_params=_cp("arbitrary", "arbitrary"))(*parts, w, m, v)


def _pack(flat_list, row_mult):
    flat = jnp.concatenate(flat_list, axis=-1)
    n = flat.shape[-1]
    chunk = row_mult * PACK_COLS
    pad = -n % chunk
    flat = jnp.pad(flat, [(0, 0)] * (flat.ndim - 1) + [(0, pad)])
    return flat.reshape(flat.shape[:-1] + ((n + pad) // PACK_COLS, PACK_COLS))


def _unpack(packed, shapes):
    lead = packed.shape[:-2]
    flat = packed.reshape(lead + (-1,))
    out = []
    off = 0
    for shp in shapes:
        n = 1
        for d in shp:
            n *= d
        out.append(flat[..., off:off + n].reshape(lead + tuple(shp)))
        off += n
    return out


def _ffn_bwd(dx, dxb, saved, wgu, wd, tag):
    x, g, h, ab, act = saved
    dgu = _ffn_bwd_act(dxb, wd, ab, f"{tag}_bwd_act")
    dwd = _ffn_dwd(act, dxb, f"{tag}_dwd")
    dwgu = _ffn_dwgu(h, dgu, f"{tag}_dwgu")
    D = dwd.shape[-1]
    return _ffn_bwd_in(dgu, wgu, x, g, dx, f"{tag}_bwd_in", [dwgu.reshape(N_DEV, -1, D), dwd.reshape(N_DEV, -1, D)])


def _mix_fwd(x, g, w_in8, w_conv8, b_f, g_conv, g_att, w_out, tag):
    T, D = x.shape
    w_int = w_in8.reshape(-1, D)
    cw = D // 2
    aw = D - cw
    H = aw // HEAD_DIM
    B = _tile(T, ATT_BLOCK, LANES)
    nb = T // B
    w_c = w_int[:3 * cw]
    w_at = w_int[3 * cw:3 * cw + 3 * aw]
    w_ft = jnp.pad(w_int[3 * cw + 3 * aw:], ((0, SUBLANES - H), (0, 0)))
    zc, h, ync = _mix_in_conv(x, g, w_c, w_conv8, g_conv, f"{tag}_in_conv")
    zat, zft8, norms = _proj_t(h, w_at, w_ft, B, f"{tag}_in_att")
    zft = zft8[:H]
    bcol = b_f.reshape(H, 1)
    c, qa, ka = _logf_cumsum(zft, bcol, B, f"{tag}_cumsum")
    cb = c.reshape(H, nb, B)
    jmin, imax = _skip_table(cb[:, :, 0:1], cb[:, :, B - 1].reshape(H, 1, nb), norms, f"{tag}_skip")
    ot, lse = _attn_fwd(jmin.reshape(H * nb), zat, qa, ka, f"{tag}_attn")
    g_col = g_att.reshape(aw, 1)
    yna = _gnorm_t_fwd(ot, g_col, f"{tag}_gnorm")
    xo = _mm_res([ync, yna], w_out, x, f"{tag}_out")
    saved = (x, g, h, zc, zft, bcol, zat, qa, ka, imax.reshape(H * nb), ot, lse, ync, yna, w_c, w_at, w_ft, g_col)
    return xo, saved


def _mix_bwd(dx, dxb, saved, w_conv8, g_conv, w_out, tag, also_scatter):
    x, g, h, zc, zft, bcol, zat, qa, ka, imax, ot, lse, ync, yna, w_c, w_at, w_ft, g_col = saved
    T, D = x.shape
    H = zft.shape[0]
    dw_out = jnp.concatenate([_mm_tn(ync, dxb, f"{tag}_dwout_c"), _mm_tn(yna, dxb, f"{tag}_dwout_a")], axis=0)
    dzc, dwc8, dg_conv, dot, delta, dg_att, *p_also = _mix_bwd_head(dxb, w_out, zc, w_conv8, g_conv, ot, g_col,
                                                                   f"{tag}_bwd_out", also_scatter)
    dq, dqa, dk, dka, dv = _attn_bwd(imax, zat, qa, ka, dot, lse, delta, f"{tag}_attn_bwd")
    dcq = dqa[:, :, 3, :].reshape(H, T)
    dck = dka[:, :, 0, :].reshape(H, T)
    dzft, db = _logf_cumsum_bwd(dcq, dck, zft, bcol, f"{tag}_cumsum_bwd")
    dzft8 = jnp.pad(dzft, ((0, SUBLANES - H), (0, 0)))
    dw_c = _mm_tn(dzc, h, f"{tag}_dwin_c")
    dw_at, dw_ft = _mix_dwt(dq, dk, dv, dzft8, h, f"{tag}_dwin_a")
    dw_int = jnp.concatenate([dw_c, dw_at.astype(BF16), dw_ft[:H].astype(BF16)], axis=0)
    scatter = [dw_int.reshape(N_DEV, -1, D), dwc8[:3].reshape(3, N_DEV, -1).transpose(1, 0, 2),
               dw_out.reshape(N_DEV, D // N_DEV, D)]
    dx, dxb, dg, p_in, p_conv, p_out = _mix_bwd_in(dzc, dq, dk, dv, dzft8, w_c, w_at, w_ft, x, g, dx, f"{tag}_bwd_in",
                                                   scatter)
    return dx, dxb, dg, p_in, p_conv, db.reshape(H), dg_conv, dg_att[:, 0], p_out, p_also


def _xattn_block_fwd(x, g, mem, g_mem, w_q, w_kv, w_o, tag):
    kv, hm = _mem_kv(mem, g_mem, w_kv, f"{tag}_kv")
    xo, h, q, o = _xattn(x, g, w_q, kv, w_o, f"{tag}_block")
    return xo, (x, g, h, mem, g_mem, hm, q, kv, o)


def _xattn_block_bwd(dx, dxb, saved, w_q, w_kv, w_o, tag):
    x, g, h, mem, g_mem, hm, q, kv, o = saved
    dw_o = _mm_tn(o, dxb, f"{tag}_dwo")
    dx, dxb, dg, dq, dkv = _xattn_bwd(dxb, dx, x, g, q, kv, w_q, w_o, f"{tag}_block_bwd")
    dw_q = _mm_tn(h, dq, f"{tag}_dwq")
    dw_kv, dg_mem = _mem_kv_bwd(dkv, hm, w_kv, mem, g_mem, f"{tag}_kv_bwd")
    D = dw_q.shape[0]
    return dx, dxb, dg, dg_mem, [dw_q.reshape(N_DEV, D // N_DEV, D), dw_kv, dw_o.reshape(N_DEV, D // N_DEV, D)]


GATHER_FIRST = ['w_ffn1_gu']
GATHER_REST = ['w_ffn1_down', 'w_mix_in', 'w_mix_out', 'w_xq', 'w_xkv', 'w_xo', 'w_ffn2_gu', 'w_ffn2_down']
GATHER_NEXT_A = ['w_ffn1_gu', 'w_ffn1_down', 'w_mix_in', 'w_mix_out', 'w_xq', 'w_xkv', 'w_xo']
GATHER_NEXT_B = ['w_ffn2_gu', 'w_ffn2_down']


def _local_step(x, mem, tgt, P):
    L = P['g_ffn1'].shape[0]
    shards = lambda names, l: [(P[n][l].T if n in LONG_AXIS_LAST else P[n][l]).astype(BF16) for n in names]
    wgut = lambda n, l: W[n][l].reshape(-1, W[n][l].shape[-1])
    W = {n: [None] * L for n in SHARDED}

    def keep(names, l, arrays):
        for n, a in zip(names, arrays):
            W[n][l] = a

    *first, conv = _all_gather(shards(GATHER_FIRST, 0) + [P['w_conv']], "gather_first")
    keep(GATHER_FIRST, 0, first)
    w_conv = conv.transpose(1, 2, 0, 3).reshape(L, 3, -1)
    saved = []
    for l in range(L):
        row = lambda n: P[n][l][None, :]
        wc8 = jnp.pad(w_conv[l], ((0, SUBLANES - 3), (0, 0)))
        x1 = x
        bring = GATHER_REST if l == 0 else []
        ab, act, h, *got = _ffn_gu(x, row('g_ffn1'), wgut('w_ffn1_gu', l), f"l{l}_ffn1_gu", gather=shards(bring, l))
        keep(bring, l, got)
        x, = _ffn_down(act, W['w_ffn1_down'][l], x, f"l{l}_ffn1_down")
        s1 = (x1, row('g_ffn1'), h, ab, act)
        x, s2 = _mix_fwd(x, row('g_mix'), W['w_mix_in'][l], wc8, P['b_f'][l], row('g_conv_out'), P['g_att_out'][l],
                         W['w_mix_out'][l], f"l{l}_mix")
        x, s3 = _xattn_block_fwd(x, row('g_xattn'), mem, row('g_mem'), W['w_xq'][l], W['w_xkv'][l], W['w_xo'][l],
                                 f"l{l}_xattn")
        x4 = x
        bring_a, bring_b = (GATHER_NEXT_A, GATHER_NEXT_B) if l + 1 < L else ([], [])
        ab, act, h, *got = _ffn_gu(x, row('g_ffn2'), wgut('w_ffn2_gu', l), f"l{l}_ffn2_gu", gather=shards(bring_a, l + 1))
        keep(bring_a, l + 1, got)
        x, *got = _ffn_down(act, W['w_ffn2_down'][l], x, f"l{l}_ffn2_down", gather=shards(bring_b, l + 1))
        keep(bring_b, l + 1, got)
        saved.append((s1, s2, s3, (x4, row('g_ffn2'), h, ab, act), wc8))
    loss, dx, dxb, dg_final = _loss_head(x, P['g_final'][None, :], tgt, "loss_head")
    G = {n: [None] * L for n in WEIGHT_NAMES if n != 'g_final'}
    for l in reversed(range(L)):
        row = lambda n: P[n][l][None, :]
        s1, s2, s3, s4, wc8 = saved[l]
        dx, dxb, G['g_ffn2'][l], G['w_ffn2_gu'][l], G['w_ffn2_down'][l] = _ffn_bwd(
            dx, dxb, s4, wgut('w_ffn2_gu', l), W['w_ffn2_down'][l], f"l{l}_ffn2")
        dx, dxb, G['g_xattn'][l], G['g_mem'][l], xattn_grads = _xattn_block_bwd(
            dx, dxb, s3, W['w_xq'][l], W['w_xkv'][l], W['w_xo'][l], f"l{l}_xattn")
        (dx, dxb, G['g_mix'][l], G['w_mix_in'][l], G['w_conv'][l], G['b_f'][l], G['g_conv_out'][l], G['g_att_out'][l],
         G['w_mix_out'][l], (G['w_xq'][l], G['w_xkv'][l], G['w_xo'][l])) = _mix_bwd(
            dx, dxb, s2, wc8, row('g_conv_out'), W['w_mix_out'][l], f"l{l}_mix", xattn_grads)
        dx, dxb, G['g_ffn1'][l], G['w_ffn1_gu'][l], G['w_ffn1_down'][l] = _ffn_bwd(
            dx, dxb, s1, wgut('w_ffn1_gu', l), W['w_ffn1_down'][l], f"l{l}_ffn1")
    G['g_final'] = [dg_final]
    return loss, dx, G


def kernel(x, mem, g_ffn1, w_ffn1_gu, w_ffn1_down, g_mix, w_mix_in, w_conv, b_f, g_conv_out, g_att_out, w_mix_out, g_xattn, g_mem, w_xq, w_xkv, w_xo, g_ffn2, w_ffn2_gu, w_ffn2_down, g_final, loss_target, m_g_ffn1, m_w_ffn1_gu, m_w_ffn1_down, m_g_mix, m_w_mix_in, m_w_conv, m_b_f, m_g_conv_out, m_g_att_out, m_w_mix_out, m_g_xattn, m_g_mem, m_w_xq, m_w_xkv, m_w_xo, m_g_ffn2, m_w_ffn2_gu, m_w_ffn2_down, m_g_final, v_g_ffn1, v_w_ffn1_gu, v_w_ffn1_down, v_g_mix, v_w_mix_in, v_w_conv, v_b_f, v_g_conv_out, v_g_att_out, v_w_mix_out, v_g_xattn, v_g_mem, v_w_xq, v_w_xkv, v_w_xo, v_g_ffn2, v_w_ffn2_gu, v_w_ffn2_down, v_g_final):
    args = (x, mem, g_ffn1, w_ffn1_gu, w_ffn1_down, g_mix, w_mix_in, w_conv, b_f, g_conv_out, g_att_out, w_mix_out,
            g_xattn, g_mem, w_xq, w_xkv, w_xo, g_ffn2, w_ffn2_gu, w_ffn2_down, g_final)
    P = dict(zip(IN_NAMES, args))
    moms = (m_g_ffn1, m_w_ffn1_gu, m_w_ffn1_down, m_g_mix, m_w_mix_in, m_w_conv, m_b_f, m_g_conv_out, m_g_att_out,
            m_w_mix_out, m_g_xattn, m_g_mem, m_w_xq, m_w_xkv, m_w_xo, m_g_ffn2, m_w_ffn2_gu, m_w_ffn2_down, m_g_final)
    vars_ = (v_g_ffn1, v_w_ffn1_gu, v_w_ffn1_down, v_g_mix, v_w_mix_in, v_w_conv, v_b_f, v_g_conv_out, v_g_att_out,
             v_w_mix_out, v_g_xattn, v_g_mem, v_w_xq, v_w_xkv, v_w_xo, v_g_ffn2, v_w_ffn2_gu, v_w_ffn2_down, v_g_final)
    MOM = dict(zip(WEIGHT_NAMES, moms))
    VAR = dict(zip(WEIGHT_NAMES, vars_))

    loss_part, dx, grads = _local_step(x[0], mem[0], loss_target[0], {n: P[n] for n in WEIGHT_NAMES})

    small_shapes = [P[n].shape for n in REPLICATED] + [(1,)]
    small = _pack([a.reshape(-1) for n in REPLICATED for a in grads[n]] + [loss_part[0, :1]], SUBLANES)
    small_parts = _all_gather([small], "gather_small_grads")[0]

    res = {}
    for n in SHARDED:
        t = (lambda a: a.transpose(0, 2, 1)) if n in LONG_AXIS_LAST else (lambda a: a)
        res[n] = [t(o) for o in _sum_adamw(grads[n], t(P[n]), t(MOM[n]), t(VAR[n]), "adamw_" + n)]
    zero = [jnp.zeros((1,), F32)]
    w_, m_, v_ = (_pack([d[n].reshape(-1) for n in REPLICATED] + zero, SUBLANES)[None] for d in (P, MOM, VAR))
    small_out = [_unpack(o[0], small_shapes) for o in _sum_adamw([small_parts], w_, m_, v_, "adamw_vectors")]
    for k, n in enumerate(REPLICATED):
        res[n] = [o[k] for o in small_out]
    out = []
    for k in range(4):
        out += [res[n][k] for n in WEIGHT_NAMES]
    return (small_out[0][-1].reshape(()), dx[None], *out)
```

```python
import jax
import jax.numpy as jnp
from jax import lax
from jax.experimental import pallas as pl
from jax.experimental.pallas import tpu as pltpu

F32 = jnp.float32
BF16 = jnp.bfloat16
I32 = jnp.int32
SDS = jax.ShapeDtypeStruct

EPS = 1e-6
HEAD_DIM = 64
N_XHEADS = 4
N_DEV = 8
LANES = 128
SUBLANES = 8
MXU_COLS = 256
AUG = 16
ATT_BLOCK = 512
NEG = -1e30
SKIP_MARGIN = 115.0
PACK_COLS = 1024
VMEM_LIMIT = 58 * 1024 * 1024

ADAM_LR = 0.001
ADAM_B1 = 0.9
ADAM_B2 = 0.999
ADAM_EPS = 1e-08
ADAM_WD = 0.01
ADAM_STEP = 10

IN_NAMES = ['x', 'mem', 'g_ffn1', 'w_ffn1_gu', 'w_ffn1_down', 'g_mix', 'w_mix_in', 'w_conv', 'b_f', 'g_conv_out',
            'g_att_out', 'w_mix_out', 'g_xattn', 'g_mem', 'w_xq', 'w_xkv', 'w_xo', 'g_ffn2', 'w_ffn2_gu',
            'w_ffn2_down', 'g_final']
WEIGHT_NAMES = IN_NAMES[2:]
SHARDED = ['w_ffn1_gu', 'w_ffn1_down', 'w_mix_in', 'w_conv', 'w_mix_out', 'w_xq', 'w_xkv', 'w_xo', 'w_ffn2_gu',
           'w_ffn2_down']
REPLICATED = [n for n in WEIGHT_NAMES if n not in SHARDED]
LONG_AXIS_LAST = ['w_ffn1_gu', 'w_ffn2_gu', 'w_mix_in']


def _tile(n, pref, mult):
    t = min(pref, n) // mult * mult
    while t >= mult:
        if n % t == 0:
            return t
        t -= mult
    return n


def _cp(*sem):
    return pltpu.CompilerParams(dimension_semantics=sem, vmem_limit_bytes=VMEM_LIMIT)


def _nt(a, b):
    return lax.dot_general(a, b, (((1,), (1,)), ((), ())), preferred_element_type=F32)


def _tn(a, b):
    return lax.dot_general(a, b, (((0,), (0,)), ((), ())), preferred_element_type=F32)


def _nn(a, b):
    return jnp.dot(a, b, preferred_element_type=F32)


def _rstd(xv):
    return lax.rsqrt(jnp.mean(xv * xv, axis=-1, keepdims=True) + EPS)


def _full(a):
    nd = a.ndim
    return pl.BlockSpec(a.shape, lambda *_: (0,) * nd, pipeline_mode=pl.Buffered(1))


def _rows2d(w_ref):
    s, a, b = w_ref.shape
    return w_ref[...].reshape(s * a, b)


def _accumulate(ref, first, part):
    @pl.when(first)
    def _():
        ref[...] = part

    @pl.when(jnp.logical_not(first))
    def _():
        ref[...] += part


def _norm_bwd_store(dh, x_ref, g_ref, r_ref, dx_ref, dxb_ref, dg_ref, first):
    xv = x_ref[...]
    r = _rstd(xv)
    xh = xv * r
    dxh = dh * g_ref[...]
    dx = r * (dxh - xh * jnp.mean(dxh * xh, axis=-1, keepdims=True)) + r_ref[...]
    dx_ref[...] = dx
    dxb_ref[...] = dx.astype(BF16)
    _accumulate(dg_ref, first, jnp.sum(dh * xh, axis=0, keepdims=True))


def _mm_res(a_list, w, res, name, tm=512):
    T = res.shape[0]
    N = w.shape[-1]
    tm = _tile(T, tm, SUBLANES)
    n_a = len(a_list)

    def body(*refs):
        a_refs = refs[:n_a]
        w_ref, r_ref, o_ref = refs[n_a:]
        wv = _rows2d(w_ref)
        acc = r_ref[...]
        off = 0
        for a_ref in a_refs:
            k = a_ref.shape[1]
            acc = acc + _nn(a_ref[...], wv[off:off + k, :])
            off += k
        o_ref[...] = acc

    row = lambda n: pl.BlockSpec((tm, n), lambda i: (i, 0))
    return pl.pallas_call(
        body, name=name, grid=(T // tm,),
        in_specs=[row(a.shape[1]) for a in a_list] + [_full(w), row(N)], out_specs=row(N),
        out_shape=SDS((T, N), F32), compiler_params=_cp("parallel"))(*a_list, w, res)


def _mm_tn(a, b, name, out_dtype=BF16, tt=2048):
    T, K = a.shape
    N = b.shape[1]
    tt = _tile(T, tt, 2 * SUBLANES)
    nt = T // tt

    def body(a_ref, b_ref, o_ref, acc_ref):
        t = pl.program_id(0)
        _accumulate(acc_ref, t == 0, _tn(a_ref[...], b_ref[...]))

        @pl.when(t == nt - 1)
        def _():
            o_ref[...] = acc_ref[...].astype(o_ref.dtype)

    return pl.pallas_call(
        body, name=name, grid=(nt,),
        in_specs=[pl.BlockSpec((tt, K), lambda t: (t, 0)), pl.BlockSpec((tt, N), lambda t: (t, 0))],
        out_specs=pl.BlockSpec((K, N), lambda t: (0, 0)), out_shape=SDS((K, N), out_dtype),
        scratch_shapes=[pltpu.VMEM((K, N), F32)], compiler_params=_cp("arbitrary"))(a, b)


def _hosted_call(body, name, n_steps, in_specs, out_specs, out_shape, args, scatter=(), gather=(), scratch_shapes=()):
    scatter, gather, scratch_shapes = list(scatter), list(gather), list(scratch_shapes)
    n_s, n_g = len(scatter), len(gather)
    if n_s + n_g == 0:
        return pl.pallas_call(body, name=name, grid=(n_steps,), in_specs=list(in_specs), out_specs=list(out_specs),
                              out_shape=list(out_shape), scratch_shapes=scratch_shapes,
                              compiler_params=_cp("arbitrary"))(*args)
    n_in, n_out, n_scr = len(in_specs), len(out_specs), len(scratch_shapes)
    relay_step = min(max(1, 3 * n_steps // 4), n_steps - 1)

    def wrapped(*refs):
        ins, refs = refs[:n_in], refs[n_in:]
        xs, shards, refs = refs[:n_s], refs[n_s:n_s + n_g], refs[n_s + n_g:]
        outs, refs = refs[:n_out], refs[n_out:]
        recvs, gathered, refs = refs[:n_s], refs[n_s:n_s + n_g], refs[n_s + n_g:]
        outs, sems = outs + refs[:n_scr], refs[n_scr:]
        s_sems, g_sems = (sems[:3], sems[3:]) if n_s else ((), sems)
        step = pl.program_id(0)

        @pl.when(step == 0)
        def _():
            if n_s:
                _scatter_start(xs, recvs, *s_sems)
            if n_g:
                _gather_start(shards, gathered, *g_sems)
        body(*ins, *outs)
        if n_g:
            @pl.when(step == relay_step)
            def _():
                _gather_relay(shards, gathered, *g_sems)

        @pl.when(step == n_steps - 1)
        def _():
            if n_s:
                _scatter_wait(xs, recvs, *s_sems)
            if n_g:
                _gather_finish(shards, gathered, *g_sems)

    any_spec = pl.BlockSpec(memory_space=pl.ANY)
    sems = []
    for n in (n_s, n_g):
        if n:
            sems += [pltpu.SemaphoreType.DMA((n, N_DEV - 1)), pltpu.SemaphoreType.DMA((n, N_DEV - 1)),
                     pltpu.SemaphoreType.DMA((n,))]
    return pl.pallas_call(
        wrapped, name=name, grid=(n_steps,), in_specs=list(in_specs) + [any_spec] * (n_s + n_g),
        out_specs=list(out_specs) + [any_spec] * (n_s + n_g),
        out_shape=list(out_shape) + [SDS(b.shape, b.dtype) for b in scatter]
        + [SDS((N_DEV,) + b.shape, b.dtype) for b in gather],
        scratch_shapes=scratch_shapes + sems, compiler_params=_cp("arbitrary"))(*args, *scatter, *gather)


def _hidden_chunks(F):
    edges = list(range(0, F, 3 * MXU_COLS)) + [F]
    return list(zip(edges[:-1], edges[1:]))


def _ffn_gu(x, g, wgut, name, gather=(), tm=512):
    T, D = x.shape
    F = wgut.shape[0] // 2
    tm = _tile(T, tm, SUBLANES)

    def body(x_ref, g_ref, w_ref, ab_ref, act_ref, h_ref):
        xv = x_ref[...]
        h = (xv * _rstd(xv) * g_ref[...]).astype(BF16)
        h_ref[...] = h
        for c0, c1 in _hidden_chunks(F):
            gt = _nt(h, w_ref[c0:c1, :])
            up = _nt(h, w_ref[F + c0:F + c1, :])
            sg = jax.nn.sigmoid(gt)
            silu = gt * sg
            ab_ref[0, :, c0:c1] = (0.5 * up * (sg * (1.0 + gt * (1.0 - sg)))).astype(BF16)
            ab_ref[1, :, c0:c1] = (0.5 * silu).astype(BF16)
            act_ref[:, c0:c1] = (silu * up).astype(BF16)

    row = lambda n: pl.BlockSpec((tm, n), lambda i: (i, 0))
    return _hosted_call(
        body, name, T // tm, [row(D), _full(g), _full(wgut)],
        [pl.BlockSpec((2, tm, F), lambda i: (0, i, 0)), row(F), row(D)],
        [SDS((2, T, F), BF16), SDS((T, F), BF16), SDS((T, D), BF16)], (x, g, wgut), gather=gather)


def _ffn_down(act, wd, res, name, gather=(), tm=1024):
    T, F = act.shape
    D = wd.shape[2]
    tm = _tile(T, tm, SUBLANES)

    def body(a_ref, w_ref, r_ref, o_ref):
        o_ref[...] = r_ref[...] + 0.5 * _nn(a_ref[...], _rows2d(w_ref))

    row = lambda n: pl.BlockSpec((tm, n), lambda i: (i, 0))
    return _hosted_call(body, name, T // tm, [row(F), _full(wd), row(D)], [row(D)], [SDS((T, D), F32)],
                        (act, wd, res), gather=gather)


def _ffn_bwd_act(dyb, wd, ab, name, tm=512):
    T, D = dyb.shape
    _, _, F = ab.shape
    tm = _tile(T, tm, SUBLANES)

    def body(dy_ref, w_ref, ab_ref, o_ref):
        wv = _rows2d(w_ref)
        dy = dy_ref[...]
        for c0, c1 in _hidden_chunks(F):
            da = _nt(dy, wv[c0:c1, :])
            o_ref[0, :, c0:c1] = (da * ab_ref[0, :, c0:c1].astype(F32)).astype(BF16)
            o_ref[1, :, c0:c1] = (da * ab_ref[1, :, c0:c1].astype(F32)).astype(BF16)

    blocks = pl.BlockSpec((2, tm, F), lambda i: (0, i, 0))
    return pl.pallas_call(
        body, name=name, grid=(T // tm,),
        in_specs=[pl.BlockSpec((tm, D), lambda i: (i, 0)), _full(wd), blocks], out_specs=blocks,
        out_shape=SDS((2, T, F), BF16), compiler_params=_cp("parallel"))(dyb, wd, ab)


def _ffn_dwd(act, dyb, name, tt=2048):
    T, F = act.shape
    D = dyb.shape[1]
    tt = _tile(T, tt, 2 * SUBLANES)
    nt = T // tt
    fh = F // 2

    def body(a_ref, b_ref, o_ref, acc_ref):
        t = pl.program_id(1)
        _accumulate(acc_ref, t == 0, _tn(a_ref[...], b_ref[...]))

        @pl.when(t == nt - 1)
        def _():
            o_ref[...] = (0.5 * acc_ref[...]).astype(BF16)

    return pl.pallas_call(
        body, name=name, grid=(2, nt),
        in_specs=[pl.BlockSpec((tt, fh), lambda j, t: (t, j)), pl.BlockSpec((tt, D), lambda j, t: (t, 0))],
        out_specs=pl.BlockSpec((fh, D), lambda j, t: (j, 0)), out_shape=SDS((F, D), BF16),
        scratch_shapes=[pltpu.VMEM((fh, D), F32)], compiler_params=_cp("parallel", "arbitrary"))(act, dyb)


def _ffn_dwgu(h, dgu, name, tt=2048):
    T, D = h.shape
    _, _, F = dgu.shape
    tt = _tile(T, tt, 2 * SUBLANES)
    nt = T // tt
    fh = F // 2

    def body(a_ref, b_ref, o_ref, acc_ref):
        t = pl.program_id(1)
        _accumulate(acc_ref, t == 0, _tn(b_ref[...], a_ref[...]))

        @pl.when(t == nt - 1)
        def _():
            o_ref[...] = acc_ref[...].astype(BF16)

    return pl.pallas_call(
        body, name=name, grid=(4, nt),
        in_specs=[pl.BlockSpec((tt, D), lambda q, t: (t, 0)),
                  pl.BlockSpec((None, tt, fh), lambda q, t: (q // 2, t, q % 2))],
        out_specs=pl.BlockSpec((fh, D), lambda q, t: (q, 0)), out_shape=SDS((2 * F, D), BF16),
        scratch_shapes=[pltpu.VMEM((fh, D), F32)], compiler_params=_cp("parallel", "arbitrary"))(h, dgu)


def _ffn_bwd_in(dgu, wgut, x, g, dres, name, scatter, tm=512):
    T, D = x.shape
    _, _, F = dgu.shape
    tm = _tile(T, tm, SUBLANES)

    def body(a_ref, w_ref, x_ref, g_ref, r_ref, dx_ref, dxb_ref, dg_ref):
        dh = _nn(a_ref[0], w_ref[:F, :]) + _nn(a_ref[1], w_ref[F:, :])
        _norm_bwd_store(dh, x_ref, g_ref, r_ref, dx_ref, dxb_ref, dg_ref, pl.program_id(0) == 0)

    row = pl.BlockSpec((tm, D), lambda i: (i, 0))
    return _hosted_call(
        body, name, T // tm, [pl.BlockSpec((2, tm, F), lambda i: (0, i, 0)), _full(wgut), row, _full(g), row],
        [row, row, pl.BlockSpec((1, D), lambda i: (0, 0))],
        [SDS((T, D), F32), SDS((T, D), BF16), SDS((1, D), F32)], (dgu, wgut, x, g, dres), scatter)


def _loss_head(x, g, tgt, name, tm=512):
    T, D = x.shape
    tm = _tile(T, tm, SUBLANES)

    def body(x_ref, g_ref, t_ref, loss_ref, dx_ref, dxb_ref, dg_ref):
        first = pl.program_id(0) == 0
        xv = x_ref[...]
        r = _rstd(xv)
        xh = xv * r
        e = xh * g_ref[...] - t_ref[...]
        part = 0.5 * jnp.sum(jnp.mean(e * e, axis=-1, keepdims=True), axis=0, keepdims=True)
        dy = e * (1.0 / D)
        dxh = dy * g_ref[...]
        dx = r * (dxh - xh * jnp.mean(dxh * xh, axis=-1, keepdims=True))
        dx_ref[...] = dx
        dxb_ref[...] = dx.astype(BF16)
        _accumulate(loss_ref, first, jnp.broadcast_to(part, loss_ref.shape))
        _accumulate(dg_ref, first, jnp.sum(dy * xh, axis=0, keepdims=True))

    row = pl.BlockSpec((tm, D), lambda i: (i, 0))
    vec = pl.BlockSpec((1, D), lambda i: (0, 0))
    return pl.pallas_call(
        body, name=name, grid=(T // tm,), in_specs=[row, vec, row],
        out_specs=[pl.BlockSpec((SUBLANES, LANES), lambda i: (0, 0)), row, row, vec],
        out_shape=[SDS((SUBLANES, LANES), F32), SDS((T, D), F32), SDS((T, D), BF16), SDS((1, D), F32)],
        compiler_params=_cp("arbitrary"))(x, g, tgt)


def _shift_down(u, k, prev_row):
    rows = lax.broadcasted_iota(I32, u.shape, 0)
    s = pltpu.roll(u, k, 0)
    for t in range(k):
        s = jnp.where(rows == t, prev_row(SUBLANES - k + t), s)
    return s


def _shift_up(u, k, next_row):
    n = u.shape[0]
    rows = lax.broadcasted_iota(I32, u.shape, 0)
    s = pltpu.roll(u, n - k, 0)
    for t in range(k):
        s = jnp.where(rows == n - k + t, next_row(t), s)
    return s


def _conv_taps(z_ref, zp_ref, w_ref, cw, first):
    u = z_ref[:, cw:2 * cw] * z_ref[:, 2 * cw:]

    def prev_row(r):
        return jnp.where(first, 0.0, zp_ref[r:r + 1, cw:2 * cw] * zp_ref[r:r + 1, 2 * cw:])

    u1 = _shift_down(u, 1, prev_row)
    u2 = _shift_down(u, 2, prev_row)
    cv = w_ref[0:1, :] * u2 + w_ref[1:2, :] * u1 + w_ref[2:3, :] * u
    return u, u1, u2, cv


def _mix_in_conv(x, g, w, wc, gc, name, tm=512):
    T, D = x.shape
    cw = w.shape[0] // 3
    tm = _tile(T, tm, SUBLANES)

    def body(x_ref, g_ref, w_ref, wc_ref, gc_ref, z_ref, h_ref, o_ref, prev_ref):
        xv = x_ref[...]
        h = (xv * _rstd(xv) * g_ref[...]).astype(BF16)
        h_ref[...] = h
        z_ref[...] = _nt(h, w_ref[...])
        _, _, _, cv = _conv_taps(z_ref, prev_ref, wc_ref, cw, pl.program_id(0) == 0)
        y = z_ref[:, :cw] * cv
        o_ref[...] = (y * _rstd(y) * gc_ref[...]).astype(BF16)
        prev_ref[...] = z_ref[tm - SUBLANES:, :]

    row = lambda n: pl.BlockSpec((tm, n), lambda i: (i, 0))
    return pl.pallas_call(
        body, name=name, grid=(T // tm,), in_specs=[row(D), _full(g), _full(w), _full(wc), _full(gc)],
        out_specs=[row(3 * cw), row(D), row(cw)],
        out_shape=[SDS((T, 3 * cw), F32), SDS((T, D), BF16), SDS((T, cw), BF16)],
        scratch_shapes=[pltpu.VMEM((SUBLANES, 3 * cw), F32)], compiler_params=_cp("arbitrary"))(x, g, w, wc, gc)


def _mix_bwd_head(dyb, w_out, zc, wc, gc, ot, g_col, name, scatter):
    nb, aw, tm = ot.shape
    T, D = dyb.shape
    cw = zc.shape[1] // 3
    H = aw // HEAD_DIM
    hb = tm // SUBLANES

    def body(dyb_ref, wo_ref, z_ref, zp_ref, w_ref, g_ref, o_ref, ga_ref,
             dz_ref, dw_ref, dg_ref, do_ref, dl_ref, dga_ref, carry_ref):
        s = pl.program_id(0)
        dy = _nt(dyb_ref[...], _rows2d(wo_ref))
        ov = o_ref[...]
        ra = lax.rsqrt(jnp.mean(ov * ov, axis=0, keepdims=True) + EPS)
        oh = ov * ra
        dya = dy[:, cw:].T
        dyah = dya * ga_ref[...]
        dov = ra * (dyah - oh * jnp.mean(dyah * oh, axis=0, keepdims=True))
        do_ref[...] = dov.astype(BF16)
        dl_ref[...] = jnp.sum((dov * ov).reshape(H, HEAD_DIM, tm), axis=1)
        _accumulate(dga_ref, s == 0, jnp.broadcast_to(jnp.sum(dya * oh, axis=1, keepdims=True), dga_ref.shape))
        u, u1, u2, cv = _conv_taps(z_ref, zp_ref, w_ref, cw, s == nb - 1)
        zb = z_ref[:, :cw]
        y = zb * cv
        r = _rstd(y)
        yh = y * r
        dyn = dy[:, :cw]
        dyh = dyn * g_ref[...]
        dyc = r * (dyh - yh * jnp.mean(dyh * yh, axis=-1, keepdims=True))
        dcv = dyc * zb

        def next_row(t):
            return jnp.where(s == 0, 0.0, carry_ref[t:t + 1, :])

        du = w_ref[2:3, :] * dcv + w_ref[1:2, :] * _shift_up(dcv, 1, next_row) + w_ref[0:1, :] * _shift_up(dcv, 2, next_row)
        carry_ref[...] = dcv[0:SUBLANES, :]
        dz_ref[:, :cw] = (dyc * cv).astype(BF16)
        dz_ref[:, cw:2 * cw] = (du * z_ref[:, 2 * cw:]).astype(BF16)
        dz_ref[:, 2 * cw:] = (du * z_ref[:, cw:2 * cw]).astype(BF16)
        tap = lax.broadcasted_iota(I32, (SUBLANES, cw), 0)
        dwp = jnp.where(tap == 0, jnp.sum(dcv * u2, axis=0, keepdims=True),
                        jnp.where(tap == 1, jnp.sum(dcv * u1, axis=0, keepdims=True),
                                  jnp.where(tap == 2, jnp.sum(dcv * u, axis=0, keepdims=True), 0.0)))
        _accumulate(dw_ref, s == 0, dwp)
        _accumulate(dg_ref, s == 0, jnp.sum(dyn * yh, axis=0, keepdims=True))

    rev = lambda s: nb - 1 - s
    rows = lambda n: pl.BlockSpec((tm, n), lambda s: (rev(s), 0))
    blk = pl.BlockSpec((None, aw, tm), lambda s: (rev(s), 0, 0))
    return _hosted_call(
        body, name, nb,
        [rows(D), _full(w_out), rows(3 * cw),
         pl.BlockSpec((SUBLANES, 3 * cw), lambda s: (jnp.maximum(rev(s) * hb - 1, 0), 0)),
         _full(wc), _full(gc), blk, _full(g_col)],
        [rows(3 * cw), pl.BlockSpec((SUBLANES, cw), lambda s: (0, 0)), pl.BlockSpec((1, cw), lambda s: (0, 0)),
         blk, pl.BlockSpec((None, H, tm), lambda s: (rev(s), 0, 0)), pl.BlockSpec((aw, LANES), lambda s: (0, 0))],
        [SDS((T, 3 * cw), BF16), SDS((SUBLANES, cw), F32), SDS((1, cw), F32),
         SDS((nb, aw, tm), BF16), SDS((nb, H, tm), F32), SDS((aw, LANES), F32)],
        (dyb, w_out, zc, zc, wc, gc, ot, g_col), scatter=scatter, scratch_shapes=[pltpu.VMEM((SUBLANES, cw), F32)])


def _proj_t(h, wat, wft, B, name):
    T, D = h.shape
    R = wat.shape[0]
    nb = T // B
    qk = 2 * R // 3
    g = qk // HEAD_DIM

    def body(h_ref, wa_ref, wf_ref, za_ref, zf_ref, n_ref):
        hv = h_ref[...]
        zab = _nt(wa_ref[...], hv).astype(BF16)
        za_ref[...] = zab
        zf_ref[...] = _nt(wf_ref[...], hv)
        zv = zab[:qk].astype(F32)
        ss = jnp.sum((zv * zv).reshape(g, HEAD_DIM, B), axis=1)
        mx = jnp.broadcast_to(jnp.max(ss, axis=1, keepdims=True), n_ref.shape)

        @pl.when(pl.program_id(0) == 0)
        def _():
            n_ref[...] = mx

        @pl.when(pl.program_id(0) > 0)
        def _():
            n_ref[...] = jnp.maximum(n_ref[...], mx)

    return pl.pallas_call(
        body, name=name, grid=(nb,), in_specs=[pl.BlockSpec((B, D), lambda i: (i, 0)), _full(wat), _full(wft)],
        out_specs=[pl.BlockSpec((None, R, B), lambda i: (i, 0, 0)), pl.BlockSpec((SUBLANES, B), lambda i: (0, i)),
                   pl.BlockSpec((g, LANES), lambda i: (0, 0))],
        out_shape=[SDS((nb, R, B), BF16), SDS((SUBLANES, T), F32), SDS((g, LANES), F32)],
        compiler_params=_cp("arbitrary"))(h, wat, wft)


def _split3(v):
    hi = v.astype(BF16).astype(F32)
    r1 = v - hi
    mid = r1.astype(BF16).astype(F32)
    lo = (r1 - mid).astype(BF16).astype(F32)
    return hi, mid, lo


def _tri_dot3(v, tri):
    hi, mid, lo = _split3(v)
    return _nn(hi.astype(BF16), tri) + _nn(mid.astype(BF16), tri) + _nn(lo.astype(BF16), tri)


def _logf_cumsum(zft, b, tb, name):
    H, T = zft.shape

    def body(z_ref, b_ref, c_ref, qa_ref, ka_ref, carry_ref, hi_ref, mid_ref, lo_ref):
        @pl.when(pl.program_id(0) == 0)
        def _():
            carry_ref[...] = jnp.zeros_like(carry_ref)
        xv = z_ref[...] + b_ref[...]
        lf = jnp.minimum(xv, 0.0) - jnp.log(1.0 + jnp.exp(-jnp.abs(xv)))
        src = lax.broadcasted_iota(I32, (tb, tb), 0)
        dst = lax.broadcasted_iota(I32, (tb, tb), 1)
        tri = jnp.where(src <= dst, 1.0, 0.0).astype(BF16)
        cs = _tri_dot3(lf, tri) + carry_ref[...]
        c_ref[...] = cs
        hi_ref[...], mid_ref[...], lo_ref[...] = _split3(cs)
        col = lax.broadcasted_iota(I32, cs.shape, 1)
        carry_ref[...] = jnp.sum(jnp.where(col == tb - 1, cs, 0.0), axis=1, keepdims=True)
        row = lax.broadcasted_iota(I32, (AUG, tb), 0)
        for h in range(H):
            terms = [r[h:h + 1, :] for r in (hi_ref, mid_ref, lo_ref)]
            qa, ka = jnp.where(row < 3, 1.0, 0.0), jnp.where((row >= 3) & (row < 6), 1.0, 0.0)
            for k, t in enumerate(terms):
                qa = jnp.where(row == 3 + k, t, qa)
                ka = jnp.where(row == k, -t, ka)
            qa_ref[h] = qa
            ka_ref[h] = ka

    blk = pl.BlockSpec((H, tb), lambda i: (0, i))
    aug = pl.BlockSpec((H, None, AUG, tb), lambda i: (0, i, 0, 0))
    return pl.pallas_call(
        body, name=name, grid=(T // tb,), in_specs=[blk, _full(b)], out_specs=[blk, aug, aug],
        out_shape=[SDS((H, T), F32), SDS((H, T // tb, AUG, tb), F32), SDS((H, T // tb, AUG, tb), F32)],
        scratch_shapes=[pltpu.VMEM((H, 1), F32)] + [pltpu.VMEM((H, tb), F32)] * 3,
        compiler_params=_cp("arbitrary"))(zft, b)


def _logf_cumsum_bwd(dcq, dck, zft, b, name, tb=512):
    H, T = zft.shape
    tb = _tile(T, tb, LANES)
    nb = T // tb

    def body(dq_ref, dk_ref, z_ref, b_ref, o_ref, db_ref, carry_ref):
        s = pl.program_id(0)

        @pl.when(s == 0)
        def _():
            carry_ref[...] = jnp.zeros_like(carry_ref)
        dc = dq_ref[...] - dk_ref[...]
        src = lax.broadcasted_iota(I32, (tb, tb), 0)
        dst = lax.broadcasted_iota(I32, (tb, tb), 1)
        tri = jnp.where(src >= dst, 1.0, 0.0).astype(BF16)
        dl = _tri_dot3(dc, tri) + carry_ref[...]
        col = lax.broadcasted_iota(I32, dl.shape, 1)
        carry_ref[...] = jnp.sum(jnp.where(col == 0, dl, 0.0), axis=1, keepdims=True)
        dz = dl * jax.nn.sigmoid(-(z_ref[...] + b_ref[...]))
        o_ref[...] = dz
        _accumulate(db_ref, s == 0, jnp.sum(dz, axis=1, keepdims=True))

    blk = pl.BlockSpec((H, tb), lambda s: (0, nb - 1 - s))
    vec = pl.BlockSpec((H, 1), lambda s: (0, 0))
    return pl.pallas_call(
        body, name=name, grid=(nb,), in_specs=[blk, blk, blk, vec], out_specs=[blk, vec],
        out_shape=[SDS((H, T), F32), SDS((H, 1), F32)],
        scratch_shapes=[pltpu.VMEM((H, 1), F32)], compiler_params=_cp("arbitrary"))(dcq, dck, zft, b)


def _skip_table(cs_col, ce_row, norms, name):
    H, nb, _ = cs_col.shape

    def body(cs_ref, ce_ref, n_ref, jm_ref, im_ref):
        h = pl.program_id(0)
        nq = n_ref[pl.ds(h, 1), 0:1]
        nk = n_ref[pl.ds(H + h, 1), 0:1]
        bound = 2.0 * jnp.sqrt(nq * nk) * 0.125
        skip = jnp.where(bound + cs_ref[...] - ce_ref[...] <= -SKIP_MARGIN, 1, 0).astype(I32)
        jm_ref[...] = jnp.sum(skip, axis=1, keepdims=True)
        im_ref[...] = nb - 1 - jnp.sum(skip, axis=0, keepdims=True)

    col = pl.BlockSpec((None, nb, 1), lambda h: (h, 0, 0))
    row = pl.BlockSpec((None, 1, nb), lambda h: (h, 0, 0))
    return pl.pallas_call(
        body, name=name, grid=(H,), in_specs=[col, row, _full(norms)], out_specs=[col, row],
        out_shape=[SDS((H, nb, 1), I32), SDS((H, 1, nb), I32)], compiler_params=_cp("parallel"))(cs_col, ce_row, norms)


def _causal_mask(B):
    krow = lax.broadcasted_iota(I32, (B, B), 0)
    qcol = lax.broadcasted_iota(I32, (B, B), 1)
    return krow <= qcol


def _attn_fwd(jmin, zat, qa, ka, name):
    nb, R, B = zat.shape
    aw = R // 3
    H = aw // HEAD_DIM
    hd = HEAD_DIM

    def body(jm_ref, q_ref, k_ref, v_ref, qa_ref, ka_ref, o_ref, lse_ref):
        h = pl.program_id(0)
        i = pl.program_id(1)
        qt = jnp.concatenate([q_ref[...] * 0.125, qa_ref[...].astype(BF16)], axis=0)

        def keys(j):
            return jnp.concatenate([k_ref[j], ka_ref[j].astype(BF16)], axis=0)

        def update(s, vt, carry):
            m, l, acc = carry
            m_new = jnp.maximum(m, jnp.max(s, axis=0, keepdims=True))
            p = jnp.exp(s - m_new)
            a = jnp.exp(m - m_new)
            l = a * l + jnp.sum(p, axis=0, keepdims=True)
            acc = a * acc + _nn(vt, p.astype(BF16))
            return m_new, l, acc

        def block(j, carry, masked):
            s = _tn(keys(j), qt)
            if masked:
                s = jnp.where(_causal_mask(B), s, NEG)
            return update(s, v_ref[j], carry)

        def two_blocks(t, carry):
            j = j0 + 2 * t
            s = _tn(jnp.concatenate([keys(j), keys(j + 1)], axis=1), qt)
            return update(s, jnp.concatenate([v_ref[j], v_ref[j + 1]], axis=1), carry)

        j0 = jm_ref[h * nb + i]
        pairs = (i - j0) // 2
        init = (jnp.full((1, B), NEG, F32), jnp.zeros((1, B), F32), jnp.zeros((hd, B), F32))
        carry = lax.fori_loop(0, pairs, two_blocks, init)
        carry = lax.fori_loop(j0 + 2 * pairs, i, lambda j, c: block(j, c, False), carry)
        m, l, acc = block(i, carry, True)
        o_ref[...] = acc / l
        lse_ref[...] = m + jnp.log(l)

    grid_spec = pltpu.PrefetchScalarGridSpec(
        num_scalar_prefetch=1, grid=(H, nb),
        in_specs=[pl.BlockSpec((None, hd, B), lambda h, i, jm: (i, h, 0)),
                  pl.BlockSpec((nb, hd, B), lambda h, i, jm: (0, H + h, 0)),
                  pl.BlockSpec((nb, hd, B), lambda h, i, jm: (0, 2 * H + h, 0)),
                  pl.BlockSpec((None, None, AUG, B), lambda h, i, jm: (h, i, 0, 0)),
                  pl.BlockSpec((None, nb, AUG, B), lambda h, i, jm: (h, 0, 0, 0))],
        out_specs=[pl.BlockSpec((None, hd, B), lambda h, i, jm: (i, h, 0)),
                   pl.BlockSpec((None, None, 1, B), lambda h, i, jm: (h, i, 0, 0))])
    return pl.pallas_call(
        body, name=name, grid_spec=grid_spec,
        out_shape=[SDS((nb, aw, B), F32), SDS((H, nb, 1, B), F32)],
        compiler_params=_cp("parallel", "parallel"))(jmin, zat, zat, zat, qa, ka)


def _gnorm_t_fwd(ot, g_col, name):
    nb, aw, B = ot.shape

    def body(o_ref, g_ref, y_ref):
        ov = o_ref[...]
        r = lax.rsqrt(jnp.mean(ov * ov, axis=0, keepdims=True) + EPS)
        y_ref[...] = (ov * r * g_ref[...]).T.astype(BF16)

    return pl.pallas_call(
        body, name=name, grid=(nb,), in_specs=[pl.BlockSpec((None, aw, B), lambda i: (i, 0, 0)), _full(g_col)],
        out_specs=pl.BlockSpec((B, aw), lambda i: (i, 0)), out_shape=SDS((nb * B, aw), BF16),
        compiler_params=_cp("parallel"))(ot, g_col)


def _attn_bwd(imax, zat, qa, ka, dot, lse, delta, name):
    nb, R, B = zat.shape
    aw = R // 3
    H = aw // HEAD_DIM
    hd = HEAD_DIM

    def body(im_ref, k_ref, v_ref, ka_ref, q_ref, qa_ref, do_ref, lse_ref, dl_ref,
             dq_ref, dqa_ref, dk_ref, dka_ref, dv_ref, dq_acc):
        h = pl.program_id(0)
        j = pl.program_id(1)

        @pl.when(j == 0)
        def _():
            dq_acc[...] = jnp.zeros_like(dq_acc)
            dqa_ref[...] = jnp.zeros_like(dqa_ref)
        kt = jnp.concatenate([k_ref[...], ka_ref[...].astype(BF16)], axis=0)
        vt = v_ref[...]

        def queries(i):
            return jnp.concatenate([q_ref[i] * 0.125, qa_ref[i].astype(BF16)], axis=0)

        def grads(qt, dov, lse, dl, carry, masked):
            dk, dv = carry
            s = _tn(kt, qt)
            if masked:
                s = jnp.where(_causal_mask(B), s, NEG)
            p = jnp.exp(s - lse)
            dp = _tn(vt, dov)
            ds = (p * (dp - dl)).astype(BF16)
            return (dk + _nt(qt, ds), dv + _nt(dov, p.astype(BF16))), _nn(kt, ds)

        def add_dq(i, r):
            dq_acc[i] += 0.125 * r[:hd]
            dqa_ref[i] += r[hd:]

        def block(i, carry, masked):
            carry, r = grads(queries(i), do_ref[i], lse_ref[i], dl_ref[i, pl.ds(h, 1), :], carry, masked)
            add_dq(i, r)
            return carry

        def two_blocks(t, carry):
            i = j + 1 + 2 * t
            both = lambda f: jnp.concatenate([f(i), f(i + 1)], axis=1)
            carry, r = grads(both(queries), both(lambda n: do_ref[n]), both(lambda n: lse_ref[n]),
                             both(lambda n: dl_ref[n, pl.ds(h, 1), :]), carry, False)
            add_dq(i, r[:, :B])
            add_dq(i + 1, r[:, B:])
            return carry

        last = im_ref[h * nb + j]
        pairs = (last - j) // 2
        carry = block(j, (jnp.zeros((hd + AUG, B), F32), jnp.zeros((hd, B), F32)), True)
        carry = lax.fori_loop(0, pairs, two_blocks, carry)
        dk, dv = lax.fori_loop(j + 1 + 2 * pairs, last + 1, lambda i, c: block(i, c, False), carry)
        dk_ref[...] = dk[:hd].astype(BF16)
        dka_ref[...] = dk[hd:]
        dv_ref[...] = dv.astype(BF16)

        @pl.when(j == nb - 1)
        def _():
            dq_ref[...] = dq_acc[...].astype(BF16)

    head_rows = lambda off: pl.BlockSpec((nb, hd, B), lambda h, j, im: (0, off + h, 0))
    key_rows = lambda off: pl.BlockSpec((None, hd, B), lambda h, j, im: (j, off + h, 0))
    aug_all = pl.BlockSpec((None, nb, AUG, B), lambda h, j, im: (h, 0, 0, 0))
    aug_one = pl.BlockSpec((None, None, AUG, B), lambda h, j, im: (h, j, 0, 0))
    grid_spec = pltpu.PrefetchScalarGridSpec(
        num_scalar_prefetch=1, grid=(H, nb),
        in_specs=[key_rows(H), key_rows(2 * H), aug_one, head_rows(0), aug_all, head_rows(0),
                  pl.BlockSpec((None, nb, 1, B), lambda h, j, im: (h, 0, 0, 0)),
                  pl.BlockSpec((nb, H, B), lambda h, j, im: (0, 0, 0))],
        out_specs=[head_rows(0), aug_all, key_rows(0), aug_one, key_rows(0)],
        scratch_shapes=[pltpu.VMEM((nb, hd, B), F32)])
    return pl.pallas_call(
        body, name=name, grid_spec=grid_spec,
        out_shape=[SDS((nb, aw, B), BF16), SDS((H, nb, AUG, B), F32), SDS((nb, aw, B), BF16),
                   SDS((H, nb, AUG, B), F32), SDS((nb, aw, B), BF16)],
        compiler_params=_cp("arbitrary", "arbitrary"))(imax, zat, zat, ka, zat, qa, dot, lse, delta)


def _mix_dwt(dq, dk, dv, dzft, h, name):
    nb, aw, B = dq.shape
    D = h.shape[1]

    def body(dq_ref, dk_ref, dv_ref, dzf_ref, h_ref, o_ref, of_ref):
        first = pl.program_id(0) == 0
        hv = h_ref[...]
        part = jnp.concatenate([_nn(r[...], hv) for r in (dq_ref, dk_ref, dv_ref)], axis=0)
        _accumulate(o_ref, first, part)
        _accumulate(of_ref, first, _nn(dzf_ref[...].astype(BF16), hv))

    blk = pl.BlockSpec((None, aw, B), lambda i: (i, 0, 0))
    return pl.pallas_call(
        body, name=name, grid=(nb,),
        in_specs=[blk, blk, blk, pl.BlockSpec((SUBLANES, B), lambda i: (0, i)), pl.BlockSpec((B, D), lambda i: (i, 0))],
        out_specs=[pl.BlockSpec((3 * aw, D), lambda i: (0, 0)), pl.BlockSpec((SUBLANES, D), lambda i: (0, 0))],
        out_shape=[SDS((3 * aw, D), F32), SDS((SUBLANES, D), F32)],
        compiler_params=_cp("arbitrary"))(dq, dk, dv, dzft, h)


def _mix_bwd_in(dzc, dq, dk, dv, dzft, wc, wat, wft, x, g, dres, name, scatter):
    T, D = x.shape
    nb, aw, B = dq.shape

    def body(dzc_ref, dq_ref, dk_ref, dv_ref, dzf_ref, wc_ref, wa_ref, wf_ref, x_ref, g_ref, r_ref,
             dx_ref, dxb_ref, dg_ref):
        dh = _nn(dzc_ref[...], wc_ref[...])
        for s, r in enumerate((dq_ref, dk_ref, dv_ref)):
            dh = dh + _tn(r[...], wa_ref[s * aw:(s + 1) * aw, :])
        dh = dh + _tn(dzf_ref[...].astype(BF16), wf_ref[...])
        _norm_bwd_store(dh, x_ref, g_ref, r_ref, dx_ref, dxb_ref, dg_ref, pl.program_id(0) == 0)

    blk = pl.BlockSpec((None, aw, B), lambda i: (i, 0, 0))
    row = lambda n: pl.BlockSpec((B, n), lambda i: (i, 0))
    return _hosted_call(
        body, name, nb,
        [row(dzc.shape[1]), blk, blk, blk, pl.BlockSpec((SUBLANES, B), lambda i: (0, i)),
         _full(wc), _full(wat), _full(wft), row(D), _full(g), row(D)],
        [row(D), row(D), pl.BlockSpec((1, D), lambda i: (0, 0))],
        [SDS((T, D), F32), SDS((T, D), BF16), SDS((1, D), F32)],
        (dzc, dq, dk, dv, dzft, wc, wat, wft, x, g, dres), scatter)


def _mem_kv(mem, g, wkv, name):
    M, D = mem.shape
    dx = wkv.shape[2]

    def body(m_ref, g_ref, w_ref, kv_ref, h_ref):
        mv = m_ref[...]
        h = (mv * _rstd(mv) * g_ref[...]).astype(BF16)
        h_ref[...] = h
        for s in range(N_DEV):
            kv_ref[s] = _nn(h, w_ref[s]).astype(BF16)

    return pl.pallas_call(
        body, name=name, grid=(1,), in_specs=[_full(mem), _full(g), _full(wkv)],
        out_specs=[pl.BlockSpec((N_DEV, M, dx), lambda i: (0, 0, 0)), pl.BlockSpec((M, D), lambda i: (0, 0))],
        out_shape=[SDS((N_DEV, M, dx), BF16), SDS((M, D), BF16)], compiler_params=_cp("arbitrary"))(mem, g, wkv)


def _mem_kv_bwd(dkv, hm, wkv, mem, g, name):
    M, D = mem.shape
    dx = wkv.shape[2]

    def body(dkv_ref, h_ref, w_ref, m_ref, g_ref, dw_ref, dg_ref):
        hv = h_ref[...]
        dh = jnp.zeros((M, D), F32)
        for s in range(N_DEV):
            d = dkv_ref[s].astype(BF16)
            dw_ref[s] = _tn(hv, d).astype(BF16)
            dh = dh + _nt(d, w_ref[s])
        mv = m_ref[...]
        dg_ref[...] = jnp.sum(dh * (mv * _rstd(mv)), axis=0, keepdims=True)

    return pl.pallas_call(
        body, name=name, grid=(1,), in_specs=[_full(dkv), _full(hm), _full(wkv), _full(mem), _full(g)],
        out_specs=[pl.BlockSpec((N_DEV, D, dx), lambda i: (0, 0, 0)), pl.BlockSpec((1, D), lambda i: (0, 0))],
        out_shape=[SDS((N_DEV, D, dx), BF16), SDS((1, D), F32)], compiler_params=_cp("arbitrary"))(dkv, hm, wkv, mem, g)


def _xattn_probs(q_ref, kv_ref, h, dx, scale):
    qh = q_ref[:, h * dx:(h + 1) * dx]
    kh = kv_ref[h]
    s = _nt(qh, kh) * scale
    p = jnp.exp(s - jnp.max(s, axis=-1, keepdims=True))
    return qh, kh, p / jnp.sum(p, axis=-1, keepdims=True)


def _xattn(x, g, wq, kv, wo, name, tm=512):
    T, D = x.shape
    dx = D // N_XHEADS
    scale = dx ** -0.5
    tm = _tile(T, tm, SUBLANES)

    def body(x_ref, g_ref, wq_ref, kv_ref, wo_ref, xo_ref, h_ref, q_ref, o_ref):
        xv = x_ref[...]
        h = (xv * _rstd(xv) * g_ref[...]).astype(BF16)
        h_ref[...] = h
        q_ref[...] = _nn(h, _rows2d(wq_ref)).astype(BF16)
        for hd in range(N_XHEADS):
            _, _, p = _xattn_probs(q_ref, kv_ref, hd, dx, scale)
            o_ref[:, hd * dx:(hd + 1) * dx] = _nn(p.astype(BF16), kv_ref[N_XHEADS + hd]).astype(BF16)
        xo_ref[...] = xv + _nn(o_ref[...], _rows2d(wo_ref))

    row = pl.BlockSpec((tm, D), lambda i: (i, 0))
    return pl.pallas_call(
        body, name=name, grid=(T // tm,), in_specs=[row, _full(g), _full(wq), _full(kv), _full(wo)],
        out_specs=[row] * 4, out_shape=[SDS((T, D), F32)] + [SDS((T, D), BF16)] * 3,
        compiler_params=_cp("parallel"))(x, g, wq, kv, wo)


def _xattn_bwd(dyb, dres, x, g, q, kv, wq, wo, name, tm=512):
    T, D = x.shape
    dx = D // N_XHEADS
    scale = dx ** -0.5
    tm = _tile(T, tm, SUBLANES)

    def body(dy_ref, r_ref, x_ref, g_ref, q_ref, kv_ref, wq_ref, wo_ref, dx_ref, dxb_ref, dg_ref, dq_ref, dkv_ref):
        first = pl.program_id(0) == 0

        @pl.when(first)
        def _():
            dkv_ref[...] = jnp.zeros_like(dkv_ref)
        do = _nt(dy_ref[...], _rows2d(wo_ref)).astype(BF16)
        for hd in range(N_XHEADS):
            qh, kh, p = _xattn_probs(q_ref, kv_ref, hd, dx, scale)
            doh = do[:, hd * dx:(hd + 1) * dx]
            dp = _nt(doh, kv_ref[N_XHEADS + hd])
            ds = (p * (dp - jnp.sum(p * dp, axis=-1, keepdims=True)) * scale).astype(BF16)
            dq_ref[:, hd * dx:(hd + 1) * dx] = _nn(ds, kh).astype(BF16)
            dkv_ref[hd] += _tn(ds, qh)
            dkv_ref[N_XHEADS + hd] += _tn(p.astype(BF16), doh)
        dh = _nt(dq_ref[...], _rows2d(wq_ref))
        _norm_bwd_store(dh, x_ref, g_ref, r_ref, dx_ref, dxb_ref, dg_ref, first)

    row = pl.BlockSpec((tm, D), lambda i: (i, 0))
    return pl.pallas_call(
        body, name=name, grid=(T // tm,),
        in_specs=[row, row, row, _full(g), row, _full(kv), _full(wq), _full(wo)],
        out_specs=[row, row, pl.BlockSpec((1, D), lambda i: (0, 0)), row, pl.BlockSpec(kv.shape, lambda i: (0, 0, 0))],
        out_shape=[SDS((T, D), F32), SDS((T, D), BF16), SDS((1, D), F32), SDS((T, D), BF16), SDS(kv.shape, F32)],
        compiler_params=_cp("arbitrary"))(dyb, dres, x, g, q, kv, wq, wo)


def _mesh_position():
    return lax.axis_index("x"), lax.axis_index("y"), lax.axis_index("c")


class _GatherCopies:
    def __init__(self, x_refs, out_refs, send_sems, recv_sems, local_sems):
        self.x_refs, self.out_refs, self.n = x_refs, out_refs, len(x_refs)
        self.send_sems, self.recv_sems, self.local_sems = send_sems, recv_sems, local_sems
        x, y, c = _mesh_position()
        self.c = c
        self.me, self.sibling = (x, y, c), (x, y, 1 - c)
        self.chips = [(1 - x, y), (x, 1 - y), (1 - x, 1 - y)]

    def slot(self, w, px, py, pc):
        return self.out_refs[w].at[4 * px + 2 * py + pc]

    def copy(self, w, k, block, to, src=None):
        return pltpu.make_async_remote_copy(
            src_ref=self.slot(w, *block) if src is None else src, dst_ref=self.slot(w, *block),
            send_sem=self.send_sems.at[w, k], recv_sem=self.recv_sems.at[w, k], device_id=to,
            device_id_type=pl.DeviceIdType.MESH)

    def mine(self, w):
        return pltpu.make_async_copy(self.x_refs[w], self.slot(w, *self.me), self.local_sems.at[w])

    def first(self, w):
        src = self.x_refs[w]
        return [self.copy(w, 0, self.me, self.sibling, src=src)] + [
            self.copy(w, 1 + j, self.me, (*chip, self.c), src=src) for j, chip in enumerate(self.chips)]

    def passed(self, w):
        return [self.copy(w, 4 + j, (*chip, self.c), self.sibling) for j, chip in enumerate(self.chips)]


def _gather_start(*refs):
    cp = _GatherCopies(*refs)
    for w in range(cp.n):
        cp.mine(w).start()
        for f in cp.first(w):
            f.start()


def _gather_relay(*refs):
    cp = _GatherCopies(*refs)
    for w in range(cp.n):
        for j, (chip, fwd) in enumerate(zip(cp.chips, cp.passed(w))):
            cp.copy(w, 1 + j, (*chip, cp.c), cp.me).wait_recv()
            fwd.start()


def _gather_finish(*refs):
    cp = _GatherCopies(*refs)
    for w in range(cp.n):
        cp.copy(w, 0, cp.sibling, cp.me).wait_recv()
        for j, chip in enumerate(cp.chips):
            cp.copy(w, 4 + j, (*chip, 1 - cp.c), cp.me).wait_recv()
    for w in range(cp.n):
        for f in cp.first(w) + cp.passed(w):
            f.wait_send()
        cp.mine(w).wait()


def _all_gather(shards, name):
    n = len(shards)

    def body(*refs):
        parts = (refs[:n], refs[n:2 * n]) + refs[2 * n:]
        _gather_start(*parts)
        _gather_relay(*parts)
        _gather_finish(*parts)

    any_spec = pl.BlockSpec(memory_space=pl.ANY)
    return pl.pallas_call(
        body, name=name, out_shape=[SDS((N_DEV,) + s.shape, s.dtype) for s in shards],
        in_specs=[any_spec] * n, out_specs=[any_spec] * n,
        scratch_shapes=[pltpu.SemaphoreType.DMA((n, 7)), pltpu.SemaphoreType.DMA((n, 7)), pltpu.SemaphoreType.DMA((n,))],
    )(*shards)


def _scatter_copies(x_refs, out_refs, send_sems, recv_sems, local_sems):
    x, y, c = _mesh_position()
    me = 4 * x + 2 * y + c
    own, remote = [], []
    for w, (src, dst) in enumerate(zip(x_refs, out_refs)):
        own.append(pltpu.make_async_copy(src.at[me], dst.at[me], local_sems.at[w]))
        for k in range(1, N_DEV):
            px = 1 - x if k & 4 else x
            py = 1 - y if k & 2 else y
            pc = 1 - c if k & 1 else c
            remote.append(pltpu.make_async_remote_copy(
                src_ref=src.at[4 * px + 2 * py + pc], dst_ref=dst.at[me], send_sem=send_sems.at[w, k - 1],
                recv_sem=recv_sems.at[w, k - 1], device_id=(px, py, pc), device_id_type=pl.DeviceIdType.MESH))
    return own, remote


def _scatter_start(x_refs, out_refs, send_sems, recv_sems, local_sems):
    own, remote = _scatter_copies(x_refs, out_refs, send_sems, recv_sems, local_sems)
    for cp in own + remote:
        cp.start()


def _scatter_wait(x_refs, out_refs, send_sems, recv_sems, local_sems):
    own, remote = _scatter_copies(x_refs, out_refs, send_sems, recv_sems, local_sems)
    for cp in remote:
        cp.wait_recv()
    for cp in remote:
        cp.wait_send()
    for cp in own:
        cp.wait()


def _sum_adamw(parts, w, m, v, name, tr=256):
    L, R, C = w.shape
    tr = _tile(R, tr, SUBLANES)
    nblk = R // tr

    def body(*refs):
        p_refs = refs[:L]
        w_ref, m_ref, v_ref, g_ref, d_ref, mo_ref, vo_ref = refs[L:]
        layer = pl.program_id(0)
        g = None
        for k, p_ref in enumerate(p_refs):
            gk = p_ref[0].astype(F32)
            for s in range(1, N_DEV):
                gk = gk + p_ref[s].astype(F32)
            g = gk if g is None else jnp.where(layer == k, gk, g)
        mn = ADAM_B1 * m_ref[...] + (1.0 - ADAM_B1) * g
        vn = ADAM_B2 * v_ref[...] + (1.0 - ADAM_B2) * jnp.square(g)
        m_hat = mn / (1.0 - ADAM_B1 ** ADAM_STEP)
        v_hat = vn / (1.0 - ADAM_B2 ** ADAM_STEP)
        g_ref[...] = g
        d_ref[...] = -ADAM_LR * (m_hat / (jnp.sqrt(v_hat) + ADAM_EPS) + ADAM_WD * w_ref[...])
        mo_ref[...] = mn
        vo_ref[...] = vn

    def part_spec(k):
        return pl.BlockSpec((N_DEV, tr, C),
                            lambda l, i: (0, jnp.where(l == k, i, jnp.where(l < k, 0, nblk - 1)), 0))

    row = pl.BlockSpec((None, tr, C), lambda l, i: (l, i, 0))
    return pl.pallas_call(
        body, name=name, grid=(L, nblk), in_specs=[part_spec(k) for k in range(L)] + [row, row, row],
        out_specs=[row] * 4, out_shape=[SDS((L, R, C), F32)] * 4,
        compiler_params=_cp("arbitrary", "arbitrary"))(*parts, w, m, v)


def _pack(flat_list, row_mult):
    flat = jnp.concatenate(flat_list, axis=-1)
    n = flat.shape[-1]
    chunk = row_mult * PACK_COLS
    pad = -n % chunk
    flat = jnp.pad(flat, [(0, 0)] * (flat.ndim - 1) + [(0, pad)])
    return flat.reshape(flat.shape[:-1] + ((n + pad) // PACK_COLS, PACK_COLS))


def _unpack(packed, shapes):
    lead = packed.shape[:-2]
    flat = packed.reshape(lead + (-1,))
    out = []
    off = 0
    for shp in shapes:
        n = 1
        for d in shp:
            n *= d
        out.append(flat[..., off:off + n].reshape(lead + tuple(shp)))
        off += n
    return out


def _ffn_bwd(dx, dxb, saved, wgu, wd, tag):
    x, g, h, ab, act = saved
    dgu = _ffn_bwd_act(dxb, wd, ab, f"{tag}_bwd_act")
    dwd = _ffn_dwd(act, dxb, f"{tag}_dwd")
    dwgu = _ffn_dwgu(h, dgu, f"{tag}_dwgu")
    D = dwd.shape[-1]
    return _ffn_bwd_in(dgu, wgu, x, g, dx, f"{tag}_bwd_in", [dwgu.reshape(N_DEV, -1, D), dwd.reshape(N_DEV, -1, D)])


def _mix_fwd(x, g, w_in8, w_conv8, b_f, g_conv, g_att, w_out, tag):
    T, D = x.shape
    w_int = w_in8.reshape(-1, D)
    cw = D // 2
    aw = D - cw
    H = aw // HEAD_DIM
    B = _tile(T, ATT_BLOCK, LANES)
    nb = T // B
    w_c = w_int[:3 * cw]
    w_at = w_int[3 * cw:3 * cw + 3 * aw]
    w_ft = jnp.pad(w_int[3 * cw + 3 * aw:], ((0, SUBLANES - H), (0, 0)))
    zc, h, ync = _mix_in_conv(x, g, w_c, w_conv8, g_conv, f"{tag}_in_conv")
    zat, zft8, norms = _proj_t(h, w_at, w_ft, B, f"{tag}_in_att")
    zft = zft8[:H]
    bcol = b_f.reshape(H, 1)
    c, qa, ka = _logf_cumsum(zft, bcol, B, f"{tag}_cumsum")
    cb = c.reshape(H, nb, B)
    jmin, imax = _skip_table(cb[:, :, 0:1], cb[:, :, B - 1].reshape(H, 1, nb), norms, f"{tag}_skip")
    ot, lse = _attn_fwd(jmin.reshape(H * nb), zat, qa, ka, f"{tag}_attn")
    g_col = g_att.reshape(aw, 1)
    yna = _gnorm_t_fwd(ot, g_col, f"{tag}_gnorm")
    xo = _mm_res([ync, yna], w_out, x, f"{tag}_out")
    saved = (x, g, h, zc, zft, bcol, zat, qa, ka, imax.reshape(H * nb), ot, lse, ync, yna, w_c, w_at, w_ft, g_col)
    return xo, saved


def _mix_bwd(dx, dxb, saved, w_conv8, g_conv, w_out, tag, also_scatter):
    x, g, h, zc, zft, bcol, zat, qa, ka, imax, ot, lse, ync, yna, w_c, w_at, w_ft, g_col = saved
    T, D = x.shape
    H = zft.shape[0]
    dw_out = jnp.concatenate([_mm_tn(ync, dxb, f"{tag}_dwout_c"), _mm_tn(yna, dxb, f"{tag}_dwout_a")], axis=0)
    dzc, dwc8, dg_conv, dot, delta, dg_att, *p_also = _mix_bwd_head(dxb, w_out, zc, w_conv8, g_conv, ot, g_col,
                                                                   f"{tag}_bwd_out", also_scatter)
    dq, dqa, dk, dka, dv = _attn_bwd(imax, zat, qa, ka, dot, lse, delta, f"{tag}_attn_bwd")
    dcq = dqa[:, :, 3, :].reshape(H, T)
    dck = dka[:, :, 0, :].reshape(H, T)
    dzft, db = _logf_cumsum_bwd(dcq, dck, zft, bcol, f"{tag}_cumsum_bwd")
    dzft8 = jnp.pad(dzft, ((0, SUBLANES - H), (0, 0)))
    dw_c = _mm_tn(dzc, h, f"{tag}_dwin_c")
    dw_at, dw_ft = _mix_dwt(dq, dk, dv, dzft8, h, f"{tag}_dwin_a")
    dw_int = jnp.concatenate([dw_c, dw_at.astype(BF16), dw_ft[:H].astype(BF16)], axis=0)
    scatter = [dw_int.reshape(N_DEV, -1, D), dwc8[:3].reshape(3, N_DEV, -1).transpose(1, 0, 2),
               dw_out.reshape(N_DEV, D // N_DEV, D)]
    dx, dxb, dg, p_in, p_conv, p_out = _mix_bwd_in(dzc, dq, dk, dv, dzft8, w_c, w_at, w_ft, x, g, dx, f"{tag}_bwd_in",
                                                   scatter)
    return dx, dxb, dg, p_in, p_conv, db.reshape(H), dg_conv, dg_att[:, 0], p_out, p_also


def _xattn_block_fwd(x, g, mem, g_mem, w_q, w_kv, w_o, tag):
    kv, hm = _mem_kv(mem, g_mem, w_kv, f"{tag}_kv")
    xo, h, q, o = _xattn(x, g, w_q, kv, w_o, f"{tag}_block")
    return xo, (x, g, h, mem, g_mem, hm, q, kv, o)


def _xattn_block_bwd(dx, dxb, saved, w_q, w_kv, w_o, tag):
    x, g, h, mem, g_mem, hm, q, kv, o = saved
    dw_o = _mm_tn(o, dxb, f"{tag}_dwo")
    dx, dxb, dg, dq, dkv = _xattn_bwd(dxb, dx, x, g, q, kv, w_q, w_o, f"{tag}_block_bwd")
    dw_q = _mm_tn(h, dq, f"{tag}_dwq")
    dw_kv, dg_mem = _mem_kv_bwd(dkv, hm, w_kv, mem, g_mem, f"{tag}_kv_bwd")
    D = dw_q.shape[0]
    return dx, dxb, dg, dg_mem, [dw_q.reshape(N_DEV, D // N_DEV, D), dw_kv, dw_o.reshape(N_DEV, D // N_DEV, D)]


GATHER_FIRST = ['w_ffn1_gu']
GATHER_REST = ['w_ffn1_down', 'w_mix_in', 'w_mix_out', 'w_xq', 'w_xkv', 'w_xo', 'w_ffn2_gu', 'w_ffn2_down']
GATHER_NEXT_A = ['w_ffn1_gu', 'w_ffn1_down', 'w_mix_in', 'w_mix_out', 'w_xq', 'w_xkv', 'w_xo']
GATHER_NEXT_B = ['w_ffn2_gu', 'w_ffn2_down']


def _local_step(x, mem, tgt, P):
    L = P['g_ffn1'].shape[0]
    shards = lambda names, l: [(P[n][l].T if n in LONG_AXIS_LAST else P[n][l]).astype(BF16) for n in names]
    wgut = lambda n, l: W[n][l].reshape(-1, W[n][l].shape[-1])
    W = {n: [None] * L for n in SHARDED}

    def keep(names, l, arrays):
        for n, a in zip(names, arrays):
            W[n][l] = a

    *first, conv = _all_gather(shards(GATHER_FIRST, 0) + [P['w_conv']], "gather_first")
    keep(GATHER_FIRST, 0, first)
    w_conv = conv.transpose(1, 2, 0, 3).reshape(L, 3, -1)
    saved = []
    for l in range(L):
        row = lambda n: P[n][l][None, :]
        wc8 = jnp.pad(w_conv[l], ((0, SUBLANES - 3), (0, 0)))
        x1 = x
        bring = GATHER_REST if l == 0 else []
        ab, act, h, *got = _ffn_gu(x, row('g_ffn1'), wgut('w_ffn1_gu', l), f"l{l}_ffn1_gu", gather=shards(bring, l))
        keep(bring, l, got)
        x, = _ffn_down(act, W['w_ffn1_down'][l], x, f"l{l}_ffn1_down")
        s1 = (x1, row('g_ffn1'), h, ab, act)
        x, s2 = _mix_fwd(x, row('g_mix'), W['w_mix_in'][l], wc8, P['b_f'][l], row('g_conv_out'), P['g_att_out'][l],
                         W['w_mix_out'][l], f"l{l}_mix")
        x, s3 = _xattn_block_fwd(x, row('g_xattn'), mem, row('g_mem'), W['w_xq'][l], W['w_xkv'][l], W['w_xo'][l],
                                 f"l{l}_xattn")
        x4 = x
        bring_a, bring_b = (GATHER_NEXT_A, GATHER_NEXT_B) if l + 1 < L else ([], [])
        ab, act, h, *got = _ffn_gu(x, row('g_ffn2'), wgut('w_ffn2_gu', l), f"l{l}_ffn2_gu", gather=shards(bring_a, l + 1))
        keep(bring_a, l + 1, got)
        x, *got = _ffn_down(act, W['w_ffn2_down'][l], x, f"l{l}_ffn2_down", gather=shards(bring_b, l + 1))
        keep(bring_b, l + 1, got)
        saved.append((s1, s2, s3, (x4, row('g_ffn2'), h, ab, act), wc8))
    loss, dx, dxb, dg_final = _loss_head(x, P['g_final'][None, :], tgt, "loss_head")
    G = {n: [None] * L for n in WEIGHT_NAMES if n != 'g_final'}
    for l in reversed(range(L)):
        row = lambda n: P[n][l][None, :]
        s1, s2, s3, s4, wc8 = saved[l]
        dx, dxb, G['g_ffn2'][l], G['w_ffn2_gu'][l], G['w_ffn2_down'][l] = _ffn_bwd(
            dx, dxb, s4, wgut('w_ffn2_gu', l), W['w_ffn2_down'][l], f"l{l}_ffn2")
        dx, dxb, G['g_xattn'][l], G['g_mem'][l], xattn_grads = _xattn_block_bwd(
            dx, dxb, s3, W['w_xq'][l], W['w_xkv'][l], W['w_xo'][l], f"l{l}_xattn")
        (dx, dxb, G['g_mix'][l], G['w_mix_in'][l], G['w_conv'][l], G['b_f'][l], G['g_conv_out'][l], G['g_att_out'][l],
         G['w_mix_out'][l], (G['w_xq'][l], G['w_xkv'][l], G['w_xo'][l])) = _mix_bwd(
            dx, dxb, s2, wc8, row('g_conv_out'), W['w_mix_out'][l], f"l{l}_mix", xattn_grads)
        dx, dxb, G['g_ffn1'][l], G['w_ffn1_gu'][l], G['w_ffn1_down'][l] = _ffn_bwd(
            dx, dxb, s1, wgut('w_ffn1_gu', l), W['w_ffn1_down'][l], f"l{l}_ffn1")
    G['g_final'] = [dg_final]
    return loss, dx, G


def kernel(x, mem, g_ffn1, w_ffn1_gu, w_ffn1_down, g_mix, w_mix_in, w_conv, b_f, g_conv_out, g_att_out, w_mix_out, g_xattn, g_mem, w_xq, w_xkv, w_xo, g_ffn2, w_ffn2_gu, w_ffn2_down, g_final, loss_target, m_g_ffn1, m_w_ffn1_gu, m_w_ffn1_down, m_g_mix, m_w_mix_in, m_w_conv, m_b_f, m_g_conv_out, m_g_att_out, m_w_mix_out, m_g_xattn, m_g_mem, m_w_xq, m_w_xkv, m_w_xo, m_g_ffn2, m_w_ffn2_gu, m_w_ffn2_down, m_g_final, v_g_ffn1, v_w_ffn1_gu, v_w_ffn1_down, v_g_mix, v_w_mix_in, v_w_conv, v_b_f, v_g_conv_out, v_g_att_out, v_w_mix_out, v_g_xattn, v_g_mem, v_w_xq, v_w_xkv, v_w_xo, v_g_ffn2, v_w_ffn2_gu, v_w_ffn2_down, v_g_final):
    args = (x, mem, g_ffn1, w_ffn1_gu, w_ffn1_down, g_mix, w_mix_in, w_conv, b_f, g_conv_out, g_att_out, w_mix_out,
            g_xattn, g_mem, w_xq, w_xkv, w_xo, g_ffn2, w_ffn2_gu, w_ffn2_down, g_final)
    P = dict(zip(IN_NAMES, args))
    moms = (m_g_ffn1, m_w_ffn1_gu, m_w_ffn1_down, m_g_mix, m_w_mix_in, m_w_conv, m_b_f, m_g_conv_out, m_g_att_out,
            m_w_mix_out, m_g_xattn, m_g_mem, m_w_xq, m_w_xkv, m_w_xo, m_g_ffn2, m_w_ffn2_gu, m_w_ffn2_down, m_g_final)
    vars_ = (v_g_ffn1, v_w_ffn1_gu, v_w_ffn1_down, v_g_mix, v_w_mix_in, v_w_conv, v_b_f, v_g_conv_out, v_g_att_out,
             v_w_mix_out, v_g_xattn, v_g_mem, v_w_xq, v_w_xkv, v_w_xo, v_g_ffn2, v_w_ffn2_gu, v_w_ffn2_down, v_g_final)
    MOM = dict(zip(WEIGHT_NAMES, moms))
    VAR = dict(zip(WEIGHT_NAMES, vars_))

    loss_part, dx, grads = _local_step(x[0], mem[0], loss_target[0], {n: P[n] for n in WEIGHT_NAMES})

    small_shapes = [P[n].shape for n in REPLICATED] + [(1,)]
    small = _pack([a.reshape(-1) for n in REPLICATED for a in grads[n]] + [loss_part[0, :1]], SUBLANES)
    small_parts = _all_gather([small], "gather_small_grads")[0]

    res = {}
    for n in SHARDED:
        t = (lambda a: a.transpose(0, 2, 1)) if n in LONG_AXIS_LAST else (lambda a: a)
        res[n] = [t(o) for o in _sum_adamw(grads[n], t(P[n]), t(MOM[n]), t(VAR[n]), "adamw_" + n)]
    zero = [jnp.zeros((1,), F32)]
    w_, m_, v_ = (_pack([d[n].reshape(-1) for n in REPLICATED] + zero, SUBLANES)[None] for d in (P, MOM, VAR))
    small_out = [_unpack(o[0], small_shapes) for o in _sum_adamw([small_parts], w_, m_, v_, "adamw_vectors")]
    for k, n in enumerate(REPLICATED):
        res[n] = [o[k] for o in small_out]
    out = []
    for k in range(4):
        out += [res[n][k] for n in WEIGHT_NAMES]
    return (small_out[0][-1].reshape(()), dx[None], *out)
```

```python
import jax
import jax.numpy as jnp
from jax import lax
from jax.experimental import pallas as pl
from jax.experimental.pallas import tpu as pltpu

F32 = jnp.float32
BF16 = jnp.bfloat16
I32 = jnp.int32
SDS = jax.ShapeDtypeStruct

EPS = 1e-6
HEAD_DIM = 64
N_XHEADS = 4
N_DEV = 8
LANES = 128
SUBLANES = 8
MXU_COLS = 256
AUG = 16
ATT_BLOCK = 512
NEG = -1e30
SKIP_MARGIN = 115.0
PACK_COLS = 1024
VMEM_LIMIT = 58 * 1024 * 1024

ADAM_LR = 0.001
ADAM_B1 = 0.9
ADAM_B2 = 0.999
ADAM_EPS = 1e-08
ADAM_WD = 0.01
ADAM_STEP = 10

IN_NAMES = ['x', 'mem', 'g_ffn1', 'w_ffn1_gu', 'w_ffn1_down', 'g_mix', 'w_mix_in', 'w_conv', 'b_f', 'g_conv_out',
            'g_att_out', 'w_mix_out', 'g_xattn', 'g_mem', 'w_xq', 'w_xkv', 'w_xo', 'g_ffn2', 'w_ffn2_gu',
            'w_ffn2_down', 'g_final']
WEIGHT_NAMES = IN_NAMES[2:]
SHARDED = ['w_ffn1_gu', 'w_ffn1_down', 'w_mix_in', 'w_conv', 'w_mix_out', 'w_xq', 'w_xkv', 'w_xo', 'w_ffn2_gu',
           'w_ffn2_down']
REPLICATED = [n for n in WEIGHT_NAMES if n not in SHARDED]
LONG_AXIS_LAST = ['w_ffn1_gu', 'w_ffn2_gu', 'w_mix_in']


def _tile(n, pref, mult):
    t = min(pref, n) // mult * mult
    while t >= mult:
        if n % t == 0:
            return t
        t -= mult
    return n


def _cp(*sem):
    return pltpu.CompilerParams(dimension_semantics=sem, vmem_limit_bytes=VMEM_LIMIT)


def _nt(a, b):
    return lax.dot_general(a, b, (((1,), (1,)), ((), ())), preferred_element_type=F32)


def _tn(a, b):
    return lax.dot_general(a, b, (((0,), (0,)), ((), ())), preferred_element_type=F32)


def _nn(a, b):
    return jnp.dot(a, b, preferred_element_type=F32)


def _rstd(xv):
    return lax.rsqrt(jnp.mean(xv * xv, axis=-1, keepdims=True) + EPS)


def _full(a):
    nd = a.ndim
    return pl.BlockSpec(a.shape, lambda *_: (0,) * nd, pipeline_mode=pl.Buffered(1))


def _rows2d(w_ref):
    s, a, b = w_ref.shape
    return w_ref[...].reshape(s * a, b)


def _accumulate(ref, first, part):
    @pl.when(first)
    def _():
        ref[...] = part

    @pl.when(jnp.logical_not(first))
    def _():
        ref[...] += part


def _norm_bwd_store(dh, x_ref, g_ref, r_ref, dx_ref, dxb_ref, dg_ref, first):
    xv = x_ref[...]
    r = _rstd(xv)
    xh = xv * r
    dxh = dh * g_ref[...]
    dx = r * (dxh - xh * jnp.mean(dxh * xh, axis=-1, keepdims=True)) + r_ref[...]
    dx_ref[...] = dx
    dxb_ref[...] = dx.astype(BF16)
    _accumulate(dg_ref, first, jnp.sum(dh * xh, axis=0, keepdims=True))


def _mm_res(a_list, w, res, name, tm=512):
    T = res.shape[0]
    N = w.shape[-1]
    tm = _tile(T, tm, SUBLANES)
    n_a = len(a_list)

    def body(*refs):
        a_refs = refs[:n_a]
        w_ref, r_ref, o_ref = refs[n_a:]
        wv = _rows2d(w_ref)
        acc = r_ref[...]
        off = 0
        for a_ref in a_refs:
            k = a_ref.shape[1]
            acc = acc + _nn(a_ref[...], wv[off:off + k, :])
            off += k
        o_ref[...] = acc

    row = lambda n: pl.BlockSpec((tm, n), lambda i: (i, 0))
    return pl.pallas_call(
        body, name=name, grid=(T // tm,),
        in_specs=[row(a.shape[1]) for a in a_list] + [_full(w), row(N)], out_specs=row(N),
        out_shape=SDS((T, N), F32), compiler_params=_cp("parallel"))(*a_list, w, res)


def _mm_tn(a, b, name, out_dtype=BF16, tt=2048):
    T, K = a.shape
    N = b.shape[1]
    tt = _tile(T, tt, 2 * SUBLANES)
    nt = T // tt

    def body(a_ref, b_ref, o_ref, acc_ref):
        t = pl.program_id(0)
        _accumulate(acc_ref, t == 0, _tn(a_ref[...], b_ref[...]))

        @pl.when(t == nt - 1)
        def _():
            o_ref[...] = acc_ref[...].astype(o_ref.dtype)

    return pl.pallas_call(
        body, name=name, grid=(nt,),
        in_specs=[pl.BlockSpec((tt, K), lambda t: (t, 0)), pl.BlockSpec((tt, N), lambda t: (t, 0))],
        out_specs=pl.BlockSpec((K, N), lambda t: (0, 0)), out_shape=SDS((K, N), out_dtype),
        scratch_shapes=[pltpu.VMEM((K, N), F32)], compiler_params=_cp("arbitrary"))(a, b)


def _hosted_call(body, name, n_steps, in_specs, out_specs, out_shape, args, scatter=(), gather=(), scratch_shapes=()):
    scatter, gather, scratch_shapes = list(scatter), list(gather), list(scratch_shapes)
    n_s, n_g = len(scatter), len(gather)
    if n_s + n_g == 0:
        return pl.pallas_call(body, name=name, grid=(n_steps,), in_specs=list(in_specs), out_specs=list(out_specs),
                              out_shape=list(out_shape), scratch_shapes=scratch_shapes,
                              compiler_params=_cp("arbitrary"))(*args)
    n_in, n_out, n_scr = len(in_specs), len(out_specs), len(scratch_shapes)
    relay_step = min(max(1, 3 * n_steps // 4), n_steps - 1)

    def wrapped(*refs):
        ins, refs = refs[:n_in], refs[n_in:]
        xs, shards, refs = refs[:n_s], refs[n_s:n_s + n_g], refs[n_s + n_g:]
        outs, refs = refs[:n_out], refs[n_out:]
        recvs, gathered, refs = refs[:n_s], refs[n_s:n_s + n_g], refs[n_s + n_g:]
        outs, sems = outs + refs[:n_scr], refs[n_scr:]
        s_sems, g_sems = (sems[:3], sems[3:]) if n_s else ((), sems)
        step = pl.program_id(0)

        @pl.when(step == 0)
        def _():
            if n_s:
                _scatter_start(xs, recvs, *s_sems)
            if n_g:
                _gather_start(shards, gathered, *g_sems)
        body(*ins, *outs)
        if n_g:
            @pl.when(step == relay_step)
            def _():
                _gather_relay(shards, gathered, *g_sems)

        @pl.when(step == n_steps - 1)
        def _():
            if n_s:
                _scatter_wait(xs, recvs, *s_sems)
            if n_g:
                _gather_finish(shards, gathered, *g_sems)

    any_spec = pl.BlockSpec(memory_space=pl.ANY)
    sems = []
    for n in (n_s, n_g):
        if n:
            sems += [pltpu.SemaphoreType.DMA((n, N_DEV - 1)), pltpu.SemaphoreType.DMA((n, N_DEV - 1)),
                     pltpu.SemaphoreType.DMA((n,))]
    return pl.pallas_call(
        wrapped, name=name, grid=(n_steps,), in_specs=list(in_specs) + [any_spec] * (n_s + n_g),
        out_specs=list(out_specs) + [any_spec] * (n_s + n_g),
        out_shape=list(out_shape) + [SDS(b.shape, b.dtype) for b in scatter]
        + [SDS((N_DEV,) + b.shape, b.dtype) for b in gather],
        scratch_shapes=scratch_shapes + sems, compiler_params=_cp("arbitrary"))(*args, *scatter, *gather)


def _hidden_chunks(F):
    edges = list(range(0, F, 3 * MXU_COLS)) + [F]
    return list(zip(edges[:-1], edges[1:]))


def _ffn_gu(x, g, wgut, name, gather=(), tm=512):
    T, D = x.shape
    F = wgut.shape[0] // 2
    tm = _tile(T, tm, SUBLANES)

    def body(x_ref, g_ref, w_ref, ab_ref, act_ref, h_ref):
        xv = x_ref[...]
        h = (xv * _rstd(xv) * g_ref[...]).astype(BF16)
        h_ref[...] = h
        for c0, c1 in _hidden_chunks(F):
            gt = _nt(h, w_ref[c0:c1, :])
            up = _nt(h, w_ref[F + c0:F + c1, :])
            sg = jax.nn.sigmoid(gt)
            silu = gt * sg
            ab_ref[0, :, c0:c1] = (0.5 * up * (sg * (1.0 + gt * (1.0 - sg)))).astype(BF16)
            ab_ref[1, :, c0:c1] = (0.5 * silu).astype(BF16)
            act_ref[:, c0:c1] = (silu * up).astype(BF16)

    row = lambda n: pl.BlockSpec((tm, n), lambda i: (i, 0))
    return _hosted_call(
        body, name, T // tm, [row(D), _full(g), _full(wgut)],
        [pl.BlockSpec((2, tm, F), lambda i: (0, i, 0)), row(F), row(D)],
        [SDS((2, T, F), BF16), SDS((T, F), BF16), SDS((T, D), BF16)], (x, g, wgut), gather=gather)


def _ffn_down(act, wd, res, name, gather=(), tm=1024):
    T, F = act.shape
    D = wd.shape[2]
    tm = _tile(T, tm, SUBLANES)

    def body(a_ref, w_ref, r_ref, o_ref):
        o_ref[...] = r_ref[...] + 0.5 * _nn(a_ref[...], _rows2d(w_ref))

    row = lambda n: pl.BlockSpec((tm, n), lambda i: (i, 0))
    return _hosted_call(body, name, T // tm, [row(F), _full(wd), row(D)], [row(D)], [SDS((T, D), F32)],
                        (act, wd, res), gather=gather)


def _ffn_bwd_act(dyb, wd, ab, name, tm=512):
    T, D = dyb.shape
    _, _, F = ab.shape
    tm = _tile(T, tm, SUBLANES)

    def body(dy_ref, w_ref, ab_ref, o_ref):
        wv = _rows2d(w_ref)
        dy = dy_ref[...]
        for c0, c1 in _hidden_chunks(F):
            da = _nt(dy, wv[c0:c1, :])
            o_ref[0, :, c0:c1] = (da * ab_ref[0, :, c0:c1].astype(F32)).astype(BF16)
            o_ref[1, :, c0:c1] = (da * ab_ref[1, :, c0:c1].astype(F32)).astype(BF16)

    blocks = pl.BlockSpec((2, tm, F), lambda i: (0, i, 0))
    return pl.pallas_call(
        body, name=name, grid=(T // tm,),
        in_specs=[pl.BlockSpec((tm, D), lambda i: (i, 0)), _full(wd), blocks], out_specs=blocks,
        out_shape=SDS((2, T, F), BF16), compiler_params=_cp("parallel"))(dyb, wd, ab)


def _ffn_dwd(act, dyb, name, tt=2048):
    T, F = act.shape
    D = dyb.shape[1]
    tt = _tile(T, tt, 2 * SUBLANES)
    nt = T // tt
    fh = F // 2

    def body(a_ref, b_ref, o_ref, acc_ref):
        t = pl.program_id(1)
        _accumulate(acc_ref, t == 0, _tn(a_ref[...], b_ref[...]))

        @pl.when(t == nt - 1)
        def _():
            o_ref[...] = (0.5 * acc_ref[...]).astype(BF16)

    return pl.pallas_call(
        body, name=name, grid=(2, nt),
        in_specs=[pl.BlockSpec((tt, fh), lambda j, t: (t, j)), pl.BlockSpec((tt, D), lambda j, t: (t, 0))],
        out_specs=pl.BlockSpec((fh, D), lambda j, t: (j, 0)), out_shape=SDS((F, D), BF16),
        scratch_shapes=[pltpu.VMEM((fh, D), F32)], compiler_params=_cp("parallel", "arbitrary"))(act, dyb)


def _ffn_dwgu(h, dgu, name, tt=2048):
    T, D = h.shape
    _, _, F = dgu.shape
    tt = _tile(T, tt, 2 * SUBLANES)
    nt = T // tt
    fh = F // 2

    def body(a_ref, b_ref, o_ref, acc_ref):
        t = pl.program_id(1)
        _accumulate(acc_ref, t == 0, _tn(b_ref[...], a_ref[...]))

        @pl.when(t == nt - 1)
        def _():
            o_ref[...] = acc_ref[...].astype(BF16)

    return pl.pallas_call(
        body, name=name, grid=(4, nt),
        in_specs=[pl.BlockSpec((tt, D), lambda q, t: (t, 0)),
                  pl.BlockSpec((None, tt, fh), lambda q, t: (q // 2, t, q % 2))],
        out_specs=pl.BlockSpec((fh, D), lambda q, t: (q, 0)), out_shape=SDS((2 * F, D), BF16),
        scratch_shapes=[pltpu.VMEM((fh, D), F32)], compiler_params=_cp("parallel", "arbitrary"))(h, dgu)


def _ffn_bwd_in(wgut, x, g, dres, name, scatter, dgu=None, act_bwd=None):
    T, D = x.shape
    F = wgut.shape[0] // 2
    fused = dgu is None
    tm = _tile(T, 256 if fused else 512, SUBLANES)

    def body(*refs):
        if fused:
            dy_ref, wd_ref, ab_ref, w_ref, x_ref, g_ref, r_ref, dgu_ref, dx_ref, dxb_ref, dg_ref = refs
            wdv = _rows2d(wd_ref)
            dy = dy_ref[...]
            dh = jnp.zeros((tm, D), F32)
            for c0, c1 in _hidden_chunks(F):
                da = _nt(dy, wdv[c0:c1, :])
                for k in range(2):
                    d = (da * ab_ref[k, :, c0:c1].astype(F32)).astype(BF16)
                    dgu_ref[k, :, c0:c1] = d
                    dh = dh + _nn(d, w_ref[k * F + c0:k * F + c1, :])
        else:
            a_ref, w_ref, x_ref, g_ref, r_ref, dx_ref, dxb_ref, dg_ref = refs
            dh = _nn(a_ref[0], w_ref[:F, :]) + _nn(a_ref[1], w_ref[F:, :])
        _norm_bwd_store(dh, x_ref, g_ref, r_ref, dx_ref, dxb_ref, dg_ref, pl.program_id(0) == 0)

    row = pl.BlockSpec((tm, D), lambda i: (i, 0))
    hidden = pl.BlockSpec((2, tm, F), lambda i: (0, i, 0))
    outs = [row, row, pl.BlockSpec((1, D), lambda i: (0, 0))]
    shapes = [SDS((T, D), F32), SDS((T, D), BF16), SDS((1, D), F32)]
    if fused:
        dyb, wd, ab = act_bwd
        return _hosted_call(body, name, T // tm, [row, _full(wd), hidden, _full(wgut), row, _full(g), row],
                            [hidden] + outs, [SDS((2, T, F), BF16)] + shapes, (dyb, wd, ab, wgut, x, g, dres), scatter)
    return _hosted_call(body, name, T // tm, [hidden, _full(wgut), row, _full(g), row], outs, shapes,
                        (dgu, wgut, x, g, dres), scatter)


def _loss_head(x, g, tgt, name, tm=512):
    T, D = x.shape
    tm = _tile(T, tm, SUBLANES)

    def body(x_ref, g_ref, t_ref, loss_ref, dx_ref, dxb_ref, dg_ref):
        first = pl.program_id(0) == 0
        xv = x_ref[...]
        r = _rstd(xv)
        xh = xv * r
        e = xh * g_ref[...] - t_ref[...]
        part = 0.5 * jnp.sum(jnp.mean(e * e, axis=-1, keepdims=True), axis=0, keepdims=True)
        dy = e * (1.0 / D)
        dxh = dy * g_ref[...]
        dx = r * (dxh - xh * jnp.mean(dxh * xh, axis=-1, keepdims=True))
        dx_ref[...] = dx
        dxb_ref[...] = dx.astype(BF16)
        _accumulate(loss_ref, first, jnp.broadcast_to(part, loss_ref.shape))
        _accumulate(dg_ref, first, jnp.sum(dy * xh, axis=0, keepdims=True))

    row = pl.BlockSpec((tm, D), lambda i: (i, 0))
    vec = pl.BlockSpec((1, D), lambda i: (0, 0))
    return pl.pallas_call(
        body, name=name, grid=(T // tm,), in_specs=[row, vec, row],
        out_specs=[pl.BlockSpec((SUBLANES, LANES), lambda i: (0, 0)), row, row, vec],
        out_shape=[SDS((SUBLANES, LANES), F32), SDS((T, D), F32), SDS((T, D), BF16), SDS((1, D), F32)],
        compiler_params=_cp("arbitrary"))(x, g, tgt)


def _shift_down(u, k, prev_row):
    rows = lax.broadcasted_iota(I32, u.shape, 0)
    s = pltpu.roll(u, k, 0)
    for t in range(k):
        s = jnp.where(rows == t, prev_row(SUBLANES - k + t), s)
    return s


def _shift_up(u, k, next_row):
    n = u.shape[0]
    rows = lax.broadcasted_iota(I32, u.shape, 0)
    s = pltpu.roll(u, n - k, 0)
    for t in range(k):
        s = jnp.where(rows == n - k + t, next_row(t), s)
    return s


def _conv_taps(z_ref, zp_ref, w_ref, cw, first):
    u = z_ref[:, cw:2 * cw] * z_ref[:, 2 * cw:]

    def prev_row(r):
        return jnp.where(first, 0.0, zp_ref[r:r + 1, cw:2 * cw] * zp_ref[r:r + 1, 2 * cw:])

    u1 = _shift_down(u, 1, prev_row)
    u2 = _shift_down(u, 2, prev_row)
    cv = w_ref[0:1, :] * u2 + w_ref[1:2, :] * u1 + w_ref[2:3, :] * u
    return u, u1, u2, cv


def _mix_in_conv(x, g, w, wc, gc, name, tm=512):
    T, D = x.shape
    cw = w.shape[0] // 3
    tm = _tile(T, tm, SUBLANES)

    def body(x_ref, g_ref, w_ref, wc_ref, gc_ref, z_ref, h_ref, o_ref, prev_ref):
        xv = x_ref[...]
        h = (xv * _rstd(xv) * g_ref[...]).astype(BF16)
        h_ref[...] = h
        z_ref[...] = _nt(h, w_ref[...])
        _, _, _, cv = _conv_taps(z_ref, prev_ref, wc_ref, cw, pl.program_id(0) == 0)
        y = z_ref[:, :cw] * cv
        o_ref[...] = (y * _rstd(y) * gc_ref[...]).astype(BF16)
        prev_ref[...] = z_ref[tm - SUBLANES:, :]

    row = lambda n: pl.BlockSpec((tm, n), lambda i: (i, 0))
    return pl.pallas_call(
        body, name=name, grid=(T // tm,), in_specs=[row(D), _full(g), _full(w), _full(wc), _full(gc)],
        out_specs=[row(3 * cw), row(D), row(cw)],
        out_shape=[SDS((T, 3 * cw), F32), SDS((T, D), BF16), SDS((T, cw), BF16)],
        scratch_shapes=[pltpu.VMEM((SUBLANES, 3 * cw), F32)], compiler_params=_cp("arbitrary"))(x, g, w, wc, gc)


def _mix_bwd_head(dyb, w_out, zc, wc, gc, ot, g_col, name, scatter):
    nb, aw, tm = ot.shape
    T, D = dyb.shape
    cw = zc.shape[1] // 3
    H = aw // HEAD_DIM
    hb = tm // SUBLANES

    def body(dyb_ref, wo_ref, z_ref, zp_ref, w_ref, g_ref, o_ref, ga_ref,
             dz_ref, dw_ref, dg_ref, do_ref, dl_ref, dga_ref, carry_ref):
        s = pl.program_id(0)
        dy = _nt(dyb_ref[...], _rows2d(wo_ref))
        ov = o_ref[...]
        ra = lax.rsqrt(jnp.mean(ov * ov, axis=0, keepdims=True) + EPS)
        oh = ov * ra
        dya = dy[:, cw:].T
        dyah = dya * ga_ref[...]
        dov = ra * (dyah - oh * jnp.mean(dyah * oh, axis=0, keepdims=True))
        do_ref[...] = dov.astype(BF16)
        dl_ref[...] = jnp.sum((dov * ov).reshape(H, HEAD_DIM, tm), axis=1)
        _accumulate(dga_ref, s == 0, jnp.broadcast_to(jnp.sum(dya * oh, axis=1, keepdims=True), dga_ref.shape))
        u, u1, u2, cv = _conv_taps(z_ref, zp_ref, w_ref, cw, s == nb - 1)
        zb = z_ref[:, :cw]
        y = zb * cv
        r = _rstd(y)
        yh = y * r
        dyn = dy[:, :cw]
        dyh = dyn * g_ref[...]
        dyc = r * (dyh - yh * jnp.mean(dyh * yh, axis=-1, keepdims=True))
        dcv = dyc * zb

        def next_row(t):
            return jnp.where(s == 0, 0.0, carry_ref[t:t + 1, :])

        du = w_ref[2:3, :] * dcv + w_ref[1:2, :] * _shift_up(dcv, 1, next_row) + w_ref[0:1, :] * _shift_up(dcv, 2, next_row)
        carry_ref[...] = dcv[0:SUBLANES, :]
        dz_ref[:, :cw] = (dyc * cv).astype(BF16)
        dz_ref[:, cw:2 * cw] = (du * z_ref[:, 2 * cw:]).astype(BF16)
        dz_ref[:, 2 * cw:] = (du * z_ref[:, cw:2 * cw]).astype(BF16)
        tap = lax.broadcasted_iota(I32, (SUBLANES, cw), 0)
        dwp = jnp.where(tap == 0, jnp.sum(dcv * u2, axis=0, keepdims=True),
                        jnp.where(tap == 1, jnp.sum(dcv * u1, axis=0, keepdims=True),
                                  jnp.where(tap == 2, jnp.sum(dcv * u, axis=0, keepdims=True), 0.0)))
        _accumulate(dw_ref, s == 0, dwp)
        _accumulate(dg_ref, s == 0, jnp.sum(dyn * yh, axis=0, keepdims=True))

    rev = lambda s: nb - 1 - s
    rows = lambda n: pl.BlockSpec((tm, n), lambda s: (rev(s), 0))
    blk = pl.BlockSpec((None, aw, tm), lambda s: (rev(s), 0, 0))
    return _hosted_call(
        body, name, nb,
        [rows(D), _full(w_out), rows(3 * cw),
         pl.BlockSpec((SUBLANES, 3 * cw), lambda s: (jnp.maximum(rev(s) * hb - 1, 0), 0)),
         _full(wc), _full(gc), blk, _full(g_col)],
        [rows(3 * cw), pl.BlockSpec((SUBLANES, cw), lambda s: (0, 0)), pl.BlockSpec((1, cw), lambda s: (0, 0)),
         blk, pl.BlockSpec((None, H, tm), lambda s: (rev(s), 0, 0)), pl.BlockSpec((aw, LANES), lambda s: (0, 0))],
        [SDS((T, 3 * cw), BF16), SDS((SUBLANES, cw), F32), SDS((1, cw), F32),
         SDS((nb, aw, tm), BF16), SDS((nb, H, tm), F32), SDS((aw, LANES), F32)],
        (dyb, w_out, zc, zc, wc, gc, ot, g_col), scatter=scatter, scratch_shapes=[pltpu.VMEM((SUBLANES, cw), F32)])


def _proj_t(h, wat, wft, B, name):
    T, D = h.shape
    R = wat.shape[0]
    nb = T // B
    qk = 2 * R // 3
    g = qk // HEAD_DIM

    def body(h_ref, wa_ref, wf_ref, za_ref, zf_ref, n_ref):
        hv = h_ref[...]
        zab = _nt(wa_ref[...], hv).astype(BF16)
        za_ref[...] = zab
        zf_ref[...] = _nt(wf_ref[...], hv)
        zv = zab[:qk].astype(F32)
        ss = jnp.sum((zv * zv).reshape(g, HEAD_DIM, B), axis=1)
        mx = jnp.broadcast_to(jnp.max(ss, axis=1, keepdims=True), n_ref.shape)

        @pl.when(pl.program_id(0) == 0)
        def _():
            n_ref[...] = mx

        @pl.when(pl.program_id(0) > 0)
        def _():
            n_ref[...] = jnp.maximum(n_ref[...], mx)

    return pl.pallas_call(
        body, name=name, grid=(nb,), in_specs=[pl.BlockSpec((B, D), lambda i: (i, 0)), _full(wat), _full(wft)],
        out_specs=[pl.BlockSpec((None, R, B), lambda i: (i, 0, 0)), pl.BlockSpec((SUBLANES, B), lambda i: (0, i)),
                   pl.BlockSpec((g, LANES), lambda i: (0, 0))],
        out_shape=[SDS((nb, R, B), BF16), SDS((SUBLANES, T), F32), SDS((g, LANES), F32)],
        compiler_params=_cp("arbitrary"))(h, wat, wft)


def _split3(v):
    hi = v.astype(BF16).astype(F32)
    r1 = v - hi
    mid = r1.astype(BF16).astype(F32)
    lo = (r1 - mid).astype(BF16).astype(F32)
    return hi, mid, lo


def _tri_dot3(v, tri):
    hi, mid, lo = _split3(v)
    return _nn(hi.astype(BF16), tri) + _nn(mid.astype(BF16), tri) + _nn(lo.astype(BF16), tri)


def _logf_cumsum(zft, b, tb, name):
    H, T = zft.shape

    def body(z_ref, b_ref, c_ref, qa_ref, ka_ref, carry_ref, hi_ref, mid_ref, lo_ref):
        @pl.when(pl.program_id(0) == 0)
        def _():
            carry_ref[...] = jnp.zeros_like(carry_ref)
        xv = z_ref[...] + b_ref[...]
        lf = jnp.minimum(xv, 0.0) - jnp.log(1.0 + jnp.exp(-jnp.abs(xv)))
        src = lax.broadcasted_iota(I32, (tb, tb), 0)
        dst = lax.broadcasted_iota(I32, (tb, tb), 1)
        tri = jnp.where(src <= dst, 1.0, 0.0).astype(BF16)
        cs = _tri_dot3(lf, tri) + carry_ref[...]
        c_ref[...] = cs
        hi_ref[...], mid_ref[...], lo_ref[...] = _split3(cs)
        col = lax.broadcasted_iota(I32, cs.shape, 1)
        carry_ref[...] = jnp.sum(jnp.where(col == tb - 1, cs, 0.0), axis=1, keepdims=True)
        row = lax.broadcasted_iota(I32, (AUG, tb), 0)
        for h in range(H):
            terms = [r[h:h + 1, :] for r in (hi_ref, mid_ref, lo_ref)]
            qa, ka = jnp.where(row < 3, 1.0, 0.0), jnp.where((row >= 3) & (row < 6), 1.0, 0.0)
            for k, t in enumerate(terms):
                qa = jnp.where(row == 3 + k, t, qa)
                ka = jnp.where(row == k, -t, ka)
            qa_ref[h] = qa
            ka_ref[h] = ka

    blk = pl.BlockSpec((H, tb), lambda i: (0, i))
    aug = pl.BlockSpec((H, None, AUG, tb), lambda i: (0, i, 0, 0))
    return pl.pallas_call(
        body, name=name, grid=(T // tb,), in_specs=[blk, _full(b)], out_specs=[blk, aug, aug],
        out_shape=[SDS((H, T), F32), SDS((H, T // tb, AUG, tb), F32), SDS((H, T // tb, AUG, tb), F32)],
        scratch_shapes=[pltpu.VMEM((H, 1), F32)] + [pltpu.VMEM((H, tb), F32)] * 3,
        compiler_params=_cp("arbitrary"))(zft, b)


def _logf_cumsum_bwd(dcq, dck, zft, b, name, tb=512):
    H, T = zft.shape
    tb = _tile(T, tb, LANES)
    nb = T // tb

    def body(dq_ref, dk_ref, z_ref, b_ref, o_ref, db_ref, carry_ref):
        s = pl.program_id(0)

        @pl.when(s == 0)
        def _():
            carry_ref[...] = jnp.zeros_like(carry_ref)
        dc = dq_ref[...] - dk_ref[...]
        src = lax.broadcasted_iota(I32, (tb, tb), 0)
        dst = lax.broadcasted_iota(I32, (tb, tb), 1)
        tri = jnp.where(src >= dst, 1.0, 0.0).astype(BF16)
        dl = _tri_dot3(dc, tri) + carry_ref[...]
        col = lax.broadcasted_iota(I32, dl.shape, 1)
        carry_ref[...] = jnp.sum(jnp.where(col == 0, dl, 0.0), axis=1, keepdims=True)
        dz = dl * jax.nn.sigmoid(-(z_ref[...] + b_ref[...]))
        o_ref[...] = dz
        _accumulate(db_ref, s == 0, jnp.sum(dz, axis=1, keepdims=True))

    blk = pl.BlockSpec((H, tb), lambda s: (0, nb - 1 - s))
    vec = pl.BlockSpec((H, 1), lambda s: (0, 0))
    return pl.pallas_call(
        body, name=name, grid=(nb,), in_specs=[blk, blk, blk, vec], out_specs=[blk, vec],
        out_shape=[SDS((H, T), F32), SDS((H, 1), F32)],
        scratch_shapes=[pltpu.VMEM((H, 1), F32)], compiler_params=_cp("arbitrary"))(dcq, dck, zft, b)


def _skip_table(cs_col, ce_row, norms, name):
    H, nb, _ = cs_col.shape

    def body(cs_ref, ce_ref, n_ref, jm_ref, im_ref):
        h = pl.program_id(0)
        nq = n_ref[pl.ds(h, 1), 0:1]
        nk = n_ref[pl.ds(H + h, 1), 0:1]
        bound = 2.0 * jnp.sqrt(nq * nk) * 0.125
        skip = jnp.where(bound + cs_ref[...] - ce_ref[...] <= -SKIP_MARGIN, 1, 0).astype(I32)
        jm_ref[...] = jnp.sum(skip, axis=1, keepdims=True)
        im_ref[...] = nb - 1 - jnp.sum(skip, axis=0, keepdims=True)

    col = pl.BlockSpec((None, nb, 1), lambda h: (h, 0, 0))
    row = pl.BlockSpec((None, 1, nb), lambda h: (h, 0, 0))
    return pl.pallas_call(
        body, name=name, grid=(H,), in_specs=[col, row, _full(norms)], out_specs=[col, row],
        out_shape=[SDS((H, nb, 1), I32), SDS((H, 1, nb), I32)], compiler_params=_cp("parallel"))(cs_col, ce_row, norms)


def _causal_mask(B):
    krow = lax.broadcasted_iota(I32, (B, B), 0)
    qcol = lax.broadcasted_iota(I32, (B, B), 1)
    return krow <= qcol


def _attn_fwd(jmin, zat, qa, ka, name):
    nb, R, B = zat.shape
    aw = R // 3
    H = aw // HEAD_DIM
    hd = HEAD_DIM

    def body(jm_ref, q_ref, k_ref, v_ref, qa_ref, ka_ref, o_ref, lse_ref):
        h = pl.program_id(0)
        i = pl.program_id(1)
        qt = jnp.concatenate([q_ref[...] * 0.125, qa_ref[...].astype(BF16)], axis=0)

        def keys(j):
            return jnp.concatenate([k_ref[j], ka_ref[j].astype(BF16)], axis=0)

        def update(s, vt, carry):
            m, l, acc = carry
            m_new = jnp.maximum(m, jnp.max(s, axis=0, keepdims=True))
            p = jnp.exp(s - m_new)
            a = jnp.exp(m - m_new)
            l = a * l + jnp.sum(p, axis=0, keepdims=True)
            acc = a * acc + _nn(vt, p.astype(BF16))
            return m_new, l, acc

        def block(j, carry, masked):
            s = _tn(keys(j), qt)
            if masked:
                s = jnp.where(_causal_mask(B), s, NEG)
            return update(s, v_ref[j], carry)

        def two_blocks(t, carry):
            j = j0 + 2 * t
            s = _tn(jnp.concatenate([keys(j), keys(j + 1)], axis=1), qt)
            return update(s, jnp.concatenate([v_ref[j], v_ref[j + 1]], axis=1), carry)

        j0 = jm_ref[h * nb + i]
        pairs = (i - j0) // 2
        init = (jnp.full((1, B), NEG, F32), jnp.zeros((1, B), F32), jnp.zeros((hd, B), F32))
        carry = lax.fori_loop(0, pairs, two_blocks, init)
        carry = lax.fori_loop(j0 + 2 * pairs, i, lambda j, c: block(j, c, False), carry)
        m, l, acc = block(i, carry, True)
        o_ref[...] = acc / l
        lse_ref[...] = m + jnp.log(l)

    grid_spec = pltpu.PrefetchScalarGridSpec(
        num_scalar_prefetch=1, grid=(H, nb),
        in_specs=[pl.BlockSpec((None, hd, B), lambda h, i, jm: (i, h, 0)),
                  pl.BlockSpec((nb, hd, B), lambda h, i, jm: (0, H + h, 0)),
                  pl.BlockSpec((nb, hd, B), lambda h, i, jm: (0, 2 * H + h, 0)),
                  pl.BlockSpec((None, None, AUG, B), lambda h, i, jm: (h, i, 0, 0)),
                  pl.BlockSpec((None, nb, AUG, B), lambda h, i, jm: (h, 0, 0, 0))],
        out_specs=[pl.BlockSpec((None, hd, B), lambda h, i, jm: (i, h, 0)),
                   pl.BlockSpec((None, None, 1, B), lambda h, i, jm: (h, i, 0, 0))])
    return pl.pallas_call(
        body, name=name, grid_spec=grid_spec,
        out_shape=[SDS((nb, aw, B), F32), SDS((H, nb, 1, B), F32)],
        compiler_params=_cp("parallel", "parallel"))(jmin, zat, zat, zat, qa, ka)


def _gnorm_t_fwd(ot, g_col, name):
    nb, aw, B = ot.shape

    def body(o_ref, g_ref, y_ref):
        ov = o_ref[...]
        r = lax.rsqrt(jnp.mean(ov * ov, axis=0, keepdims=True) + EPS)
        y_ref[...] = (ov * r * g_ref[...]).T.astype(BF16)

    return pl.pallas_call(
        body, name=name, grid=(nb,), in_specs=[pl.BlockSpec((None, aw, B), lambda i: (i, 0, 0)), _full(g_col)],
        out_specs=pl.BlockSpec((B, aw), lambda i: (i, 0)), out_shape=SDS((nb * B, aw), BF16),
        compiler_params=_cp("parallel"))(ot, g_col)


def _attn_bwd(imax, zat, qa, ka, dot, lse, delta, name):
    nb, R, B = zat.shape
    aw = R // 3
    H = aw // HEAD_DIM
    hd = HEAD_DIM

    def body(im_ref, k_ref, v_ref, ka_ref, q_ref, qa_ref, do_ref, lse_ref, dl_ref,
             dq_ref, dqa_ref, dk_ref, dka_ref, dv_ref, dq_acc):
        h = pl.program_id(0)
        j = pl.program_id(1)

        @pl.when(j == 0)
        def _():
            dq_acc[...] = jnp.zeros_like(dq_acc)
            dqa_ref[...] = jnp.zeros_like(dqa_ref)
        kt = jnp.concatenate([k_ref[...], ka_ref[...].astype(BF16)], axis=0)
        vt = v_ref[...]

        def queries(i):
            return jnp.concatenate([q_ref[i] * 0.125, qa_ref[i].astype(BF16)], axis=0)

        def grads(qt, dov, lse, dl, carry, masked):
            dk, dv = carry
            s = _tn(kt, qt)
            if masked:
                s = jnp.where(_causal_mask(B), s, NEG)
            p = jnp.exp(s - lse)
            dp = _tn(vt, dov)
            ds = (p * (dp - dl)).astype(BF16)
            return (dk + _nt(qt, ds), dv + _nt(dov, p.astype(BF16))), _nn(kt, ds)

        def add_dq(i, r):
            dq_acc[i] += 0.125 * r[:hd]
            dqa_ref[i] += r[hd:]

        def block(i, carry, masked):
            carry, r = grads(queries(i), do_ref[i], lse_ref[i], dl_ref[i, pl.ds(h, 1), :], carry, masked)
            add_dq(i, r)
            return carry

        def two_blocks(t, carry):
            i = j + 1 + 2 * t
            both = lambda f: jnp.concatenate([f(i), f(i + 1)], axis=1)
            carry, r = grads(both(queries), both(lambda n: do_ref[n]), both(lambda n: lse_ref[n]),
                             both(lambda n: dl_ref[n, pl.ds(h, 1), :]), carry, False)
            add_dq(i, r[:, :B])
            add_dq(i + 1, r[:, B:])
            return carry

        last = im_ref[h * nb + j]
        pairs = (last - j) // 2
        carry = block(j, (jnp.zeros((hd + AUG, B), F32), jnp.zeros((hd, B), F32)), True)
        carry = lax.fori_loop(0, pairs, two_blocks, carry)
        dk, dv = lax.fori_loop(j + 1 + 2 * pairs, last + 1, lambda i, c: block(i, c, False), carry)
        dk_ref[...] = dk[:hd].astype(BF16)
        dka_ref[...] = dk[hd:]
        dv_ref[...] = dv.astype(BF16)

        @pl.when(j == nb - 1)
        def _():
            dq_ref[...] = dq_acc[...].astype(BF16)

    head_rows = lambda off: pl.BlockSpec((nb, hd, B), lambda h, j, im: (0, off + h, 0))
    key_rows = lambda off: pl.BlockSpec((None, hd, B), lambda h, j, im: (j, off + h, 0))
    aug_all = pl.BlockSpec((None, nb, AUG, B), lambda h, j, im: (h, 0, 0, 0))
    aug_one = pl.BlockSpec((None, None, AUG, B), lambda h, j, im: (h, j, 0, 0))
    grid_spec = pltpu.PrefetchScalarGridSpec(
        num_scalar_prefetch=1, grid=(H, nb),
        in_specs=[key_rows(H), key_rows(2 * H), aug_one, head_rows(0), aug_all, head_rows(0),
                  pl.BlockSpec((None, nb, 1, B), lambda h, j, im: (h, 0, 0, 0)),
                  pl.BlockSpec((nb, H, B), lambda h, j, im: (0, 0, 0))],
        out_specs=[head_rows(0), aug_all, key_rows(0), aug_one, key_rows(0)],
        scratch_shapes=[pltpu.VMEM((nb, hd, B), F32)])
    return pl.pallas_call(
        body, name=name, grid_spec=grid_spec,
        out_shape=[SDS((nb, aw, B), BF16), SDS((H, nb, AUG, B), F32), SDS((nb, aw, B), BF16),
                   SDS((H, nb, AUG, B), F32), SDS((nb, aw, B), BF16)],
        compiler_params=_cp("arbitrary", "arbitrary"))(imax, zat, zat, ka, zat, qa, dot, lse, delta)


def _mix_dwt(dq, dk, dv, dzft, h, name):
    nb, aw, B = dq.shape
    D = h.shape[1]

    def body(dq_ref, dk_ref, dv_ref, dzf_ref, h_ref, o_ref, of_ref):
        first = pl.program_id(0) == 0
        hv = h_ref[...]
        part = jnp.concatenate([_nn(r[...], hv) for r in (dq_ref, dk_ref, dv_ref)], axis=0)
        _accumulate(o_ref, first, part)
        _accumulate(of_ref, first, _nn(dzf_ref[...].astype(BF16), hv))

    blk = pl.BlockSpec((None, aw, B), lambda i: (i, 0, 0))
    return pl.pallas_call(
        body, name=name, grid=(nb,),
        in_specs=[blk, blk, blk, pl.BlockSpec((SUBLANES, B), lambda i: (0, i)), pl.BlockSpec((B, D), lambda i: (i, 0))],
        out_specs=[pl.BlockSpec((3 * aw, D), lambda i: (0, 0)), pl.BlockSpec((SUBLANES, D), lambda i: (0, 0))],
        out_shape=[SDS((3 * aw, D), F32), SDS((SUBLANES, D), F32)],
        compiler_params=_cp("arbitrary"))(dq, dk, dv, dzft, h)


def _mix_bwd_in(dzc, dq, dk, dv, dzft, wc, wat, wft, x, g, dres, name, scatter):
    T, D = x.shape
    nb, aw, B = dq.shape

    def body(dzc_ref, dq_ref, dk_ref, dv_ref, dzf_ref, wc_ref, wa_ref, wf_ref, x_ref, g_ref, r_ref,
             dx_ref, dxb_ref, dg_ref):
        dh = _nn(dzc_ref[...], wc_ref[...])
        for s, r in enumerate((dq_ref, dk_ref, dv_ref)):
            dh = dh + _tn(r[...], wa_ref[s * aw:(s + 1) * aw, :])
        dh = dh + _tn(dzf_ref[...].astype(BF16), wf_ref[...])
        _norm_bwd_store(dh, x_ref, g_ref, r_ref, dx_ref, dxb_ref, dg_ref, pl.program_id(0) == 0)

    blk = pl.BlockSpec((None, aw, B), lambda i: (i, 0, 0))
    row = lambda n: pl.BlockSpec((B, n), lambda i: (i, 0))
    return _hosted_call(
        body, name, nb,
        [row(dzc.shape[1]), blk, blk, blk, pl.BlockSpec((SUBLANES, B), lambda i: (0, i)),
         _full(wc), _full(wat), _full(wft), row(D), _full(g), row(D)],
        [row(D), row(D), pl.BlockSpec((1, D), lambda i: (0, 0))],
        [SDS((T, D), F32), SDS((T, D), BF16), SDS((1, D), F32)],
        (dzc, dq, dk, dv, dzft, wc, wat, wft, x, g, dres), scatter)


def _mem_kv(mem, g, wkv, name):
    M, D = mem.shape
    dx = wkv.shape[2]

    def body(m_ref, g_ref, w_ref, kv_ref, h_ref):
        mv = m_ref[...]
        h = (mv * _rstd(mv) * g_ref[...]).astype(BF16)
        h_ref[...] = h
        for s in range(N_DEV):
            kv_ref[s] = _nn(h, w_ref[s]).astype(BF16)

    return pl.pallas_call(
        body, name=name, grid=(1,), in_specs=[_full(mem), _full(g), _full(wkv)],
        out_specs=[pl.BlockSpec((N_DEV, M, dx), lambda i: (0, 0, 0)), pl.BlockSpec((M, D), lambda i: (0, 0))],
        out_shape=[SDS((N_DEV, M, dx), BF16), SDS((M, D), BF16)], compiler_params=_cp("arbitrary"))(mem, g, wkv)


def _mem_kv_bwd(dkv, hm, wkv, mem, g, name):
    M, D = mem.shape
    dx = wkv.shape[2]

    def body(dkv_ref, h_ref, w_ref, m_ref, g_ref, dw_ref, dg_ref):
        hv = h_ref[...]
        dh = jnp.zeros((M, D), F32)
        for s in range(N_DEV):
            d = dkv_ref[s].astype(BF16)
            dw_ref[s] = _tn(hv, d).astype(BF16)
            dh = dh + _nt(d, w_ref[s])
        mv = m_ref[...]
        dg_ref[...] = jnp.sum(dh * (mv * _rstd(mv)), axis=0, keepdims=True)

    return pl.pallas_call(
        body, name=name, grid=(1,), in_specs=[_full(dkv), _full(hm), _full(wkv), _full(mem), _full(g)],
        out_specs=[pl.BlockSpec((N_DEV, D, dx), lambda i: (0, 0, 0)), pl.BlockSpec((1, D), lambda i: (0, 0))],
        out_shape=[SDS((N_DEV, D, dx), BF16), SDS((1, D), F32)], compiler_params=_cp("arbitrary"))(dkv, hm, wkv, mem, g)


def _xattn_probs(q_ref, kv_ref, h, dx, scale):
    qh = q_ref[:, h * dx:(h + 1) * dx]
    kh = kv_ref[h]
    s = _nt(qh, kh) * scale
    p = jnp.exp(s - jnp.max(s, axis=-1, keepdims=True))
    return qh, kh, p / jnp.sum(p, axis=-1, keepdims=True)


def _xattn(x, g, wq, kv, wo, name, tm=512):
    T, D = x.shape
    dx = D // N_XHEADS
    scale = dx ** -0.5
    tm = _tile(T, tm, SUBLANES)

    def body(x_ref, g_ref, wq_ref, kv_ref, wo_ref, xo_ref, h_ref, q_ref, o_ref):
        xv = x_ref[...]
        h = (xv * _rstd(xv) * g_ref[...]).astype(BF16)
        h_ref[...] = h
        q_ref[...] = _nn(h, _rows2d(wq_ref)).astype(BF16)
        for hd in range(N_XHEADS):
            _, _, p = _xattn_probs(q_ref, kv_ref, hd, dx, scale)
            o_ref[:, hd * dx:(hd + 1) * dx] = _nn(p.astype(BF16), kv_ref[N_XHEADS + hd]).astype(BF16)
        xo_ref[...] = xv + _nn(o_ref[...], _rows2d(wo_ref))

    row = pl.BlockSpec((tm, D), lambda i: (i, 0))
    return pl.pallas_call(
        body, name=name, grid=(T // tm,), in_specs=[row, _full(g), _full(wq), _full(kv), _full(wo)],
        out_specs=[row] * 4, out_shape=[SDS((T, D), F32)] + [SDS((T, D), BF16)] * 3,
        compiler_params=_cp("parallel"))(x, g, wq, kv, wo)


def _xattn_bwd(dyb, dres, x, g, q, kv, wq, wo, name, scatter, tm=512):
    T, D = x.shape
    dx = D // N_XHEADS
    scale = dx ** -0.5
    tm = _tile(T, tm, SUBLANES)

    def body(dy_ref, r_ref, x_ref, g_ref, q_ref, kv_ref, wq_ref, wo_ref, dx_ref, dxb_ref, dg_ref, dq_ref, dkv_ref):
        first = pl.program_id(0) == 0

        @pl.when(first)
        def _():
            dkv_ref[...] = jnp.zeros_like(dkv_ref)
        do = _nt(dy_ref[...], _rows2d(wo_ref)).astype(BF16)
        for hd in range(N_XHEADS):
            qh, kh, p = _xattn_probs(q_ref, kv_ref, hd, dx, scale)
            doh = do[:, hd * dx:(hd + 1) * dx]
            dp = _nt(doh, kv_ref[N_XHEADS + hd])
            ds = (p * (dp - jnp.sum(p * dp, axis=-1, keepdims=True)) * scale).astype(BF16)
            dq_ref[:, hd * dx:(hd + 1) * dx] = _nn(ds, kh).astype(BF16)
            dkv_ref[hd] += _tn(ds, qh)
            dkv_ref[N_XHEADS + hd] += _tn(p.astype(BF16), doh)
        dh = _nt(dq_ref[...], _rows2d(wq_ref))
        _norm_bwd_store(dh, x_ref, g_ref, r_ref, dx_ref, dxb_ref, dg_ref, first)

    row = pl.BlockSpec((tm, D), lambda i: (i, 0))
    return _hosted_call(
        body, name, T // tm, [row, row, row, _full(g), row, _full(kv), _full(wq), _full(wo)],
        [row, row, pl.BlockSpec((1, D), lambda i: (0, 0)), row, pl.BlockSpec(kv.shape, lambda i: (0, 0, 0))],
        [SDS((T, D), F32), SDS((T, D), BF16), SDS((1, D), F32), SDS((T, D), BF16), SDS(kv.shape, F32)],
        (dyb, dres, x, g, q, kv, wq, wo), scatter)


def _mesh_position():
    return lax.axis_index("x"), lax.axis_index("y"), lax.axis_index("c")


class _GatherCopies:
    def __init__(self, x_refs, out_refs, send_sems, recv_sems, local_sems):
        self.x_refs, self.out_refs, self.n = x_refs, out_refs, len(x_refs)
        self.send_sems, self.recv_sems, self.local_sems = send_sems, recv_sems, local_sems
        x, y, c = _mesh_position()
        self.c = c
        self.me, self.sibling = (x, y, c), (x, y, 1 - c)
        self.chips = [(1 - x, y), (x, 1 - y), (1 - x, 1 - y)]

    def slot(self, w, px, py, pc):
        return self.out_refs[w].at[4 * px + 2 * py + pc]

    def copy(self, w, k, block, to, src=None):
        return pltpu.make_async_remote_copy(
            src_ref=self.slot(w, *block) if src is None else src, dst_ref=self.slot(w, *block),
            send_sem=self.send_sems.at[w, k], recv_sem=self.recv_sems.at[w, k], device_id=to,
            device_id_type=pl.DeviceIdType.MESH)

    def mine(self, w):
        return pltpu.make_async_copy(self.x_refs[w], self.slot(w, *self.me), self.local_sems.at[w])

    def first(self, w):
        src = self.x_refs[w]
        return [self.copy(w, 0, self.me, self.sibling, src=src)] + [
            self.copy(w, 1 + j, self.me, (*chip, self.c), src=src) for j, chip in enumerate(self.chips)]

    def passed(self, w):
        return [self.copy(w, 4 + j, (*chip, self.c), self.sibling) for j, chip in enumerate(self.chips)]


def _gather_start(*refs):
    cp = _GatherCopies(*refs)
    for w in range(cp.n):
        cp.mine(w).start()
        for f in cp.first(w):
            f.start()


def _gather_relay(*refs):
    cp = _GatherCopies(*refs)
    for w in range(cp.n):
        for j, (chip, fwd) in enumerate(zip(cp.chips, cp.passed(w))):
            cp.copy(w, 1 + j, (*chip, cp.c), cp.me).wait_recv()
            fwd.start()


def _gather_finish(*refs):
    cp = _GatherCopies(*refs)
    for w in range(cp.n):
        cp.copy(w, 0, cp.sibling, cp.me).wait_recv()
        for j, chip in enumerate(cp.chips):
            cp.copy(w, 4 + j, (*chip, 1 - cp.c), cp.me).wait_recv()
    for w in range(cp.n):
        for f in cp.first(w) + cp.passed(w):
            f.wait_send()
        cp.mine(w).wait()


def _all_gather(shards, name):
    n = len(shards)

    def body(*refs):
        parts = (refs[:n], refs[n:2 * n]) + refs[2 * n:]
        _gather_start(*parts)
        _gather_relay(*parts)
        _gather_finish(*parts)

    any_spec = pl.BlockSpec(memory_space=pl.ANY)
    return pl.pallas_call(
        body, name=name, out_shape=[SDS((N_DEV,) + s.shape, s.dtype) for s in shards],
        in_specs=[any_spec] * n, out_specs=[any_spec] * n,
        scratch_shapes=[pltpu.SemaphoreType.DMA((n, 7)), pltpu.SemaphoreType.DMA((n, 7)), pltpu.SemaphoreType.DMA((n,))],
    )(*shards)


def _scatter_copies(x_refs, out_refs, send_sems, recv_sems, local_sems):
    x, y, c = _mesh_position()
    me = 4 * x + 2 * y + c
    own, remote = [], []
    for w, (src, dst) in enumerate(zip(x_refs, out_refs)):
        own.append(pltpu.make_async_copy(src.at[me], dst.at[me], local_sems.at[w]))
        for k in range(1, N_DEV):
            px = 1 - x if k & 4 else x
            py = 1 - y if k & 2 else y
            pc = 1 - c if k & 1 else c
            remote.append(pltpu.make_async_remote_copy(
                src_ref=src.at[4 * px + 2 * py + pc], dst_ref=dst.at[me], send_sem=send_sems.at[w, k - 1],
                recv_sem=recv_sems.at[w, k - 1], device_id=(px, py, pc), device_id_type=pl.DeviceIdType.MESH))
    return own, remote


def _scatter_start(x_refs, out_refs, send_sems, recv_sems, local_sems):
    own, remote = _scatter_copies(x_refs, out_refs, send_sems, recv_sems, local_sems)
    for cp in own + remote:
        cp.start()


def _scatter_wait(x_refs, out_refs, send_sems, recv_sems, local_sems):
    own, remote = _scatter_copies(x_refs, out_refs, send_sems, recv_sems, local_sems)
    for cp in remote:
        cp.wait_recv()
    for cp in remote:
        cp.wait_send()
    for cp in own:
        cp.wait()


def _sum_adamw(parts, w, m, v, name, tr=256):
    L, R, C = w.shape
    tr = _tile(R, tr, SUBLANES)
    nblk = R // tr

    def body(*refs):
        p_refs = refs[:L]
        w_ref, m_ref, v_ref, g_ref, d_ref, mo_ref, vo_ref = refs[L:]
        layer = pl.program_id(0)
        g = None
        for k, p_ref in enumerate(p_refs):
            gk = p_ref[0].astype(F32)
            for s in range(1, N_DEV):
                gk = gk + p_ref[s].astype(F32)
            g = gk if g is None else jnp.where(layer == k, gk, g)
        mn = ADAM_B1 * m_ref[...] + (1.0 - ADAM_B1) * g
        vn = ADAM_B2 * v_ref[...] + (1.0 - ADAM_B2) * jnp.square(g)
        m_hat = mn / (1.0 - ADAM_B1 ** ADAM_STEP)
        v_hat = vn / (1.0 - ADAM_B2 ** ADAM_STEP)
        g_ref[...] = g
        d_ref[...] = -ADAM_LR * (m_hat / (jnp.sqrt(v_hat) + ADAM_EPS) + ADAM_WD * w_ref[...])
        mo_ref[...] = mn
        vo_ref[...] = vn

    def part_spec(k):
        return pl.BlockSpec((N_DEV, tr, C),
                            lambda l, i: (0, jnp.where(l == k, i, jnp.where(l < k, 0, nblk - 1)), 0))

    row = pl.BlockSpec((None, tr, C), lambda l, i: (l, i, 0))
    return pl.pallas_call(
        body, name=name, grid=(L, nblk), in_specs=[part_spec(k) for k in range(L)] + [row, row, row],
        out_specs=[row] * 4, out_shape=[SDS((L, R, C), F32)] * 4,
        compiler_params=_cp("arbitrary", "arbitrary"))(*parts, w, m, v)


def _pack(flat_list, row_mult):
    flat = jnp.concatenate(flat_list, axis=-1)
    n = flat.shape[-1]
    chunk = row_mult * PACK_COLS
    pad = -n % chunk
    flat = jnp.pad(flat, [(0, 0)] * (flat.ndim - 1) + [(0, pad)])
    return flat.reshape(flat.shape[:-1] + ((n + pad) // PACK_COLS, PACK_COLS))


def _unpack(packed, shapes):
    lead = packed.shape[:-2]
    flat = packed.reshape(lead + (-1,))
    out = []
    off = 0
    for shp in shapes:
        n = 1
        for d in shp:
            n *= d
        out.append(flat[..., off:off + n].reshape(lead + tuple(shp)))
        off += n
    return out


def _ffn_bwd(dx, dxb, saved, wgu, wd, tag, also_scatter, send_later):
    x, g, h, ab, act = saved
    D = x.shape[1]
    dwd = _ffn_dwd(act, dxb, f"{tag}_dwd").reshape(N_DEV, -1, D)
    if send_later:
        dgu, dx, dxb, dg, *p_also = _ffn_bwd_in(wgu, x, g, dx, f"{tag}_bwd_in", also_scatter, act_bwd=(dxb, wd, ab))
        return dx, dxb, dg, [_ffn_dwgu(h, dgu, f"{tag}_dwgu").reshape(N_DEV, -1, D), dwd], p_also
    dgu = _ffn_bwd_act(dxb, wd, ab, f"{tag}_bwd_act")
    dwgu = _ffn_dwgu(h, dgu, f"{tag}_dwgu").reshape(N_DEV, -1, D)
    dx, dxb, dg, p_gu, p_d, *p_also = _ffn_bwd_in(wgu, x, g, dx, f"{tag}_bwd_in", [dwgu, dwd] + list(also_scatter), dgu=dgu)
    return dx, dxb, dg, [p_gu, p_d], p_also


def _mix_fwd(x, g, w_in8, w_conv8, b_f, g_conv, g_att, w_out, tag):
    T, D = x.shape
    w_int = w_in8.reshape(-1, D)
    cw = D // 2
    aw = D - cw
    H = aw // HEAD_DIM
    B = _tile(T, ATT_BLOCK, LANES)
    nb = T // B
    w_c = w_int[:3 * cw]
    w_at = w_int[3 * cw:3 * cw + 3 * aw]
    w_ft = jnp.pad(w_int[3 * cw + 3 * aw:], ((0, SUBLANES - H), (0, 0)))
    zc, h, ync = _mix_in_conv(x, g, w_c, w_conv8, g_conv, f"{tag}_in_conv")
    zat, zft8, norms = _proj_t(h, w_at, w_ft, B, f"{tag}_in_att")
    zft = zft8[:H]
    bcol = b_f.reshape(H, 1)
    c, qa, ka = _logf_cumsum(zft, bcol, B, f"{tag}_cumsum")
    cb = c.reshape(H, nb, B)
    jmin, imax = _skip_table(cb[:, :, 0:1], cb[:, :, B - 1].reshape(H, 1, nb), norms, f"{tag}_skip")
    ot, lse = _attn_fwd(jmin.reshape(H * nb), zat, qa, ka, f"{tag}_attn")
    g_col = g_att.reshape(aw, 1)
    yna = _gnorm_t_fwd(ot, g_col, f"{tag}_gnorm")
    xo = _mm_res([ync, yna], w_out, x, f"{tag}_out")
    saved = (x, g, h, zc, zft, bcol, zat, qa, ka, imax.reshape(H * nb), ot, lse, ync, yna, w_c, w_at, w_ft, g_col)
    return xo, saved


def _mix_bwd(dx, dxb, saved, w_conv8, g_conv, w_out, tag, also_scatter):
    x, g, h, zc, zft, bcol, zat, qa, ka, imax, ot, lse, ync, yna, w_c, w_at, w_ft, g_col = saved
    T, D = x.shape
    H = zft.shape[0]
    dw_out = jnp.concatenate([_mm_tn(ync, dxb, f"{tag}_dwout_c"), _mm_tn(yna, dxb, f"{tag}_dwout_a")], axis=0)
    dzc, dwc8, dg_conv, dot, delta, dg_att, *p_also = _mix_bwd_head(dxb, w_out, zc, w_conv8, g_conv, ot, g_col,
                                                                   f"{tag}_bwd_out", also_scatter)
    dq, dqa, dk, dka, dv = _attn_bwd(imax, zat, qa, ka, dot, lse, delta, f"{tag}_attn_bwd")
    dcq = dqa[:, :, 3, :].reshape(H, T)
    dck = dka[:, :, 0, :].reshape(H, T)
    dzft, db = _logf_cumsum_bwd(dcq, dck, zft, bcol, f"{tag}_cumsum_bwd")
    dzft8 = jnp.pad(dzft, ((0, SUBLANES - H), (0, 0)))
    dw_c = _mm_tn(dzc, h, f"{tag}_dwin_c")
    dw_at, dw_ft = _mix_dwt(dq, dk, dv, dzft8, h, f"{tag}_dwin_a")
    dw_int = jnp.concatenate([dw_c, dw_at.astype(BF16), dw_ft[:H].astype(BF16)], axis=0)
    scatter = [dw_int.reshape(N_DEV, -1, D), dwc8[:3].reshape(3, N_DEV, -1).transpose(1, 0, 2),
               dw_out.reshape(N_DEV, D // N_DEV, D)]
    dx, dxb, dg, p_in, p_conv, p_out = _mix_bwd_in(dzc, dq, dk, dv, dzft8, w_c, w_at, w_ft, x, g, dx, f"{tag}_bwd_in",
                                                   scatter)
    return dx, dxb, dg, p_in, p_conv, db.reshape(H), dg_conv, dg_att[:, 0], p_out, p_also


def _xattn_block_fwd(x, g, mem, g_mem, w_q, w_kv, w_o, tag):
    kv, hm = _mem_kv(mem, g_mem, w_kv, f"{tag}_kv")
    xo, h, q, o = _xattn(x, g, w_q, kv, w_o, f"{tag}_block")
    return xo, (x, g, h, mem, g_mem, hm, q, kv, o)


def _xattn_block_bwd(dx, dxb, saved, w_q, w_kv, w_o, tag, also_scatter):
    x, g, h, mem, g_mem, hm, q, kv, o = saved
    dw_o = _mm_tn(o, dxb, f"{tag}_dwo")
    dx, dxb, dg, dq, dkv, *p_also = _xattn_bwd(dxb, dx, x, g, q, kv, w_q, w_o, f"{tag}_block_bwd", also_scatter)
    dw_q = _mm_tn(h, dq, f"{tag}_dwq")
    dw_kv, dg_mem = _mem_kv_bwd(dkv, hm, w_kv, mem, g_mem, f"{tag}_kv_bwd")
    D = dw_q.shape[0]
    return (dx, dxb, dg, dg_mem, [dw_q.reshape(N_DEV, D // N_DEV, D), dw_kv, dw_o.reshape(N_DEV, D // N_DEV, D)],
            p_also)


GATHER_FIRST = ['w_ffn1_gu']
GATHER_REST = ['w_ffn1_down', 'w_mix_in', 'w_mix_out', 'w_xq', 'w_xkv', 'w_xo', 'w_ffn2_gu', 'w_ffn2_down']
GATHER_NEXT_A = ['w_ffn1_gu', 'w_ffn1_down', 'w_mix_in', 'w_mix_out', 'w_xq', 'w_xkv', 'w_xo']
GATHER_NEXT_B = ['w_ffn2_gu', 'w_ffn2_down']


def _local_step(x, mem, tgt, P):
    L = P['g_ffn1'].shape[0]
    shards = lambda names, l: [(P[n][l].T if n in LONG_AXIS_LAST else P[n][l]).astype(BF16) for n in names]
    wgut = lambda n, l: W[n][l].reshape(-1, W[n][l].shape[-1])
    W = {n: [None] * L for n in SHARDED}

    def keep(names, l, arrays):
        for n, a in zip(names, arrays):
            W[n][l] = a

    *first, conv = _all_gather(shards(GATHER_FIRST, 0) + [P['w_conv']], "gather_first")
    keep(GATHER_FIRST, 0, first)
    w_conv = conv.transpose(1, 2, 0, 3).reshape(L, 3, -1)
    saved = []
    for l in range(L):
        row = lambda n: P[n][l][None, :]
        wc8 = jnp.pad(w_conv[l], ((0, SUBLANES - 3), (0, 0)))
        x1 = x
        bring = GATHER_REST if l == 0 else []
        ab, act, h, *got = _ffn_gu(x, row('g_ffn1'), wgut('w_ffn1_gu', l), f"l{l}_ffn1_gu", gather=shards(bring, l))
        keep(bring, l, got)
        x, = _ffn_down(act, W['w_ffn1_down'][l], x, f"l{l}_ffn1_down")
        s1 = (x1, row('g_ffn1'), h, ab, act)
        x, s2 = _mix_fwd(x, row('g_mix'), W['w_mix_in'][l], wc8, P['b_f'][l], row('g_conv_out'), P['g_att_out'][l],
                         W['w_mix_out'][l], f"l{l}_mix")
        x, s3 = _xattn_block_fwd(x, row('g_xattn'), mem, row('g_mem'), W['w_xq'][l], W['w_xkv'][l], W['w_xo'][l],
                                 f"l{l}_xattn")
        x4 = x
        bring_a, bring_b = (GATHER_NEXT_A, GATHER_NEXT_B) if l + 1 < L else ([], [])
        ab, act, h, *got = _ffn_gu(x, row('g_ffn2'), wgut('w_ffn2_gu', l), f"l{l}_ffn2_gu", gather=shards(bring_a, l + 1))
        keep(bring_a, l + 1, got)
        x, *got = _ffn_down(act, W['w_ffn2_down'][l], x, f"l{l}_ffn2_down", gather=shards(bring_b, l + 1))
        keep(bring_b, l + 1, got)
        saved.append((s1, s2, s3, (x4, row('g_ffn2'), h, ab, act), wc8))
    loss, dx, dxb, dg_final = _loss_head(x, P['g_final'][None, :], tgt, "loss_head")
    G = {n: [None] * L for n in WEIGHT_NAMES if n != 'g_final'}
    unsent = []
    for l in reversed(range(L)):
        row = lambda n: P[n][l][None, :]
        s1, s2, s3, s4, wc8 = saved[l]
        dx, dxb, G['g_ffn2'][l], ffn2_grads, got = _ffn_bwd(
            dx, dxb, s4, wgut('w_ffn2_gu', l), W['w_ffn2_down'][l], f"l{l}_ffn2", unsent, True)
        if unsent:
            G['w_ffn1_gu'][l + 1], G['w_ffn1_down'][l + 1] = got
        dx, dxb, G['g_xattn'][l], G['g_mem'][l], xattn_grads, (G['w_ffn2_gu'][l], G['w_ffn2_down'][l]) = _xattn_block_bwd(
            dx, dxb, s3, W['w_xq'][l], W['w_xkv'][l], W['w_xo'][l], f"l{l}_xattn", ffn2_grads)
        (dx, dxb, G['g_mix'][l], G['w_mix_in'][l], G['w_conv'][l], G['b_f'][l], G['g_conv_out'][l], G['g_att_out'][l],
         G['w_mix_out'][l], (G['w_xq'][l], G['w_xkv'][l], G['w_xo'][l])) = _mix_bwd(
            dx, dxb, s2, wc8, row('g_conv_out'), W['w_mix_out'][l], f"l{l}_mix", xattn_grads)
        dx, dxb, G['g_ffn1'][l], unsent, _ = _ffn_bwd(
            dx, dxb, s1, wgut('w_ffn1_gu', l), W['w_ffn1_down'][l], f"l{l}_ffn1", [], l > 0)
    G['w_ffn1_gu'][0], G['w_ffn1_down'][0] = unsent
    G['g_final'] = [dg_final]
    return loss, dx, G


def kernel(x, mem, g_ffn1, w_ffn1_gu, w_ffn1_down, g_mix, w_mix_in, w_conv, b_f, g_conv_out, g_att_out, w_mix_out, g_xattn, g_mem, w_xq, w_xkv, w_xo, g_ffn2, w_ffn2_gu, w_ffn2_down, g_final, loss_target, m_g_ffn1, m_w_ffn1_gu, m_w_ffn1_down, m_g_mix, m_w_mix_in, m_w_conv, m_b_f, m_g_conv_out, m_g_att_out, m_w_mix_out, m_g_xattn, m_g_mem, m_w_xq, m_w_xkv, m_w_xo, m_g_ffn2, m_w_ffn2_gu, m_w_ffn2_down, m_g_final, v_g_ffn1, v_w_ffn1_gu, v_w_ffn1_down, v_g_mix, v_w_mix_in, v_w_conv, v_b_f, v_g_conv_out, v_g_att_out, v_w_mix_out, v_g_xattn, v_g_mem, v_w_xq, v_w_xkv, v_w_xo, v_g_ffn2, v_w_ffn2_gu, v_w_ffn2_down, v_g_final):
    args = (x, mem, g_ffn1, w_ffn1_gu, w_ffn1_down, g_mix, w_mix_in, w_conv, b_f, g_conv_out, g_att_out, w_mix_out,
            g_xattn, g_mem, w_xq, w_xkv, w_xo, g_ffn2, w_ffn2_gu, w_ffn2_down, g_final)
    P = dict(zip(IN_NAMES, args))
    moms = (m_g_ffn1, m_w_ffn1_gu, m_w_ffn1_down, m_g_mix, m_w_mix_in, m_w_conv, m_b_f, m_g_conv_out, m_g_att_out,
            m_w_mix_out, m_g_xattn, m_g_mem, m_w_xq, m_w_xkv, m_w_xo, m_g_ffn2, m_w_ffn2_gu, m_w_ffn2_down, m_g_final)
    vars_ = (v_g_ffn1, v_w_ffn1_gu, v_w_ffn1_down, v_g_mix, v_w_mix_in, v_w_conv, v_b_f, v_g_conv_out, v_g_att_out,
             v_w_mix_out, v_g_xattn, v_g_mem, v_w_xq, v_w_xkv, v_w_xo, v_g_ffn2, v_w_ffn2_gu, v_w_ffn2_down, v_g_final)
    MOM = dict(zip(WEIGHT_NAMES, moms))
    VAR = dict(zip(WEIGHT_NAMES, vars_))

    loss_part, dx, grads = _local_step(x[0], mem[0], loss_target[0], {n: P[n] for n in WEIGHT_NAMES})

    small_shapes = [P[n].shape for n in REPLICATED] + [(1,)]
    small = _pack([a.reshape(-1) for n in REPLICATED for a in grads[n]] + [loss_part[0, :1]], SUBLANES)
    small_parts = _all_gather([small], "gather_small_grads")[0]

    res = {}
    for n in SHARDED:
        t = (lambda a: a.transpose(0, 2, 1)) if n in LONG_AXIS_LAST else (lambda a: a)
        res[n] = [t(o) for o in _sum_adamw(grads[n], t(P[n]), t(MOM[n]), t(VAR[n]), "adamw_" + n)]
    zero = [jnp.zeros((1,), F32)]
    w_, m_, v_ = (_pack([d[n].reshape(-1) for n in REPLICATED] + zero, SUBLANES)[None] for d in (P, MOM, VAR))
    small_out = [_unpack(o[0], small_shapes) for o in _sum_adamw([small_parts], w_, m_, v_, "adamw_vectors")]
    for k, n in enumerate(REPLICATED):
        res[n] = [o[k] for o in small_out]
    out = []
    for k in range(4):
        out += [res[n][k] for n in WEIGHT_NAMES]
    return (small_out[0][-1].reshape(()), dx[None], *out)
```

```python
import jax
import jax.numpy as jnp
from jax import lax
from jax.experimental import pallas as pl
from jax.experimental.pallas import tpu as pltpu

F32 = jnp.float32
BF16 = jnp.bfloat16
I32 = jnp.int32
SDS = jax.ShapeDtypeStruct

EPS = 1e-6
HEAD_DIM = 64
N_XHEADS = 4
N_DEV = 8
LANES = 128
SUBLANES = 8
MXU_COLS = 256
AUG = 16
ATT_BLOCK = 512
NEG = -1e30
SKIP_MARGIN = 115.0
PACK_COLS = 1024
VMEM_LIMIT = 58 * 1024 * 1024

ADAM_LR = 0.001
ADAM_B1 = 0.9
ADAM_B2 = 0.999
ADAM_EPS = 1e-08
ADAM_WD = 0.01
ADAM_STEP = 10

IN_NAMES = ['x', 'mem', 'g_ffn1', 'w_ffn1_gu', 'w_ffn1_down', 'g_mix', 'w_mix_in', 'w_conv', 'b_f', 'g_conv_out',
            'g_att_out', 'w_mix_out', 'g_xattn', 'g_mem', 'w_xq', 'w_xkv', 'w_xo', 'g_ffn2', 'w_ffn2_gu',
            'w_ffn2_down', 'g_final']
WEIGHT_NAMES = IN_NAMES[2:]
SHARDED = ['w_ffn1_gu', 'w_ffn1_down', 'w_mix_in', 'w_conv', 'w_mix_out', 'w_xq', 'w_xkv', 'w_xo', 'w_ffn2_gu',
           'w_ffn2_down']
REPLICATED = [n for n in WEIGHT_NAMES if n not in SHARDED]
LONG_AXIS_LAST = ['w_ffn1_gu', 'w_ffn2_gu', 'w_mix_in']


def _tile(n, pref, mult):
    t = min(pref, n) // mult * mult
    while t >= mult:
        if n % t == 0:
            return t
        t -= mult
    return n


def _cp(*sem):
    return pltpu.CompilerParams(dimension_semantics=sem, vmem_limit_bytes=VMEM_LIMIT)


def _nt(a, b):
    return lax.dot_general(a, b, (((1,), (1,)), ((), ())), preferred_element_type=F32)


def _tn(a, b):
    return lax.dot_general(a, b, (((0,), (0,)), ((), ())), preferred_element_type=F32)


def _nn(a, b):
    return jnp.dot(a, b, preferred_element_type=F32)


def _rstd(xv):
    return lax.rsqrt(jnp.mean(xv * xv, axis=-1, keepdims=True) + EPS)


def _full(a):
    nd = a.ndim
    return pl.BlockSpec(a.shape, lambda *_: (0,) * nd, pipeline_mode=pl.Buffered(1))


def _rows2d(w_ref):
    s, a, b = w_ref.shape
    return w_ref[...].reshape(s * a, b)


def _accumulate(ref, first, part):
    @pl.when(first)
    def _():
        ref[...] = part

    @pl.when(jnp.logical_not(first))
    def _():
        ref[...] += part


def _norm_bwd_store(dh, x_ref, g_ref, r_ref, dx_ref, dxb_ref, dg_ref, first):
    xv = x_ref[...]
    r = _rstd(xv)
    xh = xv * r
    dxh = dh * g_ref[...]
    dx = r * (dxh - xh * jnp.mean(dxh * xh, axis=-1, keepdims=True)) + r_ref[...]
    dx_ref[...] = dx
    dxb_ref[...] = dx.astype(BF16)
    _accumulate(dg_ref, first, jnp.sum(dh * xh, axis=0, keepdims=True))


def _mm_res(a_list, w, res, name, tm=512):
    T = res.shape[0]
    N = w.shape[-1]
    tm = _tile(T, tm, SUBLANES)
    n_a = len(a_list)

    def body(*refs):
        a_refs = refs[:n_a]
        w_ref, r_ref, o_ref = refs[n_a:]
        wv = _rows2d(w_ref)
        acc = r_ref[...]
        off = 0
        for a_ref in a_refs:
            k = a_ref.shape[1]
            acc = acc + _nn(a_ref[...], wv[off:off + k, :])
            off += k
        o_ref[...] = acc

    row = lambda n: pl.BlockSpec((tm, n), lambda i: (i, 0))
    return pl.pallas_call(
        body, name=name, grid=(T // tm,),
        in_specs=[row(a.shape[1]) for a in a_list] + [_full(w), row(N)], out_specs=row(N),
        out_shape=SDS((T, N), F32), compiler_params=_cp("parallel"))(*a_list, w, res)


def _mm_tn(a, b, name, out_dtype=BF16, tt=2048):
    T, K = a.shape
    N = b.shape[1]
    tt = _tile(T, tt, 2 * SUBLANES)
    nt = T // tt

    def body(a_ref, b_ref, o_ref, acc_ref):
        t = pl.program_id(0)
        _accumulate(acc_ref, t == 0, _tn(a_ref[...], b_ref[...]))

        @pl.when(t == nt - 1)
        def _():
            o_ref[...] = acc_ref[...].astype(o_ref.dtype)

    return pl.pallas_call(
        body, name=name, grid=(nt,),
        in_specs=[pl.BlockSpec((tt, K), lambda t: (t, 0)), pl.BlockSpec((tt, N), lambda t: (t, 0))],
        out_specs=pl.BlockSpec((K, N), lambda t: (0, 0)), out_shape=SDS((K, N), out_dtype),
        scratch_shapes=[pltpu.VMEM((K, N), F32)], compiler_params=_cp("arbitrary"))(a, b)


def _hosted_call(body, name, n_steps, in_specs, out_specs, out_shape, args, scatter=(), gather=(), scratch_shapes=()):
    scatter, gather, scratch_shapes = list(scatter), list(gather), list(scratch_shapes)
    n_s, n_g = len(scatter), len(gather)
    if n_s + n_g == 0:
        return pl.pallas_call(body, name=name, grid=(n_steps,), in_specs=list(in_specs), out_specs=list(out_specs),
                              out_shape=list(out_shape), scratch_shapes=scratch_shapes,
                              compiler_params=_cp("arbitrary"))(*args)
    n_in, n_out, n_scr = len(in_specs), len(out_specs), len(scratch_shapes)
    relay_step = min(max(1, 3 * n_steps // 4), n_steps - 1)

    def wrapped(*refs):
        ins, refs = refs[:n_in], refs[n_in:]
        xs, shards, refs = refs[:n_s], refs[n_s:n_s + n_g], refs[n_s + n_g:]
        outs, refs = refs[:n_out], refs[n_out:]
        recvs, gathered, refs = refs[:n_s], refs[n_s:n_s + n_g], refs[n_s + n_g:]
        outs, sems = outs + refs[:n_scr], refs[n_scr:]
        s_sems, g_sems = (sems[:3], sems[3:]) if n_s else ((), sems)
        step = pl.program_id(0)

        @pl.when(step == 0)
        def _():
            if n_s:
                _scatter_start(xs, recvs, *s_sems)
            if n_g:
                _gather_start(shards, gathered, *g_sems)
        body(*ins, *outs)
        if n_g:
            @pl.when(step == relay_step)
            def _():
                _gather_relay(shards, gathered, *g_sems)

        @pl.when(step == n_steps - 1)
        def _():
            if n_s:
                _scatter_wait(xs, recvs, *s_sems)
            if n_g:
                _gather_finish(shards, gathered, *g_sems)

    any_spec = pl.BlockSpec(memory_space=pl.ANY)
    sems = []
    for n in (n_s, n_g):
        if n:
            sems += [pltpu.SemaphoreType.DMA((n, N_DEV - 1)), pltpu.SemaphoreType.DMA((n, N_DEV - 1)),
                     pltpu.SemaphoreType.DMA((n,))]
    return pl.pallas_call(
        wrapped, name=name, grid=(n_steps,), in_specs=list(in_specs) + [any_spec] * (n_s + n_g),
        out_specs=list(out_specs) + [any_spec] * (n_s + n_g),
        out_shape=list(out_shape) + [SDS(b.shape, b.dtype) for b in scatter]
        + [SDS((N_DEV,) + b.shape, b.dtype) for b in gather],
        scratch_shapes=scratch_shapes + sems, compiler_params=_cp("arbitrary"))(*args, *scatter, *gather)


def _hidden_chunks(F):
    edges = list(range(0, F, 3 * MXU_COLS)) + [F]
    return list(zip(edges[:-1], edges[1:]))


def _ffn_gu(x, g, wgut, name, gather=(), tm=512):
    T, D = x.shape
    F = wgut.shape[0] // 2
    tm = _tile(T, tm, SUBLANES)

    def body(x_ref, g_ref, w_ref, ab_ref, act_ref, h_ref):
        xv = x_ref[...]
        h = (xv * _rstd(xv) * g_ref[...]).astype(BF16)
        h_ref[...] = h
        for c0, c1 in _hidden_chunks(F):
            gt = _nt(h, w_ref[c0:c1, :])
            up = _nt(h, w_ref[F + c0:F + c1, :])
            sg = jax.nn.sigmoid(gt)
            silu = gt * sg
            ab_ref[0, :, c0:c1] = (0.5 * up * (sg * (1.0 + gt * (1.0 - sg)))).astype(BF16)
            ab_ref[1, :, c0:c1] = (0.5 * silu).astype(BF16)
            act_ref[:, c0:c1] = (silu * up).astype(BF16)

    row = lambda n: pl.BlockSpec((tm, n), lambda i: (i, 0))
    return _hosted_call(
        body, name, T // tm, [row(D), _full(g), _full(wgut)],
        [pl.BlockSpec((2, tm, F), lambda i: (0, i, 0)), row(F), row(D)],
        [SDS((2, T, F), BF16), SDS((T, F), BF16), SDS((T, D), BF16)], (x, g, wgut), gather=gather)


def _ffn_down(act, wd, res, name, gather=(), tm=1024):
    T, F = act.shape
    D = wd.shape[2]
    tm = _tile(T, tm, SUBLANES)

    def body(a_ref, w_ref, r_ref, o_ref):
        o_ref[...] = r_ref[...] + 0.5 * _nn(a_ref[...], _rows2d(w_ref))

    row = lambda n: pl.BlockSpec((tm, n), lambda i: (i, 0))
    return _hosted_call(body, name, T // tm, [row(F), _full(wd), row(D)], [row(D)], [SDS((T, D), F32)],
                        (act, wd, res), gather=gather)


def _ffn_bwd_act(dyb, wd, ab, name, tm=512):
    T, D = dyb.shape
    _, _, F = ab.shape
    tm = _tile(T, tm, SUBLANES)

    def body(dy_ref, w_ref, ab_ref, o_ref):
        wv = _rows2d(w_ref)
        dy = dy_ref[...]
        for c0, c1 in _hidden_chunks(F):
            da = _nt(dy, wv[c0:c1, :])
            o_ref[0, :, c0:c1] = (da * ab_ref[0, :, c0:c1].astype(F32)).astype(BF16)
            o_ref[1, :, c0:c1] = (da * ab_ref[1, :, c0:c1].astype(F32)).astype(BF16)

    blocks = pl.BlockSpec((2, tm, F), lambda i: (0, i, 0))
    return pl.pallas_call(
        body, name=name, grid=(T // tm,),
        in_specs=[pl.BlockSpec((tm, D), lambda i: (i, 0)), _full(wd), blocks], out_specs=blocks,
        out_shape=SDS((2, T, F), BF16), compiler_params=_cp("parallel"))(dyb, wd, ab)


def _ffn_dwd(act, dyb, name, tt=2048):
    T, F = act.shape
    D = dyb.shape[1]
    tt = _tile(T, tt, 2 * SUBLANES)
    nt = T // tt
    fh = F // 2

    def body(a_ref, b_ref, o_ref, acc_ref):
        t = pl.program_id(1)
        _accumulate(acc_ref, t == 0, _tn(a_ref[...], b_ref[...]))

        @pl.when(t == nt - 1)
        def _():
            o_ref[...] = (0.5 * acc_ref[...]).astype(BF16)

    return pl.pallas_call(
        body, name=name, grid=(2, nt),
        in_specs=[pl.BlockSpec((tt, fh), lambda j, t: (t, j)), pl.BlockSpec((tt, D), lambda j, t: (t, 0))],
        out_specs=pl.BlockSpec((fh, D), lambda j, t: (j, 0)), out_shape=SDS((F, D), BF16),
        scratch_shapes=[pltpu.VMEM((fh, D), F32)], compiler_params=_cp("parallel", "arbitrary"))(act, dyb)


def _ffn_dwgu(h, dgu, name, tt=2048):
    T, D = h.shape
    _, _, F = dgu.shape
    tt = _tile(T, tt, 2 * SUBLANES)
    nt = T // tt
    fh = F // 2

    def body(a_ref, b_ref, o_ref, acc_ref):
        t = pl.program_id(1)
        _accumulate(acc_ref, t == 0, _tn(b_ref[...], a_ref[...]))

        @pl.when(t == nt - 1)
        def _():
            o_ref[...] = acc_ref[...].astype(BF16)

    return pl.pallas_call(
        body, name=name, grid=(4, nt),
        in_specs=[pl.BlockSpec((tt, D), lambda q, t: (t, 0)),
                  pl.BlockSpec((None, tt, fh), lambda q, t: (q // 2, t, q % 2))],
        out_specs=pl.BlockSpec((fh, D), lambda q, t: (q, 0)), out_shape=SDS((2 * F, D), BF16),
        scratch_shapes=[pltpu.VMEM((fh, D), F32)], compiler_params=_cp("parallel", "arbitrary"))(h, dgu)


def _ffn_bwd_in(wgut, x, g, dres, name, scatter, dgu=None, act_bwd=None):
    T, D = x.shape
    F = wgut.shape[0] // 2
    fh = F // 2
    tm = _tile(T, 512, SUBLANES)
    shapes = [SDS((T, D), F32), SDS((T, D), BF16), SDS((1, D), F32)]
    one_row = lambda i: (0, 0)

    if dgu is not None:
        def body(a_ref, w_ref, x_ref, g_ref, r_ref, dx_ref, dxb_ref, dg_ref):
            dh = _nn(a_ref[0], w_ref[:F, :]) + _nn(a_ref[1], w_ref[F:, :])
            _norm_bwd_store(dh, x_ref, g_ref, r_ref, dx_ref, dxb_ref, dg_ref, pl.program_id(0) == 0)

        row = pl.BlockSpec((tm, D), lambda i: (i, 0))
        return _hosted_call(
            body, name, T // tm, [pl.BlockSpec((2, tm, F), lambda i: (0, i, 0)), _full(wgut), row, _full(g), row],
            [row, row, pl.BlockSpec((1, D), one_row)], shapes, (dgu, wgut, x, g, dres), scatter)

    half_slots = N_DEV // 2

    def fused_body(dy_ref, wd_ref, ab_ref, w_ref, x_ref, g_ref, r_ref, dgu_ref, dx_ref, dxb_ref, dg_ref, dh_ref):
        step = pl.program_id(0)

        def half(k):
            wdv = wd_ref[k * half_slots:(k + 1) * half_slots].reshape(fh, D)
            dy = dy_ref[...]
            dh = None
            for c0, c1 in _hidden_chunks(fh):
                da = _nt(dy, wdv[c0:c1, :])
                for u in range(2):
                    d = (da * ab_ref[u, :, c0:c1].astype(F32)).astype(BF16)
                    dgu_ref[u, :, c0:c1] = d
                    part = _nn(d, w_ref[u * F + k * fh + c0:u * F + k * fh + c1, :])
                    dh = part if dh is None else dh + part
            return dh

        @pl.when(step % 2 == 0)
        def _():
            dh_ref[...] = half(0)

        @pl.when(step % 2 == 1)
        def _():
            _norm_bwd_store(dh_ref[...] + half(1), x_ref, g_ref, r_ref, dx_ref, dxb_ref, dg_ref, step == 1)

    dyb, wd, ab = act_bwd
    row = pl.BlockSpec((tm, D), lambda i: (i // 2, 0))
    hidden = pl.BlockSpec((2, tm, fh), lambda i: (0, i // 2, i % 2))
    return _hosted_call(
        fused_body, name, 2 * (T // tm), [row, _full(wd), hidden, _full(wgut), row, _full(g), row],
        [hidden, row, row, pl.BlockSpec((1, D), one_row)], [SDS((2, T, F), BF16)] + shapes,
        (dyb, wd, ab, wgut, x, g, dres), scatter, scratch_shapes=[pltpu.VMEM((tm, D), F32)])


def _loss_head(x, g, tgt, name, tm=512):
    T, D = x.shape
    tm = _tile(T, tm, SUBLANES)

    def body(x_ref, g_ref, t_ref, loss_ref, dx_ref, dxb_ref, dg_ref):
        first = pl.program_id(0) == 0
        xv = x_ref[...]
        r = _rstd(xv)
        xh = xv * r
        e = xh * g_ref[...] - t_ref[...]
        part = 0.5 * jnp.sum(jnp.mean(e * e, axis=-1, keepdims=True), axis=0, keepdims=True)
        dy = e * (1.0 / D)
        dxh = dy * g_ref[...]
        dx = r * (dxh - xh * jnp.mean(dxh * xh, axis=-1, keepdims=True))
        dx_ref[...] = dx
        dxb_ref[...] = dx.astype(BF16)
        _accumulate(loss_ref, first, jnp.broadcast_to(part, loss_ref.shape))
        _accumulate(dg_ref, first, jnp.sum(dy * xh, axis=0, keepdims=True))

    row = pl.BlockSpec((tm, D), lambda i: (i, 0))
    vec = pl.BlockSpec((1, D), lambda i: (0, 0))
    return pl.pallas_call(
        body, name=name, grid=(T // tm,), in_specs=[row, vec, row],
        out_specs=[pl.BlockSpec((SUBLANES, LANES), lambda i: (0, 0)), row, row, vec],
        out_shape=[SDS((SUBLANES, LANES), F32), SDS((T, D), F32), SDS((T, D), BF16), SDS((1, D), F32)],
        compiler_params=_cp("arbitrary"))(x, g, tgt)


def _shift_down(u, k, prev_row):
    rows = lax.broadcasted_iota(I32, u.shape, 0)
    s = pltpu.roll(u, k, 0)
    for t in range(k):
        s = jnp.where(rows == t, prev_row(SUBLANES - k + t), s)
    return s


def _shift_up(u, k, next_row):
    n = u.shape[0]
    rows = lax.broadcasted_iota(I32, u.shape, 0)
    s = pltpu.roll(u, n - k, 0)
    for t in range(k):
        s = jnp.where(rows == n - k + t, next_row(t), s)
    return s


def _conv_taps(z_ref, zp_ref, w_ref, cw, first):
    u = z_ref[:, cw:2 * cw] * z_ref[:, 2 * cw:]

    def prev_row(r):
        return jnp.where(first, 0.0, zp_ref[r:r + 1, cw:2 * cw] * zp_ref[r:r + 1, 2 * cw:])

    u1 = _shift_down(u, 1, prev_row)
    u2 = _shift_down(u, 2, prev_row)
    cv = w_ref[0:1, :] * u2 + w_ref[1:2, :] * u1 + w_ref[2:3, :] * u
    return u, u1, u2, cv


def _mix_in_conv(x, g, w, wc, gc, name, tm=512):
    T, D = x.shape
    cw = w.shape[0] // 3
    tm = _tile(T, tm, SUBLANES)

    def body(x_ref, g_ref, w_ref, wc_ref, gc_ref, z_ref, h_ref, o_ref, prev_ref):
        xv = x_ref[...]
        h = (xv * _rstd(xv) * g_ref[...]).astype(BF16)
        h_ref[...] = h
        z_ref[...] = _nt(h, w_ref[...])
        _, _, _, cv = _conv_taps(z_ref, prev_ref, wc_ref, cw, pl.program_id(0) == 0)
        y = z_ref[:, :cw] * cv
        o_ref[...] = (y * _rstd(y) * gc_ref[...]).astype(BF16)
        prev_ref[...] = z_ref[tm - SUBLANES:, :]

    row = lambda n: pl.BlockSpec((tm, n), lambda i: (i, 0))
    return pl.pallas_call(
        body, name=name, grid=(T // tm,), in_specs=[row(D), _full(g), _full(w), _full(wc), _full(gc)],
        out_specs=[row(3 * cw), row(D), row(cw)],
        out_shape=[SDS((T, 3 * cw), F32), SDS((T, D), BF16), SDS((T, cw), BF16)],
        scratch_shapes=[pltpu.VMEM((SUBLANES, 3 * cw), F32)], compiler_params=_cp("arbitrary"))(x, g, w, wc, gc)


def _mix_bwd_head(dyb, w_out, zc, wc, gc, ot, g_col, name, scatter):
    nb, aw, tm = ot.shape
    T, D = dyb.shape
    cw = zc.shape[1] // 3
    H = aw // HEAD_DIM
    hb = tm // SUBLANES

    def body(dyb_ref, wo_ref, z_ref, zp_ref, w_ref, g_ref, o_ref, ga_ref,
             dz_ref, dw_ref, dg_ref, do_ref, dl_ref, dga_ref, carry_ref):
        s = pl.program_id(0)
        dy = _nt(dyb_ref[...], _rows2d(wo_ref))
        ov = o_ref[...]
        ra = lax.rsqrt(jnp.mean(ov * ov, axis=0, keepdims=True) + EPS)
        oh = ov * ra
        dya = dy[:, cw:].T
        dyah = dya * ga_ref[...]
        dov = ra * (dyah - oh * jnp.mean(dyah * oh, axis=0, keepdims=True))
        do_ref[...] = dov.astype(BF16)
        dl_ref[...] = jnp.sum((dov * ov).reshape(H, HEAD_DIM, tm), axis=1)
        _accumulate(dga_ref, s == 0, jnp.broadcast_to(jnp.sum(dya * oh, axis=1, keepdims=True), dga_ref.shape))
        u, u1, u2, cv = _conv_taps(z_ref, zp_ref, w_ref, cw, s == nb - 1)
        zb = z_ref[:, :cw]
        y = zb * cv
        r = _rstd(y)
        yh = y * r
        dyn = dy[:, :cw]
        dyh = dyn * g_ref[...]
        dyc = r * (dyh - yh * jnp.mean(dyh * yh, axis=-1, keepdims=True))
        dcv = dyc * zb

        def next_row(t):
            return jnp.where(s == 0, 0.0, carry_ref[t:t + 1, :])

        du = w_ref[2:3, :] * dcv + w_ref[1:2, :] * _shift_up(dcv, 1, next_row) + w_ref[0:1, :] * _shift_up(dcv, 2, next_row)
        carry_ref[...] = dcv[0:SUBLANES, :]
        dz_ref[:, :cw] = (dyc * cv).astype(BF16)
        dz_ref[:, cw:2 * cw] = (du * z_ref[:, 2 * cw:]).astype(BF16)
        dz_ref[:, 2 * cw:] = (du * z_ref[:, cw:2 * cw]).astype(BF16)
        tap = lax.broadcasted_iota(I32, (SUBLANES, cw), 0)
        dwp = jnp.where(tap == 0, jnp.sum(dcv * u2, axis=0, keepdims=True),
                        jnp.where(tap == 1, jnp.sum(dcv * u1, axis=0, keepdims=True),
                                  jnp.where(tap == 2, jnp.sum(dcv * u, axis=0, keepdims=True), 0.0)))
        _accumulate(dw_ref, s == 0, dwp)
        _accumulate(dg_ref, s == 0, jnp.sum(dyn * yh, axis=0, keepdims=True))

    rev = lambda s: nb - 1 - s
    rows = lambda n: pl.BlockSpec((tm, n), lambda s: (rev(s), 0))
    blk = pl.BlockSpec((None, aw, tm), lambda s: (rev(s), 0, 0))
    return _hosted_call(
        body, name, nb,
        [rows(D), _full(w_out), rows(3 * cw),
         pl.BlockSpec((SUBLANES, 3 * cw), lambda s: (jnp.maximum(rev(s) * hb - 1, 0), 0)),
         _full(wc), _full(gc), blk, _full(g_col)],
        [rows(3 * cw), pl.BlockSpec((SUBLANES, cw), lambda s: (0, 0)), pl.BlockSpec((1, cw), lambda s: (0, 0)),
         blk, pl.BlockSpec((None, H, tm), lambda s: (rev(s), 0, 0)), pl.BlockSpec((aw, LANES), lambda s: (0, 0))],
        [SDS((T, 3 * cw), BF16), SDS((SUBLANES, cw), F32), SDS((1, cw), F32),
         SDS((nb, aw, tm), BF16), SDS((nb, H, tm), F32), SDS((aw, LANES), F32)],
        (dyb, w_out, zc, zc, wc, gc, ot, g_col), scatter=scatter, scratch_shapes=[pltpu.VMEM((SUBLANES, cw), F32)])


def _proj_t(h, wat, wft, B, name):
    T, D = h.shape
    R = wat.shape[0]
    nb = T // B
    qk = 2 * R // 3
    g = qk // HEAD_DIM

    def body(h_ref, wa_ref, wf_ref, za_ref, zf_ref, n_ref):
        hv = h_ref[...]
        zab = _nt(wa_ref[...], hv).astype(BF16)
        za_ref[...] = zab
        zf_ref[...] = _nt(wf_ref[...], hv)
        zv = zab[:qk].astype(F32)
        ss = jnp.sum((zv * zv).reshape(g, HEAD_DIM, B), axis=1)
        mx = jnp.broadcast_to(jnp.max(ss, axis=1, keepdims=True), n_ref.shape)

        @pl.when(pl.program_id(0) == 0)
        def _():
            n_ref[...] = mx

        @pl.when(pl.program_id(0) > 0)
        def _():
            n_ref[...] = jnp.maximum(n_ref[...], mx)

    return pl.pallas_call(
        body, name=name, grid=(nb,), in_specs=[pl.BlockSpec((B, D), lambda i: (i, 0)), _full(wat), _full(wft)],
        out_specs=[pl.BlockSpec((None, R, B), lambda i: (i, 0, 0)), pl.BlockSpec((SUBLANES, B), lambda i: (0, i)),
                   pl.BlockSpec((g, LANES), lambda i: (0, 0))],
        out_shape=[SDS((nb, R, B), BF16), SDS((SUBLANES, T), F32), SDS((g, LANES), F32)],
        compiler_params=_cp("arbitrary"))(h, wat, wft)


def _split3(v):
    hi = v.astype(BF16).astype(F32)
    r1 = v - hi
    mid = r1.astype(BF16).astype(F32)
    lo = (r1 - mid).astype(BF16).astype(F32)
    return hi, mid, lo


def _tri_dot3(v, tri):
    hi, mid, lo = _split3(v)
    return _nn(hi.astype(BF16), tri) + _nn(mid.astype(BF16), tri) + _nn(lo.astype(BF16), tri)


def _logf_cumsum(zft, b, tb, name):
    H, T = zft.shape

    def body(z_ref, b_ref, c_ref, qa_ref, ka_ref, carry_ref, hi_ref, mid_ref, lo_ref):
        @pl.when(pl.program_id(0) == 0)
        def _():
            carry_ref[...] = jnp.zeros_like(carry_ref)
        xv = z_ref[...] + b_ref[...]
        lf = jnp.minimum(xv, 0.0) - jnp.log(1.0 + jnp.exp(-jnp.abs(xv)))
        src = lax.broadcasted_iota(I32, (tb, tb), 0)
        dst = lax.broadcasted_iota(I32, (tb, tb), 1)
        tri = jnp.where(src <= dst, 1.0, 0.0).astype(BF16)
        cs = _tri_dot3(lf, tri) + carry_ref[...]
        c_ref[...] = cs
        hi_ref[...], mid_ref[...], lo_ref[...] = _split3(cs)
        col = lax.broadcasted_iota(I32, cs.shape, 1)
        carry_ref[...] = jnp.sum(jnp.where(col == tb - 1, cs, 0.0), axis=1, keepdims=True)
        row = lax.broadcasted_iota(I32, (AUG, tb), 0)
        for h in range(H):
            terms = [r[h:h + 1, :] for r in (hi_ref, mid_ref, lo_ref)]
            qa, ka = jnp.where(row < 3, 1.0, 0.0), jnp.where((row >= 3) & (row < 6), 1.0, 0.0)
            for k, t in enumerate(terms):
                qa = jnp.where(row == 3 + k, t, qa)
                ka = jnp.where(row == k, -t, ka)
            qa_ref[h] = qa
            ka_ref[h] = ka

    blk = pl.BlockSpec((H, tb), lambda i: (0, i))
    aug = pl.BlockSpec((H, None, AUG, tb), lambda i: (0, i, 0, 0))
    return pl.pallas_call(
        body, name=name, grid=(T // tb,), in_specs=[blk, _full(b)], out_specs=[blk, aug, aug],
        out_shape=[SDS((H, T), F32), SDS((H, T // tb, AUG, tb), F32), SDS((H, T // tb, AUG, tb), F32)],
        scratch_shapes=[pltpu.VMEM((H, 1), F32)] + [pltpu.VMEM((H, tb), F32)] * 3,
        compiler_params=_cp("arbitrary"))(zft, b)


def _logf_cumsum_bwd(dcq, dck, zft, b, name, tb=512):
    H, T = zft.shape
    tb = _tile(T, tb, LANES)
    nb = T // tb

    def body(dq_ref, dk_ref, z_ref, b_ref, o_ref, db_ref, carry_ref):
        s = pl.program_id(0)

        @pl.when(s == 0)
        def _():
            carry_ref[...] = jnp.zeros_like(carry_ref)
        dc = dq_ref[...] - dk_ref[...]
        src = lax.broadcasted_iota(I32, (tb, tb), 0)
        dst = lax.broadcasted_iota(I32, (tb, tb), 1)
        tri = jnp.where(src >= dst, 1.0, 0.0).astype(BF16)
        dl = _tri_dot3(dc, tri) + carry_ref[...]
        col = lax.broadcasted_iota(I32, dl.shape, 1)
        carry_ref[...] = jnp.sum(jnp.where(col == 0, dl, 0.0), axis=1, keepdims=True)
        dz = dl * jax.nn.sigmoid(-(z_ref[...] + b_ref[...]))
        o_ref[...] = dz
        _accumulate(db_ref, s == 0, jnp.sum(dz, axis=1, keepdims=True))

    blk = pl.BlockSpec((H, tb), lambda s: (0, nb - 1 - s))
    vec = pl.BlockSpec((H, 1), lambda s: (0, 0))
    return pl.pallas_call(
        body, name=name, grid=(nb,), in_specs=[blk, blk, blk, vec], out_specs=[blk, vec],
        out_shape=[SDS((H, T), F32), SDS((H, 1), F32)],
        scratch_shapes=[pltpu.VMEM((H, 1), F32)], compiler_params=_cp("arbitrary"))(dcq, dck, zft, b)


def _skip_table(cs_col, ce_row, norms, name):
    H, nb, _ = cs_col.shape

    def body(cs_ref, ce_ref, n_ref, jm_ref, im_ref):
        h = pl.program_id(0)
        nq = n_ref[pl.ds(h, 1), 0:1]
        nk = n_ref[pl.ds(H + h, 1), 0:1]
        bound = 2.0 * jnp.sqrt(nq * nk) * 0.125
        skip = jnp.where(bound + cs_ref[...] - ce_ref[...] <= -SKIP_MARGIN, 1, 0).astype(I32)
        jm_ref[...] = jnp.sum(skip, axis=1, keepdims=True)
        im_ref[...] = nb - 1 - jnp.sum(skip, axis=0, keepdims=True)

    col = pl.BlockSpec((None, nb, 1), lambda h: (h, 0, 0))
    row = pl.BlockSpec((None, 1, nb), lambda h: (h, 0, 0))
    return pl.pallas_call(
        body, name=name, grid=(H,), in_specs=[col, row, _full(norms)], out_specs=[col, row],
        out_shape=[SDS((H, nb, 1), I32), SDS((H, 1, nb), I32)], compiler_params=_cp("parallel"))(cs_col, ce_row, norms)


def _causal_mask(B):
    krow = lax.broadcasted_iota(I32, (B, B), 0)
    qcol = lax.broadcasted_iota(I32, (B, B), 1)
    return krow <= qcol


def _attn_fwd(jmin, zat, qa, ka, name):
    nb, R, B = zat.shape
    aw = R // 3
    H = aw // HEAD_DIM
    hd = HEAD_DIM

    def body(jm_ref, q_ref, k_ref, v_ref, qa_ref, ka_ref, o_ref, lse_ref):
        h = pl.program_id(0)
        i = pl.program_id(1)
        qt = jnp.concatenate([q_ref[...] * 0.125, qa_ref[...].astype(BF16)], axis=0)

        def keys(j):
            return jnp.concatenate([k_ref[j], ka_ref[j].astype(BF16)], axis=0)

        def update(s, vt, carry):
            m, l, acc = carry
            m_new = jnp.maximum(m, jnp.max(s, axis=0, keepdims=True))
            p = jnp.exp(s - m_new)
            a = jnp.exp(m - m_new)
            l = a * l + jnp.sum(p, axis=0, keepdims=True)
            acc = a * acc + _nn(vt, p.astype(BF16))
            return m_new, l, acc

        def block(j, carry, masked):
            s = _tn(keys(j), qt)
            if masked:
                s = jnp.where(_causal_mask(B), s, NEG)
            return update(s, v_ref[j], carry)

        def two_blocks(t, carry):
            j = j0 + 2 * t
            s = _tn(jnp.concatenate([keys(j), keys(j + 1)], axis=1), qt)
            return update(s, jnp.concatenate([v_ref[j], v_ref[j + 1]], axis=1), carry)

        j0 = jm_ref[h * nb + i]
        pairs = (i - j0) // 2
        init = (jnp.full((1, B), NEG, F32), jnp.zeros((1, B), F32), jnp.zeros((hd, B), F32))
        carry = lax.fori_loop(0, pairs, two_blocks, init)
        carry = lax.fori_loop(j0 + 2 * pairs, i, lambda j, c: block(j, c, False), carry)
        m, l, acc = block(i, carry, True)
        o_ref[...] = acc / l
        lse_ref[...] = m + jnp.log(l)

    grid_spec = pltpu.PrefetchScalarGridSpec(
        num_scalar_prefetch=1, grid=(H, nb),
        in_specs=[pl.BlockSpec((None, hd, B), lambda h, i, jm: (i, h, 0)),
                  pl.BlockSpec((nb, hd, B), lambda h, i, jm: (0, H + h, 0)),
                  pl.BlockSpec((nb, hd, B), lambda h, i, jm: (0, 2 * H + h, 0)),
                  pl.BlockSpec((None, None, AUG, B), lambda h, i, jm: (h, i, 0, 0)),
                  pl.BlockSpec((None, nb, AUG, B), lambda h, i, jm: (h, 0, 0, 0))],
        out_specs=[pl.BlockSpec((None, hd, B), lambda h, i, jm: (i, h, 0)),
                   pl.BlockSpec((None, None, 1, B), lambda h, i, jm: (h, i, 0, 0))])
    return pl.pallas_call(
        body, name=name, grid_spec=grid_spec,
        out_shape=[SDS((nb, aw, B), F32), SDS((H, nb, 1, B), F32)],
        compiler_params=_cp("parallel", "parallel"))(jmin, zat, zat, zat, qa, ka)


def _gnorm_t_fwd(ot, g_col, name):
    nb, aw, B = ot.shape

    def body(o_ref, g_ref, y_ref):
        ov = o_ref[...]
        r = lax.rsqrt(jnp.mean(ov * ov, axis=0, keepdims=True) + EPS)
        y_ref[...] = (ov * r * g_ref[...]).T.astype(BF16)

    return pl.pallas_call(
        body, name=name, grid=(nb,), in_specs=[pl.BlockSpec((None, aw, B), lambda i: (i, 0, 0)), _full(g_col)],
        out_specs=pl.BlockSpec((B, aw), lambda i: (i, 0)), out_shape=SDS((nb * B, aw), BF16),
        compiler_params=_cp("parallel"))(ot, g_col)


def _attn_bwd(imax, zat, qa, ka, dot, lse, delta, name):
    nb, R, B = zat.shape
    aw = R // 3
    H = aw // HEAD_DIM
    hd = HEAD_DIM

    def body(im_ref, k_ref, v_ref, ka_ref, q_ref, qa_ref, do_ref, lse_ref, dl_ref,
             dq_ref, dqa_ref, dk_ref, dka_ref, dv_ref, dq_acc):
        h = pl.program_id(0)
        j = pl.program_id(1)

        @pl.when(j == 0)
        def _():
            dq_acc[...] = jnp.zeros_like(dq_acc)
            dqa_ref[...] = jnp.zeros_like(dqa_ref)
        kt = jnp.concatenate([k_ref[...], ka_ref[...].astype(BF16)], axis=0)
        vt = v_ref[...]

        def queries(i):
            return jnp.concatenate([q_ref[i] * 0.125, qa_ref[i].astype(BF16)], axis=0)

        def grads(qt, dov, lse, dl, carry, masked):
            dk, dv = carry
            s = _tn(kt, qt)
            if masked:
                s = jnp.where(_causal_mask(B), s, NEG)
            p = jnp.exp(s - lse)
            dp = _tn(vt, dov)
            ds = (p * (dp - dl)).astype(BF16)
            return (dk + _nt(qt, ds), dv + _nt(dov, p.astype(BF16))), _nn(kt, ds)

        def add_dq(i, r):
            dq_acc[i] += 0.125 * r[:hd]
            dqa_ref[i] += r[hd:]

        def block(i, carry, masked):
            carry, r = grads(queries(i), do_ref[i], lse_ref[i], dl_ref[i, pl.ds(h, 1), :], carry, masked)
            add_dq(i, r)
            return carry

        def two_blocks(t, carry):
            i = j + 1 + 2 * t
            both = lambda f: jnp.concatenate([f(i), f(i + 1)], axis=1)
            carry, r = grads(both(queries), both(lambda n: do_ref[n]), both(lambda n: lse_ref[n]),
                             both(lambda n: dl_ref[n, pl.ds(h, 1), :]), carry, False)
            add_dq(i, r[:, :B])
            add_dq(i + 1, r[:, B:])
            return carry

        last = im_ref[h * nb + j]
        pairs = (last - j) // 2
        carry = block(j, (jnp.zeros((hd + AUG, B), F32), jnp.zeros((hd, B), F32)), True)
        carry = lax.fori_loop(0, pairs, two_blocks, carry)
        dk, dv = lax.fori_loop(j + 1 + 2 * pairs, last + 1, lambda i, c: block(i, c, False), carry)
        dk_ref[...] = dk[:hd].astype(BF16)
        dka_ref[...] = dk[hd:]
        dv_ref[...] = dv.astype(BF16)

        @pl.when(j == nb - 1)
        def _():
            dq_ref[...] = dq_acc[...].astype(BF16)

    head_rows = lambda off: pl.BlockSpec((nb, hd, B), lambda h, j, im: (0, off + h, 0))
    key_rows = lambda off: pl.BlockSpec((None, hd, B), lambda h, j, im: (j, off + h, 0))
    aug_all = pl.BlockSpec((None, nb, AUG, B), lambda h, j, im: (h, 0, 0, 0))
    aug_one = pl.BlockSpec((None, None, AUG, B), lambda h, j, im: (h, j, 0, 0))
    grid_spec = pltpu.PrefetchScalarGridSpec(
        num_scalar_prefetch=1, grid=(H, nb),
        in_specs=[key_rows(H), key_rows(2 * H), aug_one, head_rows(0), aug_all, head_rows(0),
                  pl.BlockSpec((None, nb, 1, B), lambda h, j, im: (h, 0, 0, 0)),
                  pl.BlockSpec((nb, H, B), lambda h, j, im: (0, 0, 0))],
        out_specs=[head_rows(0), aug_all, key_rows(0), aug_one, key_rows(0)],
        scratch_shapes=[pltpu.VMEM((nb, hd, B), F32)])
    return pl.pallas_call(
        body, name=name, grid_spec=grid_spec,
        out_shape=[SDS((nb, aw, B), BF16), SDS((H, nb, AUG, B), F32), SDS((nb, aw, B), BF16),
                   SDS((H, nb, AUG, B), F32), SDS((nb, aw, B), BF16)],
        compiler_params=_cp("arbitrary", "arbitrary"))(imax, zat, zat, ka, zat, qa, dot, lse, delta)


def _mix_dwt(dq, dk, dv, dzft, h, name):
    nb, aw, B = dq.shape
    D = h.shape[1]

    def body(dq_ref, dk_ref, dv_ref, dzf_ref, h_ref, o_ref, of_ref):
        first = pl.program_id(0) == 0
        hv = h_ref[...]
        part = jnp.concatenate([_nn(r[...], hv) for r in (dq_ref, dk_ref, dv_ref)], axis=0)
        _accumulate(o_ref, first, part)
        _accumulate(of_ref, first, _nn(dzf_ref[...].astype(BF16), hv))

    blk = pl.BlockSpec((None, aw, B), lambda i: (i, 0, 0))
    return pl.pallas_call(
        body, name=name, grid=(nb,),
        in_specs=[blk, blk, blk, pl.BlockSpec((SUBLANES, B), lambda i: (0, i)), pl.BlockSpec((B, D), lambda i: (i, 0))],
        out_specs=[pl.BlockSpec((3 * aw, D), lambda i: (0, 0)), pl.BlockSpec((SUBLANES, D), lambda i: (0, 0))],
        out_shape=[SDS((3 * aw, D), F32), SDS((SUBLANES, D), F32)],
        compiler_params=_cp("arbitrary"))(dq, dk, dv, dzft, h)


def _mix_bwd_in(dzc, dq, dk, dv, dzft, wc, wat, wft, x, g, dres, name, scatter):
    T, D = x.shape
    nb, aw, B = dq.shape

    def body(dzc_ref, dq_ref, dk_ref, dv_ref, dzf_ref, wc_ref, wa_ref, wf_ref, x_ref, g_ref, r_ref,
             dx_ref, dxb_ref, dg_ref):
        dh = _nn(dzc_ref[...], wc_ref[...])
        for s, r in enumerate((dq_ref, dk_ref, dv_ref)):
            dh = dh + _tn(r[...], wa_ref[s * aw:(s + 1) * aw, :])
        dh = dh + _tn(dzf_ref[...].astype(BF16), wf_ref[...])
        _norm_bwd_store(dh, x_ref, g_ref, r_ref, dx_ref, dxb_ref, dg_ref, pl.program_id(0) == 0)

    blk = pl.BlockSpec((None, aw, B), lambda i: (i, 0, 0))
    row = lambda n: pl.BlockSpec((B, n), lambda i: (i, 0))
    return _hosted_call(
        body, name, nb,
        [row(dzc.shape[1]), blk, blk, blk, pl.BlockSpec((SUBLANES, B), lambda i: (0, i)),
         _full(wc), _full(wat), _full(wft), row(D), _full(g), row(D)],
        [row(D), row(D), pl.BlockSpec((1, D), lambda i: (0, 0))],
        [SDS((T, D), F32), SDS((T, D), BF16), SDS((1, D), F32)],
        (dzc, dq, dk, dv, dzft, wc, wat, wft, x, g, dres), scatter)


def _mem_kv(mem, g, wkv, name):
    M, D = mem.shape
    dx = wkv.shape[2]

    def body(m_ref, g_ref, w_ref, kv_ref, h_ref):
        mv = m_ref[...]
        h = (mv * _rstd(mv) * g_ref[...]).astype(BF16)
        h_ref[...] = h
        for s in range(N_DEV):
            kv_ref[s] = _nn(h, w_ref[s]).astype(BF16)

    return pl.pallas_call(
        body, name=name, grid=(1,), in_specs=[_full(mem), _full(g), _full(wkv)],
        out_specs=[pl.BlockSpec((N_DEV, M, dx), lambda i: (0, 0, 0)), pl.BlockSpec((M, D), lambda i: (0, 0))],
        out_shape=[SDS((N_DEV, M, dx), BF16), SDS((M, D), BF16)], compiler_params=_cp("arbitrary"))(mem, g, wkv)


def _mem_kv_bwd(dkv, hm, wkv, mem, g, name):
    M, D = mem.shape
    dx = wkv.shape[2]

    def body(dkv_ref, h_ref, w_ref, m_ref, g_ref, dw_ref, dg_ref):
        hv = h_ref[...]
        dh = jnp.zeros((M, D), F32)
        for s in range(N_DEV):
            d = dkv_ref[s].astype(BF16)
            dw_ref[s] = _tn(hv, d).astype(BF16)
            dh = dh + _nt(d, w_ref[s])
        mv = m_ref[...]
        dg_ref[...] = jnp.sum(dh * (mv * _rstd(mv)), axis=0, keepdims=True)

    return pl.pallas_call(
        body, name=name, grid=(1,), in_specs=[_full(dkv), _full(hm), _full(wkv), _full(mem), _full(g)],
        out_specs=[pl.BlockSpec((N_DEV, D, dx), lambda i: (0, 0, 0)), pl.BlockSpec((1, D), lambda i: (0, 0))],
        out_shape=[SDS((N_DEV, D, dx), BF16), SDS((1, D), F32)], compiler_params=_cp("arbitrary"))(dkv, hm, wkv, mem, g)


def _xattn_probs(q_ref, kv_ref, h, dx, scale):
    qh = q_ref[:, h * dx:(h + 1) * dx]
    kh = kv_ref[h]
    s = _nt(qh, kh) * scale
    p = jnp.exp(s - jnp.max(s, axis=-1, keepdims=True))
    return qh, kh, p / jnp.sum(p, axis=-1, keepdims=True)


def _xattn(x, g, wq, kv, wo, name, tm=512):
    T, D = x.shape
    dx = D // N_XHEADS
    scale = dx ** -0.5
    tm = _tile(T, tm, SUBLANES)

    def body(x_ref, g_ref, wq_ref, kv_ref, wo_ref, xo_ref, h_ref, q_ref, o_ref):
        xv = x_ref[...]
        h = (xv * _rstd(xv) * g_ref[...]).astype(BF16)
        h_ref[...] = h
        q_ref[...] = _nn(h, _rows2d(wq_ref)).astype(BF16)
        for hd in range(N_XHEADS):
            _, _, p = _xattn_probs(q_ref, kv_ref, hd, dx, scale)
            o_ref[:, hd * dx:(hd + 1) * dx] = _nn(p.astype(BF16), kv_ref[N_XHEADS + hd]).astype(BF16)
        xo_ref[...] = xv + _nn(o_ref[...], _rows2d(wo_ref))

    row = pl.BlockSpec((tm, D), lambda i: (i, 0))
    return pl.pallas_call(
        body, name=name, grid=(T // tm,), in_specs=[row, _full(g), _full(wq), _full(kv), _full(wo)],
        out_specs=[row] * 4, out_shape=[SDS((T, D), F32)] + [SDS((T, D), BF16)] * 3,
        compiler_params=_cp("parallel"))(x, g, wq, kv, wo)


def _xattn_bwd(dyb, dres, x, g, q, kv, wq, wo, name, scatter, tm=512):
    T, D = x.shape
    dx = D // N_XHEADS
    scale = dx ** -0.5
    tm = _tile(T, tm, SUBLANES)

    def body(dy_ref, r_ref, x_ref, g_ref, q_ref, kv_ref, wq_ref, wo_ref, dx_ref, dxb_ref, dg_ref, dq_ref, dkv_ref):
        first = pl.program_id(0) == 0

        @pl.when(first)
        def _():
            dkv_ref[...] = jnp.zeros_like(dkv_ref)
        do = _nt(dy_ref[...], _rows2d(wo_ref)).astype(BF16)
        for hd in range(N_XHEADS):
            qh, kh, p = _xattn_probs(q_ref, kv_ref, hd, dx, scale)
            doh = do[:, hd * dx:(hd + 1) * dx]
            dp = _nt(doh, kv_ref[N_XHEADS + hd])
            ds = (p * (dp - jnp.sum(p * dp, axis=-1, keepdims=True)) * scale).astype(BF16)
            dq_ref[:, hd * dx:(hd + 1) * dx] = _nn(ds, kh).astype(BF16)
            dkv_ref[hd] += _tn(ds, qh)
            dkv_ref[N_XHEADS + hd] += _tn(p.astype(BF16), doh)
        dh = _nt(dq_ref[...], _rows2d(wq_ref))
        _norm_bwd_store(dh, x_ref, g_ref, r_ref, dx_ref, dxb_ref, dg_ref, first)

    row = pl.BlockSpec((tm, D), lambda i: (i, 0))
    return _hosted_call(
        body, name, T // tm, [row, row, row, _full(g), row, _full(kv), _full(wq), _full(wo)],
        [row, row, pl.BlockSpec((1, D), lambda i: (0, 0)), row, pl.BlockSpec(kv.shape, lambda i: (0, 0, 0))],
        [SDS((T, D), F32), SDS((T, D), BF16), SDS((1, D), F32), SDS((T, D), BF16), SDS(kv.shape, F32)],
        (dyb, dres, x, g, q, kv, wq, wo), scatter)


def _mesh_position():
    return lax.axis_index("x"), lax.axis_index("y"), lax.axis_index("c")


class _GatherCopies:
    def __init__(self, x_refs, out_refs, send_sems, recv_sems, local_sems):
        self.x_refs, self.out_refs, self.n = x_refs, out_refs, len(x_refs)
        self.send_sems, self.recv_sems, self.local_sems = send_sems, recv_sems, local_sems
        x, y, c = _mesh_position()
        self.c = c
        self.me, self.sibling = (x, y, c), (x, y, 1 - c)
        self.chips = [(1 - x, y), (x, 1 - y), (1 - x, 1 - y)]

    def slot(self, w, px, py, pc):
        return self.out_refs[w].at[4 * px + 2 * py + pc]

    def copy(self, w, k, block, to, src=None):
        return pltpu.make_async_remote_copy(
            src_ref=self.slot(w, *block) if src is None else src, dst_ref=self.slot(w, *block),
            send_sem=self.send_sems.at[w, k], recv_sem=self.recv_sems.at[w, k], device_id=to,
            device_id_type=pl.DeviceIdType.MESH)

    def mine(self, w):
        return pltpu.make_async_copy(self.x_refs[w], self.slot(w, *self.me), self.local_sems.at[w])

    def first(self, w):
        src = self.x_refs[w]
        return [self.copy(w, 0, self.me, self.sibling, src=src)] + [
            self.copy(w, 1 + j, self.me, (*chip, self.c), src=src) for j, chip in enumerate(self.chips)]

    def passed(self, w):
        return [self.copy(w, 4 + j, (*chip, self.c), self.sibling) for j, chip in enumerate(self.chips)]


def _gather_start(*refs):
    cp = _GatherCopies(*refs)
    for w in range(cp.n):
        cp.mine(w).start()
        for f in cp.first(w):
            f.start()


def _gather_relay(*refs):
    cp = _GatherCopies(*refs)
    for w in range(cp.n):
        for j, (chip, fwd) in enumerate(zip(cp.chips, cp.passed(w))):
            cp.copy(w, 1 + j, (*chip, cp.c), cp.me).wait_recv()
            fwd.start()


def _gather_finish(*refs):
    cp = _GatherCopies(*refs)
    for w in range(cp.n):
        cp.copy(w, 0, cp.sibling, cp.me).wait_recv()
        for j, chip in enumerate(cp.chips):
            cp.copy(w, 4 + j, (*chip, 1 - cp.c), cp.me).wait_recv()
    for w in range(cp.n):
        for f in cp.first(w) + cp.passed(w):
            f.wait_send()
        cp.mine(w).wait()


def _all_gather(shards, name):
    n = len(shards)

    def body(*refs):
        parts = (refs[:n], refs[n:2 * n]) + refs[2 * n:]
        _gather_start(*parts)
        _gather_relay(*parts)
        _gather_finish(*parts)

    any_spec = pl.BlockSpec(memory_space=pl.ANY)
    return pl.pallas_call(
        body, name=name, out_shape=[SDS((N_DEV,) + s.shape, s.dtype) for s in shards],
        in_specs=[any_spec] * n, out_specs=[any_spec] * n,
        scratch_shapes=[pltpu.SemaphoreType.DMA((n, 7)), pltpu.SemaphoreType.DMA((n, 7)), pltpu.SemaphoreType.DMA((n,))],
    )(*shards)


def _scatter_copies(x_refs, out_refs, send_sems, recv_sems, local_sems):
    x, y, c = _mesh_position()
    me = 4 * x + 2 * y + c
    own, remote = [], []
    for w, (src, dst) in enumerate(zip(x_refs, out_refs)):
        own.append(pltpu.make_async_copy(src.at[me], dst.at[me], local_sems.at[w]))
        for k in range(1, N_DEV):
            px = 1 - x if k & 4 else x
            py = 1 - y if k & 2 else y
            pc = 1 - c if k & 1 else c
            remote.append(pltpu.make_async_remote_copy(
                src_ref=src.at[4 * px + 2 * py + pc], dst_ref=dst.at[me], send_sem=send_sems.at[w, k - 1],
                recv_sem=recv_sems.at[w, k - 1], device_id=(px, py, pc), device_id_type=pl.DeviceIdType.MESH))
    return own, remote


def _scatter_start(x_refs, out_refs, send_sems, recv_sems, local_sems):
    own, remote = _scatter_copies(x_refs, out_refs, send_sems, recv_sems, local_sems)
    for cp in own + remote:
        cp.start()


def _scatter_wait(x_refs, out_refs, send_sems, recv_sems, local_sems):
    own, remote = _scatter_copies(x_refs, out_refs, send_sems, recv_sems, local_sems)
    for cp in remote:
        cp.wait_recv()
    for cp in remote:
        cp.wait_send()
    for cp in own:
        cp.wait()


def _sum_adamw(parts, w, m, v, name, tr=256):
    L, R, C = w.shape
    tr = _tile(R, tr, SUBLANES)
    nblk = R // tr

    def body(*refs):
        p_refs = refs[:L]
        w_ref, m_ref, v_ref, g_ref, d_ref, mo_ref, vo_ref = refs[L:]
        layer = pl.program_id(0)
        g = None
        for k, p_ref in enumerate(p_refs):
            gk = p_ref[0].astype(F32)
            for s in range(1, N_DEV):
                gk = gk + p_ref[s].astype(F32)
            g = gk if g is None else jnp.where(layer == k, gk, g)
        mn = ADAM_B1 * m_ref[...] + (1.0 - ADAM_B1) * g
        vn = ADAM_B2 * v_ref[...] + (1.0 - ADAM_B2) * jnp.square(g)
        m_hat = mn / (1.0 - ADAM_B1 ** ADAM_STEP)
        v_hat = vn / (1.0 - ADAM_B2 ** ADAM_STEP)
        g_ref[...] = g
        d_ref[...] = -ADAM_LR * (m_hat / (jnp.sqrt(v_hat) + ADAM_EPS) + ADAM_WD * w_ref[...])
        mo_ref[...] = mn
        vo_ref[...] = vn

    def part_spec(k):
        return pl.BlockSpec((N_DEV, tr, C),
                            lambda l, i: (0, jnp.where(l == k, i, jnp.where(l < k, 0, nblk - 1)), 0))

    row = pl.BlockSpec((None, tr, C), lambda l, i: (l, i, 0))
    return pl.pallas_call(
        body, name=name, grid=(L, nblk), in_specs=[part_spec(k) for k in range(L)] + [row, row, row],
        out_specs=[row] * 4, out_shape=[SDS((L, R, C), F32)] * 4,
        compiler_params=_cp("arbitrary", "arbitrary"))(*parts, w, m, v)


def _pack(flat_list, row_mult):
    flat = jnp.concatenate(flat_list, axis=-1)
    n = flat.shape[-1]
    chunk = row_mult * PACK_COLS
    pad = -n % chunk
    flat = jnp.pad(flat, [(0, 0)] * (flat.ndim - 1) + [(0, pad)])
    return flat.reshape(flat.shape[:-1] + ((n + pad) // PACK_COLS, PACK_COLS))


def _unpack(packed, shapes):
    lead = packed.shape[:-2]
    flat = packed.reshape(lead + (-1,))
    out = []
    off = 0
    for shp in shapes:
        n = 1
        for d in shp:
            n *= d
        out.append(flat[..., off:off + n].reshape(lead + tuple(shp)))
        off += n
    return out


def _ffn_bwd(dx, dxb, saved, wgu, wd, tag, also_scatter, send_later):
    x, g, h, ab, act = saved
    D = x.shape[1]
    dwd = _ffn_dwd(act, dxb, f"{tag}_dwd").reshape(N_DEV, -1, D)
    if send_later:
        dgu, dx, dxb, dg, *p_also = _ffn_bwd_in(wgu, x, g, dx, f"{tag}_bwd_in", also_scatter, act_bwd=(dxb, wd, ab))
        return dx, dxb, dg, [_ffn_dwgu(h, dgu, f"{tag}_dwgu").reshape(N_DEV, -1, D), dwd], p_also
    dgu = _ffn_bwd_act(dxb, wd, ab, f"{tag}_bwd_act")
    dwgu = _ffn_dwgu(h, dgu, f"{tag}_dwgu").reshape(N_DEV, -1, D)
    dx, dxb, dg, p_gu, p_d, *p_also = _ffn_bwd_in(wgu, x, g, dx, f"{tag}_bwd_in", [dwgu, dwd] + list(also_scatter), dgu=dgu)
    return dx, dxb, dg, [p_gu, p_d], p_also


def _mix_fwd(x, g, w_in8, w_conv8, b_f, g_conv, g_att, w_out, tag):
    T, D = x.shape
    w_int = w_in8.reshape(-1, D)
    cw = D // 2
    aw = D - cw
    H = aw // HEAD_DIM
    B = _tile(T, ATT_BLOCK, LANES)
    nb = T // B
    w_c = w_int[:3 * cw]
    w_at = w_int[3 * cw:3 * cw + 3 * aw]
    w_ft = jnp.pad(w_int[3 * cw + 3 * aw:], ((0, SUBLANES - H), (0, 0)))
    zc, h, ync = _mix_in_conv(x, g, w_c, w_conv8, g_conv, f"{tag}_in_conv")
    zat, zft8, norms = _proj_t(h, w_at, w_ft, B, f"{tag}_in_att")
    zft = zft8[:H]
    bcol = b_f.reshape(H, 1)
    c, qa, ka = _logf_cumsum(zft, bcol, B, f"{tag}_cumsum")
    cb = c.reshape(H, nb, B)
    jmin, imax = _skip_table(cb[:, :, 0:1], cb[:, :, B - 1].reshape(H, 1, nb), norms, f"{tag}_skip")
    ot, lse = _attn_fwd(jmin.reshape(H * nb), zat, qa, ka, f"{tag}_attn")
    g_col = g_att.reshape(aw, 1)
    yna = _gnorm_t_fwd(ot, g_col, f"{tag}_gnorm")
    xo = _mm_res([ync, yna], w_out, x, f"{tag}_out")
    saved = (x, g, h, zc, zft, bcol, zat, qa, ka, imax.reshape(H * nb), ot, lse, ync, yna, w_c, w_at, w_ft, g_col)
    return xo, saved


def _mix_bwd(dx, dxb, saved, w_conv8, g_conv, w_out, tag, also_scatter):
    x, g, h, zc, zft, bcol, zat, qa, ka, imax, ot, lse, ync, yna, w_c, w_at, w_ft, g_col = saved
    T, D = x.shape
    H = zft.shape[0]
    dw_out = jnp.concatenate([_mm_tn(ync, dxb, f"{tag}_dwout_c"), _mm_tn(yna, dxb, f"{tag}_dwout_a")], axis=0)
    dzc, dwc8, dg_conv, dot, delta, dg_att, *p_also = _mix_bwd_head(dxb, w_out, zc, w_conv8, g_conv, ot, g_col,
                                                                   f"{tag}_bwd_out", also_scatter)
    dq, dqa, dk, dka, dv = _attn_bwd(imax, zat, qa, ka, dot, lse, delta, f"{tag}_attn_bwd")
    dcq = dqa[:, :, 3, :].reshape(H, T)
    dck = dka[:, :, 0, :].reshape(H, T)
    dzft, db = _logf_cumsum_bwd(dcq, dck, zft, bcol, f"{tag}_cumsum_bwd")
    dzft8 = jnp.pad(dzft, ((0, SUBLANES - H), (0, 0)))
    dw_c = _mm_tn(dzc, h, f"{tag}_dwin_c")
    dw_at, dw_ft = _mix_dwt(dq, dk, dv, dzft8, h, f"{tag}_dwin_a")
    dw_int = jnp.concatenate([dw_c, dw_at.astype(BF16), dw_ft[:H].astype(BF16)], axis=0)
    scatter = [dw_int.reshape(N_DEV, -1, D), dwc8[:3].reshape(3, N_DEV, -1).transpose(1, 0, 2),
               dw_out.reshape(N_DEV, D // N_DEV, D)]
    dx, dxb, dg, p_in, p_conv, p_out = _mix_bwd_in(dzc, dq, dk, dv, dzft8, w_c, w_at, w_ft, x, g, dx, f"{tag}_bwd_in",
                                                   scatter)
    return dx, dxb, dg, p_in, p_conv, db.reshape(H), dg_conv, dg_att[:, 0], p_out, p_also


def _xattn_block_fwd(x, g, mem, g_mem, w_q, w_kv, w_o, tag):
    kv, hm = _mem_kv(mem, g_mem, w_kv, f"{tag}_kv")
    xo, h, q, o = _xattn(x, g, w_q, kv, w_o, f"{tag}_block")
    return xo, (x, g, h, mem, g_mem, hm, q, kv, o)


def _xattn_block_bwd(dx, dxb, saved, w_q, w_kv, w_o, tag, also_scatter):
    x, g, h, mem, g_mem, hm, q, kv, o = saved
    dw_o = _mm_tn(o, dxb, f"{tag}_dwo")
    dx, dxb, dg, dq, dkv, *p_also = _xattn_bwd(dxb, dx, x, g, q, kv, w_q, w_o, f"{tag}_block_bwd", also_scatter)
    dw_q = _mm_tn(h, dq, f"{tag}_dwq")
    dw_kv, dg_mem = _mem_kv_bwd(dkv, hm, w_kv, mem, g_mem, f"{tag}_kv_bwd")
    D = dw_q.shape[0]
    return (dx, dxb, dg, dg_mem, [dw_q.reshape(N_DEV, D // N_DEV, D), dw_kv, dw_o.reshape(N_DEV, D // N_DEV, D)],
            p_also)


GATHER_FIRST = ['w_ffn1_gu']
GATHER_REST = ['w_ffn1_down', 'w_mix_in', 'w_mix_out', 'w_xq', 'w_xkv', 'w_xo', 'w_ffn2_gu', 'w_ffn2_down']
GATHER_NEXT_A = ['w_ffn1_gu', 'w_ffn1_down', 'w_mix_in', 'w_mix_out', 'w_xq', 'w_xkv', 'w_xo']
GATHER_NEXT_B = ['w_ffn2_gu', 'w_ffn2_down']


def _local_step(x, mem, tgt, P):
    L = P['g_ffn1'].shape[0]
    shards = lambda names, l: [(P[n][l].T if n in LONG_AXIS_LAST else P[n][l]).astype(BF16) for n in names]
    wgut = lambda n, l: W[n][l].reshape(-1, W[n][l].shape[-1])
    W = {n: [None] * L for n in SHARDED}

    def keep(names, l, arrays):
        for n, a in zip(names, arrays):
            W[n][l] = a

    *first, conv = _all_gather(shards(GATHER_FIRST, 0) + [P['w_conv']], "gather_first")
    keep(GATHER_FIRST, 0, first)
    w_conv = conv.transpose(1, 2, 0, 3).reshape(L, 3, -1)
    saved = []
    for l in range(L):
        row = lambda n: P[n][l][None, :]
        wc8 = jnp.pad(w_conv[l], ((0, SUBLANES - 3), (0, 0)))
        x1 = x
        bring = GATHER_REST if l == 0 else []
        ab, act, h, *got = _ffn_gu(x, row('g_ffn1'), wgut('w_ffn1_gu', l), f"l{l}_ffn1_gu", gather=shards(bring, l))
        keep(bring, l, got)
        x, = _ffn_down(act, W['w_ffn1_down'][l], x, f"l{l}_ffn1_down")
        s1 = (x1, row('g_ffn1'), h, ab, act)
        x, s2 = _mix_fwd(x, row('g_mix'), W['w_mix_in'][l], wc8, P['b_f'][l], row('g_conv_out'), P['g_att_out'][l],
                         W['w_mix_out'][l], f"l{l}_mix")
        x, s3 = _xattn_block_fwd(x, row('g_xattn'), mem, row('g_mem'), W['w_xq'][l], W['w_xkv'][l], W['w_xo'][l],
                                 f"l{l}_xattn")
        x4 = x
        bring_a, bring_b = (GATHER_NEXT_A, GATHER_NEXT_B) if l + 1 < L else ([], [])
        ab, act, h, *got = _ffn_gu(x, row('g_ffn2'), wgut('w_ffn2_gu', l), f"l{l}_ffn2_gu", gather=shards(bring_a, l + 1))
        keep(bring_a, l + 1, got)
        x, *got = _ffn_down(act, W['w_ffn2_down'][l], x, f"l{l}_ffn2_down", gather=shards(bring_b, l + 1))
        keep(bring_b, l + 1, got)
        saved.append((s1, s2, s3, (x4, row('g_ffn2'), h, ab, act), wc8))
    loss, dx, dxb, dg_final = _loss_head(x, P['g_final'][None, :], tgt, "loss_head")
    G = {n: [None] * L for n in WEIGHT_NAMES if n != 'g_final'}
    unsent = []
    for l in reversed(range(L)):
        row = lambda n: P[n][l][None, :]
        s1, s2, s3, s4, wc8 = saved[l]
        dx, dxb, G['g_ffn2'][l], ffn2_grads, got = _ffn_bwd(
            dx, dxb, s4, wgut('w_ffn2_gu', l), W['w_ffn2_down'][l], f"l{l}_ffn2", unsent, True)
        if unsent:
            G['w_ffn1_gu'][l + 1], G['w_ffn1_down'][l + 1] = got
        dx, dxb, G['g_xattn'][l], G['g_mem'][l], xattn_grads, (G['w_ffn2_gu'][l], G['w_ffn2_down'][l]) = _xattn_block_bwd(
            dx, dxb, s3, W['w_xq'][l], W['w_xkv'][l], W['w_xo'][l], f"l{l}_xattn", ffn2_grads)
        (dx, dxb, G['g_mix'][l], G['w_mix_in'][l], G['w_conv'][l], G['b_f'][l], G['g_conv_out'][l], G['g_att_out'][l],
         G['w_mix_out'][l], (G['w_xq'][l], G['w_xkv'][l], G['w_xo'][l])) = _mix_bwd(
            dx, dxb, s2, wc8, row('g_conv_out'), W['w_mix_out'][l], f"l{l}_mix", xattn_grads)
        dx, dxb, G['g_ffn1'][l], unsent, _ = _ffn_bwd(
            dx, dxb, s1, wgut('w_ffn1_gu', l), W['w_ffn1_down'][l], f"l{l}_ffn1", [], l > 0)
    G['w_ffn1_gu'][0], G['w_ffn1_down'][0] = unsent
    G['g_final'] = [dg_final]
    return loss, dx, G


def kernel(x, mem, g_ffn1, w_ffn1_gu, w_ffn1_down, g_mix, w_mix_in, w_conv, b_f, g_conv_out, g_att_out, w_mix_out, g_xattn, g_mem, w_xq, w_xkv, w_xo, g_ffn2, w_ffn2_gu, w_ffn2_down, g_final, loss_target, m_g_ffn1, m_w_ffn1_gu, m_w_ffn1_down, m_g_mix, m_w_mix_in, m_w_conv, m_b_f, m_g_conv_out, m_g_att_out, m_w_mix_out, m_g_xattn, m_g_mem, m_w_xq, m_w_xkv, m_w_xo, m_g_ffn2, m_w_ffn2_gu, m_w_ffn2_down, m_g_final, v_g_ffn1, v_w_ffn1_gu, v_w_ffn1_down, v_g_mix, v_w_mix_in, v_w_conv, v_b_f, v_g_conv_out, v_g_att_out, v_w_mix_out, v_g_xattn, v_g_mem, v_w_xq, v_w_xkv, v_w_xo, v_g_ffn2, v_w_ffn2_gu, v_w_ffn2_down, v_g_final):
    args = (x, mem, g_ffn1, w_ffn1_gu, w_ffn1_down, g_mix, w_mix_in, w_conv, b_f, g_conv_out, g_att_out, w_mix_out,
            g_xattn, g_mem, w_xq, w_xkv, w_xo, g_ffn2, w_ffn2_gu, w_ffn2_down, g_final)
    P = dict(zip(IN_NAMES, args))
    moms = (m_g_ffn1, m_w_ffn1_gu, m_w_ffn1_down, m_g_mix, m_w_mix_in, m_w_conv, m_b_f, m_g_conv_out, m_g_att_out,
            m_w_mix_out, m_g_xattn, m_g_mem, m_w_xq, m_w_xkv, m_w_xo, m_g_ffn2, m_w_ffn2_gu, m_w_ffn2_down, m_g_final)
    vars_ = (v_g_ffn1, v_w_ffn1_gu, v_w_ffn1_down, v_g_mix, v_w_mix_in, v_w_conv, v_b_f, v_g_conv_out, v_g_att_out,
             v_w_mix_out, v_g_xattn, v_g_mem, v_w_xq, v_w_xkv, v_w_xo, v_g_ffn2, v_w_ffn2_gu, v_w_ffn2_down, v_g_final)
    MOM = dict(zip(WEIGHT_NAMES, moms))
    VAR = dict(zip(WEIGHT_NAMES, vars_))

    loss_part, dx, grads = _local_step(x[0], mem[0], loss_target[0], {n: P[n] for n in WEIGHT_NAMES})

    small_shapes = [P[n].shape for n in REPLICATED] + [(1,)]
    small = _pack([a.reshape(-1) for n in REPLICATED for a in grads[n]] + [loss_part[0, :1]], SUBLANES)
    small_parts = _all_gather([small], "gather_small_grads")[0]

    res = {}
    for n in SHARDED:
        t = (lambda a: a.transpose(0, 2, 1)) if n in LONG_AXIS_LAST else (lambda a: a)
        res[n] = [t(o) for o in _sum_adamw(grads[n], t(P[n]), t(MOM[n]), t(VAR[n]), "adamw_" + n)]
    zero = [jnp.zeros((1,), F32)]
    w_, m_, v_ = (_pack([d[n].reshape(-1) for n in REPLICATED] + zero, SUBLANES)[None] for d in (P, MOM, VAR))
    small_out = [_unpack(o[0], small_shapes) for o in _sum_adamw([small_parts], w_, m_, v_, "adamw_vectors")]
    for k, n in enumerate(REPLICATED):
        res[n] = [o[k] for o in small_out]
    out = []
    for k in range(4):
        out += [res[n][k] for n in WEIGHT_NAMES]
    return (small_out[0][-1].reshape(()), dx[None], *out)
```

```python
import jax
import jax.numpy as jnp
from jax import lax
from jax.experimental import pallas as pl
from jax.experimental.pallas import tpu as pltpu

F32 = jnp.float32
BF16 = jnp.bfloat16
I32 = jnp.int32
SDS = jax.ShapeDtypeStruct

EPS = 1e-6
HEAD_DIM = 64
N_XHEADS = 4
N_DEV = 8
LANES = 128
SUBLANES = 8
MXU_COLS = 256
AUG = 16
TAIL = 16
ATT_BLOCK = 512
NEG = -1e30
SKIP_MARGIN = 115.0
PACK_COLS = 1024
VMEM_LIMIT = 58 * 1024 * 1024

ADAM_LR = 0.001
ADAM_B1 = 0.9
ADAM_B2 = 0.999
ADAM_EPS = 1e-08
ADAM_WD = 0.01
ADAM_STEP = 10

IN_NAMES = ['x', 'mem', 'g_ffn1', 'w_ffn1_gu', 'w_ffn1_down', 'g_mix', 'w_mix_in', 'w_conv', 'b_f', 'g_conv_out',
            'g_att_out', 'w_mix_out', 'g_xattn', 'g_mem', 'w_xq', 'w_xkv', 'w_xo', 'g_ffn2', 'w_ffn2_gu',
            'w_ffn2_down', 'g_final']
WEIGHT_NAMES = IN_NAMES[2:]
SHARDED = ['w_ffn1_gu', 'w_ffn1_down', 'w_mix_in', 'w_conv', 'w_mix_out', 'w_xq', 'w_xkv', 'w_xo', 'w_ffn2_gu',
           'w_ffn2_down']
REPLICATED = [n for n in WEIGHT_NAMES if n not in SHARDED]
LONG_AXIS_LAST = ['w_ffn1_gu', 'w_ffn2_gu', 'w_mix_in']


def _tile(n, pref, mult):
    t = min(pref, n) // mult * mult
    while t >= mult:
        if n % t == 0:
            return t
        t -= mult
    return n


def _cp(*sem):
    return pltpu.CompilerParams(dimension_semantics=sem, vmem_limit_bytes=VMEM_LIMIT)


def _nt(a, b):
    return lax.dot_general(a, b, (((1,), (1,)), ((), ())), preferred_element_type=F32)


def _tn(a, b):
    return lax.dot_general(a, b, (((0,), (0,)), ((), ())), preferred_element_type=F32)


def _nn(a, b):
    return jnp.dot(a, b, preferred_element_type=F32)


def _rstd(xv):
    return lax.rsqrt(jnp.mean(xv * xv, axis=-1, keepdims=True) + EPS)


def _full(a):
    nd = a.ndim
    return pl.BlockSpec(a.shape, lambda *_: (0,) * nd, pipeline_mode=pl.Buffered(1))


def _rows2d(w_ref):
    s, a, b = w_ref.shape
    return w_ref[...].reshape(s * a, b)


def _accumulate(ref, first, part):
    @pl.when(first)
    def _():
        ref[...] = part

    @pl.when(jnp.logical_not(first))
    def _():
        ref[...] += part


def _norm_bwd_store(dh, x_ref, g_ref, r_ref, dx_ref, dxb_ref, dg_ref, first):
    xv = x_ref[...]
    r = _rstd(xv)
    xh = xv * r
    dxh = dh * g_ref[...]
    dx = r * (dxh - xh * jnp.mean(dxh * xh, axis=-1, keepdims=True)) + r_ref[...]
    dx_ref[...] = dx
    dxb_ref[...] = dx.astype(BF16)
    _accumulate(dg_ref, first, jnp.sum(dh * xh, axis=0, keepdims=True))


def _mm_res(a_list, w, res, name, tm=512):
    T = res.shape[0]
    N = w.shape[-1]
    tm = _tile(T, tm, SUBLANES)
    n_a = len(a_list)

    def body(*refs):
        a_refs = refs[:n_a]
        w_ref, r_ref, o_ref = refs[n_a:]
        wv = _rows2d(w_ref)
        acc = r_ref[...]
        off = 0
        for a_ref in a_refs:
            k = a_ref.shape[1]
            acc = acc + _nn(a_ref[...], wv[off:off + k, :])
            off += k
        o_ref[...] = acc

    row = lambda n: pl.BlockSpec((tm, n), lambda i: (i, 0))
    return pl.pallas_call(
        body, name=name, grid=(T // tm,),
        in_specs=[row(a.shape[1]) for a in a_list] + [_full(w), row(N)], out_specs=row(N),
        out_shape=SDS((T, N), F32), compiler_params=_cp("parallel"))(*a_list, w, res)


def _mm_tn(a, b, name, out_dtype=BF16, tt=2048):
    T, K = a.shape
    N = b.shape[1]
    tt = _tile(T, tt, 2 * SUBLANES)
    nt = T // tt

    def body(a_ref, b_ref, o_ref, acc_ref):
        t = pl.program_id(0)
        _accumulate(acc_ref, t == 0, _tn(a_ref[...], b_ref[...]))

        @pl.when(t == nt - 1)
        def _():
            o_ref[...] = acc_ref[...].astype(o_ref.dtype)

    return pl.pallas_call(
        body, name=name, grid=(nt,),
        in_specs=[pl.BlockSpec((tt, K), lambda t: (t, 0)), pl.BlockSpec((tt, N), lambda t: (t, 0))],
        out_specs=pl.BlockSpec((K, N), lambda t: (0, 0)), out_shape=SDS((K, N), out_dtype),
        scratch_shapes=[pltpu.VMEM((K, N), F32)], compiler_params=_cp("arbitrary"))(a, b)


def _hosted_call(body, name, n_steps, in_specs, out_specs, out_shape, args, scatter=(), gather=(), scratch_shapes=()):
    scatter, gather, scratch_shapes = list(scatter), list(gather), list(scratch_shapes)
    n_s, n_g = len(scatter), len(gather)
    if n_s + n_g == 0:
        return pl.pallas_call(body, name=name, grid=(n_steps,), in_specs=list(in_specs), out_specs=list(out_specs),
                              out_shape=list(out_shape), scratch_shapes=scratch_shapes,
                              compiler_params=_cp("arbitrary"))(*args)
    n_in, n_out, n_scr = len(in_specs), len(out_specs), len(scratch_shapes)
    relay_step = min(max(1, 3 * n_steps // 4), n_steps - 1)

    def wrapped(*refs):
        ins, refs = refs[:n_in], refs[n_in:]
        xs, shards, refs = refs[:n_s], refs[n_s:n_s + n_g], refs[n_s + n_g:]
        outs, refs = refs[:n_out], refs[n_out:]
        recvs, gathered, refs = refs[:n_s], refs[n_s:n_s + n_g], refs[n_s + n_g:]
        outs, sems = outs + refs[:n_scr], refs[n_scr:]
        s_sems, g_sems = (sems[:3], sems[3:]) if n_s else ((), sems)
        step = pl.program_id(0)

        @pl.when(step == 0)
        def _():
            if n_s:
                _scatter_start(xs, recvs, *s_sems)
            if n_g:
                _gather_start(shards, gathered, *g_sems)
        body(*ins, *outs)
        if n_g:
            @pl.when(step == relay_step)
            def _():
                _gather_relay(shards, gathered, *g_sems)

        @pl.when(step == n_steps - 1)
        def _():
            if n_s:
                _scatter_wait(xs, recvs, *s_sems)
            if n_g:
                _gather_finish(shards, gathered, *g_sems)

    any_spec = pl.BlockSpec(memory_space=pl.ANY)
    sems = []
    for n in (n_s, n_g):
        if n:
            sems += [pltpu.SemaphoreType.DMA((n, N_DEV - 1)), pltpu.SemaphoreType.DMA((n, N_DEV - 1)),
                     pltpu.SemaphoreType.DMA((n,))]
    return pl.pallas_call(
        wrapped, name=name, grid=(n_steps,), in_specs=list(in_specs) + [any_spec] * (n_s + n_g),
        out_specs=list(out_specs) + [any_spec] * (n_s + n_g),
        out_shape=list(out_shape) + [SDS(b.shape, b.dtype) for b in scatter]
        + [SDS((N_DEV,) + b.shape, b.dtype) for b in gather],
        scratch_shapes=scratch_shapes + sems, compiler_params=_cp("arbitrary"))(*args, *scatter, *gather)


def _hidden_chunks(F):
    edges = list(range(0, F, 3 * MXU_COLS)) + [F]
    return list(zip(edges[:-1], edges[1:]))


def _ffn_gu(x, g, wgut, name, gather=(), tm=512):
    T, D = x.shape
    F = wgut.shape[0] // 2
    tm = _tile(T, tm, SUBLANES)

    def body(x_ref, g_ref, w_ref, ab_ref, act_ref, h_ref):
        xv = x_ref[...]
        h = (xv * _rstd(xv) * g_ref[...]).astype(BF16)
        h_ref[...] = h
        for c0, c1 in _hidden_chunks(F):
            gt = _nt(h, w_ref[c0:c1, :])
            up = _nt(h, w_ref[F + c0:F + c1, :])
            sg = jax.nn.sigmoid(gt)
            silu = gt * sg
            ab_ref[0, :, c0:c1] = (0.5 * up * (sg * (1.0 + gt * (1.0 - sg)))).astype(BF16)
            ab_ref[1, :, c0:c1] = (0.5 * silu).astype(BF16)
            act_ref[:, c0:c1] = (silu * up).astype(BF16)

    row = lambda n: pl.BlockSpec((tm, n), lambda i: (i, 0))
    return _hosted_call(
        body, name, T // tm, [row(D), _full(g), _full(wgut)],
        [pl.BlockSpec((2, tm, F), lambda i: (0, i, 0)), row(F), row(D)],
        [SDS((2, T, F), BF16), SDS((T, F), BF16), SDS((T, D), BF16)], (x, g, wgut), gather=gather)


def _ffn_down(act, wd, res, name, gather=(), tm=1024):
    T, F = act.shape
    D = wd.shape[2]
    tm = _tile(T, tm, SUBLANES)

    def body(a_ref, w_ref, r_ref, o_ref):
        o_ref[...] = r_ref[...] + 0.5 * _nn(a_ref[...], _rows2d(w_ref))

    row = lambda n: pl.BlockSpec((tm, n), lambda i: (i, 0))
    return _hosted_call(body, name, T // tm, [row(F), _full(wd), row(D)], [row(D)], [SDS((T, D), F32)],
                        (act, wd, res), gather=gather)


def _ffn_bwd_act(dyb, wd, ab, name, tm=512):
    T, D = dyb.shape
    _, _, F = ab.shape
    tm = _tile(T, tm, SUBLANES)

    def body(dy_ref, w_ref, ab_ref, o_ref):
        wv = _rows2d(w_ref)
        dy = dy_ref[...]
        for c0, c1 in _hidden_chunks(F):
            da = _nt(dy, wv[c0:c1, :])
            o_ref[0, :, c0:c1] = (da * ab_ref[0, :, c0:c1].astype(F32)).astype(BF16)
            o_ref[1, :, c0:c1] = (da * ab_ref[1, :, c0:c1].astype(F32)).astype(BF16)

    blocks = pl.BlockSpec((2, tm, F), lambda i: (0, i, 0))
    return pl.pallas_call(
        body, name=name, grid=(T // tm,),
        in_specs=[pl.BlockSpec((tm, D), lambda i: (i, 0)), _full(wd), blocks], out_specs=blocks,
        out_shape=SDS((2, T, F), BF16), compiler_params=_cp("parallel"))(dyb, wd, ab)


def _ffn_dwd(act, dyb, name, tt=2048):
    T, F = act.shape
    D = dyb.shape[1]
    tt = _tile(T, tt, 2 * SUBLANES)
    nt = T // tt
    fh = F // 2

    def body(a_ref, b_ref, o_ref, acc_ref):
        t = pl.program_id(1)
        _accumulate(acc_ref, t == 0, _tn(a_ref[...], b_ref[...]))

        @pl.when(t == nt - 1)
        def _():
            o_ref[...] = (0.5 * acc_ref[...]).astype(BF16)

    return pl.pallas_call(
        body, name=name, grid=(2, nt),
        in_specs=[pl.BlockSpec((tt, fh), lambda j, t: (t, j)), pl.BlockSpec((tt, D), lambda j, t: (t, 0))],
        out_specs=pl.BlockSpec((fh, D), lambda j, t: (j, 0)), out_shape=SDS((F, D), BF16),
        scratch_shapes=[pltpu.VMEM((fh, D), F32)], compiler_params=_cp("parallel", "arbitrary"))(act, dyb)


def _ffn_dwgu(h, dgu, name, tt=2048):
    T, D = h.shape
    _, _, F = dgu.shape
    tt = _tile(T, tt, 2 * SUBLANES)
    nt = T // tt
    fh = F // 2

    def body(a_ref, b_ref, o_ref, acc_ref):
        t = pl.program_id(1)
        _accumulate(acc_ref, t == 0, _tn(b_ref[...], a_ref[...]))

        @pl.when(t == nt - 1)
        def _():
            o_ref[...] = acc_ref[...].astype(BF16)

    return pl.pallas_call(
        body, name=name, grid=(4, nt),
        in_specs=[pl.BlockSpec((tt, D), lambda q, t: (t, 0)),
                  pl.BlockSpec((None, tt, fh), lambda q, t: (q // 2, t, q % 2))],
        out_specs=pl.BlockSpec((fh, D), lambda q, t: (q, 0)), out_shape=SDS((2 * F, D), BF16),
        scratch_shapes=[pltpu.VMEM((fh, D), F32)], compiler_params=_cp("parallel", "arbitrary"))(h, dgu)


def _ffn_bwd_in(dgu, wgut, x, g, dres, name, scatter, tm=512):
    T, D = x.shape
    _, _, F = dgu.shape
    tm = _tile(T, tm, SUBLANES)

    def body(a_ref, w_ref, x_ref, g_ref, r_ref, dx_ref, dxb_ref, dg_ref):
        dh = _nn(a_ref[0], w_ref[:F, :]) + _nn(a_ref[1], w_ref[F:, :])
        _norm_bwd_store(dh, x_ref, g_ref, r_ref, dx_ref, dxb_ref, dg_ref, pl.program_id(0) == 0)

    row = pl.BlockSpec((tm, D), lambda i: (i, 0))
    return _hosted_call(
        body, name, T // tm, [pl.BlockSpec((2, tm, F), lambda i: (0, i, 0)), _full(wgut), row, _full(g), row],
        [row, row, pl.BlockSpec((1, D), lambda i: (0, 0))],
        [SDS((T, D), F32), SDS((T, D), BF16), SDS((1, D), F32)], (dgu, wgut, x, g, dres), scatter)


def _loss_head(x, g, tgt, name, tm=512):
    T, D = x.shape
    tm = _tile(T, tm, SUBLANES)

    def body(x_ref, g_ref, t_ref, loss_ref, dx_ref, dxb_ref, dg_ref):
        first = pl.program_id(0) == 0
        xv = x_ref[...]
        r = _rstd(xv)
        xh = xv * r
        e = xh * g_ref[...] - t_ref[...]
        part = 0.5 * jnp.sum(jnp.mean(e * e, axis=-1, keepdims=True), axis=0, keepdims=True)
        dy = e * (1.0 / D)
        dxh = dy * g_ref[...]
        dx = r * (dxh - xh * jnp.mean(dxh * xh, axis=-1, keepdims=True))
        dx_ref[...] = dx
        dxb_ref[...] = dx.astype(BF16)
        _accumulate(loss_ref, first, jnp.broadcast_to(part, loss_ref.shape))
        _accumulate(dg_ref, first, jnp.sum(dy * xh, axis=0, keepdims=True))

    row = pl.BlockSpec((tm, D), lambda i: (i, 0))
    vec = pl.BlockSpec((1, D), lambda i: (0, 0))
    return pl.pallas_call(
        body, name=name, grid=(T // tm,), in_specs=[row, vec, row],
        out_specs=[pl.BlockSpec((SUBLANES, LANES), lambda i: (0, 0)), row, row, vec],
        out_shape=[SDS((SUBLANES, LANES), F32), SDS((T, D), F32), SDS((T, D), BF16), SDS((1, D), F32)],
        compiler_params=_cp("arbitrary"))(x, g, tgt)


def _shift_down(u, k, prev_row):
    rows = lax.broadcasted_iota(I32, u.shape, 0)
    s = pltpu.roll(u, k, 0)
    for t in range(k):
        s = jnp.where(rows == t, prev_row(SUBLANES - k + t), s)
    return s


def _shift_up(u, k, next_row):
    n = u.shape[0]
    rows = lax.broadcasted_iota(I32, u.shape, 0)
    s = pltpu.roll(u, n - k, 0)
    for t in range(k):
        s = jnp.where(rows == n - k + t, next_row(t), s)
    return s


def _conv_taps(z_ref, zp_ref, w_ref, cw, first):
    u = z_ref[:, cw:2 * cw] * z_ref[:, 2 * cw:]

    def prev_row(r):
        return jnp.where(first, 0.0, zp_ref[r:r + 1, cw:2 * cw] * zp_ref[r:r + 1, 2 * cw:])

    u1 = _shift_down(u, 1, prev_row)
    u2 = _shift_down(u, 2, prev_row)
    cv = w_ref[0:1, :] * u2 + w_ref[1:2, :] * u1 + w_ref[2:3, :] * u
    return u, u1, u2, cv


def _mix_in_conv(x, g, w, wc, gc, name, tm=512):
    T, D = x.shape
    cw = w.shape[0] // 3
    tm = _tile(T, tm, SUBLANES)

    def body(x_ref, g_ref, w_ref, wc_ref, gc_ref, z_ref, h_ref, o_ref, prev_ref):
        xv = x_ref[...]
        h = (xv * _rstd(xv) * g_ref[...]).astype(BF16)
        h_ref[...] = h
        z_ref[...] = _nt(h, w_ref[...])
        _, _, _, cv = _conv_taps(z_ref, prev_ref, wc_ref, cw, pl.program_id(0) == 0)
        y = z_ref[:, :cw] * cv
        o_ref[...] = (y * _rstd(y) * gc_ref[...]).astype(BF16)
        prev_ref[...] = z_ref[tm - SUBLANES:, :]

    row = lambda n: pl.BlockSpec((tm, n), lambda i: (i, 0))
    return pl.pallas_call(
        body, name=name, grid=(T // tm,), in_specs=[row(D), _full(g), _full(w), _full(wc), _full(gc)],
        out_specs=[row(3 * cw), row(D), row(cw)],
        out_shape=[SDS((T, 3 * cw), F32), SDS((T, D), BF16), SDS((T, cw), BF16)],
        scratch_shapes=[pltpu.VMEM((SUBLANES, 3 * cw), F32)], compiler_params=_cp("arbitrary"))(x, g, w, wc, gc)


def _mix_bwd_head(dyb, w_out, zc, wc, gc, ot, g_col, name, scatter):
    nb, aw, tm = ot.shape
    T, D = dyb.shape
    cw = zc.shape[1] // 3
    H = aw // HEAD_DIM
    hb = tm // SUBLANES

    def body(dyb_ref, wo_ref, z_ref, zp_ref, w_ref, g_ref, o_ref, ga_ref,
             dz_ref, dw_ref, dg_ref, do_ref, dl_ref, dga_ref, carry_ref):
        s = pl.program_id(0)
        dy = _nt(dyb_ref[...], _rows2d(wo_ref))
        ov = o_ref[...]
        ra = lax.rsqrt(jnp.mean(ov * ov, axis=0, keepdims=True) + EPS)
        oh = ov * ra
        dya = dy[:, cw:].T
        dyah = dya * ga_ref[...]
        dov = ra * (dyah - oh * jnp.mean(dyah * oh, axis=0, keepdims=True))
        do_ref[...] = dov.astype(BF16)
        dl_ref[...] = jnp.sum((dov * ov).reshape(H, HEAD_DIM, tm), axis=1)
        _accumulate(dga_ref, s == 0, jnp.broadcast_to(jnp.sum(dya * oh, axis=1, keepdims=True), dga_ref.shape))
        u, u1, u2, cv = _conv_taps(z_ref, zp_ref, w_ref, cw, s == nb - 1)
        zb = z_ref[:, :cw]
        y = zb * cv
        r = _rstd(y)
        yh = y * r
        dyn = dy[:, :cw]
        dyh = dyn * g_ref[...]
        dyc = r * (dyh - yh * jnp.mean(dyh * yh, axis=-1, keepdims=True))
        dcv = dyc * zb

        def next_row(t):
            return jnp.where(s == 0, 0.0, carry_ref[t:t + 1, :])

        du = w_ref[2:3, :] * dcv + w_ref[1:2, :] * _shift_up(dcv, 1, next_row) + w_ref[0:1, :] * _shift_up(dcv, 2, next_row)
        carry_ref[...] = dcv[0:SUBLANES, :]
        dz_ref[:, :cw] = (dyc * cv).astype(BF16)
        dz_ref[:, cw:2 * cw] = (du * z_ref[:, 2 * cw:]).astype(BF16)
        dz_ref[:, 2 * cw:] = (du * z_ref[:, cw:2 * cw]).astype(BF16)
        tap = lax.broadcasted_iota(I32, (SUBLANES, cw), 0)
        dwp = jnp.where(tap == 0, jnp.sum(dcv * u2, axis=0, keepdims=True),
                        jnp.where(tap == 1, jnp.sum(dcv * u1, axis=0, keepdims=True),
                                  jnp.where(tap == 2, jnp.sum(dcv * u, axis=0, keepdims=True), 0.0)))
        _accumulate(dw_ref, s == 0, dwp)
        _accumulate(dg_ref, s == 0, jnp.sum(dyn * yh, axis=0, keepdims=True))

    rev = lambda s: nb - 1 - s
    rows = lambda n: pl.BlockSpec((tm, n), lambda s: (rev(s), 0))
    blk = pl.BlockSpec((None, aw, tm), lambda s: (rev(s), 0, 0))
    return _hosted_call(
        body, name, nb,
        [rows(D), _full(w_out), rows(3 * cw),
         pl.BlockSpec((SUBLANES, 3 * cw), lambda s: (jnp.maximum(rev(s) * hb - 1, 0), 0)),
         _full(wc), _full(gc), blk, _full(g_col)],
        [rows(3 * cw), pl.BlockSpec((SUBLANES, cw), lambda s: (0, 0)), pl.BlockSpec((1, cw), lambda s: (0, 0)),
         blk, pl.BlockSpec((None, H, tm), lambda s: (rev(s), 0, 0)), pl.BlockSpec((aw, LANES), lambda s: (0, 0))],
        [SDS((T, 3 * cw), BF16), SDS((SUBLANES, cw), F32), SDS((1, cw), F32),
         SDS((nb, aw, tm), BF16), SDS((nb, H, tm), F32), SDS((aw, LANES), F32)],
        (dyb, w_out, zc, zc, wc, gc, ot, g_col), scatter=scatter, scratch_shapes=[pltpu.VMEM((SUBLANES, cw), F32)])


def _proj_t(h, waf, B, name):
    T, D = h.shape
    R = waf.shape[0] - TAIL
    nb = T // B
    qk = 2 * R // 3
    g = qk // HEAD_DIM

    def body(h_ref, w_ref, za_ref, zf_ref, n_ref):
        z = _nt(w_ref[...], h_ref[...])
        zab = z[:R].astype(BF16)
        za_ref[...] = zab
        zf_ref[...] = z[R:R + SUBLANES]
        zv = zab[:qk].astype(F32)
        ss = jnp.sum((zv * zv).reshape(g, HEAD_DIM, B), axis=1)
        mx = jnp.broadcast_to(jnp.max(ss, axis=1, keepdims=True), n_ref.shape)

        @pl.when(pl.program_id(0) == 0)
        def _():
            n_ref[...] = mx

        @pl.when(pl.program_id(0) > 0)
        def _():
            n_ref[...] = jnp.maximum(n_ref[...], mx)

    return pl.pallas_call(
        body, name=name, grid=(nb,), in_specs=[pl.BlockSpec((B, D), lambda i: (i, 0)), _full(waf)],
        out_specs=[pl.BlockSpec((None, R, B), lambda i: (i, 0, 0)), pl.BlockSpec((SUBLANES, B), lambda i: (0, i)),
                   pl.BlockSpec((g, LANES), lambda i: (0, 0))],
        out_shape=[SDS((nb, R, B), BF16), SDS((SUBLANES, T), F32), SDS((g, LANES), F32)],
        compiler_params=_cp("arbitrary"))(h, waf)


def _split3(v):
    hi = v.astype(BF16).astype(F32)
    r1 = v - hi
    mid = r1.astype(BF16).astype(F32)
    lo = (r1 - mid).astype(BF16).astype(F32)
    return hi, mid, lo


def _tri_dot3(v, tri):
    hi, mid, lo = _split3(v)
    return _nn(hi.astype(BF16), tri) + _nn(mid.astype(BF16), tri) + _nn(lo.astype(BF16), tri)


def _logf_cumsum(zft, b, tb, name):
    H, T = zft.shape

    def body(z_ref, b_ref, c_ref, qa_ref, ka_ref, carry_ref, hi_ref, mid_ref, lo_ref):
        @pl.when(pl.program_id(0) == 0)
        def _():
            carry_ref[...] = jnp.zeros_like(carry_ref)
        xv = z_ref[...] + b_ref[...]
        lf = jnp.minimum(xv, 0.0) - jnp.log(1.0 + jnp.exp(-jnp.abs(xv)))
        src = lax.broadcasted_iota(I32, (tb, tb), 0)
        dst = lax.broadcasted_iota(I32, (tb, tb), 1)
        tri = jnp.where(src <= dst, 1.0, 0.0).astype(BF16)
        cs = _tri_dot3(lf, tri) + carry_ref[...]
        c_ref[...] = cs
        hi_ref[...], mid_ref[...], lo_ref[...] = _split3(cs)
        col = lax.broadcasted_iota(I32, cs.shape, 1)
        carry_ref[...] = jnp.sum(jnp.where(col == tb - 1, cs, 0.0), axis=1, keepdims=True)
        row = lax.broadcasted_iota(I32, (AUG, tb), 0)
        for h in range(H):
            terms = [r[h:h + 1, :] for r in (hi_ref, mid_ref, lo_ref)]
            qa, ka = jnp.where(row < 3, 1.0, 0.0), jnp.where((row >= 3) & (row < 6), 1.0, 0.0)
            for k, t in enumerate(terms):
                qa = jnp.where(row == 3 + k, t, qa)
                ka = jnp.where(row == k, -t, ka)
            qa_ref[h] = qa
            ka_ref[h] = ka

    blk = pl.BlockSpec((H, tb), lambda i: (0, i))
    aug = pl.BlockSpec((H, None, AUG, tb), lambda i: (0, i, 0, 0))
    return pl.pallas_call(
        body, name=name, grid=(T // tb,), in_specs=[blk, _full(b)], out_specs=[blk, aug, aug],
        out_shape=[SDS((H, T), F32), SDS((H, T // tb, AUG, tb), F32), SDS((H, T // tb, AUG, tb), F32)],
        scratch_shapes=[pltpu.VMEM((H, 1), F32)] + [pltpu.VMEM((H, tb), F32)] * 3,
        compiler_params=_cp("arbitrary"))(zft, b)


def _logf_cumsum_bwd(dcq, dck, zft, b, name, tb=512):
    H, T = zft.shape
    tb = _tile(T, tb, LANES)
    nb = T // tb

    def body(dq_ref, dk_ref, z_ref, b_ref, o_ref, db_ref, carry_ref):
        s = pl.program_id(0)

        @pl.when(s == 0)
        def _():
            carry_ref[...] = jnp.zeros_like(carry_ref)
        dc = dq_ref[...] - dk_ref[...]
        src = lax.broadcasted_iota(I32, (tb, tb), 0)
        dst = lax.broadcasted_iota(I32, (tb, tb), 1)
        tri = jnp.where(src >= dst, 1.0, 0.0).astype(BF16)
        dl = _tri_dot3(dc, tri) + carry_ref[...]
        col = lax.broadcasted_iota(I32, dl.shape, 1)
        carry_ref[...] = jnp.sum(jnp.where(col == 0, dl, 0.0), axis=1, keepdims=True)
        dz = dl * jax.nn.sigmoid(-(z_ref[...] + b_ref[...]))
        o_ref[...] = dz
        _accumulate(db_ref, s == 0, jnp.sum(dz, axis=1, keepdims=True))

    blk = pl.BlockSpec((H, tb), lambda s: (0, nb - 1 - s))
    vec = pl.BlockSpec((H, 1), lambda s: (0, 0))
    return pl.pallas_call(
        body, name=name, grid=(nb,), in_specs=[blk, blk, blk, vec], out_specs=[blk, vec],
        out_shape=[SDS((H, T), F32), SDS((H, 1), F32)],
        scratch_shapes=[pltpu.VMEM((H, 1), F32)], compiler_params=_cp("arbitrary"))(dcq, dck, zft, b)


def _skip_table(cs_col, ce_row, norms, name):
    H, nb, _ = cs_col.shape

    def body(cs_ref, ce_ref, n_ref, jm_ref, im_ref):
        h = pl.program_id(0)
        nq = n_ref[pl.ds(h, 1), 0:1]
        nk = n_ref[pl.ds(H + h, 1), 0:1]
        bound = 2.0 * jnp.sqrt(nq * nk) * 0.125
        skip = jnp.where(bound + cs_ref[...] - ce_ref[...] <= -SKIP_MARGIN, 1, 0).astype(I32)
        jm_ref[...] = jnp.sum(skip, axis=1, keepdims=True)
        im_ref[...] = nb - 1 - jnp.sum(skip, axis=0, keepdims=True)

    col = pl.BlockSpec((None, nb, 1), lambda h: (h, 0, 0))
    row = pl.BlockSpec((None, 1, nb), lambda h: (h, 0, 0))
    return pl.pallas_call(
        body, name=name, grid=(H,), in_specs=[col, row, _full(norms)], out_specs=[col, row],
        out_shape=[SDS((H, nb, 1), I32), SDS((H, 1, nb), I32)], compiler_params=_cp("parallel"))(cs_col, ce_row, norms)


def _causal_mask(B):
    krow = lax.broadcasted_iota(I32, (B, B), 0)
    qcol = lax.broadcasted_iota(I32, (B, B), 1)
    return krow <= qcol


def _attn_fwd(jmin, zat, qa, ka, name):
    nb, R, B = zat.shape
    aw = R // 3
    H = aw // HEAD_DIM
    hd = HEAD_DIM

    def body(jm_ref, q_ref, k_ref, v_ref, qa_ref, ka_ref, o_ref, lse_ref):
        h = pl.program_id(0)
        i = pl.program_id(1)
        qt = jnp.concatenate([q_ref[...] * 0.125, qa_ref[...].astype(BF16)], axis=0)

        def keys(j):
            return jnp.concatenate([k_ref[j], ka_ref[j].astype(BF16)], axis=0)

        def update(s, vt, carry):
            m, l, acc = carry
            m_new = jnp.maximum(m, jnp.max(s, axis=0, keepdims=True))
            p = jnp.exp(s - m_new)
            a = jnp.exp(m - m_new)
            l = a * l + jnp.sum(p, axis=0, keepdims=True)
            acc = a * acc + _nn(vt, p.astype(BF16))
            return m_new, l, acc

        def block(j, carry, masked):
            s = _tn(keys(j), qt)
            if masked:
                s = jnp.where(_causal_mask(B), s, NEG)
            return update(s, v_ref[j], carry)

        def two_blocks(t, carry):
            j = j0 + 2 * t
            s = _tn(jnp.concatenate([keys(j), keys(j + 1)], axis=1), qt)
            return update(s, jnp.concatenate([v_ref[j], v_ref[j + 1]], axis=1), carry)

        j0 = jm_ref[h * nb + i]
        pairs = (i - j0) // 2
        init = (jnp.full((1, B), NEG, F32), jnp.zeros((1, B), F32), jnp.zeros((hd, B), F32))
        carry = lax.fori_loop(0, pairs, two_blocks, init)
        carry = lax.fori_loop(j0 + 2 * pairs, i, lambda j, c: block(j, c, False), carry)
        m, l, acc = block(i, carry, True)
        o_ref[...] = acc / l
        lse_ref[...] = m + jnp.log(l)

    grid_spec = pltpu.PrefetchScalarGridSpec(
        num_scalar_prefetch=1, grid=(H, nb),
        in_specs=[pl.BlockSpec((None, hd, B), lambda h, i, jm: (i, h, 0)),
                  pl.BlockSpec((nb, hd, B), lambda h, i, jm: (0, H + h, 0)),
                  pl.BlockSpec((nb, hd, B), lambda h, i, jm: (0, 2 * H + h, 0)),
                  pl.BlockSpec((None, None, AUG, B), lambda h, i, jm: (h, i, 0, 0)),
                  pl.BlockSpec((None, nb, AUG, B), lambda h, i, jm: (h, 0, 0, 0))],
        out_specs=[pl.BlockSpec((None, hd, B), lambda h, i, jm: (i, h, 0)),
                   pl.BlockSpec((None, None, 1, B), lambda h, i, jm: (h, i, 0, 0))])
    return pl.pallas_call(
        body, name=name, grid_spec=grid_spec,
        out_shape=[SDS((nb, aw, B), F32), SDS((H, nb, 1, B), F32)],
        compiler_params=_cp("parallel", "parallel"))(jmin, zat, zat, zat, qa, ka)


def _gnorm_t_fwd(ot, g_col, name):
    nb, aw, B = ot.shape

    def body(o_ref, g_ref, y_ref):
        ov = o_ref[...]
        r = lax.rsqrt(jnp.mean(ov * ov, axis=0, keepdims=True) + EPS)
        y_ref[...] = (ov * r * g_ref[...]).T.astype(BF16)

    return pl.pallas_call(
        body, name=name, grid=(nb,), in_specs=[pl.BlockSpec((None, aw, B), lambda i: (i, 0, 0)), _full(g_col)],
        out_specs=pl.BlockSpec((B, aw), lambda i: (i, 0)), out_shape=SDS((nb * B, aw), BF16),
        compiler_params=_cp("parallel"))(ot, g_col)


def _attn_bwd(imax, zat, qa, ka, dot, lse, delta, name):
    nb, R, B = zat.shape
    aw = R // 3
    H = aw // HEAD_DIM
    hd = HEAD_DIM

    def body(im_ref, k_ref, v_ref, ka_ref, q_ref, qa_ref, do_ref, lse_ref, dl_ref,
             dq_ref, dqa_ref, dk_ref, dka_ref, dv_ref, dq_acc):
        h = pl.program_id(0)
        j = pl.program_id(1)

        @pl.when(j == 0)
        def _():
            dq_acc[...] = jnp.zeros_like(dq_acc)
            dqa_ref[...] = jnp.zeros_like(dqa_ref)
        kt = jnp.concatenate([k_ref[...], ka_ref[...].astype(BF16)], axis=0)
        vt = v_ref[...]

        def queries(i):
            return jnp.concatenate([q_ref[i] * 0.125, qa_ref[i].astype(BF16)], axis=0)

        def grads(qt, dov, lse, dl, carry, masked):
            dk, dv = carry
            s = _tn(kt, qt)
            if masked:
                s = jnp.where(_causal_mask(B), s, NEG)
            p = jnp.exp(s - lse)
            dp = _tn(vt, dov)
            ds = (p * (dp - dl)).astype(BF16)
            return (dk + _nt(qt, ds), dv + _nt(dov, p.astype(BF16))), _nn(kt, ds)

        def add_dq(i, r):
            dq_acc[i] += 0.125 * r[:hd]
            dqa_ref[i] += r[hd:]

        def block(i, carry, masked):
            carry, r = grads(queries(i), do_ref[i], lse_ref[i], dl_ref[i, pl.ds(h, 1), :], carry, masked)
            add_dq(i, r)
            return carry

        def two_blocks(t, carry):
            i = j + 1 + 2 * t
            both = lambda f: jnp.concatenate([f(i), f(i + 1)], axis=1)
            carry, r = grads(both(queries), both(lambda n: do_ref[n]), both(lambda n: lse_ref[n]),
                             both(lambda n: dl_ref[n, pl.ds(h, 1), :]), carry, False)
            add_dq(i, r[:, :B])
            add_dq(i + 1, r[:, B:])
            return carry

        last = im_ref[h * nb + j]
        pairs = (last - j) // 2
        carry = block(j, (jnp.zeros((hd + AUG, B), F32), jnp.zeros((hd, B), F32)), True)
        carry = lax.fori_loop(0, pairs, two_blocks, carry)
        dk, dv = lax.fori_loop(j + 1 + 2 * pairs, last + 1, lambda i, c: block(i, c, False), carry)
        dk_ref[...] = dk[:hd].astype(BF16)
        dka_ref[...] = dk[hd:]
        dv_ref[...] = dv.astype(BF16)

        @pl.when(j == nb - 1)
        def _():
            dq_ref[...] = dq_acc[...].astype(BF16)

    head_rows = lambda off: pl.BlockSpec((nb, hd, B), lambda h, j, im: (0, off + h, 0))
    key_rows = lambda off: pl.BlockSpec((None, hd, B), lambda h, j, im: (j, off + h, 0))
    aug_all = pl.BlockSpec((None, nb, AUG, B), lambda h, j, im: (h, 0, 0, 0))
    aug_one = pl.BlockSpec((None, None, AUG, B), lambda h, j, im: (h, j, 0, 0))
    grid_spec = pltpu.PrefetchScalarGridSpec(
        num_scalar_prefetch=1, grid=(H, nb),
        in_specs=[key_rows(H), key_rows(2 * H), aug_one, head_rows(0), aug_all, head_rows(0),
                  pl.BlockSpec((None, nb, 1, B), lambda h, j, im: (h, 0, 0, 0)),
                  pl.BlockSpec((nb, H, B), lambda h, j, im: (0, 0, 0))],
        out_specs=[head_rows(0), aug_all, key_rows(0), aug_one, key_rows(0)],
        scratch_shapes=[pltpu.VMEM((nb, hd, B), F32)])
    return pl.pallas_call(
        body, name=name, grid_spec=grid_spec,
        out_shape=[SDS((nb, aw, B), BF16), SDS((H, nb, AUG, B), F32), SDS((nb, aw, B), BF16),
                   SDS((H, nb, AUG, B), F32), SDS((nb, aw, B), BF16)],
        compiler_params=_cp("arbitrary", "arbitrary"))(imax, zat, zat, ka, zat, qa, dot, lse, delta)


def _stack_t(dq_ref, dk_ref, dv_ref, dzf_ref):
    dzf = dzf_ref[...]
    tail = jnp.concatenate([dzf, jnp.zeros((TAIL - dzf.shape[0], dzf.shape[1]), F32)], axis=0).astype(BF16)
    return jnp.concatenate([dq_ref[...], dk_ref[...], dv_ref[...], tail], axis=0)


def _mix_dwt(dq, dk, dv, dzft, h, name):
    nb, aw, B = dq.shape
    D = h.shape[1]

    def body(dq_ref, dk_ref, dv_ref, dzf_ref, h_ref, o_ref):
        _accumulate(o_ref, pl.program_id(0) == 0, _nn(_stack_t(dq_ref, dk_ref, dv_ref, dzf_ref), h_ref[...]))

    blk = pl.BlockSpec((None, aw, B), lambda i: (i, 0, 0))
    return pl.pallas_call(
        body, name=name, grid=(nb,),
        in_specs=[blk, blk, blk, pl.BlockSpec((SUBLANES, B), lambda i: (0, i)), pl.BlockSpec((B, D), lambda i: (i, 0))],
        out_specs=pl.BlockSpec((3 * aw + TAIL, D), lambda i: (0, 0)), out_shape=SDS((3 * aw + TAIL, D), F32),
        compiler_params=_cp("arbitrary"))(dq, dk, dv, dzft, h)


def _mix_bwd_in(dzc, dq, dk, dv, dzft, wc, waf, x, g, dres, name, scatter):
    T, D = x.shape
    nb, aw, B = dq.shape

    def body(dzc_ref, dq_ref, dk_ref, dv_ref, dzf_ref, wc_ref, wa_ref, x_ref, g_ref, r_ref, dx_ref, dxb_ref, dg_ref):
        dh = _nn(dzc_ref[...], wc_ref[...]) + _tn(_stack_t(dq_ref, dk_ref, dv_ref, dzf_ref), wa_ref[...])
        _norm_bwd_store(dh, x_ref, g_ref, r_ref, dx_ref, dxb_ref, dg_ref, pl.program_id(0) == 0)

    blk = pl.BlockSpec((None, aw, B), lambda i: (i, 0, 0))
    row = lambda n: pl.BlockSpec((B, n), lambda i: (i, 0))
    return _hosted_call(
        body, name, nb,
        [row(dzc.shape[1]), blk, blk, blk, pl.BlockSpec((SUBLANES, B), lambda i: (0, i)),
         _full(wc), _full(waf), row(D), _full(g), row(D)],
        [row(D), row(D), pl.BlockSpec((1, D), lambda i: (0, 0))],
        [SDS((T, D), F32), SDS((T, D), BF16), SDS((1, D), F32)],
        (dzc, dq, dk, dv, dzft, wc, waf, x, g, dres), scatter)


def _mem_kv(mem, g, wkv, name):
    M, D = mem.shape
    dx = wkv.shape[2]

    def body(m_ref, g_ref, w_ref, kv_ref, h_ref):
        mv = m_ref[...]
        h = (mv * _rstd(mv) * g_ref[...]).astype(BF16)
        h_ref[...] = h
        for s in range(N_DEV):
            kv_ref[s] = _nn(h, w_ref[s]).astype(BF16)

    return pl.pallas_call(
        body, name=name, grid=(1,), in_specs=[_full(mem), _full(g), _full(wkv)],
        out_specs=[pl.BlockSpec((N_DEV, M, dx), lambda i: (0, 0, 0)), pl.BlockSpec((M, D), lambda i: (0, 0))],
        out_shape=[SDS((N_DEV, M, dx), BF16), SDS((M, D), BF16)], compiler_params=_cp("arbitrary"))(mem, g, wkv)


def _mem_kv_bwd(dkv, hm, wkv, mem, g, name):
    M, D = mem.shape
    dx = wkv.shape[2]

    def body(dkv_ref, h_ref, w_ref, m_ref, g_ref, dw_ref, dg_ref):
        hv = h_ref[...]
        dh = jnp.zeros((M, D), F32)
        for s in range(N_DEV):
            d = dkv_ref[s].astype(BF16)
            dw_ref[s] = _tn(hv, d).astype(BF16)
            dh = dh + _nt(d, w_ref[s])
        mv = m_ref[...]
        dg_ref[...] = jnp.sum(dh * (mv * _rstd(mv)), axis=0, keepdims=True)

    return pl.pallas_call(
        body, name=name, grid=(1,), in_specs=[_full(dkv), _full(hm), _full(wkv), _full(mem), _full(g)],
        out_specs=[pl.BlockSpec((N_DEV, D, dx), lambda i: (0, 0, 0)), pl.BlockSpec((1, D), lambda i: (0, 0))],
        out_shape=[SDS((N_DEV, D, dx), BF16), SDS((1, D), F32)], compiler_params=_cp("arbitrary"))(dkv, hm, wkv, mem, g)


def _xattn_probs(q_ref, kv_ref, h, dx, scale):
    qh = q_ref[:, h * dx:(h + 1) * dx]
    kh = kv_ref[h]
    s = _nt(qh, kh) * scale
    p = jnp.exp(s - jnp.max(s, axis=-1, keepdims=True))
    return qh, kh, p / jnp.sum(p, axis=-1, keepdims=True)


def _xattn(x, g, wq, kv, wo, name, tm=512):
    T, D = x.shape
    dx = D // N_XHEADS
    scale = dx ** -0.5
    tm = _tile(T, tm, SUBLANES)

    def body(x_ref, g_ref, wq_ref, kv_ref, wo_ref, xo_ref, h_ref, q_ref, o_ref):
        xv = x_ref[...]
        h = (xv * _rstd(xv) * g_ref[...]).astype(BF16)
        h_ref[...] = h
        q_ref[...] = _nn(h, _rows2d(wq_ref)).astype(BF16)
        for hd in range(N_XHEADS):
            _, _, p = _xattn_probs(q_ref, kv_ref, hd, dx, scale)
            o_ref[:, hd * dx:(hd + 1) * dx] = _nn(p.astype(BF16), kv_ref[N_XHEADS + hd]).astype(BF16)
        xo_ref[...] = xv + _nn(o_ref[...], _rows2d(wo_ref))

    row = pl.BlockSpec((tm, D), lambda i: (i, 0))
    return pl.pallas_call(
        body, name=name, grid=(T // tm,), in_specs=[row, _full(g), _full(wq), _full(kv), _full(wo)],
        out_specs=[row] * 4, out_shape=[SDS((T, D), F32)] + [SDS((T, D), BF16)] * 3,
        compiler_params=_cp("parallel"))(x, g, wq, kv, wo)


def _xattn_bwd(dyb, dres, x, g, q, kv, wq, wo, name, tm=512):
    T, D = x.shape
    dx = D // N_XHEADS
    scale = dx ** -0.5
    tm = _tile(T, tm, SUBLANES)

    def body(dy_ref, r_ref, x_ref, g_ref, q_ref, kv_ref, wq_ref, wo_ref, dx_ref, dxb_ref, dg_ref, dq_ref, dkv_ref):
        first = pl.program_id(0) == 0

        @pl.when(first)
        def _():
            dkv_ref[...] = jnp.zeros_like(dkv_ref)
        do = _nt(dy_ref[...], _rows2d(wo_ref)).astype(BF16)
        for hd in range(N_XHEADS):
            qh, kh, p = _xattn_probs(q_ref, kv_ref, hd, dx, scale)
            doh = do[:, hd * dx:(hd + 1) * dx]
            dp = _nt(doh, kv_ref[N_XHEADS + hd])
            ds = (p * (dp - jnp.sum(p * dp, axis=-1, keepdims=True)) * scale).astype(BF16)
            dq_ref[:, hd * dx:(hd + 1) * dx] = _nn(ds, kh).astype(BF16)
            dkv_ref[hd] += _tn(ds, qh)
            dkv_ref[N_XHEADS + hd] += _tn(p.astype(BF16), doh)
        dh = _nt(dq_ref[...], _rows2d(wq_ref))
        _norm_bwd_store(dh, x_ref, g_ref, r_ref, dx_ref, dxb_ref, dg_ref, first)

    row = pl.BlockSpec((tm, D), lambda i: (i, 0))
    return pl.pallas_call(
        body, name=name, grid=(T // tm,),
        in_specs=[row, row, row, _full(g), row, _full(kv), _full(wq), _full(wo)],
        out_specs=[row, row, pl.BlockSpec((1, D), lambda i: (0, 0)), row, pl.BlockSpec(kv.shape, lambda i: (0, 0, 0))],
        out_shape=[SDS((T, D), F32), SDS((T, D), BF16), SDS((1, D), F32), SDS((T, D), BF16), SDS(kv.shape, F32)],
        compiler_params=_cp("arbitrary"))(dyb, dres, x, g, q, kv, wq, wo)


def _mesh_position():
    return lax.axis_index("x"), lax.axis_index("y"), lax.axis_index("c")


class _GatherCopies:
    def __init__(self, x_refs, out_refs, send_sems, recv_sems, local_sems):
        self.x_refs, self.out_refs, self.n = x_refs, out_refs, len(x_refs)
        self.send_sems, self.recv_sems, self.local_sems = send_sems, recv_sems, local_sems
        x, y, c = _mesh_position()
        self.c = c
        self.me, self.sibling = (x, y, c), (x, y, 1 - c)
        self.chips = [(1 - x, y), (x, 1 - y), (1 - x, 1 - y)]

    def slot(self, w, px, py, pc):
        return self.out_refs[w].at[4 * px + 2 * py + pc]

    def copy(self, w, k, block, to, src=None):
        return pltpu.make_async_remote_copy(
            src_ref=self.slot(w, *block) if src is None else src, dst_ref=self.slot(w, *block),
            send_sem=self.send_sems.at[w, k], recv_sem=self.recv_sems.at[w, k], device_id=to,
            device_id_type=pl.DeviceIdType.MESH)

    def mine(self, w):
        return pltpu.make_async_copy(self.x_refs[w], self.slot(w, *self.me), self.local_sems.at[w])

    def first(self, w):
        src = self.x_refs[w]
        return [self.copy(w, 0, self.me, self.sibling, src=src)] + [
            self.copy(w, 1 + j, self.me, (*chip, self.c), src=src) for j, chip in enumerate(self.chips)]

    def passed(self, w):
        return [self.copy(w, 4 + j, (*chip, self.c), self.sibling) for j, chip in enumerate(self.chips)]


def _gather_start(*refs):
    cp = _GatherCopies(*refs)
    for w in range(cp.n):
        cp.mine(w).start()
        for f in cp.first(w):
            f.start()


def _gather_relay(*refs):
    cp = _GatherCopies(*refs)
    for w in range(cp.n):
        for j, (chip, fwd) in enumerate(zip(cp.chips, cp.passed(w))):
            cp.copy(w, 1 + j, (*chip, cp.c), cp.me).wait_recv()
            fwd.start()


def _gather_finish(*refs):
    cp = _GatherCopies(*refs)
    for w in range(cp.n):
        cp.copy(w, 0, cp.sibling, cp.me).wait_recv()
        for j, chip in enumerate(cp.chips):
            cp.copy(w, 4 + j, (*chip, 1 - cp.c), cp.me).wait_recv()
    for w in range(cp.n):
        for f in cp.first(w) + cp.passed(w):
            f.wait_send()
        cp.mine(w).wait()


def _all_gather(shards, name):
    n = len(shards)

    def body(*refs):
        parts = (refs[:n], refs[n:2 * n]) + refs[2 * n:]
        _gather_start(*parts)
        _gather_relay(*parts)
        _gather_finish(*parts)

    any_spec = pl.BlockSpec(memory_space=pl.ANY)
    return pl.pallas_call(
        body, name=name, out_shape=[SDS((N_DEV,) + s.shape, s.dtype) for s in shards],
        in_specs=[any_spec] * n, out_specs=[any_spec] * n,
        scratch_shapes=[pltpu.SemaphoreType.DMA((n, 7)), pltpu.SemaphoreType.DMA((n, 7)), pltpu.SemaphoreType.DMA((n,))],
    )(*shards)


def _scatter_copies(x_refs, out_refs, send_sems, recv_sems, local_sems):
    x, y, c = _mesh_position()
    me = 4 * x + 2 * y + c
    own, remote = [], []
    for w, (src, dst) in enumerate(zip(x_refs, out_refs)):
        own.append(pltpu.make_async_copy(src.at[me], dst.at[me], local_sems.at[w]))
        for k in range(1, N_DEV):
            px = 1 - x if k & 4 else x
            py = 1 - y if k & 2 else y
            pc = 1 - c if k & 1 else c
            remote.append(pltpu.make_async_remote_copy(
                src_ref=src.at[4 * px + 2 * py + pc], dst_ref=dst.at[me], send_sem=send_sems.at[w, k - 1],
                recv_sem=recv_sems.at[w, k - 1], device_id=(px, py, pc), device_id_type=pl.DeviceIdType.MESH))
    return own, remote


def _scatter_start(x_refs, out_refs, send_sems, recv_sems, local_sems):
    own, remote = _scatter_copies(x_refs, out_refs, send_sems, recv_sems, local_sems)
    for cp in own + remote:
        cp.start()


def _scatter_wait(x_refs, out_refs, send_sems, recv_sems, local_sems):
    own, remote = _scatter_copies(x_refs, out_refs, send_sems, recv_sems, local_sems)
    for cp in remote:
        cp.wait_recv()
    for cp in remote:
        cp.wait_send()
    for cp in own:
        cp.wait()


def _sum_adamw(parts, w, m, v, name, tr=256):
    L, R, C = w.shape
    tr = _tile(R, tr, SUBLANES)
    nblk = R // tr

    def body(*refs):
        p_refs = refs[:L]
        w_ref, m_ref, v_ref, g_ref, d_ref, mo_ref, vo_ref = refs[L:]
        layer = pl.program_id(0)
        g = None
        for k, p_ref in enumerate(p_refs):
            gk = p_ref[0].astype(F32)
            for s in range(1, N_DEV):
                gk = gk + p_ref[s].astype(F32)
            g = gk if g is None else jnp.where(layer == k, gk, g)
        mn = ADAM_B1 * m_ref[...] + (1.0 - ADAM_B1) * g
        vn = ADAM_B2 * v_ref[...] + (1.0 - ADAM_B2) * jnp.square(g)
        m_hat = mn / (1.0 - ADAM_B1 ** ADAM_STEP)
        v_hat = vn / (1.0 - ADAM_B2 ** ADAM_STEP)
        g_ref[...] = g
        d_ref[...] = -ADAM_LR * (m_hat / (jnp.sqrt(v_hat) + ADAM_EPS) + ADAM_WD * w_ref[...])
        mo_ref[...] = mn
        vo_ref[...] = vn

    def part_spec(k):
        return pl.BlockSpec((N_DEV, tr, C),
                            lambda l, i: (0, jnp.where(l == k, i, jnp.where(l < k, 0, nblk - 1)), 0))

    row = pl.BlockSpec((None, tr, C), lambda l, i: (l, i, 0))
    return pl.pallas_call(
        body, name=name, grid=(L, nblk), in_specs=[part_spec(k) for k in range(L)] + [row, row, row],
        out_specs=[row] * 4, out_shape=[SDS((L, R, C), F32)] * 4,
        compiler_params=_cp("arbitrary", "arbitrary"))(*parts, w, m, v)


def _pack(flat_list, row_mult):
    flat = jnp.concatenate(flat_list, axis=-1)
    n = flat.shape[-1]
    chunk = row_mult * PACK_COLS
    pad = -n % chunk
    flat = jnp.pad(flat, [(0, 0)] * (flat.ndim - 1) + [(0, pad)])
    return flat.reshape(flat.shape[:-1] + ((n + pad) // PACK_COLS, PACK_COLS))


def _unpack(packed, shapes):
    lead = packed.shape[:-2]
    flat = packed.reshape(lead + (-1,))
    out = []
    off = 0
    for shp in shapes:
        n = 1
        for d in shp:
            n *= d
        out.append(flat[..., off:off + n].reshape(lead + tuple(shp)))
        off += n
    return out


def _ffn_bwd(dx, dxb, saved, wgu, wd, tag):
    x, g, h, ab, act = saved
    dgu = _ffn_bwd_act(dxb, wd, ab, f"{tag}_bwd_act")
    dwd = _ffn_dwd(act, dxb, f"{tag}_dwd")
    dwgu = _ffn_dwgu(h, dgu, f"{tag}_dwgu")
    D = dwd.shape[-1]
    return _ffn_bwd_in(dgu, wgu, x, g, dx, f"{tag}_bwd_in", [dwgu.reshape(N_DEV, -1, D), dwd.reshape(N_DEV, -1, D)])


def _mix_fwd(x, g, w_in8, w_conv8, b_f, g_conv, g_att, w_out, tag):
    T, D = x.shape
    w_int = w_in8.reshape(-1, D)
    cw = D // 2
    aw = D - cw
    H = aw // HEAD_DIM
    B = _tile(T, ATT_BLOCK, LANES)
    nb = T // B
    w_c = w_int[:3 * cw]
    w_af = jnp.pad(w_int[3 * cw:], ((0, TAIL - H), (0, 0)))
    zc, h, ync = _mix_in_conv(x, g, w_c, w_conv8, g_conv, f"{tag}_in_conv")
    zat, zft8, norms = _proj_t(h, w_af, B, f"{tag}_in_att")
    zft = zft8[:H]
    bcol = b_f.reshape(H, 1)
    c, qa, ka = _logf_cumsum(zft, bcol, B, f"{tag}_cumsum")
    cb = c.reshape(H, nb, B)
    jmin, imax = _skip_table(cb[:, :, 0:1], cb[:, :, B - 1].reshape(H, 1, nb), norms, f"{tag}_skip")
    ot, lse = _attn_fwd(jmin.reshape(H * nb), zat, qa, ka, f"{tag}_attn")
    g_col = g_att.reshape(aw, 1)
    yna = _gnorm_t_fwd(ot, g_col, f"{tag}_gnorm")
    xo = _mm_res([ync, yna], w_out, x, f"{tag}_out")
    saved = (x, g, h, zc, zft, bcol, zat, qa, ka, imax.reshape(H * nb), ot, lse, ync, yna, w_c, w_af, g_col)
    return xo, saved


def _mix_bwd(dx, dxb, saved, w_conv8, g_conv, w_out, tag, also_scatter):
    x, g, h, zc, zft, bcol, zat, qa, ka, imax, ot, lse, ync, yna, w_c, w_af, g_col = saved
    T, D = x.shape
    H = zft.shape[0]
    dw_out = jnp.concatenate([_mm_tn(ync, dxb, f"{tag}_dwout_c"), _mm_tn(yna, dxb, f"{tag}_dwout_a")], axis=0)
    dzc, dwc8, dg_conv, dot, delta, dg_att, *p_also = _mix_bwd_head(dxb, w_out, zc, w_conv8, g_conv, ot, g_col,
                                                                   f"{tag}_bwd_out", also_scatter)
    dq, dqa, dk, dka, dv = _attn_bwd(imax, zat, qa, ka, dot, lse, delta, f"{tag}_attn_bwd")
    dcq = dqa[:, :, 3, :].reshape(H, T)
    dck = dka[:, :, 0, :].reshape(H, T)
    dzft, db = _logf_cumsum_bwd(dcq, dck, zft, bcol, f"{tag}_cumsum_bwd")
    dzft8 = jnp.pad(dzft, ((0, SUBLANES - H), (0, 0)))
    dw_c = _mm_tn(dzc, h, f"{tag}_dwin_c")
    dw_af = _mix_dwt(dq, dk, dv, dzft8, h, f"{tag}_dwin_a")
    dw_int = jnp.concatenate([dw_c, dw_af[:dw_af.shape[0] - TAIL + H].astype(BF16)], axis=0)
    scatter = [dw_int.reshape(N_DEV, -1, D), dwc8[:3].reshape(3, N_DEV, -1).transpose(1, 0, 2),
               dw_out.reshape(N_DEV, D // N_DEV, D)]
    dx, dxb, dg, p_in, p_conv, p_out = _mix_bwd_in(dzc, dq, dk, dv, dzft8, w_c, w_af, x, g, dx, f"{tag}_bwd_in",
                                                   scatter)
    return dx, dxb, dg, p_in, p_conv, db.reshape(H), dg_conv, dg_att[:, 0], p_out, p_also


def _xattn_block_fwd(x, g, mem, g_mem, w_q, w_kv, w_o, tag):
    kv, hm = _mem_kv(mem, g_mem, w_kv, f"{tag}_kv")
    xo, h, q, o = _xattn(x, g, w_q, kv, w_o, f"{tag}_block")
    return xo, (x, g, h, mem, g_mem, hm, q, kv, o)


def _xattn_block_bwd(dx, dxb, saved, w_q, w_kv, w_o, tag):
    x, g, h, mem, g_mem, hm, q, kv, o = saved
    dw_o = _mm_tn(o, dxb, f"{tag}_dwo")
    dx, dxb, dg, dq, dkv = _xattn_bwd(dxb, dx, x, g, q, kv, w_q, w_o, f"{tag}_block_bwd")
    dw_q = _mm_tn(h, dq, f"{tag}_dwq")
    dw_kv, dg_mem = _mem_kv_bwd(dkv, hm, w_kv, mem, g_mem, f"{tag}_kv_bwd")
    D = dw_q.shape[0]
    return dx, dxb, dg, dg_mem, [dw_q.reshape(N_DEV, D // N_DEV, D), dw_kv, dw_o.reshape(N_DEV, D // N_DEV, D)]


GATHER_FIRST = ['w_ffn1_gu']
GATHER_REST = ['w_ffn1_down', 'w_mix_in', 'w_mix_out', 'w_xq', 'w_xkv', 'w_xo', 'w_ffn2_gu', 'w_ffn2_down']
GATHER_NEXT_A = ['w_ffn1_gu', 'w_ffn1_down', 'w_mix_in', 'w_mix_out', 'w_xq', 'w_xkv', 'w_xo']
GATHER_NEXT_B = ['w_ffn2_gu', 'w_ffn2_down']


def _local_step(x, mem, tgt, P):
    L = P['g_ffn1'].shape[0]
    shards = lambda names, l: [(P[n][l].T if n in LONG_AXIS_LAST else P[n][l]).astype(BF16) for n in names]
    wgut = lambda n, l: W[n][l].reshape(-1, W[n][l].shape[-1])
    W = {n: [None] * L for n in SHARDED}

    def keep(names, l, arrays):
        for n, a in zip(names, arrays):
            W[n][l] = a

    *first, conv = _all_gather(shards(GATHER_FIRST, 0) + [P['w_conv']], "gather_first")
    keep(GATHER_FIRST, 0, first)
    w_conv = conv.transpose(1, 2, 0, 3).reshape(L, 3, -1)
    saved = []
    for l in range(L):
        row = lambda n: P[n][l][None, :]
        wc8 = jnp.pad(w_conv[l], ((0, SUBLANES - 3), (0, 0)))
        x1 = x
        bring = GATHER_REST if l == 0 else []
        ab, act, h, *got = _ffn_gu(x, row('g_ffn1'), wgut('w_ffn1_gu', l), f"l{l}_ffn1_gu", gather=shards(bring, l))
        keep(bring, l, got)
        x, = _ffn_down(act, W['w_ffn1_down'][l], x, f"l{l}_ffn1_down")
        s1 = (x1, row('g_ffn1'), h, ab, act)
        x, s2 = _mix_fwd(x, row('g_mix'), W['w_mix_in'][l], wc8, P['b_f'][l], row('g_conv_out'), P['g_att_out'][l],
                         W['w_mix_out'][l], f"l{l}_mix")
        x, s3 = _xattn_block_fwd(x, row('g_xattn'), mem, row('g_mem'), W['w_xq'][l], W['w_xkv'][l], W['w_xo'][l],
                                 f"l{l}_xattn")
        x4 = x
        bring_a, bring_b = (GATHER_NEXT_A, GATHER_NEXT_B) if l + 1 < L else ([], [])
        ab, act, h, *got = _ffn_gu(x, row('g_ffn2'), wgut('w_ffn2_gu', l), f"l{l}_ffn2_gu", gather=shards(bring_a, l + 1))
        keep(bring_a, l + 1, got)
        x, *got = _ffn_down(act, W['w_ffn2_down'][l], x, f"l{l}_ffn2_down", gather=shards(bring_b, l + 1))
        keep(bring_b, l + 1, got)
        saved.append((s1, s2, s3, (x4, row('g_ffn2'), h, ab, act), wc8))
    loss, dx, dxb, dg_final = _loss_head(x, P['g_final'][None, :], tgt, "loss_head")
    G = {n: [None] * L for n in WEIGHT_NAMES if n != 'g_final'}
    for l in reversed(range(L)):
        row = lambda n: P[n][l][None, :]
        s1, s2, s3, s4, wc8 = saved[l]
        dx, dxb, G['g_ffn2'][l], G['w_ffn2_gu'][l], G['w_ffn2_down'][l] = _ffn_bwd(
            dx, dxb, s4, wgut('w_ffn2_gu', l), W['w_ffn2_down'][l], f"l{l}_ffn2")
        dx, dxb, G['g_xattn'][l], G['g_mem'][l], xattn_grads = _xattn_block_bwd(
            dx, dxb, s3, W['w_xq'][l], W['w_xkv'][l], W['w_xo'][l], f"l{l}_xattn")
        (dx, dxb, G['g_mix'][l], G['w_mix_in'][l], G['w_conv'][l], G['b_f'][l], G['g_conv_out'][l], G['g_att_out'][l],
         G['w_mix_out'][l], (G['w_xq'][l], G['w_xkv'][l], G['w_xo'][l])) = _mix_bwd(
            dx, dxb, s2, wc8, row('g_conv_out'), W['w_mix_out'][l], f"l{l}_mix", xattn_grads)
        dx, dxb, G['g_ffn1'][l], G['w_ffn1_gu'][l], G['w_ffn1_down'][l] = _ffn_bwd(
            dx, dxb, s1, wgut('w_ffn1_gu', l), W['w_ffn1_down'][l], f"l{l}_ffn1")
    G['g_final'] = [dg_final]
    return loss, dx, G


def kernel(x, mem, g_ffn1, w_ffn1_gu, w_ffn1_down, g_mix, w_mix_in, w_conv, b_f, g_conv_out, g_att_out, w_mix_out, g_xattn, g_mem, w_xq, w_xkv, w_xo, g_ffn2, w_ffn2_gu, w_ffn2_down, g_final, loss_target, m_g_ffn1, m_w_ffn1_gu, m_w_ffn1_down, m_g_mix, m_w_mix_in, m_w_conv, m_b_f, m_g_conv_out, m_g_att_out, m_w_mix_out, m_g_xattn, m_g_mem, m_w_xq, m_w_xkv, m_w_xo, m_g_ffn2, m_w_ffn2_gu, m_w_ffn2_down, m_g_final, v_g_ffn1, v_w_ffn1_gu, v_w_ffn1_down, v_g_mix, v_w_mix_in, v_w_conv, v_b_f, v_g_conv_out, v_g_att_out, v_w_mix_out, v_g_xattn, v_g_mem, v_w_xq, v_w_xkv, v_w_xo, v_g_ffn2, v_w_ffn2_gu, v_w_ffn2_down, v_g_final):
    args = (x, mem, g_ffn1, w_ffn1_gu, w_ffn1_down, g_mix, w_mix_in, w_conv, b_f, g_conv_out, g_att_out, w_mix_out,
            g_xattn, g_mem, w_xq, w_xkv, w_xo, g_ffn2, w_ffn2_gu, w_ffn2_down, g_final)
    P = dict(zip(IN_NAMES, args))
    moms = (m_g_ffn1, m_w_ffn1_gu, m_w_ffn1_down, m_g_mix, m_w_mix_in, m_w_conv, m_b_f, m_g_conv_out, m_g_att_out,
            m_w_mix_out, m_g_xattn, m_g_mem, m_w_xq, m_w_xkv, m_w_xo, m_g_ffn2, m_w_ffn2_gu, m_w_ffn2_down, m_g_final)
    vars_ = (v_g_ffn1, v_w_ffn1_gu, v_w_ffn1_down, v_g_mix, v_w_mix_in, v_w_conv, v_b_f, v_g_conv_out, v_g_att_out,
             v_w_mix_out, v_g_xattn, v_g_mem, v_w_xq, v_w_xkv, v_w_xo, v_g_ffn2, v_w_ffn2_gu, v_w_ffn2_down, v_g_final)
    MOM = dict(zip(WEIGHT_NAMES, moms))
    VAR = dict(zip(WEIGHT_NAMES, vars_))

    loss_part, dx, grads = _local_step(x[0], mem[0], loss_target[0], {n: P[n] for n in WEIGHT_NAMES})

    small_shapes = [P[n].shape for n in REPLICATED] + [(1,)]
    small = _pack([a.reshape(-1) for n in REPLICATED for a in grads[n]] + [loss_part[0, :1]], SUBLANES)
    small_parts = _all_gather([small], "gather_small_grads")[0]

    res = {}
    for n in SHARDED:
        t = (lambda a: a.transpose(0, 2, 1)) if n in LONG_AXIS_LAST else (lambda a: a)
        res[n] = [t(o) for o in _sum_adamw(grads[n], t(P[n]), t(MOM[n]), t(VAR[n]), "adamw_" + n)]
    zero = [jnp.zeros((1,), F32)]
    w_, m_, v_ = (_pack([d[n].reshape(-1) for n in REPLICATED] + zero, SUBLANES)[None] for d in (P, MOM, VAR))
    small_out = [_unpack(o[0], small_shapes) for o in _sum_adamw([small_parts], w_, m_, v_, "adamw_vectors")]
    for k, n in enumerate(REPLICATED):
        res[n] = [o[k] for o in small_out]
    out = []
    for k in range(4):
        out += [res[n][k] for n in WEIGHT_NAMES]
    return (small_out[0][-1].reshape(()), dx[None], *out)
```

```python
import jax
import jax.numpy as jnp
from jax import lax
from jax.experimental import pallas as pl
from jax.experimental.pallas import tpu as pltpu

F32 = jnp.float32
BF16 = jnp.bfloat16
I32 = jnp.int32
SDS = jax.ShapeDtypeStruct

EPS = 1e-6
HEAD_DIM = 64
N_XHEADS = 4
N_DEV = 8
LANES = 128
SUBLANES = 8
MXU_COLS = 256
AUG = 16
TAIL = 16
ATT_BLOCK = 512
NEG = -1e30
SKIP_MARGIN = 115.0
PACK_COLS = 1024
VMEM_LIMIT = 58 * 1024 * 1024

ADAM_LR = 0.001
ADAM_B1 = 0.9
ADAM_B2 = 0.999
ADAM_EPS = 1e-08
ADAM_WD = 0.01
ADAM_STEP = 10

IN_NAMES = ['x', 'mem', 'g_ffn1', 'w_ffn1_gu', 'w_ffn1_down', 'g_mix', 'w_mix_in', 'w_conv', 'b_f', 'g_conv_out',
            'g_att_out', 'w_mix_out', 'g_xattn', 'g_mem', 'w_xq', 'w_xkv', 'w_xo', 'g_ffn2', 'w_ffn2_gu',
            'w_ffn2_down', 'g_final']
WEIGHT_NAMES = IN_NAMES[2:]
SHARDED = ['w_ffn1_gu', 'w_ffn1_down', 'w_mix_in', 'w_conv', 'w_mix_out', 'w_xq', 'w_xkv', 'w_xo', 'w_ffn2_gu',
           'w_ffn2_down']
REPLICATED = [n for n in WEIGHT_NAMES if n not in SHARDED]
LONG_AXIS_LAST = ['w_ffn1_gu', 'w_ffn2_gu', 'w_mix_in']


def _tile(n, pref, mult):
    t = min(pref, n) // mult * mult
    while t >= mult:
        if n % t == 0:
            return t
        t -= mult
    return n


def _cp(*sem):
    return pltpu.CompilerParams(dimension_semantics=sem, vmem_limit_bytes=VMEM_LIMIT)


def _nt(a, b):
    return lax.dot_general(a, b, (((1,), (1,)), ((), ())), preferred_element_type=F32)


def _tn(a, b):
    return lax.dot_general(a, b, (((0,), (0,)), ((), ())), preferred_element_type=F32)


def _nn(a, b):
    return jnp.dot(a, b, preferred_element_type=F32)


def _rstd(xv):
    return lax.rsqrt(jnp.mean(xv * xv, axis=-1, keepdims=True) + EPS)


def _full(a):
    nd = a.ndim
    return pl.BlockSpec(a.shape, lambda *_: (0,) * nd, pipeline_mode=pl.Buffered(1))


def _rows2d(w_ref):
    s, a, b = w_ref.shape
    return w_ref[...].reshape(s * a, b)


def _accumulate(ref, first, part):
    @pl.when(first)
    def _():
        ref[...] = part

    @pl.when(jnp.logical_not(first))
    def _():
        ref[...] += part


def _norm_bwd_store(dh, x_ref, g_ref, r_ref, dx_ref, dxb_ref, dg_ref, first):
    xv = x_ref[...]
    r = _rstd(xv)
    xh = xv * r
    dxh = dh * g_ref[...]
    dx = r * (dxh - xh * jnp.mean(dxh * xh, axis=-1, keepdims=True)) + r_ref[...]
    dx_ref[...] = dx
    dxb_ref[...] = dx.astype(BF16)
    _accumulate(dg_ref, first, jnp.sum(dh * xh, axis=0, keepdims=True))


def _mm_res(a_list, w, res, name, tm=512):
    T = res.shape[0]
    N = w.shape[-1]
    tm = _tile(T, tm, SUBLANES)
    n_a = len(a_list)

    def body(*refs):
        a_refs = refs[:n_a]
        w_ref, r_ref, o_ref = refs[n_a:]
        wv = _rows2d(w_ref)
        acc = r_ref[...]
        off = 0
        for a_ref in a_refs:
            k = a_ref.shape[1]
            acc = acc + _nn(a_ref[...], wv[off:off + k, :])
            off += k
        o_ref[...] = acc

    row = lambda n: pl.BlockSpec((tm, n), lambda i: (i, 0))
    return pl.pallas_call(
        body, name=name, grid=(T // tm,),
        in_specs=[row(a.shape[1]) for a in a_list] + [_full(w), row(N)], out_specs=row(N),
        out_shape=SDS((T, N), F32), compiler_params=_cp("parallel"))(*a_list, w, res)


def _mm_tn(a, b, name, out_dtype=BF16, tt=2048):
    T, K = a.shape
    N = b.shape[1]
    tt = _tile(T, tt, 2 * SUBLANES)
    nt = T // tt

    def body(a_ref, b_ref, o_ref, acc_ref):
        t = pl.program_id(0)
        _accumulate(acc_ref, t == 0, _tn(a_ref[...], b_ref[...]))

        @pl.when(t == nt - 1)
        def _():
            o_ref[...] = acc_ref[...].astype(o_ref.dtype)

    return pl.pallas_call(
        body, name=name, grid=(nt,),
        in_specs=[pl.BlockSpec((tt, K), lambda t: (t, 0)), pl.BlockSpec((tt, N), lambda t: (t, 0))],
        out_specs=pl.BlockSpec((K, N), lambda t: (0, 0)), out_shape=SDS((K, N), out_dtype),
        scratch_shapes=[pltpu.VMEM((K, N), F32)], compiler_params=_cp("arbitrary"))(a, b)


def _hosted_call(body, name, n_steps, in_specs, out_specs, out_shape, args, scatter=(), gather=(), scratch_shapes=()):
    scatter, gather, scratch_shapes = list(scatter), list(gather), list(scratch_shapes)
    n_s, n_g = len(scatter), len(gather)
    if n_s + n_g == 0:
        return pl.pallas_call(body, name=name, grid=(n_steps,), in_specs=list(in_specs), out_specs=list(out_specs),
                              out_shape=list(out_shape), scratch_shapes=scratch_shapes,
                              compiler_params=_cp("arbitrary"))(*args)
    n_in, n_out, n_scr = len(in_specs), len(out_specs), len(scratch_shapes)
    relay_step = min(max(1, 3 * n_steps // 4), n_steps - 1)

    def wrapped(*refs):
        ins, refs = refs[:n_in], refs[n_in:]
        xs, shards, refs = refs[:n_s], refs[n_s:n_s + n_g], refs[n_s + n_g:]
        outs, refs = refs[:n_out], refs[n_out:]
        recvs, gathered, refs = refs[:n_s], refs[n_s:n_s + n_g], refs[n_s + n_g:]
        outs, sems = outs + refs[:n_scr], refs[n_scr:]
        s_sems, g_sems = (sems[:3], sems[3:]) if n_s else ((), sems)
        step = pl.program_id(0)

        @pl.when(step == 0)
        def _():
            if n_s:
                _scatter_start(xs, recvs, *s_sems)
            if n_g:
                _gather_start(shards, gathered, *g_sems)
        body(*ins, *outs)
        if n_g:
            @pl.when(step == relay_step)
            def _():
                _gather_relay(shards, gathered, *g_sems)

        @pl.when(step == n_steps - 1)
        def _():
            if n_s:
                _scatter_wait(xs, recvs, *s_sems)
            if n_g:
                _gather_finish(shards, gathered, *g_sems)

    any_spec = pl.BlockSpec(memory_space=pl.ANY)
    sems = []
    for n in (n_s, n_g):
        if n:
            sems += [pltpu.SemaphoreType.DMA((n, N_DEV - 1)), pltpu.SemaphoreType.DMA((n, N_DEV - 1)),
                     pltpu.SemaphoreType.DMA((n,))]
    return pl.pallas_call(
        wrapped, name=name, grid=(n_steps,), in_specs=list(in_specs) + [any_spec] * (n_s + n_g),
        out_specs=list(out_specs) + [any_spec] * (n_s + n_g),
        out_shape=list(out_shape) + [SDS(b.shape, b.dtype) for b in scatter]
        + [SDS((N_DEV,) + b.shape, b.dtype) for b in gather],
        scratch_shapes=scratch_shapes + sems, compiler_params=_cp("arbitrary"))(*args, *scatter, *gather)


def _hidden_chunks(F):
    edges = list(range(0, F, 3 * MXU_COLS)) + [F]
    return list(zip(edges[:-1], edges[1:]))


def _ffn_gu(x, g, wgut, name, gather=(), tm=512):
    T, D = x.shape
    F = wgut.shape[0] // 2
    tm = _tile(T, tm, SUBLANES)

    def body(x_ref, g_ref, w_ref, ab_ref, act_ref, h_ref):
        xv = x_ref[...]
        h = (xv * _rstd(xv) * g_ref[...]).astype(BF16)
        h_ref[...] = h
        for c0, c1 in _hidden_chunks(F):
            gt = _nt(h, w_ref[c0:c1, :])
            up = _nt(h, w_ref[F + c0:F + c1, :])
            sg = jax.nn.sigmoid(gt)
            silu = gt * sg
            ab_ref[0, :, c0:c1] = (0.5 * up * (sg * (1.0 + gt * (1.0 - sg)))).astype(BF16)
            ab_ref[1, :, c0:c1] = (0.5 * silu).astype(BF16)
            act_ref[:, c0:c1] = (silu * up).astype(BF16)

    row = lambda n: pl.BlockSpec((tm, n), lambda i: (i, 0))
    return _hosted_call(
        body, name, T // tm, [row(D), _full(g), _full(wgut)],
        [pl.BlockSpec((2, tm, F), lambda i: (0, i, 0)), row(F), row(D)],
        [SDS((2, T, F), BF16), SDS((T, F), BF16), SDS((T, D), BF16)], (x, g, wgut), gather=gather)


def _ffn_down(act, wd, res, name, gather=(), tm=1024):
    T, F = act.shape
    D = wd.shape[2]
    tm = _tile(T, tm, SUBLANES)

    def body(a_ref, w_ref, r_ref, o_ref):
        o_ref[...] = r_ref[...] + 0.5 * _nn(a_ref[...], _rows2d(w_ref))

    row = lambda n: pl.BlockSpec((tm, n), lambda i: (i, 0))
    return _hosted_call(body, name, T // tm, [row(F), _full(wd), row(D)], [row(D)], [SDS((T, D), F32)],
                        (act, wd, res), gather=gather)


def _ffn_bwd_act(dyb, wd, ab, name, tm=512):
    T, D = dyb.shape
    _, _, F = ab.shape
    tm = _tile(T, tm, SUBLANES)

    def body(dy_ref, w_ref, ab_ref, o_ref):
        wv = _rows2d(w_ref)
        dy = dy_ref[...]
        for c0, c1 in _hidden_chunks(F):
            da = _nt(dy, wv[c0:c1, :])
            o_ref[0, :, c0:c1] = (da * ab_ref[0, :, c0:c1].astype(F32)).astype(BF16)
            o_ref[1, :, c0:c1] = (da * ab_ref[1, :, c0:c1].astype(F32)).astype(BF16)

    blocks = pl.BlockSpec((2, tm, F), lambda i: (0, i, 0))
    return pl.pallas_call(
        body, name=name, grid=(T // tm,),
        in_specs=[pl.BlockSpec((tm, D), lambda i: (i, 0)), _full(wd), blocks], out_specs=blocks,
        out_shape=SDS((2, T, F), BF16), compiler_params=_cp("parallel"))(dyb, wd, ab)


def _ffn_dwd(act, dyb, name, tt=2048):
    T, F = act.shape
    D = dyb.shape[1]
    tt = _tile(T, tt, 2 * SUBLANES)
    nt = T // tt
    fh = F // 2

    def body(a_ref, b_ref, o_ref, acc_ref):
        t = pl.program_id(1)
        _accumulate(acc_ref, t == 0, _tn(a_ref[...], b_ref[...]))

        @pl.when(t == nt - 1)
        def _():
            o_ref[...] = (0.5 * acc_ref[...]).astype(BF16)

    return pl.pallas_call(
        body, name=name, grid=(2, nt),
        in_specs=[pl.BlockSpec((tt, fh), lambda j, t: (t, j)), pl.BlockSpec((tt, D), lambda j, t: (t, 0))],
        out_specs=pl.BlockSpec((fh, D), lambda j, t: (j, 0)), out_shape=SDS((F, D), BF16),
        scratch_shapes=[pltpu.VMEM((fh, D), F32)], compiler_params=_cp("parallel", "arbitrary"))(act, dyb)


def _ffn_dwgu(h, dgu, name, tt=2048):
    T, D = h.shape
    _, _, F = dgu.shape
    tt = _tile(T, tt, 2 * SUBLANES)
    nt = T // tt
    fh = F // 2

    def body(a_ref, b_ref, o_ref, acc_ref):
        t = pl.program_id(1)
        _accumulate(acc_ref, t == 0, _tn(b_ref[...], a_ref[...]))

        @pl.when(t == nt - 1)
        def _():
            o_ref[...] = acc_ref[...].astype(BF16)

    return pl.pallas_call(
        body, name=name, grid=(4, nt),
        in_specs=[pl.BlockSpec((tt, D), lambda q, t: (t, 0)),
                  pl.BlockSpec((None, tt, fh), lambda q, t: (q // 2, t, q % 2))],
        out_specs=pl.BlockSpec((fh, D), lambda q, t: (q, 0)), out_shape=SDS((2 * F, D), BF16),
        scratch_shapes=[pltpu.VMEM((fh, D), F32)], compiler_params=_cp("parallel", "arbitrary"))(h, dgu)


def _ffn_bwd_in(dgu, wgut, x, g, dres, name, scatter, tm=512):
    T, D = x.shape
    _, _, F = dgu.shape
    tm = _tile(T, tm, SUBLANES)

    def body(a_ref, w_ref, x_ref, g_ref, r_ref, dx_ref, dxb_ref, dg_ref):
        dh = _nn(a_ref[0], w_ref[:F, :]) + _nn(a_ref[1], w_ref[F:, :])
        _norm_bwd_store(dh, x_ref, g_ref, r_ref, dx_ref, dxb_ref, dg_ref, pl.program_id(0) == 0)

    row = pl.BlockSpec((tm, D), lambda i: (i, 0))
    return _hosted_call(
        body, name, T // tm, [pl.BlockSpec((2, tm, F), lambda i: (0, i, 0)), _full(wgut), row, _full(g), row],
        [row, row, pl.BlockSpec((1, D), lambda i: (0, 0))],
        [SDS((T, D), F32), SDS((T, D), BF16), SDS((1, D), F32)], (dgu, wgut, x, g, dres), scatter)


def _loss_head(x, g, tgt, name, tm=512):
    T, D = x.shape
    tm = _tile(T, tm, SUBLANES)

    def body(x_ref, g_ref, t_ref, loss_ref, dx_ref, dxb_ref, dg_ref):
        first = pl.program_id(0) == 0
        xv = x_ref[...]
        r = _rstd(xv)
        xh = xv * r
        e = xh * g_ref[...] - t_ref[...]
        part = 0.5 * jnp.sum(jnp.mean(e * e, axis=-1, keepdims=True), axis=0, keepdims=True)
        dy = e * (1.0 / D)
        dxh = dy * g_ref[...]
        dx = r * (dxh - xh * jnp.mean(dxh * xh, axis=-1, keepdims=True))
        dx_ref[...] = dx
        dxb_ref[...] = dx.astype(BF16)
        _accumulate(loss_ref, first, jnp.broadcast_to(part, loss_ref.shape))
        _accumulate(dg_ref, first, jnp.sum(dy * xh, axis=0, keepdims=True))

    row = pl.BlockSpec((tm, D), lambda i: (i, 0))
    vec = pl.BlockSpec((1, D), lambda i: (0, 0))
    return pl.pallas_call(
        body, name=name, grid=(T // tm,), in_specs=[row, vec, row],
        out_specs=[pl.BlockSpec((SUBLANES, LANES), lambda i: (0, 0)), row, row, vec],
        out_shape=[SDS((SUBLANES, LANES), F32), SDS((T, D), F32), SDS((T, D), BF16), SDS((1, D), F32)],
        compiler_params=_cp("arbitrary"))(x, g, tgt)


def _shift_down(u, k, prev_row):
    rows = lax.broadcasted_iota(I32, u.shape, 0)
    s = pltpu.roll(u, k, 0)
    for t in range(k):
        s = jnp.where(rows == t, prev_row(SUBLANES - k + t), s)
    return s


def _shift_up(u, k, next_row):
    n = u.shape[0]
    rows = lax.broadcasted_iota(I32, u.shape, 0)
    s = pltpu.roll(u, n - k, 0)
    for t in range(k):
        s = jnp.where(rows == n - k + t, next_row(t), s)
    return s


def _conv_taps(z_ref, zp_ref, w_ref, cw, first):
    u = z_ref[:, cw:2 * cw] * z_ref[:, 2 * cw:]

    def prev_row(r):
        return jnp.where(first, 0.0, zp_ref[r:r + 1, cw:2 * cw] * zp_ref[r:r + 1, 2 * cw:])

    u1 = _shift_down(u, 1, prev_row)
    u2 = _shift_down(u, 2, prev_row)
    cv = w_ref[0:1, :] * u2 + w_ref[1:2, :] * u1 + w_ref[2:3, :] * u
    return u, u1, u2, cv


def _mix_in_conv(x, g, w, wc, gc, name, tm=512):
    T, D = x.shape
    cw = w.shape[0] // 3
    tm = _tile(T, tm, SUBLANES)

    def body(x_ref, g_ref, w_ref, wc_ref, gc_ref, z_ref, h_ref, o_ref, prev_ref):
        xv = x_ref[...]
        h = (xv * _rstd(xv) * g_ref[...]).astype(BF16)
        h_ref[...] = h
        z_ref[...] = _nt(h, w_ref[...])
        _, _, _, cv = _conv_taps(z_ref, prev_ref, wc_ref, cw, pl.program_id(0) == 0)
        y = z_ref[:, :cw] * cv
        o_ref[...] = (y * _rstd(y) * gc_ref[...]).astype(BF16)
        prev_ref[...] = z_ref[tm - SUBLANES:, :]

    row = lambda n: pl.BlockSpec((tm, n), lambda i: (i, 0))
    return pl.pallas_call(
        body, name=name, grid=(T // tm,), in_specs=[row(D), _full(g), _full(w), _full(wc), _full(gc)],
        out_specs=[row(3 * cw), row(D), row(cw)],
        out_shape=[SDS((T, 3 * cw), F32), SDS((T, D), BF16), SDS((T, cw), BF16)],
        scratch_shapes=[pltpu.VMEM((SUBLANES, 3 * cw), F32)], compiler_params=_cp("arbitrary"))(x, g, w, wc, gc)


def _mix_bwd_head(dyb, w_out, zc, wc, gc, ot, g_col, name, scatter):
    nb, aw, tm = ot.shape
    T, D = dyb.shape
    cw = zc.shape[1] // 3
    H = aw // HEAD_DIM
    hb = tm // SUBLANES

    def body(dyb_ref, wo_ref, z_ref, zp_ref, w_ref, g_ref, o_ref, ga_ref,
             dz_ref, dw_ref, dg_ref, do_ref, dl_ref, dga_ref, carry_ref):
        s = pl.program_id(0)
        dy = _nt(dyb_ref[...], _rows2d(wo_ref))
        ov = o_ref[...]
        ra = lax.rsqrt(jnp.mean(ov * ov, axis=0, keepdims=True) + EPS)
        oh = ov * ra
        dya = dy[:, cw:].T
        dyah = dya * ga_ref[...]
        dov = ra * (dyah - oh * jnp.mean(dyah * oh, axis=0, keepdims=True))
        do_ref[...] = dov.astype(BF16)
        dl_ref[...] = jnp.sum((dov * ov).reshape(H, HEAD_DIM, tm), axis=1)
        _accumulate(dga_ref, s == 0, jnp.broadcast_to(jnp.sum(dya * oh, axis=1, keepdims=True), dga_ref.shape))
        u, u1, u2, cv = _conv_taps(z_ref, zp_ref, w_ref, cw, s == nb - 1)
        zb = z_ref[:, :cw]
        y = zb * cv
        r = _rstd(y)
        yh = y * r
        dyn = dy[:, :cw]
        dyh = dyn * g_ref[...]
        dyc = r * (dyh - yh * jnp.mean(dyh * yh, axis=-1, keepdims=True))
        dcv = dyc * zb

        def next_row(t):
            return jnp.where(s == 0, 0.0, carry_ref[t:t + 1, :])

        du = w_ref[2:3, :] * dcv + w_ref[1:2, :] * _shift_up(dcv, 1, next_row) + w_ref[0:1, :] * _shift_up(dcv, 2, next_row)
        carry_ref[...] = dcv[0:SUBLANES, :]
        dz_ref[:, :cw] = (dyc * cv).astype(BF16)
        dz_ref[:, cw:2 * cw] = (du * z_ref[:, 2 * cw:]).astype(BF16)
        dz_ref[:, 2 * cw:] = (du * z_ref[:, cw:2 * cw]).astype(BF16)
        tap = lax.broadcasted_iota(I32, (SUBLANES, cw), 0)
        dwp = jnp.where(tap == 0, jnp.sum(dcv * u2, axis=0, keepdims=True),
                        jnp.where(tap == 1, jnp.sum(dcv * u1, axis=0, keepdims=True),
                                  jnp.where(tap == 2, jnp.sum(dcv * u, axis=0, keepdims=True), 0.0)))
        _accumulate(dw_ref, s == 0, dwp)
        _accumulate(dg_ref, s == 0, jnp.sum(dyn * yh, axis=0, keepdims=True))

    rev = lambda s: nb - 1 - s
    rows = lambda n: pl.BlockSpec((tm, n), lambda s: (rev(s), 0))
    blk = pl.BlockSpec((None, aw, tm), lambda s: (rev(s), 0, 0))
    return _hosted_call(
        body, name, nb,
        [rows(D), _full(w_out), rows(3 * cw),
         pl.BlockSpec((SUBLANES, 3 * cw), lambda s: (jnp.maximum(rev(s) * hb - 1, 0), 0)),
         _full(wc), _full(gc), blk, _full(g_col)],
        [rows(3 * cw), pl.BlockSpec((SUBLANES, cw), lambda s: (0, 0)), pl.BlockSpec((1, cw), lambda s: (0, 0)),
         blk, pl.BlockSpec((None, H, tm), lambda s: (rev(s), 0, 0)), pl.BlockSpec((aw, LANES), lambda s: (0, 0))],
        [SDS((T, 3 * cw), BF16), SDS((SUBLANES, cw), F32), SDS((1, cw), F32),
         SDS((nb, aw, tm), BF16), SDS((nb, H, tm), F32), SDS((aw, LANES), F32)],
        (dyb, w_out, zc, zc, wc, gc, ot, g_col), scatter=scatter, scratch_shapes=[pltpu.VMEM((SUBLANES, cw), F32)])


def _proj_t(h, waf, B, name):
    T, D = h.shape
    R = waf.shape[0] - TAIL
    nb = T // B
    qk = 2 * R // 3
    g = qk // HEAD_DIM

    def body(h_ref, w_ref, za_ref, zf_ref, n_ref):
        z = _nt(w_ref[...], h_ref[...])
        zab = z[:R].astype(BF16)
        za_ref[...] = zab
        zf_ref[...] = z[R:R + SUBLANES]
        zv = zab[:qk].astype(F32)
        ss = jnp.sum((zv * zv).reshape(g, HEAD_DIM, B), axis=1)
        mx = jnp.broadcast_to(jnp.max(ss, axis=1, keepdims=True), n_ref.shape)

        @pl.when(pl.program_id(0) == 0)
        def _():
            n_ref[...] = mx

        @pl.when(pl.program_id(0) > 0)
        def _():
            n_ref[...] = jnp.maximum(n_ref[...], mx)

    return pl.pallas_call(
        body, name=name, grid=(nb,), in_specs=[pl.BlockSpec((B, D), lambda i: (i, 0)), _full(waf)],
        out_specs=[pl.BlockSpec((None, R, B), lambda i: (i, 0, 0)), pl.BlockSpec((SUBLANES, B), lambda i: (0, i)),
                   pl.BlockSpec((g, LANES), lambda i: (0, 0))],
        out_shape=[SDS((nb, R, B), BF16), SDS((SUBLANES, T), F32), SDS((g, LANES), F32)],
        compiler_params=_cp("arbitrary"))(h, waf)


def _split3(v):
    hi = v.astype(BF16).astype(F32)
    r1 = v - hi
    mid = r1.astype(BF16).astype(F32)
    lo = (r1 - mid).astype(BF16).astype(F32)
    return hi, mid, lo


def _tri_dot3(v, tri):
    hi, mid, lo = _split3(v)
    return _nn(hi.astype(BF16), tri) + _nn(mid.astype(BF16), tri) + _nn(lo.astype(BF16), tri)


def _logf_cumsum(zft, b, tb, name):
    H, T = zft.shape

    def body(z_ref, b_ref, c_ref, qa_ref, ka_ref, carry_ref, hi_ref, mid_ref, lo_ref):
        @pl.when(pl.program_id(0) == 0)
        def _():
            carry_ref[...] = jnp.zeros_like(carry_ref)
        xv = z_ref[...] + b_ref[...]
        lf = jnp.minimum(xv, 0.0) - jnp.log(1.0 + jnp.exp(-jnp.abs(xv)))
        src = lax.broadcasted_iota(I32, (tb, tb), 0)
        dst = lax.broadcasted_iota(I32, (tb, tb), 1)
        tri = jnp.where(src <= dst, 1.0, 0.0).astype(BF16)
        cs = _tri_dot3(lf, tri) + carry_ref[...]
        c_ref[...] = cs
        hi_ref[...], mid_ref[...], lo_ref[...] = _split3(cs)
        col = lax.broadcasted_iota(I32, cs.shape, 1)
        carry_ref[...] = jnp.sum(jnp.where(col == tb - 1, cs, 0.0), axis=1, keepdims=True)
        row = lax.broadcasted_iota(I32, (AUG, tb), 0)
        for h in range(H):
            terms = [r[h:h + 1, :] for r in (hi_ref, mid_ref, lo_ref)]
            qa, ka = jnp.where(row < 3, 1.0, 0.0), jnp.where((row >= 3) & (row < 6), 1.0, 0.0)
            for k, t in enumerate(terms):
                qa = jnp.where(row == 3 + k, t, qa)
                ka = jnp.where(row == k, -t, ka)
            qa_ref[h] = qa
            ka_ref[h] = ka

    blk = pl.BlockSpec((H, tb), lambda i: (0, i))
    aug = pl.BlockSpec((H, None, AUG, tb), lambda i: (0, i, 0, 0))
    return pl.pallas_call(
        body, name=name, grid=(T // tb,), in_specs=[blk, _full(b)], out_specs=[blk, aug, aug],
        out_shape=[SDS((H, T), F32), SDS((H, T // tb, AUG, tb), F32), SDS((H, T // tb, AUG, tb), F32)],
        scratch_shapes=[pltpu.VMEM((H, 1), F32)] + [pltpu.VMEM((H, tb), F32)] * 3,
        compiler_params=_cp("arbitrary"))(zft, b)


def _logf_cumsum_bwd(dcq, dck, zft, b, name, tb=512):
    H, T = zft.shape
    tb = _tile(T, tb, LANES)
    nb = T // tb

    def body(dq_ref, dk_ref, z_ref, b_ref, o_ref, db_ref, carry_ref):
        s = pl.program_id(0)

        @pl.when(s == 0)
        def _():
            carry_ref[...] = jnp.zeros_like(carry_ref)
        dc = dq_ref[...] - dk_ref[...]
        src = lax.broadcasted_iota(I32, (tb, tb), 0)
        dst = lax.broadcasted_iota(I32, (tb, tb), 1)
        tri = jnp.where(src >= dst, 1.0, 0.0).astype(BF16)
        dl = _tri_dot3(dc, tri) + carry_ref[...]
        col = lax.broadcasted_iota(I32, dl.shape, 1)
        carry_ref[...] = jnp.sum(jnp.where(col == 0, dl, 0.0), axis=1, keepdims=True)
        dz = dl * jax.nn.sigmoid(-(z_ref[...] + b_ref[...]))
        o_ref[...] = dz
        _accumulate(db_ref, s == 0, jnp.sum(dz, axis=1, keepdims=True))

    blk = pl.BlockSpec((H, tb), lambda s: (0, nb - 1 - s))
    vec = pl.BlockSpec((H, 1), lambda s: (0, 0))
    return pl.pallas_call(
        body, name=name, grid=(nb,), in_specs=[blk, blk, blk, vec], out_specs=[blk, vec],
        out_shape=[SDS((H, T), F32), SDS((H, 1), F32)],
        scratch_shapes=[pltpu.VMEM((H, 1), F32)], compiler_params=_cp("arbitrary"))(dcq, dck, zft, b)


def _skip_table(cs_col, ce_row, norms, name):
    H, nb, _ = cs_col.shape

    def body(cs_ref, ce_ref, n_ref, jm_ref, im_ref):
        h = pl.program_id(0)
        nq = n_ref[pl.ds(h, 1), 0:1]
        nk = n_ref[pl.ds(H + h, 1), 0:1]
        bound = 2.0 * jnp.sqrt(nq * nk) * 0.125
        skip = jnp.where(bound + cs_ref[...] - ce_ref[...] <= -SKIP_MARGIN, 1, 0).astype(I32)
        jm_ref[...] = jnp.sum(skip, axis=1, keepdims=True)
        im_ref[...] = nb - 1 - jnp.sum(skip, axis=0, keepdims=True)

    col = pl.BlockSpec((None, nb, 1), lambda h: (h, 0, 0))
    row = pl.BlockSpec((None, 1, nb), lambda h: (h, 0, 0))
    return pl.pallas_call(
        body, name=name, grid=(H,), in_specs=[col, row, _full(norms)], out_specs=[col, row],
        out_shape=[SDS((H, nb, 1), I32), SDS((H, 1, nb), I32)], compiler_params=_cp("parallel"))(cs_col, ce_row, norms)


def _causal_mask(B):
    krow = lax.broadcasted_iota(I32, (B, B), 0)
    qcol = lax.broadcasted_iota(I32, (B, B), 1)
    return krow <= qcol


def _attn_fwd(jmin, zat, qa, ka, name):
    nb, R, B = zat.shape
    aw = R // 3
    H = aw // HEAD_DIM
    hd = HEAD_DIM

    hp = 2 if H % 2 == 0 else 1

    def body(jm_ref, q_ref, k_ref, v_ref, qa_ref, ka_ref, o_ref, lse_ref):
        for n in range(hp):
            one_head(n, pl.program_id(0) * hp + n, jm_ref, q_ref, k_ref, v_ref, qa_ref, ka_ref, o_ref, lse_ref)

    def one_head(n, h, jm_ref, q_ref, k_ref, v_ref, qa_ref, ka_ref, o_ref, lse_ref):
        i = pl.program_id(1)
        rows = slice(n * hd, (n + 1) * hd)
        qt = jnp.concatenate([q_ref[rows, :] * 0.125, qa_ref[n].astype(BF16)], axis=0)

        def keys(j):
            return jnp.concatenate([k_ref[j, rows, :], ka_ref[n, j].astype(BF16)], axis=0)

        def values(j):
            return v_ref[j, rows, :]

        def update(s, vt, carry):
            m, l, acc = carry
            m_new = jnp.maximum(m, jnp.max(s, axis=0, keepdims=True))
            p = jnp.exp(s - m_new)
            a = jnp.exp(m - m_new)
            l = a * l + jnp.sum(p, axis=0, keepdims=True)
            acc = a * acc + _nn(vt, p.astype(BF16))
            return m_new, l, acc

        def block(j, carry, masked):
            s = _tn(keys(j), qt)
            if masked:
                s = jnp.where(_causal_mask(B), s, NEG)
            return update(s, values(j), carry)

        def two_blocks(t, carry):
            j = j0 + 2 * t
            s = _tn(jnp.concatenate([keys(j), keys(j + 1)], axis=1), qt)
            return update(s, jnp.concatenate([values(j), values(j + 1)], axis=1), carry)

        j0 = jm_ref[h * nb + i]
        pairs = (i - j0) // 2
        init = (jnp.full((1, B), NEG, F32), jnp.zeros((1, B), F32), jnp.zeros((hd, B), F32))
        carry = lax.fori_loop(0, pairs, two_blocks, init)
        carry = lax.fori_loop(j0 + 2 * pairs, i, lambda j, c: block(j, c, False), carry)
        m, l, acc = block(i, carry, True)
        o_ref[rows, :] = acc / l
        lse_ref[n] = m + jnp.log(l)

    G = H // hp
    grid_spec = pltpu.PrefetchScalarGridSpec(
        num_scalar_prefetch=1, grid=(G, nb),
        in_specs=[pl.BlockSpec((None, hp * hd, B), lambda h, i, jm: (i, h, 0)),
                  pl.BlockSpec((nb, hp * hd, B), lambda h, i, jm: (0, G + h, 0)),
                  pl.BlockSpec((nb, hp * hd, B), lambda h, i, jm: (0, 2 * G + h, 0)),
                  pl.BlockSpec((hp, None, AUG, B), lambda h, i, jm: (h, i, 0, 0)),
                  pl.BlockSpec((hp, nb, AUG, B), lambda h, i, jm: (h, 0, 0, 0))],
        out_specs=[pl.BlockSpec((None, hp * hd, B), lambda h, i, jm: (i, h, 0)),
                   pl.BlockSpec((hp, None, 1, B), lambda h, i, jm: (h, i, 0, 0))])
    return pl.pallas_call(
        body, name=name, grid_spec=grid_spec,
        out_shape=[SDS((nb, aw, B), F32), SDS((H, nb, 1, B), F32)],
        compiler_params=_cp("parallel", "parallel"))(jmin, zat, zat, zat, qa, ka)


def _gnorm_t_fwd(ot, g_col, name):
    nb, aw, B = ot.shape

    def body(o_ref, g_ref, y_ref):
        ov = o_ref[...]
        r = lax.rsqrt(jnp.mean(ov * ov, axis=0, keepdims=True) + EPS)
        y_ref[...] = (ov * r * g_ref[...]).T.astype(BF16)

    return pl.pallas_call(
        body, name=name, grid=(nb,), in_specs=[pl.BlockSpec((None, aw, B), lambda i: (i, 0, 0)), _full(g_col)],
        out_specs=pl.BlockSpec((B, aw), lambda i: (i, 0)), out_shape=SDS((nb * B, aw), BF16),
        compiler_params=_cp("parallel"))(ot, g_col)


def _attn_bwd(imax, zat, qa, ka, dot, lse, delta, name):
    nb, R, B = zat.shape
    aw = R // 3
    H = aw // HEAD_DIM
    hd = HEAD_DIM

    def body(im_ref, k_ref, v_ref, ka_ref, q_ref, qa_ref, do_ref, lse_ref, dl_ref,
             dq_ref, dqa_ref, dk_ref, dka_ref, dv_ref, dq_acc):
        h = pl.program_id(0)
        j = pl.program_id(1)

        @pl.when(j == 0)
        def _():
            dq_acc[...] = jnp.zeros_like(dq_acc)
            dqa_ref[...] = jnp.zeros_like(dqa_ref)
        kt = jnp.concatenate([k_ref[...], ka_ref[...].astype(BF16)], axis=0)
        vt = v_ref[...]

        def queries(i):
            return jnp.concatenate([q_ref[i] * 0.125, qa_ref[i].astype(BF16)], axis=0)

        def grads(qt, dov, lse, dl, carry, masked):
            dk, dv = carry
            s = _tn(kt, qt)
            if masked:
                s = jnp.where(_causal_mask(B), s, NEG)
            p = jnp.exp(s - lse)
            dp = _tn(vt, dov)
            ds = (p * (dp - dl)).astype(BF16)
            return (dk + _nt(qt, ds), dv + _nt(dov, p.astype(BF16))), _nn(kt, ds)

        def add_dq(i, r):
            dq_acc[i] += 0.125 * r[:hd]
            dqa_ref[i] += r[hd:]

        def block(i, carry, masked):
            carry, r = grads(queries(i), do_ref[i], lse_ref[i], dl_ref[i, pl.ds(h, 1), :], carry, masked)
            add_dq(i, r)
            return carry

        def two_blocks(t, carry):
            i = j + 1 + 2 * t
            both = lambda f: jnp.concatenate([f(i), f(i + 1)], axis=1)
            carry, r = grads(both(queries), both(lambda n: do_ref[n]), both(lambda n: lse_ref[n]),
                             both(lambda n: dl_ref[n, pl.ds(h, 1), :]), carry, False)
            add_dq(i, r[:, :B])
            add_dq(i + 1, r[:, B:])
            return carry

        last = im_ref[h * nb + j]
        pairs = (last - j) // 2
        carry = block(j, (jnp.zeros((hd + AUG, B), F32), jnp.zeros((hd, B), F32)), True)
        carry = lax.fori_loop(0, pairs, two_blocks, carry)
        dk, dv = lax.fori_loop(j + 1 + 2 * pairs, last + 1, lambda i, c: block(i, c, False), carry)
        dk_ref[...] = dk[:hd].astype(BF16)
        dka_ref[...] = dk[hd:]
        dv_ref[...] = dv.astype(BF16)

        @pl.when(j == nb - 1)
        def _():
            dq_ref[...] = dq_acc[...].astype(BF16)

    head_rows = lambda off: pl.BlockSpec((nb, hd, B), lambda h, j, im: (0, off + h, 0))
    key_rows = lambda off: pl.BlockSpec((None, hd, B), lambda h, j, im: (j, off + h, 0))
    aug_all = pl.BlockSpec((None, nb, AUG, B), lambda h, j, im: (h, 0, 0, 0))
    aug_one = pl.BlockSpec((None, None, AUG, B), lambda h, j, im: (h, j, 0, 0))
    grid_spec = pltpu.PrefetchScalarGridSpec(
        num_scalar_prefetch=1, grid=(H, nb),
        in_specs=[key_rows(H), key_rows(2 * H), aug_one, head_rows(0), aug_all, head_rows(0),
                  pl.BlockSpec((None, nb, 1, B), lambda h, j, im: (h, 0, 0, 0)),
                  pl.BlockSpec((nb, H, B), lambda h, j, im: (0, 0, 0))],
        out_specs=[head_rows(0), aug_all, key_rows(0), aug_one, key_rows(0)],
        scratch_shapes=[pltpu.VMEM((nb, hd, B), F32)])
    return pl.pallas_call(
        body, name=name, grid_spec=grid_spec,
        out_shape=[SDS((nb, aw, B), BF16), SDS((H, nb, AUG, B), F32), SDS((nb, aw, B), BF16),
                   SDS((H, nb, AUG, B), F32), SDS((nb, aw, B), BF16)],
        compiler_params=_cp("arbitrary", "arbitrary"))(imax, zat, zat, ka, zat, qa, dot, lse, delta)


def _stack_t(dq_ref, dk_ref, dv_ref, dzf_ref):
    dzf = dzf_ref[...]
    tail = jnp.concatenate([dzf, jnp.zeros((TAIL - dzf.shape[0], dzf.shape[1]), F32)], axis=0).astype(BF16)
    return jnp.concatenate([dq_ref[...], dk_ref[...], dv_ref[...], tail], axis=0)


def _mix_dwt(dq, dk, dv, dzft, h, name):
    nb, aw, B = dq.shape
    D = h.shape[1]

    def body(dq_ref, dk_ref, dv_ref, dzf_ref, h_ref, o_ref):
        _accumulate(o_ref, pl.program_id(0) == 0, _nn(_stack_t(dq_ref, dk_ref, dv_ref, dzf_ref), h_ref[...]))

    blk = pl.BlockSpec((None, aw, B), lambda i: (i, 0, 0))
    return pl.pallas_call(
        body, name=name, grid=(nb,),
        in_specs=[blk, blk, blk, pl.BlockSpec((SUBLANES, B), lambda i: (0, i)), pl.BlockSpec((B, D), lambda i: (i, 0))],
        out_specs=pl.BlockSpec((3 * aw + TAIL, D), lambda i: (0, 0)), out_shape=SDS((3 * aw + TAIL, D), F32),
        compiler_params=_cp("arbitrary"))(dq, dk, dv, dzft, h)


def _mix_bwd_in(dzc, dq, dk, dv, dzft, wc, waf, x, g, dres, name, scatter):
    T, D = x.shape
    nb, aw, B = dq.shape

    def body(dzc_ref, dq_ref, dk_ref, dv_ref, dzf_ref, wc_ref, wa_ref, x_ref, g_ref, r_ref, dx_ref, dxb_ref, dg_ref):
        dh = _nn(dzc_ref[...], wc_ref[...]) + _tn(_stack_t(dq_ref, dk_ref, dv_ref, dzf_ref), wa_ref[...])
        _norm_bwd_store(dh, x_ref, g_ref, r_ref, dx_ref, dxb_ref, dg_ref, pl.program_id(0) == 0)

    blk = pl.BlockSpec((None, aw, B), lambda i: (i, 0, 0))
    row = lambda n: pl.BlockSpec((B, n), lambda i: (i, 0))
    return _hosted_call(
        body, name, nb,
        [row(dzc.shape[1]), blk, blk, blk, pl.BlockSpec((SUBLANES, B), lambda i: (0, i)),
         _full(wc), _full(waf), row(D), _full(g), row(D)],
        [row(D), row(D), pl.BlockSpec((1, D), lambda i: (0, 0))],
        [SDS((T, D), F32), SDS((T, D), BF16), SDS((1, D), F32)],
        (dzc, dq, dk, dv, dzft, wc, waf, x, g, dres), scatter)


def _mem_kv(mem, g, wkv, name):
    M, D = mem.shape
    dx = wkv.shape[2]

    def body(m_ref, g_ref, w_ref, kv_ref, h_ref):
        mv = m_ref[...]
        h = (mv * _rstd(mv) * g_ref[...]).astype(BF16)
        h_ref[...] = h
        for s in range(N_DEV):
            kv_ref[s] = _nn(h, w_ref[s]).astype(BF16)

    return pl.pallas_call(
        body, name=name, grid=(1,), in_specs=[_full(mem), _full(g), _full(wkv)],
        out_specs=[pl.BlockSpec((N_DEV, M, dx), lambda i: (0, 0, 0)), pl.BlockSpec((M, D), lambda i: (0, 0))],
        out_shape=[SDS((N_DEV, M, dx), BF16), SDS((M, D), BF16)], compiler_params=_cp("arbitrary"))(mem, g, wkv)


def _mem_kv_bwd(dkv, hm, wkv, mem, g, name):
    M, D = mem.shape
    dx = wkv.shape[2]

    def body(dkv_ref, h_ref, w_ref, m_ref, g_ref, dw_ref, dg_ref):
        hv = h_ref[...]
        dh = jnp.zeros((M, D), F32)
        for s in range(N_DEV):
            d = dkv_ref[s].astype(BF16)
            dw_ref[s] = _tn(hv, d).astype(BF16)
            dh = dh + _nt(d, w_ref[s])
        mv = m_ref[...]
        dg_ref[...] = jnp.sum(dh * (mv * _rstd(mv)), axis=0, keepdims=True)

    return pl.pallas_call(
        body, name=name, grid=(1,), in_specs=[_full(dkv), _full(hm), _full(wkv), _full(mem), _full(g)],
        out_specs=[pl.BlockSpec((N_DEV, D, dx), lambda i: (0, 0, 0)), pl.BlockSpec((1, D), lambda i: (0, 0))],
        out_shape=[SDS((N_DEV, D, dx), BF16), SDS((1, D), F32)], compiler_params=_cp("arbitrary"))(dkv, hm, wkv, mem, g)


def _xattn_probs(q_ref, kv_ref, h, dx, scale):
    qh = q_ref[:, h * dx:(h + 1) * dx]
    kh = kv_ref[h]
    s = _nt(qh, kh) * scale
    p = jnp.exp(s - jnp.max(s, axis=-1, keepdims=True))
    return qh, kh, p / jnp.sum(p, axis=-1, keepdims=True)


def _xattn(x, g, wq, kv, wo, name, tm=512):
    T, D = x.shape
    dx = D // N_XHEADS
    scale = dx ** -0.5
    tm = _tile(T, tm, SUBLANES)

    def body(x_ref, g_ref, wq_ref, kv_ref, wo_ref, xo_ref, h_ref, q_ref, o_ref):
        xv = x_ref[...]
        h = (xv * _rstd(xv) * g_ref[...]).astype(BF16)
        h_ref[...] = h
        q_ref[...] = _nn(h, _rows2d(wq_ref)).astype(BF16)
        for hd in range(N_XHEADS):
            _, _, p = _xattn_probs(q_ref, kv_ref, hd, dx, scale)
            o_ref[:, hd * dx:(hd + 1) * dx] = _nn(p.astype(BF16), kv_ref[N_XHEADS + hd]).astype(BF16)
        xo_ref[...] = xv + _nn(o_ref[...], _rows2d(wo_ref))

    row = pl.BlockSpec((tm, D), lambda i: (i, 0))
    return pl.pallas_call(
        body, name=name, grid=(T // tm,), in_specs=[row, _full(g), _full(wq), _full(kv), _full(wo)],
        out_specs=[row] * 4, out_shape=[SDS((T, D), F32)] + [SDS((T, D), BF16)] * 3,
        compiler_params=_cp("parallel"))(x, g, wq, kv, wo)


def _xattn_bwd(dyb, dres, x, g, q, kv, wq, wo, name, tm=512):
    T, D = x.shape
    dx = D // N_XHEADS
    scale = dx ** -0.5
    tm = _tile(T, tm, SUBLANES)

    def body(dy_ref, r_ref, x_ref, g_ref, q_ref, kv_ref, wq_ref, wo_ref, dx_ref, dxb_ref, dg_ref, dq_ref, dkv_ref):
        first = pl.program_id(0) == 0

        @pl.when(first)
        def _():
            dkv_ref[...] = jnp.zeros_like(dkv_ref)
        do = _nt(dy_ref[...], _rows2d(wo_ref)).astype(BF16)
        for hd in range(N_XHEADS):
            qh, kh, p = _xattn_probs(q_ref, kv_ref, hd, dx, scale)
            doh = do[:, hd * dx:(hd + 1) * dx]
            dp = _nt(doh, kv_ref[N_XHEADS + hd])
            ds = (p * (dp - jnp.sum(p * dp, axis=-1, keepdims=True)) * scale).astype(BF16)
            dq_ref[:, hd * dx:(hd + 1) * dx] = _nn(ds, kh).astype(BF16)
            dkv_ref[hd] += _tn(ds, qh)
            dkv_ref[N_XHEADS + hd] += _tn(p.astype(BF16), doh)
        dh = _nt(dq_ref[...], _rows2d(wq_ref))
        _norm_bwd_store(dh, x_ref, g_ref, r_ref, dx_ref, dxb_ref, dg_ref, first)

    row = pl.BlockSpec((tm, D), lambda i: (i, 0))
    return pl.pallas_call(
        body, name=name, grid=(T // tm,),
        in_specs=[row, row, row, _full(g), row, _full(kv), _full(wq), _full(wo)],
        out_specs=[row, row, pl.BlockSpec((1, D), lambda i: (0, 0)), row, pl.BlockSpec(kv.shape, lambda i: (0, 0, 0))],
        out_shape=[SDS((T, D), F32), SDS((T, D), BF16), SDS((1, D), F32), SDS((T, D), BF16), SDS(kv.shape, F32)],
        compiler_params=_cp("arbitrary"))(dyb, dres, x, g, q, kv, wq, wo)


def _mesh_position():
    return lax.axis_index("x"), lax.axis_index("y"), lax.axis_index("c")


class _GatherCopies:
    def __init__(self, x_refs, out_refs, send_sems, recv_sems, local_sems):
        self.x_refs, self.out_refs, self.n = x_refs, out_refs, len(x_refs)
        self.send_sems, self.recv_sems, self.local_sems = send_sems, recv_sems, local_sems
        x, y, c = _mesh_position()
        self.c = c
        self.me, self.sibling = (x, y, c), (x, y, 1 - c)
        self.chips = [(1 - x, y), (x, 1 - y), (1 - x, 1 - y)]

    def slot(self, w, px, py, pc):
        return self.out_refs[w].at[4 * px + 2 * py + pc]

    def copy(self, w, k, block, to, src=None):
        return pltpu.make_async_remote_copy(
            src_ref=self.slot(w, *block) if src is None else src, dst_ref=self.slot(w, *block),
            send_sem=self.send_sems.at[w, k], recv_sem=self.recv_sems.at[w, k], device_id=to,
            device_id_type=pl.DeviceIdType.MESH)

    def mine(self, w):
        return pltpu.make_async_copy(self.x_refs[w], self.slot(w, *self.me), self.local_sems.at[w])

    def first(self, w):
        src = self.x_refs[w]
        return [self.copy(w, 0, self.me, self.sibling, src=src)] + [
            self.copy(w, 1 + j, self.me, (*chip, self.c), src=src) for j, chip in enumerate(self.chips)]

    def passed(self, w):
        return [self.copy(w, 4 + j, (*chip, self.c), self.sibling) for j, chip in enumerate(self.chips)]


def _gather_start(*refs):
    cp = _GatherCopies(*refs)
    for w in range(cp.n):
        cp.mine(w).start()
        for f in cp.first(w):
            f.start()


def _gather_relay(*refs):
    cp = _GatherCopies(*refs)
    for w in range(cp.n):
        for j, (chip, fwd) in enumerate(zip(cp.chips, cp.passed(w))):
            cp.copy(w, 1 + j, (*chip, cp.c), cp.me).wait_recv()
            fwd.start()


def _gather_finish(*refs):
    cp = _GatherCopies(*refs)
    for w in range(cp.n):
        cp.copy(w, 0, cp.sibling, cp.me).wait_recv()
        for j, chip in enumerate(cp.chips):
            cp.copy(w, 4 + j, (*chip, 1 - cp.c), cp.me).wait_recv()
    for w in range(cp.n):
        for f in cp.first(w) + cp.passed(w):
            f.wait_send()
        cp.mine(w).wait()


def _all_gather(shards, name):
    n = len(shards)

    def body(*refs):
        parts = (refs[:n], refs[n:2 * n]) + refs[2 * n:]
        _gather_start(*parts)
        _gather_relay(*parts)
        _gather_finish(*parts)

    any_spec = pl.BlockSpec(memory_space=pl.ANY)
    return pl.pallas_call(
        body, name=name, out_shape=[SDS((N_DEV,) + s.shape, s.dtype) for s in shards],
        in_specs=[any_spec] * n, out_specs=[any_spec] * n,
        scratch_shapes=[pltpu.SemaphoreType.DMA((n, 7)), pltpu.SemaphoreType.DMA((n, 7)), pltpu.SemaphoreType.DMA((n,))],
    )(*shards)


def _scatter_copies(x_refs, out_refs, send_sems, recv_sems, local_sems):
    x, y, c = _mesh_position()
    me = 4 * x + 2 * y + c
    own, remote = [], []
    for w, (src, dst) in enumerate(zip(x_refs, out_refs)):
        own.append(pltpu.make_async_copy(src.at[me], dst.at[me], local_sems.at[w]))
        for k in range(1, N_DEV):
            px = 1 - x if k & 4 else x
            py = 1 - y if k & 2 else y
            pc = 1 - c if k & 1 else c
            remote.append(pltpu.make_async_remote_copy(
                src_ref=src.at[4 * px + 2 * py + pc], dst_ref=dst.at[me], send_sem=send_sems.at[w, k - 1],
                recv_sem=recv_sems.at[w, k - 1], device_id=(px, py, pc), device_id_type=pl.DeviceIdType.MESH))
    return own, remote


def _scatter_start(x_refs, out_refs, send_sems, recv_sems, local_sems):
    own, remote = _scatter_copies(x_refs, out_refs, send_sems, recv_sems, local_sems)
    for cp in own + remote:
        cp.start()


def _scatter_wait(x_refs, out_refs, send_sems, recv_sems, local_sems):
    own, remote = _scatter_copies(x_refs, out_refs, send_sems, recv_sems, local_sems)
    for cp in remote:
        cp.wait_recv()
    for cp in remote:
        cp.wait_send()
    for cp in own:
        cp.wait()


def _sum_adamw(parts, w, m, v, name, tr=256):
    L, R, C = w.shape
    tr = _tile(R, tr, SUBLANES)
    nblk = R // tr

    def body(*refs):
        p_refs = refs[:L]
        w_ref, m_ref, v_ref, g_ref, d_ref, mo_ref, vo_ref = refs[L:]
        layer = pl.program_id(0)
        g = None
        for k, p_ref in enumerate(p_refs):
            gk = p_ref[0].astype(F32)
            for s in range(1, N_DEV):
                gk = gk + p_ref[s].astype(F32)
            g = gk if g is None else jnp.where(layer == k, gk, g)
        mn = ADAM_B1 * m_ref[...] + (1.0 - ADAM_B1) * g
        vn = ADAM_B2 * v_ref[...] + (1.0 - ADAM_B2) * jnp.square(g)
        m_hat = mn / (1.0 - ADAM_B1 ** ADAM_STEP)
        v_hat = vn / (1.0 - ADAM_B2 ** ADAM_STEP)
        g_ref[...] = g
        d_ref[...] = -ADAM_LR * (m_hat / (jnp.sqrt(v_hat) + ADAM_EPS) + ADAM_WD * w_ref[...])
        mo_ref[...] = mn
        vo_ref[...] = vn

    def part_spec(k):
        return pl.BlockSpec((N_DEV, tr, C),
                            lambda l, i: (0, jnp.where(l == k, i, jnp.where(l < k, 0, nblk - 1)), 0))

    row = pl.BlockSpec((None, tr, C), lambda l, i: (l, i, 0))
    return pl.pallas_call(
        body, name=name, grid=(L, nblk), in_specs=[part_spec(k) for k in range(L)] + [row, row, row],
        out_specs=[row] * 4, out_shape=[SDS((L, R, C), F32)] * 4,
        compiler_params=_cp("arbitrary", "arbitrary"))(*parts, w, m, v)


def _pack(flat_list, row_mult):
    flat = jnp.concatenate(flat_list, axis=-1)
    n = flat.shape[-1]
    chunk = row_mult * PACK_COLS
    pad = -n % chunk
    flat = jnp.pad(flat, [(0, 0)] * (flat.ndim - 1) + [(0, pad)])
    return flat.reshape(flat.shape[:-1] + ((n + pad) // PACK_COLS, PACK_COLS))


def _unpack(packed, shapes):
    lead = packed.shape[:-2]
    flat = packed.reshape(lead + (-1,))
    out = []
    off = 0
    for shp in shapes:
        n = 1
        for d in shp:
            n *= d
        out.append(flat[..., off:off + n].reshape(lead + tuple(shp)))
        off += n
    return out


def _ffn_bwd(dx, dxb, saved, wgu, wd, tag):
    x, g, h, ab, act = saved
    dgu = _ffn_bwd_act(dxb, wd, ab, f"{tag}_bwd_act")
    dwd = _ffn_dwd(act, dxb, f"{tag}_dwd")
    dwgu = _ffn_dwgu(h, dgu, f"{tag}_dwgu")
    D = dwd.shape[-1]
    return _ffn_bwd_in(dgu, wgu, x, g, dx, f"{tag}_bwd_in", [dwgu.reshape(N_DEV, -1, D), dwd.reshape(N_DEV, -1, D)])


def _mix_fwd(x, g, w_in8, w_conv8, b_f, g_conv, g_att, w_out, tag):
    T, D = x.shape
    w_int = w_in8.reshape(-1, D)
    cw = D // 2
    aw = D - cw
    H = aw // HEAD_DIM
    B = _tile(T, ATT_BLOCK, LANES)
    nb = T // B
    w_c = w_int[:3 * cw]
    w_af = jnp.pad(w_int[3 * cw:], ((0, TAIL - H), (0, 0)))
    zc, h, ync = _mix_in_conv(x, g, w_c, w_conv8, g_conv, f"{tag}_in_conv")
    zat, zft8, norms = _proj_t(h, w_af, B, f"{tag}_in_att")
    zft = zft8[:H]
    bcol = b_f.reshape(H, 1)
    c, qa, ka = _logf_cumsum(zft, bcol, B, f"{tag}_cumsum")
    cb = c.reshape(H, nb, B)
    jmin, imax = _skip_table(cb[:, :, 0:1], cb[:, :, B - 1].reshape(H, 1, nb), norms, f"{tag}_skip")
    ot, lse = _attn_fwd(jmin.reshape(H * nb), zat, qa, ka, f"{tag}_attn")
    g_col = g_att.reshape(aw, 1)
    yna = _gnorm_t_fwd(ot, g_col, f"{tag}_gnorm")
    xo = _mm_res([ync, yna], w_out, x, f"{tag}_out")
    saved = (x, g, h, zc, zft, bcol, zat, qa, ka, imax.reshape(H * nb), ot, lse, ync, yna, w_c, w_af, g_col)
    return xo, saved


def _mix_bwd(dx, dxb, saved, w_conv8, g_conv, w_out, tag, also_scatter):
    x, g, h, zc, zft, bcol, zat, qa, ka, imax, ot, lse, ync, yna, w_c, w_af, g_col = saved
    T, D = x.shape
    H = zft.shape[0]
    dw_out = jnp.concatenate([_mm_tn(ync, dxb, f"{tag}_dwout_c"), _mm_tn(yna, dxb, f"{tag}_dwout_a")], axis=0)
    dzc, dwc8, dg_conv, dot, delta, dg_att, *p_also = _mix_bwd_head(dxb, w_out, zc, w_conv8, g_conv, ot, g_col,
                                                                   f"{tag}_bwd_out", also_scatter)
    dq, dqa, dk, dka, dv = _attn_bwd(imax, zat, qa, ka, dot, lse, delta, f"{tag}_attn_bwd")
    dcq = dqa[:, :, 3, :].reshape(H, T)
    dck = dka[:, :, 0, :].reshape(H, T)
    dzft, db = _logf_cumsum_bwd(dcq, dck, zft, bcol, f"{tag}_cumsum_bwd")
    dzft8 = jnp.pad(dzft, ((0, SUBLANES - H), (0, 0)))
    dw_c = _mm_tn(dzc, h, f"{tag}_dwin_c")
    dw_af = _mix_dwt(dq, dk, dv, dzft8, h, f"{tag}_dwin_a")
    dw_int = jnp.concatenate([dw_c, dw_af[:dw_af.shape[0] - TAIL + H].astype(BF16)], axis=0)
    scatter = [dw_int.reshape(N_DEV, -1, D), dwc8[:3].reshape(3, N_DEV, -1).transpose(1, 0, 2),
               dw_out.reshape(N_DEV, D // N_DEV, D)]
    dx, dxb, dg, p_in, p_conv, p_out = _mix_bwd_in(dzc, dq, dk, dv, dzft8, w_c, w_af, x, g, dx, f"{tag}_bwd_in",
                                                   scatter)
    return dx, dxb, dg, p_in, p_conv, db.reshape(H), dg_conv, dg_att[:, 0], p_out, p_also


def _xattn_block_fwd(x, g, mem, g_mem, w_q, w_kv, w_o, tag):
    kv, hm = _mem_kv(mem, g_mem, w_kv, f"{tag}_kv")
    xo, h, q, o = _xattn(x, g, w_q, kv, w_o, f"{tag}_block")
    return xo, (x, g, h, mem, g_mem, hm, q, kv, o)


def _xattn_block_bwd(dx, dxb, saved, w_q, w_kv, w_o, tag):
    x, g, h, mem, g_mem, hm, q, kv, o = saved
    dw_o = _mm_tn(o, dxb, f"{tag}_dwo")
    dx, dxb, dg, dq, dkv = _xattn_bwd(dxb, dx, x, g, q, kv, w_q, w_o, f"{tag}_block_bwd")
    dw_q = _mm_tn(h, dq, f"{tag}_dwq")
    dw_kv, dg_mem = _mem_kv_bwd(dkv, hm, w_kv, mem, g_mem, f"{tag}_kv_bwd")
    D = dw_q.shape[0]
    return dx, dxb, dg, dg_mem, [dw_q.reshape(N_DEV, D // N_DEV, D), dw_kv, dw_o.reshape(N_DEV, D // N_DEV, D)]


GATHER_FIRST = ['w_ffn1_gu']
GATHER_REST = ['w_ffn1_down', 'w_mix_in', 'w_mix_out', 'w_xq', 'w_xkv', 'w_xo', 'w_ffn2_gu', 'w_ffn2_down']
GATHER_NEXT_A = ['w_ffn1_gu', 'w_ffn1_down', 'w_mix_in', 'w_mix_out', 'w_xq', 'w_xkv', 'w_xo']
GATHER_NEXT_B = ['w_ffn2_gu', 'w_ffn2_down']


def _local_step(x, mem, tgt, P):
    L = P['g_ffn1'].shape[0]
    shards = lambda names, l: [(P[n][l].T if n in LONG_AXIS_LAST else P[n][l]).astype(BF16) for n in names]
    wgut = lambda n, l: W[n][l].reshape(-1, W[n][l].shape[-1])
    W = {n: [None] * L for n in SHARDED}

    def keep(names, l, arrays):
        for n, a in zip(names, arrays):
            W[n][l] = a

    *first, conv = _all_gather(shards(GATHER_FIRST, 0) + [P['w_conv']], "gather_first")
    keep(GATHER_FIRST, 0, first)
    w_conv = conv.transpose(1, 2, 0, 3).reshape(L, 3, -1)
    saved = []
    for l in range(L):
        row = lambda n: P[n][l][None, :]
        wc8 = jnp.pad(w_conv[l], ((0, SUBLANES - 3), (0, 0)))
        x1 = x
        bring = GATHER_REST if l == 0 else []
        ab, act, h, *got = _ffn_gu(x, row('g_ffn1'), wgut('w_ffn1_gu', l), f"l{l}_ffn1_gu", gather=shards(bring, l))
        keep(bring, l, got)
        x, = _ffn_down(act, W['w_ffn1_down'][l], x, f"l{l}_ffn1_down")
        s1 = (x1, row('g_ffn1'), h, ab, act)
        x, s2 = _mix_fwd(x, row('g_mix'), W['w_mix_in'][l], wc8, P['b_f'][l], row('g_conv_out'), P['g_att_out'][l],
                         W['w_mix_out'][l], f"l{l}_mix")
        x, s3 = _xattn_block_fwd(x, row('g_xattn'), mem, row('g_mem'), W['w_xq'][l], W['w_xkv'][l], W['w_xo'][l],
                                 f"l{l}_xattn")
        x4 = x
        bring_a, bring_b = (GATHER_NEXT_A, GATHER_NEXT_B) if l + 1 < L else ([], [])
        ab, act, h, *got = _ffn_gu(x, row('g_ffn2'), wgut('w_ffn2_gu', l), f"l{l}_ffn2_gu", gather=shards(bring_a, l + 1))
        keep(bring_a, l + 1, got)
        x, *got = _ffn_down(act, W['w_ffn2_down'][l], x, f"l{l}_ffn2_down", gather=shards(bring_b, l + 1))
        keep(bring_b, l + 1, got)
        saved.append((s1, s2, s3, (x4, row('g_ffn2'), h, ab, act), wc8))
    loss, dx, dxb, dg_final = _loss_head(x, P['g_final'][None, :], tgt, "loss_head")
    G = {n: [None] * L for n in WEIGHT_NAMES if n != 'g_final'}
    for l in reversed(range(L)):
        row = lambda n: P[n][l][None, :]
        s1, s2, s3, s4, wc8 = saved[l]
        dx, dxb, G['g_ffn2'][l], G['w_ffn2_gu'][l], G['w_ffn2_down'][l] = _ffn_bwd(
            dx, dxb, s4, wgut('w_ffn2_gu', l), W['w_ffn2_down'][l], f"l{l}_ffn2")
        dx, dxb, G['g_xattn'][l], G['g_mem'][l], xattn_grads = _xattn_block_bwd(
            dx, dxb, s3, W['w_xq'][l], W['w_xkv'][l], W['w_xo'][l], f"l{l}_xattn")
        (dx, dxb, G['g_mix'][l], G['w_mix_in'][l], G['w_conv'][l], G['b_f'][l], G['g_conv_out'][l], G['g_att_out'][l],
         G['w_mix_out'][l], (G['w_xq'][l], G['w_xkv'][l], G['w_xo'][l])) = _mix_bwd(
            dx, dxb, s2, wc8, row('g_conv_out'), W['w_mix_out'][l], f"l{l}_mix", xattn_grads)
        dx, dxb, G['g_ffn1'][l], G['w_ffn1_gu'][l], G['w_ffn1_down'][l] = _ffn_bwd(
            dx, dxb, s1, wgut('w_ffn1_gu', l), W['w_ffn1_down'][l], f"l{l}_ffn1")
    G['g_final'] = [dg_final]
    return loss, dx, G


def kernel(x, mem, g_ffn1, w_ffn1_gu, w_ffn1_down, g_mix, w_mix_in, w_conv, b_f, g_conv_out, g_att_out, w_mix_out, g_xattn, g_mem, w_xq, w_xkv, w_xo, g_ffn2, w_ffn2_gu, w_ffn2_down, g_final, loss_target, m_g_ffn1, m_w_ffn1_gu, m_w_ffn1_down, m_g_mix, m_w_mix_in, m_w_conv, m_b_f, m_g_conv_out, m_g_att_out, m_w_mix_out, m_g_xattn, m_g_mem, m_w_xq, m_w_xkv, m_w_xo, m_g_ffn2, m_w_ffn2_gu, m_w_ffn2_down, m_g_final, v_g_ffn1, v_w_ffn1_gu, v_w_ffn1_down, v_g_mix, v_w_mix_in, v_w_conv, v_b_f, v_g_conv_out, v_g_att_out, v_w_mix_out, v_g_xattn, v_g_mem, v_w_xq, v_w_xkv, v_w_xo, v_g_ffn2, v_w_ffn2_gu, v_w_ffn2_down, v_g_final):
    args = (x, mem, g_ffn1, w_ffn1_gu, w_ffn1_down, g_mix, w_mix_in, w_conv, b_f, g_conv_out, g_att_out, w_mix_out,
            g_xattn, g_mem, w_xq, w_xkv, w_xo, g_ffn2, w_ffn2_gu, w_ffn2_down, g_final)
    P = dict(zip(IN_NAMES, args))
    moms = (m_g_ffn1, m_w_ffn1_gu, m_w_ffn1_down, m_g_mix, m_w_mix_in, m_w_conv, m_b_f, m_g_conv_out, m_g_att_out,
            m_w_mix_out, m_g_xattn, m_g_mem, m_w_xq, m_w_xkv, m_w_xo, m_g_ffn2, m_w_ffn2_gu, m_w_ffn2_down, m_g_final)
    vars_ = (v_g_ffn1, v_w_ffn1_gu, v_w_ffn1_down, v_g_mix, v_w_mix_in, v_w_conv, v_b_f, v_g_conv_out, v_g_att_out,
             v_w_mix_out, v_g_xattn, v_g_mem, v_w_xq, v_w_xkv, v_w_xo, v_g_ffn2, v_w_ffn2_gu, v_w_ffn2_down, v_g_final)
    MOM = dict(zip(WEIGHT_NAMES, moms))
    VAR = dict(zip(WEIGHT_NAMES, vars_))

    loss_part, dx, grads = _local_step(x[0], mem[0], loss_target[0], {n: P[n] for n in WEIGHT_NAMES})

    small_shapes = [P[n].shape for n in REPLICATED] + [(1,)]
    small = _pack([a.reshape(-1) for n in REPLICATED for a in grads[n]] + [loss_part[0, :1]], SUBLANES)
    small_parts = _all_gather([small], "gather_small_grads")[0]

    res = {}
    for n in SHARDED:
        t = (lambda a: a.transpose(0, 2, 1)) if n in LONG_AXIS_LAST else (lambda a: a)
        res[n] = [t(o) for o in _sum_adamw(grads[n], t(P[n]), t(MOM[n]), t(VAR[n]), "adamw_" + n)]
    zero = [jnp.zeros((1,), F32)]
    w_, m_, v_ = (_pack([d[n].reshape(-1) for n in REPLICATED] + zero, SUBLANES)[None] for d in (P, MOM, VAR))
    small_out = [_unpack(o[0], small_shapes) for o in _sum_adamw([small_parts], w_, m_, v_, "adamw_vectors")]
    for k, n in enumerate(REPLICATED):
        res[n] = [o[k] for o in small_out]
    out = []
    for k in range(4):
        out += [res[n][k] for n in WEIGHT_NAMES]
    return (small_out[0][-1].reshape(()), dx[None], *out)
```

```python
import jax
import jax.numpy as jnp
from jax import lax
from jax.experimental import pallas as pl
from jax.experimental.pallas import tpu as pltpu

F32 = jnp.float32
BF16 = jnp.bfloat16
I32 = jnp.int32
SDS = jax.ShapeDtypeStruct

EPS = 1e-6
HEAD_DIM = 64
N_XHEADS = 4
N_DEV = 8
LANES = 128
SUBLANES = 8
MXU_COLS = 256
AUG = 16
TAIL = 16
ATT_BLOCK = 512
NEG = -1e30
SKIP_MARGIN = 115.0
PACK_COLS = 1024
VMEM_LIMIT = 58 * 1024 * 1024

ADAM_LR = 0.001
ADAM_B1 = 0.9
ADAM_B2 = 0.999
ADAM_EPS = 1e-08
ADAM_WD = 0.01
ADAM_STEP = 10

IN_NAMES = ['x', 'mem', 'g_ffn1', 'w_ffn1_gu', 'w_ffn1_down', 'g_mix', 'w_mix_in', 'w_conv', 'b_f', 'g_conv_out',
            'g_att_out', 'w_mix_out', 'g_xattn', 'g_mem', 'w_xq', 'w_xkv', 'w_xo', 'g_ffn2', 'w_ffn2_gu',
            'w_ffn2_down', 'g_final']
WEIGHT_NAMES = IN_NAMES[2:]
SHARDED = ['w_ffn1_gu', 'w_ffn1_down', 'w_mix_in', 'w_conv', 'w_mix_out', 'w_xq', 'w_xkv', 'w_xo', 'w_ffn2_gu',
           'w_ffn2_down']
REPLICATED = [n for n in WEIGHT_NAMES if n not in SHARDED]
LONG_AXIS_LAST = ['w_ffn1_gu', 'w_ffn2_gu', 'w_mix_in']


def _tile(n, pref, mult):
    t = min(pref, n) // mult * mult
    while t >= mult:
        if n % t == 0:
            return t
        t -= mult
    return n


def _cp(*sem):
    return pltpu.CompilerParams(dimension_semantics=sem, vmem_limit_bytes=VMEM_LIMIT)


def _nt(a, b):
    return lax.dot_general(a, b, (((1,), (1,)), ((), ())), preferred_element_type=F32)


def _tn(a, b):
    return lax.dot_general(a, b, (((0,), (0,)), ((), ())), preferred_element_type=F32)


def _nn(a, b):
    return jnp.dot(a, b, preferred_element_type=F32)


def _rstd(xv):
    return lax.rsqrt(jnp.mean(xv * xv, axis=-1, keepdims=True) + EPS)


def _full(a):
    nd = a.ndim
    return pl.BlockSpec(a.shape, lambda *_: (0,) * nd, pipeline_mode=pl.Buffered(1))


def _rows2d(w_ref):
    s, a, b = w_ref.shape
    return w_ref[...].reshape(s * a, b)


def _accumulate(ref, first, part):
    @pl.when(first)
    def _():
        ref[...] = part

    @pl.when(jnp.logical_not(first))
    def _():
        ref[...] += part


def _norm_bwd_store(dh, x_ref, g_ref, r_ref, dx_ref, dxb_ref, dg_ref, first):
    xv = x_ref[...]
    r = _rstd(xv)
    xh = xv * r
    dxh = dh * g_ref[...]
    dx = r * (dxh - xh * jnp.mean(dxh * xh, axis=-1, keepdims=True)) + r_ref[...]
    dx_ref[...] = dx
    dxb_ref[...] = dx.astype(BF16)
    _accumulate(dg_ref, first, jnp.sum(dh * xh, axis=0, keepdims=True))


def _mm_res(a_list, w, res, name, tm=512):
    T = res.shape[0]
    N = w.shape[-1]
    tm = _tile(T, tm, SUBLANES)
    n_a = len(a_list)

    def body(*refs):
        a_refs = refs[:n_a]
        w_ref, r_ref, o_ref = refs[n_a:]
        wv = _rows2d(w_ref)
        acc = r_ref[...]
        off = 0
        for a_ref in a_refs:
            k = a_ref.shape[1]
            acc = acc + _nn(a_ref[...], wv[off:off + k, :])
            off += k
        o_ref[...] = acc

    row = lambda n: pl.BlockSpec((tm, n), lambda i: (i, 0))
    return pl.pallas_call(
        body, name=name, grid=(T // tm,),
        in_specs=[row(a.shape[1]) for a in a_list] + [_full(w), row(N)], out_specs=row(N),
        out_shape=SDS((T, N), F32), compiler_params=_cp("parallel"))(*a_list, w, res)


def _mm_tn(a, b, name, out_dtype=BF16, tt=2048):
    T, K = a.shape
    N = b.shape[1]
    tt = _tile(T, tt, 2 * SUBLANES)
    nt = T // tt

    def body(a_ref, b_ref, o_ref, acc_ref):
        t = pl.program_id(0)
        _accumulate(acc_ref, t == 0, _tn(a_ref[...], b_ref[...]))

        @pl.when(t == nt - 1)
        def _():
            o_ref[...] = acc_ref[...].astype(o_ref.dtype)

    return pl.pallas_call(
        body, name=name, grid=(nt,),
        in_specs=[pl.BlockSpec((tt, K), lambda t: (t, 0)), pl.BlockSpec((tt, N), lambda t: (t, 0))],
        out_specs=pl.BlockSpec((K, N), lambda t: (0, 0)), out_shape=SDS((K, N), out_dtype),
        scratch_shapes=[pltpu.VMEM((K, N), F32)], compiler_params=_cp("arbitrary"))(a, b)


def _hosted_call(body, name, n_steps, in_specs, out_specs, out_shape, args, scatter=(), gather=(), scratch_shapes=()):
    scatter, gather, scratch_shapes = list(scatter), list(gather), list(scratch_shapes)
    n_s, n_g = len(scatter), len(gather)
    if n_s + n_g == 0:
        return pl.pallas_call(body, name=name, grid=(n_steps,), in_specs=list(in_specs), out_specs=list(out_specs),
                              out_shape=list(out_shape), scratch_shapes=scratch_shapes,
                              compiler_params=_cp("arbitrary"))(*args)
    n_in, n_out, n_scr = len(in_specs), len(out_specs), len(scratch_shapes)
    relay_step = min(max(1, 3 * n_steps // 4), n_steps - 1)

    def wrapped(*refs):
        ins, refs = refs[:n_in], refs[n_in:]
        xs, shards, refs = refs[:n_s], refs[n_s:n_s + n_g], refs[n_s + n_g:]
        outs, refs = refs[:n_out], refs[n_out:]
        recvs, gathered, refs = refs[:n_s], refs[n_s:n_s + n_g], refs[n_s + n_g:]
        outs, sems = outs + refs[:n_scr], refs[n_scr:]
        s_sems, g_sems = (sems[:3], sems[3:]) if n_s else ((), sems)
        step = pl.program_id(0)

        @pl.when(step == 0)
        def _():
            if n_s:
                _scatter_start(xs, recvs, *s_sems)
            if n_g:
                _gather_start(shards, gathered, *g_sems)
        body(*ins, *outs)
        if n_g:
            @pl.when(step == relay_step)
            def _():
                _gather_relay(shards, gathered, *g_sems)

        @pl.when(step == n_steps - 1)
        def _():
            if n_s:
                _scatter_wait(xs, recvs, *s_sems)
            if n_g:
                _gather_finish(shards, gathered, *g_sems)

    any_spec = pl.BlockSpec(memory_space=pl.ANY)
    sems = []
    for n in (n_s, n_g):
        if n:
            sems += [pltpu.SemaphoreType.DMA((n, N_DEV - 1)), pltpu.SemaphoreType.DMA((n, N_DEV - 1)),
                     pltpu.SemaphoreType.DMA((n,))]
    return pl.pallas_call(
        wrapped, name=name, grid=(n_steps,), in_specs=list(in_specs) + [any_spec] * (n_s + n_g),
        out_specs=list(out_specs) + [any_spec] * (n_s + n_g),
        out_shape=list(out_shape) + [SDS(b.shape, b.dtype) for b in scatter]
        + [SDS((N_DEV,) + b.shape, b.dtype) for b in gather],
        scratch_shapes=scratch_shapes + sems, compiler_params=_cp("arbitrary"))(*args, *scatter, *gather)


def _hidden_chunks(F):
    edges = list(range(0, F, 3 * MXU_COLS)) + [F]
    return list(zip(edges[:-1], edges[1:]))


def _ffn_gu(x, g, wgut, name, gather=(), tm=512):
    T, D = x.shape
    F = wgut.shape[0] // 2
    tm = _tile(T, tm, SUBLANES)

    def body(x_ref, g_ref, w_ref, ab_ref, act_ref, h_ref):
        xv = x_ref[...]
        h = (xv * _rstd(xv) * g_ref[...]).astype(BF16)
        h_ref[...] = h
        for c0, c1 in _hidden_chunks(F):
            gt = _nt(h, w_ref[c0:c1, :])
            up = _nt(h, w_ref[F + c0:F + c1, :])
            sg = jax.nn.sigmoid(gt)
            silu = gt * sg
            ab_ref[0, :, c0:c1] = (0.5 * up * (sg * (1.0 + gt * (1.0 - sg)))).astype(BF16)
            ab_ref[1, :, c0:c1] = (0.5 * silu).astype(BF16)
            act_ref[:, c0:c1] = (silu * up).astype(BF16)

    row = lambda n: pl.BlockSpec((tm, n), lambda i: (i, 0))
    return _hosted_call(
        body, name, T // tm, [row(D), _full(g), _full(wgut)],
        [pl.BlockSpec((2, tm, F), lambda i: (0, i, 0)), row(F), row(D)],
        [SDS((2, T, F), BF16), SDS((T, F), BF16), SDS((T, D), BF16)], (x, g, wgut), gather=gather)


def _ffn_down(act, wd, res, name, gather=(), tm=1024):
    T, F = act.shape
    D = wd.shape[2]
    tm = _tile(T, tm, SUBLANES)

    def body(a_ref, w_ref, r_ref, o_ref):
        o_ref[...] = r_ref[...] + 0.5 * _nn(a_ref[...], _rows2d(w_ref))

    row = lambda n: pl.BlockSpec((tm, n), lambda i: (i, 0))
    return _hosted_call(body, name, T // tm, [row(F), _full(wd), row(D)], [row(D)], [SDS((T, D), F32)],
                        (act, wd, res), gather=gather)


def _ffn_bwd_act(dyb, wd, ab, name, tm=512):
    T, D = dyb.shape
    _, _, F = ab.shape
    tm = _tile(T, tm, SUBLANES)

    def body(dy_ref, w_ref, ab_ref, o_ref):
        wv = _rows2d(w_ref)
        dy = dy_ref[...]
        for c0, c1 in _hidden_chunks(F):
            da = _nt(dy, wv[c0:c1, :])
            o_ref[0, :, c0:c1] = (da * ab_ref[0, :, c0:c1].astype(F32)).astype(BF16)
            o_ref[1, :, c0:c1] = (da * ab_ref[1, :, c0:c1].astype(F32)).astype(BF16)

    blocks = pl.BlockSpec((2, tm, F), lambda i: (0, i, 0))
    return pl.pallas_call(
        body, name=name, grid=(T // tm,),
        in_specs=[pl.BlockSpec((tm, D), lambda i: (i, 0)), _full(wd), blocks], out_specs=blocks,
        out_shape=SDS((2, T, F), BF16), compiler_params=_cp("parallel"))(dyb, wd, ab)


def _ffn_dwd(act, dyb, name, tt=2048):
    T, F = act.shape
    D = dyb.shape[1]
    tt = _tile(T, tt, 2 * SUBLANES)
    nt = T // tt
    fh = F // 2

    def body(a_ref, b_ref, o_ref, acc_ref):
        t = pl.program_id(1)
        _accumulate(acc_ref, t == 0, _tn(a_ref[...], b_ref[...]))

        @pl.when(t == nt - 1)
        def _():
            o_ref[...] = (0.5 * acc_ref[...]).astype(BF16)

    return pl.pallas_call(
        body, name=name, grid=(2, nt),
        in_specs=[pl.BlockSpec((tt, fh), lambda j, t: (t, j)), pl.BlockSpec((tt, D), lambda j, t: (t, 0))],
        out_specs=pl.BlockSpec((fh, D), lambda j, t: (j, 0)), out_shape=SDS((F, D), BF16),
        scratch_shapes=[pltpu.VMEM((fh, D), F32)], compiler_params=_cp("parallel", "arbitrary"))(act, dyb)


def _ffn_dwgu(h, dgu, name, tt=2048):
    T, D = h.shape
    _, _, F = dgu.shape
    tt = _tile(T, tt, 2 * SUBLANES)
    nt = T // tt
    fh = F // 2

    def body(a_ref, b_ref, o_ref, acc_ref):
        t = pl.program_id(1)
        _accumulate(acc_ref, t == 0, _tn(b_ref[...], a_ref[...]))

        @pl.when(t == nt - 1)
        def _():
            o_ref[...] = acc_ref[...].astype(BF16)

    return pl.pallas_call(
        body, name=name, grid=(4, nt),
        in_specs=[pl.BlockSpec((tt, D), lambda q, t: (t, 0)),
                  pl.BlockSpec((None, tt, fh), lambda q, t: (q // 2, t, q % 2))],
        out_specs=pl.BlockSpec((fh, D), lambda q, t: (q, 0)), out_shape=SDS((2 * F, D), BF16),
        scratch_shapes=[pltpu.VMEM((fh, D), F32)], compiler_params=_cp("parallel", "arbitrary"))(h, dgu)


def _ffn_bwd_in(dgu, wgut, x, g, dres, name, scatter, tm=512):
    T, D = x.shape
    _, _, F = dgu.shape
    tm = _tile(T, tm, SUBLANES)

    def body(a_ref, w_ref, x_ref, g_ref, r_ref, dx_ref, dxb_ref, dg_ref):
        dh = _nn(a_ref[0], w_ref[:F, :]) + _nn(a_ref[1], w_ref[F:, :])
        _norm_bwd_store(dh, x_ref, g_ref, r_ref, dx_ref, dxb_ref, dg_ref, pl.program_id(0) == 0)

    row = pl.BlockSpec((tm, D), lambda i: (i, 0))
    return _hosted_call(
        body, name, T // tm, [pl.BlockSpec((2, tm, F), lambda i: (0, i, 0)), _full(wgut), row, _full(g), row],
        [row, row, pl.BlockSpec((1, D), lambda i: (0, 0))],
        [SDS((T, D), F32), SDS((T, D), BF16), SDS((1, D), F32)], (dgu, wgut, x, g, dres), scatter)


def _loss_head(x, g, tgt, name, tm=512):
    T, D = x.shape
    tm = _tile(T, tm, SUBLANES)

    def body(x_ref, g_ref, t_ref, loss_ref, dx_ref, dxb_ref, dg_ref):
        first = pl.program_id(0) == 0
        xv = x_ref[...]
        r = _rstd(xv)
        xh = xv * r
        e = xh * g_ref[...] - t_ref[...]
        part = 0.5 * jnp.sum(jnp.mean(e * e, axis=-1, keepdims=True), axis=0, keepdims=True)
        dy = e * (1.0 / D)
        dxh = dy * g_ref[...]
        dx = r * (dxh - xh * jnp.mean(dxh * xh, axis=-1, keepdims=True))
        dx_ref[...] = dx
        dxb_ref[...] = dx.astype(BF16)
        _accumulate(loss_ref, first, jnp.broadcast_to(part, loss_ref.shape))
        _accumulate(dg_ref, first, jnp.sum(dy * xh, axis=0, keepdims=True))

    row = pl.BlockSpec((tm, D), lambda i: (i, 0))
    vec = pl.BlockSpec((1, D), lambda i: (0, 0))
    return pl.pallas_call(
        body, name=name, grid=(T // tm,), in_specs=[row, vec, row],
        out_specs=[pl.BlockSpec((SUBLANES, LANES), lambda i: (0, 0)), row, row, vec],
        out_shape=[SDS((SUBLANES, LANES), F32), SDS((T, D), F32), SDS((T, D), BF16), SDS((1, D), F32)],
        compiler_params=_cp("arbitrary"))(x, g, tgt)


def _shift_down(u, k, prev_row):
    rows = lax.broadcasted_iota(I32, u.shape, 0)
    s = pltpu.roll(u, k, 0)
    for t in range(k):
        s = jnp.where(rows == t, prev_row(SUBLANES - k + t), s)
    return s


def _shift_up(u, k, next_row):
    n = u.shape[0]
    rows = lax.broadcasted_iota(I32, u.shape, 0)
    s = pltpu.roll(u, n - k, 0)
    for t in range(k):
        s = jnp.where(rows == n - k + t, next_row(t), s)
    return s


def _conv_taps(z_ref, zp_ref, w_ref, cw, first):
    u = z_ref[:, cw:2 * cw] * z_ref[:, 2 * cw:]

    def prev_row(r):
        return jnp.where(first, 0.0, zp_ref[r:r + 1, cw:2 * cw] * zp_ref[r:r + 1, 2 * cw:])

    u1 = _shift_down(u, 1, prev_row)
    u2 = _shift_down(u, 2, prev_row)
    cv = w_ref[0:1, :] * u2 + w_ref[1:2, :] * u1 + w_ref[2:3, :] * u
    return u, u1, u2, cv


def _mix_in_conv(x, g, w, wc, gc, name, tm=512):
    T, D = x.shape
    cw = w.shape[0] // 3
    tm = _tile(T, tm, SUBLANES)

    def body(x_ref, g_ref, w_ref, wc_ref, gc_ref, z_ref, h_ref, o_ref, prev_ref):
        xv = x_ref[...]
        h = (xv * _rstd(xv) * g_ref[...]).astype(BF16)
        h_ref[...] = h
        z_ref[...] = _nt(h, w_ref[...])
        _, _, _, cv = _conv_taps(z_ref, prev_ref, wc_ref, cw, pl.program_id(0) == 0)
        y = z_ref[:, :cw] * cv
        o_ref[...] = (y * _rstd(y) * gc_ref[...]).astype(BF16)
        prev_ref[...] = z_ref[tm - SUBLANES:, :]

    row = lambda n: pl.BlockSpec((tm, n), lambda i: (i, 0))
    return pl.pallas_call(
        body, name=name, grid=(T // tm,), in_specs=[row(D), _full(g), _full(w), _full(wc), _full(gc)],
        out_specs=[row(3 * cw), row(D), row(cw)],
        out_shape=[SDS((T, 3 * cw), F32), SDS((T, D), BF16), SDS((T, cw), BF16)],
        scratch_shapes=[pltpu.VMEM((SUBLANES, 3 * cw), F32)], compiler_params=_cp("arbitrary"))(x, g, w, wc, gc)


def _mix_bwd_head(dyb, w_out, zc, wc, gc, ot, g_col, name, scatter):
    nb, aw, tm = ot.shape
    T, D = dyb.shape
    cw = zc.shape[1] // 3
    H = aw // HEAD_DIM
    hb = tm // SUBLANES

    def body(dyb_ref, wo_ref, z_ref, zp_ref, w_ref, g_ref, o_ref, ga_ref,
             dz_ref, dw_ref, dg_ref, do_ref, dl_ref, dga_ref, carry_ref):
        s = pl.program_id(0)
        dy = _nt(dyb_ref[...], _rows2d(wo_ref))
        ov = o_ref[...]
        ra = lax.rsqrt(jnp.mean(ov * ov, axis=0, keepdims=True) + EPS)
        oh = ov * ra
        dya = dy[:, cw:].T
        dyah = dya * ga_ref[...]
        dov = ra * (dyah - oh * jnp.mean(dyah * oh, axis=0, keepdims=True))
        do_ref[...] = dov.astype(BF16)
        dl_ref[...] = jnp.sum((dov * ov).reshape(H, HEAD_DIM, tm), axis=1)
        _accumulate(dga_ref, s == 0, jnp.broadcast_to(jnp.sum(dya * oh, axis=1, keepdims=True), dga_ref.shape))
        u, u1, u2, cv = _conv_taps(z_ref, zp_ref, w_ref, cw, s == nb - 1)
        zb = z_ref[:, :cw]
        y = zb * cv
        r = _rstd(y)
        yh = y * r
        dyn = dy[:, :cw]
        dyh = dyn * g_ref[...]
        dyc = r * (dyh - yh * jnp.mean(dyh * yh, axis=-1, keepdims=True))
        dcv = dyc * zb

        def next_row(t):
            return jnp.where(s == 0, 0.0, carry_ref[t:t + 1, :])

        du = w_ref[2:3, :] * dcv + w_ref[1:2, :] * _shift_up(dcv, 1, next_row) + w_ref[0:1, :] * _shift_up(dcv, 2, next_row)
        carry_ref[...] = dcv[0:SUBLANES, :]
        dz_ref[:, :cw] = (dyc * cv).astype(BF16)
        dz_ref[:, cw:2 * cw] = (du * z_ref[:, 2 * cw:]).astype(BF16)
        dz_ref[:, 2 * cw:] = (du * z_ref[:, cw:2 * cw]).astype(BF16)
        tap = lax.broadcasted_iota(I32, (SUBLANES, cw), 0)
        dwp = jnp.where(tap == 0, jnp.sum(dcv * u2, axis=0, keepdims=True),
                        jnp.where(tap == 1, jnp.sum(dcv * u1, axis=0, keepdims=True),
                                  jnp.where(tap == 2, jnp.sum(dcv * u, axis=0, keepdims=True), 0.0)))
        _accumulate(dw_ref, s == 0, dwp)
        _accumulate(dg_ref, s == 0, jnp.sum(dyn * yh, axis=0, keepdims=True))

    rev = lambda s: nb - 1 - s
    rows = lambda n: pl.BlockSpec((tm, n), lambda s: (rev(s), 0))
    blk = pl.BlockSpec((None, aw, tm), lambda s: (rev(s), 0, 0))
    return _hosted_call(
        body, name, nb,
        [rows(D), _full(w_out), rows(3 * cw),
         pl.BlockSpec((SUBLANES, 3 * cw), lambda s: (jnp.maximum(rev(s) * hb - 1, 0), 0)),
         _full(wc), _full(gc), blk, _full(g_col)],
        [rows(3 * cw), pl.BlockSpec((SUBLANES, cw), lambda s: (0, 0)), pl.BlockSpec((1, cw), lambda s: (0, 0)),
         blk, pl.BlockSpec((None, H, tm), lambda s: (rev(s), 0, 0)), pl.BlockSpec((aw, LANES), lambda s: (0, 0))],
        [SDS((T, 3 * cw), BF16), SDS((SUBLANES, cw), F32), SDS((1, cw), F32),
         SDS((nb, aw, tm), BF16), SDS((nb, H, tm), F32), SDS((aw, LANES), F32)],
        (dyb, w_out, zc, zc, wc, gc, ot, g_col), scatter=scatter, scratch_shapes=[pltpu.VMEM((SUBLANES, cw), F32)])


def _proj_t(h, waf, B, name):
    T, D = h.shape
    R = waf.shape[0] - TAIL
    nb = T // B
    qk = 2 * R // 3
    g = qk // HEAD_DIM

    def body(h_ref, w_ref, za_ref, zf_ref, n_ref):
        z = _nt(w_ref[...], h_ref[...])
        zab = z[:R].astype(BF16)
        za_ref[...] = zab
        zf_ref[...] = z[R:R + SUBLANES]
        zv = zab[:qk].astype(F32)
        ss = jnp.sum((zv * zv).reshape(g, HEAD_DIM, B), axis=1)
        mx = jnp.broadcast_to(jnp.max(ss, axis=1, keepdims=True), n_ref.shape)

        @pl.when(pl.program_id(0) == 0)
        def _():
            n_ref[...] = mx

        @pl.when(pl.program_id(0) > 0)
        def _():
            n_ref[...] = jnp.maximum(n_ref[...], mx)

    return pl.pallas_call(
        body, name=name, grid=(nb,), in_specs=[pl.BlockSpec((B, D), lambda i: (i, 0)), _full(waf)],
        out_specs=[pl.BlockSpec((None, R, B), lambda i: (i, 0, 0)), pl.BlockSpec((SUBLANES, B), lambda i: (0, i)),
                   pl.BlockSpec((g, LANES), lambda i: (0, 0))],
        out_shape=[SDS((nb, R, B), BF16), SDS((SUBLANES, T), F32), SDS((g, LANES), F32)],
        compiler_params=_cp("arbitrary"))(h, waf)


def _split3(v):
    hi = v.astype(BF16).astype(F32)
    r1 = v - hi
    mid = r1.astype(BF16).astype(F32)
    lo = (r1 - mid).astype(BF16).astype(F32)
    return hi, mid, lo


def _tri_dot3(v, tri):
    hi, mid, lo = _split3(v)
    return _nn(hi.astype(BF16), tri) + _nn(mid.astype(BF16), tri) + _nn(lo.astype(BF16), tri)


def _logf_cumsum(zft, b, tb, name):
    H, T = zft.shape

    def body(z_ref, b_ref, c_ref, qa_ref, ka_ref, carry_ref, hi_ref, mid_ref, lo_ref):
        @pl.when(pl.program_id(0) == 0)
        def _():
            carry_ref[...] = jnp.zeros_like(carry_ref)
        xv = z_ref[...] + b_ref[...]
        lf = jnp.minimum(xv, 0.0) - jnp.log(1.0 + jnp.exp(-jnp.abs(xv)))
        src = lax.broadcasted_iota(I32, (tb, tb), 0)
        dst = lax.broadcasted_iota(I32, (tb, tb), 1)
        tri = jnp.where(src <= dst, 1.0, 0.0).astype(BF16)
        cs = _tri_dot3(lf, tri) + carry_ref[...]
        c_ref[...] = cs
        hi_ref[...], mid_ref[...], lo_ref[...] = _split3(cs)
        col = lax.broadcasted_iota(I32, cs.shape, 1)
        carry_ref[...] = jnp.sum(jnp.where(col == tb - 1, cs, 0.0), axis=1, keepdims=True)
        row = lax.broadcasted_iota(I32, (AUG, tb), 0)
        for h in range(H):
            terms = [r[h:h + 1, :] for r in (hi_ref, mid_ref, lo_ref)]
            qa, ka = jnp.where(row < 3, 1.0, 0.0), jnp.where((row >= 3) & (row < 6), 1.0, 0.0)
            for k, t in enumerate(terms):
                qa = jnp.where(row == 3 + k, t, qa)
                ka = jnp.where(row == k, -t, ka)
            qa_ref[h] = qa
            ka_ref[h] = ka

    blk = pl.BlockSpec((H, tb), lambda i: (0, i))
    aug = pl.BlockSpec((H, None, AUG, tb), lambda i: (0, i, 0, 0))
    return pl.pallas_call(
        body, name=name, grid=(T // tb,), in_specs=[blk, _full(b)], out_specs=[blk, aug, aug],
        out_shape=[SDS((H, T), F32), SDS((H, T // tb, AUG, tb), F32), SDS((H, T // tb, AUG, tb), F32)],
        scratch_shapes=[pltpu.VMEM((H, 1), F32)] + [pltpu.VMEM((H, tb), F32)] * 3,
        compiler_params=_cp("arbitrary"))(zft, b)


def _logf_cumsum_bwd(dcq, dck, zft, b, name, tb=512):
    H, T = zft.shape
    tb = _tile(T, tb, LANES)
    nb = T // tb

    def body(dq_ref, dk_ref, z_ref, b_ref, o_ref, db_ref, carry_ref):
        s = pl.program_id(0)

        @pl.when(s == 0)
        def _():
            carry_ref[...] = jnp.zeros_like(carry_ref)
        dc = dq_ref[...] - dk_ref[...]
        src = lax.broadcasted_iota(I32, (tb, tb), 0)
        dst = lax.broadcasted_iota(I32, (tb, tb), 1)
        tri = jnp.where(src >= dst, 1.0, 0.0).astype(BF16)
        dl = _tri_dot3(dc, tri) + carry_ref[...]
        col = lax.broadcasted_iota(I32, dl.shape, 1)
        carry_ref[...] = jnp.sum(jnp.where(col == 0, dl, 0.0), axis=1, keepdims=True)
        dz = dl * jax.nn.sigmoid(-(z_ref[...] + b_ref[...]))
        o_ref[...] = dz
        _accumulate(db_ref, s == 0, jnp.sum(dz, axis=1, keepdims=True))

    blk = pl.BlockSpec((H, tb), lambda s: (0, nb - 1 - s))
    vec = pl.BlockSpec((H, 1), lambda s: (0, 0))
    return pl.pallas_call(
        body, name=name, grid=(nb,), in_specs=[blk, blk, blk, vec], out_specs=[blk, vec],
        out_shape=[SDS((H, T), F32), SDS((H, 1), F32)],
        scratch_shapes=[pltpu.VMEM((H, 1), F32)], compiler_params=_cp("arbitrary"))(dcq, dck, zft, b)


def _skip_table(cs_col, ce_row, norms, name):
    H, nb, _ = cs_col.shape

    def body(cs_ref, ce_ref, n_ref, jm_ref, im_ref):
        h = pl.program_id(0)
        nq = n_ref[pl.ds(h, 1), 0:1]
        nk = n_ref[pl.ds(H + h, 1), 0:1]
        bound = 2.0 * jnp.sqrt(nq * nk) * 0.125
        skip = jnp.where(bound + cs_ref[...] - ce_ref[...] <= -SKIP_MARGIN, 1, 0).astype(I32)
        jm_ref[...] = jnp.sum(skip, axis=1, keepdims=True)
        im_ref[...] = nb - 1 - jnp.sum(skip, axis=0, keepdims=True)

    col = pl.BlockSpec((None, nb, 1), lambda h: (h, 0, 0))
    row = pl.BlockSpec((None, 1, nb), lambda h: (h, 0, 0))
    return pl.pallas_call(
        body, name=name, grid=(H,), in_specs=[col, row, _full(norms)], out_specs=[col, row],
        out_shape=[SDS((H, nb, 1), I32), SDS((H, 1, nb), I32)], compiler_params=_cp("parallel"))(cs_col, ce_row, norms)


def _causal_mask(B):
    krow = lax.broadcasted_iota(I32, (B, B), 0)
    qcol = lax.broadcasted_iota(I32, (B, B), 1)
    return krow <= qcol


def _attn_fwd(jmin, zat, qa, ka, name):
    nb, R, B = zat.shape
    aw = R // 3
    H = aw // HEAD_DIM
    hd = HEAD_DIM

    hp = 2 if H % 2 == 0 else 1

    def body(jm_ref, q_ref, k_ref, v_ref, qa_ref, ka_ref, o_ref, lse_ref):
        for n in range(hp):
            one_head(n, pl.program_id(0) * hp + n, jm_ref, q_ref, k_ref, v_ref, qa_ref, ka_ref, o_ref, lse_ref)

    def one_head(n, h, jm_ref, q_ref, k_ref, v_ref, qa_ref, ka_ref, o_ref, lse_ref):
        i = pl.program_id(1)
        rows = slice(n * hd, (n + 1) * hd)
        qt = jnp.concatenate([q_ref[rows, :] * 0.125, qa_ref[n].astype(BF16)], axis=0)

        def keys(j):
            return jnp.concatenate([k_ref[j, rows, :], ka_ref[n, j].astype(BF16)], axis=0)

        def values(j):
            return v_ref[j, rows, :]

        def update(s, vt, carry):
            m, l, acc = carry
            m_new = jnp.maximum(m, jnp.max(s, axis=0, keepdims=True))
            p = jnp.exp(s - m_new)
            a = jnp.exp(m - m_new)
            l = a * l + jnp.sum(p, axis=0, keepdims=True)
            acc = a * acc + _nn(vt, p.astype(BF16))
            return m_new, l, acc

        def block(j, carry, masked):
            s = _tn(keys(j), qt)
            if masked:
                s = jnp.where(_causal_mask(B), s, NEG)
            return update(s, values(j), carry)

        def two_blocks(t, carry):
            j = j0 + 2 * t
            s = _tn(jnp.concatenate([keys(j), keys(j + 1)], axis=1), qt)
            return update(s, jnp.concatenate([values(j), values(j + 1)], axis=1), carry)

        j0 = jm_ref[h * nb + i]
        pairs = (i - j0) // 2
        init = (jnp.full((1, B), NEG, F32), jnp.zeros((1, B), F32), jnp.zeros((hd, B), F32))
        carry = lax.fori_loop(0, pairs, two_blocks, init)
        carry = lax.fori_loop(j0 + 2 * pairs, i, lambda j, c: block(j, c, False), carry)
        m, l, acc = block(i, carry, True)
        o_ref[rows, :] = acc / l
        lse_ref[n] = m + jnp.log(l)

    G = H // hp
    grid_spec = pltpu.PrefetchScalarGridSpec(
        num_scalar_prefetch=1, grid=(G, nb),
        in_specs=[pl.BlockSpec((None, hp * hd, B), lambda h, i, jm: (i, h, 0)),
                  pl.BlockSpec((nb, hp * hd, B), lambda h, i, jm: (0, G + h, 0)),
                  pl.BlockSpec((nb, hp * hd, B), lambda h, i, jm: (0, 2 * G + h, 0)),
                  pl.BlockSpec((hp, None, AUG, B), lambda h, i, jm: (h, i, 0, 0)),
                  pl.BlockSpec((hp, nb, AUG, B), lambda h, i, jm: (h, 0, 0, 0))],
        out_specs=[pl.BlockSpec((None, hp * hd, B), lambda h, i, jm: (i, h, 0)),
                   pl.BlockSpec((hp, None, 1, B), lambda h, i, jm: (h, i, 0, 0))])
    return pl.pallas_call(
        body, name=name, grid_spec=grid_spec,
        out_shape=[SDS((nb, aw, B), F32), SDS((H, nb, 1, B), F32)],
        compiler_params=_cp("parallel", "parallel"))(jmin, zat, zat, zat, qa, ka)


def _gnorm_t_fwd(ot, g_col, name):
    nb, aw, B = ot.shape

    def body(o_ref, g_ref, y_ref):
        ov = o_ref[...]
        r = lax.rsqrt(jnp.mean(ov * ov, axis=0, keepdims=True) + EPS)
        y_ref[...] = (ov * r * g_ref[...]).T.astype(BF16)

    return pl.pallas_call(
        body, name=name, grid=(nb,), in_specs=[pl.BlockSpec((None, aw, B), lambda i: (i, 0, 0)), _full(g_col)],
        out_specs=pl.BlockSpec((B, aw), lambda i: (i, 0)), out_shape=SDS((nb * B, aw), BF16),
        compiler_params=_cp("parallel"))(ot, g_col)


def _attn_bwd(imax, zat, qa, ka, dot, lse, delta, name):
    nb, R, B = zat.shape
    aw = R // 3
    H = aw // HEAD_DIM
    hd = HEAD_DIM

    hp = 2 if H % 2 == 0 else 1

    def body(im_ref, k_ref, v_ref, ka_ref, q_ref, qa_ref, do_ref, lse_ref, dl_ref,
             dq_ref, dqa_ref, dk_ref, dka_ref, dv_ref, dq_acc):
        j = pl.program_id(1)

        @pl.when(j == 0)
        def _():
            dq_acc[...] = jnp.zeros_like(dq_acc)
            dqa_ref[...] = jnp.zeros_like(dqa_ref)
        for n in range(hp):
            one_head(n, pl.program_id(0) * hp + n, im_ref, k_ref, v_ref, ka_ref, q_ref, qa_ref, do_ref, lse_ref, dl_ref,
                     dqa_ref, dk_ref, dka_ref, dv_ref, dq_acc)

        @pl.when(j == nb - 1)
        def _():
            dq_ref[...] = dq_acc[...].astype(BF16)

    def one_head(n, h, im_ref, k_ref, v_ref, ka_ref, q_ref, qa_ref, do_ref, lse_ref, dl_ref,
                 dqa_ref, dk_ref, dka_ref, dv_ref, dq_acc):
        j = pl.program_id(1)
        rows = slice(n * hd, (n + 1) * hd)
        kt = jnp.concatenate([k_ref[rows, :], ka_ref[n].astype(BF16)], axis=0)
        vt = v_ref[rows, :]

        def queries(i):
            return jnp.concatenate([q_ref[i, rows, :] * 0.125, qa_ref[n, i].astype(BF16)], axis=0)

        def grads(qt, dov, lse, dl, carry, masked):
            dk, dv = carry
            s = _tn(kt, qt)
            if masked:
                s = jnp.where(_causal_mask(B), s, NEG)
            p = jnp.exp(s - lse)
            dp = _tn(vt, dov)
            ds = (p * (dp - dl)).astype(BF16)
            return (dk + _nt(qt, ds), dv + _nt(dov, p.astype(BF16))), _nn(kt, ds)

        def add_dq(i, r):
            dq_acc[i, rows, :] += 0.125 * r[:hd]
            dqa_ref[n, i] += r[hd:]

        def block(i, carry, masked):
            carry, r = grads(queries(i), do_ref[i, rows, :], lse_ref[n, i], dl_ref[i, pl.ds(h, 1), :], carry, masked)
            add_dq(i, r)
            return carry

        def two_blocks(t, carry):
            i = j + 1 + 2 * t
            both = lambda f: jnp.concatenate([f(i), f(i + 1)], axis=1)
            carry, r = grads(both(queries), both(lambda b: do_ref[b, rows, :]), both(lambda b: lse_ref[n, b]),
                             both(lambda b: dl_ref[b, pl.ds(h, 1), :]), carry, False)
            add_dq(i, r[:, :B])
            add_dq(i + 1, r[:, B:])
            return carry

        last = im_ref[h * nb + j]
        pairs = (last - j) // 2
        carry = block(j, (jnp.zeros((hd + AUG, B), F32), jnp.zeros((hd, B), F32)), True)
        carry = lax.fori_loop(0, pairs, two_blocks, carry)
        dk, dv = lax.fori_loop(j + 1 + 2 * pairs, last + 1, lambda i, c: block(i, c, False), carry)
        dk_ref[rows, :] = dk[:hd].astype(BF16)
        dka_ref[n] = dk[hd:]
        dv_ref[rows, :] = dv.astype(BF16)

    G = H // hp
    head_rows = lambda off: pl.BlockSpec((nb, hp * hd, B), lambda h, j, im: (0, off + h, 0))
    key_rows = lambda off: pl.BlockSpec((None, hp * hd, B), lambda h, j, im: (j, off + h, 0))
    aug_all = pl.BlockSpec((hp, nb, AUG, B), lambda h, j, im: (h, 0, 0, 0))
    aug_one = pl.BlockSpec((hp, None, AUG, B), lambda h, j, im: (h, j, 0, 0))
    grid_spec = pltpu.PrefetchScalarGridSpec(
        num_scalar_prefetch=1, grid=(G, nb),
        in_specs=[key_rows(G), key_rows(2 * G), aug_one, head_rows(0), aug_all, head_rows(0),
                  pl.BlockSpec((hp, nb, 1, B), lambda h, j, im: (h, 0, 0, 0)),
                  pl.BlockSpec((nb, H, B), lambda h, j, im: (0, 0, 0))],
        out_specs=[head_rows(0), aug_all, key_rows(0), aug_one, key_rows(0)],
        scratch_shapes=[pltpu.VMEM((nb, hp * hd, B), F32)])
    return pl.pallas_call(
        body, name=name, grid_spec=grid_spec,
        out_shape=[SDS((nb, aw, B), BF16), SDS((H, nb, AUG, B), F32), SDS((nb, aw, B), BF16),
                   SDS((H, nb, AUG, B), F32), SDS((nb, aw, B), BF16)],
        compiler_params=_cp("arbitrary", "arbitrary"))(imax, zat, zat, ka, zat, qa, dot, lse, delta)


def _stack_t(dq_ref, dk_ref, dv_ref, dzf_ref):
    dzf = dzf_ref[...]
    tail = jnp.concatenate([dzf, jnp.zeros((TAIL - dzf.shape[0], dzf.shape[1]), F32)], axis=0).astype(BF16)
    return jnp.concatenate([dq_ref[...], dk_ref[...], dv_ref[...], tail], axis=0)


def _mix_dwt(dq, dk, dv, dzft, h, name):
    nb, aw, B = dq.shape
    D = h.shape[1]

    def body(dq_ref, dk_ref, dv_ref, dzf_ref, h_ref, o_ref):
        _accumulate(o_ref, pl.program_id(0) == 0, _nn(_stack_t(dq_ref, dk_ref, dv_ref, dzf_ref), h_ref[...]))

    blk = pl.BlockSpec((None, aw, B), lambda i: (i, 0, 0))
    return pl.pallas_call(
        body, name=name, grid=(nb,),
        in_specs=[blk, blk, blk, pl.BlockSpec((SUBLANES, B), lambda i: (0, i)), pl.BlockSpec((B, D), lambda i: (i, 0))],
        out_specs=pl.BlockSpec((3 * aw + TAIL, D), lambda i: (0, 0)), out_shape=SDS((3 * aw + TAIL, D), F32),
        compiler_params=_cp("arbitrary"))(dq, dk, dv, dzft, h)


def _mix_bwd_in(dzc, dq, dk, dv, dzft, wc, waf, x, g, dres, name, scatter):
    T, D = x.shape
    nb, aw, B = dq.shape

    def body(dzc_ref, dq_ref, dk_ref, dv_ref, dzf_ref, wc_ref, wa_ref, x_ref, g_ref, r_ref, dx_ref, dxb_ref, dg_ref):
        dh = _nn(dzc_ref[...], wc_ref[...]) + _tn(_stack_t(dq_ref, dk_ref, dv_ref, dzf_ref), wa_ref[...])
        _norm_bwd_store(dh, x_ref, g_ref, r_ref, dx_ref, dxb_ref, dg_ref, pl.program_id(0) == 0)

    blk = pl.BlockSpec((None, aw, B), lambda i: (i, 0, 0))
    row = lambda n: pl.BlockSpec((B, n), lambda i: (i, 0))
    return _hosted_call(
        body, name, nb,
        [row(dzc.shape[1]), blk, blk, blk, pl.BlockSpec((SUBLANES, B), lambda i: (0, i)),
         _full(wc), _full(waf), row(D), _full(g), row(D)],
        [row(D), row(D), pl.BlockSpec((1, D), lambda i: (0, 0))],
        [SDS((T, D), F32), SDS((T, D), BF16), SDS((1, D), F32)],
        (dzc, dq, dk, dv, dzft, wc, waf, x, g, dres), scatter)


def _mem_kv(mem, g, wkv, name):
    M, D = mem.shape
    dx = wkv.shape[2]

    def body(m_ref, g_ref, w_ref, kv_ref, h_ref):
        mv = m_ref[...]
        h = (mv * _rstd(mv) * g_ref[...]).astype(BF16)
        h_ref[...] = h
        for s in range(N_DEV):
            kv_ref[s] = _nn(h, w_ref[s]).astype(BF16)

    return pl.pallas_call(
        body, name=name, grid=(1,), in_specs=[_full(mem), _full(g), _full(wkv)],
        out_specs=[pl.BlockSpec((N_DEV, M, dx), lambda i: (0, 0, 0)), pl.BlockSpec((M, D), lambda i: (0, 0))],
        out_shape=[SDS((N_DEV, M, dx), BF16), SDS((M, D), BF16)], compiler_params=_cp("arbitrary"))(mem, g, wkv)


def _mem_kv_bwd(dkv, hm, wkv, mem, g, name):
    M, D = mem.shape
    dx = wkv.shape[2]

    def body(dkv_ref, h_ref, w_ref, m_ref, g_ref, dw_ref, dg_ref):
        hv = h_ref[...]
        dh = jnp.zeros((M, D), F32)
        for s in range(N_DEV):
            d = dkv_ref[s].astype(BF16)
            dw_ref[s] = _tn(hv, d).astype(BF16)
            dh = dh + _nt(d, w_ref[s])
        mv = m_ref[...]
        dg_ref[...] = jnp.sum(dh * (mv * _rstd(mv)), axis=0, keepdims=True)

    return pl.pallas_call(
        body, name=name, grid=(1,), in_specs=[_full(dkv), _full(hm), _full(wkv), _full(mem), _full(g)],
        out_specs=[pl.BlockSpec((N_DEV, D, dx), lambda i: (0, 0, 0)), pl.BlockSpec((1, D), lambda i: (0, 0))],
        out_shape=[SDS((N_DEV, D, dx), BF16), SDS((1, D), F32)], compiler_params=_cp("arbitrary"))(dkv, hm, wkv, mem, g)


def _xattn_probs(q_ref, kv_ref, h, dx, scale):
    qh = q_ref[:, h * dx:(h + 1) * dx]
    kh = kv_ref[h]
    s = _nt(qh, kh) * scale
    p = jnp.exp(s - jnp.max(s, axis=-1, keepdims=True))
    return qh, kh, p / jnp.sum(p, axis=-1, keepdims=True)


def _xattn(x, g, wq, kv, wo, name, tm=512):
    T, D = x.shape
    dx = D // N_XHEADS
    scale = dx ** -0.5
    tm = _tile(T, tm, SUBLANES)

    def body(x_ref, g_ref, wq_ref, kv_ref, wo_ref, xo_ref, h_ref, q_ref, o_ref):
        xv = x_ref[...]
        h = (xv * _rstd(xv) * g_ref[...]).astype(BF16)
        h_ref[...] = h
        q_ref[...] = _nn(h, _rows2d(wq_ref)).astype(BF16)
        for hd in range(N_XHEADS):
            _, _, p = _xattn_probs(q_ref, kv_ref, hd, dx, scale)
            o_ref[:, hd * dx:(hd + 1) * dx] = _nn(p.astype(BF16), kv_ref[N_XHEADS + hd]).astype(BF16)
        xo_ref[...] = xv + _nn(o_ref[...], _rows2d(wo_ref))

    row = pl.BlockSpec((tm, D), lambda i: (i, 0))
    return pl.pallas_call(
        body, name=name, grid=(T // tm,), in_specs=[row, _full(g), _full(wq), _full(kv), _full(wo)],
        out_specs=[row] * 4, out_shape=[SDS((T, D), F32)] + [SDS((T, D), BF16)] * 3,
        compiler_params=_cp("parallel"))(x, g, wq, kv, wo)


def _xattn_bwd(dyb, dres, x, g, q, kv, wq, wo, name, tm=512):
    T, D = x.shape
    dx = D // N_XHEADS
    scale = dx ** -0.5
    tm = _tile(T, tm, SUBLANES)

    def body(dy_ref, r_ref, x_ref, g_ref, q_ref, kv_ref, wq_ref, wo_ref, dx_ref, dxb_ref, dg_ref, dq_ref, dkv_ref):
        first = pl.program_id(0) == 0

        @pl.when(first)
        def _():
            dkv_ref[...] = jnp.zeros_like(dkv_ref)
        do = _nt(dy_ref[...], _rows2d(wo_ref)).astype(BF16)
        for hd in range(N_XHEADS):
            qh, kh, p = _xattn_probs(q_ref, kv_ref, hd, dx, scale)
            doh = do[:, hd * dx:(hd + 1) * dx]
            dp = _nt(doh, kv_ref[N_XHEADS + hd])
            ds = (p * (dp - jnp.sum(p * dp, axis=-1, keepdims=True)) * scale).astype(BF16)
            dq_ref[:, hd * dx:(hd + 1) * dx] = _nn(ds, kh).astype(BF16)
            dkv_ref[hd] += _tn(ds, qh)
            dkv_ref[N_XHEADS + hd] += _tn(p.astype(BF16), doh)
        dh = _nt(dq_ref[...], _rows2d(wq_ref))
        _norm_bwd_store(dh, x_ref, g_ref, r_ref, dx_ref, dxb_ref, dg_ref, first)

    row = pl.BlockSpec((tm, D), lambda i: (i, 0))
    return pl.pallas_call(
        body, name=name, grid=(T // tm,),
        in_specs=[row, row, row, _full(g), row, _full(kv), _full(wq), _full(wo)],
        out_specs=[row, row, pl.BlockSpec((1, D), lambda i: (0, 0)), row, pl.BlockSpec(kv.shape, lambda i: (0, 0, 0))],
        out_shape=[SDS((T, D), F32), SDS((T, D), BF16), SDS((1, D), F32), SDS((T, D), BF16), SDS(kv.shape, F32)],
        compiler_params=_cp("arbitrary"))(dyb, dres, x, g, q, kv, wq, wo)


def _mesh_position():
    return lax.axis_index("x"), lax.axis_index("y"), lax.axis_index("c")


class _GatherCopies:
    def __init__(self, x_refs, out_refs, send_sems, recv_sems, local_sems):
        self.x_refs, self.out_refs, self.n = x_refs, out_refs, len(x_refs)
        self.send_sems, self.recv_sems, self.local_sems = send_sems, recv_sems, local_sems
        x, y, c = _mesh_position()
        self.c = c
        self.me, self.sibling = (x, y, c), (x, y, 1 - c)
        self.chips = [(1 - x, y), (x, 1 - y), (1 - x, 1 - y)]

    def slot(self, w, px, py, pc):
        return self.out_refs[w].at[4 * px + 2 * py + pc]

    def copy(self, w, k, block, to, src=None):
        return pltpu.make_async_remote_copy(
            src_ref=self.slot(w, *block) if src is None else src, dst_ref=self.slot(w, *block),
            send_sem=self.send_sems.at[w, k], recv_sem=self.recv_sems.at[w, k], device_id=to,
            device_id_type=pl.DeviceIdType.MESH)

    def mine(self, w):
        return pltpu.make_async_copy(self.x_refs[w], self.slot(w, *self.me), self.local_sems.at[w])

    def first(self, w):
        src = self.x_refs[w]
        return [self.copy(w, 0, self.me, self.sibling, src=src)] + [
            self.copy(w, 1 + j, self.me, (*chip, self.c), src=src) for j, chip in enumerate(self.chips)]

    def passed(self, w):
        return [self.copy(w, 4 + j, (*chip, self.c), self.sibling) for j, chip in enumerate(self.chips)]


def _gather_start(*refs):
    cp = _GatherCopies(*refs)
    for w in range(cp.n):
        cp.mine(w).start()
        for f in cp.first(w):
            f.start()


def _gather_relay(*refs):
    cp = _GatherCopies(*refs)
    for w in range(cp.n):
        for j, (chip, fwd) in enumerate(zip(cp.chips, cp.passed(w))):
            cp.copy(w, 1 + j, (*chip, cp.c), cp.me).wait_recv()
            fwd.start()


def _gather_finish(*refs):
    cp = _GatherCopies(*refs)
    for w in range(cp.n):
        cp.copy(w, 0, cp.sibling, cp.me).wait_recv()
        for j, chip in enumerate(cp.chips):
            cp.copy(w, 4 + j, (*chip, 1 - cp.c), cp.me).wait_recv()
    for w in range(cp.n):
        for f in cp.first(w) + cp.passed(w):
            f.wait_send()
        cp.mine(w).wait()


def _all_gather(shards, name):
    n = len(shards)

    def body(*refs):
        parts = (refs[:n], refs[n:2 * n]) + refs[2 * n:]
        _gather_start(*parts)
        _gather_relay(*parts)
        _gather_finish(*parts)

    any_spec = pl.BlockSpec(memory_space=pl.ANY)
    return pl.pallas_call(
        body, name=name, out_shape=[SDS((N_DEV,) + s.shape, s.dtype) for s in shards],
        in_specs=[any_spec] * n, out_specs=[any_spec] * n,
        scratch_shapes=[pltpu.SemaphoreType.DMA((n, 7)), pltpu.SemaphoreType.DMA((n, 7)), pltpu.SemaphoreType.DMA((n,))],
    )(*shards)


def _scatter_copies(x_refs, out_refs, send_sems, recv_sems, local_sems):
    x, y, c = _mesh_position()
    me = 4 * x + 2 * y + c
    own, remote = [], []
    for w, (src, dst) in enumerate(zip(x_refs, out_refs)):
        own.append(pltpu.make_async_copy(src.at[me], dst.at[me], local_sems.at[w]))
        for k in range(1, N_DEV):
            px = 1 - x if k & 4 else x
            py = 1 - y if k & 2 else y
            pc = 1 - c if k & 1 else c
            remote.append(pltpu.make_async_remote_copy(
                src_ref=src.at[4 * px + 2 * py + pc], dst_ref=dst.at[me], send_sem=send_sems.at[w, k - 1],
                recv_sem=recv_sems.at[w, k - 1], device_id=(px, py, pc), device_id_type=pl.DeviceIdType.MESH))
    return own, remote


def _scatter_start(x_refs, out_refs, send_sems, recv_sems, local_sems):
    own, remote = _scatter_copies(x_refs, out_refs, send_sems, recv_sems, local_sems)
    for cp in own + remote:
        cp.start()


def _scatter_wait(x_refs, out_refs, send_sems, recv_sems, local_sems):
    own, remote = _scatter_copies(x_refs, out_refs, send_sems, recv_sems, local_sems)
    for cp in remote:
        cp.wait_recv()
    for cp in remote:
        cp.wait_send()
    for cp in own:
        cp.wait()


def _sum_adamw(parts, w, m, v, name, tr=256):
    L, R, C = w.shape
    tr = _tile(R, tr, SUBLANES)
    nblk = R // tr

    def body(*refs):
        p_refs = refs[:L]
        w_ref, m_ref, v_ref, g_ref, d_ref, mo_ref, vo_ref = refs[L:]
        layer = pl.program_id(0)
        g = None
        for k, p_ref in enumerate(p_refs):
            gk = p_ref[0].astype(F32)
            for s in range(1, N_DEV):
                gk = gk + p_ref[s].astype(F32)
            g = gk if g is None else jnp.where(layer == k, gk, g)
        mn = ADAM_B1 * m_ref[...] + (1.0 - ADAM_B1) * g
        vn = ADAM_B2 * v_ref[...] + (1.0 - ADAM_B2) * jnp.square(g)
        m_hat = mn / (1.0 - ADAM_B1 ** ADAM_STEP)
        v_hat = vn / (1.0 - ADAM_B2 ** ADAM_STEP)
        g_ref[...] = g
        d_ref[...] = -ADAM_LR * (m_hat / (jnp.sqrt(v_hat) + ADAM_EPS) + ADAM_WD * w_ref[...])
        mo_ref[...] = mn
        vo_ref[...] = vn

    def part_spec(k):
        return pl.BlockSpec((N_DEV, tr, C),
                            lambda l, i: (0, jnp.where(l == k, i, jnp.where(l < k, 0, nblk - 1)), 0))

    row = pl.BlockSpec((None, tr, C), lambda l, i: (l, i, 0))
    return pl.pallas_call(
        body, name=name, grid=(L, nblk), in_specs=[part_spec(k) for k in range(L)] + [row, row, row],
        out_specs=[row] * 4, out_shape=[SDS((L, R, C), F32)] * 4,
        compiler_params=_cp("arbitrary", "arbitrary"))(*parts, w, m, v)


def _pack(flat_list, row_mult):
    flat = jnp.concatenate(flat_list, axis=-1)
    n = flat.shape[-1]
    chunk = row_mult * PACK_COLS
    pad = -n % chunk
    flat = jnp.pad(flat, [(0, 0)] * (flat.ndim - 1) + [(0, pad)])
    return flat.reshape(flat.shape[:-1] + ((n + pad) // PACK_COLS, PACK_COLS))


def _unpack(packed, shapes):
    lead = packed.shape[:-2]
    flat = packed.reshape(lead + (-1,))
    out = []
    off = 0
    for shp in shapes:
        n = 1
        for d in shp:
            n *= d
        out.append(flat[..., off:off + n].reshape(lead + tuple(shp)))
        off += n
    return out


def _ffn_bwd(dx, dxb, saved, wgu, wd, tag):
    x, g, h, ab, act = saved
    dgu = _ffn_bwd_act(dxb, wd, ab, f"{tag}_bwd_act")
    dwd = _ffn_dwd(act, dxb, f"{tag}_dwd")
    dwgu = _ffn_dwgu(h, dgu, f"{tag}_dwgu")
    D = dwd.shape[-1]
    return _ffn_bwd_in(dgu, wgu, x, g, dx, f"{tag}_bwd_in", [dwgu.reshape(N_DEV, -1, D), dwd.reshape(N_DEV, -1, D)])


def _mix_fwd(x, g, w_in8, w_conv8, b_f, g_conv, g_att, w_out, tag):
    T, D = x.shape
    w_int = w_in8.reshape(-1, D)
    cw = D // 2
    aw = D - cw
    H = aw // HEAD_DIM
    B = _tile(T, ATT_BLOCK, LANES)
    nb = T // B
    w_c = w_int[:3 * cw]
    w_af = jnp.pad(w_int[3 * cw:], ((0, TAIL - H), (0, 0)))
    zc, h, ync = _mix_in_conv(x, g, w_c, w_conv8, g_conv, f"{tag}_in_conv")
    zat, zft8, norms = _proj_t(h, w_af, B, f"{tag}_in_att")
    zft = zft8[:H]
    bcol = b_f.reshape(H, 1)
    c, qa, ka = _logf_cumsum(zft, bcol, B, f"{tag}_cumsum")
    cb = c.reshape(H, nb, B)
    jmin, imax = _skip_table(cb[:, :, 0:1], cb[:, :, B - 1].reshape(H, 1, nb), norms, f"{tag}_skip")
    ot, lse = _attn_fwd(jmin.reshape(H * nb), zat, qa, ka, f"{tag}_attn")
    g_col = g_att.reshape(aw, 1)
    yna = _gnorm_t_fwd(ot, g_col, f"{tag}_gnorm")
    xo = _mm_res([ync, yna], w_out, x, f"{tag}_out")
    saved = (x, g, h, zc, zft, bcol, zat, qa, ka, imax.reshape(H * nb), ot, lse, ync, yna, w_c, w_af, g_col)
    return xo, saved


def _mix_bwd(dx, dxb, saved, w_conv8, g_conv, w_out, tag, also_scatter):
    x, g, h, zc, zft, bcol, zat, qa, ka, imax, ot, lse, ync, yna, w_c, w_af, g_col = saved
    T, D = x.shape
    H = zft.shape[0]
    dw_out = jnp.concatenate([_mm_tn(ync, dxb, f"{tag}_dwout_c"), _mm_tn(yna, dxb, f"{tag}_dwout_a")], axis=0)
    dzc, dwc8, dg_conv, dot, delta, dg_att, *p_also = _mix_bwd_head(dxb, w_out, zc, w_conv8, g_conv, ot, g_col,
                                                                   f"{tag}_bwd_out", also_scatter)
    dq, dqa, dk, dka, dv = _attn_bwd(imax, zat, qa, ka, dot, lse, delta, f"{tag}_attn_bwd")
    dcq = dqa[:, :, 3, :].reshape(H, T)
    dck = dka[:, :, 0, :].reshape(H, T)
    dzft, db = _logf_cumsum_bwd(dcq, dck, zft, bcol, f"{tag}_cumsum_bwd")
    dzft8 = jnp.pad(dzft, ((0, SUBLANES - H), (0, 0)))
    dw_c = _mm_tn(dzc, h, f"{tag}_dwin_c")
    dw_af = _mix_dwt(dq, dk, dv, dzft8, h, f"{tag}_dwin_a")
    dw_int = jnp.concatenate([dw_c, dw_af[:dw_af.shape[0] - TAIL + H].astype(BF16)], axis=0)
    scatter = [dw_int.reshape(N_DEV, -1, D), dwc8[:3].reshape(3, N_DEV, -1).transpose(1, 0, 2),
               dw_out.reshape(N_DEV, D // N_DEV, D)]
    dx, dxb, dg, p_in, p_conv, p_out = _mix_bwd_in(dzc, dq, dk, dv, dzft8, w_c, w_af, x, g, dx, f"{tag}_bwd_in",
                                                   scatter)
    return dx, dxb, dg, p_in, p_conv, db.reshape(H), dg_conv, dg_att[:, 0], p_out, p_also


def _xattn_block_fwd(x, g, mem, g_mem, w_q, w_kv, w_o, tag):
    kv, hm = _mem_kv(mem, g_mem, w_kv, f"{tag}_kv")
    xo, h, q, o = _xattn(x, g, w_q, kv, w_o, f"{tag}_block")
    return xo, (x, g, h, mem, g_mem, hm, q, kv, o)


def _xattn_block_bwd(dx, dxb, saved, w_q, w_kv, w_o, tag):
    x, g, h, mem, g_mem, hm, q, kv, o = saved
    dw_o = _mm_tn(o, dxb, f"{tag}_dwo")
    dx, dxb, dg, dq, dkv = _xattn_bwd(dxb, dx, x, g, q, kv, w_q, w_o, f"{tag}_block_bwd")
    dw_q = _mm_tn(h, dq, f"{tag}_dwq")
    dw_kv, dg_mem = _mem_kv_bwd(dkv, hm, w_kv, mem, g_mem, f"{tag}_kv_bwd")
    D = dw_q.shape[0]
    return dx, dxb, dg, dg_mem, [dw_q.reshape(N_DEV, D // N_DEV, D), dw_kv, dw_o.reshape(N_DEV, D // N_DEV, D)]


GATHER_FIRST = ['w_ffn1_gu']
GATHER_REST = ['w_ffn1_down', 'w_mix_in', 'w_mix_out', 'w_xq', 'w_xkv', 'w_xo', 'w_ffn2_gu', 'w_ffn2_down']
GATHER_NEXT_A = ['w_ffn1_gu', 'w_ffn1_down', 'w_mix_in', 'w_mix_out', 'w_xq', 'w_xkv', 'w_xo']
GATHER_NEXT_B = ['w_ffn2_gu', 'w_ffn2_down']


def _local_step(x, mem, tgt, P):
    L = P['g_ffn1'].shape[0]
    shards = lambda names, l: [(P[n][l].T if n in LONG_AXIS_LAST else P[n][l]).astype(BF16) for n in names]
    wgut = lambda n, l: W[n][l].reshape(-1, W[n][l].shape[-1])
    W = {n: [None] * L for n in SHARDED}

    def keep(names, l, arrays):
        for n, a in zip(names, arrays):
            W[n][l] = a

    *first, conv = _all_gather(shards(GATHER_FIRST, 0) + [P['w_conv']], "gather_first")
    keep(GATHER_FIRST, 0, first)
    w_conv = conv.transpose(1, 2, 0, 3).reshape(L, 3, -1)
    saved = []
    for l in range(L):
        row = lambda n: P[n][l][None, :]
        wc8 = jnp.pad(w_conv[l], ((0, SUBLANES - 3), (0, 0)))
        x1 = x
        bring = GATHER_REST if l == 0 else []
        ab, act, h, *got = _ffn_gu(x, row('g_ffn1'), wgut('w_ffn1_gu', l), f"l{l}_ffn1_gu", gather=shards(bring, l))
        keep(bring, l, got)
        x, = _ffn_down(act, W['w_ffn1_down'][l], x, f"l{l}_ffn1_down")
        s1 = (x1, row('g_ffn1'), h, ab, act)
        x, s2 = _mix_fwd(x, row('g_mix'), W['w_mix_in'][l], wc8, P['b_f'][l], row('g_conv_out'), P['g_att_out'][l],
                         W['w_mix_out'][l], f"l{l}_mix")
        x, s3 = _xattn_block_fwd(x, row('g_xattn'), mem, row('g_mem'), W['w_xq'][l], W['w_xkv'][l], W['w_xo'][l],
                                 f"l{l}_xattn")
        x4 = x
        bring_a, bring_b = (GATHER_NEXT_A, GATHER_NEXT_B) if l + 1 < L else ([], [])
        ab, act, h, *got = _ffn_gu(x, row('g_ffn2'), wgut('w_ffn2_gu', l), f"l{l}_ffn2_gu", gather=shards(bring_a, l + 1))
        keep(bring_a, l + 1, got)
        x, *got = _ffn_down(act, W['w_ffn2_down'][l], x, f"l{l}_ffn2_down", gather=shards(bring_b, l + 1))
        keep(bring_b, l + 1, got)
        saved.append((s1, s2, s3, (x4, row('g_ffn2'), h, ab, act), wc8))
    loss, dx, dxb, dg_final = _loss_head(x, P['g_final'][None, :], tgt, "loss_head")
    G = {n: [None] * L for n in WEIGHT_NAMES if n != 'g_final'}
    for l in reversed(range(L)):
        row = lambda n: P[n][l][None, :]
        s1, s2, s3, s4, wc8 = saved[l]
        dx, dxb, G['g_ffn2'][l], G['w_ffn2_gu'][l], G['w_ffn2_down'][l] = _ffn_bwd(
            dx, dxb, s4, wgut('w_ffn2_gu', l), W['w_ffn2_down'][l], f"l{l}_ffn2")
        dx, dxb, G['g_xattn'][l], G['g_mem'][l], xattn_grads = _xattn_block_bwd(
            dx, dxb, s3, W['w_xq'][l], W['w_xkv'][l], W['w_xo'][l], f"l{l}_xattn")
        (dx, dxb, G['g_mix'][l], G['w_mix_in'][l], G['w_conv'][l], G['b_f'][l], G['g_conv_out'][l], G['g_att_out'][l],
         G['w_mix_out'][l], (G['w_xq'][l], G['w_xkv'][l], G['w_xo'][l])) = _mix_bwd(
            dx, dxb, s2, wc8, row('g_conv_out'), W['w_mix_out'][l], f"l{l}_mix", xattn_grads)
        dx, dxb, G['g_ffn1'][l], G['w_ffn1_gu'][l], G['w_ffn1_down'][l] = _ffn_bwd(
            dx, dxb, s1, wgut('w_ffn1_gu', l), W['w_ffn1_down'][l], f"l{l}_ffn1")
    G['g_final'] = [dg_final]
    return loss, dx, G


def kernel(x, mem, g_ffn1, w_ffn1_gu, w_ffn1_down, g_mix, w_mix_in, w_conv, b_f, g_conv_out, g_att_out, w_mix_out, g_xattn, g_mem, w_xq, w_xkv, w_xo, g_ffn2, w_ffn2_gu, w_ffn2_down, g_final, loss_target, m_g_ffn1, m_w_ffn1_gu, m_w_ffn1_down, m_g_mix, m_w_mix_in, m_w_conv, m_b_f, m_g_conv_out, m_g_att_out, m_w_mix_out, m_g_xattn, m_g_mem, m_w_xq, m_w_xkv, m_w_xo, m_g_ffn2, m_w_ffn2_gu, m_w_ffn2_down, m_g_final, v_g_ffn1, v_w_ffn1_gu, v_w_ffn1_down, v_g_mix, v_w_mix_in, v_w_conv, v_b_f, v_g_conv_out, v_g_att_out, v_w_mix_out, v_g_xattn, v_g_mem, v_w_xq, v_w_xkv, v_w_xo, v_g_ffn2, v_w_ffn2_gu, v_w_ffn2_down, v_g_final):
    args = (x, mem, g_ffn1, w_ffn1_gu, w_ffn1_down, g_mix, w_mix_in, w_conv, b_f, g_conv_out, g_att_out, w_mix_out,
            g_xattn, g_mem, w_xq, w_xkv, w_xo, g_ffn2, w_ffn2_gu, w_ffn2_down, g_final)
    P = dict(zip(IN_NAMES, args))
    moms = (m_g_ffn1, m_w_ffn1_gu, m_w_ffn1_down, m_g_mix, m_w_mix_in, m_w_conv, m_b_f, m_g_conv_out, m_g_att_out,
            m_w_mix_out, m_g_xattn, m_g_mem, m_w_xq, m_w_xkv, m_w_xo, m_g_ffn2, m_w_ffn2_gu, m_w_ffn2_down, m_g_final)
    vars_ = (v_g_ffn1, v_w_ffn1_gu, v_w_ffn1_down, v_g_mix, v_w_mix_in, v_w_conv, v_b_f, v_g_conv_out, v_g_att_out,
             v_w_mix_out, v_g_xattn, v_g_mem, v_w_xq, v_w_xkv, v_w_xo, v_g_ffn2, v_w_ffn2_gu, v_w_ffn2_down, v_g_final)
    MOM = dict(zip(WEIGHT_NAMES, moms))
    VAR = dict(zip(WEIGHT_NAMES, vars_))

    loss_part, dx, grads = _local_step(x[0], mem[0], loss_target[0], {n: P[n] for n in WEIGHT_NAMES})

    small_shapes = [P[n].shape for n in REPLICATED] + [(1,)]
    small = _pack([a.reshape(-1) for n in REPLICATED for a in grads[n]] + [loss_part[0, :1]], SUBLANES)
    small_parts = _all_gather([small], "gather_small_grads")[0]

    res = {}
    for n in SHARDED:
        t = (lambda a: a.transpose(0, 2, 1)) if n in LONG_AXIS_LAST else (lambda a: a)
        res[n] = [t(o) for o in _sum_adamw(grads[n], t(P[n]), t(MOM[n]), t(VAR[n]), "adamw_" + n)]
    zero = [jnp.zeros((1,), F32)]
    w_, m_, v_ = (_pack([d[n].reshape(-1) for n in REPLICATED] + zero, SUBLANES)[None] for d in (P, MOM, VAR))
    small_out = [_unpack(o[0], small_shapes) for o in _sum_adamw([small_parts], w_, m_, v_, "adamw_vectors")]
    for k, n in enumerate(REPLICATED):
        res[n] = [o[k] for o in small_out]
    out = []
    for k in range(4):
        out += [res[n][k] for n in WEIGHT_NAMES]
    return (small_out[0][-1].reshape(()), dx[None], *out)
```
